```python
import jax, jax.numpy as jnp
from jax import lax
import numpy as np

D_MODEL = 1024
BATCH = 32
SEQ = 2048
DEPTH = 2

CONV_CH = 512
CONV_WIDTH = 31
ATTN_HEADS = 8
HEAD_DIM = 64
ATTN_W = ATTN_HEADS * HEAD_DIM
Q_BLOCK = 128
D_FF = 2816
FFN_CONV_WIDTH = 3
LN_EPS = 1e-5
FORGET_BIAS_INIT = 3.0
ALPHA = (2.0 * DEPTH) ** 0.25
BETA = (8.0 * DEPTH) ** -0.25

GLU_OFF = 0
Q_OFF = GLU_OFF + 2 * CONV_CH
K_OFF = Q_OFF + ATTN_W
V_OFF = K_OFF + ATTN_W
F_OFF = V_OFF + ATTN_W
G_OFF = F_OFF + ATTN_HEADS
N_IN = G_OFF + 2 * D_MODEL

kernel_name = 'hybrid_conformer_fox_convffn_deepnorm_adaln'


def layer_norm(x, g=None, b=None):
    xf = x.astype(jnp.float32)
    mu = jnp.mean(xf, axis=-1, keepdims=True)
    xc = xf - mu
    var = jnp.mean(xc * xc, axis=-1, keepdims=True)
    y = (xc * lax.rsqrt(var + LN_EPS)).astype(x.dtype)
    if g is not None:
        y = y * g + b
    return y


def causal_dwconv(x, w, b):
    k_w, ch = w.shape
    y = lax.conv_general_dilated(
        x, w[:, None, :], window_strides=(1,), padding=[(k_w - 1, 0)],
        dimension_numbers=('NWC', 'WIO', 'NWC'), feature_group_count=ch)
    return y + b


def forgetting_attention(q, k, v, log_f):
    seq = q.shape[1]
    scale = HEAD_DIM ** -0.5
    qh = jnp.transpose(q, (0, 2, 1, 3)) * scale
    kh = jnp.transpose(k, (0, 2, 1, 3))
    vh = jnp.transpose(v, (0, 2, 1, 3))
    cum = jnp.transpose(jnp.cumsum(log_f, axis=1), (0, 2, 1))
    neg = jnp.finfo(jnp.float32).min
    outs = []
    for i in range(seq // Q_BLOCK):
        q0 = i * Q_BLOCK
        q1 = q0 + Q_BLOCK
        qb = qh[:, :, q0:q1]
        kb = kh[:, :, :q1]
        vb = vh[:, :, :q1]
        logits = jnp.einsum('bhqd,bhkd->bhqk', qb, kb).astype(jnp.float32)
        logits = logits + cum[:, :, q0:q1, None] - cum[:, :, None, :q1]
        mask = (q0 + jnp.arange(Q_BLOCK))[:, None] >= jnp.arange(q1)[None, :]
        logits = jnp.where(mask[None, None], logits, neg)
        p = jax.nn.softmax(logits, axis=-1).astype(v.dtype)
        outs.append(jnp.einsum('bhqk,bhkd->bhqd', p, vb))
    o = jnp.concatenate(outs, axis=2)
    bsz = q.shape[0]
    return jnp.transpose(o, (0, 2, 1, 3)).reshape(bsz, seq, ATTN_W)


def hybrid_layer(x, c_act, w_ada, b_ada, w_in, b_in, conv_a_w, conv_a_b,
                 ln_conv_g, ln_conv_b, w_conv_proj, w_attn_proj, w_mix_out, b_mix_out,
                 ln1_g, ln1_b, w_ffn_up, ffn_conv_w, ffn_conv_b, w_ffn_down, ln2_g, ln2_b):
    bsz, seq, _ = x.shape
    mod = c_act @ w_ada + b_ada
    shift1, scale1, gate1, shift2, scale2, gate2 = jnp.split(mod[:, None, :], 6, axis=-1)

    u = layer_norm(x) * (1.0 + scale1) + shift1
    z = u @ w_in + b_in

    a = z[..., GLU_OFF:GLU_OFF + CONV_CH] * jax.nn.sigmoid(z[..., GLU_OFF + CONV_CH:Q_OFF])
    a = causal_dwconv(a, conv_a_w, conv_a_b)
    a = jax.nn.silu(layer_norm(a, ln_conv_g, ln_conv_b))
    y_a = a @ w_conv_proj

    q = z[..., Q_OFF:K_OFF].reshape(bsz, seq, ATTN_HEADS, HEAD_DIM)
    k = z[..., K_OFF:V_OFF].reshape(bsz, seq, ATTN_HEADS, HEAD_DIM)
    v = z[..., V_OFF:F_OFF].reshape(bsz, seq, ATTN_HEADS, HEAD_DIM)
    log_f = jax.nn.log_sigmoid(z[..., F_OFF:G_OFF].astype(jnp.float32))
    y_b = forgetting_attention(q, k, v, log_f) @ w_attn_proj

    g_a = jax.nn.sigmoid(z[..., G_OFF:G_OFF + D_MODEL])
    g_b = jax.nn.sigmoid(z[..., G_OFF + D_MODEL:N_IN])
    mix = (g_a * y_a + g_b * y_b) @ w_mix_out + b_mix_out
    x = layer_norm(ALPHA * x + (1.0 + gate1) * mix, ln1_g, ln1_b)

    u = layer_norm(x) * (1.0 + scale2) + shift2
    h = causal_dwconv(u @ w_ffn_up, ffn_conv_w, ffn_conv_b)
    f = jax.nn.gelu(h[..., :D_FF], approximate=False) * h[..., D_FF:]
    ffn = f @ w_ffn_down
    x = layer_norm(ALPHA * x + (1.0 + gate2) * ffn, ln2_g, ln2_b)
    return x


def _fwd_setup_inputs(seed: int = 0) -> dict:
    key = jax.random.key(seed)
    ks = jax.random.split(key, 24)

    def nrm(k, shape, scale):
        return jax.random.normal(k, shape, jnp.float32) * scale

    L, D = DEPTH, D_MODEL
    w_in = nrm(ks[4], (L, D, N_IN), D ** -0.5)
    w_in = w_in.at[:, :, V_OFF:F_OFF].multiply(BETA)
    b_in = nrm(ks[5], (L, N_IN), 0.02).at[:, F_OFF:G_OFF].add(FORGET_BIAS_INIT)
    return {
        'x': nrm(ks[0], (BATCH, SEQ, D), 1.0),
        'c': nrm(ks[1], (BATCH, D), 1.0),
        'w_ada': nrm(ks[2], (L, D, 6 * D), 0.5 * D ** -0.5),
        'b_ada': nrm(ks[3], (L, 6 * D), 0.02),
        'w_in': w_in,
        'b_in': b_in,
        'conv_a_w': nrm(ks[6], (L, CONV_WIDTH, CONV_CH), CONV_WIDTH ** -0.5),
        'conv_a_b': nrm(ks[7], (L, CONV_CH), 0.02),
        'ln_conv_g': 1.0 + nrm(ks[8], (L, CONV_CH), 0.02),
        'ln_conv_b': nrm(ks[9], (L, CONV_CH), 0.02),
        'w_conv_proj': nrm(ks[10], (L, CONV_CH, D), BETA * CONV_CH ** -0.5),
        'w_attn_proj': nrm(ks[11], (L, ATTN_W, D), BETA * ATTN_W ** -0.5),
        'w_mix_out': nrm(ks[12], (L, D, D), BETA * D ** -0.5),
        'b_mix_out': nrm(ks[13], (L, D), 0.02),
        'ln1_g': 1.0 + nrm(ks[14], (L, D), 0.02),
        'ln1_b': nrm(ks[15], (L, D), 0.02),
        'w_ffn_up': nrm(ks[16], (L, D, 2 * D_FF), BETA * D ** -0.5),
        'ffn_conv_w': nrm(ks[17], (L, FFN_CONV_WIDTH, 2 * D_FF), FFN_CONV_WIDTH ** -0.5),
        'ffn_conv_b': nrm(ks[18], (L, 2 * D_FF), 0.02),
        'w_ffn_down': nrm(ks[19], (L, D_FF, D), BETA * D_FF ** -0.5),
        'ln2_g': 1.0 + nrm(ks[20], (L, D), 0.02),
        'ln2_b': nrm(ks[21], (L, D), 0.02),
    }


def _fwd_reference(x, c, w_ada, b_ada, w_in, b_in, conv_a_w, conv_a_b, ln_conv_g, ln_conv_b,
              w_conv_proj, w_attn_proj, w_mix_out, b_mix_out, ln1_g, ln1_b,
              w_ffn_up, ffn_conv_w, ffn_conv_b, w_ffn_down, ln2_g, ln2_b):
    c_act = jax.nn.silu(c)
    for l in range(DEPTH):
        x = hybrid_layer(
            x, c_act, w_ada[l], b_ada[l], w_in[l], b_in[l], conv_a_w[l], conv_a_b[l],
            ln_conv_g[l], ln_conv_b[l], w_conv_proj[l], w_attn_proj[l], w_mix_out[l],
            b_mix_out[l], ln1_g[l], ln1_b[l], w_ffn_up[l], ffn_conv_w[l], ffn_conv_b[l],
            w_ffn_down[l], ln2_g[l], ln2_b[l])
    return x


import jax as _jax
import jax.numpy as _jnp

TWIN_FORMAT = 'train_step'
FWD_PARAMS = ['x', 'c', 'w_ada', 'b_ada', 'w_in', 'b_in', 'conv_a_w', 'conv_a_b', 'ln_conv_g', 'ln_conv_b', 'w_conv_proj', 'w_attn_proj', 'w_mix_out', 'b_mix_out', 'ln1_g', 'ln1_b', 'w_ffn_up', 'ffn_conv_w', 'ffn_conv_b', 'w_ffn_down', 'ln2_g', 'ln2_b']
TWIN_WEIGHTS = ['w_ada', 'b_ada', 'w_in', 'b_in', 'conv_a_w', 'conv_a_b', 'ln_conv_g', 'ln_conv_b', 'w_conv_proj', 'w_attn_proj', 'w_mix_out', 'b_mix_out', 'ln1_g', 'ln1_b', 'w_ffn_up', 'ffn_conv_w', 'ffn_conv_b', 'w_ffn_down', 'ln2_g', 'ln2_b']
TWIN_DIFF_INPUT = 'x'
TWIN_INPUTS = ['x', 'c', 'w_ada', 'b_ada', 'w_in', 'b_in', 'conv_a_w', 'conv_a_b', 'ln_conv_g', 'ln_conv_b', 'w_conv_proj', 'w_attn_proj', 'w_mix_out', 'b_mix_out', 'ln1_g', 'ln1_b', 'w_ffn_up', 'ffn_conv_w', 'ffn_conv_b', 'w_ffn_down', 'ln2_g', 'ln2_b', 'loss_target', 'm_w_ada', 'm_b_ada', 'm_w_in', 'm_b_in', 'm_conv_a_w', 'm_conv_a_b', 'm_ln_conv_g', 'm_ln_conv_b', 'm_w_conv_proj', 'm_w_attn_proj', 'm_w_mix_out', 'm_b_mix_out', 'm_ln1_g', 'm_ln1_b', 'm_w_ffn_up', 'm_ffn_conv_w', 'm_ffn_conv_b', 'm_w_ffn_down', 'm_ln2_g', 'm_ln2_b', 'v_w_ada', 'v_b_ada', 'v_w_in', 'v_b_in', 'v_conv_a_w', 'v_conv_a_b', 'v_ln_conv_g', 'v_ln_conv_b', 'v_w_conv_proj', 'v_w_attn_proj', 'v_w_mix_out', 'v_b_mix_out', 'v_ln1_g', 'v_ln1_b', 'v_w_ffn_up', 'v_ffn_conv_w', 'v_ffn_conv_b', 'v_w_ffn_down', 'v_ln2_g', 'v_ln2_b']
TWIN_OUTPUTS = ['loss', 'grad_x', 'grad_w_ada', 'grad_b_ada', 'grad_w_in', 'grad_b_in', 'grad_conv_a_w', 'grad_conv_a_b', 'grad_ln_conv_g', 'grad_ln_conv_b', 'grad_w_conv_proj', 'grad_w_attn_proj', 'grad_w_mix_out', 'grad_b_mix_out', 'grad_ln1_g', 'grad_ln1_b', 'grad_w_ffn_up', 'grad_ffn_conv_w', 'grad_ffn_conv_b', 'grad_w_ffn_down', 'grad_ln2_g', 'grad_ln2_b', 'delta_w_ada', 'delta_b_ada', 'delta_w_in', 'delta_b_in', 'delta_conv_a_w', 'delta_conv_a_b', 'delta_ln_conv_g', 'delta_ln_conv_b', 'delta_w_conv_proj', 'delta_w_attn_proj', 'delta_w_mix_out', 'delta_b_mix_out', 'delta_ln1_g', 'delta_ln1_b', 'delta_w_ffn_up', 'delta_ffn_conv_w', 'delta_ffn_conv_b', 'delta_w_ffn_down', 'delta_ln2_g', 'delta_ln2_b', 'new_m_w_ada', 'new_m_b_ada', 'new_m_w_in', 'new_m_b_in', 'new_m_conv_a_w', 'new_m_conv_a_b', 'new_m_ln_conv_g', 'new_m_ln_conv_b', 'new_m_w_conv_proj', 'new_m_w_attn_proj', 'new_m_w_mix_out', 'new_m_b_mix_out', 'new_m_ln1_g', 'new_m_ln1_b', 'new_m_w_ffn_up', 'new_m_ffn_conv_w', 'new_m_ffn_conv_b', 'new_m_w_ffn_down', 'new_m_ln2_g', 'new_m_ln2_b', 'new_v_w_ada', 'new_v_b_ada', 'new_v_w_in', 'new_v_b_in', 'new_v_conv_a_w', 'new_v_conv_a_b', 'new_v_ln_conv_g', 'new_v_ln_conv_b', 'new_v_w_conv_proj', 'new_v_w_attn_proj', 'new_v_w_mix_out', 'new_v_b_mix_out', 'new_v_ln1_g', 'new_v_ln1_b', 'new_v_w_ffn_up', 'new_v_ffn_conv_w', 'new_v_ffn_conv_b', 'new_v_w_ffn_down', 'new_v_ln2_g', 'new_v_ln2_b']
TWIN_LEAF_KINDS = {'loss': 'loss', 'grad_x': 'grad_x', 'grad_w_ada': 'grad_w', 'grad_b_ada': 'grad_w', 'grad_w_in': 'grad_w', 'grad_b_in': 'grad_w', 'grad_conv_a_w': 'grad_w', 'grad_conv_a_b': 'grad_w', 'grad_ln_conv_g': 'grad_w', 'grad_ln_conv_b': 'grad_w', 'grad_w_conv_proj': 'grad_w', 'grad_w_attn_proj': 'grad_w', 'grad_w_mix_out': 'grad_w', 'grad_b_mix_out': 'grad_w', 'grad_ln1_g': 'grad_w', 'grad_ln1_b': 'grad_w', 'grad_w_ffn_up': 'grad_w', 'grad_ffn_conv_w': 'grad_w', 'grad_ffn_conv_b': 'grad_w', 'grad_w_ffn_down': 'grad_w', 'grad_ln2_g': 'grad_w', 'grad_ln2_b': 'grad_w', 'delta_w_ada': 'delta_w', 'delta_b_ada': 'delta_w', 'delta_w_in': 'delta_w', 'delta_b_in': 'delta_w', 'delta_conv_a_w': 'delta_w', 'delta_conv_a_b': 'delta_w', 'delta_ln_conv_g': 'delta_w', 'delta_ln_conv_b': 'delta_w', 'delta_w_conv_proj': 'delta_w', 'delta_w_attn_proj': 'delta_w', 'delta_w_mix_out': 'delta_w', 'delta_b_mix_out': 'delta_w', 'delta_ln1_g': 'delta_w', 'delta_ln1_b': 'delta_w', 'delta_w_ffn_up': 'delta_w', 'delta_ffn_conv_w': 'delta_w', 'delta_ffn_conv_b': 'delta_w', 'delta_w_ffn_down': 'delta_w', 'delta_ln2_g': 'delta_w', 'delta_ln2_b': 'delta_w', 'new_m_w_ada': 'new_m', 'new_m_b_ada': 'new_m', 'new_m_w_in': 'new_m', 'new_m_b_in': 'new_m', 'new_m_conv_a_w': 'new_m', 'new_m_conv_a_b': 'new_m', 'new_m_ln_conv_g': 'new_m', 'new_m_ln_conv_b': 'new_m', 'new_m_w_conv_proj': 'new_m', 'new_m_w_attn_proj': 'new_m', 'new_m_w_mix_out': 'new_m', 'new_m_b_mix_out': 'new_m', 'new_m_ln1_g': 'new_m', 'new_m_ln1_b': 'new_m', 'new_m_w_ffn_up': 'new_m', 'new_m_ffn_conv_w': 'new_m', 'new_m_ffn_conv_b': 'new_m', 'new_m_w_ffn_down': 'new_m', 'new_m_ln2_g': 'new_m', 'new_m_ln2_b': 'new_m', 'new_v_w_ada': 'new_v', 'new_v_b_ada': 'new_v', 'new_v_w_in': 'new_v', 'new_v_b_in': 'new_v', 'new_v_conv_a_w': 'new_v', 'new_v_conv_a_b': 'new_v', 'new_v_ln_conv_g': 'new_v', 'new_v_ln_conv_b': 'new_v', 'new_v_w_conv_proj': 'new_v', 'new_v_w_attn_proj': 'new_v', 'new_v_w_mix_out': 'new_v', 'new_v_b_mix_out': 'new_v', 'new_v_ln1_g': 'new_v', 'new_v_ln1_b': 'new_v', 'new_v_w_ffn_up': 'new_v', 'new_v_ffn_conv_w': 'new_v', 'new_v_ffn_conv_b': 'new_v', 'new_v_w_ffn_down': 'new_v', 'new_v_ln2_g': 'new_v', 'new_v_ln2_b': 'new_v'}


def _forward(args):
    return _fwd_reference(*[args[k] for k in FWD_PARAMS])


def _output_shape():
    out = _jax.eval_shape(lambda: _forward(_fwd_setup_inputs(0)))
    return out.shape, out.dtype

N_MICROBATCH = 1
ADAM_LR = 0.001
ADAM_B1 = 0.9
ADAM_B2 = 0.999
ADAM_EPS = 1e-08
ADAM_WD = 0.01
ADAM_STEP = 10
PER_EXAMPLE_BATCH_AXIS = {'x': 0, 'c': 0, 'loss_target': 0}
SHARED_INPUTS = []
_WEIGHT_DTYPES = {'w_ada': _jnp.float32, 'b_ada': _jnp.float32, 'w_in': _jnp.float32, 'b_in': _jnp.float32, 'conv_a_w': _jnp.float32, 'conv_a_b': _jnp.float32, 'ln_conv_g': _jnp.float32, 'ln_conv_b': _jnp.float32, 'w_conv_proj': _jnp.float32, 'w_attn_proj': _jnp.float32, 'w_mix_out': _jnp.float32, 'b_mix_out': _jnp.float32, 'ln1_g': _jnp.float32, 'ln1_b': _jnp.float32, 'w_ffn_up': _jnp.float32, 'ffn_conv_w': _jnp.float32, 'ffn_conv_b': _jnp.float32, 'w_ffn_down': _jnp.float32, 'ln2_g': _jnp.float32, 'ln2_b': _jnp.float32}
MOMENT_SCALE = {'w_ada': 1.332449e-02, 'b_ada': 2.476309e-02, 'w_in': 1.129357e-02, 'b_in': 3.219661e-02, 'conv_a_w': 2.403481e-02, 'conv_a_b': 6.668568e-02, 'ln_conv_g': 3.809548e-02, 'ln_conv_b': 4.831499e-02, 'w_conv_proj': 3.751304e-02, 'w_attn_proj': 1.403548e-02, 'w_mix_out': 3.936684e-02, 'b_mix_out': 5.384683e-01, 'ln1_g': 1.966247e+00, 'ln1_b': 7.634213e-01, 'w_ffn_up': 2.075994e-02, 'ffn_conv_w': 1.044131e-02, 'ffn_conv_b': 2.114487e-02, 'w_ffn_down': 3.395021e-02, 'ln2_g': 4.534456e+01, 'ln2_b': 1.981982e+00}


def _to_microbatches(a, axis):
    t = _jnp.moveaxis(a, axis, 0)
    t = t.reshape((N_MICROBATCH, t.shape[0] // N_MICROBATCH) + t.shape[1:])
    return _jnp.moveaxis(t, 1, axis + 1)


def setup_inputs(seed: int = 0) -> dict:
    inp = _fwd_setup_inputs(seed)
    key = _jax.random.fold_in(_jax.random.key(seed), 7919)
    shape, _ = _output_shape()
    out = dict(inp)
    out["loss_target"] = _jax.random.normal(_jax.random.fold_in(key, 0), shape, _jnp.float32)
    for i, name in enumerate(TWIN_WEIGHTS):
        w = inp[name].astype(_jnp.float32)
        if MOMENT_SCALE is None:
            s = _jnp.sqrt(_jnp.mean(_jnp.square(w)) + 1e-30)
        else:
            s = MOMENT_SCALE[name]
        km, kv = _jax.random.split(_jax.random.fold_in(key, i + 1))
        out[name] = w
        out["m_" + name] = s * _jax.random.normal(km, w.shape, _jnp.float32)
        out["v_" + name] = (s * s) * _jax.random.uniform(kv, w.shape, _jnp.float32, 0.5, 1.5)
    if N_MICROBATCH > 1:
        for name, axis in PER_EXAMPLE_BATCH_AXIS.items():
            out[name] = _to_microbatches(out[name], axis)
    return {'x': out['x'], 'c': out['c'], 'w_ada': out['w_ada'], 'b_ada': out['b_ada'], 'w_in': out['w_in'], 'b_in': out['b_in'], 'conv_a_w': out['conv_a_w'], 'conv_a_b': out['conv_a_b'], 'ln_conv_g': out['ln_conv_g'], 'ln_conv_b': out['ln_conv_b'], 'w_conv_proj': out['w_conv_proj'], 'w_attn_proj': out['w_attn_proj'], 'w_mix_out': out['w_mix_out'], 'b_mix_out': out['b_mix_out'], 'ln1_g': out['ln1_g'], 'ln1_b': out['ln1_b'], 'w_ffn_up': out['w_ffn_up'], 'ffn_conv_w': out['ffn_conv_w'], 'ffn_conv_b': out['ffn_conv_b'], 'w_ffn_down': out['w_ffn_down'], 'ln2_g': out['ln2_g'], 'ln2_b': out['ln2_b'], 'loss_target': out['loss_target'], 'm_w_ada': out['m_w_ada'], 'm_b_ada': out['m_b_ada'], 'm_w_in': out['m_w_in'], 'm_b_in': out['m_b_in'], 'm_conv_a_w': out['m_conv_a_w'], 'm_conv_a_b': out['m_conv_a_b'], 'm_ln_conv_g': out['m_ln_conv_g'], 'm_ln_conv_b': out['m_ln_conv_b'], 'm_w_conv_proj': out['m_w_conv_proj'], 'm_w_attn_proj': out['m_w_attn_proj'], 'm_w_mix_out': out['m_w_mix_out'], 'm_b_mix_out': out['m_b_mix_out'], 'm_ln1_g': out['m_ln1_g'], 'm_ln1_b': out['m_ln1_b'], 'm_w_ffn_up': out['m_w_ffn_up'], 'm_ffn_conv_w': out['m_ffn_conv_w'], 'm_ffn_conv_b': out['m_ffn_conv_b'], 'm_w_ffn_down': out['m_w_ffn_down'], 'm_ln2_g': out['m_ln2_g'], 'm_ln2_b': out['m_ln2_b'], 'v_w_ada': out['v_w_ada'], 'v_b_ada': out['v_b_ada'], 'v_w_in': out['v_w_in'], 'v_b_in': out['v_b_in'], 'v_conv_a_w': out['v_conv_a_w'], 'v_conv_a_b': out['v_conv_a_b'], 'v_ln_conv_g': out['v_ln_conv_g'], 'v_ln_conv_b': out['v_ln_conv_b'], 'v_w_conv_proj': out['v_w_conv_proj'], 'v_w_attn_proj': out['v_w_attn_proj'], 'v_w_mix_out': out['v_w_mix_out'], 'v_b_mix_out': out['v_b_mix_out'], 'v_ln1_g': out['v_ln1_g'], 'v_ln1_b': out['v_ln1_b'], 'v_w_ffn_up': out['v_w_ffn_up'], 'v_ffn_conv_w': out['v_ffn_conv_w'], 'v_ffn_conv_b': out['v_ffn_conv_b'], 'v_w_ffn_down': out['v_w_ffn_down'], 'v_ln2_g': out['v_ln2_g'], 'v_ln2_b': out['v_ln2_b']}


def _loss(weights, diff, rest, loss_target):
    with _jax.named_scope("forward"):
        args = {**rest, TWIN_DIFF_INPUT: diff, **{k: w.astype(_WEIGHT_DTYPES[k]) for k, w in weights.items()}}
        y = _forward(args)
    with _jax.named_scope("loss_head"):
        err = _jnp.square(y.astype(_jnp.float32) - loss_target)
        return 0.5 * _jnp.sum(_jnp.mean(err, axis=-1)) if err.ndim else 0.5 * err


def _adamw(w, g, m, v):
    m = ADAM_B1 * m + (1.0 - ADAM_B1) * g
    v = ADAM_B2 * v + (1.0 - ADAM_B2) * _jnp.square(g)
    m_hat = m / (1.0 - ADAM_B1 ** ADAM_STEP)
    v_hat = v / (1.0 - ADAM_B2 ** ADAM_STEP)
    delta = -ADAM_LR * (m_hat / (_jnp.sqrt(v_hat) + ADAM_EPS) + ADAM_WD * w)
    return delta, m, v


def reference(x, c, w_ada, b_ada, w_in, b_in, conv_a_w, conv_a_b, ln_conv_g, ln_conv_b, w_conv_proj, w_attn_proj, w_mix_out, b_mix_out, ln1_g, ln1_b, w_ffn_up, ffn_conv_w, ffn_conv_b, w_ffn_down, ln2_g, ln2_b, loss_target, m_w_ada, m_b_ada, m_w_in, m_b_in, m_conv_a_w, m_conv_a_b, m_ln_conv_g, m_ln_conv_b, m_w_conv_proj, m_w_attn_proj, m_w_mix_out, m_b_mix_out, m_ln1_g, m_ln1_b, m_w_ffn_up, m_ffn_conv_w, m_ffn_conv_b, m_w_ffn_down, m_ln2_g, m_ln2_b, v_w_ada, v_b_ada, v_w_in, v_b_in, v_conv_a_w, v_conv_a_b, v_ln_conv_g, v_ln_conv_b, v_w_conv_proj, v_w_attn_proj, v_w_mix_out, v_b_mix_out, v_ln1_g, v_ln1_b, v_w_ffn_up, v_ffn_conv_w, v_ffn_conv_b, v_w_ffn_down, v_ln2_g, v_ln2_b):
    given = dict(x=x, c=c, w_ada=w_ada, b_ada=b_ada, w_in=w_in, b_in=b_in, conv_a_w=conv_a_w, conv_a_b=conv_a_b, ln_conv_g=ln_conv_g, ln_conv_b=ln_conv_b, w_conv_proj=w_conv_proj, w_attn_proj=w_attn_proj, w_mix_out=w_mix_out, b_mix_out=b_mix_out, ln1_g=ln1_g, ln1_b=ln1_b, w_ffn_up=w_ffn_up, ffn_conv_w=ffn_conv_w, ffn_conv_b=ffn_conv_b, w_ffn_down=w_ffn_down, ln2_g=ln2_g, ln2_b=ln2_b, loss_target=loss_target, m_w_ada=m_w_ada, m_b_ada=m_b_ada, m_w_in=m_w_in, m_b_in=m_b_in, m_conv_a_w=m_conv_a_w, m_conv_a_b=m_conv_a_b, m_ln_conv_g=m_ln_conv_g, m_ln_conv_b=m_ln_conv_b, m_w_conv_proj=m_w_conv_proj, m_w_attn_proj=m_w_attn_proj, m_w_mix_out=m_w_mix_out, m_b_mix_out=m_b_mix_out, m_ln1_g=m_ln1_g, m_ln1_b=m_ln1_b, m_w_ffn_up=m_w_ffn_up, m_ffn_conv_w=m_ffn_conv_w, m_ffn_conv_b=m_ffn_conv_b, m_w_ffn_down=m_w_ffn_down, m_ln2_g=m_ln2_g, m_ln2_b=m_ln2_b, v_w_ada=v_w_ada, v_b_ada=v_b_ada, v_w_in=v_w_in, v_b_in=v_b_in, v_conv_a_w=v_conv_a_w, v_conv_a_b=v_conv_a_b, v_ln_conv_g=v_ln_conv_g, v_ln_conv_b=v_ln_conv_b, v_w_conv_proj=v_w_conv_proj, v_w_attn_proj=v_w_attn_proj, v_w_mix_out=v_w_mix_out, v_b_mix_out=v_b_mix_out, v_ln1_g=v_ln1_g, v_ln1_b=v_ln1_b, v_w_ffn_up=v_w_ffn_up, v_ffn_conv_w=v_ffn_conv_w, v_ffn_conv_b=v_ffn_conv_b, v_w_ffn_down=v_w_ffn_down, v_ln2_g=v_ln2_g, v_ln2_b=v_ln2_b)
    weights = {n: given[n] for n in TWIN_WEIGHTS}
    shared = {n: given[n] for n in SHARED_INPUTS}
    per_example = {n: given[n] for n in ['x', 'c']}
    grad_fn = _jax.value_and_grad(_loss, argnums=(0, 1))

    def one_microbatch(ex, loss_target):
        ex = dict(ex)
        diff = ex.pop(TWIN_DIFF_INPUT)
        return grad_fn(weights, diff, {**shared, **ex}, loss_target)

    if N_MICROBATCH == 1:
        loss, (grad_w, grad_x) = one_microbatch(per_example, given["loss_target"])
    else:
        def body(carry, xs):
            loss_sum, grad_sum = carry
            l_k, (gw_k, gx_k) = one_microbatch(xs[0], xs[1])
            with _jax.named_scope("update"):
                return (loss_sum + l_k, _jax.tree.map(_jnp.add, grad_sum, gw_k)), gx_k

        init = (_jnp.zeros((), _jnp.float32), _jax.tree.map(_jnp.zeros_like, weights))
        (loss, grad_w), grad_x = _jax.lax.scan(body, init, (per_example, given["loss_target"]))
    with _jax.named_scope("update"):
        delta_w, new_m, new_v = {}, {}, {}
        for n in TWIN_WEIGHTS:
            delta_w[n], new_m[n], new_v[n] = _adamw(weights[n], grad_w[n], given["m_" + n], given["v_" + n])
    return (loss, grad_x, *[grad_w[n] for n in TWIN_WEIGHTS], *[delta_w[n] for n in TWIN_WEIGHTS],
            *[new_m[n] for n in TWIN_WEIGHTS], *[new_v[n] for n in TWIN_WEIGHTS])
```

```python
import functools
import math

import jax
import jax.numpy as jnp
from jax import lax
from jax.experimental import pallas as pl
from jax.experimental.pallas import tpu as pltpu

F32 = jnp.float32
BF16 = jnp.bfloat16
MESH = pl.DeviceIdType.MESH

LN_EPS = 1e-5
HEAD_DIM = 64
ATTN_SCALE = HEAD_DIM ** -0.5
NEG = -1e30
CONV_A_HALO = 32
FFN_CONV_HALO = 8
LANES = 128
PACK_COLS = 1024
N_CHIPS = 4
VMEM_LIMIT = 56 * 1024 * 1024

ADAM_LR = 0.001
ADAM_B1 = 0.9
ADAM_B2 = 0.999
ADAM_EPS = 1e-08
ADAM_WD = 0.01
ADAM_STEP = 10

SHARDED = (
    ("w_ada", 1, True), ("w_in", 1, True), ("conv_a_w", 1, False), ("w_conv_proj", 1, True),
    ("w_attn_proj", 1, True), ("w_mix_out", 0, True), ("w_ffn_up", 1, True),
    ("ffn_conv_w", 1, False), ("w_ffn_down", 0, True))
REPLICATED = ("b_ada", "b_in", "conv_a_b", "ln_conv_g", "ln_conv_b", "b_mix_out", "ln1_g", "ln1_b",
              "ffn_conv_b", "ln2_g", "ln2_b")
WEIGHTS = ("w_ada", "b_ada", "w_in", "b_in", "conv_a_w", "conv_a_b", "ln_conv_g", "ln_conv_b",
           "w_conv_proj", "w_attn_proj", "w_mix_out", "b_mix_out", "ln1_g", "ln1_b", "w_ffn_up",
           "ffn_conv_w", "ffn_conv_b", "w_ffn_down", "ln2_g", "ln2_b")


def _pick(n, cands):
    for cand in cands:
        if n % cand == 0:
            return cand
    return n


def _call(body, *, name, grid, in_specs, out_specs, out_shape, scratch=(), sem=None):
    return pl.pallas_call(
        body, name=name, grid=grid, in_specs=in_specs, out_specs=out_specs, out_shape=out_shape,
        scratch_shapes=list(scratch),
        compiler_params=pltpu.CompilerParams(dimension_semantics=sem, vmem_limit_bytes=VMEM_LIMIT))


def _sds(shape, dtype):
    return jax.ShapeDtypeStruct(tuple(shape), dtype)


def _matmul(a, b, mode, out_dtype, *, bias=None, add=None, name):
    if mode == "nn":
        (m, k), (_, n) = a.shape, b.shape
    elif mode == "nt":
        (m, k), (n, _) = a.shape, b.shape
    else:
        (k, m), (_, n) = a.shape, b.shape
    tm = _pick(m, (512, 256, 128)) if mode != "tn" else _pick(m, (512, 256, 128))
    tn = _pick(n, (512, 256, 128))
    tk = _pick(k, (512, 256, 128))
    nk = k // tk
    if mode == "nn":
        a_spec = pl.BlockSpec((tm, tk), lambda i, j, kk: (i, kk))
        b_spec = pl.BlockSpec((tk, tn), lambda i, j, kk: (kk, j))
        dims = (((1,), (0,)), ((), ()))
    elif mode == "nt":
        a_spec = pl.BlockSpec((tm, tk), lambda i, j, kk: (i, kk))
        b_spec = pl.BlockSpec((tn, tk), lambda i, j, kk: (j, kk))
        dims = (((1,), (1,)), ((), ()))
    else:
        a_spec = pl.BlockSpec((tk, tm), lambda i, j, kk: (kk, i))
        b_spec = pl.BlockSpec((tk, tn), lambda i, j, kk: (kk, j))
        dims = (((0,), (0,)), ((), ()))
    in_specs = [a_spec, b_spec]
    operands = [a, b]
    if bias is not None:
        in_specs.append(pl.BlockSpec((1, tn), lambda i, j, kk: (0, j)))
        operands.append(bias)
    if add is not None:
        in_specs.append(pl.BlockSpec((tm, tn), lambda i, j, kk: (i, j)))
        operands.append(add)

    def body(a_ref, b_ref, *rest):
        rest = list(rest)
        bias_ref = rest.pop(0) if bias is not None else None
        add_ref = rest.pop(0) if add is not None else None
        o_ref, acc_ref = rest
        kk = pl.program_id(2)

        @pl.when(kk == 0)
        def _():
            acc_ref[...] = jnp.zeros_like(acc_ref)

        acc_ref[...] += lax.dot_general(a_ref[...].astype(BF16), b_ref[...].astype(BF16), dims,
                                        preferred_element_type=F32)

        @pl.when(kk == nk - 1)
        def _():
            r = acc_ref[...]
            if bias_ref is not None:
                r = r + bias_ref[...]
            if add_ref is not None:
                r = r + add_ref[...]
            o_ref[...] = r.astype(o_ref.dtype)

    return _call(body, name=name, grid=(m // tm, n // tn, nk), in_specs=in_specs,
                 out_specs=pl.BlockSpec((tm, tn), lambda i, j, kk: (i, j)),
                 out_shape=_sds((m, n), out_dtype), scratch=[pltpu.VMEM((tm, tn), F32)],
                 sem=("parallel", "parallel", "arbitrary"))(*operands)


def _colsum(x, *, name):
    rows, n = x.shape
    tr = _pick(rows, (512, 256, 128))
    tn = _pick(n, (512, 256, 128))

    def body(x_ref, o_ref):
        @pl.when(pl.program_id(1) == 0)
        def _():
            o_ref[...] = jnp.zeros_like(o_ref)

        o_ref[...] += jnp.sum(x_ref[...].astype(F32), axis=0, keepdims=True)

    return _call(body, name=name, grid=(n // tn, rows // tr),
                 in_specs=[pl.BlockSpec((tr, tn), lambda j, i: (i, j))],
                 out_specs=pl.BlockSpec((1, tn), lambda j, i: (0, j)),
                 out_shape=_sds((1, n), F32), sem=("parallel", "arbitrary"))(x)


def _ln_stats(x):
    mu = jnp.mean(x, axis=-1, keepdims=True)
    xc = x - mu
    var = jnp.mean(xc * xc, axis=-1, keepdims=True)
    rstd = lax.rsqrt(var + LN_EPS)
    return xc * rstd, rstd


def _ln_bwd(dn, n, rstd):
    return rstd * (dn - jnp.mean(dn, axis=-1, keepdims=True) - n * jnp.mean(dn * n, axis=-1, keepdims=True))


def _seq_tiles(t, bsz, cands=(256, 128, 64, 32, 16, 8)):
    s = t // bsz
    ts = _pick(s, cands)
    return s, ts, s // ts


def _ln_mod_fwd(x, scale, shift, bsz, *, name):
    t, d = x.shape
    _, ts, ns = _seq_tiles(t, bsz)

    def body(x_ref, sc_ref, sh_ref, u_ref):
        n, _ = _ln_stats(x_ref[...])
        u_ref[...] = (n * (1.0 + sc_ref[0]) + sh_ref[0]).astype(u_ref.dtype)

    row = pl.BlockSpec((ts, d), lambda b, i: (b * ns + i, 0))
    per = pl.BlockSpec((1, 1, d), lambda b, i: (b, 0, 0))
    return _call(body, name=name, grid=(bsz, ns), in_specs=[row, per, per], out_specs=row,
                 out_shape=_sds((t, d), BF16), sem=("parallel", "parallel"))(x, scale, shift)


def _ln_mod_bwd(du, x, scale, dr, alpha, bsz, *, name):
    t, d = x.shape
    _, ts, ns = _seq_tiles(t, bsz)

    def body(du_ref, x_ref, sc_ref, dr_ref, dx_ref, dsc_ref, dsh_ref):
        @pl.when(pl.program_id(1) == 0)
        def _():
            dsc_ref[...] = jnp.zeros_like(dsc_ref)
            dsh_ref[...] = jnp.zeros_like(dsh_ref)

        du_v = du_ref[...]
        n, rstd = _ln_stats(x_ref[...])
        dsc_ref[0] += jnp.sum(du_v * n, axis=0, keepdims=True)
        dsh_ref[0] += jnp.sum(du_v, axis=0, keepdims=True)
        dn = du_v * (1.0 + sc_ref[0])
        dx_ref[...] = alpha * dr_ref[...] + _ln_bwd(dn, n, rstd)

    row = pl.BlockSpec((ts, d), lambda b, i: (b * ns + i, 0))
    per = pl.BlockSpec((1, 1, d), lambda b, i: (b, 0, 0))
    return _call(body, name=name, grid=(bsz, ns), in_specs=[row, row, per, row],
                 out_specs=[row, per, per],
                 out_shape=[_sds((t, d), F32), _sds((bsz, 1, d), F32), _sds((bsz, 1, d), F32)],
                 sem=("parallel", "arbitrary"))(du, x, scale, dr)


def _ln_res_fwd(x, y, gate, g, b, alpha, bsz, *, name):
    t, d = x.shape
    _, ts, ns = _seq_tiles(t, bsz)

    def body(x_ref, y_ref, gt_ref, g_ref, b_ref, o_ref):
        r = alpha * x_ref[...] + (1.0 + gt_ref[0]) * y_ref[...]
        n, _ = _ln_stats(r)
        o_ref[...] = n * g_ref[...] + b_ref[...]

    row = pl.BlockSpec((ts, d), lambda bb, i: (bb * ns + i, 0))
    per = pl.BlockSpec((1, 1, d), lambda bb, i: (bb, 0, 0))
    vec = pl.BlockSpec((1, d), lambda bb, i: (0, 0))
    return _call(body, name=name, grid=(bsz, ns), in_specs=[row, row, per, vec, vec], out_specs=row,
                 out_shape=_sds((t, d), F32), sem=("parallel", "parallel"))(x, y, gate, g, b)


def _ln_res_bwd(do, x, y, gate, g, alpha, bsz, *, name):
    t, d = x.shape
    _, ts, ns = _seq_tiles(t, bsz)

    def body(do_ref, x_ref, y_ref, gt_ref, g_ref, dr_ref, dy_ref, dgt_ref, dg_ref, db_ref, dys_ref):
        first_tile = pl.program_id(1) == 0

        @pl.when(first_tile)
        def _():
            dgt_ref[...] = jnp.zeros_like(dgt_ref)

        @pl.when(jnp.logical_and(first_tile, pl.program_id(0) == 0))
        def _():
            dg_ref[...] = jnp.zeros_like(dg_ref)
            db_ref[...] = jnp.zeros_like(db_ref)
            dys_ref[...] = jnp.zeros_like(dys_ref)

        do_v = do_ref[...]
        y_v = y_ref[...]
        one_gate = 1.0 + gt_ref[0]
        r = alpha * x_ref[...] + one_gate * y_v
        n, rstd = _ln_stats(r)
        dg_ref[...] += jnp.sum(do_v * n, axis=0, keepdims=True)
        db_ref[...] += jnp.sum(do_v, axis=0, keepdims=True)
        dr = _ln_bwd(do_v * g_ref[...], n, rstd)
        dr_ref[...] = dr
        dy = one_gate * dr
        dy_ref[...] = dy.astype(dy_ref.dtype)
        dys_ref[...] += jnp.sum(dy, axis=0, keepdims=True)
        dgt_ref[0] += jnp.sum(dr * y_v, axis=0, keepdims=True)

    row = pl.BlockSpec((ts, d), lambda bb, i: (bb * ns + i, 0))
    per = pl.BlockSpec((1, 1, d), lambda bb, i: (bb, 0, 0))
    vec = pl.BlockSpec((1, d), lambda bb, i: (0, 0))
    return _call(body, name=name, grid=(bsz, ns), in_specs=[row, row, row, per, vec],
                 out_specs=[row, row, per, vec, vec, vec],
                 out_shape=[_sds((t, d), F32), _sds((t, d), BF16), _sds((bsz, 1, d), F32),
                            _sds((1, d), F32), _sds((1, d), F32), _sds((1, d), F32)],
                 sem=("arbitrary", "arbitrary"))(do, x, y, gate, g)


def _loss_head(y, target, *, name):
    t, d = y.shape
    tr = _pick(t, (256, 128, 64, 32, 16, 8))

    def body(y_ref, t_ref, dy_ref, s_ref):
        @pl.when(pl.program_id(0) == 0)
        def _():
            s_ref[...] = jnp.zeros_like(s_ref)

        e = y_ref[...] - t_ref[...]
        dy_ref[...] = e * (1.0 / d)
        s_ref[...] += jnp.sum(e * e, axis=0, keepdims=True)

    row = pl.BlockSpec((tr, d), lambda i: (i, 0))
    return _call(body, name=name, grid=(t // tr,), in_specs=[row, row],
                 out_specs=[row, pl.BlockSpec((1, d), lambda i: (0, 0))],
                 out_shape=[_sds((t, d), F32), _sds((1, d), F32)], sem=("arbitrary",))(y, target)


def _sigmoid(v):
    return 1.0 / (1.0 + jnp.exp(-v))


def _silu_rows(c, rows, *, name):
    bsz, d = c.shape
    cp = jnp.pad(c, ((0, rows - bsz), (0, 0)))

    def body(c_ref, o_ref):
        v = c_ref[...]
        o_ref[...] = (v * _sigmoid(v)).astype(o_ref.dtype)

    full = pl.BlockSpec((rows, d), lambda i: (0, 0))
    return _call(body, name=name, grid=(1,), in_specs=[full], out_specs=full,
                 out_shape=_sds((rows, d), BF16), sem=("arbitrary",))(cp)


def _glu_fwd(z, ch, *, name):
    t = z.shape[0]
    tr = _pick(t, (256, 128, 64, 32, 16, 8))

    def body(z_ref, o_ref):
        o_ref[...] = z_ref[:, :ch] * _sigmoid(z_ref[:, ch:])

    return _call(body, name=name, grid=(t // tr,),
                 in_specs=[pl.BlockSpec((tr, 2 * ch), lambda i: (i, 0))],
                 out_specs=pl.BlockSpec((tr, ch), lambda i: (i, 0)),
                 out_shape=_sds((t, ch), F32), sem=("parallel",))(z)


def _glu_bwd(z, da0, ch, *, name):
    t = z.shape[0]
    tr = _pick(t, (256, 128, 64, 32, 16, 8))

    def body(z_ref, d_ref, o_ref):
        s = _sigmoid(z_ref[:, ch:])
        d = d_ref[...]
        o_ref[:, :ch] = (d * s).astype(o_ref.dtype)
        o_ref[:, ch:] = (d * z_ref[:, :ch] * s * (1.0 - s)).astype(o_ref.dtype)

    return _call(body, name=name, grid=(t // tr,),
                 in_specs=[pl.BlockSpec((tr, 2 * ch), lambda i: (i, 0)),
                           pl.BlockSpec((tr, ch), lambda i: (i, 0))],
                 out_specs=pl.BlockSpec((tr, 2 * ch), lambda i: (i, 0)),
                 out_shape=_sds((t, 2 * ch), BF16), sem=("parallel",))(z, da0)


def _lnsilu_fwd(a1, g, b, *, name):
    t, ch = a1.shape
    tr = _pick(t, (256, 128, 64, 32, 16, 8))

    def body(a_ref, g_ref, b_ref, o_ref):
        n, _ = _ln_stats(a_ref[...])
        a2 = n * g_ref[...] + b_ref[...]
        o_ref[...] = (a2 * _sigmoid(a2)).astype(o_ref.dtype)

    row = pl.BlockSpec((tr, ch), lambda i: (i, 0))
    vec = pl.BlockSpec((1, ch), lambda i: (0, 0))
    return _call(body, name=name, grid=(t // tr,), in_specs=[row, vec, vec], out_specs=row,
                 out_shape=_sds((t, ch), BF16), sem=("parallel",))(a1, g, b)


def _lnsilu_bwd(a1, da3, g, b, *, name):
    t, ch = a1.shape
    tr = _pick(t, (256, 128, 64, 32, 16, 8))

    def body(a_ref, d_ref, g_ref, b_ref, o_ref, dg_ref, db_ref):
        @pl.when(pl.program_id(0) == 0)
        def _():
            dg_ref[...] = jnp.zeros_like(dg_ref)
            db_ref[...] = jnp.zeros_like(db_ref)

        n, rstd = _ln_stats(a_ref[...])
        a2 = n * g_ref[...] + b_ref[...]
        s = _sigmoid(a2)
        da2 = d_ref[...] * (s * (1.0 + a2 * (1.0 - s)))
        dg_ref[...] += jnp.sum(da2 * n, axis=0, keepdims=True)
        db_ref[...] += jnp.sum(da2, axis=0, keepdims=True)
        o_ref[...] = _ln_bwd(da2 * g_ref[...], n, rstd)

    row = pl.BlockSpec((tr, ch), lambda i: (i, 0))
    vec = pl.BlockSpec((1, ch), lambda i: (0, 0))
    return _call(body, name=name, grid=(t // tr,), in_specs=[row, row, vec, vec],
                 out_specs=[row, vec, vec],
                 out_shape=[_sds((t, ch), F32), _sds((1, ch), F32), _sds((1, ch), F32)],
                 sem=("arbitrary",))(a1, da3, g, b)


def _gate_cols(d, ga_off):
    tc = _pick(math.gcd(d, ga_off), (512, 256, 128))
    return tc, ga_off // tc, (ga_off + d) // tc


def _gate_merge_fwd(z, ya, yb, ga_off, *, name):
    t, d = ya.shape
    tr = _pick(t, (256, 128, 64, 32, 16, 8))
    tc, ga_blk, gb_blk = _gate_cols(d, ga_off)

    def body(ga_ref, gb_ref, ya_ref, yb_ref, o_ref):
        o_ref[...] = (_sigmoid(ga_ref[...]) * ya_ref[...] + _sigmoid(gb_ref[...]) * yb_ref[...]).astype(o_ref.dtype)

    blk = pl.BlockSpec((tr, tc), lambda i, j: (i, j))
    return _call(body, name=name, grid=(t // tr, d // tc),
                 in_specs=[pl.BlockSpec((tr, tc), lambda i, j: (i, ga_blk + j)),
                           pl.BlockSpec((tr, tc), lambda i, j: (i, gb_blk + j)), blk, blk],
                 out_specs=blk, out_shape=_sds((t, d), BF16), sem=("parallel", "parallel"))(z, z, ya, yb)


def _gate_merge_bwd(z, ya, yb, dm, ga_off, *, name):
    t, d = ya.shape
    tr = _pick(t, (256, 128, 64, 32, 16, 8))
    tc, ga_blk, gb_blk = _gate_cols(d, ga_off)

    def body(ga_ref, gb_ref, ya_ref, yb_ref, dm_ref, dya_ref, dyb_ref, dga_ref, dgb_ref):
        dm_v = dm_ref[...]
        sa = _sigmoid(ga_ref[...])
        sb = _sigmoid(gb_ref[...])
        dya_ref[...] = (dm_v * sa).astype(dya_ref.dtype)
        dyb_ref[...] = (dm_v * sb).astype(dyb_ref.dtype)
        dga_ref[...] = (dm_v * ya_ref[...] * sa * (1.0 - sa)).astype(dga_ref.dtype)
        dgb_ref[...] = (dm_v * yb_ref[...] * sb * (1.0 - sb)).astype(dgb_ref.dtype)

    blk = pl.BlockSpec((tr, tc), lambda i, j: (i, j))
    return _call(body, name=name, grid=(t // tr, d // tc),
                 in_specs=[pl.BlockSpec((tr, tc), lambda i, j: (i, ga_blk + j)),
                           pl.BlockSpec((tr, tc), lambda i, j: (i, gb_blk + j)), blk, blk, blk],
                 out_specs=[blk, blk, blk, blk], out_shape=[_sds((t, d), BF16)] * 4,
                 sem=("parallel", "parallel"))(z, z, ya, yb, dm)


def _gelu(v):
    return 0.5 * v * (1.0 + lax.erf(v * (2.0 ** -0.5)))


def _gelugate_fwd(h, tcf, *, name):
    t, two_f = h.shape
    tr = _pick(t, (256, 128, 64, 32, 16, 8))

    def body(h_ref, o_ref):
        o_ref[...] = (_gelu(h_ref[:, :tcf]) * h_ref[:, tcf:]).astype(o_ref.dtype)

    return _call(body, name=name, grid=(t // tr, two_f // (2 * tcf)),
                 in_specs=[pl.BlockSpec((tr, 2 * tcf), lambda i, j: (i, j))],
                 out_specs=pl.BlockSpec((tr, tcf), lambda i, j: (i, j)),
                 out_shape=_sds((t, two_f // 2), BF16), sem=("parallel", "parallel"))(h)


def _gelugate_bwd(h, df, tcf, *, name):
    t, two_f = h.shape
    tr = _pick(t, (256, 128, 64, 32, 16, 8))

    def body(h_ref, d_ref, o_ref):
        hg = h_ref[:, :tcf]
        d = d_ref[...]
        cdf = 0.5 * (1.0 + lax.erf(hg * (2.0 ** -0.5)))
        pdf = jnp.exp(-0.5 * hg * hg) * (1.0 / math.sqrt(2.0 * math.pi))
        o_ref[:, :tcf] = d * h_ref[:, tcf:] * (cdf + hg * pdf)
        o_ref[:, tcf:] = d * (hg * cdf)

    return _call(body, name=name, grid=(t // tr, two_f // (2 * tcf)),
                 in_specs=[pl.BlockSpec((tr, 2 * tcf), lambda i, j: (i, j)),
                           pl.BlockSpec((tr, tcf), lambda i, j: (i, j))],
                 out_specs=pl.BlockSpec((tr, 2 * tcf), lambda i, j: (i, j)),
                 out_shape=_sds((t, two_f), F32), sem=("parallel", "parallel"))(h, df)


def _dwconv_fwd(x, w, b, bsz, halo, *, name):
    t, ch = x.shape
    kw = w.shape[0]
    _, ts, ns = _seq_tiles(t, bsz, (256, 128, 64, 32))
    tc = _pick(ch, (512, 256, 128))
    hb = ts // halo

    def body(x_ref, h_ref, w_ref, b_ref, y_ref, ext_ref):
        ext_ref[pl.ds(0, halo), :] = jnp.where(pl.program_id(1) > 0, h_ref[...], 0.0)
        ext_ref[pl.ds(halo, ts), :] = x_ref[...]
        acc = jnp.zeros((ts, tc), F32) + b_ref[...]
        for k in range(kw):
            acc = acc + w_ref[pl.ds(k, 1), :] * ext_ref[pl.ds(halo - (kw - 1) + k, ts), :]
        y_ref[...] = acc

    cur = pl.BlockSpec((ts, tc), lambda bb, i, j: (bb * ns + i, j))
    prev = pl.BlockSpec((halo, tc), lambda bb, i, j: (jnp.maximum((bb * ns + i) * hb - 1, 0), j))
    return _call(body, name=name, grid=(bsz, ns, ch // tc),
                 in_specs=[cur, prev, pl.BlockSpec((kw, tc), lambda bb, i, j: (0, j)),
                           pl.BlockSpec((1, tc), lambda bb, i, j: (0, j))],
                 out_specs=cur, out_shape=_sds((t, ch), F32), scratch=[pltpu.VMEM((halo + ts, tc), F32)],
                 sem=("parallel", "parallel", "parallel"))(x, x, w, b)


def _dwconv_bwd(x, dy, w, bsz, halo, dx_dtype, *, name):
    t, ch = x.shape
    kw = w.shape[0]
    _, ts, ns = _seq_tiles(t, bsz, (256, 128, 64, 32))
    tc = _pick(ch, (512, 256, 128))
    hb = ts // halo
    last_halo_blk = t // halo - 1

    def body(x_ref, xh_ref, dy_ref, dyh_ref, w_ref, dx_ref, dw_ref, db_ref, extx_ref, exty_ref):
        i = pl.program_id(2)

        @pl.when(jnp.logical_and(pl.program_id(1) == 0, i == 0))
        def _():
            dw_ref[...] = jnp.zeros_like(dw_ref)
            db_ref[...] = jnp.zeros_like(db_ref)

        extx_ref[pl.ds(0, halo), :] = jnp.where(i > 0, xh_ref[...], 0.0)
        extx_ref[pl.ds(halo, ts), :] = x_ref[...]
        dyc = dy_ref[...]
        exty_ref[pl.ds(0, ts), :] = dyc
        exty_ref[pl.ds(ts, halo), :] = jnp.where(i < ns - 1, dyh_ref[...], 0.0)
        acc = jnp.zeros((ts, tc), F32)
        for k in range(kw):
            acc = acc + w_ref[pl.ds(k, 1), :] * exty_ref[pl.ds(kw - 1 - k, ts), :]
            dw_ref[pl.ds(k, 1), :] += jnp.sum(dyc * extx_ref[pl.ds(halo - (kw - 1) + k, ts), :],
                                              axis=0, keepdims=True)
        dx_ref[...] = acc.astype(dx_ref.dtype)
        db_ref[...] += jnp.sum(dyc, axis=0, keepdims=True)

    cur = pl.BlockSpec((ts, tc), lambda j, bb, i: (bb * ns + i, j))
    prev = pl.BlockSpec((halo, tc), lambda j, bb, i: (jnp.maximum((bb * ns + i) * hb - 1, 0), j))
    nxt = pl.BlockSpec((halo, tc), lambda j, bb, i: (jnp.minimum((bb * ns + i + 1) * hb, last_halo_blk), j))
    return _call(body, name=name, grid=(ch // tc, bsz, ns),
                 in_specs=[cur, prev, cur, nxt, pl.BlockSpec((kw, tc), lambda j, bb, i: (0, j))],
                 out_specs=[cur, pl.BlockSpec((kw, tc), lambda j, bb, i: (0, j)),
                            pl.BlockSpec((1, tc), lambda j, bb, i: (0, j))],
                 out_shape=[_sds((t, ch), dx_dtype), _sds((kw, ch), F32), _sds((1, ch), F32)],
                 scratch=[pltpu.VMEM((halo + ts, tc), F32), pltpu.VMEM((ts + halo, tc), F32)],
                 sem=("parallel", "arbitrary", "arbitrary"))(x, x, dy, dy, w)


def _split3(v):
    hi = v.astype(BF16)
    r = v - hi.astype(F32)
    mid = r.astype(BF16)
    lo = (r - mid.astype(F32)).astype(BF16)
    return hi, mid, lo


def _tri_dot(tri, v):
    out = None
    for part in _split3(v):
        term = jnp.dot(tri, part, preferred_element_type=F32)
        out = term if out is None else out + term
    return out


def _fgate_fwd(zf, bsz, heads, *, name):
    t, lanes = zf.shape
    s, blk, nb = _seq_tiles(t, bsz, (256, 128))

    def body(z_ref, cum_ref, cumt_ref, carry_ref):
        @pl.when(pl.program_id(1) == 0)
        def _():
            carry_ref[...] = jnp.zeros_like(carry_ref)

        z = z_ref[...]
        lf = jnp.minimum(z, 0.0) - jnp.log1p(jnp.exp(-jnp.abs(z)))
        r = lax.broadcasted_iota(jnp.int32, (blk, blk), 0)
        c = lax.broadcasted_iota(jnp.int32, (blk, blk), 1)
        tri = (r >= c).astype(BF16)
        cum = _tri_dot(tri, lf) + carry_ref[...]
        cum_ref[...] = cum
        carry_ref[...] = cum[blk - 1:blk, :]
        cumt_ref[0] = jnp.transpose(cum)[:heads, :]

    return _call(body, name=name, grid=(bsz, nb),
                 in_specs=[pl.BlockSpec((blk, lanes), lambda b, i: (b * nb + i, 0))],
                 out_specs=[pl.BlockSpec((blk, lanes), lambda b, i: (b * nb + i, 0)),
                            pl.BlockSpec((1, heads, blk), lambda b, i: (b, 0, i))],
                 out_shape=[_sds((t, lanes), F32), _sds((bsz, heads, s), F32)],
                 scratch=[pltpu.VMEM((1, lanes), F32)], sem=("parallel", "arbitrary"))(zf)


def _fgate_bwd(dcumt, zf, bsz, heads, *, name):
    t, lanes = zf.shape
    s, blk, nb = _seq_tiles(t, bsz, (256, 128))

    def body(d_ref, z_ref, o_ref, carry_ref):
        @pl.when(pl.program_id(1) == 0)
        def _():
            carry_ref[...] = jnp.zeros_like(carry_ref)

        d = jnp.concatenate([d_ref[0], jnp.zeros((lanes - heads, blk), F32)], axis=0)
        dcol = jnp.transpose(d)
        r = lax.broadcasted_iota(jnp.int32, (blk, blk), 0)
        c = lax.broadcasted_iota(jnp.int32, (blk, blk), 1)
        tri = (c >= r).astype(BF16)
        suf = _tri_dot(tri, dcol) + carry_ref[...]
        carry_ref[...] = suf[0:1, :]
        o_ref[...] = suf * _sigmoid(-z_ref[...])

    return _call(body, name=name, grid=(bsz, nb),
                 in_specs=[pl.BlockSpec((1, heads, blk), lambda b, i: (b, 0, nb - 1 - i)),
                           pl.BlockSpec((blk, lanes), lambda b, i: (b * nb + nb - 1 - i, 0))],
                 out_specs=pl.BlockSpec((blk, lanes), lambda b, i: (b * nb + nb - 1 - i, 0)),
                 out_shape=_sds((t, lanes), F32), scratch=[pltpu.VMEM((1, lanes), F32)],
                 sem=("parallel", "arbitrary"))(dcumt, zf)


def _attn_fwd(z, cum, cumt, bsz, heads, q_off, *, name):
    t = z.shape[0]
    width = heads * HEAD_DIM
    s, bq, nq = _seq_tiles(t, bsz, (256, 128))
    qb = q_off // width
    lanes = cum.shape[1]

    def body(q_ref, k_ref, v_ref, cc_ref, cr_ref, o_ref, lse_ref):
        i = pl.program_id(1)
        row = i * bq + lax.broadcasted_iota(jnp.int32, (bq, bq), 0)
        col = lax.broadcasted_iota(jnp.int32, (bq, bq), 1)
        lse_ref[...] = jnp.zeros_like(lse_ref)
        for h in range(heads):
            sl = slice(h * HEAD_DIM, (h + 1) * HEAD_DIM)
            qh = q_ref[:, sl].astype(BF16)
            cq = cc_ref[:, h:h + 1]

            def step(j, carry, sl=sl, qh=qh, cq=cq, h=h):
                m, l, acc = carry
                off = pl.multiple_of(j * bq, bq)
                kj = k_ref[pl.ds(off, bq), sl].astype(BF16)
                vj = v_ref[pl.ds(off, bq), sl].astype(BF16)
                sc = lax.dot_general(qh, kj, (((1,), (1,)), ((), ())), preferred_element_type=F32) * ATTN_SCALE
                sc = (sc + cq) - cr_ref[0, h:h + 1, pl.ds(off, bq)]
                sc = jnp.where(row >= col + off, sc, NEG)
                m_new = jnp.maximum(m, jnp.max(sc, axis=-1, keepdims=True))
                p = jnp.exp(sc - m_new)
                a = jnp.exp(m - m_new)
                l = a * l + jnp.sum(p, axis=-1, keepdims=True)
                acc = a * acc + jnp.dot(p.astype(BF16), vj, preferred_element_type=F32)
                return m_new, l, acc

            init = (jnp.full((bq, 1), NEG, F32), jnp.zeros((bq, 1), F32), jnp.zeros((bq, HEAD_DIM), F32))
            m, l, acc = lax.fori_loop(0, i + 1, step, init)
            o_ref[:, sl] = acc / l
            lse_ref[:, h:h + 1] = m + jnp.log(l)

    return _call(body, name=name, grid=(bsz, nq),
                 in_specs=[pl.BlockSpec((bq, width), lambda b, i: (b * nq + i, qb)),
                           pl.BlockSpec((s, width), lambda b, i: (b, qb + 1)),
                           pl.BlockSpec((s, width), lambda b, i: (b, qb + 2)),
                           pl.BlockSpec((bq, lanes), lambda b, i: (b * nq + i, 0)),
                           pl.BlockSpec((1, heads, s), lambda b, i: (b, 0, 0))],
                 out_specs=[pl.BlockSpec((bq, width), lambda b, i: (b * nq + i, 0)),
                            pl.BlockSpec((bq, lanes), lambda b, i: (b * nq + i, 0))],
                 out_shape=[_sds((t, width), F32), _sds((t, lanes), F32)],
                 sem=("parallel", "parallel"))(z, z, z, cum, cumt)


def _attn_bwd(z, cum, cumt, o, do, lse, bsz, heads, q_off, *, name):
    t = z.shape[0]
    width = heads * HEAD_DIM
    s, bq, nq = _seq_tiles(t, bsz, (256, 128))
    qb = q_off // width
    lanes = cum.shape[1]
    tn_dims = (((0,), (0,)), ((), ()))
    nt_dims = (((1,), (1,)), ((), ()))

    def body(q_ref, k_ref, v_ref, cc_ref, cr_ref, o_ref, do_ref, lse_ref,
             dq_ref, dk_ref, dv_ref, dcr_ref, dk_acc, dv_acc):
        i = pl.program_id(1)

        @pl.when(i == 0)
        def _():
            dk_acc[...] = jnp.zeros_like(dk_acc)
            dv_acc[...] = jnp.zeros_like(dv_acc)
            dcr_ref[...] = jnp.zeros_like(dcr_ref)

        row = i * bq + lax.broadcasted_iota(jnp.int32, (bq, bq), 0)
        col = lax.broadcasted_iota(jnp.int32, (bq, bq), 1)
        for h in range(heads):
            sl = slice(h * HEAD_DIM, (h + 1) * HEAD_DIM)
            qh = q_ref[:, sl].astype(BF16)
            do_f = do_ref[:, sl]
            doh = do_f.astype(BF16)
            delta = jnp.sum(do_f * o_ref[:, sl], axis=-1, keepdims=True)
            cq = cc_ref[:, h:h + 1]
            lse_h = lse_ref[:, h:h + 1]

            def step(j, dq, sl=sl, qh=qh, doh=doh, delta=delta, cq=cq, lse_h=lse_h, h=h):
                off = pl.multiple_of(j * bq, bq)
                kj = k_ref[pl.ds(off, bq), sl].astype(BF16)
                vj = v_ref[pl.ds(off, bq), sl].astype(BF16)
                sc = lax.dot_general(qh, kj, nt_dims, preferred_element_type=F32) * ATTN_SCALE
                sc = (sc + cq) - cr_ref[0, h:h + 1, pl.ds(off, bq)]
                p = jnp.where(row >= col + off, jnp.exp(sc - lse_h), 0.0)
                dp = lax.dot_general(doh, vj, nt_dims, preferred_element_type=F32)
                ds = p * (dp - delta)
                dsb = ds.astype(BF16)
                dv_acc[pl.ds(off, bq), sl] += lax.dot_general(p.astype(BF16), doh, tn_dims,
                                                               preferred_element_type=F32)
                dk_acc[pl.ds(off, bq), sl] += lax.dot_general(dsb, qh, tn_dims,
                                                               preferred_element_type=F32) * ATTN_SCALE
                dcr_ref[0, h:h + 1, pl.ds(off, bq)] -= jnp.sum(ds, axis=0, keepdims=True)
                return dq + jnp.dot(dsb, kj, preferred_element_type=F32) * ATTN_SCALE

            dq = lax.fori_loop(0, i + 1, step, jnp.zeros((bq, HEAD_DIM), F32))
            dq_ref[:, sl] = dq.astype(dq_ref.dtype)

        @pl.when(i == nq - 1)
        def _():
            dk_ref[...] = dk_acc[...].astype(dk_ref.dtype)
            dv_ref[...] = dv_acc[...].astype(dv_ref.dtype)

    qrow = pl.BlockSpec((bq, width), lambda b, i: (b * nq + i, 0))
    whole = pl.BlockSpec((s, width), lambda b, i: (b, 0))
    return _call(body, name=name, grid=(bsz, nq),
                 in_specs=[pl.BlockSpec((bq, width), lambda b, i: (b * nq + i, qb)),
                           pl.BlockSpec((s, width), lambda b, i: (b, qb + 1)),
                           pl.BlockSpec((s, width), lambda b, i: (b, qb + 2)),
                           pl.BlockSpec((bq, lanes), lambda b, i: (b * nq + i, 0)),
                           pl.BlockSpec((1, heads, s), lambda b, i: (b, 0, 0)),
                           qrow, qrow, pl.BlockSpec((bq, lanes), lambda b, i: (b * nq + i, 0))],
                 out_specs=[qrow, whole, whole, pl.BlockSpec((1, heads, s), lambda b, i: (b, 0, 0))],
                 out_shape=[_sds((t, width), BF16), _sds((t, width), BF16), _sds((t, width), BF16),
                            _sds((bsz, heads, s), F32)],
                 scratch=[pltpu.VMEM((s, width), F32), pltpu.VMEM((s, width), F32)],
                 sem=("parallel", "arbitrary"))(z, z, z, cum, cumt, o, do, lse)


def _adamw(w, g, m, v, *, name):
    shape = w.shape
    cols = shape[-1]
    rows = w.size // cols
    tr = rows if rows <= 256 else _pick(rows, (256, 128, 64, 32, 16, 8))
    bc1 = 1.0 - ADAM_B1 ** ADAM_STEP
    bc2 = 1.0 - ADAM_B2 ** ADAM_STEP

    def body(w_ref, g_ref, m_ref, v_ref, d_ref, nm_ref, nv_ref):
        g_v = g_ref[...]
        nm = ADAM_B1 * m_ref[...] + (1.0 - ADAM_B1) * g_v
        nv = ADAM_B2 * v_ref[...] + (1.0 - ADAM_B2) * (g_v * g_v)
        nm_ref[...] = nm
        nv_ref[...] = nv
        d_ref[...] = -ADAM_LR * ((nm / bc1) / (jnp.sqrt(nv / bc2) + ADAM_EPS) + ADAM_WD * w_ref[...])

    blk = pl.BlockSpec((tr, cols), lambda i: (i, 0))
    flat = [a.reshape(rows, cols) for a in (w, g, m, v)]
    outs = _call(body, name=name, grid=(rows // tr,), in_specs=[blk] * 4, out_specs=[blk] * 3,
                 out_shape=[_sds((rows, cols), F32)] * 3, sem=("parallel",))(*flat)
    return tuple(a.reshape(shape) for a in outs)


_ANY = pl.BlockSpec(memory_space=pl.ANY)


def _comm_call(body, *, name, out_shape, n_sems):
    scratch = [pltpu.SemaphoreType.DMA((n_sems,)), pltpu.SemaphoreType.DMA((n_sems,)), pltpu.SemaphoreType.DMA]
    return pl.pallas_call(body, name=name, in_specs=[_ANY], out_specs=_ANY, out_shape=out_shape,
                          scratch_shapes=scratch)


def _place():
    x, y, c = lax.axis_index("x"), lax.axis_index("y"), lax.axis_index("c")
    return x, y, c, [(1 - x, y), (x, 1 - y), (1 - x, 1 - y)]


def _all_gather_chips(wp, *, name):
    rows, cols = wp.shape
    half = rows // 2

    def body(w_ref, out_ref, send_sems, recv_sems, local_sem):
        x, y, c, chips = _place()
        me = 2 * x + y

        def part(chip, which):
            return out_ref.at[chip, pl.ds(which * half, half), :]

        def copy(sem, chip, which, to, src=None):
            return pltpu.make_async_remote_copy(
                src_ref=part(chip, which) if src is None else src, dst_ref=part(chip, which),
                send_sem=send_sems.at[sem], recv_sem=recv_sems.at[sem], device_id=to, device_id_type=MESH)

        mine = pltpu.make_async_copy(w_ref, out_ref.at[me], local_sem)
        mine.start()
        first = [copy(r, me, c, (cx, cy, c), src=w_ref.at[pl.ds(c * half, half), :])
                 for r, (cx, cy) in enumerate(chips)]
        for cp in first:
            cp.start()
        passed = [copy(3 + r, 2 * cx + cy, c, (x, y, 1 - c)) for r, (cx, cy) in enumerate(chips)]
        for r, (cx, cy) in enumerate(chips):
            copy(r, 2 * cx + cy, c, (cx, cy, c)).wait_recv()
            passed[r].start()
        for r, (cx, cy) in enumerate(chips):
            copy(3 + r, 2 * cx + cy, 1 - c, (x, y, 1 - c)).wait_recv()
        for cp in first + passed:
            cp.wait_send()
        mine.wait()

    return _comm_call(body, name=name, out_shape=_sds((N_CHIPS, rows, cols), wp.dtype), n_sems=6)(wp)


def _sibling_swap(g2, *, name):
    def body(g_ref, out_ref, send_sems, recv_sems, local_sem):
        x, y, c, _ = _place()
        cp = pltpu.make_async_remote_copy(src_ref=g_ref.at[1 - c], dst_ref=out_ref, send_sem=send_sems.at[0],
                                          recv_sem=recv_sems.at[0], device_id=(x, y, 1 - c), device_id_type=MESH)
        cp.start()
        cp.wait()

    return _comm_call(body, name=name, out_shape=_sds(g2.shape[1:], g2.dtype), n_sems=1)(g2)


def _chip_scatter(part, *, name):
    def body(p_ref, out_ref, send_sems, recv_sems, local_sem):
        x, y, c, chips = _place()
        me = 2 * x + y
        mine = pltpu.make_async_copy(p_ref.at[me], out_ref.at[me], local_sem)
        mine.start()
        sends = [pltpu.make_async_remote_copy(
            src_ref=p_ref.at[2 * cx + cy], dst_ref=out_ref.at[me], send_sem=send_sems.at[r],
            recv_sem=recv_sems.at[r], device_id=(cx, cy, c), device_id_type=MESH) for r, (cx, cy) in enumerate(chips)]
        for cp in sends:
            cp.start()
        for r, (cx, cy) in enumerate(chips):
            k = 2 * cx + cy
            pltpu.make_async_remote_copy(src_ref=p_ref.at[k], dst_ref=out_ref.at[k], send_sem=send_sems.at[r],
                                         recv_sem=recv_sems.at[r], device_id=(cx, cy, c),
                                         device_id_type=MESH).wait_recv()
        for cp in sends:
            cp.wait_send()
        mine.wait()

    return _comm_call(body, name=name, out_shape=_sds(part.shape, part.dtype), n_sems=3)(part)


def _sibling_gather(fh, *, name):
    def body(f_ref, out_ref, send_sems, recv_sems, local_sem):
        x, y, c, _ = _place()
        mine = pltpu.make_async_copy(f_ref, out_ref.at[c], local_sem)
        mine.start()
        cp = pltpu.make_async_remote_copy(src_ref=f_ref, dst_ref=out_ref.at[c], send_sem=send_sems.at[0],
                                          recv_sem=recv_sems.at[0], device_id=(x, y, 1 - c), device_id_type=MESH)
        cp.start()
        pltpu.make_async_remote_copy(src_ref=f_ref, dst_ref=out_ref.at[1 - c], send_sem=send_sems.at[0],
                                     recv_sem=recv_sems.at[0], device_id=(x, y, 1 - c),
                                     device_id_type=MESH).wait_recv()
        cp.wait_send()
        mine.wait()

    return _comm_call(body, name=name, out_shape=_sds((2,) + fh.shape, fh.dtype), n_sems=1)(fh)


def _add_own_half(g2, other, core, *, name):
    _, n, hh, cols = g2.shape
    tr = _pick(hh, (256, 128, 64, 32, 16, 8))

    grid_spec = pltpu.PrefetchScalarGridSpec(
        num_scalar_prefetch=1, grid=(n, hh // tr),
        in_specs=[pl.BlockSpec((1, 1, tr, cols), lambda k, i, core_ref: (core_ref[0], k, i, 0)),
                  pl.BlockSpec((1, tr, cols), lambda k, i, core_ref: (k, i, 0))],
        out_specs=pl.BlockSpec((1, tr, cols), lambda k, i, core_ref: (k, i, 0)))

    def body(core_ref, g_ref, o_ref, out_ref):
        out_ref[...] = g_ref[0] + o_ref[...]

    return pl.pallas_call(body, name=name, grid_spec=grid_spec, out_shape=_sds((n, hh, cols), g2.dtype),
                          compiler_params=pltpu.CompilerParams(dimension_semantics=("parallel", "parallel"),
                                                               vmem_limit_bytes=VMEM_LIMIT))(core, g2, other)


def _sum_chips(parts, *, name):
    n, hh, cols = parts.shape
    tr = _pick(hh, (256, 128, 64, 32, 16, 8))

    def body(p_ref, o_ref):
        acc = p_ref[0] + p_ref[1]
        for k in range(2, n):
            acc = acc + p_ref[k]
        o_ref[...] = acc

    return _call(body, name=name, grid=(hh // tr,),
                 in_specs=[pl.BlockSpec((n, tr, cols), lambda i: (0, i, 0))],
                 out_specs=pl.BlockSpec((tr, cols), lambda i: (i, 0)),
                 out_shape=_sds((hh, cols), parts.dtype), sem=("parallel",))(parts)


def _pad_rows(flat, unit):
    n = flat.shape[-1]
    total = -(-n // (unit * PACK_COLS)) * unit * PACK_COLS
    pad = [(0, 0)] * (flat.ndim - 1) + [(0, total - n)]
    return jnp.pad(flat, pad).reshape(flat.shape[:-1] + (total // PACK_COLS, PACK_COLS))


def _pack_weights(shards):
    parts = []
    for wname, _, as_bf16 in SHARDED:
        w = shards[wname]
        if as_bf16:
            parts.append(w.astype(BF16).reshape(-1))
        else:
            parts.append(lax.bitcast_convert_type(w, BF16).reshape(-1))
    return _pad_rows(jnp.concatenate(parts), 32)


def _unpack_weights(full, shards):
    flat = full.reshape(N_CHIPS, -1)
    out, off = {}, 0
    for wname, axis, as_bf16 in SHARDED:
        shp = shards[wname].shape
        n = math.prod(shp) * (1 if as_bf16 else 2)
        seg = flat[:, off:off + n]
        off += n
        if as_bf16:
            seg = seg.reshape((N_CHIPS,) + shp)
        else:
            seg = lax.bitcast_convert_type(seg.reshape((N_CHIPS,) + shp + (2,)), F32)
        out[wname] = _merge_chips(seg, axis)
    return out


def _merge_chips(seg, axis):
    _, layers, a, b = seg.shape
    if axis == 0:
        return jnp.transpose(seg, (1, 0, 2, 3)).reshape(layers, N_CHIPS * a, b)
    return jnp.transpose(seg, (1, 2, 0, 3)).reshape(layers, a, N_CHIPS * b)


def _split_chips(g, axis):
    layers, a, b = g.shape
    if axis == 0:
        return g.reshape(layers, N_CHIPS, a // N_CHIPS, b).transpose(1, 0, 2, 3).reshape(N_CHIPS, -1)
    return g.reshape(layers, a, N_CHIPS, b // N_CHIPS).transpose(2, 0, 1, 3).reshape(N_CHIPS, -1)


def _reduce_grads(big, small, shards, small_shapes):
    parts = [_split_chips(big[wname], axis) for wname, axis, _ in SHARDED]
    small_flat = jnp.concatenate([small[wname].reshape(-1) for wname in REPLICATED])
    parts.append(jnp.broadcast_to(small_flat[None], (N_CHIPS, small_flat.shape[0])))
    packed = _pad_rows(jnp.concatenate(parts, axis=1), 16)
    rows = packed.shape[1]
    g2 = packed.reshape(N_CHIPS, 2, rows // 2, PACK_COLS).transpose(1, 0, 2, 3)
    core = lax.axis_index("c").astype(jnp.int32).reshape(1)
    from_sibling = _sibling_swap(g2, name="rs_sibling_swap")
    chip_part = _add_own_half(g2, from_sibling, core, name="rs_add_sibling")
    from_chips = _chip_scatter(chip_part, name="rs_chip_scatter")
    my_half = _sum_chips(from_chips, name="rs_sum_chips")
    flat = _sibling_gather(my_half, name="rs_sibling_gather").reshape(-1)
    out_big, out_small, off = {}, {}, 0
    for wname, _, _ in SHARDED:
        shp = shards[wname].shape
        n = math.prod(shp)
        out_big[wname] = flat[off:off + n].reshape(shp)
        off += n
    for wname in REPLICATED:
        shp = small_shapes[wname]
        n = math.prod(shp)
        out_small[wname] = flat[off:off + n].reshape(shp)
        off += n
    return out_big, out_small


def _interleave(w, tcf):
    lead = w.shape[:-1]
    f = w.shape[-1] // 2
    return jnp.swapaxes(w.reshape(lead + (2, f // tcf, tcf)), -3, -2).reshape(lead + (2 * f,))


def _deinterleave(w, tcf):
    lead = w.shape[:-1]
    f = w.shape[-1] // 2
    return jnp.swapaxes(w.reshape(lead + (f // tcf, 2, tcf)), -3, -2).reshape(lead + (2 * f,))


def _layer_weights(full, rep, layer, dims):
    f_off, n_heads, tcf = dims["f_off"], dims["heads"], dims["tcf"]
    w_in = full["w_in"][layer]
    b_in = rep["b_in"][layer]
    pad = LANES - n_heads
    p = {
        "w_ada": full["w_ada"][layer],
        "b_ada": rep["b_ada"][layer][None],
        "w_main": jnp.concatenate([w_in[:, :f_off], w_in[:, f_off + n_heads:]], axis=1),
        "b_main": jnp.concatenate([b_in[:f_off], b_in[f_off + n_heads:]])[None],
        "w_f": jnp.pad(w_in[:, f_off:f_off + n_heads], ((0, 0), (0, pad))),
        "b_f": jnp.pad(b_in[f_off:f_off + n_heads], (0, pad))[None],
        "conv_a_w": full["conv_a_w"][layer],
        "conv_a_b": rep["conv_a_b"][layer][None],
        "ln_conv_g": rep["ln_conv_g"][layer][None],
        "ln_conv_b": rep["ln_conv_b"][layer][None],
        "w_conv_proj": full["w_conv_proj"][layer],
        "w_attn_proj": full["w_attn_proj"][layer],
        "w_mix_out": full["w_mix_out"][layer],
        "b_mix_out": rep["b_mix_out"][layer][None],
        "ln1_g": rep["ln1_g"][layer][None],
        "ln1_b": rep["ln1_b"][layer][None],
        "w_ffn_up": _interleave(full["w_ffn_up"][layer], tcf),
        "ffn_conv_w": _interleave(full["ffn_conv_w"][layer], tcf),
        "ffn_conv_b": _interleave(rep["ffn_conv_b"][layer], tcf)[None],
        "w_ffn_down": full["w_ffn_down"][layer],
        "ln2_g": rep["ln2_g"][layer][None],
        "ln2_b": rep["ln2_b"][layer][None],
    }
    return p


def _layer_fwd(x, c_act, p, dims, tag):
    bsz, d, ch, heads, alpha = dims["bsz"], dims["d"], dims["ch"], dims["heads"], dims["alpha"]
    mod = _matmul(c_act, p["w_ada"], "nn", F32, bias=p["b_ada"], name=f"ada_{tag}")
    mods = [mod[:bsz, k * d:(k + 1) * d][:, None, :] for k in range(6)]
    shift1, scale1, gate1, shift2, scale2, gate2 = mods
    u = _ln_mod_fwd(x, scale1, shift1, bsz, name=f"ln_mod1_{tag}")
    zm = _matmul(u, p["w_main"], "nn", F32, bias=p["b_main"], name=f"in_main_{tag}")
    zf = _matmul(u, p["w_f"], "nn", F32, bias=p["b_f"], name=f"in_forget_{tag}")
    a0 = _glu_fwd(zm, ch, name=f"glu_{tag}")
    a1 = _dwconv_fwd(a0, p["conv_a_w"], p["conv_a_b"], bsz, CONV_A_HALO, name=f"conv_a_{tag}")
    a3 = _lnsilu_fwd(a1, p["ln_conv_g"], p["ln_conv_b"], name=f"lnsilu_{tag}")
    ya = _matmul(a3, p["w_conv_proj"], "nn", F32, name=f"conv_proj_{tag}")
    cum, cumt = _fgate_fwd(zf, bsz, heads, name=f"fgate_{tag}")
    o, lse = _attn_fwd(zm, cum, cumt, bsz, heads, 2 * ch, name=f"attn_{tag}")
    yb = _matmul(o, p["w_attn_proj"], "nn", F32, name=f"attn_proj_{tag}")
    m = _gate_merge_fwd(zm, ya, yb, dims["ga_off"], name=f"merge_{tag}")
    mix = _matmul(m, p["w_mix_out"], "nn", F32, bias=p["b_mix_out"], name=f"mix_out_{tag}")
    x1 = _ln_res_fwd(x, mix, gate1, p["ln1_g"], p["ln1_b"], alpha, bsz, name=f"ln_res1_{tag}")
    u2 = _ln_mod_fwd(x1, scale2, shift2, bsz, name=f"ln_mod2_{tag}")
    hp = _matmul(u2, p["w_ffn_up"], "nn", F32, name=f"ffn_up_{tag}")
    hh = _dwconv_fwd(hp, p["ffn_conv_w"], p["ffn_conv_b"], bsz, FFN_CONV_HALO, name=f"ffn_conv_{tag}")
    f = _gelugate_fwd(hh, dims["tcf"], name=f"gelugate_{tag}")
    ffn = _matmul(f, p["w_ffn_down"], "nn", F32, name=f"ffn_down_{tag}")
    x2 = _ln_res_fwd(x1, ffn, gate2, p["ln2_g"], p["ln2_b"], alpha, bsz, name=f"ln_res2_{tag}")
    saved = dict(x=x, mods=mods, u=u, zm=zm, zf=zf, a0=a0, a1=a1, a3=a3, ya=ya, yb=yb, cum=cum, cumt=cumt,
                 o=o, lse=lse, m=m, mix=mix, x1=x1, u2=u2, hp=hp, hh=hh, f=f, ffn=ffn)
    return x2, saved


def _layer_bwd(dx2, c_act, p, sv, dims, tag):
    bsz, ch, heads, alpha = dims["bsz"], dims["ch"], dims["heads"], dims["alpha"]
    f_off, tcf = dims["f_off"], dims["tcf"]
    shift1, scale1, gate1, shift2, scale2, gate2 = sv["mods"]
    g = {}
    dr2, dffn, dgate2, g["ln2_g"], g["ln2_b"], _ = _ln_res_bwd(
        dx2, sv["x1"], sv["ffn"], gate2, p["ln2_g"], alpha, bsz, name=f"ln_res2_bwd_{tag}")
    df = _matmul(dffn, p["w_ffn_down"], "nt", F32, name=f"ffn_down_dx_{tag}")
    g["w_ffn_down"] = _matmul(sv["f"], dffn, "tn", F32, name=f"ffn_down_dw_{tag}")
    dh = _gelugate_bwd(sv["hh"], df, tcf, name=f"gelugate_bwd_{tag}")
    dhp, dfw, dfb = _dwconv_bwd(sv["hp"], dh, p["ffn_conv_w"], bsz, FFN_CONV_HALO, BF16, name=f"ffn_conv_bwd_{tag}")
    g["ffn_conv_w"] = _deinterleave(dfw, tcf)
    g["ffn_conv_b"] = _deinterleave(dfb, tcf)[0]
    du2 = _matmul(dhp, p["w_ffn_up"], "nt", F32, name=f"ffn_up_dx_{tag}")
    g["w_ffn_up"] = _deinterleave(_matmul(sv["u2"], dhp, "tn", F32, name=f"ffn_up_dw_{tag}"), tcf)
    dx1, dscale2, dshift2 = _ln_mod_bwd(du2, sv["x1"], scale2, dr2, alpha, bsz, name=f"ln_mod2_bwd_{tag}")
    dr1, dmix, dgate1, g["ln1_g"], g["ln1_b"], g["b_mix_out"] = _ln_res_bwd(
        dx1, sv["x"], sv["mix"], gate1, p["ln1_g"], alpha, bsz, name=f"ln_res1_bwd_{tag}")
    dm = _matmul(dmix, p["w_mix_out"], "nt", F32, name=f"mix_out_dx_{tag}")
    g["w_mix_out"] = _matmul(sv["m"], dmix, "tn", F32, name=f"mix_out_dw_{tag}")
    dya, dyb, dzga, dzgb = _gate_merge_bwd(sv["zm"], sv["ya"], sv["yb"], dm, dims["ga_off"], name=f"merge_bwd_{tag}")
    da3 = _matmul(dya, p["w_conv_proj"], "nt", F32, name=f"conv_proj_dx_{tag}")
    g["w_conv_proj"] = _matmul(sv["a3"], dya, "tn", F32, name=f"conv_proj_dw_{tag}")
    do = _matmul(dyb, p["w_attn_proj"], "nt", F32, name=f"attn_proj_dx_{tag}")
    g["w_attn_proj"] = _matmul(sv["o"], dyb, "tn", F32, name=f"attn_proj_dw_{tag}")
    da1, g["ln_conv_g"], g["ln_conv_b"] = _lnsilu_bwd(sv["a1"], da3, p["ln_conv_g"], p["ln_conv_b"],
                                                      name=f"lnsilu_bwd_{tag}")
    da0, g["conv_a_w"], dcb = _dwconv_bwd(sv["a0"], da1, p["conv_a_w"], bsz, CONV_A_HALO, F32, name=f"conv_a_bwd_{tag}")
    g["conv_a_b"] = dcb[0]
    dzglu = _glu_bwd(sv["zm"], da0, ch, name=f"glu_bwd_{tag}")
    dq, dk, dv, dcumt = _attn_bwd(sv["zm"], sv["cum"], sv["cumt"], sv["o"], do, sv["lse"], bsz, heads, 2 * ch,
                                  name=f"attn_bwd_{tag}")
    dzf = _fgate_bwd(dcumt, sv["zf"], bsz, heads, name=f"fgate_bwd_{tag}")
    dzm = jnp.concatenate([dzglu, dq, dk, dv, dzga, dzgb], axis=1)
    du = _matmul(dzm, p["w_main"], "nt", F32, name=f"in_main_dx_{tag}")
    du = _matmul(dzf, p["w_f"], "nt", F32, add=du, name=f"in_forget_dx_{tag}")
    dwm = _matmul(sv["u"], dzm, "tn", F32, name=f"in_main_dw_{tag}")
    dwf = _matmul(sv["u"], dzf, "tn", F32, name=f"in_forget_dw_{tag}")
    dbm = _colsum(dzm, name=f"in_main_db_{tag}")[0]
    dbf = _colsum(dzf, name=f"in_forget_db_{tag}")[0]
    g["w_in"] = jnp.concatenate([dwm[:, :f_off], dwf[:, :heads], dwm[:, f_off:]], axis=1)
    g["b_in"] = jnp.concatenate([dbm[:f_off], dbf[:heads], dbm[f_off:]])
    dx, dscale1, dshift1 = _ln_mod_bwd(du, sv["x"], scale1, dr1, alpha, bsz, name=f"ln_mod1_bwd_{tag}")
    dmod = jnp.concatenate([dshift1, dscale1, dgate1, dshift2, dscale2, dgate2], axis=2)[:, 0, :]
    dmod = jnp.pad(dmod, ((0, c_act.shape[0] - bsz), (0, 0)))
    g["w_ada"] = _matmul(c_act, dmod, "tn", F32, name=f"ada_dw_{tag}")
    g["b_ada"] = _colsum(dmod, name=f"ada_db_{tag}")[0]
    return dx, g


def _local_step(x, c, loss_target, full, rep, dims):
    bsz, seq, d = x.shape
    layers = rep["b_ada"].shape[0]
    params = [_layer_weights(full, rep, layer, dims) for layer in range(layers)]
    c_act = _silu_rows(c, 16, name="silu_c")
    h = x.reshape(bsz * seq, d)
    saved = []
    for layer in range(layers):
        h, sv = _layer_fwd(h, c_act, params[layer], dims, f"l{layer}")
        saved.append(sv)
    dh, sq = _loss_head(h, loss_target.reshape(bsz * seq, d), name="loss_head")
    loss_local = 0.5 * jnp.sum(sq) / d
    grads = [None] * layers
    for layer in reversed(range(layers)):
        dh, grads[layer] = _layer_bwd(dh, c_act, params[layer], saved[layer], dims, f"l{layer}")
    stacked = {wname: jnp.stack([grads[layer][wname] for layer in range(layers)]) for wname in WEIGHTS}
    return loss_local, dh.reshape(bsz, seq, d), stacked


def kernel(x, c, w_ada, b_ada, w_in, b_in, conv_a_w, conv_a_b, ln_conv_g, ln_conv_b, w_conv_proj, w_attn_proj, w_mix_out, b_mix_out, ln1_g, ln1_b, w_ffn_up, ffn_conv_w, ffn_conv_b, w_ffn_down, ln2_g, ln2_b, loss_target, m_w_ada, m_b_ada, m_w_in, m_b_in, m_conv_a_w, m_conv_a_b, m_ln_conv_g, m_ln_conv_b, m_w_conv_proj, m_w_attn_proj, m_w_mix_out, m_b_mix_out, m_ln1_g, m_ln1_b, m_w_ffn_up, m_ffn_conv_w, m_ffn_conv_b, m_w_ffn_down, m_ln2_g, m_ln2_b, v_w_ada, v_b_ada, v_w_in, v_b_in, v_conv_a_w, v_conv_a_b, v_ln_conv_g, v_ln_conv_b, v_w_conv_proj, v_w_attn_proj, v_w_mix_out, v_b_mix_out, v_ln1_g, v_ln1_b, v_w_ffn_up, v_ffn_conv_w, v_ffn_conv_b, v_w_ffn_down, v_ln2_g, v_ln2_b):
    weights = dict(zip(WEIGHTS, (w_ada, b_ada, w_in, b_in, conv_a_w, conv_a_b, ln_conv_g, ln_conv_b, w_conv_proj,
                                 w_attn_proj, w_mix_out, b_mix_out, ln1_g, ln1_b, w_ffn_up, ffn_conv_w, ffn_conv_b,
                                 w_ffn_down, ln2_g, ln2_b)))
    mom1 = dict(zip(WEIGHTS, (m_w_ada, m_b_ada, m_w_in, m_b_in, m_conv_a_w, m_conv_a_b, m_ln_conv_g, m_ln_conv_b,
                              m_w_conv_proj, m_w_attn_proj, m_w_mix_out, m_b_mix_out, m_ln1_g, m_ln1_b, m_w_ffn_up,
                              m_ffn_conv_w, m_ffn_conv_b, m_w_ffn_down, m_ln2_g, m_ln2_b)))
    mom2 = dict(zip(WEIGHTS, (v_w_ada, v_b_ada, v_w_in, v_b_in, v_conv_a_w, v_conv_a_b, v_ln_conv_g, v_ln_conv_b,
                              v_w_conv_proj, v_w_attn_proj, v_w_mix_out, v_b_mix_out, v_ln1_g, v_ln1_b, v_w_ffn_up,
                              v_ffn_conv_w, v_ffn_conv_b, v_w_ffn_down, v_ln2_g, v_ln2_b)))
    bsz, seq, d = x.shape
    layers = w_ada.shape[0]
    ch = conv_a_w.shape[2] * N_CHIPS
    width = w_attn_proj.shape[1]
    heads = width // HEAD_DIM
    d_ff = w_ffn_down.shape[1] * N_CHIPS
    dims = dict(bsz=bsz, d=d, ch=ch, heads=heads, alpha=(2.0 * layers) ** 0.25, f_off=2 * ch + 3 * width,
                ga_off=2 * ch + 3 * width, tcf=_pick(d_ff, (256, 128)))

    shards = {wname: weights[wname] for wname, _, _ in SHARDED}
    rep = {wname: weights[wname] for wname in REPLICATED}
    full = _unpack_weights(_all_gather_chips(_pack_weights(shards), name="gather_weights"), shards)
    loss_local, grad_x, stacked = _local_step(x, c, loss_target, full, rep, dims)
    loss = lax.psum(loss_local, ("x", "y", "c"))
    big, small = _reduce_grads({wname: stacked[wname] for wname, _, _ in SHARDED},
                               {wname: stacked[wname] for wname in REPLICATED}, shards,
                               {wname: weights[wname].shape for wname in REPLICATED})
    grad = {**big, **small}
    delta, new_m, new_v = {}, {}, {}
    for wname in WEIGHTS:
        delta[wname], new_m[wname], new_v[wname] = _adamw(weights[wname], grad[wname], mom1[wname], mom2[wname],
                                                          name=f"adamw_{wname}")
    return (loss, grad_x, *[grad[wname] for wname in WEIGHTS], *[delta[wname] for wname in WEIGHTS],
            *[new_m[wname] for wname in WEIGHTS], *[new_v[wname] for wname in WEIGHTS])
```

```python
import math

import jax
import jax.numpy as jnp
from jax import lax
from jax.experimental import pallas as pl
from jax.experimental.pallas import tpu as pltpu

F32 = jnp.float32
BF16 = jnp.bfloat16
MESH = pl.DeviceIdType.MESH

LN_EPS = 1e-5
HEAD_DIM = 64
ATTN_SCALE = HEAD_DIM ** -0.5
NEG = -1e30
CONV_A_HALO = 32
FFN_PAD = 8
LANES = 128
SUBLANES = 8
ROW_CHUNK = 256
N_CHIPS = 4
N_DEVICES = 8
VMEM_LIMIT = 56 * 1024 * 1024

ADAM_LR = 0.001
ADAM_B1 = 0.9
ADAM_B2 = 0.999
ADAM_EPS = 1e-08
ADAM_WD = 0.01
ADAM_STEP = 10

GATHERED = (("w_in", "slab", True), ("conv_a_w", "cols", False), ("w_conv_proj", "cols", True),
            ("w_attn_proj", "cols", True), ("w_mix_out", "rows", True), ("w_ffn_up", "cols", True),
            ("ffn_conv_w", "cols", False), ("w_ffn_down", "rows", True))
REPLICATED = ("b_in", "conv_a_b", "ln_conv_g", "ln_conv_b", "b_mix_out", "ln1_g", "ln1_b",
              "ffn_conv_b", "ln2_g", "ln2_b")
WEIGHTS = ("w_ada", "b_ada", "w_in", "b_in", "conv_a_w", "conv_a_b", "ln_conv_g", "ln_conv_b",
           "w_conv_proj", "w_attn_proj", "w_mix_out", "b_mix_out", "ln1_g", "ln1_b", "w_ffn_up",
           "ffn_conv_w", "ffn_conv_b", "w_ffn_down", "ln2_g", "ln2_b")


def _pick(n, cands):
    for cand in cands:
        if n % cand == 0:
            return cand
    return n


def _call(body, *, name, grid, in_specs, out_specs, out_shape, scratch=(), sem=None):
    return pl.pallas_call(
        body, name=name, grid=grid, in_specs=in_specs, out_specs=out_specs, out_shape=out_shape,
        scratch_shapes=list(scratch),
        compiler_params=pltpu.CompilerParams(dimension_semantics=sem, vmem_limit_bytes=VMEM_LIMIT))


def _sds(shape, dtype):
    return jax.ShapeDtypeStruct(tuple(shape), dtype)


def _chunked(rows, fn):
    chunk = min(ROW_CHUNK, rows)
    if rows == chunk:
        fn(pl.ds(0, rows))
        return

    def step(i, carry):
        fn(pl.ds(pl.multiple_of(i * chunk, chunk), chunk))
        return carry

    lax.fori_loop(0, rows // chunk, step, 0)


def _matmul(a, b, mode, out_dtype, *, bias=None, add=None, name):
    if mode == "nn":
        (m, k), (_, n) = a.shape, b.shape
    elif mode == "nt":
        (m, k), (n, _) = a.shape, b.shape
    else:
        (k, m), (_, n) = a.shape, b.shape
    tm = _pick(m, (1024, 512, 256, 128))
    tn = _pick(n, (512, 1408, 256, 128))
    tk = k if k <= 1536 else _pick(k, (1024, 1536, 1408, 512, 256, 128))
    nk = k // tk
    if mode == "nn":
        a_spec = pl.BlockSpec((tm, tk), lambda i, j, kk: (i, kk))
        b_spec = pl.BlockSpec((tk, tn), lambda i, j, kk: (kk, j))
        dims = (((1,), (0,)), ((), ()))
    elif mode == "nt":
        a_spec = pl.BlockSpec((tm, tk), lambda i, j, kk: (i, kk))
        b_spec = pl.BlockSpec((tn, tk), lambda i, j, kk: (j, kk))
        dims = (((1,), (1,)), ((), ()))
    else:
        a_spec = pl.BlockSpec((tk, tm), lambda i, j, kk: (kk, i))
        b_spec = pl.BlockSpec((tk, tn), lambda i, j, kk: (kk, j))
        dims = (((0,), (0,)), ((), ()))
    in_specs = [a_spec, b_spec]
    operands = [a, b]
    if bias is not None:
        in_specs.append(pl.BlockSpec((1, tn), lambda i, j, kk: (0, j)))
        operands.append(bias)
    if add is not None:
        in_specs.append(pl.BlockSpec((tm, tn), lambda i, j, kk: (i, j)))
        operands.append(add)

    def body(a_ref, b_ref, *rest):
        rest = list(rest)
        bias_ref = rest.pop(0) if bias is not None else None
        add_ref = rest.pop(0) if add is not None else None
        o_ref = rest.pop(0)
        prod = lax.dot_general(a_ref[...].astype(BF16), b_ref[...].astype(BF16), dims,
                               preferred_element_type=F32)

        def finish(r):
            if bias_ref is not None:
                r = r + bias_ref[...]
            if add_ref is not None:
                r = r + add_ref[...]
            o_ref[...] = r.astype(o_ref.dtype)

        if nk == 1:
            finish(prod)
            return
        acc_ref = rest.pop(0)
        kk = pl.program_id(2)

        @pl.when(kk == 0)
        def _():
            acc_ref[...] = prod

        @pl.when(kk > 0)
        def _():
            acc_ref[...] += prod

        @pl.when(kk == nk - 1)
        def _():
            finish(acc_ref[...])

    return _call(body, name=name, grid=(m // tm, n // tn, nk), in_specs=in_specs,
                 out_specs=pl.BlockSpec((tm, tn), lambda i, j, kk: (i, j)),
                 out_shape=_sds((m, n), out_dtype),
                 scratch=[pltpu.VMEM((tm, tn), F32)] if nk > 1 else [],
                 sem=("parallel", "parallel", "arbitrary"))(*operands)


def _colsum(x, *, name):
    rows, n = x.shape
    tr = _pick(rows, (1024, 512, 256, 128))
    tn = _pick(n, (512, 256, 128))

    def body(x_ref, o_ref):
        @pl.when(pl.program_id(1) == 0)
        def _():
            o_ref[...] = jnp.zeros_like(o_ref)

        o_ref[...] += jnp.sum(x_ref[...].astype(F32), axis=0, keepdims=True)

    return _call(body, name=name, grid=(n // tn, rows // tr),
                 in_specs=[pl.BlockSpec((tr, tn), lambda j, i: (i, j))],
                 out_specs=pl.BlockSpec((1, tn), lambda j, i: (0, j)),
                 out_shape=_sds((1, n), F32), sem=("parallel", "arbitrary"))(x)


def _ln_stats(x):
    mu = jnp.mean(x, axis=-1, keepdims=True)
    xc = x - mu
    var = jnp.mean(xc * xc, axis=-1, keepdims=True)
    rstd = lax.rsqrt(var + LN_EPS)
    return xc * rstd, rstd


def _ln_bwd(dn, n, rstd):
    return rstd * (dn - jnp.mean(dn, axis=-1, keepdims=True) - n * jnp.mean(dn * n, axis=-1, keepdims=True))


def _seq_tiles(t, bsz, cands=(1024, 512, 256, 128, 64, 32, 16, 8)):
    s = t // bsz
    ts = _pick(s, cands)
    return s, ts, s // ts


def _ln_mod_fwd(x, scale, shift, bsz, *, name):
    t, d = x.shape
    _, ts, ns = _seq_tiles(t, bsz)

    def body(x_ref, sc_ref, sh_ref, u_ref):
        one_scale = 1.0 + sc_ref[0]
        shift_v = sh_ref[0]

        def piece(rows):
            n, _ = _ln_stats(x_ref[rows, :])
            u_ref[rows, :] = (n * one_scale + shift_v).astype(u_ref.dtype)

        _chunked(ts, piece)

    row = pl.BlockSpec((ts, d), lambda b, i: (b * ns + i, 0))
    per = pl.BlockSpec((1, 1, d), lambda b, i: (b, 0, 0))
    return _call(body, name=name, grid=(bsz, ns), in_specs=[row, per, per], out_specs=row,
                 out_shape=_sds((t, d), BF16), sem=("parallel", "parallel"))(x, scale, shift)


def _ln_mod_bwd(du, x, scale, dr, alpha, bsz, *, name):
    t, d = x.shape
    _, ts, ns = _seq_tiles(t, bsz)

    def body(du_ref, x_ref, sc_ref, dr_ref, dx_ref, dsc_ref, dsh_ref):
        @pl.when(pl.program_id(1) == 0)
        def _():
            dsc_ref[...] = jnp.zeros_like(dsc_ref)
            dsh_ref[...] = jnp.zeros_like(dsh_ref)

        one_scale = 1.0 + sc_ref[0]

        def piece(rows):
            du_v = du_ref[rows, :]
            n, rstd = _ln_stats(x_ref[rows, :])
            dsc_ref[0] += jnp.sum(du_v * n, axis=0, keepdims=True)
            dsh_ref[0] += jnp.sum(du_v, axis=0, keepdims=True)
            dx_ref[rows, :] = alpha * dr_ref[rows, :] + _ln_bwd(du_v * one_scale, n, rstd)

        _chunked(ts, piece)

    row = pl.BlockSpec((ts, d), lambda b, i: (b * ns + i, 0))
    per = pl.BlockSpec((1, 1, d), lambda b, i: (b, 0, 0))
    return _call(body, name=name, grid=(bsz, ns), in_specs=[row, row, per, row],
                 out_specs=[row, per, per],
                 out_shape=[_sds((t, d), F32), _sds((bsz, 1, d), F32), _sds((bsz, 1, d), F32)],
                 sem=("parallel", "arbitrary"))(du, x, scale, dr)


def _ln_res_fwd(x, y, gate, g, b, alpha, bsz, *, name):
    t, d = x.shape
    _, ts, ns = _seq_tiles(t, bsz)

    def body(x_ref, y_ref, gt_ref, g_ref, b_ref, o_ref):
        one_gate = 1.0 + gt_ref[0]

        def piece(rows):
            n, _ = _ln_stats(alpha * x_ref[rows, :] + one_gate * y_ref[rows, :])
            o_ref[rows, :] = n * g_ref[...] + b_ref[...]

        _chunked(ts, piece)

    row = pl.BlockSpec((ts, d), lambda bb, i: (bb * ns + i, 0))
    per = pl.BlockSpec((1, 1, d), lambda bb, i: (bb, 0, 0))
    vec = pl.BlockSpec((1, d), lambda bb, i: (0, 0))
    return _call(body, name=name, grid=(bsz, ns), in_specs=[row, row, per, vec, vec], out_specs=row,
                 out_shape=_sds((t, d), F32), sem=("parallel", "parallel"))(x, y, gate, g, b)


def _ln_res_bwd(do, x, y, gate, g, alpha, bsz, *, name):
    t, d = x.shape
    _, ts, ns = _seq_tiles(t, bsz)

    def body(do_ref, x_ref, y_ref, gt_ref, g_ref, dr_ref, dy_ref, dgt_ref, dg_ref, db_ref, dys_ref):
        first_tile = pl.program_id(1) == 0

        @pl.when(first_tile)
        def _():
            dgt_ref[...] = jnp.zeros_like(dgt_ref)

        @pl.when(jnp.logical_and(first_tile, pl.program_id(0) == 0))
        def _():
            dg_ref[...] = jnp.zeros_like(dg_ref)
            db_ref[...] = jnp.zeros_like(db_ref)
            dys_ref[...] = jnp.zeros_like(dys_ref)

        one_gate = 1.0 + gt_ref[0]

        def piece(rows):
            do_v = do_ref[rows, :]
            y_v = y_ref[rows, :]
            n, rstd = _ln_stats(alpha * x_ref[rows, :] + one_gate * y_v)
            dg_ref[...] += jnp.sum(do_v * n, axis=0, keepdims=True)
            db_ref[...] += jnp.sum(do_v, axis=0, keepdims=True)
            dr = _ln_bwd(do_v * g_ref[...], n, rstd)
            dr_ref[rows, :] = dr
            dy = one_gate * dr
            dy_ref[rows, :] = dy.astype(dy_ref.dtype)
            dys_ref[...] += jnp.sum(dy, axis=0, keepdims=True)
            dgt_ref[0] += jnp.sum(dr * y_v, axis=0, keepdims=True)

        _chunked(ts, piece)

    row = pl.BlockSpec((ts, d), lambda bb, i: (bb * ns + i, 0))
    per = pl.BlockSpec((1, 1, d), lambda bb, i: (bb, 0, 0))
    vec = pl.BlockSpec((1, d), lambda bb, i: (0, 0))
    return _call(body, name=name, grid=(bsz, ns), in_specs=[row, row, row, per, vec],
                 out_specs=[row, row, per, vec, vec, vec],
                 out_shape=[_sds((t, d), F32), _sds((t, d), BF16), _sds((bsz, 1, d), F32),
                            _sds((1, d), F32), _sds((1, d), F32), _sds((1, d), F32)],
                 sem=("arbitrary", "arbitrary"))(do, x, y, gate, g)


def _loss_head(y, target, *, name):
    t, d = y.shape
    tr = _pick(t, (1024, 512, 256, 128, 64, 32, 16, 8))

    def body(y_ref, t_ref, dy_ref, s_ref):
        @pl.when(pl.program_id(0) == 0)
        def _():
            s_ref[...] = jnp.zeros_like(s_ref)

        def piece(rows):
            e = y_ref[rows, :] - t_ref[rows, :]
            dy_ref[rows, :] = e * (1.0 / d)
            s_ref[...] += jnp.sum(e * e, axis=0, keepdims=True)

        _chunked(tr, piece)

    row = pl.BlockSpec((tr, d), lambda i: (i, 0))
    return _call(body, name=name, grid=(t // tr,), in_specs=[row, row],
                 out_specs=[row, pl.BlockSpec((1, d), lambda i: (0, 0))],
                 out_shape=[_sds((t, d), F32), _sds((1, d), F32)], sem=("arbitrary",))(y, target)


def _sigmoid(v):
    return 1.0 / (1.0 + jnp.exp(-v))


def _silu_rows(c, *, name):
    rows, d = c.shape

    def body(c_ref, o_ref):
        v = c_ref[...]
        o_ref[...] = (v * _sigmoid(v)).astype(o_ref.dtype)

    full = pl.BlockSpec((rows, d), lambda i: (0, 0))
    return _call(body, name=name, grid=(1,), in_specs=[full], out_specs=full,
                 out_shape=_sds((rows, d), BF16), sem=("arbitrary",))(c)


def _glu_fwd(z, ch, *, name):
    t = z.shape[0]
    tr = _pick(t, (1024, 512, 256, 128, 64, 32, 16, 8))

    def body(z_ref, o_ref):
        def piece(rows):
            o_ref[rows, :] = z_ref[rows, :ch] * _sigmoid(z_ref[rows, ch:])

        _chunked(tr, piece)

    return _call(body, name=name, grid=(t // tr,),
                 in_specs=[pl.BlockSpec((tr, 2 * ch), lambda i: (i, 0))],
                 out_specs=pl.BlockSpec((tr, ch), lambda i: (i, 0)),
                 out_shape=_sds((t, ch), F32), sem=("parallel",))(z)


def _glu_bwd(z, da0, ch, *, name):
    t = z.shape[0]
    tr = _pick(t, (1024, 512, 256, 128, 64, 32, 16, 8))

    def body(z_ref, d_ref, o_ref):
        def piece(rows):
            s = _sigmoid(z_ref[rows, ch:])
            d = d_ref[rows, :]
            o_ref[rows, :ch] = (d * s).astype(o_ref.dtype)
            o_ref[rows, ch:] = (d * z_ref[rows, :ch] * s * (1.0 - s)).astype(o_ref.dtype)

        _chunked(tr, piece)

    return _call(body, name=name, grid=(t // tr,),
                 in_specs=[pl.BlockSpec((tr, 2 * ch), lambda i: (i, 0)),
                           pl.BlockSpec((tr, ch), lambda i: (i, 0))],
                 out_specs=pl.BlockSpec((tr, 2 * ch), lambda i: (i, 0)),
                 out_shape=_sds((t, 2 * ch), BF16), sem=("parallel",))(z, da0)


def _lnsilu_fwd(a1, g, b, *, name):
    t, ch = a1.shape
    tr = _pick(t, (1024, 512, 256, 128, 64, 32, 16, 8))

    def body(a_ref, g_ref, b_ref, o_ref):
        def piece(rows):
            n, _ = _ln_stats(a_ref[rows, :])
            a2 = n * g_ref[...] + b_ref[...]
            o_ref[rows, :] = (a2 * _sigmoid(a2)).astype(o_ref.dtype)

        _chunked(tr, piece)

    row = pl.BlockSpec((tr, ch), lambda i: (i, 0))
    vec = pl.BlockSpec((1, ch), lambda i: (0, 0))
    return _call(body, name=name, grid=(t // tr,), in_specs=[row, vec, vec], out_specs=row,
                 out_shape=_sds((t, ch), BF16), sem=("parallel",))(a1, g, b)


def _lnsilu_bwd(a1, da3, g, b, *, name):
    t, ch = a1.shape
    tr = _pick(t, (1024, 512, 256, 128, 64, 32, 16, 8))

    def body(a_ref, d_ref, g_ref, b_ref, o_ref, dg_ref, db_ref):
        @pl.when(pl.program_id(0) == 0)
        def _():
            dg_ref[...] = jnp.zeros_like(dg_ref)
            db_ref[...] = jnp.zeros_like(db_ref)

        def piece(rows):
            n, rstd = _ln_stats(a_ref[rows, :])
            a2 = n * g_ref[...] + b_ref[...]
            s = _sigmoid(a2)
            da2 = d_ref[rows, :] * (s * (1.0 + a2 * (1.0 - s)))
            dg_ref[...] += jnp.sum(da2 * n, axis=0, keepdims=True)
            db_ref[...] += jnp.sum(da2, axis=0, keepdims=True)
            o_ref[rows, :] = _ln_bwd(da2 * g_ref[...], n, rstd)

        _chunked(tr, piece)

    row = pl.BlockSpec((tr, ch), lambda i: (i, 0))
    vec = pl.BlockSpec((1, ch), lambda i: (0, 0))
    return _call(body, name=name, grid=(t // tr,), in_specs=[row, row, vec, vec],
                 out_specs=[row, vec, vec],
                 out_shape=[_sds((t, ch), F32), _sds((1, ch), F32), _sds((1, ch), F32)],
                 sem=("arbitrary",))(a1, da3, g, b)


def _gate_cols(d, ga_off):
    tc = _pick(math.gcd(d, ga_off), (512, 256, 128))
    return tc, ga_off // tc, (ga_off + d) // tc


def _gate_merge_fwd(z, ya, yb, ga_off, *, name):
    t, d = ya.shape
    tr = _pick(t, (1024, 512, 256, 128, 64, 32, 16, 8))
    tc, ga_blk, gb_blk = _gate_cols(d, ga_off)

    def body(ga_ref, gb_ref, ya_ref, yb_ref, o_ref):
        def piece(rows):
            o_ref[rows, :] = (_sigmoid(ga_ref[rows, :]) * ya_ref[rows, :]
                              + _sigmoid(gb_ref[rows, :]) * yb_ref[rows, :]).astype(o_ref.dtype)

        _chunked(tr, piece)

    blk = pl.BlockSpec((tr, tc), lambda i, j: (i, j))
    return _call(body, name=name, grid=(t // tr, d // tc),
                 in_specs=[pl.BlockSpec((tr, tc), lambda i, j: (i, ga_blk + j)),
                           pl.BlockSpec((tr, tc), lambda i, j: (i, gb_blk + j)), blk, blk],
                 out_specs=blk, out_shape=_sds((t, d), BF16), sem=("parallel", "parallel"))(z, z, ya, yb)


def _gate_merge_bwd(z, ya, yb, dm, ga_off, *, name):
    t, d = ya.shape
    tr = _pick(t, (1024, 512, 256, 128, 64, 32, 16, 8))
    tc, ga_blk, gb_blk = _gate_cols(d, ga_off)

    def body(ga_ref, gb_ref, ya_ref, yb_ref, dm_ref, dya_ref, dyb_ref, dga_ref, dgb_ref):
        def piece(rows):
            dm_v = dm_ref[rows, :]
            sa = _sigmoid(ga_ref[rows, :])
            sb = _sigmoid(gb_ref[rows, :])
            dya_ref[rows, :] = (dm_v * sa).astype(dya_ref.dtype)
            dyb_ref[rows, :] = (dm_v * sb).astype(dyb_ref.dtype)
            dga_ref[rows, :] = (dm_v * ya_ref[rows, :] * sa * (1.0 - sa)).astype(dga_ref.dtype)
            dgb_ref[rows, :] = (dm_v * yb_ref[rows, :] * sb * (1.0 - sb)).astype(dgb_ref.dtype)

        _chunked(tr, piece)

    blk = pl.BlockSpec((tr, tc), lambda i, j: (i, j))
    return _call(body, name=name, grid=(t // tr, d // tc),
                 in_specs=[pl.BlockSpec((tr, tc), lambda i, j: (i, ga_blk + j)),
                           pl.BlockSpec((tr, tc), lambda i, j: (i, gb_blk + j)), blk, blk, blk],
                 out_specs=[blk, blk, blk, blk], out_shape=[_sds((t, d), BF16)] * 4,
                 sem=("parallel", "parallel"))(z, z, ya, yb, dm)


def _dwconv_fwd(x, w, b, bsz, halo, *, name):
    t, ch = x.shape
    kw = w.shape[0]
    _, ts, ns = _seq_tiles(t, bsz, (256, 128, 64, 32))
    tc = _pick(ch, (512, 256, 128))
    hb = ts // halo

    def body(x_ref, h_ref, w_ref, b_ref, y_ref, ext_ref):
        ext_ref[pl.ds(0, halo), :] = jnp.where(pl.program_id(1) > 0, h_ref[...], 0.0)
        ext_ref[pl.ds(halo, ts), :] = x_ref[...]
        acc = jnp.zeros((ts, tc), F32) + b_ref[...]
        for k in range(kw):
            acc = acc + w_ref[pl.ds(k, 1), :] * ext_ref[pl.ds(halo - (kw - 1) + k, ts), :]
        y_ref[...] = acc

    cur = pl.BlockSpec((ts, tc), lambda bb, i, j: (bb * ns + i, j))
    prev = pl.BlockSpec((halo, tc), lambda bb, i, j: (jnp.maximum((bb * ns + i) * hb - 1, 0), j))
    return _call(body, name=name, grid=(bsz, ns, ch // tc),
                 in_specs=[cur, prev, pl.BlockSpec((kw, tc), lambda bb, i, j: (0, j)),
                           pl.BlockSpec((1, tc), lambda bb, i, j: (0, j))],
                 out_specs=cur, out_shape=_sds((t, ch), F32), scratch=[pltpu.VMEM((halo + ts, tc), F32)],
                 sem=("parallel", "parallel", "parallel"))(x, x, w, b)


def _dwconv_bwd(x, dy, w, bsz, halo, dx_dtype, *, name):
    t, ch = x.shape
    kw = w.shape[0]
    _, ts, ns = _seq_tiles(t, bsz, (256, 128, 64, 32))
    tc = _pick(ch, (512, 256, 128))
    hb = ts // halo
    last_halo_blk = t // halo - 1

    def body(x_ref, xh_ref, dy_ref, dyh_ref, w_ref, dx_ref, dw_ref, db_ref, extx_ref, exty_ref):
        i = pl.program_id(2)

        @pl.when(jnp.logical_and(pl.program_id(1) == 0, i == 0))
        def _():
            dw_ref[...] = jnp.zeros_like(dw_ref)
            db_ref[...] = jnp.zeros_like(db_ref)

        extx_ref[pl.ds(0, halo), :] = jnp.where(i > 0, xh_ref[...], 0.0)
        extx_ref[pl.ds(halo, ts), :] = x_ref[...]
        dyc = dy_ref[...]
        exty_ref[pl.ds(0, ts), :] = dyc
        exty_ref[pl.ds(ts, halo), :] = jnp.where(i < ns - 1, dyh_ref[...], 0.0)
        acc = jnp.zeros((ts, tc), F32)
        for k in range(kw):
            acc = acc + w_ref[pl.ds(k, 1), :] * exty_ref[pl.ds(kw - 1 - k, ts), :]
            dw_ref[pl.ds(k, 1), :] += jnp.sum(dyc * extx_ref[pl.ds(halo - (kw - 1) + k, ts), :],
                                              axis=0, keepdims=True)
        dx_ref[...] = acc.astype(dx_ref.dtype)
        db_ref[...] += jnp.sum(dyc, axis=0, keepdims=True)

    cur = pl.BlockSpec((ts, tc), lambda j, bb, i: (bb * ns + i, j))
    prev = pl.BlockSpec((halo, tc), lambda j, bb, i: (jnp.maximum((bb * ns + i) * hb - 1, 0), j))
    nxt = pl.BlockSpec((halo, tc), lambda j, bb, i: (jnp.minimum((bb * ns + i + 1) * hb, last_halo_blk), j))
    return _call(body, name=name, grid=(ch // tc, bsz, ns),
                 in_specs=[cur, prev, cur, nxt, pl.BlockSpec((kw, tc), lambda j, bb, i: (0, j))],
                 out_specs=[cur, pl.BlockSpec((kw, tc), lambda j, bb, i: (0, j)),
                            pl.BlockSpec((1, tc), lambda j, bb, i: (0, j))],
                 out_shape=[_sds((t, ch), dx_dtype), _sds((kw, ch), F32), _sds((1, ch), F32)],
                 scratch=[pltpu.VMEM((halo + ts, tc), F32), pltpu.VMEM((ts + halo, tc), F32)],
                 sem=("parallel", "arbitrary", "arbitrary"))(x, x, dy, dy, w)


FFN_ROWS = 64


def _gelu_parts(v):
    cdf = 0.5 * (1.0 + lax.erf(v * (2.0 ** -0.5)))
    return cdf, v * cdf


def _ffn_conv_piece(ext_ref, w_ref, b_ref, base):
    win = ext_ref[pl.ds(base, FFN_ROWS + FFN_PAD), :]
    acc = b_ref[...] + w_ref[pl.ds(2, 1), :] * win[FFN_PAD:]
    acc = acc + w_ref[pl.ds(1, 1), :] * pltpu.roll(win, 1, axis=0)[FFN_PAD:]
    acc = acc + w_ref[pl.ds(0, 1), :] * pltpu.roll(win, 2, axis=0)[FFN_PAD:]
    return acc, win


def _ffn_act_fwd(hp, w, b, bsz, tcf, *, name):
    t, two_f = hp.shape
    s = t // bsz
    gw = 2 * tcf

    def body(h_ref, w_ref, b_ref, f_ref, ext_ref):
        ext_ref[pl.ds(0, FFN_PAD), :] = jnp.zeros((FFN_PAD, gw), F32)
        ext_ref[pl.ds(FFN_PAD, s), :] = h_ref[...].astype(F32)

        def step(i, carry):
            base = pl.multiple_of(i * FFN_ROWS, FFN_ROWS)
            hh, _ = _ffn_conv_piece(ext_ref, w_ref, b_ref, base)
            _, gelu = _gelu_parts(hh[:, :tcf])
            f_ref[pl.ds(base, FFN_ROWS), :] = (gelu * hh[:, tcf:]).astype(f_ref.dtype)
            return carry

        lax.fori_loop(0, s // FFN_ROWS, step, 0)

    return _call(body, name=name, grid=(bsz, two_f // gw),
                 in_specs=[pl.BlockSpec((s, gw), lambda bb, j: (bb, j)),
                           pl.BlockSpec((3, gw), lambda bb, j: (0, j)),
                           pl.BlockSpec((1, gw), lambda bb, j: (0, j))],
                 out_specs=pl.BlockSpec((s, tcf), lambda bb, j: (bb, j)),
                 out_shape=_sds((t, two_f // 2), BF16), scratch=[pltpu.VMEM((FFN_PAD + s, gw), F32)],
                 sem=("parallel", "parallel"))(hp, w, b)


def _ffn_act_bwd(hp, df, w, b, bsz, tcf, *, name):
    t, two_f = hp.shape
    s = t // bsz
    gw = 2 * tcf
    n_rows = FFN_ROWS + FFN_PAD

    def body(h_ref, df_ref, w_ref, b_ref, dhp_ref, dw_ref, db_ref, ext_ref, dh_ref):
        @pl.when(pl.program_id(1) == 0)
        def _():
            dw_ref[...] = jnp.zeros_like(dw_ref)
            db_ref[...] = jnp.zeros_like(db_ref)

        ext_ref[pl.ds(0, FFN_PAD), :] = jnp.zeros((FFN_PAD, gw), F32)
        ext_ref[pl.ds(FFN_PAD, s), :] = h_ref[...].astype(F32)
        dh_ref[pl.ds(s, FFN_PAD), :] = jnp.zeros((FFN_PAD, gw), F32)

        def grad_h(i, carry):
            base = pl.multiple_of(i * FFN_ROWS, FFN_ROWS)
            hh, _ = _ffn_conv_piece(ext_ref, w_ref, b_ref, base)
            hg = hh[:, :tcf]
            d = df_ref[pl.ds(base, FFN_ROWS), :]
            cdf, gelu = _gelu_parts(hg)
            pdf = jnp.exp(-0.5 * hg * hg) * (1.0 / math.sqrt(2.0 * math.pi))
            dh_ref[pl.ds(base, FFN_ROWS), :tcf] = d * hh[:, tcf:] * (cdf + hg * pdf)
            dh_ref[pl.ds(base, FFN_ROWS), tcf:] = d * gelu
            return carry

        lax.fori_loop(0, s // FFN_ROWS, grad_h, 0)

        def grad_x(i, carry):
            dw0, dw1, dw2, dbs = carry
            base = pl.multiple_of(i * FFN_ROWS, FFN_ROWS)
            nxt = dh_ref[pl.ds(base, n_rows), :]
            dyc = nxt[:FFN_ROWS]
            dx = w_ref[pl.ds(2, 1), :] * dyc
            dx = dx + w_ref[pl.ds(1, 1), :] * pltpu.roll(nxt, n_rows - 1, axis=0)[:FFN_ROWS]
            dx = dx + w_ref[pl.ds(0, 1), :] * pltpu.roll(nxt, n_rows - 2, axis=0)[:FFN_ROWS]
            dhp_ref[pl.ds(base, FFN_ROWS), :] = dx.astype(dhp_ref.dtype)
            win = ext_ref[pl.ds(base, n_rows), :]
            dw2 = dw2 + jnp.sum(dyc * win[FFN_PAD:], axis=0, keepdims=True)
            dw1 = dw1 + jnp.sum(dyc * pltpu.roll(win, 1, axis=0)[FFN_PAD:], axis=0, keepdims=True)
            dw0 = dw0 + jnp.sum(dyc * pltpu.roll(win, 2, axis=0)[FFN_PAD:], axis=0, keepdims=True)
            return dw0, dw1, dw2, dbs + jnp.sum(dyc, axis=0, keepdims=True)

        zero = jnp.zeros((1, gw), F32)
        dw0, dw1, dw2, dbs = lax.fori_loop(0, s // FFN_ROWS, grad_x, (zero, zero, zero, zero))
        dw_ref[pl.ds(0, 1), :] += dw0
        dw_ref[pl.ds(1, 1), :] += dw1
        dw_ref[pl.ds(2, 1), :] += dw2
        db_ref[...] += dbs

    grp = pl.BlockSpec((s, gw), lambda j, bb: (bb, j))
    return _call(body, name=name, grid=(two_f // gw, bsz),
                 in_specs=[grp, pl.BlockSpec((s, tcf), lambda j, bb: (bb, j)),
                           pl.BlockSpec((3, gw), lambda j, bb: (0, j)),
                           pl.BlockSpec((1, gw), lambda j, bb: (0, j))],
                 out_specs=[grp, pl.BlockSpec((3, gw), lambda j, bb: (0, j)),
                            pl.BlockSpec((1, gw), lambda j, bb: (0, j))],
                 out_shape=[_sds((t, two_f), BF16), _sds((3, two_f), F32), _sds((1, two_f), F32)],
                 scratch=[pltpu.VMEM((FFN_PAD + s, gw), F32), pltpu.VMEM((s + FFN_PAD, gw), F32)],
                 sem=("parallel", "arbitrary"))(hp, df, w, b)


def _split3(v):
    hi = v.astype(BF16)
    r = v - hi.astype(F32)
    mid = r.astype(BF16)
    lo = (r - mid.astype(F32)).astype(BF16)
    return hi, mid, lo


def _tri_dot(tri, v):
    out = None
    for part in _split3(v):
        term = jnp.dot(tri, part, preferred_element_type=F32)
        out = term if out is None else out + term
    return out


def _fgate_fwd(zf, bsz, heads, *, name):
    t, lanes = zf.shape
    s, blk, nb = _seq_tiles(t, bsz, (256, 128))

    def body(z_ref, cum_ref, cumt_ref, carry_ref):
        @pl.when(pl.program_id(1) == 0)
        def _():
            carry_ref[...] = jnp.zeros_like(carry_ref)

        z = z_ref[...]
        lf = jnp.minimum(z, 0.0) - jnp.log1p(jnp.exp(-jnp.abs(z)))
        r = lax.broadcasted_iota(jnp.int32, (blk, blk), 0)
        c = lax.broadcasted_iota(jnp.int32, (blk, blk), 1)
        tri = (r >= c).astype(BF16)
        cum = _tri_dot(tri, lf) + carry_ref[...]
        cum_ref[...] = cum
        carry_ref[...] = cum[blk - 1:blk, :]
        cumt_ref[0] = jnp.transpose(cum)[:heads, :]

    return _call(body, name=name, grid=(bsz, nb),
                 in_specs=[pl.BlockSpec((blk, lanes), lambda b, i: (b * nb + i, 0))],
                 out_specs=[pl.BlockSpec((blk, lanes), lambda b, i: (b * nb + i, 0)),
                            pl.BlockSpec((1, heads, blk), lambda b, i: (b, 0, i))],
                 out_shape=[_sds((t, lanes), F32), _sds((bsz, heads, s), F32)],
                 scratch=[pltpu.VMEM((1, lanes), F32)], sem=("parallel", "arbitrary"))(zf)


def _fgate_bwd(dcumt, zf, bsz, heads, *, name):
    t, lanes = zf.shape
    s, blk, nb = _seq_tiles(t, bsz, (256, 128))

    def body(d_ref, z_ref, o_ref, carry_ref):
        @pl.when(pl.program_id(1) == 0)
        def _():
            carry_ref[...] = jnp.zeros_like(carry_ref)

        d = jnp.concatenate([d_ref[0], jnp.zeros((lanes - heads, blk), F32)], axis=0)
        dcol = jnp.transpose(d)
        r = lax.broadcasted_iota(jnp.int32, (blk, blk), 0)
        c = lax.broadcasted_iota(jnp.int32, (blk, blk), 1)
        tri = (c >= r).astype(BF16)
        suf = _tri_dot(tri, dcol) + carry_ref[...]
        carry_ref[...] = suf[0:1, :]
        o_ref[...] = suf * _sigmoid(-z_ref[...])

    return _call(body, name=name, grid=(bsz, nb),
                 in_specs=[pl.BlockSpec((1, heads, blk), lambda b, i: (b, 0, nb - 1 - i)),
                           pl.BlockSpec((blk, lanes), lambda b, i: (b * nb + nb - 1 - i, 0))],
                 out_specs=pl.BlockSpec((blk, lanes), lambda b, i: (b * nb + nb - 1 - i, 0)),
                 out_shape=_sds((t, lanes), F32), scratch=[pltpu.VMEM((1, lanes), F32)],
                 sem=("parallel", "arbitrary"))(dcumt, zf)


def _attn_fwd(z, cum, cumt, bsz, heads, q_off, *, name):
    t = z.shape[0]
    width = heads * HEAD_DIM
    s, bq, nq = _seq_tiles(t, bsz, (256, 128))
    qb = q_off // width
    lanes = cum.shape[1]

    def body(q_ref, k_ref, v_ref, cc_ref, cr_ref, o_ref, lse_ref):
        i = pl.program_id(1)
        row = i * bq + lax.broadcasted_iota(jnp.int32, (bq, bq), 0)
        col = lax.broadcasted_iota(jnp.int32, (bq, bq), 1)
        lse_ref[...] = jnp.zeros_like(lse_ref)
        for h in range(heads):
            sl = slice(h * HEAD_DIM, (h + 1) * HEAD_DIM)
            qh = q_ref[:, sl].astype(BF16)
            cq = cc_ref[:, h:h + 1]

            def step(j, carry, sl=sl, qh=qh, cq=cq, h=h):
                m, l, acc = carry
                off = pl.multiple_of(j * bq, bq)
                kj = k_ref[pl.ds(off, bq), sl].astype(BF16)
                vj = v_ref[pl.ds(off, bq), sl].astype(BF16)
                sc = lax.dot_general(qh, kj, (((1,), (1,)), ((), ())), preferred_element_type=F32) * ATTN_SCALE
                sc = (sc + cq) - cr_ref[0, h:h + 1, pl.ds(off, bq)]
                sc = jnp.where(row >= col + off, sc, NEG)
                m_new = jnp.maximum(m, jnp.max(sc, axis=-1, keepdims=True))
                p = jnp.exp(sc - m_new)
                a = jnp.exp(m - m_new)
                l = a * l + jnp.sum(p, axis=-1, keepdims=True)
                p_hi = p.astype(BF16)
                p_lo = (p - p_hi.astype(F32)).astype(BF16)
                acc = a * acc + (jnp.dot(p_hi, vj, preferred_element_type=F32)
                                 + jnp.dot(p_lo, vj, preferred_element_type=F32))
                return m_new, l, acc

            init = (jnp.full((bq, 1), NEG, F32), jnp.zeros((bq, 1), F32), jnp.zeros((bq, HEAD_DIM), F32))
            m, l, acc = lax.fori_loop(0, i + 1, step, init)
            o_ref[:, sl] = acc / l
            lse_ref[:, h:h + 1] = m + jnp.log(l)

    return _call(body, name=name, grid=(bsz, nq),
                 in_specs=[pl.BlockSpec((bq, width), lambda b, i: (b * nq + i, qb)),
                           pl.BlockSpec((s, width), lambda b, i: (b, qb + 1)),
                           pl.BlockSpec((s, width), lambda b, i: (b, qb + 2)),
                           pl.BlockSpec((bq, lanes), lambda b, i: (b * nq + i, 0)),
                           pl.BlockSpec((1, heads, s), lambda b, i: (b, 0, 0))],
                 out_specs=[pl.BlockSpec((bq, width), lambda b, i: (b * nq + i, 0)),
                            pl.BlockSpec((bq, lanes), lambda b, i: (b * nq + i, 0))],
                 out_shape=[_sds((t, width), F32), _sds((t, lanes), F32)],
                 sem=("parallel", "parallel"))(z, z, z, cum, cumt)


def _attn_bwd(z, cum, cumt, o, do, lse, bsz, heads, q_off, *, name):
    t = z.shape[0]
    width = heads * HEAD_DIM
    s, bq, nq = _seq_tiles(t, bsz, (256, 128))
    qb = q_off // width
    lanes = cum.shape[1]
    tn_dims = (((0,), (0,)), ((), ()))
    nt_dims = (((1,), (1,)), ((), ()))

    def body(q_ref, k_ref, v_ref, cc_ref, cr_ref, o_ref, do_ref, lse_ref,
             dq_ref, dk_ref, dv_ref, dcr_ref, dk_acc, dv_acc):
        i = pl.program_id(1)

        @pl.when(i == 0)
        def _():
            dk_acc[...] = jnp.zeros_like(dk_acc)
            dv_acc[...] = jnp.zeros_like(dv_acc)
            dcr_ref[...] = jnp.zeros_like(dcr_ref)

        row = i * bq + lax.broadcasted_iota(jnp.int32, (bq, bq), 0)
        col = lax.broadcasted_iota(jnp.int32, (bq, bq), 1)
        for h in range(heads):
            sl = slice(h * HEAD_DIM, (h + 1) * HEAD_DIM)
            qh = q_ref[:, sl].astype(BF16)
            doh = do_ref[:, sl].astype(BF16)
            delta = jnp.sum(doh.astype(F32) * o_ref[:, sl], axis=-1, keepdims=True)
            cq = cc_ref[:, h:h + 1]
            lse_h = lse_ref[:, h:h + 1]

            def step(j, dq, sl=sl, qh=qh, doh=doh, delta=delta, cq=cq, lse_h=lse_h, h=h):
                off = pl.multiple_of(j * bq, bq)
                kj = k_ref[pl.ds(off, bq), sl].astype(BF16)
                vj = v_ref[pl.ds(off, bq), sl].astype(BF16)
                sc = lax.dot_general(qh, kj, nt_dims, preferred_element_type=F32) * ATTN_SCALE
                sc = (sc + cq) - cr_ref[0, h:h + 1, pl.ds(off, bq)]
                p = jnp.where(row >= col + off, jnp.exp(sc - lse_h), 0.0)
                dp = lax.dot_general(doh, vj, nt_dims, preferred_element_type=F32)
                ds = p * (dp - delta)
                dsb = ds.astype(BF16)
                dv_acc[pl.ds(off, bq), sl] += lax.dot_general(p.astype(BF16), doh, tn_dims,
                                                               preferred_element_type=F32)
                dk_acc[pl.ds(off, bq), sl] += lax.dot_general(dsb, qh, tn_dims,
                                                               preferred_element_type=F32) * ATTN_SCALE
                dcr_ref[0, h:h + 1, pl.ds(off, bq)] -= jnp.sum(ds, axis=0, keepdims=True)
                return dq + jnp.dot(dsb, kj, preferred_element_type=F32) * ATTN_SCALE

            dq = lax.fori_loop(0, i + 1, step, jnp.zeros((bq, HEAD_DIM), F32))
            dq_ref[:, sl] = dq.astype(dq_ref.dtype)

        @pl.when(i == nq - 1)
        def _():
            dk_ref[...] = dk_acc[...].astype(dk_ref.dtype)
            dv_ref[...] = dv_acc[...].astype(dv_ref.dtype)

    qrow = pl.BlockSpec((bq, width), lambda b, i: (b * nq + i, 0))
    whole = pl.BlockSpec((s, width), lambda b, i: (b, 0))
    return _call(body, name=name, grid=(bsz, nq),
                 in_specs=[pl.BlockSpec((bq, width), lambda b, i: (b * nq + i, qb)),
                           pl.BlockSpec((s, width), lambda b, i: (b, qb + 1)),
                           pl.BlockSpec((s, width), lambda b, i: (b, qb + 2)),
                           pl.BlockSpec((bq, lanes), lambda b, i: (b * nq + i, 0)),
                           pl.BlockSpec((1, heads, s), lambda b, i: (b, 0, 0)),
                           qrow, qrow, pl.BlockSpec((bq, lanes), lambda b, i: (b * nq + i, 0))],
                 out_specs=[qrow, whole, whole, pl.BlockSpec((1, heads, s), lambda b, i: (b, 0, 0))],
                 out_shape=[_sds((t, width), BF16), _sds((t, width), BF16), _sds((t, width), BF16),
                            _sds((bsz, heads, s), F32)],
                 scratch=[pltpu.VMEM((s, width), F32), pltpu.VMEM((s, width), F32)],
                 sem=("parallel", "arbitrary"))(z, z, z, cum, cumt, o, do, lse)


def _adamw(w, g, m, v, *, name):
    shape = w.shape
    cols = shape[-1]
    rows = w.size // cols
    tr = rows if rows <= 256 else _pick(rows, (256, 128, 64, 32, 16, 8))
    bc1 = 1.0 - ADAM_B1 ** ADAM_STEP
    bc2 = 1.0 - ADAM_B2 ** ADAM_STEP

    def body(w_ref, g_ref, m_ref, v_ref, d_ref, nm_ref, nv_ref):
        g_v = g_ref[...]
        nm = ADAM_B1 * m_ref[...] + (1.0 - ADAM_B1) * g_v
        nv = ADAM_B2 * v_ref[...] + (1.0 - ADAM_B2) * (g_v * g_v)
        nm_ref[...] = nm
        nv_ref[...] = nv
        d_ref[...] = -ADAM_LR * ((nm / bc1) / (jnp.sqrt(nv / bc2) + ADAM_EPS) + ADAM_WD * w_ref[...])

    blk = pl.BlockSpec((tr, cols), lambda i: (i, 0))
    flat = [a.reshape(rows, cols) for a in (w, g, m, v)]
    outs = _call(body, name=name, grid=(rows // tr,), in_specs=[blk] * 4, out_specs=[blk] * 3,
                 out_shape=[_sds((rows, cols), F32)] * 3, sem=("parallel",))(*flat)
    return tuple(a.reshape(shape) for a in outs)


_ANY = pl.BlockSpec(memory_space=pl.ANY)


def _comm_call(body, *, name, n_in, out_shape, n_sems):
    scratch = [pltpu.SemaphoreType.DMA((n_sems,)), pltpu.SemaphoreType.DMA((n_sems,)),
               pltpu.SemaphoreType.DMA((len(out_shape),))]
    return pl.pallas_call(body, name=name, in_specs=[_ANY] * n_in, out_specs=[_ANY] * len(out_shape),
                          out_shape=out_shape, scratch_shapes=scratch)


def _place():
    x, y, c = lax.axis_index("x"), lax.axis_index("y"), lax.axis_index("c")
    return x, y, c, [(1 - x, y), (x, 1 - y), (1 - x, 1 - y)]


def _remote(src, dst, send_sems, recv_sems, sem, to):
    return pltpu.make_async_remote_copy(src_ref=src, dst_ref=dst, send_sem=send_sems.at[sem],
                                        recv_sem=recv_sems.at[sem], device_id=to, device_id_type=MESH)


def _all_gather8(v, *, name):
    def body(v_ref, out_ref, send_sems, recv_sems, local_sems):
        x, y, c, _ = _place()
        me = 4 * x + 2 * y + c
        mine = pltpu.make_async_copy(v_ref, out_ref.at[me], local_sems.at[0])
        mine.start()
        peers = []
        for k in range(1, N_DEVICES):
            px = 1 - x if k & 4 else x
            py = 1 - y if k & 2 else y
            pc = 1 - c if k & 1 else c
            peers.append((px, py, pc))
        sends = [_remote(v_ref, out_ref.at[me], send_sems, recv_sems, k, peer) for k, peer in enumerate(peers)]
        for cp in sends:
            cp.start()
        for k, (px, py, pc) in enumerate(peers):
            _remote(v_ref, out_ref.at[4 * px + 2 * py + pc], send_sems, recv_sems, k, (px, py, pc)).wait_recv()
        for cp in sends:
            cp.wait_send()
        mine.wait()

    out = _comm_call(body, name=name, n_in=1, out_shape=[_sds((N_DEVICES,) + v.shape, v.dtype)],
                     n_sems=N_DEVICES - 1)(v)
    return out[0]


def _window(ref, mode, layer, chip, rows, cols):
    if mode == "slab":
        return ref.at[layer, chip]
    if mode == "cols":
        return ref.at[layer, :, pl.ds(pl.multiple_of(chip * cols, LANES), cols)]
    return ref.at[layer, pl.ds(pl.multiple_of(chip * rows, SUBLANES), rows), :]


def _whole_shape(mode, shard_shape):
    layers, rows, cols = shard_shape
    if mode == "slab":
        return (layers, N_CHIPS, rows, cols)
    if mode == "cols":
        assert cols % LANES == 0
        return (layers, rows, N_CHIPS * cols)
    assert rows % 16 == 0
    return (layers, N_CHIPS * rows, cols)


def _gather_weights(shards, modes, *, name):
    n = len(shards)
    meta = [(mode,) + tuple(a.shape[1:]) for a, mode in zip(shards, modes)]
    for a in shards:
        assert a.shape[0] == 2

    def body(*refs):
        ins, outs = refs[:n], refs[n:2 * n]
        send_sems, recv_sems, local_sems = refs[2 * n:]
        x, y, c, chips = _place()
        me = 2 * x + y
        local, first, passed = [], [], []
        for i, (mode, rows, cols) in enumerate(meta):
            cp = pltpu.make_async_copy(ins[i], _window(outs[i], mode, slice(None), me, rows, cols), local_sems.at[i])
            cp.start()
            local.append(cp)
            for r, (cx, cy) in enumerate(chips):
                cp = _remote(ins[i].at[c], _window(outs[i], mode, c, me, rows, cols), send_sems, recv_sems,
                             6 * i + r, (cx, cy, c))
                cp.start()
                first.append(cp)
        for i, (mode, rows, cols) in enumerate(meta):
            for r, (cx, cy) in enumerate(chips):
                win = _window(outs[i], mode, c, 2 * cx + cy, rows, cols)
                _remote(win, win, send_sems, recv_sems, 6 * i + r, (cx, cy, c)).wait_recv()
                cp = _remote(win, win, send_sems, recv_sems, 6 * i + 3 + r, (x, y, 1 - c))
                cp.start()
                passed.append(cp)
        for i, (mode, rows, cols) in enumerate(meta):
            for r, (cx, cy) in enumerate(chips):
                win = _window(outs[i], mode, 1 - c, 2 * cx + cy, rows, cols)
                _remote(win, win, send_sems, recv_sems, 6 * i + 3 + r, (x, y, 1 - c)).wait_recv()
        for cp in first + passed:
            cp.wait_send()
        for cp in local:
            cp.wait()

    out_shape = [_sds(_whole_shape(mode, a.shape), a.dtype) for a, mode in zip(shards, modes)]
    return _comm_call(body, name=name, n_in=n, out_shape=out_shape, n_sems=6 * n)(*shards)


def _rs_swap(grads, *, name):
    n = len(grads)

    def body(*refs):
        ins, outs = refs[:n], refs[n:2 * n]
        send_sems, recv_sems, _ = refs[2 * n:]
        x, y, c, _ = _place()
        copies = [_remote(ins[i].at[1 - c], outs[i], send_sems, recv_sems, i, (x, y, 1 - c)) for i in range(n)]
        for cp in copies:
            cp.start()
        for cp in copies:
            cp.wait()

    return _comm_call(body, name=name, n_in=n, out_shape=[_sds(g.shape[1:], g.dtype) for g in grads], n_sems=n)(*grads)


def _part(ref, mode, chip, rows, cols):
    if mode == "slab":
        return ref.at[chip]
    if mode == "cols":
        return ref.at[:, pl.ds(pl.multiple_of(chip * cols, LANES), cols)]
    return ref.at[pl.ds(pl.multiple_of(chip * rows, SUBLANES), rows), :]


def _rs_scatter(parts, modes, shard_shapes, *, name):
    n = len(parts)
    meta = [(mode,) + tuple(shp[1:]) for mode, shp in zip(modes, shard_shapes)]

    def body(*refs):
        ins, outs = refs[:n], refs[n:2 * n]
        send_sems, recv_sems, local_sems = refs[2 * n:]
        x, y, c, chips = _place()
        me = 2 * x + y
        local, sends = [], []
        for i, (mode, rows, cols) in enumerate(meta):
            cp = pltpu.make_async_copy(_part(ins[i], mode, me, rows, cols), outs[i].at[me], local_sems.at[i])
            cp.start()
            local.append(cp)
            for r, (cx, cy) in enumerate(chips):
                cp = _remote(_part(ins[i], mode, 2 * cx + cy, rows, cols), outs[i].at[me], send_sems, recv_sems,
                             3 * i + r, (cx, cy, c))
                cp.start()
                sends.append(cp)
        for i, (mode, rows, cols) in enumerate(meta):
            for r, (cx, cy) in enumerate(chips):
                k = 2 * cx + cy
                _remote(_part(ins[i], mode, k, rows, cols), outs[i].at[k], send_sems, recv_sems, 3 * i + r,
                        (cx, cy, c)).wait_recv()
        for cp in sends:
            cp.wait_send()
        for cp in local:
            cp.wait()

    out_shape = [_sds((N_CHIPS,) + tuple(shp[1:]), p.dtype) for p, shp in zip(parts, shard_shapes)]
    return _comm_call(body, name=name, n_in=n, out_shape=out_shape, n_sems=3 * n)(*parts)


def _rs_gather(sums, *, name):
    n = len(sums)

    def body(*refs):
        ins, outs = refs[:n], refs[n:2 * n]
        send_sems, recv_sems, local_sems = refs[2 * n:]
        x, y, c, _ = _place()
        local, sends = [], []
        for i in range(n):
            cp = pltpu.make_async_copy(ins[i], outs[i].at[c], local_sems.at[i])
            cp.start()
            local.append(cp)
            cp = _remote(ins[i], outs[i].at[c], send_sems, recv_sems, i, (x, y, 1 - c))
            cp.start()
            sends.append(cp)
        for i in range(n):
            _remote(ins[i], outs[i].at[1 - c], send_sems, recv_sems, i, (x, y, 1 - c)).wait_recv()
        for cp in sends:
            cp.wait_send()
        for cp in local:
            cp.wait()

    return _comm_call(body, name=name, n_in=n, out_shape=[_sds((2,) + s.shape, s.dtype) for s in sums], n_sems=n)(*sums)


def _row_tile(rows, cols, itemsize):
    target = max(SUBLANES, (2 << 20) // (cols * itemsize))
    cands = [c for c in (2048, 1024, 512, 256, 128, 64, 32, 16) if c <= target]
    tr = _pick(rows, cands)
    return tr


def _add_layer(g, other, core, *, name):
    _, rows, cols = g.shape
    tr = _row_tile(rows, cols, 4)

    def body(core_ref, g_ref, o_ref, out_ref):
        out_ref[...] = (g_ref[0] + o_ref[...]).astype(out_ref.dtype)

    grid_spec = pltpu.PrefetchScalarGridSpec(
        num_scalar_prefetch=1, grid=(rows // tr,),
        in_specs=[pl.BlockSpec((1, tr, cols), lambda i, core_ref: (core_ref[0], i, 0)),
                  pl.BlockSpec((tr, cols), lambda i, core_ref: (i, 0))],
        out_specs=pl.BlockSpec((tr, cols), lambda i, core_ref: (i, 0)))
    return pl.pallas_call(body, name=name, grid_spec=grid_spec, out_shape=_sds((rows, cols), BF16),
                          compiler_params=pltpu.CompilerParams(dimension_semantics=("parallel",),
                                                               vmem_limit_bytes=VMEM_LIMIT))(core, g, other)


def _sum_slots(parts, *, name):
    n, rows, cols = parts.shape
    tr = _row_tile(rows, cols, 4)

    def body(p_ref, o_ref):
        acc = p_ref[0].astype(F32) + p_ref[1].astype(F32)
        for k in range(2, n):
            acc = acc + p_ref[k].astype(F32)
        o_ref[...] = acc

    return _call(body, name=name, grid=(rows // tr,),
                 in_specs=[pl.BlockSpec((n, tr, cols), lambda i: (0, i, 0))],
                 out_specs=pl.BlockSpec((tr, cols), lambda i: (i, 0)),
                 out_shape=_sds((rows, cols), F32), sem=("parallel",))(parts)


def _reduce_scatter(grads, modes, shard_shapes):
    core = lax.axis_index("c").astype(jnp.int32).reshape(1)
    flat = [g.reshape(g.shape[0], -1, g.shape[-1]) for g in grads]
    from_sibling = _rs_swap(flat, name="rs_swap")
    parts = []
    for i, (g, o) in enumerate(zip(flat, from_sibling)):
        p = _add_layer(g, o, core, name=f"rs_add_{i}")
        parts.append(p.reshape(grads[i].shape[1:]))
    from_chips = _rs_scatter(parts, modes, shard_shapes, name="rs_scatter")
    sums = [_sum_slots(r, name=f"rs_sum_{i}") for i, r in enumerate(from_chips)]
    return _rs_gather(sums, name="rs_gather")


def _interleave(w, tcf):
    lead = w.shape[:-1]
    f = w.shape[-1] // 2
    return jnp.swapaxes(w.reshape(lead + (2, f // tcf, tcf)), -3, -2).reshape(lead + (2 * f,))


def _deinterleave(w, tcf):
    lead = w.shape[:-1]
    f = w.shape[-1] // 2
    return jnp.swapaxes(w.reshape(lead + (f // tcf, 2, tcf)), -3, -2).reshape(lead + (2 * f,))


def _layer_weights(full, rep, layer, dims):
    f_off, n_heads, tcf = dims["f_off"], dims["heads"], dims["tcf"]
    w_in = full["w_in"][layer]
    b_in = rep["b_in"][layer]
    pad = LANES - n_heads
    return {
        "w_main": jnp.concatenate([w_in[:, :f_off], w_in[:, f_off + n_heads:]], axis=1),
        "b_main": jnp.concatenate([b_in[:f_off], b_in[f_off + n_heads:]])[None],
        "w_f": jnp.pad(w_in[:, f_off:f_off + n_heads], ((0, 0), (0, pad))),
        "b_f": jnp.pad(b_in[f_off:f_off + n_heads], (0, pad))[None],
        "conv_a_w": full["conv_a_w"][layer],
        "conv_a_b": rep["conv_a_b"][layer][None],
        "ln_conv_g": rep["ln_conv_g"][layer][None],
        "ln_conv_b": rep["ln_conv_b"][layer][None],
        "w_conv_proj": full["w_conv_proj"][layer],
        "w_attn_proj": full["w_attn_proj"][layer],
        "w_mix_out": full["w_mix_out"][layer],
        "b_mix_out": rep["b_mix_out"][layer][None],
        "ln1_g": rep["ln1_g"][layer][None],
        "ln1_b": rep["ln1_b"][layer][None],
        "w_ffn_up": _interleave(full["w_ffn_up"][layer], tcf),
        "ffn_conv_w": _interleave(full["ffn_conv_w"][layer], tcf),
        "ffn_conv_b": _interleave(rep["ffn_conv_b"][layer], tcf)[None],
        "w_ffn_down": full["w_ffn_down"][layer],
        "ln2_g": rep["ln2_g"][layer][None],
        "ln2_b": rep["ln2_b"][layer][None],
    }


def _layer_fwd(x, mod, p, dims, tag):
    bsz, d, ch, heads, alpha = dims["bsz"], dims["d"], dims["ch"], dims["heads"], dims["alpha"]
    mods = [mod[:, k * d:(k + 1) * d][:, None, :] for k in range(6)]
    shift1, scale1, gate1, shift2, scale2, gate2 = mods
    u = _ln_mod_fwd(x, scale1, shift1, bsz, name=f"ln_mod1_{tag}")
    zm = _matmul(u, p["w_main"], "nn", F32, bias=p["b_main"], name=f"in_main_{tag}")
    zf = _matmul(u, p["w_f"], "nn", F32, bias=p["b_f"], name=f"in_forget_{tag}")
    a0 = _glu_fwd(zm, ch, name=f"glu_{tag}")
    a1 = _dwconv_fwd(a0, p["conv_a_w"], p["conv_a_b"], bsz, CONV_A_HALO, name=f"conv_a_{tag}")
    a3 = _lnsilu_fwd(a1, p["ln_conv_g"], p["ln_conv_b"], name=f"lnsilu_{tag}")
    ya = _matmul(a3, p["w_conv_proj"], "nn", F32, name=f"conv_proj_{tag}")
    cum, cumt = _fgate_fwd(zf, bsz, heads, name=f"fgate_{tag}")
    o, lse = _attn_fwd(zm, cum, cumt, bsz, heads, 2 * ch, name=f"attn_{tag}")
    yb = _matmul(o, p["w_attn_proj"], "nn", F32, name=f"attn_proj_{tag}")
    m = _gate_merge_fwd(zm, ya, yb, dims["ga_off"], name=f"merge_{tag}")
    mix = _matmul(m, p["w_mix_out"], "nn", F32, bias=p["b_mix_out"], name=f"mix_out_{tag}")
    x1 = _ln_res_fwd(x, mix, gate1, p["ln1_g"], p["ln1_b"], alpha, bsz, name=f"ln_res1_{tag}")
    u2 = _ln_mod_fwd(x1, scale2, shift2, bsz, name=f"ln_mod2_{tag}")
    hp = _matmul(u2, p["w_ffn_up"], "nn", BF16, name=f"ffn_up_{tag}")
    f = _ffn_act_fwd(hp, p["ffn_conv_w"], p["ffn_conv_b"], bsz, dims["tcf"], name=f"ffn_act_{tag}")
    ffn = _matmul(f, p["w_ffn_down"], "nn", F32, name=f"ffn_down_{tag}")
    x2 = _ln_res_fwd(x1, ffn, gate2, p["ln2_g"], p["ln2_b"], alpha, bsz, name=f"ln_res2_{tag}")
    saved = dict(x=x, mods=mods, u=u, zm=zm, zf=zf, a0=a0, a1=a1, a3=a3, ya=ya, yb=yb, cum=cum, cumt=cumt,
                 o=o, lse=lse, m=m, mix=mix, x1=x1, u2=u2, hp=hp, f=f, ffn=ffn)
    return x2, saved


def _layer_bwd(dx2, p, sv, dims, tag):
    bsz, ch, heads, alpha = dims["bsz"], dims["ch"], dims["heads"], dims["alpha"]
    f_off, tcf = dims["f_off"], dims["tcf"]
    shift1, scale1, gate1, shift2, scale2, gate2 = sv["mods"]
    g = {}
    dr2, dffn, dgate2, g["ln2_g"], g["ln2_b"], _ = _ln_res_bwd(
        dx2, sv["x1"], sv["ffn"], gate2, p["ln2_g"], alpha, bsz, name=f"ln_res2_bwd_{tag}")
    df = _matmul(dffn, p["w_ffn_down"], "nt", F32, name=f"ffn_down_dx_{tag}")
    g["w_ffn_down"] = _matmul(sv["f"], dffn, "tn", F32, name=f"ffn_down_dw_{tag}")
    dhp, dfw, dfb = _ffn_act_bwd(sv["hp"], df, p["ffn_conv_w"], p["ffn_conv_b"], bsz, tcf, name=f"ffn_act_bwd_{tag}")
    g["ffn_conv_w"] = _deinterleave(dfw, tcf)
    g["ffn_conv_b"] = _deinterleave(dfb, tcf)[0]
    du2 = _matmul(dhp, p["w_ffn_up"], "nt", F32, name=f"ffn_up_dx_{tag}")
    g["w_ffn_up"] = _deinterleave(_matmul(sv["u2"], dhp, "tn", F32, name=f"ffn_up_dw_{tag}"), tcf)
    dx1, dscale2, dshift2 = _ln_mod_bwd(du2, sv["x1"], scale2, dr2, alpha, bsz, name=f"ln_mod2_bwd_{tag}")
    dr1, dmix, dgate1, g["ln1_g"], g["ln1_b"], g["b_mix_out"] = _ln_res_bwd(
        dx1, sv["x"], sv["mix"], gate1, p["ln1_g"], alpha, bsz, name=f"ln_res1_bwd_{tag}")
    dm = _matmul(dmix, p["w_mix_out"], "nt", F32, name=f"mix_out_dx_{tag}")
    g["w_mix_out"] = _matmul(sv["m"], dmix, "tn", F32, name=f"mix_out_dw_{tag}")
    dya, dyb, dzga, dzgb = _gate_merge_bwd(sv["zm"], sv["ya"], sv["yb"], dm, dims["ga_off"], name=f"merge_bwd_{tag}")
    da3 = _matmul(dya, p["w_conv_proj"], "nt", F32, name=f"conv_proj_dx_{tag}")
    g["w_conv_proj"] = _matmul(sv["a3"], dya, "tn", F32, name=f"conv_proj_dw_{tag}")
    do = _matmul(dyb, p["w_attn_proj"], "nt", F32, name=f"attn_proj_dx_{tag}")
    g["w_attn_proj"] = _matmul(sv["o"], dyb, "tn", F32, name=f"attn_proj_dw_{tag}")
    da1, g["ln_conv_g"], g["ln_conv_b"] = _lnsilu_bwd(sv["a1"], da3, p["ln_conv_g"], p["ln_conv_b"],
                                                      name=f"lnsilu_bwd_{tag}")
    da0, g["conv_a_w"], dcb = _dwconv_bwd(sv["a0"], da1, p["conv_a_w"], bsz, CONV_A_HALO, F32, name=f"conv_a_bwd_{tag}")
    g["conv_a_b"] = dcb[0]
    dzglu = _glu_bwd(sv["zm"], da0, ch, name=f"glu_bwd_{tag}")
    dq, dk, dv, dcumt = _attn_bwd(sv["zm"], sv["cum"], sv["cumt"], sv["o"], do, sv["lse"], bsz, heads, 2 * ch,
                                  name=f"attn_bwd_{tag}")
    dzf = _fgate_bwd(dcumt, sv["zf"], bsz, heads, name=f"fgate_bwd_{tag}")
    dzm = jnp.concatenate([dzglu, dq, dk, dv, dzga, dzgb], axis=1)
    du = _matmul(dzm, p["w_main"], "nt", F32, name=f"in_main_dx_{tag}")
    du = _matmul(dzf, p["w_f"], "nt", F32, add=du, name=f"in_forget_dx_{tag}")
    dwm = _matmul(sv["u"], dzm, "tn", F32, name=f"in_main_dw_{tag}")
    dwf = _matmul(sv["u"], dzf, "tn", F32, name=f"in_forget_dw_{tag}")
    dbm = _colsum(dzm, name=f"in_main_db_{tag}")[0]
    dbf = _colsum(dzf, name=f"in_forget_db_{tag}")[0]
    g["w_in"] = jnp.concatenate([dwm[:, :f_off], dwf[:, :heads], dwm[:, f_off:]], axis=1)
    g["b_in"] = jnp.concatenate([dbm[:f_off], dbf[:heads], dbm[f_off:]])
    dx, dscale1, dshift1 = _ln_mod_bwd(du, sv["x"], scale1, dr1, alpha, bsz, name=f"ln_mod1_bwd_{tag}")
    dmod = jnp.concatenate([dshift1, dscale1, dgate1, dshift2, dscale2, dgate2], axis=2)[:, 0, :]
    return dx, g, dmod


def _local_step(x, mod, loss_target, full, rep, dims):
    bsz, seq, d = x.shape
    layers = mod.shape[0]
    params = [_layer_weights(full, rep, layer, dims) for layer in range(layers)]
    h = x.reshape(bsz * seq, d)
    saved = []
    for layer in range(layers):
        h, sv = _layer_fwd(h, mod[layer], params[layer], dims, f"l{layer}")
        saved.append(sv)
    dh, sq = _loss_head(h, loss_target.reshape(bsz * seq, d), name="loss_head")
    loss_local = 0.5 * jnp.sum(sq) / d
    grads, dmods = [None] * layers, [None] * layers
    for layer in reversed(range(layers)):
        dh, grads[layer], dmods[layer] = _layer_bwd(dh, params[layer], saved[layer], dims, f"l{layer}")
    stacked = {wname: jnp.stack([grads[layer][wname] for layer in range(layers)]) for wname in grads[0]}
    return loss_local, dh.reshape(bsz, seq, d), stacked, jnp.stack(dmods)


def _pad_rows(a):
    extra = -a.shape[-2] % (2 * SUBLANES)
    if extra == 0:
        return a
    return jnp.pad(a, [(0, 0)] * (a.ndim - 2) + [(0, extra), (0, 0)])


def _to_slab(g):
    layers, k, n4 = g.shape
    return jnp.transpose(g.reshape(layers, k, N_CHIPS, n4 // N_CHIPS), (0, 2, 1, 3))


def _from_slab(w):
    layers, _, k, n = w.shape
    return jnp.transpose(w, (0, 2, 1, 3)).reshape(layers, k, N_CHIPS * n)


def kernel(x, c, w_ada, b_ada, w_in, b_in, conv_a_w, conv_a_b, ln_conv_g, ln_conv_b, w_conv_proj, w_attn_proj, w_mix_out, b_mix_out, ln1_g, ln1_b, w_ffn_up, ffn_conv_w, ffn_conv_b, w_ffn_down, ln2_g, ln2_b, loss_target, m_w_ada, m_b_ada, m_w_in, m_b_in, m_conv_a_w, m_conv_a_b, m_ln_conv_g, m_ln_conv_b, m_w_conv_proj, m_w_attn_proj, m_w_mix_out, m_b_mix_out, m_ln1_g, m_ln1_b, m_w_ffn_up, m_ffn_conv_w, m_ffn_conv_b, m_w_ffn_down, m_ln2_g, m_ln2_b, v_w_ada, v_b_ada, v_w_in, v_b_in, v_conv_a_w, v_conv_a_b, v_ln_conv_g, v_ln_conv_b, v_w_conv_proj, v_w_attn_proj, v_w_mix_out, v_b_mix_out, v_ln1_g, v_ln1_b, v_w_ffn_up, v_ffn_conv_w, v_ffn_conv_b, v_w_ffn_down, v_ln2_g, v_ln2_b):
    weights = dict(zip(WEIGHTS, (w_ada, b_ada, w_in, b_in, conv_a_w, conv_a_b, ln_conv_g, ln_conv_b, w_conv_proj,
                                 w_attn_proj, w_mix_out, b_mix_out, ln1_g, ln1_b, w_ffn_up, ffn_conv_w, ffn_conv_b,
                                 w_ffn_down, ln2_g, ln2_b)))
    mom1 = dict(zip(WEIGHTS, (m_w_ada, m_b_ada, m_w_in, m_b_in, m_conv_a_w, m_conv_a_b, m_ln_conv_g, m_ln_conv_b,
                              m_w_conv_proj, m_w_attn_proj, m_w_mix_out, m_b_mix_out, m_ln1_g, m_ln1_b, m_w_ffn_up,
                              m_ffn_conv_w, m_ffn_conv_b, m_w_ffn_down, m_ln2_g, m_ln2_b)))
    mom2 = dict(zip(WEIGHTS, (v_w_ada, v_b_ada, v_w_in, v_b_in, v_conv_a_w, v_conv_a_b, v_ln_conv_g, v_ln_conv_b,
                              v_w_conv_proj, v_w_attn_proj, v_w_mix_out, v_b_mix_out, v_ln1_g, v_ln1_b, v_w_ffn_up,
                              v_ffn_conv_w, v_ffn_conv_b, v_w_ffn_down, v_ln2_g, v_ln2_b)))
    bsz, seq, d = x.shape
    layers = w_ada.shape[0]
    ch = conv_a_w.shape[2] * N_CHIPS
    width = w_attn_proj.shape[1]
    heads = width // HEAD_DIM
    d_ff = w_ffn_down.shape[1] * N_CHIPS
    dims = dict(bsz=bsz, d=d, ch=ch, heads=heads, alpha=(2.0 * layers) ** 0.25, f_off=2 * ch + 3 * width,
                ga_off=2 * ch + 3 * width, tcf=_pick(d_ff, (256, 128)))
    chip = 2 * lax.axis_index("x") + lax.axis_index("y")
    device = 2 * chip + lax.axis_index("c")
    ada_cols = w_ada.shape[2]

    c_act = _silu_rows(_all_gather8(c, name="gather_c").reshape(N_DEVICES * bsz, d), name="silu_c")
    b_ada_mine = lax.dynamic_slice_in_dim(b_ada, chip * ada_cols, ada_cols, axis=1)
    mod_cols = jnp.stack([_matmul(c_act, w_ada[layer], "nn", F32, bias=b_ada_mine[layer][None], name=f"ada_l{layer}")
                          for layer in range(layers)])
    mod_all = _all_gather8(mod_cols, name="gather_mod")
    mod_all = jnp.concatenate([mod_all[2 * k] for k in range(N_CHIPS)], axis=-1)
    mod = lax.dynamic_slice_in_dim(mod_all, device * bsz, bsz, axis=1)

    shards = [_pad_rows(weights[wname].astype(BF16) if as_bf16 else weights[wname]) for wname, _, as_bf16 in GATHERED]
    modes = [mode for _, mode, _ in GATHERED]
    whole = _gather_weights(shards, modes, name="gather_weights")
    full = {wname: w[:, :weights[wname].shape[1]] if mode == "cols" else w
            for (wname, mode, _), w in zip(GATHERED, whole)}
    full["w_in"] = _from_slab(full["w_in"])
    rep = {wname: weights[wname] for wname in REPLICATED}

    loss_local, grad_x, grads, dmod = _local_step(x, mod, loss_target, full, rep, dims)
    loss = lax.psum(loss_local, ("x", "y", "c"))

    grads["w_in"] = _to_slab(grads["w_in"])
    shard_shapes = [s.shape for s in shards]
    reduced = _reduce_scatter([_pad_rows(grads[wname]) for wname, _, _ in GATHERED], modes, shard_shapes)
    grad = {wname: r[:, :weights[wname].shape[1]] for (wname, _, _), r in zip(GATHERED, reduced)}

    small = jnp.concatenate([dmod.reshape(-1)] + [grads[wname].reshape(-1) for wname in REPLICATED])
    n_small = small.shape[0]
    rows = -(-n_small // (SUBLANES * LANES)) * SUBLANES
    small = jnp.pad(small, (0, rows * LANES - n_small)).reshape(rows, LANES)
    gathered = _all_gather8(small, name="gather_small")
    n_dmod = dmod.size
    dmod_all = gathered.reshape(N_DEVICES, -1)[:, :n_dmod].reshape(N_DEVICES, layers, bsz, 6 * d)
    dmod_all = jnp.transpose(dmod_all, (1, 0, 2, 3)).reshape(layers, N_DEVICES * bsz, 6 * d)
    summed = _sum_slots(gathered, name="sum_small").reshape(-1)
    off = n_dmod
    for wname in REPLICATED:
        n = weights[wname].size
        grad[wname] = summed[off:off + n].reshape(weights[wname].shape)
        off += n
    dmod_mine = lax.dynamic_slice_in_dim(dmod_all, chip * ada_cols, ada_cols, axis=2)
    grad["w_ada"] = jnp.stack([_matmul(c_act, dmod_mine[layer], "tn", F32, name=f"ada_dw_l{layer}")
                               for layer in range(layers)])
    grad["b_ada"] = jnp.stack([_colsum(dmod_all[layer], name=f"ada_db_l{layer}")[0] for layer in range(layers)])

    delta, new_m, new_v = {}, {}, {}
    for wname in WEIGHTS:
        delta[wname], new_m[wname], new_v[wname] = _adamw(weights[wname], grad[wname], mom1[wname], mom2[wname],
                                                          name=f"adamw_{wname}")
    return (loss, grad_x, *[grad[wname] for wname in WEIGHTS], *[delta[wname] for wname in WEIGHTS],
            *[new_m[wname] for wname in WEIGHTS], *[new_v[wname] for wname in WEIGHTS])
```

```python
import math

import jax
import jax.numpy as jnp
from jax import lax
from jax.experimental import pallas as pl
from jax.experimental.pallas import tpu as pltpu

F32 = jnp.float32
BF16 = jnp.bfloat16
MESH = pl.DeviceIdType.MESH

LN_EPS = 1e-5
HEAD_DIM = 64
ATTN_SCALE = HEAD_DIM ** -0.5
NEG = -1e30
CONV_A_HALO = 32
FFN_PAD = 8
LANES = 128
SUBLANES = 8
ROW_CHUNK = 256
ATTN_BLOCK = 256
N_CHIPS = 4
N_DEVICES = 8
VMEM_LIMIT = 56 * 1024 * 1024

ADAM_LR = 0.001
ADAM_B1 = 0.9
ADAM_B2 = 0.999
ADAM_EPS = 1e-08
ADAM_WD = 0.01
ADAM_STEP = 10

GATHERED = (("w_in", "slab", True), ("conv_a_w", "cols", False), ("w_conv_proj", "cols", True),
            ("w_attn_proj", "cols", True), ("w_mix_out", "rows", True), ("w_ffn_up", "cols", True),
            ("ffn_conv_w", "cols", False), ("w_ffn_down", "rows", True))
REPLICATED = ("b_in", "conv_a_b", "ln_conv_g", "ln_conv_b", "b_mix_out", "ln1_g", "ln1_b",
              "ffn_conv_b", "ln2_g", "ln2_b")
WEIGHTS = ("w_ada", "b_ada", "w_in", "b_in", "conv_a_w", "conv_a_b", "ln_conv_g", "ln_conv_b",
           "w_conv_proj", "w_attn_proj", "w_mix_out", "b_mix_out", "ln1_g", "ln1_b", "w_ffn_up",
           "ffn_conv_w", "ffn_conv_b", "w_ffn_down", "ln2_g", "ln2_b")


def _pick(n, cands):
    for cand in cands:
        if n % cand == 0:
            return cand
    return n


def _call(body, *, name, grid, in_specs, out_specs, out_shape, scratch=(), sem=None):
    return pl.pallas_call(
        body, name=name, grid=grid, in_specs=in_specs, out_specs=out_specs, out_shape=out_shape,
        scratch_shapes=list(scratch),
        compiler_params=pltpu.CompilerParams(dimension_semantics=sem, vmem_limit_bytes=VMEM_LIMIT))


def _sds(shape, dtype):
    return jax.ShapeDtypeStruct(tuple(shape), dtype)


def _chunked(rows, fn):
    chunk = min(ROW_CHUNK, rows)
    if rows == chunk:
        fn(pl.ds(0, rows))
        return

    def step(i, carry):
        fn(pl.ds(pl.multiple_of(i * chunk, chunk), chunk))
        return carry

    lax.fori_loop(0, rows // chunk, step, 0)


def _matmul(a, b, mode, out_dtype, *, bias=None, add=None, name):
    if mode == "nn":
        (m, k), (_, n) = a.shape, b.shape
    elif mode == "nt":
        (m, k), (n, _) = a.shape, b.shape
    else:
        (k, m), (_, n) = a.shape, b.shape
    tm = _pick(m, (1024, 1408, 512, 256, 128))
    tn = _pick(n, (512, 1408, 256, 128))
    tk = k if k <= 1536 else _pick(k, (1024, 1536, 1408, 512, 256, 128))
    nk = k // tk
    if mode == "nn":
        a_spec = pl.BlockSpec((tm, tk), lambda i, j, kk: (i, kk))
        b_spec = pl.BlockSpec((tk, tn), lambda i, j, kk: (kk, j))
        dims = (((1,), (0,)), ((), ()))
    elif mode == "nt":
        a_spec = pl.BlockSpec((tm, tk), lambda i, j, kk: (i, kk))
        b_spec = pl.BlockSpec((tn, tk), lambda i, j, kk: (j, kk))
        dims = (((1,), (1,)), ((), ()))
    else:
        a_spec = pl.BlockSpec((tk, tm), lambda i, j, kk: (kk, i))
        b_spec = pl.BlockSpec((tk, tn), lambda i, j, kk: (kk, j))
        dims = (((0,), (0,)), ((), ()))
    in_specs = [a_spec, b_spec]
    operands = [a, b]
    if bias is not None:
        in_specs.append(pl.BlockSpec((1, tn), lambda i, j, kk: (0, j)))
        operands.append(bias)
    if add is not None:
        in_specs.append(pl.BlockSpec((tm, tn), lambda i, j, kk: (i, j)))
        operands.append(add)

    def body(a_ref, b_ref, *rest):
        rest = list(rest)
        bias_ref = rest.pop(0) if bias is not None else None
        add_ref = rest.pop(0) if add is not None else None
        o_ref = rest.pop(0)
        prod = lax.dot_general(a_ref[...].astype(BF16), b_ref[...].astype(BF16), dims,
                               preferred_element_type=F32)

        def finish(r):
            if bias_ref is not None:
                r = r + bias_ref[...]
            if add_ref is not None:
                r = r + add_ref[...]
            o_ref[...] = r.astype(o_ref.dtype)

        if nk == 1:
            finish(prod)
            return
        acc_ref = rest.pop(0)
        kk = pl.program_id(2)

        @pl.when(kk == 0)
        def _():
            acc_ref[...] = prod

        @pl.when(kk > 0)
        def _():
            acc_ref[...] += prod

        @pl.when(kk == nk - 1)
        def _():
            finish(acc_ref[...])

    return _call(body, name=name, grid=(m // tm, n // tn, nk), in_specs=in_specs,
                 out_specs=pl.BlockSpec((tm, tn), lambda i, j, kk: (i, j)),
                 out_shape=_sds((m, n), out_dtype),
                 scratch=[pltpu.VMEM((tm, tn), F32)] if nk > 1 else [],
                 sem=("parallel", "parallel", "arbitrary"))(*operands)


def _colsum(x, *, name):
    rows, n = x.shape
    tr = _pick(rows, (1024, 512, 256, 128))
    tn = _pick(n, (512, 256, 128))

    def body(x_ref, o_ref):
        @pl.when(pl.program_id(1) == 0)
        def _():
            o_ref[...] = jnp.zeros_like(o_ref)

        o_ref[...] += jnp.sum(x_ref[...].astype(F32), axis=0, keepdims=True)

    return _call(body, name=name, grid=(n // tn, rows // tr),
                 in_specs=[pl.BlockSpec((tr, tn), lambda j, i: (i, j))],
                 out_specs=pl.BlockSpec((1, tn), lambda j, i: (0, j)),
                 out_shape=_sds((1, n), F32), sem=("parallel", "arbitrary"))(x)


def _ln_stats(x):
    mu = jnp.mean(x, axis=-1, keepdims=True)
    xc = x - mu
    var = jnp.mean(xc * xc, axis=-1, keepdims=True)
    rstd = lax.rsqrt(var + LN_EPS)
    return xc * rstd, rstd


def _ln_bwd(dn, n, rstd):
    return rstd * (dn - jnp.mean(dn, axis=-1, keepdims=True) - n * jnp.mean(dn * n, axis=-1, keepdims=True))


def _seq_tiles(t, bsz, cands=(1024, 512, 256, 128, 64, 32, 16, 8)):
    s = t // bsz
    ts = _pick(s, cands)
    return s, ts, s // ts


def _ln_mod_fwd(x, scale, shift, bsz, *, name):
    t, d = x.shape
    _, ts, ns = _seq_tiles(t, bsz)

    def body(x_ref, sc_ref, sh_ref, u_ref):
        one_scale = 1.0 + sc_ref[0]
        shift_v = sh_ref[0]

        def piece(rows):
            n, _ = _ln_stats(x_ref[rows, :])
            u_ref[rows, :] = (n * one_scale + shift_v).astype(u_ref.dtype)

        _chunked(ts, piece)

    row = pl.BlockSpec((ts, d), lambda b, i: (b * ns + i, 0))
    per = pl.BlockSpec((1, 1, d), lambda b, i: (b, 0, 0))
    return _call(body, name=name, grid=(bsz, ns), in_specs=[row, per, per], out_specs=row,
                 out_shape=_sds((t, d), BF16), sem=("parallel", "parallel"))(x, scale, shift)


def _ln_mod_bwd(du, x, scale, dr, alpha, bsz, *, name):
    t, d = x.shape
    _, ts, ns = _seq_tiles(t, bsz)

    def body(du_ref, x_ref, sc_ref, dr_ref, dx_ref, dsc_ref, dsh_ref):
        @pl.when(pl.program_id(1) == 0)
        def _():
            dsc_ref[...] = jnp.zeros_like(dsc_ref)
            dsh_ref[...] = jnp.zeros_like(dsh_ref)

        one_scale = 1.0 + sc_ref[0]

        def piece(rows):
            du_v = du_ref[rows, :]
            n, rstd = _ln_stats(x_ref[rows, :])
            dsc_ref[0] += jnp.sum(du_v * n, axis=0, keepdims=True)
            dsh_ref[0] += jnp.sum(du_v, axis=0, keepdims=True)
            dx_ref[rows, :] = alpha * dr_ref[rows, :] + _ln_bwd(du_v * one_scale, n, rstd)

        _chunked(ts, piece)

    row = pl.BlockSpec((ts, d), lambda b, i: (b * ns + i, 0))
    per = pl.BlockSpec((1, 1, d), lambda b, i: (b, 0, 0))
    return _call(body, name=name, grid=(bsz, ns), in_specs=[row, row, per, row],
                 out_specs=[row, per, per],
                 out_shape=[_sds((t, d), F32), _sds((bsz, 1, d), F32), _sds((bsz, 1, d), F32)],
                 sem=("parallel", "arbitrary"))(du, x, scale, dr)


def _ln_res_fwd(x, y, gate, g, b, alpha, bsz, *, name):
    t, d = x.shape
    _, ts, ns = _seq_tiles(t, bsz)

    def body(x_ref, y_ref, gt_ref, g_ref, b_ref, o_ref):
        one_gate = 1.0 + gt_ref[0]

        def piece(rows):
            n, _ = _ln_stats(alpha * x_ref[rows, :] + one_gate * y_ref[rows, :])
            o_ref[rows, :] = n * g_ref[...] + b_ref[...]

        _chunked(ts, piece)

    row = pl.BlockSpec((ts, d), lambda bb, i: (bb * ns + i, 0))
    per = pl.BlockSpec((1, 1, d), lambda bb, i: (bb, 0, 0))
    vec = pl.BlockSpec((1, d), lambda bb, i: (0, 0))
    return _call(body, name=name, grid=(bsz, ns), in_specs=[row, row, per, vec, vec], out_specs=row,
                 out_shape=_sds((t, d), F32), sem=("parallel", "parallel"))(x, y, gate, g, b)


def _ln_res_bwd(do, x, y, gate, g, alpha, bsz, *, name):
    t, d = x.shape
    _, ts, ns = _seq_tiles(t, bsz)

    def body(do_ref, x_ref, y_ref, gt_ref, g_ref, dr_ref, dy_ref, dgt_ref, dg_ref, db_ref, dys_ref):
        first_tile = pl.program_id(1) == 0

        @pl.when(first_tile)
        def _():
            dgt_ref[...] = jnp.zeros_like(dgt_ref)

        @pl.when(jnp.logical_and(first_tile, pl.program_id(0) == 0))
        def _():
            dg_ref[...] = jnp.zeros_like(dg_ref)
            db_ref[...] = jnp.zeros_like(db_ref)
            dys_ref[...] = jnp.zeros_like(dys_ref)

        one_gate = 1.0 + gt_ref[0]

        def piece(rows):
            do_v = do_ref[rows, :]
            y_v = y_ref[rows, :]
            n, rstd = _ln_stats(alpha * x_ref[rows, :] + one_gate * y_v)
            dg_ref[...] += jnp.sum(do_v * n, axis=0, keepdims=True)
            db_ref[...] += jnp.sum(do_v, axis=0, keepdims=True)
            dr = _ln_bwd(do_v * g_ref[...], n, rstd)
            dr_ref[rows, :] = dr
            dy = one_gate * dr
            dy_ref[rows, :] = dy.astype(dy_ref.dtype)
            dys_ref[...] += jnp.sum(dy, axis=0, keepdims=True)
            dgt_ref[0] += jnp.sum(dr * y_v, axis=0, keepdims=True)

        _chunked(ts, piece)

    row = pl.BlockSpec((ts, d), lambda bb, i: (bb * ns + i, 0))
    per = pl.BlockSpec((1, 1, d), lambda bb, i: (bb, 0, 0))
    vec = pl.BlockSpec((1, d), lambda bb, i: (0, 0))
    return _call(body, name=name, grid=(bsz, ns), in_specs=[row, row, row, per, vec],
                 out_specs=[row, row, per, vec, vec, vec],
                 out_shape=[_sds((t, d), F32), _sds((t, d), BF16), _sds((bsz, 1, d), F32),
                            _sds((1, d), F32), _sds((1, d), F32), _sds((1, d), F32)],
                 sem=("arbitrary", "arbitrary"))(do, x, y, gate, g)


def _loss_head(y, target, *, name):
    t, d = y.shape
    tr = _pick(t, (1024, 512, 256, 128, 64, 32, 16, 8))

    def body(y_ref, t_ref, dy_ref, s_ref):
        @pl.when(pl.program_id(0) == 0)
        def _():
            s_ref[...] = jnp.zeros_like(s_ref)

        def piece(rows):
            e = y_ref[rows, :] - t_ref[rows, :]
            dy_ref[rows, :] = e * (1.0 / d)
            s_ref[...] += jnp.sum(e * e, axis=0, keepdims=True)

        _chunked(tr, piece)

    row = pl.BlockSpec((tr, d), lambda i: (i, 0))
    return _call(body, name=name, grid=(t // tr,), in_specs=[row, row],
                 out_specs=[row, pl.BlockSpec((1, d), lambda i: (0, 0))],
                 out_shape=[_sds((t, d), F32), _sds((1, d), F32)], sem=("arbitrary",))(y, target)


def _sigmoid(v):
    return 1.0 / (1.0 + jnp.exp(-v))


def _silu_rows(c, *, name):
    rows, d = c.shape

    def body(c_ref, o_ref):
        v = c_ref[...]
        o_ref[...] = (v * _sigmoid(v)).astype(o_ref.dtype)

    full = pl.BlockSpec((rows, d), lambda i: (0, 0))
    return _call(body, name=name, grid=(1,), in_specs=[full], out_specs=full,
                 out_shape=_sds((rows, d), BF16), sem=("arbitrary",))(c)


def _glu_fwd(z, ch, *, name):
    t = z.shape[0]
    tr = _pick(t, (1024, 512, 256, 128, 64, 32, 16, 8))

    def body(z_ref, o_ref):
        def piece(rows):
            o_ref[rows, :] = z_ref[rows, :ch] * _sigmoid(z_ref[rows, ch:])

        _chunked(tr, piece)

    return _call(body, name=name, grid=(t // tr,),
                 in_specs=[pl.BlockSpec((tr, 2 * ch), lambda i: (i, 0))],
                 out_specs=pl.BlockSpec((tr, ch), lambda i: (i, 0)),
                 out_shape=_sds((t, ch), F32), sem=("parallel",))(z)


def _glu_bwd(z, da0, ch, *, name):
    t = z.shape[0]
    tr = _pick(t, (1024, 512, 256, 128, 64, 32, 16, 8))

    def body(z_ref, d_ref, o_ref):
        def piece(rows):
            s = _sigmoid(z_ref[rows, ch:])
            d = d_ref[rows, :]
            o_ref[rows, :ch] = (d * s).astype(o_ref.dtype)
            o_ref[rows, ch:] = (d * z_ref[rows, :ch] * s * (1.0 - s)).astype(o_ref.dtype)

        _chunked(tr, piece)

    return _call(body, name=name, grid=(t // tr,),
                 in_specs=[pl.BlockSpec((tr, 2 * ch), lambda i: (i, 0)),
                           pl.BlockSpec((tr, ch), lambda i: (i, 0))],
                 out_specs=pl.BlockSpec((tr, 2 * ch), lambda i: (i, 0)),
                 out_shape=_sds((t, 2 * ch), BF16), sem=("parallel",))(z, da0)


def _lnsilu_fwd(a1, g, b, *, name):
    t, ch = a1.shape
    tr = _pick(t, (1024, 512, 256, 128, 64, 32, 16, 8))

    def body(a_ref, g_ref, b_ref, o_ref):
        def piece(rows):
            n, _ = _ln_stats(a_ref[rows, :])
            a2 = n * g_ref[...] + b_ref[...]
            o_ref[rows, :] = (a2 * _sigmoid(a2)).astype(o_ref.dtype)

        _chunked(tr, piece)

    row = pl.BlockSpec((tr, ch), lambda i: (i, 0))
    vec = pl.BlockSpec((1, ch), lambda i: (0, 0))
    return _call(body, name=name, grid=(t // tr,), in_specs=[row, vec, vec], out_specs=row,
                 out_shape=_sds((t, ch), BF16), sem=("parallel",))(a1, g, b)


def _lnsilu_bwd(a1, da3, g, b, *, name):
    t, ch = a1.shape
    tr = _pick(t, (1024, 512, 256, 128, 64, 32, 16, 8))

    def body(a_ref, d_ref, g_ref, b_ref, o_ref, dg_ref, db_ref):
        @pl.when(pl.program_id(0) == 0)
        def _():
            dg_ref[...] = jnp.zeros_like(dg_ref)
            db_ref[...] = jnp.zeros_like(db_ref)

        def piece(rows):
            n, rstd = _ln_stats(a_ref[rows, :])
            a2 = n * g_ref[...] + b_ref[...]
            s = _sigmoid(a2)
            da2 = d_ref[rows, :] * (s * (1.0 + a2 * (1.0 - s)))
            dg_ref[...] += jnp.sum(da2 * n, axis=0, keepdims=True)
            db_ref[...] += jnp.sum(da2, axis=0, keepdims=True)
            o_ref[rows, :] = _ln_bwd(da2 * g_ref[...], n, rstd)

        _chunked(tr, piece)

    row = pl.BlockSpec((tr, ch), lambda i: (i, 0))
    vec = pl.BlockSpec((1, ch), lambda i: (0, 0))
    return _call(body, name=name, grid=(t // tr,), in_specs=[row, row, vec, vec],
                 out_specs=[row, vec, vec],
                 out_shape=[_sds((t, ch), F32), _sds((1, ch), F32), _sds((1, ch), F32)],
                 sem=("arbitrary",))(a1, da3, g, b)


def _gate_cols(d, ga_off):
    tc = _pick(math.gcd(d, ga_off), (512, 256, 128))
    return tc, ga_off // tc, (ga_off + d) // tc


def _gate_merge_fwd(z, ya, yb, ga_off, *, name):
    t, d = ya.shape
    tr = _pick(t, (1024, 512, 256, 128, 64, 32, 16, 8))
    tc, ga_blk, gb_blk = _gate_cols(d, ga_off)

    def body(ga_ref, gb_ref, ya_ref, yb_ref, o_ref):
        def piece(rows):
            o_ref[rows, :] = (_sigmoid(ga_ref[rows, :]) * ya_ref[rows, :]
                              + _sigmoid(gb_ref[rows, :]) * yb_ref[rows, :]).astype(o_ref.dtype)

        _chunked(tr, piece)

    blk = pl.BlockSpec((tr, tc), lambda i, j: (i, j))
    return _call(body, name=name, grid=(t // tr, d // tc),
                 in_specs=[pl.BlockSpec((tr, tc), lambda i, j: (i, ga_blk + j)),
                           pl.BlockSpec((tr, tc), lambda i, j: (i, gb_blk + j)), blk, blk],
                 out_specs=blk, out_shape=_sds((t, d), BF16), sem=("parallel", "parallel"))(z, z, ya, yb)


def _gate_merge_bwd(z, ya, yb, dm, ga_off, *, name):
    t, d = ya.shape
    tr = _pick(t, (1024, 512, 256, 128, 64, 32, 16, 8))
    tc, ga_blk, gb_blk = _gate_cols(d, ga_off)

    def body(ga_ref, gb_ref, ya_ref, yb_ref, dm_ref, dya_ref, dyb_ref, dga_ref, dgb_ref):
        def piece(rows):
            dm_v = dm_ref[rows, :]
            sa = _sigmoid(ga_ref[rows, :])
            sb = _sigmoid(gb_ref[rows, :])
            dya_ref[rows, :] = (dm_v * sa).astype(dya_ref.dtype)
            dyb_ref[rows, :] = (dm_v * sb).astype(dyb_ref.dtype)
            dga_ref[rows, :] = (dm_v * ya_ref[rows, :] * sa * (1.0 - sa)).astype(dga_ref.dtype)
            dgb_ref[rows, :] = (dm_v * yb_ref[rows, :] * sb * (1.0 - sb)).astype(dgb_ref.dtype)

        _chunked(tr, piece)

    blk = pl.BlockSpec((tr, tc), lambda i, j: (i, j))
    return _call(body, name=name, grid=(t // tr, d // tc),
                 in_specs=[pl.BlockSpec((tr, tc), lambda i, j: (i, ga_blk + j)),
                           pl.BlockSpec((tr, tc), lambda i, j: (i, gb_blk + j)), blk, blk, blk],
                 out_specs=[blk, blk, blk, blk], out_shape=[_sds((t, d), BF16)] * 4,
                 sem=("parallel", "parallel"))(z, z, ya, yb, dm)


def _dwconv_fwd(x, w, b, bsz, halo, *, name):
    t, ch = x.shape
    kw = w.shape[0]
    _, ts, ns = _seq_tiles(t, bsz, (256, 128, 64, 32))
    tc = _pick(ch, (512, 256, 128))
    hb = ts // halo

    def body(x_ref, h_ref, w_ref, b_ref, y_ref, ext_ref):
        ext_ref[pl.ds(0, halo), :] = jnp.where(pl.program_id(1) > 0, h_ref[...], 0.0)
        ext_ref[pl.ds(halo, ts), :] = x_ref[...]
        acc = jnp.zeros((ts, tc), F32) + b_ref[...]
        for k in range(kw):
            acc = acc + w_ref[pl.ds(k, 1), :] * ext_ref[pl.ds(halo - (kw - 1) + k, ts), :]
        y_ref[...] = acc

    cur = pl.BlockSpec((ts, tc), lambda bb, i, j: (bb * ns + i, j))
    prev = pl.BlockSpec((halo, tc), lambda bb, i, j: (jnp.maximum((bb * ns + i) * hb - 1, 0), j))
    return _call(body, name=name, grid=(bsz, ns, ch // tc),
                 in_specs=[cur, prev, pl.BlockSpec((kw, tc), lambda bb, i, j: (0, j)),
                           pl.BlockSpec((1, tc), lambda bb, i, j: (0, j))],
                 out_specs=cur, out_shape=_sds((t, ch), F32), scratch=[pltpu.VMEM((halo + ts, tc), F32)],
                 sem=("parallel", "parallel", "parallel"))(x, x, w, b)


def _dwconv_bwd(x, dy, w, bsz, halo, dx_dtype, *, name):
    t, ch = x.shape
    kw = w.shape[0]
    _, ts, ns = _seq_tiles(t, bsz, (256, 128, 64, 32))
    tc = _pick(ch, (512, 256, 128))
    hb = ts // halo
    last_halo_blk = t // halo - 1

    def body(x_ref, xh_ref, dy_ref, dyh_ref, w_ref, dx_ref, dw_ref, db_ref, extx_ref, exty_ref):
        i = pl.program_id(2)

        @pl.when(jnp.logical_and(pl.program_id(1) == 0, i == 0))
        def _():
            dw_ref[...] = jnp.zeros_like(dw_ref)
            db_ref[...] = jnp.zeros_like(db_ref)

        extx_ref[pl.ds(0, halo), :] = jnp.where(i > 0, xh_ref[...], 0.0)
        extx_ref[pl.ds(halo, ts), :] = x_ref[...]
        dyc = dy_ref[...]
        exty_ref[pl.ds(0, ts), :] = dyc
        exty_ref[pl.ds(ts, halo), :] = jnp.where(i < ns - 1, dyh_ref[...], 0.0)
        acc = jnp.zeros((ts, tc), F32)
        for k in range(kw):
            acc = acc + w_ref[pl.ds(k, 1), :] * exty_ref[pl.ds(kw - 1 - k, ts), :]
            dw_ref[pl.ds(k, 1), :] += jnp.sum(dyc * extx_ref[pl.ds(halo - (kw - 1) + k, ts), :],
                                              axis=0, keepdims=True)
        dx_ref[...] = acc.astype(dx_ref.dtype)
        db_ref[...] += jnp.sum(dyc, axis=0, keepdims=True)

    cur = pl.BlockSpec((ts, tc), lambda j, bb, i: (bb * ns + i, j))
    prev = pl.BlockSpec((halo, tc), lambda j, bb, i: (jnp.maximum((bb * ns + i) * hb - 1, 0), j))
    nxt = pl.BlockSpec((halo, tc), lambda j, bb, i: (jnp.minimum((bb * ns + i + 1) * hb, last_halo_blk), j))
    return _call(body, name=name, grid=(ch // tc, bsz, ns),
                 in_specs=[cur, prev, cur, nxt, pl.BlockSpec((kw, tc), lambda j, bb, i: (0, j))],
                 out_specs=[cur, pl.BlockSpec((kw, tc), lambda j, bb, i: (0, j)),
                            pl.BlockSpec((1, tc), lambda j, bb, i: (0, j))],
                 out_shape=[_sds((t, ch), dx_dtype), _sds((kw, ch), F32), _sds((1, ch), F32)],
                 scratch=[pltpu.VMEM((halo + ts, tc), F32), pltpu.VMEM((ts + halo, tc), F32)],
                 sem=("parallel", "arbitrary", "arbitrary"))(x, x, dy, dy, w)


FFN_ROWS = 64


def _gelu_parts(v):
    cdf = 0.5 * (1.0 + lax.erf(v * (2.0 ** -0.5)))
    return cdf, v * cdf


def _ffn_conv_piece(ext_ref, w_ref, b_ref, base):
    win = ext_ref[pl.ds(base, FFN_ROWS + FFN_PAD), :]
    acc = b_ref[...] + w_ref[pl.ds(2, 1), :] * win[FFN_PAD:]
    acc = acc + w_ref[pl.ds(1, 1), :] * pltpu.roll(win, 1, axis=0)[FFN_PAD:]
    acc = acc + w_ref[pl.ds(0, 1), :] * pltpu.roll(win, 2, axis=0)[FFN_PAD:]
    return acc, win


def _ffn_act_fwd(hp, w, b, bsz, tcf, *, name):
    t, two_f = hp.shape
    s = t // bsz
    gw = 2 * tcf

    def body(h_ref, w_ref, b_ref, f_ref, ext_ref):
        ext_ref[pl.ds(0, FFN_PAD), :] = jnp.zeros((FFN_PAD, gw), F32)
        ext_ref[pl.ds(FFN_PAD, s), :] = h_ref[...].astype(F32)

        def step(i, carry):
            base = pl.multiple_of(i * FFN_ROWS, FFN_ROWS)
            hh, _ = _ffn_conv_piece(ext_ref, w_ref, b_ref, base)
            _, gelu = _gelu_parts(hh[:, :tcf])
            f_ref[pl.ds(base, FFN_ROWS), :] = (gelu * hh[:, tcf:]).astype(f_ref.dtype)
            return carry

        lax.fori_loop(0, s // FFN_ROWS, step, 0)

    return _call(body, name=name, grid=(bsz, two_f // gw),
                 in_specs=[pl.BlockSpec((s, gw), lambda bb, j: (bb, j)),
                           pl.BlockSpec((3, gw), lambda bb, j: (0, j)),
                           pl.BlockSpec((1, gw), lambda bb, j: (0, j))],
                 out_specs=pl.BlockSpec((s, tcf), lambda bb, j: (bb, j)),
                 out_shape=_sds((t, two_f // 2), BF16), scratch=[pltpu.VMEM((FFN_PAD + s, gw), F32)],
                 sem=("parallel", "parallel"))(hp, w, b)


def _ffn_act_bwd(hp, df, w, b, bsz, tcf, *, name):
    t, two_f = hp.shape
    s = t // bsz
    gw = 2 * tcf
    n_rows = FFN_ROWS + FFN_PAD

    def body(h_ref, df_ref, w_ref, b_ref, dhp_ref, dw_ref, db_ref, ext_ref, dh_ref):
        @pl.when(pl.program_id(1) == 0)
        def _():
            dw_ref[...] = jnp.zeros_like(dw_ref)
            db_ref[...] = jnp.zeros_like(db_ref)

        ext_ref[pl.ds(0, FFN_PAD), :] = jnp.zeros((FFN_PAD, gw), F32)
        ext_ref[pl.ds(FFN_PAD, s), :] = h_ref[...].astype(F32)
        dh_ref[pl.ds(s, FFN_PAD), :] = jnp.zeros((FFN_PAD, gw), F32)

        def grad_h(i, carry):
            base = pl.multiple_of(i * FFN_ROWS, FFN_ROWS)
            hh, _ = _ffn_conv_piece(ext_ref, w_ref, b_ref, base)
            hg = hh[:, :tcf]
            d = df_ref[pl.ds(base, FFN_ROWS), :]
            cdf, gelu = _gelu_parts(hg)
            pdf = jnp.exp(-0.5 * hg * hg) * (1.0 / math.sqrt(2.0 * math.pi))
            dh_ref[pl.ds(base, FFN_ROWS), :tcf] = d * hh[:, tcf:] * (cdf + hg * pdf)
            dh_ref[pl.ds(base, FFN_ROWS), tcf:] = d * gelu
            return carry

        lax.fori_loop(0, s // FFN_ROWS, grad_h, 0)

        def grad_x(i, carry):
            dw0, dw1, dw2, dbs = carry
            base = pl.multiple_of(i * FFN_ROWS, FFN_ROWS)
            nxt = dh_ref[pl.ds(base, n_rows), :]
            dyc = nxt[:FFN_ROWS]
            dx = w_ref[pl.ds(2, 1), :] * dyc
            dx = dx + w_ref[pl.ds(1, 1), :] * pltpu.roll(nxt, n_rows - 1, axis=0)[:FFN_ROWS]
            dx = dx + w_ref[pl.ds(0, 1), :] * pltpu.roll(nxt, n_rows - 2, axis=0)[:FFN_ROWS]
            dhp_ref[pl.ds(base, FFN_ROWS), :] = dx.astype(dhp_ref.dtype)
            win = ext_ref[pl.ds(base, n_rows), :]
            dw2 = dw2 + jnp.sum(dyc * win[FFN_PAD:], axis=0, keepdims=True)
            dw1 = dw1 + jnp.sum(dyc * pltpu.roll(win, 1, axis=0)[FFN_PAD:], axis=0, keepdims=True)
            dw0 = dw0 + jnp.sum(dyc * pltpu.roll(win, 2, axis=0)[FFN_PAD:], axis=0, keepdims=True)
            return dw0, dw1, dw2, dbs + jnp.sum(dyc, axis=0, keepdims=True)

        zero = jnp.zeros((1, gw), F32)
        dw0, dw1, dw2, dbs = lax.fori_loop(0, s // FFN_ROWS, grad_x, (zero, zero, zero, zero))
        dw_ref[pl.ds(0, 1), :] += dw0
        dw_ref[pl.ds(1, 1), :] += dw1
        dw_ref[pl.ds(2, 1), :] += dw2
        db_ref[...] += dbs

    grp = pl.BlockSpec((s, gw), lambda j, bb: (bb, j))
    return _call(body, name=name, grid=(two_f // gw, bsz),
                 in_specs=[grp, pl.BlockSpec((s, tcf), lambda j, bb: (bb, j)),
                           pl.BlockSpec((3, gw), lambda j, bb: (0, j)),
                           pl.BlockSpec((1, gw), lambda j, bb: (0, j))],
                 out_specs=[grp, pl.BlockSpec((3, gw), lambda j, bb: (0, j)),
                            pl.BlockSpec((1, gw), lambda j, bb: (0, j))],
                 out_shape=[_sds((t, two_f), BF16), _sds((3, two_f), F32), _sds((1, two_f), F32)],
                 scratch=[pltpu.VMEM((FFN_PAD + s, gw), F32), pltpu.VMEM((s + FFN_PAD, gw), F32)],
                 sem=("parallel", "arbitrary"))(hp, df, w, b)


def _split3(v):
    hi = v.astype(BF16)
    r = v - hi.astype(F32)
    mid = r.astype(BF16)
    lo = (r - mid.astype(F32)).astype(BF16)
    return hi, mid, lo


def _tri_dot(tri, v):
    out = None
    for part in _split3(v):
        term = jnp.dot(tri, part, preferred_element_type=F32)
        out = term if out is None else out + term
    return out


def _fgate_fwd(zf, bsz, heads, *, name):
    t, lanes = zf.shape
    s, blk, nb = _seq_tiles(t, bsz, (ATTN_BLOCK, 128))

    def body(z_ref, cumt_ref, cumb_ref, carry_ref):
        @pl.when(pl.program_id(1) == 0)
        def _():
            carry_ref[...] = jnp.zeros_like(carry_ref)

        z = z_ref[...]
        lf = jnp.minimum(z, 0.0) - jnp.log1p(jnp.exp(-jnp.abs(z)))
        r = lax.broadcasted_iota(jnp.int32, (blk, blk), 0)
        c = lax.broadcasted_iota(jnp.int32, (blk, blk), 1)
        tri = (r >= c).astype(BF16)
        cum = _tri_dot(tri, lf) + carry_ref[...]
        carry_ref[...] = cum[blk - 1:blk, :]
        cumt_ref[0] = jnp.transpose(cum)[:heads, :]
        for h in range(heads):
            cumb_ref[0, h] = jnp.broadcast_to(cum[:, h:h + 1], (blk, lanes))

    return _call(body, name=name, grid=(bsz, nb),
                 in_specs=[pl.BlockSpec((blk, lanes), lambda b, i: (b * nb + i, 0))],
                 out_specs=[pl.BlockSpec((1, heads, blk), lambda b, i: (b, 0, i)),
                            pl.BlockSpec((1, heads, blk, lanes), lambda b, i: (b, 0, i, 0))],
                 out_shape=[_sds((bsz, heads, s), F32), _sds((bsz, heads, s, lanes), F32)],
                 scratch=[pltpu.VMEM((1, lanes), F32)], sem=("parallel", "arbitrary"))(zf)


def _fgate_bwd(dcum, zf, bsz, *, name):
    t, lanes = zf.shape
    pairs = dcum.shape[1]
    s, blk, nb = _seq_tiles(t, bsz, (ATTN_BLOCK, 128))

    def body(d_ref, z_ref, o_ref, carry_ref):
        @pl.when(pl.program_id(1) == 0)
        def _():
            carry_ref[...] = jnp.zeros_like(carry_ref)

        dcol = d_ref[0, 0]
        for p in range(1, pairs):
            dcol = dcol + d_ref[0, p]
        r = lax.broadcasted_iota(jnp.int32, (blk, blk), 0)
        c = lax.broadcasted_iota(jnp.int32, (blk, blk), 1)
        tri = (c >= r).astype(BF16)
        suf = _tri_dot(tri, dcol) + carry_ref[...]
        carry_ref[...] = suf[0:1, :]
        o_ref[...] = suf * _sigmoid(-z_ref[...])

    return _call(body, name=name, grid=(bsz, nb),
                 in_specs=[pl.BlockSpec((1, pairs, blk, lanes), lambda b, i: (b, 0, nb - 1 - i, 0)),
                           pl.BlockSpec((blk, lanes), lambda b, i: (b * nb + nb - 1 - i, 0))],
                 out_specs=pl.BlockSpec((blk, lanes), lambda b, i: (b * nb + nb - 1 - i, 0)),
                 out_shape=_sds((t, lanes), F32), scratch=[pltpu.VMEM((1, lanes), F32)],
                 sem=("parallel", "arbitrary"))(dcum, zf)


def _to_features_major(z, col_off, width, n, *, name):
    t = z.shape[0]
    tr = _pick(t, (512, 256, 128))
    first = col_off // width

    def body(*refs):
        o_ref = refs[n]
        for g in range(n):
            o_ref[pl.ds(g * width, width), :] = jnp.transpose(refs[g][...].astype(F32)).astype(o_ref.dtype)

    return _call(body, name=name, grid=(t // tr,),
                 in_specs=[pl.BlockSpec((tr, width), lambda i, g=g: (i, first + g)) for g in range(n)],
                 out_specs=pl.BlockSpec((n * width, tr), lambda i: (0, i)),
                 out_shape=_sds((n * width, t), BF16), sem=("parallel",))(*([z] * n))


def _to_rows_major(xt, *, name):
    w, t = xt.shape
    tr = _pick(t, (512, 256, 128))

    def body(x_ref, o_ref):
        o_ref[...] = jnp.transpose(x_ref[...]).astype(o_ref.dtype)

    return _call(body, name=name, grid=(t // tr,),
                 in_specs=[pl.BlockSpec((w, tr), lambda i: (0, i))],
                 out_specs=pl.BlockSpec((tr, w), lambda i: (i, 0)),
                 out_shape=_sds((t, w), BF16), sem=("parallel",))(xt)


def _head_masks(shape, axis):
    feat = lax.broadcasted_iota(jnp.int32, shape, axis)
    return feat < HEAD_DIM, feat >= HEAD_DIM


def _attn_fwd(z, qkvt, cumt, cumb, bsz, heads, q_off, *, name):
    t = z.shape[0]
    width = heads * HEAD_DIM
    pairs = heads // 2
    s = t // bsz
    blk = ATTN_BLOCK
    nq = s // blk
    k_col = (q_off + width) // LANES
    v_row = 2 * width // LANES
    reps = blk // LANES

    def body(k_ref, qt_ref, vt_ref, cqt_ref, ckb_ref, ot_ref, lse_ref):
        p_id = pl.program_id(1)
        i = pl.program_id(2)
        qt = qt_ref[...]
        masks = _head_masks((LANES, blk), 0)
        qtm = [jnp.where(mk, qt, jnp.zeros_like(qt)) for mk in masks]
        cq = [cqt_ref[0, pl.ds(2 * p_id + hh, 1), :] for hh in range(2)]
        kidx = lax.broadcasted_iota(jnp.int32, (blk, blk), 0)
        qidx = lax.broadcasted_iota(jnp.int32, (blk, blk), 1)

        def block(j, carry, masked):
            off = pl.multiple_of(j * blk, blk)
            kp = k_ref[pl.ds(off, blk), :].astype(BF16)
            vtp = vt_ref[:, pl.ds(off, blk)]
            out = []
            for hh in range(2):
                m, l, acc = carry[hh]
                sc = jnp.dot(kp, qtm[hh], preferred_element_type=F32) * ATTN_SCALE
                ck = ckb_ref[0, hh, pl.ds(off, blk), :]
                sc = (sc + cq[hh]) - jnp.concatenate([ck] * reps, axis=1)
                if masked:
                    sc = jnp.where(qidx >= kidx, sc, NEG)
                m_new = jnp.maximum(m, jnp.max(sc, axis=0, keepdims=True))
                pr = jnp.exp(sc - m_new)
                a = jnp.exp(m - m_new)
                l = a * l + jnp.sum(pr, axis=0, keepdims=True)
                p_hi = pr.astype(BF16)
                p_lo = (pr - p_hi.astype(F32)).astype(BF16)
                pv = (jnp.dot(vtp, p_hi, preferred_element_type=F32)
                      + jnp.dot(vtp, p_lo, preferred_element_type=F32))
                acc = a * acc + pv[hh * HEAD_DIM:(hh + 1) * HEAD_DIM]
                out.append((m_new, l, acc))
            return tuple(out)

        init = tuple((jnp.full((1, blk), NEG, F32), jnp.zeros((1, blk), F32), jnp.zeros((HEAD_DIM, blk), F32))
                     for _ in range(2))
        carry = lax.fori_loop(0, i, lambda j, cr: block(j, cr, False), init)
        carry = block(i, carry, True)
        lse_ref[...] = jnp.zeros_like(lse_ref)
        for hh in range(2):
            m, l, acc = carry[hh]
            ot_ref[pl.ds(hh * HEAD_DIM, HEAD_DIM), :] = acc / l
            lse_ref[0, 0, pl.ds(hh, 1), :] = m + jnp.log(l)

    return _call(body, name=name, grid=(bsz, pairs, nq),
                 in_specs=[pl.BlockSpec((s, LANES), lambda b, p, i: (b, k_col + p)),
                           pl.BlockSpec((LANES, blk), lambda b, p, i: (p, b * nq + i)),
                           pl.BlockSpec((LANES, s), lambda b, p, i: (v_row + p, b)),
                           pl.BlockSpec((1, heads, blk), lambda b, p, i: (b, 0, i)),
                           pl.BlockSpec((1, 2, s, LANES), lambda b, p, i: (b, p, 0, 0))],
                 out_specs=[pl.BlockSpec((LANES, blk), lambda b, p, i: (p, b * nq + i)),
                            pl.BlockSpec((1, 1, SUBLANES, blk), lambda b, p, i: (b, p, 0, i))],
                 out_shape=[_sds((width, t), F32), _sds((bsz, pairs, SUBLANES, s), F32)],
                 sem=("parallel", "parallel", "parallel"))(z, qkvt, qkvt, cumt, cumb)


def _attn_bwd(z, qkvt, cumt, cumb, ot, do, dot, lse, bsz, heads, q_off, *, name):
    t = z.shape[0]
    width = heads * HEAD_DIM
    pairs = heads // 2
    s = t // bsz
    blk = ATTN_BLOCK
    nkv = s // blk
    q_col = q_off // LANES
    k_col = (q_off + width) // LANES
    v_col = (q_off + 2 * width) // LANES
    k_row = width // LANES
    reps = blk // LANES

    def body(k_ref, v_ref, kt_ref, q_ref, qt_ref, do_ref, dot_ref, ot_ref, lse_ref, ckb_ref, cqt_ref,
             dk_ref, dv_ref, dqt_ref, dcum_ref, dqt_acc, ds_acc):
        p_id = pl.program_id(1)
        j = pl.program_id(2)

        @pl.when(j == 0)
        def _():
            dqt_acc[...] = jnp.zeros_like(dqt_acc)

        kp = k_ref[...].astype(BF16)
        vp = v_ref[...].astype(BF16)
        kt = kt_ref[...]
        feat_masks = _head_masks((LANES, blk), 0)
        lane_masks = _head_masks((blk, LANES), 1)
        ktm = [jnp.where(mk, kt, jnp.zeros_like(kt)) for mk in feat_masks]
        ck = [jnp.concatenate([ckb_ref[0, hh]] * reps, axis=1) for hh in range(2)]
        kidx = lax.broadcasted_iota(jnp.int32, (blk, blk), 0)
        qidx = lax.broadcasted_iota(jnp.int32, (blk, blk), 1)
        ds_acc[...] = jnp.zeros_like(ds_acc)

        def block(i, carry, masked):
            dk, dv = carry
            off = pl.multiple_of(i * blk, blk)
            qt = qt_ref[:, pl.ds(off, blk)]
            dt = dot_ref[:, pl.ds(off, blk)]
            o_t = ot_ref[:, pl.ds(off, blk)]
            q_rows = q_ref[pl.ds(off, blk), :].astype(BF16)
            do_rows = do_ref[pl.ds(off, blk), :]
            for hh in range(2):
                qtm = jnp.where(feat_masks[hh], qt, jnp.zeros_like(qt))
                dtm = jnp.where(feat_masks[hh], dt, jnp.zeros_like(dt))
                sc = jnp.dot(kp, qtm, preferred_element_type=F32) * ATTN_SCALE
                sc = (sc + cqt_ref[0, pl.ds(2 * p_id + hh, 1), pl.ds(off, blk)]) - ck[hh]
                pr = jnp.exp(sc - lse_ref[0, 0, pl.ds(hh, 1), pl.ds(off, blk)])
                if masked:
                    pr = jnp.where(qidx >= kidx, pr, 0.0)
                dp = jnp.dot(vp, dtm, preferred_element_type=F32)
                delta = jnp.sum(dtm.astype(F32) * o_t, axis=0, keepdims=True)
                ds = pr * (dp - delta)
                ds_acc[hh] += ds
                dsb = ds.astype(BF16)
                qm = jnp.where(lane_masks[hh], q_rows, jnp.zeros_like(q_rows))
                dom = jnp.where(lane_masks[hh], do_rows, jnp.zeros_like(do_rows))
                dv = dv + jnp.dot(pr.astype(BF16), dom, preferred_element_type=F32)
                dk = dk + jnp.dot(dsb, qm, preferred_element_type=F32) * ATTN_SCALE
                dqt_acc[:, pl.ds(off, blk)] += jnp.dot(ktm[hh], dsb, preferred_element_type=F32) * ATTN_SCALE
            return dk, dv

        zero = jnp.zeros((blk, LANES), F32)
        carry = block(j, (zero, zero), True)
        dk, dv = lax.fori_loop(j + 1, nkv, lambda i, cr: block(i, cr, False), carry)
        dk_ref[...] = dk.astype(dk_ref.dtype)
        dv_ref[...] = dv.astype(dv_ref.dtype)
        lane = lax.broadcasted_iota(jnp.int32, (blk, LANES), 1)
        dcum = jnp.zeros((blk, LANES), F32)
        for hh in range(2):
            col = jnp.sum(ds_acc[hh], axis=1, keepdims=True)
            dcum = jnp.where(lane == 2 * p_id + hh, -col, dcum)
        dcum_ref[0, 0] = dcum

        @pl.when(j == nkv - 1)
        def _():
            dqt_ref[...] = dqt_acc[...]

    key_rows = lambda col: pl.BlockSpec((blk, LANES), lambda b, p, j: (b * nkv + j, col + p))
    seq_t = lambda row: pl.BlockSpec((LANES, s), lambda b, p, j: (row + p, b))
    return _call(body, name=name, grid=(bsz, pairs, nkv),
                 in_specs=[key_rows(k_col), key_rows(v_col),
                           pl.BlockSpec((LANES, blk), lambda b, p, j: (k_row + p, b * nkv + j)),
                           pl.BlockSpec((s, LANES), lambda b, p, j: (b, q_col + p)), seq_t(0),
                           pl.BlockSpec((s, LANES), lambda b, p, j: (b, p)), seq_t(0), seq_t(0),
                           pl.BlockSpec((1, 1, SUBLANES, s), lambda b, p, j: (b, p, 0, 0)),
                           pl.BlockSpec((1, 2, blk, LANES), lambda b, p, j: (b, p, j, 0)),
                           pl.BlockSpec((1, heads, s), lambda b, p, j: (b, 0, 0))],
                 out_specs=[key_rows(0), key_rows(0), seq_t(0),
                            pl.BlockSpec((1, 1, blk, LANES), lambda b, p, j: (b, p, j, 0))],
                 out_shape=[_sds((t, width), BF16), _sds((t, width), BF16), _sds((width, t), F32),
                            _sds((bsz, pairs, s, LANES), F32)],
                 scratch=[pltpu.VMEM((LANES, s), F32), pltpu.VMEM((2, blk, blk), F32)],
                 sem=("parallel", "parallel", "arbitrary"))(z, z, qkvt, z, qkvt, do, dot, ot, lse, cumb, cumt)


def _adamw(w, g, m, v, *, name):
    shape = w.shape
    cols = shape[-1]
    rows = w.size // cols
    tr = rows if rows <= 256 else _pick(rows, (256, 128, 64, 32, 16, 8))
    bc1 = 1.0 - ADAM_B1 ** ADAM_STEP
    bc2 = 1.0 - ADAM_B2 ** ADAM_STEP

    def body(w_ref, g_ref, m_ref, v_ref, d_ref, nm_ref, nv_ref):
        g_v = g_ref[...]
        nm = ADAM_B1 * m_ref[...] + (1.0 - ADAM_B1) * g_v
        nv = ADAM_B2 * v_ref[...] + (1.0 - ADAM_B2) * (g_v * g_v)
        nm_ref[...] = nm
        nv_ref[...] = nv
        d_ref[...] = -ADAM_LR * ((nm / bc1) / (jnp.sqrt(nv / bc2) + ADAM_EPS) + ADAM_WD * w_ref[...])

    blk = pl.BlockSpec((tr, cols), lambda i: (i, 0))
    flat = [a.reshape(rows, cols) for a in (w, g, m, v)]
    outs = _call(body, name=name, grid=(rows // tr,), in_specs=[blk] * 4, out_specs=[blk] * 3,
                 out_shape=[_sds((rows, cols), F32)] * 3, sem=("parallel",))(*flat)
    return tuple(a.reshape(shape) for a in outs)


_ANY = pl.BlockSpec(memory_space=pl.ANY)


def _comm_call(body, *, name, n_in, out_shape, n_sems):
    scratch = [pltpu.SemaphoreType.DMA((n_sems,)), pltpu.SemaphoreType.DMA((n_sems,)),
               pltpu.SemaphoreType.DMA((len(out_shape),))]
    return pl.pallas_call(body, name=name, in_specs=[_ANY] * n_in, out_specs=[_ANY] * len(out_shape),
                          out_shape=out_shape, scratch_shapes=scratch)


def _place():
    x, y, c = lax.axis_index("x"), lax.axis_index("y"), lax.axis_index("c")
    return x, y, c, [(1 - x, y), (x, 1 - y), (1 - x, 1 - y)]


def _remote(src, dst, send_sems, recv_sems, sem, to):
    return pltpu.make_async_remote_copy(src_ref=src, dst_ref=dst, send_sem=send_sems.at[sem],
                                        recv_sem=recv_sems.at[sem], device_id=to, device_id_type=MESH)


def _all_gather8(v, *, name):
    def body(v_ref, out_ref, send_sems, recv_sems, local_sems):
        x, y, c, _ = _place()
        me = 4 * x + 2 * y + c
        mine = pltpu.make_async_copy(v_ref, out_ref.at[me], local_sems.at[0])
        mine.start()
        peers = []
        for k in range(1, N_DEVICES):
            px = 1 - x if k & 4 else x
            py = 1 - y if k & 2 else y
            pc = 1 - c if k & 1 else c
            peers.append((px, py, pc))
        sends = [_remote(v_ref, out_ref.at[me], send_sems, recv_sems, k, peer) for k, peer in enumerate(peers)]
        for cp in sends:
            cp.start()
        for k, (px, py, pc) in enumerate(peers):
            _remote(v_ref, out_ref.at[4 * px + 2 * py + pc], send_sems, recv_sems, k, (px, py, pc)).wait_recv()
        for cp in sends:
            cp.wait_send()
        mine.wait()

    out = _comm_call(body, name=name, n_in=1, out_shape=[_sds((N_DEVICES,) + v.shape, v.dtype)],
                     n_sems=N_DEVICES - 1)(v)
    return out[0]


def _window(ref, mode, layer, chip, rows, cols):
    if mode == "slab":
        return ref.at[layer, chip]
    if mode == "cols":
        return ref.at[layer, :, pl.ds(pl.multiple_of(chip * cols, LANES), cols)]
    return ref.at[layer, pl.ds(pl.multiple_of(chip * rows, SUBLANES), rows), :]


def _whole_shape(mode, shard_shape):
    layers, rows, cols = shard_shape
    if mode == "slab":
        return (layers, N_CHIPS, rows, cols)
    if mode == "cols":
        assert cols % LANES == 0
        return (layers, rows, N_CHIPS * cols)
    assert rows % 16 == 0
    return (layers, N_CHIPS * rows, cols)


def _gather_weights(shards, modes, *, name):
    n = len(shards)
    meta = [(mode,) + tuple(a.shape[1:]) for a, mode in zip(shards, modes)]
    for a in shards:
        assert a.shape[0] == 2

    def body(*refs):
        ins, outs = refs[:n], refs[n:2 * n]
        send_sems, recv_sems, local_sems = refs[2 * n:]
        x, y, c, chips = _place()
        me = 2 * x + y
        local, first, passed = [], [], []
        for i, (mode, rows, cols) in enumerate(meta):
            cp = pltpu.make_async_copy(ins[i], _window(outs[i], mode, slice(None), me, rows, cols), local_sems.at[i])
            cp.start()
            local.append(cp)
            for r, (cx, cy) in enumerate(chips):
                cp = _remote(ins[i].at[c], _window(outs[i], mode, c, me, rows, cols), send_sems, recv_sems,
                             6 * i + r, (cx, cy, c))
                cp.start()
                first.append(cp)
        for i, (mode, rows, cols) in enumerate(meta):
            for r, (cx, cy) in enumerate(chips):
                win = _window(outs[i], mode, c, 2 * cx + cy, rows, cols)
                _remote(win, win, send_sems, recv_sems, 6 * i + r, (cx, cy, c)).wait_recv()
                cp = _remote(win, win, send_sems, recv_sems, 6 * i + 3 + r, (x, y, 1 - c))
                cp.start()
                passed.append(cp)
        for i, (mode, rows, cols) in enumerate(meta):
            for r, (cx, cy) in enumerate(chips):
                win = _window(outs[i], mode, 1 - c, 2 * cx + cy, rows, cols)
                _remote(win, win, send_sems, recv_sems, 6 * i + 3 + r, (x, y, 1 - c)).wait_recv()
        for cp in first + passed:
            cp.wait_send()
        for cp in local:
            cp.wait()

    out_shape = [_sds(_whole_shape(mode, a.shape), a.dtype) for a, mode in zip(shards, modes)]
    return _comm_call(body, name=name, n_in=n, out_shape=out_shape, n_sems=6 * n)(*shards)


def _rs_swap(grads, *, name):
    n = len(grads)

    def body(*refs):
        ins, outs = refs[:n], refs[n:2 * n]
        send_sems, recv_sems, _ = refs[2 * n:]
        x, y, c, _ = _place()
        copies = [_remote(ins[i].at[1 - c], outs[i], send_sems, recv_sems, i, (x, y, 1 - c)) for i in range(n)]
        for cp in copies:
            cp.start()
        for cp in copies:
            cp.wait()

    return _comm_call(body, name=name, n_in=n, out_shape=[_sds(g.shape[1:], g.dtype) for g in grads], n_sems=n)(*grads)


def _part(ref, mode, chip, rows, cols):
    if mode == "slab":
        return ref.at[chip]
    if mode == "cols":
        return ref.at[:, pl.ds(pl.multiple_of(chip * cols, LANES), cols)]
    return ref.at[pl.ds(pl.multiple_of(chip * rows, SUBLANES), rows), :]


def _rs_scatter(parts, modes, shard_shapes, *, name):
    n = len(parts)
    meta = [(mode,) + tuple(shp[1:]) for mode, shp in zip(modes, shard_shapes)]

    def body(*refs):
        ins, outs = refs[:n], refs[n:2 * n]
        send_sems, recv_sems, local_sems = refs[2 * n:]
        x, y, c, chips = _place()
        me = 2 * x + y
        local, sends = [], []
        for i, (mode, rows, cols) in enumerate(meta):
            cp = pltpu.make_async_copy(_part(ins[i], mode, me, rows, cols), outs[i].at[me], local_sems.at[i])
            cp.start()
            local.append(cp)
            for r, (cx, cy) in enumerate(chips):
                cp = _remote(_part(ins[i], mode, 2 * cx + cy, rows, cols), outs[i].at[me], send_sems, recv_sems,
                             3 * i + r, (cx, cy, c))
                cp.start()
                sends.append(cp)
        for i, (mode, rows, cols) in enumerate(meta):
            for r, (cx, cy) in enumerate(chips):
                k = 2 * cx + cy
                _remote(_part(ins[i], mode, k, rows, cols), outs[i].at[k], send_sems, recv_sems, 3 * i + r,
                        (cx, cy, c)).wait_recv()
        for cp in sends:
            cp.wait_send()
        for cp in local:
            cp.wait()

    out_shape = [_sds((N_CHIPS,) + tuple(shp[1:]), p.dtype) for p, shp in zip(parts, shard_shapes)]
    return _comm_call(body, name=name, n_in=n, out_shape=out_shape, n_sems=3 * n)(*parts)


def _rs_gather(sums, *, name):
    n = len(sums)

    def body(*refs):
        ins, outs = refs[:n], refs[n:2 * n]
        send_sems, recv_sems, local_sems = refs[2 * n:]
        x, y, c, _ = _place()
        local, sends = [], []
        for i in range(n):
            cp = pltpu.make_async_copy(ins[i], outs[i].at[c], local_sems.at[i])
            cp.start()
            local.append(cp)
            cp = _remote(ins[i], outs[i].at[c], send_sems, recv_sems, i, (x, y, 1 - c))
            cp.start()
            sends.append(cp)
        for i in range(n):
            _remote(ins[i], outs[i].at[1 - c], send_sems, recv_sems, i, (x, y, 1 - c)).wait_recv()
        for cp in sends:
            cp.wait_send()
        for cp in local:
            cp.wait()

    return _comm_call(body, name=name, n_in=n, out_shape=[_sds((2,) + s.shape, s.dtype) for s in sums], n_sems=n)(*sums)


def _row_tile(rows, cols, itemsize):
    target = max(SUBLANES, (2 << 20) // (cols * itemsize))
    cands = [c for c in (2048, 1024, 512, 256, 128, 64, 32, 16) if c <= target]
    tr = _pick(rows, cands)
    return tr


def _add_layer(g, other, core, *, name):
    _, rows, cols = g.shape
    tr = _row_tile(rows, cols, 4)

    def body(core_ref, g_ref, o_ref, out_ref):
        out_ref[...] = (g_ref[0] + o_ref[...]).astype(out_ref.dtype)

    grid_spec = pltpu.PrefetchScalarGridSpec(
        num_scalar_prefetch=1, grid=(rows // tr,),
        in_specs=[pl.BlockSpec((1, tr, cols), lambda i, core_ref: (core_ref[0], i, 0)),
                  pl.BlockSpec((tr, cols), lambda i, core_ref: (i, 0))],
        out_specs=pl.BlockSpec((tr, cols), lambda i, core_ref: (i, 0)))
    return pl.pallas_call(body, name=name, grid_spec=grid_spec, out_shape=_sds((rows, cols), BF16),
                          compiler_params=pltpu.CompilerParams(dimension_semantics=("parallel",),
                                                               vmem_limit_bytes=VMEM_LIMIT))(core, g, other)


def _sum_slots(parts, *, name):
    n, rows, cols = parts.shape
    tr = _row_tile(rows, cols, 4)

    def body(p_ref, o_ref):
        acc = p_ref[0].astype(F32) + p_ref[1].astype(F32)
        for k in range(2, n):
            acc = acc + p_ref[k].astype(F32)
        o_ref[...] = acc

    return _call(body, name=name, grid=(rows // tr,),
                 in_specs=[pl.BlockSpec((n, tr, cols), lambda i: (0, i, 0))],
                 out_specs=pl.BlockSpec((tr, cols), lambda i: (i, 0)),
                 out_shape=_sds((rows, cols), F32), sem=("parallel",))(parts)


def _reduce_scatter(grads, modes, shard_shapes):
    core = lax.axis_index("c").astype(jnp.int32).reshape(1)
    flat = [g.reshape(g.shape[0], -1, g.shape[-1]) for g in grads]
    from_sibling = _rs_swap(flat, name="rs_swap")
    parts = []
    for i, (g, o) in enumerate(zip(flat, from_sibling)):
        p = _add_layer(g, o, core, name=f"rs_add_{i}")
        parts.append(p.reshape(grads[i].shape[1:]))
    from_chips = _rs_scatter(parts, modes, shard_shapes, name="rs_scatter")
    sums = [_sum_slots(r, name=f"rs_sum_{i}") for i, r in enumerate(from_chips)]
    return _rs_gather(sums, name="rs_gather")


def _interleave(w, tcf):
    lead = w.shape[:-1]
    f = w.shape[-1] // 2
    return jnp.swapaxes(w.reshape(lead + (2, f // tcf, tcf)), -3, -2).reshape(lead + (2 * f,))


def _deinterleave(w, tcf):
    lead = w.shape[:-1]
    f = w.shape[-1] // 2
    return jnp.swapaxes(w.reshape(lead + (f // tcf, 2, tcf)), -3, -2).reshape(lead + (2 * f,))


def _layer_weights(full, rep, layer, dims):
    f_off, n_heads, tcf = dims["f_off"], dims["heads"], dims["tcf"]
    w_in = full["w_in"][layer]
    b_in = rep["b_in"][layer]
    pad = LANES - n_heads
    return {
        "w_main": jnp.concatenate([w_in[:, :f_off], w_in[:, f_off + n_heads:]], axis=1),
        "b_main": jnp.concatenate([b_in[:f_off], b_in[f_off + n_heads:]])[None],
        "w_f": jnp.pad(w_in[:, f_off:f_off + n_heads], ((0, 0), (0, pad))),
        "b_f": jnp.pad(b_in[f_off:f_off + n_heads], (0, pad))[None],
        "conv_a_w": full["conv_a_w"][layer],
        "conv_a_b": rep["conv_a_b"][layer][None],
        "ln_conv_g": rep["ln_conv_g"][layer][None],
        "ln_conv_b": rep["ln_conv_b"][layer][None],
        "w_conv_proj": full["w_conv_proj"][layer],
        "w_attn_proj": full["w_attn_proj"][layer],
        "w_mix_out": full["w_mix_out"][layer],
        "b_mix_out": rep["b_mix_out"][layer][None],
        "ln1_g": rep["ln1_g"][layer][None],
        "ln1_b": rep["ln1_b"][layer][None],
        "w_ffn_up": _interleave(full["w_ffn_up"][layer], tcf),
        "ffn_conv_w": _interleave(full["ffn_conv_w"][layer], tcf),
        "ffn_conv_b": _interleave(rep["ffn_conv_b"][layer], tcf)[None],
        "w_ffn_down": full["w_ffn_down"][layer],
        "ln2_g": rep["ln2_g"][layer][None],
        "ln2_b": rep["ln2_b"][layer][None],
    }


def _layer_fwd(x, mod, p, dims, tag):
    bsz, d, ch, heads, alpha = dims["bsz"], dims["d"], dims["ch"], dims["heads"], dims["alpha"]
    mods = [mod[:, k * d:(k + 1) * d][:, None, :] for k in range(6)]
    shift1, scale1, gate1, shift2, scale2, gate2 = mods
    u = _ln_mod_fwd(x, scale1, shift1, bsz, name=f"ln_mod1_{tag}")
    zm = _matmul(u, p["w_main"], "nn", F32, bias=p["b_main"], name=f"in_main_{tag}")
    zf = _matmul(u, p["w_f"], "nn", F32, bias=p["b_f"], name=f"in_forget_{tag}")
    a0 = _glu_fwd(zm, ch, name=f"glu_{tag}")
    a1 = _dwconv_fwd(a0, p["conv_a_w"], p["conv_a_b"], bsz, CONV_A_HALO, name=f"conv_a_{tag}")
    a3 = _lnsilu_fwd(a1, p["ln_conv_g"], p["ln_conv_b"], name=f"lnsilu_{tag}")
    ya = _matmul(a3, p["w_conv_proj"], "nn", F32, name=f"conv_proj_{tag}")
    cumt, cumb = _fgate_fwd(zf, bsz, heads, name=f"fgate_{tag}")
    qkvt = _to_features_major(zm, 2 * ch, heads * HEAD_DIM, 3, name=f"qkv_t_{tag}")
    ot, lse = _attn_fwd(zm, qkvt, cumt, cumb, bsz, heads, 2 * ch, name=f"attn_{tag}")
    yb = _matmul(ot, p["w_attn_proj"], "tn", F32, name=f"attn_proj_{tag}")
    m = _gate_merge_fwd(zm, ya, yb, dims["ga_off"], name=f"merge_{tag}")
    mix = _matmul(m, p["w_mix_out"], "nn", F32, bias=p["b_mix_out"], name=f"mix_out_{tag}")
    x1 = _ln_res_fwd(x, mix, gate1, p["ln1_g"], p["ln1_b"], alpha, bsz, name=f"ln_res1_{tag}")
    u2 = _ln_mod_fwd(x1, scale2, shift2, bsz, name=f"ln_mod2_{tag}")
    hp = _matmul(u2, p["w_ffn_up"], "nn", BF16, name=f"ffn_up_{tag}")
    f = _ffn_act_fwd(hp, p["ffn_conv_w"], p["ffn_conv_b"], bsz, dims["tcf"], name=f"ffn_act_{tag}")
    ffn = _matmul(f, p["w_ffn_down"], "nn", F32, name=f"ffn_down_{tag}")
    x2 = _ln_res_fwd(x1, ffn, gate2, p["ln2_g"], p["ln2_b"], alpha, bsz, name=f"ln_res2_{tag}")
    saved = dict(x=x, mods=mods, u=u, zm=zm, zf=zf, a0=a0, a1=a1, a3=a3, ya=ya, yb=yb, cumt=cumt, cumb=cumb,
                 qkvt=qkvt, ot=ot, lse=lse, m=m, mix=mix, x1=x1, u2=u2, hp=hp, f=f, ffn=ffn)
    return x2, saved


def _layer_bwd(dx2, p, sv, dims, tag):
    bsz, ch, heads, alpha = dims["bsz"], dims["ch"], dims["heads"], dims["alpha"]
    f_off, tcf = dims["f_off"], dims["tcf"]
    shift1, scale1, gate1, shift2, scale2, gate2 = sv["mods"]
    g = {}
    dr2, dffn, dgate2, g["ln2_g"], g["ln2_b"], _ = _ln_res_bwd(
        dx2, sv["x1"], sv["ffn"], gate2, p["ln2_g"], alpha, bsz, name=f"ln_res2_bwd_{tag}")
    df = _matmul(dffn, p["w_ffn_down"], "nt", F32, name=f"ffn_down_dx_{tag}")
    g["w_ffn_down"] = _matmul(sv["f"], dffn, "tn", F32, name=f"ffn_down_dw_{tag}")
    dhp, dfw, dfb = _ffn_act_bwd(sv["hp"], df, p["ffn_conv_w"], p["ffn_conv_b"], bsz, tcf, name=f"ffn_act_bwd_{tag}")
    g["ffn_conv_w"] = _deinterleave(dfw, tcf)
    g["ffn_conv_b"] = _deinterleave(dfb, tcf)[0]
    du2 = _matmul(dhp, p["w_ffn_up"], "nt", F32, name=f"ffn_up_dx_{tag}")
    g["w_ffn_up"] = _deinterleave(_matmul(sv["u2"], dhp, "tn", F32, name=f"ffn_up_dw_{tag}"), tcf)
    dx1, dscale2, dshift2 = _ln_mod_bwd(du2, sv["x1"], scale2, dr2, alpha, bsz, name=f"ln_mod2_bwd_{tag}")
    dr1, dmix, dgate1, g["ln1_g"], g["ln1_b"], g["b_mix_out"] = _ln_res_bwd(
        dx1, sv["x"], sv["mix"], gate1, p["ln1_g"], alpha, bsz, name=f"ln_res1_bwd_{tag}")
    dm = _matmul(dmix, p["w_mix_out"], "nt", F32, name=f"mix_out_dx_{tag}")
    g["w_mix_out"] = _matmul(sv["m"], dmix, "tn", F32, name=f"mix_out_dw_{tag}")
    dya, dyb, dzga, dzgb = _gate_merge_bwd(sv["zm"], sv["ya"], sv["yb"], dm, dims["ga_off"], name=f"merge_bwd_{tag}")
    da3 = _matmul(dya, p["w_conv_proj"], "nt", F32, name=f"conv_proj_dx_{tag}")
    g["w_conv_proj"] = _matmul(sv["a3"], dya, "tn", F32, name=f"conv_proj_dw_{tag}")
    do = _matmul(dyb, p["w_attn_proj"], "nt", BF16, name=f"attn_proj_dx_{tag}")
    dot = _matmul(p["w_attn_proj"], dyb, "nt", BF16, name=f"attn_proj_dxt_{tag}")
    g["w_attn_proj"] = _matmul(sv["ot"], dyb, "nn", F32, name=f"attn_proj_dw_{tag}")
    da1, g["ln_conv_g"], g["ln_conv_b"] = _lnsilu_bwd(sv["a1"], da3, p["ln_conv_g"], p["ln_conv_b"],
                                                      name=f"lnsilu_bwd_{tag}")
    da0, g["conv_a_w"], dcb = _dwconv_bwd(sv["a0"], da1, p["conv_a_w"], bsz, CONV_A_HALO, F32, name=f"conv_a_bwd_{tag}")
    g["conv_a_b"] = dcb[0]
    dzglu = _glu_bwd(sv["zm"], da0, ch, name=f"glu_bwd_{tag}")
    dk, dv, dqt, dcum = _attn_bwd(sv["zm"], sv["qkvt"], sv["cumt"], sv["cumb"], sv["ot"], do, dot, sv["lse"], bsz,
                                  heads, 2 * ch, name=f"attn_bwd_{tag}")
    dq = _to_rows_major(dqt, name=f"dq_rows_{tag}")
    dzf = _fgate_bwd(dcum, sv["zf"], bsz, name=f"fgate_bwd_{tag}")
    dzm = jnp.concatenate([dzglu, dq, dk, dv, dzga, dzgb], axis=1)
    du = _matmul(dzm, p["w_main"], "nt", F32, name=f"in_main_dx_{tag}")
    du = _matmul(dzf, p["w_f"], "nt", F32, add=du, name=f"in_forget_dx_{tag}")
    dwm = _matmul(sv["u"], dzm, "tn", F32, name=f"in_main_dw_{tag}")
    dwf = _matmul(sv["u"], dzf, "tn", F32, name=f"in_forget_dw_{tag}")
    dbm = _colsum(dzm, name=f"in_main_db_{tag}")[0]
    dbf = _colsum(dzf, name=f"in_forget_db_{tag}")[0]
    g["w_in"] = jnp.concatenate([dwm[:, :f_off], dwf[:, :heads], dwm[:, f_off:]], axis=1)
    g["b_in"] = jnp.concatenate([dbm[:f_off], dbf[:heads], dbm[f_off:]])
    dx, dscale1, dshift1 = _ln_mod_bwd(du, sv["x"], scale1, dr1, alpha, bsz, name=f"ln_mod1_bwd_{tag}")
    dmod = jnp.concatenate([dshift1, dscale1, dgate1, dshift2, dscale2, dgate2], axis=2)[:, 0, :]
    return dx, g, dmod


def _local_step(x, mod, loss_target, full, rep, dims):
    bsz, seq, d = x.shape
    layers = mod.shape[0]
    params = [_layer_weights(full, rep, layer, dims) for layer in range(layers)]
    h = x.reshape(bsz * seq, d)
    saved = []
    for layer in range(layers):
        h, sv = _layer_fwd(h, mod[layer], params[layer], dims, f"l{layer}")
        saved.append(sv)
    dh, sq = _loss_head(h, loss_target.reshape(bsz * seq, d), name="loss_head")
    loss_local = 0.5 * jnp.sum(sq) / d
    grads, dmods = [None] * layers, [None] * layers
    for layer in reversed(range(layers)):
        dh, grads[layer], dmods[layer] = _layer_bwd(dh, params[layer], saved[layer], dims, f"l{layer}")
    stacked = {wname: jnp.stack([grads[layer][wname] for layer in range(layers)]) for wname in grads[0]}
    return loss_local, dh.reshape(bsz, seq, d), stacked, jnp.stack(dmods)


def _pad_rows(a):
    extra = -a.shape[-2] % (2 * SUBLANES)
    if extra == 0:
        return a
    return jnp.pad(a, [(0, 0)] * (a.ndim - 2) + [(0, extra), (0, 0)])


def _to_slab(g):
    layers, k, n4 = g.shape
    return jnp.transpose(g.reshape(layers, k, N_CHIPS, n4 // N_CHIPS), (0, 2, 1, 3))


def _from_slab(w):
    layers, _, k, n = w.shape
    return jnp.transpose(w, (0, 2, 1, 3)).reshape(layers, k, N_CHIPS * n)


def kernel(x, c, w_ada, b_ada, w_in, b_in, conv_a_w, conv_a_b, ln_conv_g, ln_conv_b, w_conv_proj, w_attn_proj, w_mix_out, b_mix_out, ln1_g, ln1_b, w_ffn_up, ffn_conv_w, ffn_conv_b, w_ffn_down, ln2_g, ln2_b, loss_target, m_w_ada, m_b_ada, m_w_in, m_b_in, m_conv_a_w, m_conv_a_b, m_ln_conv_g, m_ln_conv_b, m_w_conv_proj, m_w_attn_proj, m_w_mix_out, m_b_mix_out, m_ln1_g, m_ln1_b, m_w_ffn_up, m_ffn_conv_w, m_ffn_conv_b, m_w_ffn_down, m_ln2_g, m_ln2_b, v_w_ada, v_b_ada, v_w_in, v_b_in, v_conv_a_w, v_conv_a_b, v_ln_conv_g, v_ln_conv_b, v_w_conv_proj, v_w_attn_proj, v_w_mix_out, v_b_mix_out, v_ln1_g, v_ln1_b, v_w_ffn_up, v_ffn_conv_w, v_ffn_conv_b, v_w_ffn_down, v_ln2_g, v_ln2_b):
    weights = dict(zip(WEIGHTS, (w_ada, b_ada, w_in, b_in, conv_a_w, conv_a_b, ln_conv_g, ln_conv_b, w_conv_proj,
                                 w_attn_proj, w_mix_out, b_mix_out, ln1_g, ln1_b, w_ffn_up, ffn_conv_w, ffn_conv_b,
                                 w_ffn_down, ln2_g, ln2_b)))
    mom1 = dict(zip(WEIGHTS, (m_w_ada, m_b_ada, m_w_in, m_b_in, m_conv_a_w, m_conv_a_b, m_ln_conv_g, m_ln_conv_b,
                              m_w_conv_proj, m_w_attn_proj, m_w_mix_out, m_b_mix_out, m_ln1_g, m_ln1_b, m_w_ffn_up,
                              m_ffn_conv_w, m_ffn_conv_b, m_w_ffn_down, m_ln2_g, m_ln2_b)))
    mom2 = dict(zip(WEIGHTS, (v_w_ada, v_b_ada, v_w_in, v_b_in, v_conv_a_w, v_conv_a_b, v_ln_conv_g, v_ln_conv_b,
                              v_w_conv_proj, v_w_attn_proj, v_w_mix_out, v_b_mix_out, v_ln1_g, v_ln1_b, v_w_ffn_up,
                              v_ffn_conv_w, v_ffn_conv_b, v_w_ffn_down, v_ln2_g, v_ln2_b)))
    bsz, seq, d = x.shape
    layers = w_ada.shape[0]
    ch = conv_a_w.shape[2] * N_CHIPS
    width = w_attn_proj.shape[1]
    heads = width // HEAD_DIM
    d_ff = w_ffn_down.shape[1] * N_CHIPS
    dims = dict(bsz=bsz, d=d, ch=ch, heads=heads, alpha=(2.0 * layers) ** 0.25, f_off=2 * ch + 3 * width,
                ga_off=2 * ch + 3 * width, tcf=_pick(d_ff, (256, 128)))
    chip = 2 * lax.axis_index("x") + lax.axis_index("y")
    device = 2 * chip + lax.axis_index("c")
    ada_cols = w_ada.shape[2]

    c_act = _silu_rows(_all_gather8(c, name="gather_c").reshape(N_DEVICES * bsz, d), name="silu_c")
    b_ada_mine = lax.dynamic_slice_in_dim(b_ada, chip * ada_cols, ada_cols, axis=1)
    mod_cols = jnp.stack([_matmul(c_act, w_ada[layer], "nn", F32, bias=b_ada_mine[layer][None], name=f"ada_l{layer}")
                          for layer in range(layers)])
    mod_all = _all_gather8(mod_cols, name="gather_mod")
    mod_all = jnp.concatenate([mod_all[2 * k] for k in range(N_CHIPS)], axis=-1)
    mod = lax.dynamic_slice_in_dim(mod_all, device * bsz, bsz, axis=1)

    shards = [_pad_rows(weights[wname].astype(BF16) if as_bf16 else weights[wname]) for wname, _, as_bf16 in GATHERED]
    modes = [mode for _, mode, _ in GATHERED]
    whole = _gather_weights(shards, modes, name="gather_weights")
    full = {wname: w[:, :weights[wname].shape[1]] if mode == "cols" else w
            for (wname, mode, _), w in zip(GATHERED, whole)}
    full["w_in"] = _from_slab(full["w_in"])
    rep = {wname: weights[wname] for wname in REPLICATED}

    loss_local, grad_x, grads, dmod = _local_step(x, mod, loss_target, full, rep, dims)
    loss = lax.psum(loss_local, ("x", "y", "c"))

    grads["w_in"] = _to_slab(grads["w_in"])
    shard_shapes = [s.shape for s in shards]
    reduced = _reduce_scatter([_pad_rows(grads[wname]) for wname, _, _ in GATHERED], modes, shard_shapes)
    grad = {wname: r[:, :weights[wname].shape[1]] for (wname, _, _), r in zip(GATHERED, reduced)}

    small = jnp.concatenate([dmod.reshape(-1)] + [grads[wname].reshape(-1) for wname in REPLICATED])
    n_small = small.shape[0]
    rows = -(-n_small // (SUBLANES * LANES)) * SUBLANES
    small = jnp.pad(small, (0, rows * LANES - n_small)).reshape(rows, LANES)
    gathered = _all_gather8(small, name="gather_small")
    n_dmod = dmod.size
    dmod_all = gathered.reshape(N_DEVICES, -1)[:, :n_dmod].reshape(N_DEVICES, layers, bsz, 6 * d)
    dmod_all = jnp.transpose(dmod_all, (1, 0, 2, 3)).reshape(layers, N_DEVICES * bsz, 6 * d)
    summed = _sum_slots(gathered, name="sum_small").reshape(-1)
    off = n_dmod
    for wname in REPLICATED:
        n = weights[wname].size
        grad[wname] = summed[off:off + n].reshape(weights[wname].shape)
        off += n
    dmod_mine = lax.dynamic_slice_in_dim(dmod_all, chip * ada_cols, ada_cols, axis=2)
    grad["w_ada"] = jnp.stack([_matmul(c_act, dmod_mine[layer], "tn", F32, name=f"ada_dw_l{layer}")
                               for layer in range(layers)])
    grad["b_ada"] = jnp.stack([_colsum(dmod_all[layer], name=f"ada_db_l{layer}")[0] for layer in range(layers)])

    delta, new_m, new_v = {}, {}, {}
    for wname in WEIGHTS:
        delta[wname], new_m[wname], new_v[wname] = _adamw(weights[wname], grad[wname], mom1[wname], mom2[wname],
                                                          name=f"adamw_{wname}")
    return (loss, grad_x, *[grad[wname] for wname in WEIGHTS], *[delta[wname] for wname in WEIGHTS],
            *[new_m[wname] for wname in WEIGHTS], *[new_v[wname] for wname in WEIGHTS])
```

```python
import math

import jax
import jax.numpy as jnp
from jax import lax
from jax.experimental import pallas as pl
from jax.experimental.pallas import tpu as pltpu

F32 = jnp.float32
BF16 = jnp.bfloat16
MESH = pl.DeviceIdType.MESH

LN_EPS = 1e-5
HEAD_DIM = 64
ATTN_SCALE = HEAD_DIM ** -0.5
NEG = -1e30
CONV_A_HALO = 32
FFN_PAD = 8
LANES = 128
SUBLANES = 8
ROW_CHUNK = 256
ATTN_BLOCK = 256
N_CHIPS = 4
N_DEVICES = 8
VMEM_LIMIT = 56 * 1024 * 1024

ADAM_LR = 0.001
ADAM_B1 = 0.9
ADAM_B2 = 0.999
ADAM_EPS = 1e-08
ADAM_WD = 0.01
ADAM_STEP = 10

GATHERED = (("w_in", "slab", True), ("conv_a_w", "cols", False), ("w_conv_proj", "cols", True),
            ("w_attn_proj", "cols", True), ("w_mix_out", "rows", True), ("w_ffn_up", "cols", True),
            ("ffn_conv_w", "cols", False), ("w_ffn_down", "rows", True))
REPLICATED = ("b_in", "conv_a_b", "ln_conv_g", "ln_conv_b", "b_mix_out", "ln1_g", "ln1_b",
              "ffn_conv_b", "ln2_g", "ln2_b")
WEIGHTS = ("w_ada", "b_ada", "w_in", "b_in", "conv_a_w", "conv_a_b", "ln_conv_g", "ln_conv_b",
           "w_conv_proj", "w_attn_proj", "w_mix_out", "b_mix_out", "ln1_g", "ln1_b", "w_ffn_up",
           "ffn_conv_w", "ffn_conv_b", "w_ffn_down", "ln2_g", "ln2_b")


def _pick(n, cands):
    for cand in cands:
        if n % cand == 0:
            return cand
    return n


def _call(body, *, name, grid, in_specs, out_specs, out_shape, scratch=(), sem=None):
    return pl.pallas_call(
        body, name=name, grid=grid, in_specs=in_specs, out_specs=out_specs, out_shape=out_shape,
        scratch_shapes=list(scratch),
        compiler_params=pltpu.CompilerParams(dimension_semantics=sem, vmem_limit_bytes=VMEM_LIMIT))


def _sds(shape, dtype):
    return jax.ShapeDtypeStruct(tuple(shape), dtype)


def _chunked(rows, fn):
    chunk = min(ROW_CHUNK, rows)
    if rows == chunk:
        fn(pl.ds(0, rows))
        return

    def step(i, carry):
        fn(pl.ds(pl.multiple_of(i * chunk, chunk), chunk))
        return carry

    lax.fori_loop(0, rows // chunk, step, 0)


def _matmul(a, b, mode, out_dtype, *, bias=None, add=None, colsum=False, name):
    if mode == "nn":
        (m, k), (_, n) = a.shape, b.shape
    elif mode == "nt":
        (m, k), (n, _) = a.shape, b.shape
    else:
        (k, m), (_, n) = a.shape, b.shape
    tm = _pick(m, (1024, 1408, 512, 256, 128))
    tn = _pick(n, (512, 1408, 256, 128))
    tk = k if k <= 1536 else _pick(k, (1024, 1536, 1408, 512, 256, 128))
    nk = k // tk
    if mode == "nn":
        a_spec = pl.BlockSpec((tm, tk), lambda i, j, kk: (i, kk))
        b_spec = pl.BlockSpec((tk, tn), lambda i, j, kk: (kk, j))
        dims = (((1,), (0,)), ((), ()))
    elif mode == "nt":
        a_spec = pl.BlockSpec((tm, tk), lambda i, j, kk: (i, kk))
        b_spec = pl.BlockSpec((tn, tk), lambda i, j, kk: (j, kk))
        dims = (((1,), (1,)), ((), ()))
    else:
        a_spec = pl.BlockSpec((tk, tm), lambda i, j, kk: (kk, i))
        b_spec = pl.BlockSpec((tk, tn), lambda i, j, kk: (kk, j))
        dims = (((0,), (0,)), ((), ()))
    in_specs = [a_spec, b_spec]
    operands = [a, b]
    if bias is not None:
        in_specs.append(pl.BlockSpec((1, tn), lambda i, j, kk: (0, j)))
        operands.append(bias)
    if add is not None:
        in_specs.append(pl.BlockSpec((tm, tn), lambda i, j, kk: (i, j)))
        operands.append(add)

    def body(a_ref, b_ref, *rest):
        rest = list(rest)
        bias_ref = rest.pop(0) if bias is not None else None
        add_ref = rest.pop(0) if add is not None else None
        o_ref = rest.pop(0)
        prod = lax.dot_general(a_ref[...].astype(BF16), b_ref[...].astype(BF16), dims,
                               preferred_element_type=F32)
        if colsum:
            cs_ref = rest.pop(0)
            part = jnp.sum(b_ref[...].astype(F32), axis=0, keepdims=True)

            @pl.when(pl.program_id(2) == 0)
            def _():
                cs_ref[...] = part

            @pl.when(pl.program_id(2) > 0)
            def _():
                cs_ref[...] += part

        def finish(r):
            if bias_ref is not None:
                r = r + bias_ref[...]
            if add_ref is not None:
                r = r + add_ref[...]
            o_ref[...] = r.astype(o_ref.dtype)

        if nk == 1:
            finish(prod)
            return
        acc_ref = rest.pop(0)
        kk = pl.program_id(2)

        @pl.when(kk == 0)
        def _():
            acc_ref[...] = prod

        @pl.when(kk > 0)
        def _():
            acc_ref[...] += prod

        @pl.when(kk == nk - 1)
        def _():
            finish(acc_ref[...])

    out_specs = pl.BlockSpec((tm, tn), lambda i, j, kk: (i, j))
    out_shape = _sds((m, n), out_dtype)
    if colsum:
        assert mode == "tn" and m == tm
        out_specs = [out_specs, pl.BlockSpec((1, tn), lambda i, j, kk: (0, j))]
        out_shape = [out_shape, _sds((1, n), F32)]
    return _call(body, name=name, grid=(m // tm, n // tn, nk), in_specs=in_specs, out_specs=out_specs,
                 out_shape=out_shape, scratch=[pltpu.VMEM((tm, tn), F32)] if nk > 1 else [],
                 sem=("parallel", "parallel", "arbitrary"))(*operands)


def _colsum(x, *, name):
    rows, n = x.shape
    tr = _pick(rows, (1024, 512, 256, 128))
    tn = _pick(n, (512, 256, 128))

    def body(x_ref, o_ref):
        @pl.when(pl.program_id(1) == 0)
        def _():
            o_ref[...] = jnp.zeros_like(o_ref)

        o_ref[...] += jnp.sum(x_ref[...].astype(F32), axis=0, keepdims=True)

    return _call(body, name=name, grid=(n // tn, rows // tr),
                 in_specs=[pl.BlockSpec((tr, tn), lambda j, i: (i, j))],
                 out_specs=pl.BlockSpec((1, tn), lambda j, i: (0, j)),
                 out_shape=_sds((1, n), F32), sem=("parallel", "arbitrary"))(x)


def _ln_stats(x):
    mu = jnp.mean(x, axis=-1, keepdims=True)
    xc = x - mu
    var = jnp.mean(xc * xc, axis=-1, keepdims=True)
    rstd = lax.rsqrt(var + LN_EPS)
    return xc * rstd, rstd


def _ln_bwd(dn, n, rstd):
    return rstd * (dn - jnp.mean(dn, axis=-1, keepdims=True) - n * jnp.mean(dn * n, axis=-1, keepdims=True))


def _seq_tiles(t, bsz, cands=(1024, 512, 256, 128, 64, 32, 16, 8)):
    s = t // bsz
    ts = _pick(s, cands)
    return s, ts, s // ts


def _ln_mod_fwd(x, scale, shift, bsz, *, name):
    t, d = x.shape
    _, ts, ns = _seq_tiles(t, bsz)

    def body(x_ref, sc_ref, sh_ref, u_ref):
        one_scale = 1.0 + sc_ref[0]
        shift_v = sh_ref[0]

        def piece(rows):
            n, _ = _ln_stats(x_ref[rows, :])
            u_ref[rows, :] = (n * one_scale + shift_v).astype(u_ref.dtype)

        _chunked(ts, piece)

    row = pl.BlockSpec((ts, d), lambda b, i: (b * ns + i, 0))
    per = pl.BlockSpec((1, 1, d), lambda b, i: (b, 0, 0))
    return _call(body, name=name, grid=(bsz, ns), in_specs=[row, per, per], out_specs=row,
                 out_shape=_sds((t, d), BF16), sem=("parallel", "parallel"))(x, scale, shift)


def _ln_mod_bwd(du, x, scale, dr, alpha, bsz, *, name):
    t, d = x.shape
    _, ts, ns = _seq_tiles(t, bsz)

    def body(du_ref, x_ref, sc_ref, dr_ref, dx_ref, dsc_ref, dsh_ref):
        @pl.when(pl.program_id(1) == 0)
        def _():
            dsc_ref[...] = jnp.zeros_like(dsc_ref)
            dsh_ref[...] = jnp.zeros_like(dsh_ref)

        one_scale = 1.0 + sc_ref[0]

        def piece(rows):
            du_v = du_ref[rows, :]
            n, rstd = _ln_stats(x_ref[rows, :])
            dsc_ref[0] += jnp.sum(du_v * n, axis=0, keepdims=True)
            dsh_ref[0] += jnp.sum(du_v, axis=0, keepdims=True)
            dx_ref[rows, :] = alpha * dr_ref[rows, :] + _ln_bwd(du_v * one_scale, n, rstd)

        _chunked(ts, piece)

    row = pl.BlockSpec((ts, d), lambda b, i: (b * ns + i, 0))
    per = pl.BlockSpec((1, 1, d), lambda b, i: (b, 0, 0))
    return _call(body, name=name, grid=(bsz, ns), in_specs=[row, row, per, row],
                 out_specs=[row, per, per],
                 out_shape=[_sds((t, d), F32), _sds((bsz, 1, d), F32), _sds((bsz, 1, d), F32)],
                 sem=("parallel", "arbitrary"))(du, x, scale, dr)


def _ln_res_fwd(x, y, gate, g, b, alpha, bsz, *, name):
    t, d = x.shape
    _, ts, ns = _seq_tiles(t, bsz)

    def body(x_ref, y_ref, gt_ref, g_ref, b_ref, o_ref):
        one_gate = 1.0 + gt_ref[0]

        def piece(rows):
            n, _ = _ln_stats(alpha * x_ref[rows, :] + one_gate * y_ref[rows, :])
            o_ref[rows, :] = n * g_ref[...] + b_ref[...]

        _chunked(ts, piece)

    row = pl.BlockSpec((ts, d), lambda bb, i: (bb * ns + i, 0))
    per = pl.BlockSpec((1, 1, d), lambda bb, i: (bb, 0, 0))
    vec = pl.BlockSpec((1, d), lambda bb, i: (0, 0))
    return _call(body, name=name, grid=(bsz, ns), in_specs=[row, row, per, vec, vec], out_specs=row,
                 out_shape=_sds((t, d), F32), sem=("parallel", "parallel"))(x, y, gate, g, b)


def _ln_res_bwd(do, x, y, gate, g, alpha, bsz, *, name):
    t, d = x.shape
    _, ts, ns = _seq_tiles(t, bsz)

    def body(do_ref, x_ref, y_ref, gt_ref, g_ref, dr_ref, dy_ref, dgt_ref, dg_ref, db_ref, dys_ref):
        first_tile = pl.program_id(1) == 0

        @pl.when(first_tile)
        def _():
            dgt_ref[...] = jnp.zeros_like(dgt_ref)

        @pl.when(jnp.logical_and(first_tile, pl.program_id(0) == 0))
        def _():
            dg_ref[...] = jnp.zeros_like(dg_ref)
            db_ref[...] = jnp.zeros_like(db_ref)
            dys_ref[...] = jnp.zeros_like(dys_ref)

        one_gate = 1.0 + gt_ref[0]

        def piece(rows):
            do_v = do_ref[rows, :]
            y_v = y_ref[rows, :]
            n, rstd = _ln_stats(alpha * x_ref[rows, :] + one_gate * y_v)
            dg_ref[...] += jnp.sum(do_v * n, axis=0, keepdims=True)
            db_ref[...] += jnp.sum(do_v, axis=0, keepdims=True)
            dr = _ln_bwd(do_v * g_ref[...], n, rstd)
            dr_ref[rows, :] = dr
            dy = one_gate * dr
            dy_ref[rows, :] = dy.astype(dy_ref.dtype)
            dys_ref[...] += jnp.sum(dy, axis=0, keepdims=True)
            dgt_ref[0] += jnp.sum(dr * y_v, axis=0, keepdims=True)

        _chunked(ts, piece)

    row = pl.BlockSpec((ts, d), lambda bb, i: (bb * ns + i, 0))
    per = pl.BlockSpec((1, 1, d), lambda bb, i: (bb, 0, 0))
    vec = pl.BlockSpec((1, d), lambda bb, i: (0, 0))
    return _call(body, name=name, grid=(bsz, ns), in_specs=[row, row, row, per, vec],
                 out_specs=[row, row, per, vec, vec, vec],
                 out_shape=[_sds((t, d), F32), _sds((t, d), BF16), _sds((bsz, 1, d), F32),
                            _sds((1, d), F32), _sds((1, d), F32), _sds((1, d), F32)],
                 sem=("arbitrary", "arbitrary"))(do, x, y, gate, g)


def _loss_head(y, target, *, name):
    t, d = y.shape
    tr = _pick(t, (1024, 512, 256, 128, 64, 32, 16, 8))

    def body(y_ref, t_ref, dy_ref, s_ref):
        @pl.when(pl.program_id(0) == 0)
        def _():
            s_ref[...] = jnp.zeros_like(s_ref)

        def piece(rows):
            e = y_ref[rows, :] - t_ref[rows, :]
            dy_ref[rows, :] = e * (1.0 / d)
            s_ref[...] += jnp.sum(e * e, axis=0, keepdims=True)

        _chunked(tr, piece)

    row = pl.BlockSpec((tr, d), lambda i: (i, 0))
    return _call(body, name=name, grid=(t // tr,), in_specs=[row, row],
                 out_specs=[row, pl.BlockSpec((1, d), lambda i: (0, 0))],
                 out_shape=[_sds((t, d), F32), _sds((1, d), F32)], sem=("arbitrary",))(y, target)


def _sigmoid(v):
    return 1.0 / (1.0 + jnp.exp(-v))


def _silu_rows(c, *, name):
    rows, d = c.shape

    def body(c_ref, o_ref):
        v = c_ref[...]
        o_ref[...] = (v * _sigmoid(v)).astype(o_ref.dtype)

    full = pl.BlockSpec((rows, d), lambda i: (0, 0))
    return _call(body, name=name, grid=(1,), in_specs=[full], out_specs=full,
                 out_shape=_sds((rows, d), BF16), sem=("arbitrary",))(c)


def _glu_fwd(z, ch, *, name):
    t = z.shape[0]
    tr = _pick(t, (1024, 512, 256, 128, 64, 32, 16, 8))

    def body(z_ref, o_ref):
        def piece(rows):
            o_ref[rows, :] = z_ref[rows, :ch].astype(F32) * _sigmoid(z_ref[rows, ch:].astype(F32))

        _chunked(tr, piece)

    return _call(body, name=name, grid=(t // tr,),
                 in_specs=[pl.BlockSpec((tr, 2 * ch), lambda i: (i, 0))],
                 out_specs=pl.BlockSpec((tr, ch), lambda i: (i, 0)),
                 out_shape=_sds((t, ch), F32), sem=("parallel",))(z)


def _glu_bwd(z, da0, ch, *, name):
    t = z.shape[0]
    tr = _pick(t, (1024, 512, 256, 128, 64, 32, 16, 8))

    def body(z_ref, d_ref, o_ref):
        def piece(rows):
            s = _sigmoid(z_ref[rows, ch:].astype(F32))
            d = d_ref[rows, :]
            o_ref[rows, :ch] = (d * s).astype(o_ref.dtype)
            o_ref[rows, ch:] = (d * z_ref[rows, :ch].astype(F32) * s * (1.0 - s)).astype(o_ref.dtype)

        _chunked(tr, piece)

    return _call(body, name=name, grid=(t // tr,),
                 in_specs=[pl.BlockSpec((tr, 2 * ch), lambda i: (i, 0)),
                           pl.BlockSpec((tr, ch), lambda i: (i, 0))],
                 out_specs=pl.BlockSpec((tr, 2 * ch), lambda i: (i, 0)),
                 out_shape=_sds((t, 2 * ch), BF16), sem=("parallel",))(z, da0)


def _lnsilu_fwd(a1, g, b, *, name):
    t, ch = a1.shape
    tr = _pick(t, (1024, 512, 256, 128, 64, 32, 16, 8))

    def body(a_ref, g_ref, b_ref, o_ref):
        def piece(rows):
            n, _ = _ln_stats(a_ref[rows, :])
            a2 = n * g_ref[...] + b_ref[...]
            o_ref[rows, :] = (a2 * _sigmoid(a2)).astype(o_ref.dtype)

        _chunked(tr, piece)

    row = pl.BlockSpec((tr, ch), lambda i: (i, 0))
    vec = pl.BlockSpec((1, ch), lambda i: (0, 0))
    return _call(body, name=name, grid=(t // tr,), in_specs=[row, vec, vec], out_specs=row,
                 out_shape=_sds((t, ch), BF16), sem=("parallel",))(a1, g, b)


def _lnsilu_bwd(a1, da3, g, b, *, name):
    t, ch = a1.shape
    tr = _pick(t, (1024, 512, 256, 128, 64, 32, 16, 8))

    def body(a_ref, d_ref, g_ref, b_ref, o_ref, dg_ref, db_ref):
        @pl.when(pl.program_id(0) == 0)
        def _():
            dg_ref[...] = jnp.zeros_like(dg_ref)
            db_ref[...] = jnp.zeros_like(db_ref)

        def piece(rows):
            n, rstd = _ln_stats(a_ref[rows, :])
            a2 = n * g_ref[...] + b_ref[...]
            s = _sigmoid(a2)
            da2 = d_ref[rows, :] * (s * (1.0 + a2 * (1.0 - s)))
            dg_ref[...] += jnp.sum(da2 * n, axis=0, keepdims=True)
            db_ref[...] += jnp.sum(da2, axis=0, keepdims=True)
            o_ref[rows, :] = _ln_bwd(da2 * g_ref[...], n, rstd)

        _chunked(tr, piece)

    row = pl.BlockSpec((tr, ch), lambda i: (i, 0))
    vec = pl.BlockSpec((1, ch), lambda i: (0, 0))
    return _call(body, name=name, grid=(t // tr,), in_specs=[row, row, vec, vec],
                 out_specs=[row, vec, vec],
                 out_shape=[_sds((t, ch), F32), _sds((1, ch), F32), _sds((1, ch), F32)],
                 sem=("arbitrary",))(a1, da3, g, b)


def _gate_cols(d, ga_off):
    tc = _pick(math.gcd(d, ga_off), (512, 256, 128))
    return tc, ga_off // tc, (ga_off + d) // tc


def _gate_merge_fwd(z, ya, yb, ga_off, *, name):
    t, d = ya.shape
    tr = _pick(t, (1024, 512, 256, 128, 64, 32, 16, 8))
    tc, ga_blk, gb_blk = _gate_cols(d, ga_off)

    def body(ga_ref, gb_ref, ya_ref, yb_ref, o_ref):
        def piece(rows):
            o_ref[rows, :] = (_sigmoid(ga_ref[rows, :].astype(F32)) * ya_ref[rows, :]
                              + _sigmoid(gb_ref[rows, :].astype(F32)) * yb_ref[rows, :]).astype(o_ref.dtype)

        _chunked(tr, piece)

    blk = pl.BlockSpec((tr, tc), lambda i, j: (i, j))
    return _call(body, name=name, grid=(t // tr, d // tc),
                 in_specs=[pl.BlockSpec((tr, tc), lambda i, j: (i, ga_blk + j)),
                           pl.BlockSpec((tr, tc), lambda i, j: (i, gb_blk + j)), blk, blk],
                 out_specs=blk, out_shape=_sds((t, d), BF16), sem=("parallel", "parallel"))(z, z, ya, yb)


def _gate_merge_bwd(z, ya, yb, dm, ga_off, *, name):
    t, d = ya.shape
    tr = _pick(t, (1024, 512, 256, 128, 64, 32, 16, 8))
    tc, ga_blk, gb_blk = _gate_cols(d, ga_off)

    def body(ga_ref, gb_ref, ya_ref, yb_ref, dm_ref, dya_ref, dyb_ref, dga_ref, dgb_ref):
        def piece(rows):
            dm_v = dm_ref[rows, :]
            sa = _sigmoid(ga_ref[rows, :].astype(F32))
            sb = _sigmoid(gb_ref[rows, :].astype(F32))
            dya_ref[rows, :] = (dm_v * sa).astype(dya_ref.dtype)
            dyb_ref[rows, :] = (dm_v * sb).astype(dyb_ref.dtype)
            dga_ref[rows, :] = (dm_v * ya_ref[rows, :] * sa * (1.0 - sa)).astype(dga_ref.dtype)
            dgb_ref[rows, :] = (dm_v * yb_ref[rows, :] * sb * (1.0 - sb)).astype(dgb_ref.dtype)

        _chunked(tr, piece)

    blk = pl.BlockSpec((tr, tc), lambda i, j: (i, j))
    return _call(body, name=name, grid=(t // tr, d // tc),
                 in_specs=[pl.BlockSpec((tr, tc), lambda i, j: (i, ga_blk + j)),
                           pl.BlockSpec((tr, tc), lambda i, j: (i, gb_blk + j)), blk, blk, blk],
                 out_specs=[blk, blk, blk, blk], out_shape=[_sds((t, d), BF16)] * 4,
                 sem=("parallel", "parallel"))(z, z, ya, yb, dm)


def _dwconv_fwd(x, w, b, bsz, halo, *, name):
    t, ch = x.shape
    kw = w.shape[0]
    _, ts, ns = _seq_tiles(t, bsz, (256, 128, 64, 32))
    tc = _pick(ch, (512, 256, 128))
    hb = ts // halo

    def body(x_ref, h_ref, w_ref, b_ref, y_ref, ext_ref):
        ext_ref[pl.ds(0, halo), :] = jnp.where(pl.program_id(1) > 0, h_ref[...], 0.0)
        ext_ref[pl.ds(halo, ts), :] = x_ref[...]
        acc = jnp.zeros((ts, tc), F32) + b_ref[...]
        for k in range(kw):
            acc = acc + w_ref[pl.ds(k, 1), :] * ext_ref[pl.ds(halo - (kw - 1) + k, ts), :]
        y_ref[...] = acc

    cur = pl.BlockSpec((ts, tc), lambda bb, i, j: (bb * ns + i, j))
    prev = pl.BlockSpec((halo, tc), lambda bb, i, j: (jnp.maximum((bb * ns + i) * hb - 1, 0), j))
    return _call(body, name=name, grid=(bsz, ns, ch // tc),
                 in_specs=[cur, prev, pl.BlockSpec((kw, tc), lambda bb, i, j: (0, j)),
                           pl.BlockSpec((1, tc), lambda bb, i, j: (0, j))],
                 out_specs=cur, out_shape=_sds((t, ch), F32), scratch=[pltpu.VMEM((halo + ts, tc), F32)],
                 sem=("parallel", "parallel", "parallel"))(x, x, w, b)


def _dwconv_bwd(x, dy, w, bsz, halo, dx_dtype, *, name):
    t, ch = x.shape
    kw = w.shape[0]
    _, ts, ns = _seq_tiles(t, bsz, (256, 128, 64, 32))
    tc = _pick(ch, (512, 256, 128))
    hb = ts // halo
    last_halo_blk = t // halo - 1

    def body(x_ref, xh_ref, dy_ref, dyh_ref, w_ref, dx_ref, dw_ref, db_ref, extx_ref, exty_ref):
        i = pl.program_id(2)

        @pl.when(jnp.logical_and(pl.program_id(1) == 0, i == 0))
        def _():
            dw_ref[...] = jnp.zeros_like(dw_ref)
            db_ref[...] = jnp.zeros_like(db_ref)

        extx_ref[pl.ds(0, halo), :] = jnp.where(i > 0, xh_ref[...], 0.0)
        extx_ref[pl.ds(halo, ts), :] = x_ref[...]
        dyc = dy_ref[...]
        exty_ref[pl.ds(0, ts), :] = dyc
        exty_ref[pl.ds(ts, halo), :] = jnp.where(i < ns - 1, dyh_ref[...], 0.0)
        acc = jnp.zeros((ts, tc), F32)
        for k in range(kw):
            acc = acc + w_ref[pl.ds(k, 1), :] * exty_ref[pl.ds(kw - 1 - k, ts), :]
            dw_ref[pl.ds(k, 1), :] += jnp.sum(dyc * extx_ref[pl.ds(halo - (kw - 1) + k, ts), :],
                                              axis=0, keepdims=True)
        dx_ref[...] = acc.astype(dx_ref.dtype)
        db_ref[...] += jnp.sum(dyc, axis=0, keepdims=True)

    cur = pl.BlockSpec((ts, tc), lambda j, bb, i: (bb * ns + i, j))
    prev = pl.BlockSpec((halo, tc), lambda j, bb, i: (jnp.maximum((bb * ns + i) * hb - 1, 0), j))
    nxt = pl.BlockSpec((halo, tc), lambda j, bb, i: (jnp.minimum((bb * ns + i + 1) * hb, last_halo_blk), j))
    return _call(body, name=name, grid=(ch // tc, bsz, ns),
                 in_specs=[cur, prev, cur, nxt, pl.BlockSpec((kw, tc), lambda j, bb, i: (0, j))],
                 out_specs=[cur, pl.BlockSpec((kw, tc), lambda j, bb, i: (0, j)),
                            pl.BlockSpec((1, tc), lambda j, bb, i: (0, j))],
                 out_shape=[_sds((t, ch), dx_dtype), _sds((kw, ch), F32), _sds((1, ch), F32)],
                 scratch=[pltpu.VMEM((halo + ts, tc), F32), pltpu.VMEM((ts + halo, tc), F32)],
                 sem=("parallel", "arbitrary", "arbitrary"))(x, x, dy, dy, w)


FFN_ROWS = 64


def _gelu_parts(v):
    cdf = 0.5 * (1.0 + lax.erf(v * (2.0 ** -0.5)))
    return cdf, v * cdf


def _ffn_conv_piece(ext_ref, wb_ref, base):
    win = ext_ref[pl.ds(base, FFN_ROWS + FFN_PAD), :]
    acc = wb_ref[pl.ds(3, 1), :] + wb_ref[pl.ds(2, 1), :] * win[FFN_PAD:]
    acc = acc + wb_ref[pl.ds(1, 1), :] * pltpu.roll(win, 1, axis=0)[FFN_PAD:]
    acc = acc + wb_ref[pl.ds(0, 1), :] * pltpu.roll(win, 2, axis=0)[FFN_PAD:]
    return acc


def _ffn_stage(hg_ref, hl_ref, wg_ref, wl_ref, bg_ref, bl_ref, ext_ref, wb_ref, s, tcf):
    ext_ref[pl.ds(0, FFN_PAD), :] = jnp.zeros((FFN_PAD, 2 * tcf), F32)
    ext_ref[pl.ds(FFN_PAD, s), :tcf] = hg_ref[...].astype(F32)
    ext_ref[pl.ds(FFN_PAD, s), tcf:] = hl_ref[...].astype(F32)
    wb_ref[pl.ds(0, 3), :tcf] = wg_ref[...]
    wb_ref[pl.ds(0, 3), tcf:] = wl_ref[...]
    wb_ref[pl.ds(3, 1), :tcf] = bg_ref[...]
    wb_ref[pl.ds(3, 1), tcf:] = bl_ref[...]


def _ffn_specs(s, tcf, n_f, batch_first):
    def spec(rows, shift):
        if batch_first:
            return pl.BlockSpec((rows, tcf), lambda bb, j: (bb if rows == s else 0, shift + j))
        return pl.BlockSpec((rows, tcf), lambda j, bb: (bb if rows == s else 0, shift + j))

    return [spec(s, 0), spec(s, n_f), spec(3, 0), spec(3, n_f), spec(1, 0), spec(1, n_f)]


def _ffn_act_fwd(hp, w, b, bsz, tcf, *, name):
    t, two_f = hp.shape
    s = t // bsz
    n_f = two_f // (2 * tcf)

    def body(hg_ref, hl_ref, wg_ref, wl_ref, bg_ref, bl_ref, f_ref, ext_ref, wb_ref):
        _ffn_stage(hg_ref, hl_ref, wg_ref, wl_ref, bg_ref, bl_ref, ext_ref, wb_ref, s, tcf)

        def step(i, carry):
            base = pl.multiple_of(i * FFN_ROWS, FFN_ROWS)
            hh = _ffn_conv_piece(ext_ref, wb_ref, base)
            _, gelu = _gelu_parts(hh[:, :tcf])
            f_ref[pl.ds(base, FFN_ROWS), :] = (gelu * hh[:, tcf:]).astype(f_ref.dtype)
            return carry

        lax.fori_loop(0, s // FFN_ROWS, step, 0)

    return _call(body, name=name, grid=(bsz, n_f), in_specs=_ffn_specs(s, tcf, n_f, True),
                 out_specs=pl.BlockSpec((s, tcf), lambda bb, j: (bb, j)),
                 out_shape=_sds((t, two_f // 2), BF16),
                 scratch=[pltpu.VMEM((FFN_PAD + s, 2 * tcf), F32), pltpu.VMEM((SUBLANES, 2 * tcf), F32)],
                 sem=("parallel", "parallel"))(hp, hp, w, w, b, b)


def _ffn_act_bwd(hp, df, w, b, bsz, tcf, *, name):
    t, two_f = hp.shape
    s = t // bsz
    f_dim = two_f // 2
    n_f = f_dim // tcf
    gw = 2 * tcf
    n_rows = FFN_ROWS + FFN_PAD

    def body(hg_ref, hl_ref, wg_ref, wl_ref, bg_ref, bl_ref, df_ref,
             dhg_ref, dhl_ref, dwg_ref, dwl_ref, dbg_ref, dbl_ref, ext_ref, wb_ref, dh_ref):
        @pl.when(pl.program_id(1) == 0)
        def _():
            for ref in (dwg_ref, dwl_ref, dbg_ref, dbl_ref):
                ref[...] = jnp.zeros_like(ref)

        _ffn_stage(hg_ref, hl_ref, wg_ref, wl_ref, bg_ref, bl_ref, ext_ref, wb_ref, s, tcf)
        dh_ref[pl.ds(s, FFN_PAD), :] = jnp.zeros((FFN_PAD, gw), F32)

        def grad_h(i, carry):
            base = pl.multiple_of(i * FFN_ROWS, FFN_ROWS)
            hh = _ffn_conv_piece(ext_ref, wb_ref, base)
            hg = hh[:, :tcf]
            d = df_ref[pl.ds(base, FFN_ROWS), :]
            cdf, gelu = _gelu_parts(hg)
            pdf = jnp.exp(-0.5 * hg * hg) * (1.0 / math.sqrt(2.0 * math.pi))
            dh_ref[pl.ds(base, FFN_ROWS), :tcf] = d * hh[:, tcf:] * (cdf + hg * pdf)
            dh_ref[pl.ds(base, FFN_ROWS), tcf:] = d * gelu
            return carry

        lax.fori_loop(0, s // FFN_ROWS, grad_h, 0)

        def grad_x(i, carry):
            dw0, dw1, dw2, dbs = carry
            base = pl.multiple_of(i * FFN_ROWS, FFN_ROWS)
            nxt = dh_ref[pl.ds(base, n_rows), :]
            dyc = nxt[:FFN_ROWS]
            dx = wb_ref[pl.ds(2, 1), :] * dyc
            dx = dx + wb_ref[pl.ds(1, 1), :] * pltpu.roll(nxt, n_rows - 1, axis=0)[:FFN_ROWS]
            dx = dx + wb_ref[pl.ds(0, 1), :] * pltpu.roll(nxt, n_rows - 2, axis=0)[:FFN_ROWS]
            dhg_ref[pl.ds(base, FFN_ROWS), :] = dx[:, :tcf].astype(dhg_ref.dtype)
            dhl_ref[pl.ds(base, FFN_ROWS), :] = dx[:, tcf:].astype(dhl_ref.dtype)
            win = ext_ref[pl.ds(base, n_rows), :]
            dw2 = dw2 + jnp.sum(dyc * win[FFN_PAD:], axis=0, keepdims=True)
            dw1 = dw1 + jnp.sum(dyc * pltpu.roll(win, 1, axis=0)[FFN_PAD:], axis=0, keepdims=True)
            dw0 = dw0 + jnp.sum(dyc * pltpu.roll(win, 2, axis=0)[FFN_PAD:], axis=0, keepdims=True)
            return dw0, dw1, dw2, dbs + jnp.sum(dyc, axis=0, keepdims=True)

        zero = jnp.zeros((1, gw), F32)
        sums = lax.fori_loop(0, s // FFN_ROWS, grad_x, (zero, zero, zero, zero))
        for k in range(3):
            dwg_ref[pl.ds(k, 1), :] += sums[k][:, :tcf]
            dwl_ref[pl.ds(k, 1), :] += sums[k][:, tcf:]
        dbg_ref[...] += sums[3][:, :tcf]
        dbl_ref[...] += sums[3][:, tcf:]

    half = pl.BlockSpec((s, tcf), lambda j, bb: (bb, j))
    taps = pl.BlockSpec((3, tcf), lambda j, bb: (0, j))
    bias = pl.BlockSpec((1, tcf), lambda j, bb: (0, j))
    return _call(body, name=name, grid=(n_f, bsz), in_specs=_ffn_specs(s, tcf, n_f, False) + [half],
                 out_specs=[half, half, taps, taps, bias, bias],
                 out_shape=[_sds((t, f_dim), BF16)] * 2 + [_sds((3, f_dim), F32)] * 2 + [_sds((1, f_dim), F32)] * 2,
                 scratch=[pltpu.VMEM((FFN_PAD + s, gw), F32), pltpu.VMEM((SUBLANES, gw), F32),
                          pltpu.VMEM((s + FFN_PAD, gw), F32)],
                 sem=("parallel", "arbitrary"))(hp, hp, w, w, b, b, df)


def _split3(v):
    hi = v.astype(BF16)
    r = v - hi.astype(F32)
    mid = r.astype(BF16)
    lo = (r - mid.astype(F32)).astype(BF16)
    return hi, mid, lo


def _tri_dot(tri, v):
    out = None
    for part in _split3(v):
        term = jnp.dot(tri, part, preferred_element_type=F32)
        out = term if out is None else out + term
    return out


def _fgate_fwd(zf, bsz, heads, *, name):
    t, lanes = zf.shape
    s, blk, nb = _seq_tiles(t, bsz, (ATTN_BLOCK, 128))

    def body(z_ref, cumt_ref, cumb_ref, carry_ref):
        @pl.when(pl.program_id(1) == 0)
        def _():
            carry_ref[...] = jnp.zeros_like(carry_ref)

        z = z_ref[...]
        lf = jnp.minimum(z, 0.0) - jnp.log1p(jnp.exp(-jnp.abs(z)))
        r = lax.broadcasted_iota(jnp.int32, (blk, blk), 0)
        c = lax.broadcasted_iota(jnp.int32, (blk, blk), 1)
        tri = (r >= c).astype(BF16)
        cum = _tri_dot(tri, lf) + carry_ref[...]
        carry_ref[...] = cum[blk - 1:blk, :]
        cumt_ref[0] = jnp.transpose(cum)[:heads, :]
        for h in range(heads):
            cumb_ref[0, h] = jnp.broadcast_to(cum[:, h:h + 1], (blk, lanes))

    return _call(body, name=name, grid=(bsz, nb),
                 in_specs=[pl.BlockSpec((blk, lanes), lambda b, i: (b * nb + i, 0))],
                 out_specs=[pl.BlockSpec((1, heads, blk), lambda b, i: (b, 0, i)),
                            pl.BlockSpec((1, heads, blk, lanes), lambda b, i: (b, 0, i, 0))],
                 out_shape=[_sds((bsz, heads, s), F32), _sds((bsz, heads, s, lanes), F32)],
                 scratch=[pltpu.VMEM((1, lanes), F32)], sem=("parallel", "arbitrary"))(zf)


def _fgate_bwd(dcum, zf, bsz, *, name):
    t, lanes = zf.shape
    pairs = dcum.shape[1]
    s, blk, nb = _seq_tiles(t, bsz, (ATTN_BLOCK, 128))

    def body(d_ref, z_ref, o_ref, carry_ref):
        @pl.when(pl.program_id(1) == 0)
        def _():
            carry_ref[...] = jnp.zeros_like(carry_ref)

        dcol = d_ref[0, 0]
        for p in range(1, pairs):
            dcol = dcol + d_ref[0, p]
        r = lax.broadcasted_iota(jnp.int32, (blk, blk), 0)
        c = lax.broadcasted_iota(jnp.int32, (blk, blk), 1)
        tri = (c >= r).astype(BF16)
        suf = _tri_dot(tri, dcol) + carry_ref[...]
        carry_ref[...] = suf[0:1, :]
        o_ref[...] = suf * _sigmoid(-z_ref[...])

    return _call(body, name=name, grid=(bsz, nb),
                 in_specs=[pl.BlockSpec((1, pairs, blk, lanes), lambda b, i: (b, 0, nb - 1 - i, 0)),
                           pl.BlockSpec((blk, lanes), lambda b, i: (b * nb + nb - 1 - i, 0))],
                 out_specs=pl.BlockSpec((blk, lanes), lambda b, i: (b * nb + nb - 1 - i, 0)),
                 out_shape=_sds((t, lanes), F32), scratch=[pltpu.VMEM((1, lanes), F32)],
                 sem=("parallel", "arbitrary"))(dcum, zf)


def _to_features_major(z, col_off, width, n, *, name):
    t = z.shape[0]
    tr = _pick(t, (512, 256, 128))
    first = col_off // width

    def body(*refs):
        o_ref = refs[n]
        for g in range(n):
            o_ref[pl.ds(g * width, width), :] = jnp.transpose(refs[g][...].astype(F32)).astype(o_ref.dtype)

    return _call(body, name=name, grid=(t // tr,),
                 in_specs=[pl.BlockSpec((tr, width), lambda i, g=g: (i, first + g)) for g in range(n)],
                 out_specs=pl.BlockSpec((n * width, tr), lambda i: (0, i)),
                 out_shape=_sds((n * width, t), BF16), sem=("parallel",))(*([z] * n))


def _to_rows_major(xt, *, name):
    w, t = xt.shape
    tr = _pick(t, (512, 256, 128))

    def body(x_ref, o_ref):
        o_ref[...] = jnp.transpose(x_ref[...]).astype(o_ref.dtype)

    return _call(body, name=name, grid=(t // tr,),
                 in_specs=[pl.BlockSpec((w, tr), lambda i: (0, i))],
                 out_specs=pl.BlockSpec((tr, w), lambda i: (i, 0)),
                 out_shape=_sds((t, w), BF16), sem=("parallel",))(xt)


def _head_masks(shape, axis):
    feat = lax.broadcasted_iota(jnp.int32, shape, axis)
    return feat < HEAD_DIM, feat >= HEAD_DIM


def _attn_fwd(z, qkvt, cumt, cumb, bsz, heads, q_off, *, name):
    t = z.shape[0]
    width = heads * HEAD_DIM
    pairs = heads // 2
    s = t // bsz
    blk = ATTN_BLOCK
    nq = s // blk
    k_col = (q_off + width) // LANES
    v_row = 2 * width // LANES
    reps = blk // LANES

    def body(k_ref, qt_ref, vt_ref, cqt_ref, ckb_ref, ot_ref, lse_ref):
        p_id = pl.program_id(1)
        i = pl.program_id(2)
        qt = qt_ref[...]
        masks = _head_masks((LANES, blk), 0)
        qtm = [jnp.where(mk, qt, jnp.zeros_like(qt)) for mk in masks]
        cq = [cqt_ref[0, pl.ds(2 * p_id + hh, 1), :] for hh in range(2)]
        kidx = lax.broadcasted_iota(jnp.int32, (blk, blk), 0)
        qidx = lax.broadcasted_iota(jnp.int32, (blk, blk), 1)

        def block(j, carry, masked):
            off = pl.multiple_of(j * blk, blk)
            kp = k_ref[pl.ds(off, blk), :].astype(BF16)
            vtp = vt_ref[:, pl.ds(off, blk)]
            out = []
            for hh in range(2):
                m, l, acc = carry[hh]
                sc = jnp.dot(kp, qtm[hh], preferred_element_type=F32) * ATTN_SCALE
                ck = ckb_ref[0, hh, pl.ds(off, blk), :]
                sc = (sc + cq[hh]) - jnp.concatenate([ck] * reps, axis=1)
                if masked:
                    sc = jnp.where(qidx >= kidx, sc, NEG)
                m_new = jnp.maximum(m, jnp.max(sc, axis=0, keepdims=True))
                pr = jnp.exp(sc - m_new)
                a = jnp.exp(m - m_new)
                l = a * l + jnp.sum(pr, axis=0, keepdims=True)
                p_hi = pr.astype(BF16)
                p_lo = (pr - p_hi.astype(F32)).astype(BF16)
                pv = (jnp.dot(vtp, p_hi, preferred_element_type=F32)
                      + jnp.dot(vtp, p_lo, preferred_element_type=F32))
                acc = a * acc + pv[hh * HEAD_DIM:(hh + 1) * HEAD_DIM]
                out.append((m_new, l, acc))
            return tuple(out)

        init = tuple((jnp.full((1, blk), NEG, F32), jnp.zeros((1, blk), F32), jnp.zeros((HEAD_DIM, blk), F32))
                     for _ in range(2))
        carry = lax.fori_loop(0, i, lambda j, cr: block(j, cr, False), init)
        carry = block(i, carry, True)
        lse_ref[...] = jnp.zeros_like(lse_ref)
        for hh in range(2):
            m, l, acc = carry[hh]
            ot_ref[pl.ds(hh * HEAD_DIM, HEAD_DIM), :] = acc / l
            lse_ref[0, 0, pl.ds(hh, 1), :] = m + jnp.log(l)

    return _call(body, name=name, grid=(bsz, pairs, nq),
                 in_specs=[pl.BlockSpec((s, LANES), lambda b, p, i: (b, k_col + p)),
                           pl.BlockSpec((LANES, blk), lambda b, p, i: (p, b * nq + i)),
                           pl.BlockSpec((LANES, s), lambda b, p, i: (v_row + p, b)),
                           pl.BlockSpec((1, heads, blk), lambda b, p, i: (b, 0, i)),
                           pl.BlockSpec((1, 2, s, LANES), lambda b, p, i: (b, p, 0, 0))],
                 out_specs=[pl.BlockSpec((LANES, blk), lambda b, p, i: (p, b * nq + i)),
                            pl.BlockSpec((1, 1, SUBLANES, blk), lambda b, p, i: (b, p, 0, i))],
                 out_shape=[_sds((width, t), F32), _sds((bsz, pairs, SUBLANES, s), F32)],
                 sem=("parallel", "parallel", "parallel"))(z, qkvt, qkvt, cumt, cumb)


def _attn_bwd(z, qkvt, cumt, cumb, ot, do, dot, lse, bsz, heads, q_off, *, name):
    t = z.shape[0]
    width = heads * HEAD_DIM
    pairs = heads // 2
    s = t // bsz
    blk = ATTN_BLOCK
    nkv = s // blk
    q_col = q_off // LANES
    k_col = (q_off + width) // LANES
    v_col = (q_off + 2 * width) // LANES
    k_row = width // LANES
    reps = blk // LANES

    def body(k_ref, v_ref, kt_ref, q_ref, qt_ref, do_ref, dot_ref, ot_ref, lse_ref, ckb_ref, cqt_ref,
             dk_ref, dv_ref, dqt_ref, dcum_ref, dqt_acc, ds_acc):
        p_id = pl.program_id(1)
        j = pl.program_id(2)

        @pl.when(j == 0)
        def _():
            dqt_acc[...] = jnp.zeros_like(dqt_acc)

        kp = k_ref[...].astype(BF16)
        vp = v_ref[...].astype(BF16)
        kt = kt_ref[...]
        feat_masks = _head_masks((LANES, blk), 0)
        lane_masks = _head_masks((blk, LANES), 1)
        ktm = [jnp.where(mk, kt, jnp.zeros_like(kt)) for mk in feat_masks]
        ck = [jnp.concatenate([ckb_ref[0, hh]] * reps, axis=1) for hh in range(2)]
        kidx = lax.broadcasted_iota(jnp.int32, (blk, blk), 0)
        qidx = lax.broadcasted_iota(jnp.int32, (blk, blk), 1)
        ds_acc[...] = jnp.zeros_like(ds_acc)

        def block(i, carry, masked):
            dk, dv = carry
            off = pl.multiple_of(i * blk, blk)
            qt = qt_ref[:, pl.ds(off, blk)]
            dt = dot_ref[:, pl.ds(off, blk)]
            o_t = ot_ref[:, pl.ds(off, blk)]
            q_rows = q_ref[pl.ds(off, blk), :].astype(BF16)
            do_rows = do_ref[pl.ds(off, blk), :]
            for hh in range(2):
                qtm = jnp.where(feat_masks[hh], qt, jnp.zeros_like(qt))
                dtm = jnp.where(feat_masks[hh], dt, jnp.zeros_like(dt))
                sc = jnp.dot(kp, qtm, preferred_element_type=F32) * ATTN_SCALE
                sc = (sc + cqt_ref[0, pl.ds(2 * p_id + hh, 1), pl.ds(off, blk)]) - ck[hh]
                pr = jnp.exp(sc - lse_ref[0, 0, pl.ds(hh, 1), pl.ds(off, blk)])
                if masked:
                    pr = jnp.where(qidx >= kidx, pr, 0.0)
                dp = jnp.dot(vp, dtm, preferred_element_type=F32)
                delta = jnp.sum(dtm.astype(F32) * o_t, axis=0, keepdims=True)
                ds = pr * (dp - delta)
                ds_acc[hh] += ds
                dsb = ds.astype(BF16)
                qm = jnp.where(lane_masks[hh], q_rows, jnp.zeros_like(q_rows))
                dom = jnp.where(lane_masks[hh], do_rows, jnp.zeros_like(do_rows))
                dv = dv + jnp.dot(pr.astype(BF16), dom, preferred_element_type=F32)
                dk = dk + jnp.dot(dsb, qm, preferred_element_type=F32) * ATTN_SCALE
                dqt_acc[:, pl.ds(off, blk)] += jnp.dot(ktm[hh], dsb, preferred_element_type=F32) * ATTN_SCALE
            return dk, dv

        zero = jnp.zeros((blk, LANES), F32)
        carry = block(j, (zero, zero), True)
        dk, dv = lax.fori_loop(j + 1, nkv, lambda i, cr: block(i, cr, False), carry)
        dk_ref[...] = dk.astype(dk_ref.dtype)
        dv_ref[...] = dv.astype(dv_ref.dtype)
        lane = lax.broadcasted_iota(jnp.int32, (blk, LANES), 1)
        dcum = jnp.zeros((blk, LANES), F32)
        for hh in range(2):
            col = jnp.sum(ds_acc[hh], axis=1, keepdims=True)
            dcum = jnp.where(lane == 2 * p_id + hh, -col, dcum)
        dcum_ref[0, 0] = dcum

        @pl.when(j == nkv - 1)
        def _():
            dqt_ref[...] = dqt_acc[...]

    key_rows = lambda col: pl.BlockSpec((blk, LANES), lambda b, p, j: (b * nkv + j, col + p))
    seq_t = lambda row: pl.BlockSpec((LANES, s), lambda b, p, j: (row + p, b))
    return _call(body, name=name, grid=(bsz, pairs, nkv),
                 in_specs=[key_rows(k_col), key_rows(v_col),
                           pl.BlockSpec((LANES, blk), lambda b, p, j: (k_row + p, b * nkv + j)),
                           pl.BlockSpec((s, LANES), lambda b, p, j: (b, q_col + p)), seq_t(0),
                           pl.BlockSpec((s, LANES), lambda b, p, j: (b, p)), seq_t(0), seq_t(0),
                           pl.BlockSpec((1, 1, SUBLANES, s), lambda b, p, j: (b, p, 0, 0)),
                           pl.BlockSpec((1, 2, blk, LANES), lambda b, p, j: (b, p, j, 0)),
                           pl.BlockSpec((1, heads, s), lambda b, p, j: (b, 0, 0))],
                 out_specs=[key_rows(0), key_rows(0), seq_t(0),
                            pl.BlockSpec((1, 1, blk, LANES), lambda b, p, j: (b, p, j, 0))],
                 out_shape=[_sds((t, width), BF16), _sds((t, width), BF16), _sds((width, t), F32),
                            _sds((bsz, pairs, s, LANES), F32)],
                 scratch=[pltpu.VMEM((LANES, s), F32), pltpu.VMEM((2, blk, blk), F32)],
                 sem=("parallel", "parallel", "arbitrary"))(z, z, qkvt, z, qkvt, do, dot, ot, lse, cumb, cumt)


def _adamw(w, g, m, v, *, name):
    bc1 = 1.0 - ADAM_B1 ** ADAM_STEP
    bc2 = 1.0 - ADAM_B2 ** ADAM_STEP

    def body(w_ref, g_ref, m_ref, v_ref, d_ref, nm_ref, nv_ref):
        g_v = g_ref[...]
        nm = ADAM_B1 * m_ref[...] + (1.0 - ADAM_B1) * g_v
        nv = ADAM_B2 * v_ref[...] + (1.0 - ADAM_B2) * (g_v * g_v)
        nm_ref[...] = nm
        nv_ref[...] = nv
        d_ref[...] = -ADAM_LR * ((nm / bc1) / (jnp.sqrt(nv / bc2) + ADAM_EPS) + ADAM_WD * w_ref[...])

    if w.ndim == 2:
        grid = (1,)
        blk = pl.BlockSpec(w.shape, lambda i: (0, 0))
    else:
        layers, rows, cols = w.shape
        tr = rows if rows <= 256 else _pick(rows, (256, 128, 64, 32, 16, 8))
        grid = (layers, rows // tr)
        blk = pl.BlockSpec((1, tr, cols), lambda layer, i: (layer, i, 0))
    return tuple(_call(body, name=name, grid=grid, in_specs=[blk] * 4, out_specs=[blk] * 3,
                       out_shape=[_sds(w.shape, F32)] * 3, sem=("parallel",) * len(grid))(w, g, m, v))


_ANY = pl.BlockSpec(memory_space=pl.ANY)


def _comm_call(body, *, name, n_in, out_shape, n_sems):
    scratch = [pltpu.SemaphoreType.DMA((n_sems,)), pltpu.SemaphoreType.DMA((n_sems,)),
               pltpu.SemaphoreType.DMA((len(out_shape),))]
    return pl.pallas_call(body, name=name, in_specs=[_ANY] * n_in, out_specs=[_ANY] * len(out_shape),
                          out_shape=out_shape, scratch_shapes=scratch)


def _place():
    x, y, c = lax.axis_index("x"), lax.axis_index("y"), lax.axis_index("c")
    return x, y, c, [(1 - x, y), (x, 1 - y), (1 - x, 1 - y)]


def _remote(src, dst, send_sems, recv_sems, sem, to):
    return pltpu.make_async_remote_copy(src_ref=src, dst_ref=dst, send_sem=send_sems.at[sem],
                                        recv_sem=recv_sems.at[sem], device_id=to, device_id_type=MESH)


def _all_gather8(v, *, name):
    def body(v_ref, out_ref, send_sems, recv_sems, local_sems):
        x, y, c, _ = _place()
        me = 4 * x + 2 * y + c
        mine = pltpu.make_async_copy(v_ref, out_ref.at[me], local_sems.at[0])
        mine.start()
        peers = []
        for k in range(1, N_DEVICES):
            px = 1 - x if k & 4 else x
            py = 1 - y if k & 2 else y
            pc = 1 - c if k & 1 else c
            peers.append((px, py, pc))
        sends = [_remote(v_ref, out_ref.at[me], send_sems, recv_sems, k, peer) for k, peer in enumerate(peers)]
        for cp in sends:
            cp.start()
        for k, (px, py, pc) in enumerate(peers):
            _remote(v_ref, out_ref.at[4 * px + 2 * py + pc], send_sems, recv_sems, k, (px, py, pc)).wait_recv()
        for cp in sends:
            cp.wait_send()
        mine.wait()

    out = _comm_call(body, name=name, n_in=1, out_shape=[_sds((N_DEVICES,) + v.shape, v.dtype)],
                     n_sems=N_DEVICES - 1)(v)
    return out[0]


def _window(ref, mode, layer, chip, rows, cols):
    if mode == "slab":
        return ref.at[layer, chip]
    if mode == "cols":
        return ref.at[layer, :, pl.ds(pl.multiple_of(chip * cols, LANES), cols)]
    return ref.at[layer, pl.ds(pl.multiple_of(chip * rows, SUBLANES), rows), :]


def _whole_shape(mode, shard_shape):
    layers, rows, cols = shard_shape
    if mode == "slab":
        return (layers, N_CHIPS, rows, cols)
    if mode == "cols":
        assert cols % LANES == 0
        return (layers, rows, N_CHIPS * cols)
    assert rows % 16 == 0
    return (layers, N_CHIPS * rows, cols)


def _gather_weights(shards, modes, *, name):
    n = len(shards)
    meta = [(mode,) + tuple(a.shape[1:]) for a, mode in zip(shards, modes)]
    for a in shards:
        assert a.shape[0] == 2

    def body(*refs):
        ins, outs = refs[:n], refs[n:2 * n]
        send_sems, recv_sems, local_sems = refs[2 * n:]
        x, y, c, chips = _place()
        me = 2 * x + y
        local, first, passed = [], [], []
        for i, (mode, rows, cols) in enumerate(meta):
            cp = pltpu.make_async_copy(ins[i], _window(outs[i], mode, slice(None), me, rows, cols), local_sems.at[i])
            cp.start()
            local.append(cp)
            for r, (cx, cy) in enumerate(chips):
                cp = _remote(ins[i].at[c], _window(outs[i], mode, c, me, rows, cols), send_sems, recv_sems,
                             6 * i + r, (cx, cy, c))
                cp.start()
                first.append(cp)
        for i, (mode, rows, cols) in enumerate(meta):
            for r, (cx, cy) in enumerate(chips):
                win = _window(outs[i], mode, c, 2 * cx + cy, rows, cols)
                _remote(win, win, send_sems, recv_sems, 6 * i + r, (cx, cy, c)).wait_recv()
                cp = _remote(win, win, send_sems, recv_sems, 6 * i + 3 + r, (x, y, 1 - c))
                cp.start()
                passed.append(cp)
        for i, (mode, rows, cols) in enumerate(meta):
            for r, (cx, cy) in enumerate(chips):
                win = _window(outs[i], mode, 1 - c, 2 * cx + cy, rows, cols)
                _remote(win, win, send_sems, recv_sems, 6 * i + 3 + r, (x, y, 1 - c)).wait_recv()
        for cp in first + passed:
            cp.wait_send()
        for cp in local:
            cp.wait()

    out_shape = [_sds(_whole_shape(mode, a.shape), a.dtype) for a, mode in zip(shards, modes)]
    return _comm_call(body, name=name, n_in=n, out_shape=out_shape, n_sems=6 * n)(*shards)


def _rs_swap(grads, *, name):
    n = len(grads)

    def body(*refs):
        ins, outs = refs[:n], refs[n:2 * n]
        send_sems, recv_sems, _ = refs[2 * n:]
        x, y, c, _ = _place()
        copies = [_remote(ins[i].at[1 - c], outs[i], send_sems, recv_sems, i, (x, y, 1 - c)) for i in range(n)]
        for cp in copies:
            cp.start()
        for cp in copies:
            cp.wait()

    return _comm_call(body, name=name, n_in=n, out_shape=[_sds(g.shape[1:], g.dtype) for g in grads], n_sems=n)(*grads)


def _part(ref, mode, chip, rows, cols):
    if mode == "slab":
        return ref.at[chip]
    if mode == "cols":
        return ref.at[:, pl.ds(pl.multiple_of(chip * cols, LANES), cols)]
    return ref.at[pl.ds(pl.multiple_of(chip * rows, SUBLANES), rows), :]


def _rs_scatter(parts, modes, shard_shapes, *, name):
    n = len(parts)
    meta = [(mode,) + tuple(shp[1:]) for mode, shp in zip(modes, shard_shapes)]

    def body(*refs):
        ins, outs = refs[:n], refs[n:2 * n]
        send_sems, recv_sems, local_sems = refs[2 * n:]
        x, y, c, chips = _place()
        me = 2 * x + y
        local, sends = [], []
        for i, (mode, rows, cols) in enumerate(meta):
            cp = pltpu.make_async_copy(_part(ins[i], mode, me, rows, cols), outs[i].at[me], local_sems.at[i])
            cp.start()
            local.append(cp)
            for r, (cx, cy) in enumerate(chips):
                cp = _remote(_part(ins[i], mode, 2 * cx + cy, rows, cols), outs[i].at[me], send_sems, recv_sems,
                             3 * i + r, (cx, cy, c))
                cp.start()
                sends.append(cp)
        for i, (mode, rows, cols) in enumerate(meta):
            for r, (cx, cy) in enumerate(chips):
                k = 2 * cx + cy
                _remote(_part(ins[i], mode, k, rows, cols), outs[i].at[k], send_sems, recv_sems, 3 * i + r,
                        (cx, cy, c)).wait_recv()
        for cp in sends:
            cp.wait_send()
        for cp in local:
            cp.wait()

    out_shape = [_sds((N_CHIPS,) + tuple(shp[1:]), p.dtype) for p, shp in zip(parts, shard_shapes)]
    return _comm_call(body, name=name, n_in=n, out_shape=out_shape, n_sems=3 * n)(*parts)


def _rs_gather(sums, *, name):
    n = len(sums)

    def body(*refs):
        ins, outs = refs[:n], refs[n:2 * n]
        send_sems, recv_sems, local_sems = refs[2 * n:]
        x, y, c, _ = _place()
        local, sends = [], []
        for i in range(n):
            cp = pltpu.make_async_copy(ins[i], outs[i].at[c], local_sems.at[i])
            cp.start()
            local.append(cp)
            cp = _remote(ins[i], outs[i].at[c], send_sems, recv_sems, i, (x, y, 1 - c))
            cp.start()
            sends.append(cp)
        for i in range(n):
            _remote(ins[i], outs[i].at[1 - c], send_sems, recv_sems, i, (x, y, 1 - c)).wait_recv()
        for cp in sends:
            cp.wait_send()
        for cp in local:
            cp.wait()

    return _comm_call(body, name=name, n_in=n, out_shape=[_sds((2,) + s.shape, s.dtype) for s in sums], n_sems=n)(*sums)


def _row_tile(rows, cols, itemsize):
    target = max(SUBLANES, (2 << 20) // (cols * itemsize))
    cands = [c for c in (2048, 1024, 512, 256, 128, 64, 32, 16) if c <= target]
    tr = _pick(rows, cands)
    return tr


def _add_layer(g, other, core, *, name):
    _, rows, cols = g.shape
    tr = _row_tile(rows, cols, 4)

    def body(core_ref, g_ref, o_ref, out_ref):
        out_ref[...] = (g_ref[0] + o_ref[...]).astype(out_ref.dtype)

    grid_spec = pltpu.PrefetchScalarGridSpec(
        num_scalar_prefetch=1, grid=(rows // tr,),
        in_specs=[pl.BlockSpec((1, tr, cols), lambda i, core_ref: (core_ref[0], i, 0)),
                  pl.BlockSpec((tr, cols), lambda i, core_ref: (i, 0))],
        out_specs=pl.BlockSpec((tr, cols), lambda i, core_ref: (i, 0)))
    return pl.pallas_call(body, name=name, grid_spec=grid_spec, out_shape=_sds((rows, cols), BF16),
                          compiler_params=pltpu.CompilerParams(dimension_semantics=("parallel",),
                                                               vmem_limit_bytes=VMEM_LIMIT))(core, g, other)


def _sum_slots(parts, *, name):
    n, rows, cols = parts.shape
    tr = _row_tile(rows, cols, 4)

    def body(p_ref, o_ref):
        acc = p_ref[0].astype(F32) + p_ref[1].astype(F32)
        for k in range(2, n):
            acc = acc + p_ref[k].astype(F32)
        o_ref[...] = acc

    return _call(body, name=name, grid=(rows // tr,),
                 in_specs=[pl.BlockSpec((n, tr, cols), lambda i: (0, i, 0))],
                 out_specs=pl.BlockSpec((tr, cols), lambda i: (i, 0)),
                 out_shape=_sds((rows, cols), F32), sem=("parallel",))(parts)


def _reduce_scatter(grads, modes, shard_shapes):
    core = lax.axis_index("c").astype(jnp.int32).reshape(1)
    flat = [g.reshape(g.shape[0], -1, g.shape[-1]) for g in grads]
    from_sibling = _rs_swap(flat, name="rs_swap")
    parts = []
    for i, (g, o) in enumerate(zip(flat, from_sibling)):
        p = _add_layer(g, o, core, name=f"rs_add_{i}")
        parts.append(p.reshape(grads[i].shape[1:]))
    from_chips = _rs_scatter(parts, modes, shard_shapes, name="rs_scatter")
    sums = [_sum_slots(r, name=f"rs_sum_{i}") for i, r in enumerate(from_chips)]
    return _rs_gather(sums, name="rs_gather")


def _layer_weights(full, rep, layer, dims):
    f_off, n_heads = dims["f_off"], dims["heads"]
    d_ff = full["w_ffn_up"].shape[-1] // 2
    w_in = full["w_in"][layer]
    b_in = rep["b_in"][layer]
    pad = LANES - n_heads
    return {
        "w_main": jnp.concatenate([w_in[:, :f_off], w_in[:, f_off + n_heads:]], axis=1),
        "b_main": jnp.concatenate([b_in[:f_off], b_in[f_off + n_heads:]])[None],
        "w_f": jnp.pad(w_in[:, f_off:f_off + n_heads], ((0, 0), (0, pad))),
        "b_f": jnp.pad(b_in[f_off:f_off + n_heads], (0, pad))[None],
        "conv_a_w": full["conv_a_w"][layer],
        "conv_a_b": rep["conv_a_b"][layer][None],
        "ln_conv_g": rep["ln_conv_g"][layer][None],
        "ln_conv_b": rep["ln_conv_b"][layer][None],
        "w_conv_proj": full["w_conv_proj"][layer],
        "w_attn_proj": full["w_attn_proj"][layer],
        "w_mix_out": full["w_mix_out"][layer],
        "b_mix_out": rep["b_mix_out"][layer][None],
        "ln1_g": rep["ln1_g"][layer][None],
        "ln1_b": rep["ln1_b"][layer][None],
        "w_ffn_up": full["w_ffn_up"][layer],
        "w_ffn_up_gate": full["w_ffn_up"][layer][:, :d_ff],
        "w_ffn_up_lin": full["w_ffn_up"][layer][:, d_ff:],
        "ffn_conv_w": full["ffn_conv_w"][layer],
        "ffn_conv_b": rep["ffn_conv_b"][layer][None],
        "w_ffn_down": full["w_ffn_down"][layer],
        "ln2_g": rep["ln2_g"][layer][None],
        "ln2_b": rep["ln2_b"][layer][None],
    }


def _layer_fwd(x, mod, p, dims, tag):
    bsz, d, ch, heads, alpha = dims["bsz"], dims["d"], dims["ch"], dims["heads"], dims["alpha"]
    mods = [mod[:, k * d:(k + 1) * d][:, None, :] for k in range(6)]
    shift1, scale1, gate1, shift2, scale2, gate2 = mods
    u = _ln_mod_fwd(x, scale1, shift1, bsz, name=f"ln_mod1_{tag}")
    zm = _matmul(u, p["w_main"], "nn", BF16, bias=p["b_main"], name=f"in_main_{tag}")
    zf = _matmul(u, p["w_f"], "nn", F32, bias=p["b_f"], name=f"in_forget_{tag}")
    a0 = _glu_fwd(zm, ch, name=f"glu_{tag}")
    a1 = _dwconv_fwd(a0, p["conv_a_w"], p["conv_a_b"], bsz, CONV_A_HALO, name=f"conv_a_{tag}")
    a3 = _lnsilu_fwd(a1, p["ln_conv_g"], p["ln_conv_b"], name=f"lnsilu_{tag}")
    ya = _matmul(a3, p["w_conv_proj"], "nn", F32, name=f"conv_proj_{tag}")
    cumt, cumb = _fgate_fwd(zf, bsz, heads, name=f"fgate_{tag}")
    qkvt = _to_features_major(zm, 2 * ch, heads * HEAD_DIM, 3, name=f"qkv_t_{tag}")
    ot, lse = _attn_fwd(zm, qkvt, cumt, cumb, bsz, heads, 2 * ch, name=f"attn_{tag}")
    yb = _matmul(ot, p["w_attn_proj"], "tn", F32, name=f"attn_proj_{tag}")
    m = _gate_merge_fwd(zm, ya, yb, dims["ga_off"], name=f"merge_{tag}")
    mix = _matmul(m, p["w_mix_out"], "nn", F32, bias=p["b_mix_out"], name=f"mix_out_{tag}")
    x1 = _ln_res_fwd(x, mix, gate1, p["ln1_g"], p["ln1_b"], alpha, bsz, name=f"ln_res1_{tag}")
    u2 = _ln_mod_fwd(x1, scale2, shift2, bsz, name=f"ln_mod2_{tag}")
    hp = _matmul(u2, p["w_ffn_up"], "nn", BF16, name=f"ffn_up_{tag}")
    f = _ffn_act_fwd(hp, p["ffn_conv_w"], p["ffn_conv_b"], bsz, dims["tcf"], name=f"ffn_act_{tag}")
    ffn = _matmul(f, p["w_ffn_down"], "nn", F32, name=f"ffn_down_{tag}")
    x2 = _ln_res_fwd(x1, ffn, gate2, p["ln2_g"], p["ln2_b"], alpha, bsz, name=f"ln_res2_{tag}")
    saved = dict(x=x, mods=mods, u=u, zm=zm, zf=zf, a0=a0, a1=a1, a3=a3, ya=ya, yb=yb, cumt=cumt, cumb=cumb,
                 qkvt=qkvt, ot=ot, lse=lse, m=m, mix=mix, x1=x1, u2=u2, hp=hp, f=f, ffn=ffn)
    return x2, saved


def _layer_bwd(dx2, p, sv, dims, tag):
    bsz, ch, heads, alpha = dims["bsz"], dims["ch"], dims["heads"], dims["alpha"]
    f_off, tcf = dims["f_off"], dims["tcf"]
    shift1, scale1, gate1, shift2, scale2, gate2 = sv["mods"]
    g = {}
    dr2, dffn, dgate2, g["ln2_g"], g["ln2_b"], _ = _ln_res_bwd(
        dx2, sv["x1"], sv["ffn"], gate2, p["ln2_g"], alpha, bsz, name=f"ln_res2_bwd_{tag}")
    df = _matmul(dffn, p["w_ffn_down"], "nt", F32, name=f"ffn_down_dx_{tag}")
    g["w_ffn_down"] = _matmul(sv["f"], dffn, "tn", F32, name=f"ffn_down_dw_{tag}")
    dhg, dhl, dwg, dwl, dbg, dbl = _ffn_act_bwd(sv["hp"], df, p["ffn_conv_w"], p["ffn_conv_b"], bsz, tcf,
                                                name=f"ffn_act_bwd_{tag}")
    g["ffn_conv_w"] = jnp.concatenate([dwg, dwl], axis=1)
    g["ffn_conv_b"] = jnp.concatenate([dbg, dbl], axis=1)[0]
    du2 = _matmul(dhg, p["w_ffn_up_gate"], "nt", F32, name=f"ffn_up_gate_dx_{tag}")
    du2 = _matmul(dhl, p["w_ffn_up_lin"], "nt", F32, add=du2, name=f"ffn_up_lin_dx_{tag}")
    g["w_ffn_up"] = jnp.concatenate([_matmul(sv["u2"], dhg, "tn", F32, name=f"ffn_up_gate_dw_{tag}"),
                                     _matmul(sv["u2"], dhl, "tn", F32, name=f"ffn_up_lin_dw_{tag}")], axis=1)
    dx1, dscale2, dshift2 = _ln_mod_bwd(du2, sv["x1"], scale2, dr2, alpha, bsz, name=f"ln_mod2_bwd_{tag}")
    dr1, dmix, dgate1, g["ln1_g"], g["ln1_b"], g["b_mix_out"] = _ln_res_bwd(
        dx1, sv["x"], sv["mix"], gate1, p["ln1_g"], alpha, bsz, name=f"ln_res1_bwd_{tag}")
    dm = _matmul(dmix, p["w_mix_out"], "nt", F32, name=f"mix_out_dx_{tag}")
    g["w_mix_out"] = _matmul(sv["m"], dmix, "tn", F32, name=f"mix_out_dw_{tag}")
    dya, dyb, dzga, dzgb = _gate_merge_bwd(sv["zm"], sv["ya"], sv["yb"], dm, dims["ga_off"], name=f"merge_bwd_{tag}")
    da3 = _matmul(dya, p["w_conv_proj"], "nt", F32, name=f"conv_proj_dx_{tag}")
    g["w_conv_proj"] = _matmul(sv["a3"], dya, "tn", F32, name=f"conv_proj_dw_{tag}")
    do = _matmul(dyb, p["w_attn_proj"], "nt", BF16, name=f"attn_proj_dx_{tag}")
    dot = _matmul(p["w_attn_proj"], dyb, "nt", BF16, name=f"attn_proj_dxt_{tag}")
    g["w_attn_proj"] = _matmul(sv["ot"], dyb, "nn", F32, name=f"attn_proj_dw_{tag}")
    da1, g["ln_conv_g"], g["ln_conv_b"] = _lnsilu_bwd(sv["a1"], da3, p["ln_conv_g"], p["ln_conv_b"],
                                                      name=f"lnsilu_bwd_{tag}")
    da0, g["conv_a_w"], dcb = _dwconv_bwd(sv["a0"], da1, p["conv_a_w"], bsz, CONV_A_HALO, F32, name=f"conv_a_bwd_{tag}")
    g["conv_a_b"] = dcb[0]
    dzglu = _glu_bwd(sv["zm"], da0, ch, name=f"glu_bwd_{tag}")
    dk, dv, dqt, dcum = _attn_bwd(sv["zm"], sv["qkvt"], sv["cumt"], sv["cumb"], sv["ot"], do, dot, sv["lse"], bsz,
                                  heads, 2 * ch, name=f"attn_bwd_{tag}")
    dq = _to_rows_major(dqt, name=f"dq_rows_{tag}")
    dzf = _fgate_bwd(dcum, sv["zf"], bsz, name=f"fgate_bwd_{tag}")
    dzm = jnp.concatenate([dzglu, dq, dk, dv, dzga, dzgb], axis=1)
    du = _matmul(dzm, p["w_main"], "nt", F32, name=f"in_main_dx_{tag}")
    du = _matmul(dzf, p["w_f"], "nt", F32, add=du, name=f"in_forget_dx_{tag}")
    dwm, dbm = _matmul(sv["u"], dzm, "tn", F32, colsum=True, name=f"in_main_dw_{tag}")
    dwf, dbf = _matmul(sv["u"], dzf, "tn", F32, colsum=True, name=f"in_forget_dw_{tag}")
    dbm, dbf = dbm[0], dbf[0]
    g["w_in"] = jnp.concatenate([dwm[:, :f_off], dwf[:, :heads], dwm[:, f_off:]], axis=1)
    g["b_in"] = jnp.concatenate([dbm[:f_off], dbf[:heads], dbm[f_off:]])
    dx, dscale1, dshift1 = _ln_mod_bwd(du, sv["x"], scale1, dr1, alpha, bsz, name=f"ln_mod1_bwd_{tag}")
    dmod = jnp.concatenate([dshift1, dscale1, dgate1, dshift2, dscale2, dgate2], axis=2)[:, 0, :]
    return dx, g, dmod


def _local_step(x, mod, loss_target, full, rep, dims):
    bsz, seq, d = x.shape
    layers = mod.shape[0]
    params = [_layer_weights(full, rep, layer, dims) for layer in range(layers)]
    h = x.reshape(bsz * seq, d)
    saved = []
    for layer in range(layers):
        h, sv = _layer_fwd(h, mod[layer], params[layer], dims, f"l{layer}")
        saved.append(sv)
    dh, sq = _loss_head(h, loss_target.reshape(bsz * seq, d), name="loss_head")
    loss_local = 0.5 * jnp.sum(sq) / d
    grads, dmods = [None] * layers, [None] * layers
    for layer in reversed(range(layers)):
        dh, grads[layer], dmods[layer] = _layer_bwd(dh, params[layer], saved[layer], dims, f"l{layer}")
    stacked = {wname: jnp.stack([grads[layer][wname] for layer in range(layers)]) for wname in grads[0]}
    return loss_local, dh.reshape(bsz, seq, d), stacked, jnp.stack(dmods)


def _pad_rows(a):
    extra = -a.shape[-2] % (2 * SUBLANES)
    if extra == 0:
        return a
    return jnp.pad(a, [(0, 0)] * (a.ndim - 2) + [(0, extra), (0, 0)])


def _to_slab(g):
    layers, k, n4 = g.shape
    return jnp.transpose(g.reshape(layers, k, N_CHIPS, n4 // N_CHIPS), (0, 2, 1, 3))


def _from_slab(w):
    layers, _, k, n = w.shape
    return jnp.transpose(w, (0, 2, 1, 3)).reshape(layers, k, N_CHIPS * n)


def kernel(x, c, w_ada, b_ada, w_in, b_in, conv_a_w, conv_a_b, ln_conv_g, ln_conv_b, w_conv_proj, w_attn_proj, w_mix_out, b_mix_out, ln1_g, ln1_b, w_ffn_up, ffn_conv_w, ffn_conv_b, w_ffn_down, ln2_g, ln2_b, loss_target, m_w_ada, m_b_ada, m_w_in, m_b_in, m_conv_a_w, m_conv_a_b, m_ln_conv_g, m_ln_conv_b, m_w_conv_proj, m_w_attn_proj, m_w_mix_out, m_b_mix_out, m_ln1_g, m_ln1_b, m_w_ffn_up, m_ffn_conv_w, m_ffn_conv_b, m_w_ffn_down, m_ln2_g, m_ln2_b, v_w_ada, v_b_ada, v_w_in, v_b_in, v_conv_a_w, v_conv_a_b, v_ln_conv_g, v_ln_conv_b, v_w_conv_proj, v_w_attn_proj, v_w_mix_out, v_b_mix_out, v_ln1_g, v_ln1_b, v_w_ffn_up, v_ffn_conv_w, v_ffn_conv_b, v_w_ffn_down, v_ln2_g, v_ln2_b):
    weights = dict(zip(WEIGHTS, (w_ada, b_ada, w_in, b_in, conv_a_w, conv_a_b, ln_conv_g, ln_conv_b, w_conv_proj,
                                 w_attn_proj, w_mix_out, b_mix_out, ln1_g, ln1_b, w_ffn_up, ffn_conv_w, ffn_conv_b,
                                 w_ffn_down, ln2_g, ln2_b)))
    mom1 = dict(zip(WEIGHTS, (m_w_ada, m_b_ada, m_w_in, m_b_in, m_conv_a_w, m_conv_a_b, m_ln_conv_g, m_ln_conv_b,
                              m_w_conv_proj, m_w_attn_proj, m_w_mix_out, m_b_mix_out, m_ln1_g, m_ln1_b, m_w_ffn_up,
                              m_ffn_conv_w, m_ffn_conv_b, m_w_ffn_down, m_ln2_g, m_ln2_b)))
    mom2 = dict(zip(WEIGHTS, (v_w_ada, v_b_ada, v_w_in, v_b_in, v_conv_a_w, v_conv_a_b, v_ln_conv_g, v_ln_conv_b,
                              v_w_conv_proj, v_w_attn_proj, v_w_mix_out, v_b_mix_out, v_ln1_g, v_ln1_b, v_w_ffn_up,
                              v_ffn_conv_w, v_ffn_conv_b, v_w_ffn_down, v_ln2_g, v_ln2_b)))
    bsz, seq, d = x.shape
    layers = w_ada.shape[0]
    ch = conv_a_w.shape[2] * N_CHIPS
    width = w_attn_proj.shape[1]
    heads = width // HEAD_DIM
    d_ff = w_ffn_down.shape[1] * N_CHIPS
    dims = dict(bsz=bsz, d=d, ch=ch, heads=heads, alpha=(2.0 * layers) ** 0.25, f_off=2 * ch + 3 * width,
                ga_off=2 * ch + 3 * width, tcf=_pick(d_ff, (256, 128)))
    chip = 2 * lax.axis_index("x") + lax.axis_index("y")
    device = 2 * chip + lax.axis_index("c")
    ada_cols = w_ada.shape[2]

    c_act = _silu_rows(_all_gather8(c, name="gather_c").reshape(N_DEVICES * bsz, d), name="silu_c")
    b_ada_mine = lax.dynamic_slice_in_dim(b_ada, chip * ada_cols, ada_cols, axis=1)
    mod_cols = jnp.stack([_matmul(c_act, w_ada[layer], "nn", F32, bias=b_ada_mine[layer][None], name=f"ada_l{layer}")
                          for layer in range(layers)])
    mod_all = _all_gather8(mod_cols, name="gather_mod")
    mod_all = jnp.concatenate([mod_all[2 * k] for k in range(N_CHIPS)], axis=-1)
    mod = lax.dynamic_slice_in_dim(mod_all, device * bsz, bsz, axis=1)

    shards = [_pad_rows(weights[wname].astype(BF16) if as_bf16 else weights[wname]) for wname, _, as_bf16 in GATHERED]
    modes = [mode for _, mode, _ in GATHERED]
    whole = _gather_weights(shards, modes, name="gather_weights")
    full = {wname: w[:, :weights[wname].shape[1]] if mode == "cols" else w
            for (wname, mode, _), w in zip(GATHERED, whole)}
    full["w_in"] = _from_slab(full["w_in"])
    rep = {wname: weights[wname] for wname in REPLICATED}

    loss_local, grad_x, grads, dmod = _local_step(x, mod, loss_target, full, rep, dims)
    loss = lax.psum(loss_local, ("x", "y", "c"))

    grads["w_in"] = _to_slab(grads["w_in"])
    shard_shapes = [s.shape for s in shards]
    reduced = _reduce_scatter([_pad_rows(grads[wname]) for wname, _, _ in GATHERED], modes, shard_shapes)
    grad = {wname: r[:, :weights[wname].shape[1]] for (wname, _, _), r in zip(GATHERED, reduced)}

    small = jnp.concatenate([dmod.reshape(-1)] + [grads[wname].reshape(-1) for wname in REPLICATED])
    n_small = small.shape[0]
    rows = -(-n_small // (SUBLANES * LANES)) * SUBLANES
    small = jnp.pad(small, (0, rows * LANES - n_small)).reshape(rows, LANES)
    gathered = _all_gather8(small, name="gather_small")
    n_dmod = dmod.size
    dmod_all = gathered.reshape(N_DEVICES, -1)[:, :n_dmod].reshape(N_DEVICES, layers, bsz, 6 * d)
    dmod_all = jnp.transpose(dmod_all, (1, 0, 2, 3)).reshape(layers, N_DEVICES * bsz, 6 * d)
    summed = _sum_slots(gathered, name="sum_small").reshape(-1)
    off = n_dmod
    for wname in REPLICATED:
        n = weights[wname].size
        grad[wname] = summed[off:off + n].reshape(weights[wname].shape)
        off += n
    dmod_mine = lax.dynamic_slice_in_dim(dmod_all, chip * ada_cols, ada_cols, axis=2)
    grad["w_ada"] = jnp.stack([_matmul(c_act, dmod_mine[layer], "tn", F32, name=f"ada_dw_l{layer}")
                               for layer in range(layers)])
    grad["b_ada"] = jnp.stack([_colsum(dmod_all[layer], name=f"ada_db_l{layer}")[0] for layer in range(layers)])

    delta, new_m, new_v = {}, {}, {}
    for wname in WEIGHTS:
        delta[wname], new_m[wname], new_v[wname] = _adamw(weights[wname], grad[wname], mom1[wname], mom2[wname],
                                                          name=f"adamw_{wname}")
    return (loss, grad_x, *[grad[wname] for wname in WEIGHTS], *[delta[wname] for wname in WEIGHTS],
            *[new_m[wname] for wname in WEIGHTS], *[new_v[wname] for wname in WEIGHTS])
```

```python
import math

import jax
import jax.numpy as jnp
from jax import lax
from jax.experimental import pallas as pl
from jax.experimental.pallas import tpu as pltpu

F32 = jnp.float32
BF16 = jnp.bfloat16
MESH = pl.DeviceIdType.MESH

LN_EPS = 1e-5
HEAD_DIM = 64
ATTN_SCALE = HEAD_DIM ** -0.5
NEG = -1e30
CONV_A_HALO = 32
FFN_PAD = 8
LANES = 128
SUBLANES = 8
ROW_CHUNK = 256
ATTN_BLOCK = 256
N_CHIPS = 4
N_DEVICES = 8
VMEM_LIMIT = 56 * 1024 * 1024

ADAM_LR = 0.001
ADAM_B1 = 0.9
ADAM_B2 = 0.999
ADAM_EPS = 1e-08
ADAM_WD = 0.01
ADAM_STEP = 10

GATHERED = (("w_in", "slab", True), ("conv_a_w", "cols", False), ("w_conv_proj", "cols", True),
            ("w_attn_proj", "cols", True), ("w_mix_out", "rows", True), ("w_ffn_up", "cols", True),
            ("ffn_conv_w", "cols", False), ("w_ffn_down", "rows", True))
REPLICATED = ("b_in", "conv_a_b", "ln_conv_g", "ln_conv_b", "b_mix_out", "ln1_g", "ln1_b",
              "ffn_conv_b", "ln2_g", "ln2_b")
WEIGHTS = ("w_ada", "b_ada", "w_in", "b_in", "conv_a_w", "conv_a_b", "ln_conv_g", "ln_conv_b",
           "w_conv_proj", "w_attn_proj", "w_mix_out", "b_mix_out", "ln1_g", "ln1_b", "w_ffn_up",
           "ffn_conv_w", "ffn_conv_b", "w_ffn_down", "ln2_g", "ln2_b")


def _pick(n, cands):
    for cand in cands:
        if n % cand == 0:
            return cand
    return n


def _call(body, *, name, grid, in_specs, out_specs, out_shape, scratch=(), sem=None):
    return pl.pallas_call(
        body, name=name, grid=grid, in_specs=in_specs, out_specs=out_specs, out_shape=out_shape,
        scratch_shapes=list(scratch),
        compiler_params=pltpu.CompilerParams(dimension_semantics=sem, vmem_limit_bytes=VMEM_LIMIT))


def _sds(shape, dtype):
    return jax.ShapeDtypeStruct(tuple(shape), dtype)


def _chunked(rows, fn):
    chunk = min(ROW_CHUNK, rows)
    if rows == chunk:
        fn(pl.ds(0, rows))
        return

    def step(i, carry):
        fn(pl.ds(pl.multiple_of(i * chunk, chunk), chunk))
        return carry

    lax.fori_loop(0, rows // chunk, step, 0)


def _matmul(a, b, mode, out_dtype, *, bias=None, add=None, colsum=False, name):
    if mode == "nn":
        (m, k), (_, n) = a.shape, b.shape
    elif mode == "nt":
        (m, k), (n, _) = a.shape, b.shape
    else:
        (k, m), (_, n) = a.shape, b.shape
    tm = _pick(m, (1024, 1408, 512, 256, 128))
    tn = _pick(n, (512, 1408, 256, 128))
    tk = k if k <= 1536 else _pick(k, (1024, 1536, 1408, 512, 256, 128))
    nk = k // tk
    if mode == "nn":
        a_spec = pl.BlockSpec((tm, tk), lambda i, j, kk: (i, kk))
        b_spec = pl.BlockSpec((tk, tn), lambda i, j, kk: (kk, j))
        dims = (((1,), (0,)), ((), ()))
    elif mode == "nt":
        a_spec = pl.BlockSpec((tm, tk), lambda i, j, kk: (i, kk))
        b_spec = pl.BlockSpec((tn, tk), lambda i, j, kk: (j, kk))
        dims = (((1,), (1,)), ((), ()))
    else:
        a_spec = pl.BlockSpec((tk, tm), lambda i, j, kk: (kk, i))
        b_spec = pl.BlockSpec((tk, tn), lambda i, j, kk: (kk, j))
        dims = (((0,), (0,)), ((), ()))
    in_specs = [a_spec, b_spec]
    operands = [a, b]
    if bias is not None:
        in_specs.append(pl.BlockSpec((1, tn), lambda i, j, kk: (0, j)))
        operands.append(bias)
    if add is not None:
        in_specs.append(pl.BlockSpec((tm, tn), lambda i, j, kk: (i, j)))
        operands.append(add)

    def body(a_ref, b_ref, *rest):
        rest = list(rest)
        bias_ref = rest.pop(0) if bias is not None else None
        add_ref = rest.pop(0) if add is not None else None
        o_ref = rest.pop(0)
        prod = lax.dot_general(a_ref[...].astype(BF16), b_ref[...].astype(BF16), dims,
                               preferred_element_type=F32)
        if colsum:
            cs_ref = rest.pop(0)
            part = jnp.sum(b_ref[...].astype(F32), axis=0, keepdims=True)

            @pl.when(pl.program_id(2) == 0)
            def _():
                cs_ref[...] = part

            @pl.when(pl.program_id(2) > 0)
            def _():
                cs_ref[...] += part

        def finish(r):
            if bias_ref is not None:
                r = r + bias_ref[...]
            if add_ref is not None:
                r = r + add_ref[...]
            o_ref[...] = r.astype(o_ref.dtype)

        if nk == 1:
            finish(prod)
            return
        acc_ref = rest.pop(0)
        kk = pl.program_id(2)

        @pl.when(kk == 0)
        def _():
            acc_ref[...] = prod

        @pl.when(kk > 0)
        def _():
            acc_ref[...] += prod

        @pl.when(kk == nk - 1)
        def _():
            finish(acc_ref[...])

    out_specs = pl.BlockSpec((tm, tn), lambda i, j, kk: (i, j))
    out_shape = _sds((m, n), out_dtype)
    if colsum:
        assert mode == "tn" and m == tm
        out_specs = [out_specs, pl.BlockSpec((1, tn), lambda i, j, kk: (0, j))]
        out_shape = [out_shape, _sds((1, n), F32)]
    return _call(body, name=name, grid=(m // tm, n // tn, nk), in_specs=in_specs, out_specs=out_specs,
                 out_shape=out_shape, scratch=[pltpu.VMEM((tm, tn), F32)] if nk > 1 else [],
                 sem=("parallel", "parallel", "arbitrary"))(*operands)


def _colsum(x, *, name):
    rows, n = x.shape
    tr = _pick(rows, (1024, 512, 256, 128))
    tn = _pick(n, (512, 256, 128))

    def body(x_ref, o_ref):
        @pl.when(pl.program_id(1) == 0)
        def _():
            o_ref[...] = jnp.zeros_like(o_ref)

        o_ref[...] += jnp.sum(x_ref[...].astype(F32), axis=0, keepdims=True)

    return _call(body, name=name, grid=(n // tn, rows // tr),
                 in_specs=[pl.BlockSpec((tr, tn), lambda j, i: (i, j))],
                 out_specs=pl.BlockSpec((1, tn), lambda j, i: (0, j)),
                 out_shape=_sds((1, n), F32), sem=("parallel", "arbitrary"))(x)


def _ln_stats(x):
    mu = jnp.mean(x, axis=-1, keepdims=True)
    xc = x - mu
    var = jnp.mean(xc * xc, axis=-1, keepdims=True)
    rstd = lax.rsqrt(var + LN_EPS)
    return xc * rstd, rstd


def _ln_bwd(dn, n, rstd):
    return rstd * (dn - jnp.mean(dn, axis=-1, keepdims=True) - n * jnp.mean(dn * n, axis=-1, keepdims=True))


def _seq_tiles(t, bsz, cands=(1024, 512, 256, 128, 64, 32, 16, 8)):
    s = t // bsz
    ts = _pick(s, cands)
    return s, ts, s // ts


def _ln_mod_fwd(x, scale, shift, bsz, *, name):
    t, d = x.shape
    _, ts, ns = _seq_tiles(t, bsz)

    def body(x_ref, sc_ref, sh_ref, u_ref):
        one_scale = 1.0 + sc_ref[0]
        shift_v = sh_ref[0]

        def piece(rows):
            n, _ = _ln_stats(x_ref[rows, :])
            u_ref[rows, :] = (n * one_scale + shift_v).astype(u_ref.dtype)

        _chunked(ts, piece)

    row = pl.BlockSpec((ts, d), lambda b, i: (b * ns + i, 0))
    per = pl.BlockSpec((1, 1, d), lambda b, i: (b, 0, 0))
    return _call(body, name=name, grid=(bsz, ns), in_specs=[row, per, per], out_specs=row,
                 out_shape=_sds((t, d), BF16), sem=("parallel", "parallel"))(x, scale, shift)


def _ln_mod_bwd(du, x, scale, dr, alpha, bsz, *, name):
    t, d = x.shape
    _, ts, ns = _seq_tiles(t, bsz)

    def body(du_ref, x_ref, sc_ref, dr_ref, dx_ref, dsc_ref, dsh_ref):
        @pl.when(pl.program_id(1) == 0)
        def _():
            dsc_ref[...] = jnp.zeros_like(dsc_ref)
            dsh_ref[...] = jnp.zeros_like(dsh_ref)

        one_scale = 1.0 + sc_ref[0]

        def piece(rows):
            du_v = du_ref[rows, :]
            n, rstd = _ln_stats(x_ref[rows, :])
            dsc_ref[0] += jnp.sum(du_v * n, axis=0, keepdims=True)
            dsh_ref[0] += jnp.sum(du_v, axis=0, keepdims=True)
            dx_ref[rows, :] = alpha * dr_ref[rows, :] + _ln_bwd(du_v * one_scale, n, rstd)

        _chunked(ts, piece)

    row = pl.BlockSpec((ts, d), lambda b, i: (b * ns + i, 0))
    per = pl.BlockSpec((1, 1, d), lambda b, i: (b, 0, 0))
    return _call(body, name=name, grid=(bsz, ns), in_specs=[row, row, per, row],
                 out_specs=[row, per, per],
                 out_shape=[_sds((t, d), F32), _sds((bsz, 1, d), F32), _sds((bsz, 1, d), F32)],
                 sem=("parallel", "arbitrary"))(du, x, scale, dr)


def _ln_res_fwd(x, y, gate, g, b, alpha, bsz, *, name):
    t, d = x.shape
    _, ts, ns = _seq_tiles(t, bsz)

    def body(x_ref, y_ref, gt_ref, g_ref, b_ref, o_ref):
        one_gate = 1.0 + gt_ref[0]

        def piece(rows):
            n, _ = _ln_stats(alpha * x_ref[rows, :] + one_gate * y_ref[rows, :])
            o_ref[rows, :] = n * g_ref[...] + b_ref[...]

        _chunked(ts, piece)

    row = pl.BlockSpec((ts, d), lambda bb, i: (bb * ns + i, 0))
    per = pl.BlockSpec((1, 1, d), lambda bb, i: (bb, 0, 0))
    vec = pl.BlockSpec((1, d), lambda bb, i: (0, 0))
    return _call(body, name=name, grid=(bsz, ns), in_specs=[row, row, per, vec, vec], out_specs=row,
                 out_shape=_sds((t, d), F32), sem=("parallel", "parallel"))(x, y, gate, g, b)


def _ln_res_bwd(do, x, y, gate, g, alpha, bsz, *, name):
    t, d = x.shape
    _, ts, ns = _seq_tiles(t, bsz)

    def body(do_ref, x_ref, y_ref, gt_ref, g_ref, dr_ref, dy_ref, dgt_ref, dg_ref, db_ref, dys_ref):
        first_tile = pl.program_id(1) == 0

        @pl.when(first_tile)
        def _():
            dgt_ref[...] = jnp.zeros_like(dgt_ref)

        @pl.when(jnp.logical_and(first_tile, pl.program_id(0) == 0))
        def _():
            dg_ref[...] = jnp.zeros_like(dg_ref)
            db_ref[...] = jnp.zeros_like(db_ref)
            dys_ref[...] = jnp.zeros_like(dys_ref)

        one_gate = 1.0 + gt_ref[0]

        def piece(rows):
            do_v = do_ref[rows, :]
            y_v = y_ref[rows, :]
            n, rstd = _ln_stats(alpha * x_ref[rows, :] + one_gate * y_v)
            dg_ref[...] += jnp.sum(do_v * n, axis=0, keepdims=True)
            db_ref[...] += jnp.sum(do_v, axis=0, keepdims=True)
            dr = _ln_bwd(do_v * g_ref[...], n, rstd)
            dr_ref[rows, :] = dr
            dy = one_gate * dr
            dy_ref[rows, :] = dy.astype(dy_ref.dtype)
            dys_ref[...] += jnp.sum(dy, axis=0, keepdims=True)
            dgt_ref[0] += jnp.sum(dr * y_v, axis=0, keepdims=True)

        _chunked(ts, piece)

    row = pl.BlockSpec((ts, d), lambda bb, i: (bb * ns + i, 0))
    per = pl.BlockSpec((1, 1, d), lambda bb, i: (bb, 0, 0))
    vec = pl.BlockSpec((1, d), lambda bb, i: (0, 0))
    return _call(body, name=name, grid=(bsz, ns), in_specs=[row, row, row, per, vec],
                 out_specs=[row, row, per, vec, vec, vec],
                 out_shape=[_sds((t, d), F32), _sds((t, d), BF16), _sds((bsz, 1, d), F32),
                            _sds((1, d), F32), _sds((1, d), F32), _sds((1, d), F32)],
                 sem=("arbitrary", "arbitrary"))(do, x, y, gate, g)


def _loss_head(y, target, *, name):
    t, d = y.shape
    tr = _pick(t, (1024, 512, 256, 128, 64, 32, 16, 8))

    def body(y_ref, t_ref, dy_ref, s_ref):
        @pl.when(pl.program_id(0) == 0)
        def _():
            s_ref[...] = jnp.zeros_like(s_ref)

        def piece(rows):
            e = y_ref[rows, :] - t_ref[rows, :]
            dy_ref[rows, :] = e * (1.0 / d)
            s_ref[...] += jnp.sum(e * e, axis=0, keepdims=True)

        _chunked(tr, piece)

    row = pl.BlockSpec((tr, d), lambda i: (i, 0))
    return _call(body, name=name, grid=(t // tr,), in_specs=[row, row],
                 out_specs=[row, pl.BlockSpec((1, d), lambda i: (0, 0))],
                 out_shape=[_sds((t, d), F32), _sds((1, d), F32)], sem=("arbitrary",))(y, target)


def _sigmoid(v):
    return 1.0 / (1.0 + jnp.exp(-v))


def _silu_rows(c, *, name):
    rows, d = c.shape

    def body(c_ref, o_ref):
        v = c_ref[...]
        o_ref[...] = (v * _sigmoid(v)).astype(o_ref.dtype)

    full = pl.BlockSpec((rows, d), lambda i: (0, 0))
    return _call(body, name=name, grid=(1,), in_specs=[full], out_specs=full,
                 out_shape=_sds((rows, d), BF16), sem=("arbitrary",))(c)


def _glu_fwd(z, ch, *, name):
    t = z.shape[0]
    tr = _pick(t, (1024, 512, 256, 128, 64, 32, 16, 8))

    def body(z_ref, o_ref):
        def piece(rows):
            o_ref[rows, :] = z_ref[rows, :ch].astype(F32) * _sigmoid(z_ref[rows, ch:].astype(F32))

        _chunked(tr, piece)

    return _call(body, name=name, grid=(t // tr,),
                 in_specs=[pl.BlockSpec((tr, 2 * ch), lambda i: (i, 0))],
                 out_specs=pl.BlockSpec((tr, ch), lambda i: (i, 0)),
                 out_shape=_sds((t, ch), F32), sem=("parallel",))(z)


def _glu_bwd(z, da0, ch, *, name):
    t = z.shape[0]
    tr = _pick(t, (1024, 512, 256, 128, 64, 32, 16, 8))

    def body(z_ref, d_ref, o_ref):
        def piece(rows):
            s = _sigmoid(z_ref[rows, ch:].astype(F32))
            d = d_ref[rows, :]
            o_ref[rows, :ch] = (d * s).astype(o_ref.dtype)
            o_ref[rows, ch:] = (d * z_ref[rows, :ch].astype(F32) * s * (1.0 - s)).astype(o_ref.dtype)

        _chunked(tr, piece)

    return _call(body, name=name, grid=(t // tr,),
                 in_specs=[pl.BlockSpec((tr, 2 * ch), lambda i: (i, 0)),
                           pl.BlockSpec((tr, ch), lambda i: (i, 0))],
                 out_specs=pl.BlockSpec((tr, 2 * ch), lambda i: (i, 0)),
                 out_shape=_sds((t, 2 * ch), BF16), sem=("parallel",))(z, da0)


def _lnsilu_fwd(a1, g, b, *, name):
    t, ch = a1.shape
    tr = _pick(t, (1024, 512, 256, 128, 64, 32, 16, 8))

    def body(a_ref, g_ref, b_ref, o_ref):
        def piece(rows):
            n, _ = _ln_stats(a_ref[rows, :])
            a2 = n * g_ref[...] + b_ref[...]
            o_ref[rows, :] = (a2 * _sigmoid(a2)).astype(o_ref.dtype)

        _chunked(tr, piece)

    row = pl.BlockSpec((tr, ch), lambda i: (i, 0))
    vec = pl.BlockSpec((1, ch), lambda i: (0, 0))
    return _call(body, name=name, grid=(t // tr,), in_specs=[row, vec, vec], out_specs=row,
                 out_shape=_sds((t, ch), BF16), sem=("parallel",))(a1, g, b)


def _lnsilu_bwd(a1, da3, g, b, *, name):
    t, ch = a1.shape
    tr = _pick(t, (1024, 512, 256, 128, 64, 32, 16, 8))

    def body(a_ref, d_ref, g_ref, b_ref, o_ref, dg_ref, db_ref):
        @pl.when(pl.program_id(0) == 0)
        def _():
            dg_ref[...] = jnp.zeros_like(dg_ref)
            db_ref[...] = jnp.zeros_like(db_ref)

        def piece(rows):
            n, rstd = _ln_stats(a_ref[rows, :])
            a2 = n * g_ref[...] + b_ref[...]
            s = _sigmoid(a2)
            da2 = d_ref[rows, :] * (s * (1.0 + a2 * (1.0 - s)))
            dg_ref[...] += jnp.sum(da2 * n, axis=0, keepdims=True)
            db_ref[...] += jnp.sum(da2, axis=0, keepdims=True)
            o_ref[rows, :] = _ln_bwd(da2 * g_ref[...], n, rstd)

        _chunked(tr, piece)

    row = pl.BlockSpec((tr, ch), lambda i: (i, 0))
    vec = pl.BlockSpec((1, ch), lambda i: (0, 0))
    return _call(body, name=name, grid=(t // tr,), in_specs=[row, row, vec, vec],
                 out_specs=[row, vec, vec],
                 out_shape=[_sds((t, ch), F32), _sds((1, ch), F32), _sds((1, ch), F32)],
                 sem=("arbitrary",))(a1, da3, g, b)


def _gate_cols(d, ga_off):
    tc = _pick(math.gcd(d, ga_off), (512, 256, 128))
    return tc, ga_off // tc, (ga_off + d) // tc


def _gate_merge_fwd(z, ya, yb, ga_off, *, name):
    t, d = ya.shape
    tr = _pick(t, (1024, 512, 256, 128, 64, 32, 16, 8))
    tc, ga_blk, gb_blk = _gate_cols(d, ga_off)

    def body(ga_ref, gb_ref, ya_ref, yb_ref, o_ref):
        def piece(rows):
            o_ref[rows, :] = (_sigmoid(ga_ref[rows, :].astype(F32)) * ya_ref[rows, :]
                              + _sigmoid(gb_ref[rows, :].astype(F32)) * yb_ref[rows, :]).astype(o_ref.dtype)

        _chunked(tr, piece)

    blk = pl.BlockSpec((tr, tc), lambda i, j: (i, j))
    return _call(body, name=name, grid=(t // tr, d // tc),
                 in_specs=[pl.BlockSpec((tr, tc), lambda i, j: (i, ga_blk + j)),
                           pl.BlockSpec((tr, tc), lambda i, j: (i, gb_blk + j)), blk, blk],
                 out_specs=blk, out_shape=_sds((t, d), BF16), sem=("parallel", "parallel"))(z, z, ya, yb)


def _gate_merge_bwd(z, ya, yb, dm, ga_off, *, name):
    t, d = ya.shape
    tr = _pick(t, (1024, 512, 256, 128, 64, 32, 16, 8))
    tc, ga_blk, gb_blk = _gate_cols(d, ga_off)

    def body(ga_ref, gb_ref, ya_ref, yb_ref, dm_ref, dya_ref, dyb_ref, dga_ref, dgb_ref):
        def piece(rows):
            dm_v = dm_ref[rows, :]
            sa = _sigmoid(ga_ref[rows, :].astype(F32))
            sb = _sigmoid(gb_ref[rows, :].astype(F32))
            dya_ref[rows, :] = (dm_v * sa).astype(dya_ref.dtype)
            dyb_ref[rows, :] = (dm_v * sb).astype(dyb_ref.dtype)
            dga_ref[rows, :] = (dm_v * ya_ref[rows, :] * sa * (1.0 - sa)).astype(dga_ref.dtype)
            dgb_ref[rows, :] = (dm_v * yb_ref[rows, :] * sb * (1.0 - sb)).astype(dgb_ref.dtype)

        _chunked(tr, piece)

    blk = pl.BlockSpec((tr, tc), lambda i, j: (i, j))
    return _call(body, name=name, grid=(t // tr, d // tc),
                 in_specs=[pl.BlockSpec((tr, tc), lambda i, j: (i, ga_blk + j)),
                           pl.BlockSpec((tr, tc), lambda i, j: (i, gb_blk + j)), blk, blk, blk],
                 out_specs=[blk, blk, blk, blk], out_shape=[_sds((t, d), BF16)] * 4,
                 sem=("parallel", "parallel"))(z, z, ya, yb, dm)


def _dwconv_fwd(x, w, b, bsz, halo, *, name):
    t, ch = x.shape
    kw = w.shape[0]
    _, ts, ns = _seq_tiles(t, bsz, (256, 128, 64, 32))
    tc = _pick(ch, (512, 256, 128))
    hb = ts // halo

    def body(x_ref, h_ref, w_ref, b_ref, y_ref, ext_ref):
        ext_ref[pl.ds(0, halo), :] = jnp.where(pl.program_id(1) > 0, h_ref[...], 0.0)
        ext_ref[pl.ds(halo, ts), :] = x_ref[...]
        acc = jnp.zeros((ts, tc), F32) + b_ref[...]
        for k in range(kw):
            acc = acc + w_ref[pl.ds(k, 1), :] * ext_ref[pl.ds(halo - (kw - 1) + k, ts), :]
        y_ref[...] = acc

    cur = pl.BlockSpec((ts, tc), lambda bb, i, j: (bb * ns + i, j))
    prev = pl.BlockSpec((halo, tc), lambda bb, i, j: (jnp.maximum((bb * ns + i) * hb - 1, 0), j))
    return _call(body, name=name, grid=(bsz, ns, ch // tc),
                 in_specs=[cur, prev, pl.BlockSpec((kw, tc), lambda bb, i, j: (0, j)),
                           pl.BlockSpec((1, tc), lambda bb, i, j: (0, j))],
                 out_specs=cur, out_shape=_sds((t, ch), F32), scratch=[pltpu.VMEM((halo + ts, tc), F32)],
                 sem=("parallel", "parallel", "parallel"))(x, x, w, b)


def _dwconv_bwd(x, dy, w, bsz, halo, dx_dtype, *, name):
    t, ch = x.shape
    kw = w.shape[0]
    _, ts, ns = _seq_tiles(t, bsz, (256, 128, 64, 32))
    tc = _pick(ch, (512, 256, 128))
    hb = ts // halo
    last_halo_blk = t // halo - 1

    def body(x_ref, xh_ref, dy_ref, dyh_ref, w_ref, dx_ref, dw_ref, db_ref, extx_ref, exty_ref):
        i = pl.program_id(2)

        @pl.when(jnp.logical_and(pl.program_id(1) == 0, i == 0))
        def _():
            dw_ref[...] = jnp.zeros_like(dw_ref)
            db_ref[...] = jnp.zeros_like(db_ref)

        extx_ref[pl.ds(0, halo), :] = jnp.where(i > 0, xh_ref[...], 0.0)
        extx_ref[pl.ds(halo, ts), :] = x_ref[...]
        dyc = dy_ref[...]
        exty_ref[pl.ds(0, ts), :] = dyc
        exty_ref[pl.ds(ts, halo), :] = jnp.where(i < ns - 1, dyh_ref[...], 0.0)
        acc = jnp.zeros((ts, tc), F32)
        for k in range(kw):
            acc = acc + w_ref[pl.ds(k, 1), :] * exty_ref[pl.ds(kw - 1 - k, ts), :]
            dw_ref[pl.ds(k, 1), :] += jnp.sum(dyc * extx_ref[pl.ds(halo - (kw - 1) + k, ts), :],
                                              axis=0, keepdims=True)
        dx_ref[...] = acc.astype(dx_ref.dtype)
        db_ref[...] += jnp.sum(dyc, axis=0, keepdims=True)

    cur = pl.BlockSpec((ts, tc), lambda j, bb, i: (bb * ns + i, j))
    prev = pl.BlockSpec((halo, tc), lambda j, bb, i: (jnp.maximum((bb * ns + i) * hb - 1, 0), j))
    nxt = pl.BlockSpec((halo, tc), lambda j, bb, i: (jnp.minimum((bb * ns + i + 1) * hb, last_halo_blk), j))
    return _call(body, name=name, grid=(ch // tc, bsz, ns),
                 in_specs=[cur, prev, cur, nxt, pl.BlockSpec((kw, tc), lambda j, bb, i: (0, j))],
                 out_specs=[cur, pl.BlockSpec((kw, tc), lambda j, bb, i: (0, j)),
                            pl.BlockSpec((1, tc), lambda j, bb, i: (0, j))],
                 out_shape=[_sds((t, ch), dx_dtype), _sds((kw, ch), F32), _sds((1, ch), F32)],
                 scratch=[pltpu.VMEM((halo + ts, tc), F32), pltpu.VMEM((ts + halo, tc), F32)],
                 sem=("parallel", "arbitrary", "arbitrary"))(x, x, dy, dy, w)


FFN_ROWS = 64


def _gelu_parts(v):
    cdf = 0.5 * (1.0 + lax.erf(v * (2.0 ** -0.5)))
    return cdf, v * cdf


def _ffn_conv_piece(ext_ref, wb_ref, base):
    win = ext_ref[pl.ds(base, FFN_ROWS + FFN_PAD), :]
    acc = wb_ref[pl.ds(3, 1), :] + wb_ref[pl.ds(2, 1), :] * win[FFN_PAD:]
    acc = acc + wb_ref[pl.ds(1, 1), :] * pltpu.roll(win, 1, axis=0)[FFN_PAD:]
    acc = acc + wb_ref[pl.ds(0, 1), :] * pltpu.roll(win, 2, axis=0)[FFN_PAD:]
    return acc


def _ffn_stage(hg_ref, hl_ref, wg_ref, wl_ref, bg_ref, bl_ref, ext_ref, wb_ref, s, tcf):
    ext_ref[pl.ds(0, FFN_PAD), :] = jnp.zeros((FFN_PAD, 2 * tcf), F32)
    ext_ref[pl.ds(FFN_PAD, s), :tcf] = hg_ref[...].astype(F32)
    ext_ref[pl.ds(FFN_PAD, s), tcf:] = hl_ref[...].astype(F32)
    wb_ref[pl.ds(0, 3), :tcf] = wg_ref[...]
    wb_ref[pl.ds(0, 3), tcf:] = wl_ref[...]
    wb_ref[pl.ds(3, 1), :tcf] = bg_ref[...]
    wb_ref[pl.ds(3, 1), tcf:] = bl_ref[...]


def _ffn_specs(s, tcf, n_f, batch_first):
    def spec(rows, shift):
        if batch_first:
            return pl.BlockSpec((rows, tcf), lambda bb, j: (bb if rows == s else 0, shift + j))
        return pl.BlockSpec((rows, tcf), lambda j, bb: (bb if rows == s else 0, shift + j))

    return [spec(s, 0), spec(s, n_f), spec(3, 0), spec(3, n_f), spec(1, 0), spec(1, n_f)]


def _ffn_act_fwd(hp, w, b, bsz, tcf, *, name):
    t, two_f = hp.shape
    s = t // bsz
    n_f = two_f // (2 * tcf)

    def body(hg_ref, hl_ref, wg_ref, wl_ref, bg_ref, bl_ref, f_ref, ext_ref, wb_ref):
        _ffn_stage(hg_ref, hl_ref, wg_ref, wl_ref, bg_ref, bl_ref, ext_ref, wb_ref, s, tcf)

        def step(i, carry):
            base = pl.multiple_of(i * FFN_ROWS, FFN_ROWS)
            hh = _ffn_conv_piece(ext_ref, wb_ref, base)
            _, gelu = _gelu_parts(hh[:, :tcf])
            f_ref[pl.ds(base, FFN_ROWS), :] = (gelu * hh[:, tcf:]).astype(f_ref.dtype)
            return carry

        lax.fori_loop(0, s // FFN_ROWS, step, 0)

    return _call(body, name=name, grid=(bsz, n_f), in_specs=_ffn_specs(s, tcf, n_f, True),
                 out_specs=pl.BlockSpec((s, tcf), lambda bb, j: (bb, j)),
                 out_shape=_sds((t, two_f // 2), BF16),
                 scratch=[pltpu.VMEM((FFN_PAD + s, 2 * tcf), F32), pltpu.VMEM((SUBLANES, 2 * tcf), F32)],
                 sem=("parallel", "parallel"))(hp, hp, w, w, b, b)


def _ffn_act_bwd(hp, df, w, b, bsz, tcf, *, name):
    t, two_f = hp.shape
    s = t // bsz
    f_dim = two_f // 2
    n_f = f_dim // tcf
    gw = 2 * tcf
    n_rows = FFN_ROWS + FFN_PAD

    def body(hg_ref, hl_ref, wg_ref, wl_ref, bg_ref, bl_ref, df_ref,
             dhg_ref, dhl_ref, dwg_ref, dwl_ref, dbg_ref, dbl_ref, ext_ref, wb_ref, dh_ref):
        @pl.when(pl.program_id(1) == 0)
        def _():
            for ref in (dwg_ref, dwl_ref, dbg_ref, dbl_ref):
                ref[...] = jnp.zeros_like(ref)

        _ffn_stage(hg_ref, hl_ref, wg_ref, wl_ref, bg_ref, bl_ref, ext_ref, wb_ref, s, tcf)
        dh_ref[pl.ds(s, FFN_PAD), :] = jnp.zeros((FFN_PAD, gw), F32)

        def grad_h(i, carry):
            base = pl.multiple_of(i * FFN_ROWS, FFN_ROWS)
            hh = _ffn_conv_piece(ext_ref, wb_ref, base)
            hg = hh[:, :tcf]
            d = df_ref[pl.ds(base, FFN_ROWS), :]
            cdf, gelu = _gelu_parts(hg)
            pdf = jnp.exp(-0.5 * hg * hg) * (1.0 / math.sqrt(2.0 * math.pi))
            dh_ref[pl.ds(base, FFN_ROWS), :tcf] = d * hh[:, tcf:] * (cdf + hg * pdf)
            dh_ref[pl.ds(base, FFN_ROWS), tcf:] = d * gelu
            return carry

        lax.fori_loop(0, s // FFN_ROWS, grad_h, 0)

        def grad_x(i, carry):
            dw0, dw1, dw2, dbs = carry
            base = pl.multiple_of(i * FFN_ROWS, FFN_ROWS)
            nxt = dh_ref[pl.ds(base, n_rows), :]
            dyc = nxt[:FFN_ROWS]
            dx = wb_ref[pl.ds(2, 1), :] * dyc
            dx = dx + wb_ref[pl.ds(1, 1), :] * pltpu.roll(nxt, n_rows - 1, axis=0)[:FFN_ROWS]
            dx = dx + wb_ref[pl.ds(0, 1), :] * pltpu.roll(nxt, n_rows - 2, axis=0)[:FFN_ROWS]
            dhg_ref[pl.ds(base, FFN_ROWS), :] = dx[:, :tcf].astype(dhg_ref.dtype)
            dhl_ref[pl.ds(base, FFN_ROWS), :] = dx[:, tcf:].astype(dhl_ref.dtype)
            win = ext_ref[pl.ds(base, n_rows), :]
            dw2 = dw2 + jnp.sum(dyc * win[FFN_PAD:], axis=0, keepdims=True)
            dw1 = dw1 + jnp.sum(dyc * pltpu.roll(win, 1, axis=0)[FFN_PAD:], axis=0, keepdims=True)
            dw0 = dw0 + jnp.sum(dyc * pltpu.roll(win, 2, axis=0)[FFN_PAD:], axis=0, keepdims=True)
            return dw0, dw1, dw2, dbs + jnp.sum(dyc, axis=0, keepdims=True)

        zero = jnp.zeros((1, gw), F32)
        sums = lax.fori_loop(0, s // FFN_ROWS, grad_x, (zero, zero, zero, zero))
        for k in range(3):
            dwg_ref[pl.ds(k, 1), :] += sums[k][:, :tcf]
            dwl_ref[pl.ds(k, 1), :] += sums[k][:, tcf:]
        dbg_ref[...] += sums[3][:, :tcf]
        dbl_ref[...] += sums[3][:, tcf:]

    half = pl.BlockSpec((s, tcf), lambda j, bb: (bb, j))
    taps = pl.BlockSpec((3, tcf), lambda j, bb: (0, j))
    bias = pl.BlockSpec((1, tcf), lambda j, bb: (0, j))
    return _call(body, name=name, grid=(n_f, bsz), in_specs=_ffn_specs(s, tcf, n_f, False) + [half],
                 out_specs=[half, half, taps, taps, bias, bias],
                 out_shape=[_sds((t, f_dim), BF16)] * 2 + [_sds((3, f_dim), F32)] * 2 + [_sds((1, f_dim), F32)] * 2,
                 scratch=[pltpu.VMEM((FFN_PAD + s, gw), F32), pltpu.VMEM((SUBLANES, gw), F32),
                          pltpu.VMEM((s + FFN_PAD, gw), F32)],
                 sem=("parallel", "arbitrary"))(hp, hp, w, w, b, b, df)


def _split3(v):
    hi = v.astype(BF16)
    r = v - hi.astype(F32)
    mid = r.astype(BF16)
    lo = (r - mid.astype(F32)).astype(BF16)
    return hi, mid, lo


def _tri_dot(tri, v):
    out = None
    for part in _split3(v):
        term = jnp.dot(tri, part, preferred_element_type=F32)
        out = term if out is None else out + term
    return out


def _fgate_fwd(zf, bsz, heads, *, name):
    t, lanes = zf.shape
    s, blk, nb = _seq_tiles(t, bsz, (ATTN_BLOCK, 128))

    def body(z_ref, cumt_ref, cumb_ref, carry_ref):
        @pl.when(pl.program_id(1) == 0)
        def _():
            carry_ref[...] = jnp.zeros_like(carry_ref)

        z = z_ref[...]
        lf = jnp.minimum(z, 0.0) - jnp.log1p(jnp.exp(-jnp.abs(z)))
        r = lax.broadcasted_iota(jnp.int32, (blk, blk), 0)
        c = lax.broadcasted_iota(jnp.int32, (blk, blk), 1)
        tri = (r >= c).astype(BF16)
        cum = _tri_dot(tri, lf) + carry_ref[...]
        carry_ref[...] = cum[blk - 1:blk, :]
        cumt_ref[0] = jnp.transpose(cum)[:heads, :]
        for h in range(heads):
            cumb_ref[0, h] = jnp.broadcast_to(cum[:, h:h + 1], (blk, lanes))

    return _call(body, name=name, grid=(bsz, nb),
                 in_specs=[pl.BlockSpec((blk, lanes), lambda b, i: (b * nb + i, 0))],
                 out_specs=[pl.BlockSpec((1, heads, blk), lambda b, i: (b, 0, i)),
                            pl.BlockSpec((1, heads, blk, lanes), lambda b, i: (b, 0, i, 0))],
                 out_shape=[_sds((bsz, heads, s), F32), _sds((bsz, heads, s, lanes), F32)],
                 scratch=[pltpu.VMEM((1, lanes), F32)], sem=("parallel", "arbitrary"))(zf)


def _fgate_bwd(dcum, zf, bsz, *, name):
    t, lanes = zf.shape
    pairs = dcum.shape[1]
    s, blk, nb = _seq_tiles(t, bsz, (ATTN_BLOCK, 128))

    def body(d_ref, z_ref, o_ref, carry_ref):
        @pl.when(pl.program_id(1) == 0)
        def _():
            carry_ref[...] = jnp.zeros_like(carry_ref)

        dcol = d_ref[0, 0]
        for p in range(1, pairs):
            dcol = dcol + d_ref[0, p]
        r = lax.broadcasted_iota(jnp.int32, (blk, blk), 0)
        c = lax.broadcasted_iota(jnp.int32, (blk, blk), 1)
        tri = (c >= r).astype(BF16)
        suf = _tri_dot(tri, dcol) + carry_ref[...]
        carry_ref[...] = suf[0:1, :]
        o_ref[...] = suf * _sigmoid(-z_ref[...])

    return _call(body, name=name, grid=(bsz, nb),
                 in_specs=[pl.BlockSpec((1, pairs, blk, lanes), lambda b, i: (b, 0, nb - 1 - i, 0)),
                           pl.BlockSpec((blk, lanes), lambda b, i: (b * nb + nb - 1 - i, 0))],
                 out_specs=pl.BlockSpec((blk, lanes), lambda b, i: (b * nb + nb - 1 - i, 0)),
                 out_shape=_sds((t, lanes), F32), scratch=[pltpu.VMEM((1, lanes), F32)],
                 sem=("parallel", "arbitrary"))(dcum, zf)


def _to_features_major(z, col_off, width, n, *, name):
    t = z.shape[0]
    tr = _pick(t, (512, 256, 128))
    first = col_off // width

    def body(*refs):
        o_ref = refs[n]
        for g in range(n):
            o_ref[pl.ds(g * width, width), :] = jnp.transpose(refs[g][...].astype(F32)).astype(o_ref.dtype)

    return _call(body, name=name, grid=(t // tr,),
                 in_specs=[pl.BlockSpec((tr, width), lambda i, g=g: (i, first + g)) for g in range(n)],
                 out_specs=pl.BlockSpec((n * width, tr), lambda i: (0, i)),
                 out_shape=_sds((n * width, t), BF16), sem=("parallel",))(*([z] * n))


def _to_rows_major(xt, *, name):
    w, t = xt.shape
    tr = _pick(t, (512, 256, 128))

    def body(x_ref, o_ref):
        o_ref[...] = jnp.transpose(x_ref[...]).astype(o_ref.dtype)

    return _call(body, name=name, grid=(t // tr,),
                 in_specs=[pl.BlockSpec((w, tr), lambda i: (0, i))],
                 out_specs=pl.BlockSpec((tr, w), lambda i: (i, 0)),
                 out_shape=_sds((t, w), BF16), sem=("parallel",))(xt)


def _head_masks(shape, axis):
    feat = lax.broadcasted_iota(jnp.int32, shape, axis)
    return feat < HEAD_DIM, feat >= HEAD_DIM


def _attn_fwd(z, qkvt, cumt, cumb, bsz, heads, q_off, *, name):
    t = z.shape[0]
    width = heads * HEAD_DIM
    pairs = heads // 2
    s = t // bsz
    blk = ATTN_BLOCK
    nq = s // blk
    k_col = (q_off + width) // LANES
    v_row = 2 * width // LANES
    reps = blk // LANES

    def body(k_ref, qt_ref, vt_ref, cqt_ref, ckb_ref, ot_ref, lse_ref):
        p_id = pl.program_id(1)
        i = pl.program_id(2)
        qt = qt_ref[...]
        masks = _head_masks((LANES, blk), 0)
        qtm = [jnp.where(mk, qt, jnp.zeros_like(qt)) for mk in masks]
        cq = [cqt_ref[0, pl.ds(2 * p_id + hh, 1), :] for hh in range(2)]
        kidx = lax.broadcasted_iota(jnp.int32, (blk, blk), 0)
        qidx = lax.broadcasted_iota(jnp.int32, (blk, blk), 1)

        def block(j, carry, masked):
            off = pl.multiple_of(j * blk, blk)
            kp = k_ref[pl.ds(off, blk), :].astype(BF16)
            vtp = vt_ref[:, pl.ds(off, blk)]
            out = []
            for hh in range(2):
                m, l, acc = carry[hh]
                sc = jnp.dot(kp, qtm[hh], preferred_element_type=F32) * ATTN_SCALE
                ck = ckb_ref[0, hh, pl.ds(off, blk), :]
                sc = (sc + cq[hh]) - jnp.concatenate([ck] * reps, axis=1)
                if masked:
                    sc = jnp.where(qidx >= kidx, sc, NEG)
                m_new = jnp.maximum(m, jnp.max(sc, axis=0, keepdims=True))
                pr = jnp.exp(sc - m_new)
                a = jnp.exp(m - m_new)
                l = a * l + jnp.sum(pr, axis=0, keepdims=True)
                p_hi = pr.astype(BF16)
                p_lo = (pr - p_hi.astype(F32)).astype(BF16)
                pv = (jnp.dot(vtp, p_hi, preferred_element_type=F32)
                      + jnp.dot(vtp, p_lo, preferred_element_type=F32))
                acc = a * acc + pv[hh * HEAD_DIM:(hh + 1) * HEAD_DIM]
                out.append((m_new, l, acc))
            return tuple(out)

        init = tuple((jnp.full((1, blk), NEG, F32), jnp.zeros((1, blk), F32), jnp.zeros((HEAD_DIM, blk), F32))
                     for _ in range(2))
        carry = lax.fori_loop(0, i, lambda j, cr: block(j, cr, False), init)
        carry = block(i, carry, True)
        lse_ref[...] = jnp.zeros_like(lse_ref)
        for hh in range(2):
            m, l, acc = carry[hh]
            ot_ref[pl.ds(hh * HEAD_DIM, HEAD_DIM), :] = acc / l
            lse_ref[0, 0, pl.ds(hh, 1), :] = m + jnp.log(l)

    return _call(body, name=name, grid=(bsz, pairs, nq),
                 in_specs=[pl.BlockSpec((s, LANES), lambda b, p, i: (b, k_col + p)),
                           pl.BlockSpec((LANES, blk), lambda b, p, i: (p, b * nq + i)),
                           pl.BlockSpec((LANES, s), lambda b, p, i: (v_row + p, b)),
                           pl.BlockSpec((1, heads, blk), lambda b, p, i: (b, 0, i)),
                           pl.BlockSpec((1, 2, s, LANES), lambda b, p, i: (b, p, 0, 0))],
                 out_specs=[pl.BlockSpec((LANES, blk), lambda b, p, i: (p, b * nq + i)),
                            pl.BlockSpec((1, 1, SUBLANES, blk), lambda b, p, i: (b, p, 0, i))],
                 out_shape=[_sds((width, t), F32), _sds((bsz, pairs, SUBLANES, s), F32)],
                 sem=("parallel", "parallel", "parallel"))(z, qkvt, qkvt, cumt, cumb)


def _attn_bwd(z, qkvt, cumt, cumb, ot, do, dot, lse, bsz, heads, q_off, *, name):
    t = z.shape[0]
    width = heads * HEAD_DIM
    pairs = heads // 2
    s = t // bsz
    blk = ATTN_BLOCK
    nkv = s // blk
    q_col = q_off // LANES
    k_col = (q_off + width) // LANES
    v_col = (q_off + 2 * width) // LANES
    k_row = width // LANES
    reps = blk // LANES

    def body(k_ref, v_ref, kt_ref, q_ref, qt_ref, do_ref, dot_ref, ot_ref, lse_ref, ckb_ref, cqt_ref,
             dk_ref, dv_ref, dqt_ref, dcum_ref, dqt_acc, ds_acc):
        p_id = pl.program_id(1)
        j = pl.program_id(2)

        @pl.when(j == 0)
        def _():
            dqt_acc[...] = jnp.zeros_like(dqt_acc)

        kp = k_ref[...].astype(BF16)
        vp = v_ref[...].astype(BF16)
        kt = kt_ref[...]
        feat_masks = _head_masks((LANES, blk), 0)
        lane_masks = _head_masks((blk, LANES), 1)
        ktm = [jnp.where(mk, kt, jnp.zeros_like(kt)) for mk in feat_masks]
        ck = [jnp.concatenate([ckb_ref[0, hh]] * reps, axis=1) for hh in range(2)]
        kidx = lax.broadcasted_iota(jnp.int32, (blk, blk), 0)
        qidx = lax.broadcasted_iota(jnp.int32, (blk, blk), 1)
        ds_acc[...] = jnp.zeros_like(ds_acc)

        def block(i, carry, masked):
            dk, dv = carry
            off = pl.multiple_of(i * blk, blk)
            qt = qt_ref[:, pl.ds(off, blk)]
            dt = dot_ref[:, pl.ds(off, blk)]
            o_t = ot_ref[:, pl.ds(off, blk)]
            q_rows = q_ref[pl.ds(off, blk), :].astype(BF16)
            do_rows = do_ref[pl.ds(off, blk), :]
            for hh in range(2):
                qtm = jnp.where(feat_masks[hh], qt, jnp.zeros_like(qt))
                dtm = jnp.where(feat_masks[hh], dt, jnp.zeros_like(dt))
                sc = jnp.dot(kp, qtm, preferred_element_type=F32) * ATTN_SCALE
                sc = (sc + cqt_ref[0, pl.ds(2 * p_id + hh, 1), pl.ds(off, blk)]) - ck[hh]
                pr = jnp.exp(sc - lse_ref[0, 0, pl.ds(hh, 1), pl.ds(off, blk)])
                if masked:
                    pr = jnp.where(qidx >= kidx, pr, 0.0)
                dp = jnp.dot(vp, dtm, preferred_element_type=F32)
                delta = jnp.sum(dtm.astype(F32) * o_t, axis=0, keepdims=True)
                ds = pr * (dp - delta)
                ds_acc[hh] += ds
                dsb = ds.astype(BF16)
                qm = jnp.where(lane_masks[hh], q_rows, jnp.zeros_like(q_rows))
                dom = jnp.where(lane_masks[hh], do_rows, jnp.zeros_like(do_rows))
                dv = dv + jnp.dot(pr.astype(BF16), dom, preferred_element_type=F32)
                dk = dk + jnp.dot(dsb, qm, preferred_element_type=F32) * ATTN_SCALE
                dqt_acc[:, pl.ds(off, blk)] += jnp.dot(ktm[hh], dsb, preferred_element_type=F32) * ATTN_SCALE
            return dk, dv

        zero = jnp.zeros((blk, LANES), F32)
        carry = block(j, (zero, zero), True)
        dk, dv = lax.fori_loop(j + 1, nkv, lambda i, cr: block(i, cr, False), carry)
        dk_ref[...] = dk.astype(dk_ref.dtype)
        dv_ref[...] = dv.astype(dv_ref.dtype)
        lane = lax.broadcasted_iota(jnp.int32, (blk, LANES), 1)
        dcum = jnp.zeros((blk, LANES), F32)
        for hh in range(2):
            col = jnp.sum(ds_acc[hh], axis=1, keepdims=True)
            dcum = jnp.where(lane == 2 * p_id + hh, -col, dcum)
        dcum_ref[0, 0] = dcum

        @pl.when(j == nkv - 1)
        def _():
            dqt_ref[...] = dqt_acc[...]

    key_rows = lambda col: pl.BlockSpec((blk, LANES), lambda b, p, j: (b * nkv + j, col + p))
    seq_t = lambda row: pl.BlockSpec((LANES, s), lambda b, p, j: (row + p, b))
    return _call(body, name=name, grid=(bsz, pairs, nkv),
                 in_specs=[key_rows(k_col), key_rows(v_col),
                           pl.BlockSpec((LANES, blk), lambda b, p, j: (k_row + p, b * nkv + j)),
                           pl.BlockSpec((s, LANES), lambda b, p, j: (b, q_col + p)), seq_t(0),
                           pl.BlockSpec((s, LANES), lambda b, p, j: (b, p)), seq_t(0), seq_t(0),
                           pl.BlockSpec((1, 1, SUBLANES, s), lambda b, p, j: (b, p, 0, 0)),
                           pl.BlockSpec((1, 2, blk, LANES), lambda b, p, j: (b, p, j, 0)),
                           pl.BlockSpec((1, heads, s), lambda b, p, j: (b, 0, 0))],
                 out_specs=[key_rows(0), key_rows(0), seq_t(0),
                            pl.BlockSpec((1, 1, blk, LANES), lambda b, p, j: (b, p, j, 0))],
                 out_shape=[_sds((t, width), BF16), _sds((t, width), BF16), _sds((width, t), F32),
                            _sds((bsz, pairs, s, LANES), F32)],
                 scratch=[pltpu.VMEM((LANES, s), F32), pltpu.VMEM((2, blk, blk), F32)],
                 sem=("parallel", "parallel", "arbitrary"))(z, z, qkvt, z, qkvt, do, dot, ot, lse, cumb, cumt)


def _adamw(w, g, m, v, *, name):
    bc1 = 1.0 - ADAM_B1 ** ADAM_STEP
    bc2 = 1.0 - ADAM_B2 ** ADAM_STEP

    def body(w_ref, g_ref, m_ref, v_ref, d_ref, nm_ref, nv_ref):
        g_v = g_ref[...]
        nm = ADAM_B1 * m_ref[...] + (1.0 - ADAM_B1) * g_v
        nv = ADAM_B2 * v_ref[...] + (1.0 - ADAM_B2) * (g_v * g_v)
        nm_ref[...] = nm
        nv_ref[...] = nv
        d_ref[...] = -ADAM_LR * ((nm / bc1) / (jnp.sqrt(nv / bc2) + ADAM_EPS) + ADAM_WD * w_ref[...])

    if w.ndim == 2:
        grid = (1,)
        blk = pl.BlockSpec(w.shape, lambda i: (0, 0))
    else:
        layers, rows, cols = w.shape
        tr = rows if rows <= 256 else _pick(rows, (256, 128, 64, 32, 16, 8))
        grid = (layers, rows // tr)
        blk = pl.BlockSpec((1, tr, cols), lambda layer, i: (layer, i, 0))
    return tuple(_call(body, name=name, grid=grid, in_specs=[blk] * 4, out_specs=[blk] * 3,
                       out_shape=[_sds(w.shape, F32)] * 3, sem=("parallel",) * len(grid))(w, g, m, v))


_ANY = pl.BlockSpec(memory_space=pl.ANY)


def _comm_call(body, *, name, n_in, out_shape, n_sems):
    scratch = [pltpu.SemaphoreType.DMA((n_sems,)), pltpu.SemaphoreType.DMA((n_sems,)),
               pltpu.SemaphoreType.DMA((len(out_shape),))]
    return pl.pallas_call(body, name=name, in_specs=[_ANY] * n_in, out_specs=[_ANY] * len(out_shape),
                          out_shape=out_shape, scratch_shapes=scratch)


def _place():
    x, y, c = lax.axis_index("x"), lax.axis_index("y"), lax.axis_index("c")
    return x, y, c, [(1 - x, y), (x, 1 - y), (1 - x, 1 - y)]


def _remote(src, dst, send_sems, recv_sems, sem, to):
    return pltpu.make_async_remote_copy(src_ref=src, dst_ref=dst, send_sem=send_sems.at[sem],
                                        recv_sem=recv_sems.at[sem], device_id=to, device_id_type=MESH)


def _all_gather8(v, *, name):
    def body(v_ref, out_ref, send_sems, recv_sems, local_sems):
        x, y, c, _ = _place()
        me = 4 * x + 2 * y + c
        mine = pltpu.make_async_copy(v_ref, out_ref.at[me], local_sems.at[0])
        mine.start()
        peers = []
        for k in range(1, N_DEVICES):
            px = 1 - x if k & 4 else x
            py = 1 - y if k & 2 else y
            pc = 1 - c if k & 1 else c
            peers.append((px, py, pc))
        sends = [_remote(v_ref, out_ref.at[me], send_sems, recv_sems, k, peer) for k, peer in enumerate(peers)]
        for cp in sends:
            cp.start()
        for k, (px, py, pc) in enumerate(peers):
            _remote(v_ref, out_ref.at[4 * px + 2 * py + pc], send_sems, recv_sems, k, (px, py, pc)).wait_recv()
        for cp in sends:
            cp.wait_send()
        mine.wait()

    out = _comm_call(body, name=name, n_in=1, out_shape=[_sds((N_DEVICES,) + v.shape, v.dtype)],
                     n_sems=N_DEVICES - 1)(v)
    return out[0]


def _window(ref, mode, layer, chip, rows, cols):
    if mode == "slab":
        return ref.at[layer, chip]
    if mode == "cols":
        return ref.at[layer, :, pl.ds(pl.multiple_of(chip * cols, LANES), cols)]
    return ref.at[layer, pl.ds(pl.multiple_of(chip * rows, SUBLANES), rows), :]


def _whole_shape(mode, shard_shape):
    layers, rows, cols = shard_shape
    if mode == "slab":
        return (layers, N_CHIPS, rows, cols)
    if mode == "cols":
        assert cols % LANES == 0
        return (layers, rows, N_CHIPS * cols)
    assert rows % 16 == 0
    return (layers, N_CHIPS * rows, cols)


def _gather_weights(shards, modes, *, name):
    n = len(shards)
    meta = [(mode,) + tuple(a.shape[1:]) for a, mode in zip(shards, modes)]
    for a in shards:
        assert a.shape[0] == 2
    per = 7

    def body(*refs):
        ins, outs = refs[:n], refs[n:2 * n]
        send_sems, recv_sems, _ = refs[2 * n:]
        x, y, c, chips = _place()
        me = 2 * x + y
        sibling = (x, y, 1 - c)
        own, first, passed = [], [], []
        for i, (mode, rows, cols) in enumerate(meta):
            for r, (cx, cy) in enumerate(chips):
                cp = _remote(ins[i].at[c], _window(outs[i], mode, c, me, rows, cols), send_sems, recv_sems,
                             per * i + r, (cx, cy, c))
                cp.start()
                first.append(cp)
            cp = _remote(ins[i], _window(outs[i], mode, slice(None), me, rows, cols), send_sems, recv_sems,
                         per * i + 6, sibling)
            cp.start()
            own.append(cp)
        for i, (mode, rows, cols) in enumerate(meta):
            for r, (cx, cy) in enumerate(chips):
                win = _window(outs[i], mode, c, 2 * cx + cy, rows, cols)
                _remote(win, win, send_sems, recv_sems, per * i + r, (cx, cy, c)).wait_recv()
                cp = _remote(win, win, send_sems, recv_sems, per * i + 3 + r, sibling)
                cp.start()
                passed.append(cp)
        for i, (mode, rows, cols) in enumerate(meta):
            own[i].wait_recv()
            for r, (cx, cy) in enumerate(chips):
                win = _window(outs[i], mode, 1 - c, 2 * cx + cy, rows, cols)
                _remote(win, win, send_sems, recv_sems, per * i + 3 + r, sibling).wait_recv()
        for cp in first + passed + own:
            cp.wait_send()

    out_shape = [_sds(_whole_shape(mode, a.shape), a.dtype) for a, mode in zip(shards, modes)]
    return _comm_call(body, name=name, n_in=n, out_shape=out_shape, n_sems=per * n)(*shards)


def _rs_swap(grads, *, name):
    n = len(grads)

    def body(*refs):
        ins, outs = refs[:n], refs[n:2 * n]
        send_sems, recv_sems, _ = refs[2 * n:]
        x, y, c, _ = _place()
        copies = [_remote(ins[i].at[1 - c], outs[i], send_sems, recv_sems, i, (x, y, 1 - c)) for i in range(n)]
        for cp in copies:
            cp.start()
        for cp in copies:
            cp.wait()

    return _comm_call(body, name=name, n_in=n, out_shape=[_sds(g.shape[1:], g.dtype) for g in grads], n_sems=n)(*grads)


def _part(ref, mode, chip, rows, cols):
    if mode == "slab":
        return ref.at[chip]
    if mode == "cols":
        return ref.at[:, pl.ds(pl.multiple_of(chip * cols, LANES), cols)]
    return ref.at[pl.ds(pl.multiple_of(chip * rows, SUBLANES), rows), :]


def _rs_scatter(parts, modes, shard_shapes, *, name):
    n = len(parts)
    meta = [(mode,) + tuple(shp[1:]) for mode, shp in zip(modes, shard_shapes)]

    def body(*refs):
        ins, outs = refs[:n], refs[n:2 * n]
        send_sems, recv_sems, local_sems = refs[2 * n:]
        x, y, c, chips = _place()
        me = 2 * x + y
        local, sends = [], []
        for i, (mode, rows, cols) in enumerate(meta):
            cp = pltpu.make_async_copy(_part(ins[i], mode, me, rows, cols), outs[i].at[me], local_sems.at[i])
            cp.start()
            local.append(cp)
            for r, (cx, cy) in enumerate(chips):
                cp = _remote(_part(ins[i], mode, 2 * cx + cy, rows, cols), outs[i].at[me], send_sems, recv_sems,
                             3 * i + r, (cx, cy, c))
                cp.start()
                sends.append(cp)
        for i, (mode, rows, cols) in enumerate(meta):
            for r, (cx, cy) in enumerate(chips):
                k = 2 * cx + cy
                _remote(_part(ins[i], mode, k, rows, cols), outs[i].at[k], send_sems, recv_sems, 3 * i + r,
                        (cx, cy, c)).wait_recv()
        for cp in sends:
            cp.wait_send()
        for cp in local:
            cp.wait()

    out_shape = [_sds((N_CHIPS,) + tuple(shp[1:]), p.dtype) for p, shp in zip(parts, shard_shapes)]
    return _comm_call(body, name=name, n_in=n, out_shape=out_shape, n_sems=3 * n)(*parts)


def _rs_exchange(sums, *, name):
    n = len(sums)

    def body(*refs):
        ins, outs = refs[:n], refs[n:2 * n]
        send_sems, recv_sems, _ = refs[2 * n:]
        x, y, c, _ = _place()
        copies = [_remote(ins[i], outs[i], send_sems, recv_sems, i, (x, y, 1 - c)) for i in range(n)]
        for cp in copies:
            cp.start()
        for cp in copies:
            cp.wait()

    return _comm_call(body, name=name, n_in=n, out_shape=[_sds(s.shape, s.dtype) for s in sums], n_sems=n)(*sums)


def _row_tile(rows, cols, itemsize):
    target = max(SUBLANES, (2 << 20) // (cols * itemsize))
    cands = [c for c in (2048, 1024, 512, 256, 128, 64, 32, 16) if c <= target]
    tr = _pick(rows, cands)
    return tr


def _add_layer(g, other, core, *, name):
    _, rows, cols = g.shape
    tr = _row_tile(rows, cols, 4)

    def body(core_ref, g_ref, o_ref, out_ref):
        out_ref[...] = (g_ref[0] + o_ref[...]).astype(out_ref.dtype)

    grid_spec = pltpu.PrefetchScalarGridSpec(
        num_scalar_prefetch=1, grid=(rows // tr,),
        in_specs=[pl.BlockSpec((1, tr, cols), lambda i, core_ref: (core_ref[0], i, 0)),
                  pl.BlockSpec((tr, cols), lambda i, core_ref: (i, 0))],
        out_specs=pl.BlockSpec((tr, cols), lambda i, core_ref: (i, 0)))
    return pl.pallas_call(body, name=name, grid_spec=grid_spec, out_shape=_sds((rows, cols), BF16),
                          compiler_params=pltpu.CompilerParams(dimension_semantics=("parallel",),
                                                               vmem_limit_bytes=VMEM_LIMIT))(core, g, other)


def _sum_slots(parts, *, name):
    n, rows, cols = parts.shape
    tr = _row_tile(rows, cols, 4)

    def body(p_ref, o_ref):
        acc = p_ref[0].astype(F32) + p_ref[1].astype(F32)
        for k in range(2, n):
            acc = acc + p_ref[k].astype(F32)
        o_ref[...] = acc

    return _call(body, name=name, grid=(rows // tr,),
                 in_specs=[pl.BlockSpec((n, tr, cols), lambda i: (0, i, 0))],
                 out_specs=pl.BlockSpec((tr, cols), lambda i: (i, 0)),
                 out_shape=_sds((rows, cols), F32), sem=("parallel",))(parts)


def _reduce_scatter(grads, modes, shard_shapes):
    core = lax.axis_index("c").astype(jnp.int32).reshape(1)
    flat = [g.reshape(g.shape[0], -1, g.shape[-1]) for g in grads]
    from_sibling = _rs_swap(flat, name="rs_swap")
    parts = []
    for i, (g, o) in enumerate(zip(flat, from_sibling)):
        p = _add_layer(g, o, core, name=f"rs_add_{i}")
        parts.append(p.reshape(grads[i].shape[1:]))
    from_chips = _rs_scatter(parts, modes, shard_shapes, name="rs_scatter")
    sums = [_sum_slots(r, name=f"rs_sum_{i}") for i, r in enumerate(from_chips)]
    others = _rs_exchange(sums, name="rs_exchange")
    mine_first = lax.axis_index("c") == 0
    return [jnp.where(mine_first, jnp.stack([mine, other]), jnp.stack([other, mine]))
            for mine, other in zip(sums, others)]


def _layer_weights(full, rep, layer, dims):
    f_off, n_heads = dims["f_off"], dims["heads"]
    d_ff = full["w_ffn_up"].shape[-1] // 2
    w_in = full["w_in"][layer]
    b_in = rep["b_in"][layer]
    pad = LANES - n_heads
    return {
        "w_main": jnp.concatenate([w_in[:, :f_off], w_in[:, f_off + n_heads:]], axis=1),
        "b_main": jnp.concatenate([b_in[:f_off], b_in[f_off + n_heads:]])[None],
        "w_f": jnp.pad(w_in[:, f_off:f_off + n_heads], ((0, 0), (0, pad))),
        "b_f": jnp.pad(b_in[f_off:f_off + n_heads], (0, pad))[None],
        "conv_a_w": full["conv_a_w"][layer],
        "conv_a_b": rep["conv_a_b"][layer][None],
        "ln_conv_g": rep["ln_conv_g"][layer][None],
        "ln_conv_b": rep["ln_conv_b"][layer][None],
        "w_conv_proj": full["w_conv_proj"][layer],
        "w_attn_proj": full["w_attn_proj"][layer],
        "w_mix_out": full["w_mix_out"][layer],
        "b_mix_out": rep["b_mix_out"][layer][None],
        "ln1_g": rep["ln1_g"][layer][None],
        "ln1_b": rep["ln1_b"][layer][None],
        "w_ffn_up": full["w_ffn_up"][layer],
        "w_ffn_up_gate": full["w_ffn_up"][layer][:, :d_ff],
        "w_ffn_up_lin": full["w_ffn_up"][layer][:, d_ff:],
        "ffn_conv_w": full["ffn_conv_w"][layer],
        "ffn_conv_b": rep["ffn_conv_b"][layer][None],
        "w_ffn_down": full["w_ffn_down"][layer],
        "ln2_g": rep["ln2_g"][layer][None],
        "ln2_b": rep["ln2_b"][layer][None],
    }


def _layer_fwd(x, mod, p, dims, tag):
    bsz, d, ch, heads, alpha = dims["bsz"], dims["d"], dims["ch"], dims["heads"], dims["alpha"]
    mods = [mod[:, k * d:(k + 1) * d][:, None, :] for k in range(6)]
    shift1, scale1, gate1, shift2, scale2, gate2 = mods
    u = _ln_mod_fwd(x, scale1, shift1, bsz, name=f"ln_mod1_{tag}")
    zm = _matmul(u, p["w_main"], "nn", BF16, bias=p["b_main"], name=f"in_main_{tag}")
    zf = _matmul(u, p["w_f"], "nn", F32, bias=p["b_f"], name=f"in_forget_{tag}")
    a0 = _glu_fwd(zm, ch, name=f"glu_{tag}")
    a1 = _dwconv_fwd(a0, p["conv_a_w"], p["conv_a_b"], bsz, CONV_A_HALO, name=f"conv_a_{tag}")
    a3 = _lnsilu_fwd(a1, p["ln_conv_g"], p["ln_conv_b"], name=f"lnsilu_{tag}")
    ya = _matmul(a3, p["w_conv_proj"], "nn", F32, name=f"conv_proj_{tag}")
    cumt, cumb = _fgate_fwd(zf, bsz, heads, name=f"fgate_{tag}")
    qkvt = _to_features_major(zm, 2 * ch, heads * HEAD_DIM, 3, name=f"qkv_t_{tag}")
    ot, lse = _attn_fwd(zm, qkvt, cumt, cumb, bsz, heads, 2 * ch, name=f"attn_{tag}")
    yb = _matmul(ot, p["w_attn_proj"], "tn", F32, name=f"attn_proj_{tag}")
    m = _gate_merge_fwd(zm, ya, yb, dims["ga_off"], name=f"merge_{tag}")
    mix = _matmul(m, p["w_mix_out"], "nn", F32, bias=p["b_mix_out"], name=f"mix_out_{tag}")
    x1 = _ln_res_fwd(x, mix, gate1, p["ln1_g"], p["ln1_b"], alpha, bsz, name=f"ln_res1_{tag}")
    u2 = _ln_mod_fwd(x1, scale2, shift2, bsz, name=f"ln_mod2_{tag}")
    hp = _matmul(u2, p["w_ffn_up"], "nn", BF16, name=f"ffn_up_{tag}")
    f = _ffn_act_fwd(hp, p["ffn_conv_w"], p["ffn_conv_b"], bsz, dims["tcf"], name=f"ffn_act_{tag}")
    ffn = _matmul(f, p["w_ffn_down"], "nn", F32, name=f"ffn_down_{tag}")
    x2 = _ln_res_fwd(x1, ffn, gate2, p["ln2_g"], p["ln2_b"], alpha, bsz, name=f"ln_res2_{tag}")
    saved = dict(x=x, mods=mods, u=u, zm=zm, zf=zf, a0=a0, a1=a1, a3=a3, ya=ya, yb=yb, cumt=cumt, cumb=cumb,
                 qkvt=qkvt, ot=ot, lse=lse, m=m, mix=mix, x1=x1, u2=u2, hp=hp, f=f, ffn=ffn)
    return x2, saved


def _layer_bwd(dx2, p, sv, dims, tag):
    bsz, ch, heads, alpha = dims["bsz"], dims["ch"], dims["heads"], dims["alpha"]
    f_off, tcf = dims["f_off"], dims["tcf"]
    shift1, scale1, gate1, shift2, scale2, gate2 = sv["mods"]
    g = {}
    dr2, dffn, dgate2, g["ln2_g"], g["ln2_b"], _ = _ln_res_bwd(
        dx2, sv["x1"], sv["ffn"], gate2, p["ln2_g"], alpha, bsz, name=f"ln_res2_bwd_{tag}")
    df = _matmul(dffn, p["w_ffn_down"], "nt", F32, name=f"ffn_down_dx_{tag}")
    g["w_ffn_down"] = _matmul(sv["f"], dffn, "tn", F32, name=f"ffn_down_dw_{tag}")
    dhg, dhl, dwg, dwl, dbg, dbl = _ffn_act_bwd(sv["hp"], df, p["ffn_conv_w"], p["ffn_conv_b"], bsz, tcf,
                                                name=f"ffn_act_bwd_{tag}")
    g["ffn_conv_w"] = jnp.concatenate([dwg, dwl], axis=1)
    g["ffn_conv_b"] = jnp.concatenate([dbg, dbl], axis=1)[0]
    du2 = _matmul(dhg, p["w_ffn_up_gate"], "nt", F32, name=f"ffn_up_gate_dx_{tag}")
    du2 = _matmul(dhl, p["w_ffn_up_lin"], "nt", F32, add=du2, name=f"ffn_up_lin_dx_{tag}")
    g["w_ffn_up"] = jnp.concatenate([_matmul(sv["u2"], dhg, "tn", F32, name=f"ffn_up_gate_dw_{tag}"),
                                     _matmul(sv["u2"], dhl, "tn", F32, name=f"ffn_up_lin_dw_{tag}")], axis=1)
    dx1, dscale2, dshift2 = _ln_mod_bwd(du2, sv["x1"], scale2, dr2, alpha, bsz, name=f"ln_mod2_bwd_{tag}")
    dr1, dmix, dgate1, g["ln1_g"], g["ln1_b"], g["b_mix_out"] = _ln_res_bwd(
        dx1, sv["x"], sv["mix"], gate1, p["ln1_g"], alpha, bsz, name=f"ln_res1_bwd_{tag}")
    dm = _matmul(dmix, p["w_mix_out"], "nt", F32, name=f"mix_out_dx_{tag}")
    g["w_mix_out"] = _matmul(sv["m"], dmix, "tn", F32, name=f"mix_out_dw_{tag}")
    dya, dyb, dzga, dzgb = _gate_merge_bwd(sv["zm"], sv["ya"], sv["yb"], dm, dims["ga_off"], name=f"merge_bwd_{tag}")
    da3 = _matmul(dya, p["w_conv_proj"], "nt", F32, name=f"conv_proj_dx_{tag}")
    g["w_conv_proj"] = _matmul(sv["a3"], dya, "tn", F32, name=f"conv_proj_dw_{tag}")
    do = _matmul(dyb, p["w_attn_proj"], "nt", BF16, name=f"attn_proj_dx_{tag}")
    dot = _matmul(p["w_attn_proj"], dyb, "nt", BF16, name=f"attn_proj_dxt_{tag}")
    g["w_attn_proj"] = _matmul(sv["ot"], dyb, "nn", F32, name=f"attn_proj_dw_{tag}")
    da1, g["ln_conv_g"], g["ln_conv_b"] = _lnsilu_bwd(sv["a1"], da3, p["ln_conv_g"], p["ln_conv_b"],
                                                      name=f"lnsilu_bwd_{tag}")
    da0, g["conv_a_w"], dcb = _dwconv_bwd(sv["a0"], da1, p["conv_a_w"], bsz, CONV_A_HALO, F32, name=f"conv_a_bwd_{tag}")
    g["conv_a_b"] = dcb[0]
    dzglu = _glu_bwd(sv["zm"], da0, ch, name=f"glu_bwd_{tag}")
    dk, dv, dqt, dcum = _attn_bwd(sv["zm"], sv["qkvt"], sv["cumt"], sv["cumb"], sv["ot"], do, dot, sv["lse"], bsz,
                                  heads, 2 * ch, name=f"attn_bwd_{tag}")
    dq = _to_rows_major(dqt, name=f"dq_rows_{tag}")
    dzf = _fgate_bwd(dcum, sv["zf"], bsz, name=f"fgate_bwd_{tag}")
    dzm = jnp.concatenate([dzglu, dq, dk, dv, dzga, dzgb], axis=1)
    du = _matmul(dzm, p["w_main"], "nt", F32, name=f"in_main_dx_{tag}")
    du = _matmul(dzf, p["w_f"], "nt", F32, add=du, name=f"in_forget_dx_{tag}")
    dwm, dbm = _matmul(sv["u"], dzm, "tn", F32, colsum=True, name=f"in_main_dw_{tag}")
    dwf, dbf = _matmul(sv["u"], dzf, "tn", F32, colsum=True, name=f"in_forget_dw_{tag}")
    dbm, dbf = dbm[0], dbf[0]
    g["w_in"] = jnp.concatenate([dwm[:, :f_off], dwf[:, :heads], dwm[:, f_off:]], axis=1)
    g["b_in"] = jnp.concatenate([dbm[:f_off], dbf[:heads], dbm[f_off:]])
    dx, dscale1, dshift1 = _ln_mod_bwd(du, sv["x"], scale1, dr1, alpha, bsz, name=f"ln_mod1_bwd_{tag}")
    dmod = jnp.concatenate([dshift1, dscale1, dgate1, dshift2, dscale2, dgate2], axis=2)[:, 0, :]
    return dx, g, dmod


def _local_step(x, mod, loss_target, full, rep, dims):
    bsz, seq, d = x.shape
    layers = mod.shape[0]
    params = [_layer_weights(full, rep, layer, dims) for layer in range(layers)]
    h = x.reshape(bsz * seq, d)
    saved = []
    for layer in range(layers):
        h, sv = _layer_fwd(h, mod[layer], params[layer], dims, f"l{layer}")
        saved.append(sv)
    dh, sq = _loss_head(h, loss_target.reshape(bsz * seq, d), name="loss_head")
    loss_local = 0.5 * jnp.sum(sq) / d
    grads, dmods = [None] * layers, [None] * layers
    for layer in reversed(range(layers)):
        dh, grads[layer], dmods[layer] = _layer_bwd(dh, params[layer], saved[layer], dims, f"l{layer}")
    stacked = {wname: jnp.stack([grads[layer][wname] for layer in range(layers)]) for wname in grads[0]}
    return loss_local, dh.reshape(bsz, seq, d), stacked, jnp.stack(dmods)


def _pad_rows(a):
    extra = -a.shape[-2] % (2 * SUBLANES)
    if extra == 0:
        return a
    return jnp.pad(a, [(0, 0)] * (a.ndim - 2) + [(0, extra), (0, 0)])


def _to_slab(g):
    layers, k, n4 = g.shape
    return jnp.transpose(g.reshape(layers, k, N_CHIPS, n4 // N_CHIPS), (0, 2, 1, 3))


def _from_slab(w):
    layers, _, k, n = w.shape
    return jnp.transpose(w, (0, 2, 1, 3)).reshape(layers, k, N_CHIPS * n)


def kernel(x, c, w_ada, b_ada, w_in, b_in, conv_a_w, conv_a_b, ln_conv_g, ln_conv_b, w_conv_proj, w_attn_proj, w_mix_out, b_mix_out, ln1_g, ln1_b, w_ffn_up, ffn_conv_w, ffn_conv_b, w_ffn_down, ln2_g, ln2_b, loss_target, m_w_ada, m_b_ada, m_w_in, m_b_in, m_conv_a_w, m_conv_a_b, m_ln_conv_g, m_ln_conv_b, m_w_conv_proj, m_w_attn_proj, m_w_mix_out, m_b_mix_out, m_ln1_g, m_ln1_b, m_w_ffn_up, m_ffn_conv_w, m_ffn_conv_b, m_w_ffn_down, m_ln2_g, m_ln2_b, v_w_ada, v_b_ada, v_w_in, v_b_in, v_conv_a_w, v_conv_a_b, v_ln_conv_g, v_ln_conv_b, v_w_conv_proj, v_w_attn_proj, v_w_mix_out, v_b_mix_out, v_ln1_g, v_ln1_b, v_w_ffn_up, v_ffn_conv_w, v_ffn_conv_b, v_w_ffn_down, v_ln2_g, v_ln2_b):
    weights = dict(zip(WEIGHTS, (w_ada, b_ada, w_in, b_in, conv_a_w, conv_a_b, ln_conv_g, ln_conv_b, w_conv_proj,
                                 w_attn_proj, w_mix_out, b_mix_out, ln1_g, ln1_b, w_ffn_up, ffn_conv_w, ffn_conv_b,
                                 w_ffn_down, ln2_g, ln2_b)))
    mom1 = dict(zip(WEIGHTS, (m_w_ada, m_b_ada, m_w_in, m_b_in, m_conv_a_w, m_conv_a_b, m_ln_conv_g, m_ln_conv_b,
                              m_w_conv_proj, m_w_attn_proj, m_w_mix_out, m_b_mix_out, m_ln1_g, m_ln1_b, m_w_ffn_up,
                              m_ffn_conv_w, m_ffn_conv_b, m_w_ffn_down, m_ln2_g, m_ln2_b)))
    mom2 = dict(zip(WEIGHTS, (v_w_ada, v_b_ada, v_w_in, v_b_in, v_conv_a_w, v_conv_a_b, v_ln_conv_g, v_ln_conv_b,
                              v_w_conv_proj, v_w_attn_proj, v_w_mix_out, v_b_mix_out, v_ln1_g, v_ln1_b, v_w_ffn_up,
                              v_ffn_conv_w, v_ffn_conv_b, v_w_ffn_down, v_ln2_g, v_ln2_b)))
    bsz, seq, d = x.shape
    layers = w_ada.shape[0]
    ch = conv_a_w.shape[2] * N_CHIPS
    width = w_attn_proj.shape[1]
    heads = width // HEAD_DIM
    d_ff = w_ffn_down.shape[1] * N_CHIPS
    dims = dict(bsz=bsz, d=d, ch=ch, heads=heads, alpha=(2.0 * layers) ** 0.25, f_off=2 * ch + 3 * width,
                ga_off=2 * ch + 3 * width, tcf=_pick(d_ff, (256, 128)))
    chip = 2 * lax.axis_index("x") + lax.axis_index("y")
    device = 2 * chip + lax.axis_index("c")
    ada_cols = w_ada.shape[2]

    c_act = _silu_rows(_all_gather8(c, name="gather_c").reshape(N_DEVICES * bsz, d), name="silu_c")
    b_ada_mine = lax.dynamic_slice_in_dim(b_ada, chip * ada_cols, ada_cols, axis=1)
    mod_cols = jnp.stack([_matmul(c_act, w_ada[layer], "nn", F32, bias=b_ada_mine[layer][None], name=f"ada_l{layer}")
                          for layer in range(layers)])
    mod_all = _all_gather8(mod_cols, name="gather_mod")
    mod_all = jnp.concatenate([mod_all[2 * k] for k in range(N_CHIPS)], axis=-1)
    mod = lax.dynamic_slice_in_dim(mod_all, device * bsz, bsz, axis=1)

    shards = [_pad_rows(weights[wname].astype(BF16) if as_bf16 else weights[wname]) for wname, _, as_bf16 in GATHERED]
    modes = [mode for _, mode, _ in GATHERED]
    whole = _gather_weights(shards, modes, name="gather_weights")
    full = {wname: w[:, :weights[wname].shape[1]] if mode == "cols" else w
            for (wname, mode, _), w in zip(GATHERED, whole)}
    full["w_in"] = _from_slab(full["w_in"])
    rep = {wname: weights[wname] for wname in REPLICATED}

    loss_local, grad_x, grads, dmod = _local_step(x, mod, loss_target, full, rep, dims)
    loss = lax.psum(loss_local, ("x", "y", "c"))

    grads["w_in"] = _to_slab(grads["w_in"])
    shard_shapes = [s.shape for s in shards]
    reduced = _reduce_scatter([_pad_rows(grads[wname]) for wname, _, _ in GATHERED], modes, shard_shapes)
    grad = {wname: r[:, :weights[wname].shape[1]] for (wname, _, _), r in zip(GATHERED, reduced)}

    small = jnp.concatenate([dmod.reshape(-1)] + [grads[wname].reshape(-1) for wname in REPLICATED])
    n_small = small.shape[0]
    rows = -(-n_small // (SUBLANES * LANES)) * SUBLANES
    small = jnp.pad(small, (0, rows * LANES - n_small)).reshape(rows, LANES)
    gathered = _all_gather8(small, name="gather_small")
    n_dmod = dmod.size
    dmod_all = gathered.reshape(N_DEVICES, -1)[:, :n_dmod].reshape(N_DEVICES, layers, bsz, 6 * d)
    dmod_all = jnp.transpose(dmod_all, (1, 0, 2, 3)).reshape(layers, N_DEVICES * bsz, 6 * d)
    summed = _sum_slots(gathered, name="sum_small").reshape(-1)
    off = n_dmod
    for wname in REPLICATED:
        n = weights[wname].size
        grad[wname] = summed[off:off + n].reshape(weights[wname].shape)
        off += n
    dmod_mine = lax.dynamic_slice_in_dim(dmod_all, chip * ada_cols, ada_cols, axis=2)
    grad["w_ada"] = jnp.stack([_matmul(c_act, dmod_mine[layer], "tn", F32, name=f"ada_dw_l{layer}")
                               for layer in range(layers)])
    grad["b_ada"] = jnp.stack([_colsum(dmod_all[layer], name=f"ada_db_l{layer}")[0] for layer in range(layers)])

    delta, new_m, new_v = {}, {}, {}
    for wname in WEIGHTS:
        delta[wname], new_m[wname], new_v[wname] = _adamw(weights[wname], grad[wname], mom1[wname], mom2[wname],
                                                          name=f"adamw_{wname}")
    return (loss, grad_x, *[grad[wname] for wname in WEIGHTS], *[delta[wname] for wname in WEIGHTS],
            *[new_m[wname] for wname in WEIGHTS], *[new_v[wname] for wname in WEIGHTS])
```

```python
import math

import jax
import jax.numpy as jnp
from jax import lax
from jax.experimental import pallas as pl
from jax.experimental.pallas import tpu as pltpu

F32 = jnp.float32
BF16 = jnp.bfloat16
MESH = pl.DeviceIdType.MESH

LN_EPS = 1e-5
HEAD_DIM = 64
ATTN_SCALE = HEAD_DIM ** -0.5
NEG = -1e30
FFN_PAD = 8
LANES = 128
SUBLANES = 8
ROW_CHUNK = 256
ATTN_BLOCK = 256
N_CHIPS = 4
N_DEVICES = 8
VMEM_LIMIT = 56 * 1024 * 1024

ADAM_LR = 0.001
ADAM_B1 = 0.9
ADAM_B2 = 0.999
ADAM_EPS = 1e-08
ADAM_WD = 0.01
ADAM_STEP = 10

GATHERED = (("w_in", "slab", True), ("conv_a_w", "cols", False), ("w_conv_proj", "cols", True),
            ("w_attn_proj", "cols", True), ("w_mix_out", "rows", True), ("w_ffn_up", "cols", True),
            ("ffn_conv_w", "cols", False), ("w_ffn_down", "rows", True))
REPLICATED = ("b_in", "conv_a_b", "ln_conv_g", "ln_conv_b", "b_mix_out", "ln1_g", "ln1_b",
              "ffn_conv_b", "ln2_g", "ln2_b")
WEIGHTS = ("w_ada", "b_ada", "w_in", "b_in", "conv_a_w", "conv_a_b", "ln_conv_g", "ln_conv_b",
           "w_conv_proj", "w_attn_proj", "w_mix_out", "b_mix_out", "ln1_g", "ln1_b", "w_ffn_up",
           "ffn_conv_w", "ffn_conv_b", "w_ffn_down", "ln2_g", "ln2_b")


def _pick(n, cands):
    for cand in cands:
        if n % cand == 0:
            return cand
    return n


def _call(body, *, name, grid, in_specs, out_specs, out_shape, scratch=(), sem=None):
    return pl.pallas_call(
        body, name=name, grid=grid, in_specs=in_specs, out_specs=out_specs, out_shape=out_shape,
        scratch_shapes=list(scratch),
        compiler_params=pltpu.CompilerParams(dimension_semantics=sem, vmem_limit_bytes=VMEM_LIMIT))


def _sds(shape, dtype):
    return jax.ShapeDtypeStruct(tuple(shape), dtype)


def _chunked(rows, fn):
    chunk = min(ROW_CHUNK, rows)
    if rows == chunk:
        fn(pl.ds(0, rows))
        return

    def step(i, carry):
        fn(pl.ds(pl.multiple_of(i * chunk, chunk), chunk))
        return carry

    lax.fori_loop(0, rows // chunk, step, 0)


def _matmul(a, b, mode, out_dtype, *, bias=None, add=None, colsum=False, name):
    if mode == "nn":
        (m, k), (_, n) = a.shape, b.shape
    elif mode == "nt":
        (m, k), (n, _) = a.shape, b.shape
    else:
        (k, m), (_, n) = a.shape, b.shape
    tm = _pick(m, (1024, 1408, 512, 256, 128))
    tn = _pick(n, (512, 1408, 256, 128))
    tk = k if k <= 1536 else _pick(k, (1024, 1536, 1408, 512, 256, 128))
    nk = k // tk
    if mode == "nn":
        a_spec = pl.BlockSpec((tm, tk), lambda i, j, kk: (i, kk))
        b_spec = pl.BlockSpec((tk, tn), lambda i, j, kk: (kk, j))
        dims = (((1,), (0,)), ((), ()))
    elif mode == "nt":
        a_spec = pl.BlockSpec((tm, tk), lambda i, j, kk: (i, kk))
        b_spec = pl.BlockSpec((tn, tk), lambda i, j, kk: (j, kk))
        dims = (((1,), (1,)), ((), ()))
    else:
        a_spec = pl.BlockSpec((tk, tm), lambda i, j, kk: (kk, i))
        b_spec = pl.BlockSpec((tk, tn), lambda i, j, kk: (kk, j))
        dims = (((0,), (0,)), ((), ()))
    in_specs = [a_spec, b_spec]
    operands = [a, b]
    if bias is not None:
        in_specs.append(pl.BlockSpec((1, tn), lambda i, j, kk: (0, j)))
        operands.append(bias)
    if add is not None:
        in_specs.append(pl.BlockSpec((tm, tn), lambda i, j, kk: (i, j)))
        operands.append(add)

    def body(a_ref, b_ref, *rest):
        rest = list(rest)
        bias_ref = rest.pop(0) if bias is not None else None
        add_ref = rest.pop(0) if add is not None else None
        o_ref = rest.pop(0)
        prod = lax.dot_general(a_ref[...].astype(BF16), b_ref[...].astype(BF16), dims,
                               preferred_element_type=F32)
        if colsum:
            cs_ref = rest.pop(0)
            part = jnp.sum(b_ref[...].astype(F32), axis=0, keepdims=True)

            @pl.when(pl.program_id(2) == 0)
            def _():
                cs_ref[...] = part

            @pl.when(pl.program_id(2) > 0)
            def _():
                cs_ref[...] += part

        def finish(r):
            if bias_ref is not None:
                r = r + bias_ref[...]
            if add_ref is not None:
                r = r + add_ref[...]
            o_ref[...] = r.astype(o_ref.dtype)

        if nk == 1:
            finish(prod)
            return
        acc_ref = rest.pop(0)
        kk = pl.program_id(2)

        @pl.when(kk == 0)
        def _():
            acc_ref[...] = prod

        @pl.when(kk > 0)
        def _():
            acc_ref[...] += prod

        @pl.when(kk == nk - 1)
        def _():
            finish(acc_ref[...])

    out_specs = pl.BlockSpec((tm, tn), lambda i, j, kk: (i, j))
    out_shape = _sds((m, n), out_dtype)
    if colsum:
        assert mode == "tn" and m == tm
        out_specs = [out_specs, pl.BlockSpec((1, tn), lambda i, j, kk: (0, j))]
        out_shape = [out_shape, _sds((1, n), F32)]
    return _call(body, name=name, grid=(m // tm, n // tn, nk), in_specs=in_specs, out_specs=out_specs,
                 out_shape=out_shape, scratch=[pltpu.VMEM((tm, tn), F32)] if nk > 1 else [],
                 sem=("parallel", "parallel", "arbitrary"))(*operands)


def _colsum(x, *, name):
    rows, n = x.shape
    tr = _pick(rows, (1024, 512, 256, 128))
    tn = _pick(n, (512, 256, 128))

    def body(x_ref, o_ref):
        @pl.when(pl.program_id(1) == 0)
        def _():
            o_ref[...] = jnp.zeros_like(o_ref)

        o_ref[...] += jnp.sum(x_ref[...].astype(F32), axis=0, keepdims=True)

    return _call(body, name=name, grid=(n // tn, rows // tr),
                 in_specs=[pl.BlockSpec((tr, tn), lambda j, i: (i, j))],
                 out_specs=pl.BlockSpec((1, tn), lambda j, i: (0, j)),
                 out_shape=_sds((1, n), F32), sem=("parallel", "arbitrary"))(x)


def _ln_stats(x):
    mu = jnp.mean(x, axis=-1, keepdims=True)
    xc = x - mu
    var = jnp.mean(xc * xc, axis=-1, keepdims=True)
    rstd = lax.rsqrt(var + LN_EPS)
    return xc * rstd, rstd


def _ln_bwd(dn, n, rstd):
    return rstd * (dn - jnp.mean(dn, axis=-1, keepdims=True) - n * jnp.mean(dn * n, axis=-1, keepdims=True))


def _seq_tiles(t, bsz, cands=(1024, 512, 256, 128, 64, 32, 16, 8)):
    s = t // bsz
    ts = _pick(s, cands)
    return s, ts, s // ts


def _ln_mod_fwd(x, scale, shift, bsz, *, name):
    t, d = x.shape
    _, ts, ns = _seq_tiles(t, bsz)

    def body(x_ref, sc_ref, sh_ref, u_ref):
        one_scale = 1.0 + sc_ref[0]
        shift_v = sh_ref[0]

        def piece(rows):
            n, _ = _ln_stats(x_ref[rows, :])
            u_ref[rows, :] = (n * one_scale + shift_v).astype(u_ref.dtype)

        _chunked(ts, piece)

    row = pl.BlockSpec((ts, d), lambda b, i: (b * ns + i, 0))
    per = pl.BlockSpec((1, 1, d), lambda b, i: (b, 0, 0))
    return _call(body, name=name, grid=(bsz, ns), in_specs=[row, per, per], out_specs=row,
                 out_shape=_sds((t, d), BF16), sem=("parallel", "parallel"))(x, scale, shift)


def _ln_mod_bwd(du, x, scale, dr, alpha, bsz, *, name):
    t, d = x.shape
    _, ts, ns = _seq_tiles(t, bsz)

    def body(du_ref, x_ref, sc_ref, dr_ref, dx_ref, dsc_ref, dsh_ref):
        @pl.when(pl.program_id(1) == 0)
        def _():
            dsc_ref[...] = jnp.zeros_like(dsc_ref)
            dsh_ref[...] = jnp.zeros_like(dsh_ref)

        one_scale = 1.0 + sc_ref[0]

        def piece(rows):
            du_v = du_ref[rows, :]
            n, rstd = _ln_stats(x_ref[rows, :])
            dsc_ref[0] += jnp.sum(du_v * n, axis=0, keepdims=True)
            dsh_ref[0] += jnp.sum(du_v, axis=0, keepdims=True)
            dx_ref[rows, :] = alpha * dr_ref[rows, :] + _ln_bwd(du_v * one_scale, n, rstd)

        _chunked(ts, piece)

    row = pl.BlockSpec((ts, d), lambda b, i: (b * ns + i, 0))
    per = pl.BlockSpec((1, 1, d), lambda b, i: (b, 0, 0))
    return _call(body, name=name, grid=(bsz, ns), in_specs=[row, row, per, row],
                 out_specs=[row, per, per],
                 out_shape=[_sds((t, d), F32), _sds((bsz, 1, d), F32), _sds((bsz, 1, d), F32)],
                 sem=("parallel", "arbitrary"))(du, x, scale, dr)


def _ln_res_fwd(x, y, gate, g, b, alpha, bsz, *, name):
    t, d = x.shape
    _, ts, ns = _seq_tiles(t, bsz)

    def body(x_ref, y_ref, gt_ref, g_ref, b_ref, o_ref):
        one_gate = 1.0 + gt_ref[0]

        def piece(rows):
            n, _ = _ln_stats(alpha * x_ref[rows, :] + one_gate * y_ref[rows, :])
            o_ref[rows, :] = n * g_ref[...] + b_ref[...]

        _chunked(ts, piece)

    row = pl.BlockSpec((ts, d), lambda bb, i: (bb * ns + i, 0))
    per = pl.BlockSpec((1, 1, d), lambda bb, i: (bb, 0, 0))
    vec = pl.BlockSpec((1, d), lambda bb, i: (0, 0))
    return _call(body, name=name, grid=(bsz, ns), in_specs=[row, row, per, vec, vec], out_specs=row,
                 out_shape=_sds((t, d), F32), sem=("parallel", "parallel"))(x, y, gate, g, b)


def _ln_res_bwd(do, x, y, gate, g, alpha, bsz, *, name):
    t, d = x.shape
    _, ts, ns = _seq_tiles(t, bsz)

    def body(do_ref, x_ref, y_ref, gt_ref, g_ref, dr_ref, dy_ref, dgt_ref, dg_ref, db_ref, dys_ref):
        first_tile = pl.program_id(1) == 0

        @pl.when(first_tile)
        def _():
            dgt_ref[...] = jnp.zeros_like(dgt_ref)

        @pl.when(jnp.logical_and(first_tile, pl.program_id(0) == 0))
        def _():
            dg_ref[...] = jnp.zeros_like(dg_ref)
            db_ref[...] = jnp.zeros_like(db_ref)
            dys_ref[...] = jnp.zeros_like(dys_ref)

        one_gate = 1.0 + gt_ref[0]

        def piece(rows):
            do_v = do_ref[rows, :]
            y_v = y_ref[rows, :]
            n, rstd = _ln_stats(alpha * x_ref[rows, :] + one_gate * y_v)
            dg_ref[...] += jnp.sum(do_v * n, axis=0, keepdims=True)
            db_ref[...] += jnp.sum(do_v, axis=0, keepdims=True)
            dr = _ln_bwd(do_v * g_ref[...], n, rstd)
            dr_ref[rows, :] = dr
            dy = one_gate * dr
            dy_ref[rows, :] = dy.astype(dy_ref.dtype)
            dys_ref[...] += jnp.sum(dy, axis=0, keepdims=True)
            dgt_ref[0] += jnp.sum(dr * y_v, axis=0, keepdims=True)

        _chunked(ts, piece)

    row = pl.BlockSpec((ts, d), lambda bb, i: (bb * ns + i, 0))
    per = pl.BlockSpec((1, 1, d), lambda bb, i: (bb, 0, 0))
    vec = pl.BlockSpec((1, d), lambda bb, i: (0, 0))
    return _call(body, name=name, grid=(bsz, ns), in_specs=[row, row, row, per, vec],
                 out_specs=[row, row, per, vec, vec, vec],
                 out_shape=[_sds((t, d), F32), _sds((t, d), BF16), _sds((bsz, 1, d), F32),
                            _sds((1, d), F32), _sds((1, d), F32), _sds((1, d), F32)],
                 sem=("arbitrary", "arbitrary"))(do, x, y, gate, g)


def _loss_head(y, target, *, name):
    t, d = y.shape
    tr = _pick(t, (1024, 512, 256, 128, 64, 32, 16, 8))

    def body(y_ref, t_ref, dy_ref, s_ref):
        @pl.when(pl.program_id(0) == 0)
        def _():
            s_ref[...] = jnp.zeros_like(s_ref)

        def piece(rows):
            e = y_ref[rows, :] - t_ref[rows, :]
            dy_ref[rows, :] = e * (1.0 / d)
            s_ref[...] += jnp.sum(e * e, axis=0, keepdims=True)

        _chunked(tr, piece)

    row = pl.BlockSpec((tr, d), lambda i: (i, 0))
    return _call(body, name=name, grid=(t // tr,), in_specs=[row, row],
                 out_specs=[row, pl.BlockSpec((1, d), lambda i: (0, 0))],
                 out_shape=[_sds((t, d), F32), _sds((1, d), F32)], sem=("arbitrary",))(y, target)


def _sigmoid(v):
    return 1.0 / (1.0 + jnp.exp(-v))


def _silu_rows(c, *, name):
    rows, d = c.shape

    def body(c_ref, o_ref):
        v = c_ref[...]
        o_ref[...] = (v * _sigmoid(v)).astype(o_ref.dtype)

    full = pl.BlockSpec((rows, d), lambda i: (0, 0))
    return _call(body, name=name, grid=(1,), in_specs=[full], out_specs=full,
                 out_shape=_sds((rows, d), BF16), sem=("arbitrary",))(c)


def _gate_cols(d, ga_off):
    tc = _pick(math.gcd(d, ga_off), (512, 256, 128))
    return tc, ga_off // tc, (ga_off + d) // tc


def _gate_merge_fwd(z, ya, yb, ga_off, *, name):
    t, d = ya.shape
    tr = _pick(t, (1024, 512, 256, 128, 64, 32, 16, 8))
    tc, ga_blk, gb_blk = _gate_cols(d, ga_off)

    def body(ga_ref, gb_ref, ya_ref, yb_ref, o_ref):
        def piece(rows):
            o_ref[rows, :] = (_sigmoid(ga_ref[rows, :].astype(F32)) * ya_ref[rows, :]
                              + _sigmoid(gb_ref[rows, :].astype(F32)) * yb_ref[rows, :]).astype(o_ref.dtype)

        _chunked(tr, piece)

    blk = pl.BlockSpec((tr, tc), lambda i, j: (i, j))
    return _call(body, name=name, grid=(t // tr, d // tc),
                 in_specs=[pl.BlockSpec((tr, tc), lambda i, j: (i, ga_blk + j)),
                           pl.BlockSpec((tr, tc), lambda i, j: (i, gb_blk + j)), blk, blk],
                 out_specs=blk, out_shape=_sds((t, d), BF16), sem=("parallel", "parallel"))(z, z, ya, yb)


def _gate_merge_bwd(z, ya, yb, dm, ga_off, *, name):
    t, d = ya.shape
    tr = _pick(t, (1024, 512, 256, 128, 64, 32, 16, 8))
    tc, ga_blk, gb_blk = _gate_cols(d, ga_off)

    def body(ga_ref, gb_ref, ya_ref, yb_ref, dm_ref, dya_ref, dyb_ref, dga_ref, dgb_ref):
        def piece(rows):
            dm_v = dm_ref[rows, :]
            sa = _sigmoid(ga_ref[rows, :].astype(F32))
            sb = _sigmoid(gb_ref[rows, :].astype(F32))
            dya_ref[rows, :] = (dm_v * sa).astype(dya_ref.dtype)
            dyb_ref[rows, :] = (dm_v * sb).astype(dyb_ref.dtype)
            dga_ref[rows, :] = (dm_v * ya_ref[rows, :] * sa * (1.0 - sa)).astype(dga_ref.dtype)
            dgb_ref[rows, :] = (dm_v * yb_ref[rows, :] * sb * (1.0 - sb)).astype(dgb_ref.dtype)

        _chunked(tr, piece)

    blk = pl.BlockSpec((tr, tc), lambda i, j: (i, j))
    return _call(body, name=name, grid=(t // tr, d // tc),
                 in_specs=[pl.BlockSpec((tr, tc), lambda i, j: (i, ga_blk + j)),
                           pl.BlockSpec((tr, tc), lambda i, j: (i, gb_blk + j)), blk, blk, blk],
                 out_specs=[blk, blk, blk, blk], out_shape=[_sds((t, d), BF16)] * 4,
                 sem=("parallel", "parallel"))(z, z, ya, yb, dm)


CONV_ROWS = 64
CONV_PAD = 32


def _row_shifts(win):
    total = win.shape[0]
    return [win] + [pltpu.roll(win, total - b, axis=0) for b in range(1, SUBLANES)]


def _shifted_rows(copies, shift):
    start = SUBLANES * (shift // SUBLANES)
    return copies[shift % SUBLANES][start:start + CONV_ROWS]


def _fill_glu(z_ref, ext_ref, s, ch):
    ext_ref[pl.ds(0, CONV_PAD), :] = jnp.zeros((CONV_PAD, ch), F32)

    chunk = min(ROW_CHUNK, s)

    def piece(i, carry):
        start = pl.multiple_of(i * chunk, chunk)
        zz = z_ref[pl.ds(start, chunk), :].astype(F32)
        ext_ref[pl.ds(pl.multiple_of(CONV_PAD + start, CONV_PAD), chunk), :] = zz[:, :ch] * _sigmoid(zz[:, ch:])
        return carry

    lax.fori_loop(0, s // chunk, piece, 0)


def _conv_piece(ext_ref, w_ref, cb_ref, base, kw):
    copies = _row_shifts(ext_ref[pl.ds(base, CONV_ROWS + CONV_PAD), :])
    acc = cb_ref[...] + w_ref[pl.ds(0, 1), :] * _shifted_rows(copies, CONV_PAD - (kw - 1))
    for k in range(1, kw):
        acc = acc + w_ref[pl.ds(k, 1), :] * _shifted_rows(copies, CONV_PAD - (kw - 1) + k)
    return acc, copies


def _conv_branch_fwd(z, w, cb, lg, lb, bsz, ch, *, name):
    t = z.shape[0]
    s = t // bsz
    kw = w.shape[0]

    def body(z_ref, w_ref, cb_ref, lg_ref, lb_ref, o_ref, ext_ref):
        _fill_glu(z_ref, ext_ref, s, ch)

        def step(i, carry):
            base = pl.multiple_of(i * CONV_ROWS, CONV_ROWS)
            a1, _ = _conv_piece(ext_ref, w_ref, cb_ref, base, kw)
            n, _ = _ln_stats(a1)
            a2 = n * lg_ref[...] + lb_ref[...]
            o_ref[pl.ds(base, CONV_ROWS), :] = (a2 * _sigmoid(a2)).astype(o_ref.dtype)
            return carry

        lax.fori_loop(0, s // CONV_ROWS, step, 0)

    vec = pl.BlockSpec((1, ch), lambda b: (0, 0))
    return _call(body, name=name, grid=(bsz,),
                 in_specs=[pl.BlockSpec((s, 2 * ch), lambda b: (b, 0)), pl.BlockSpec((kw, ch), lambda b: (0, 0)),
                           vec, vec, vec],
                 out_specs=pl.BlockSpec((s, ch), lambda b: (b, 0)), out_shape=_sds((t, ch), BF16),
                 scratch=[pltpu.VMEM((CONV_PAD + s, ch), F32)], sem=("parallel",))(z, w, cb, lg, lb)


def _conv_branch_bwd(z, da3, w, cb, lg, lb, bsz, ch, *, name):
    t = z.shape[0]
    s = t // bsz
    kw = w.shape[0]
    n_rows = CONV_ROWS + CONV_PAD

    def body(z_ref, d_ref, w_ref, cb_ref, lg_ref, lb_ref, dz_ref, dw_ref, dcb_ref, dlg_ref, dlb_ref,
             ext_ref, da1_ref):
        @pl.when(pl.program_id(0) == 0)
        def _():
            for ref in (dw_ref, dcb_ref, dlg_ref, dlb_ref):
                ref[...] = jnp.zeros_like(ref)

        _fill_glu(z_ref, ext_ref, s, ch)
        da1_ref[pl.ds(s, CONV_PAD), :] = jnp.zeros((CONV_PAD, ch), F32)

        def grad_a1(i, carry):
            dlg, dlb = carry
            base = pl.multiple_of(i * CONV_ROWS, CONV_ROWS)
            a1, _ = _conv_piece(ext_ref, w_ref, cb_ref, base, kw)
            n, rstd = _ln_stats(a1)
            a2 = n * lg_ref[...] + lb_ref[...]
            sg = _sigmoid(a2)
            da2 = d_ref[pl.ds(base, CONV_ROWS), :] * (sg * (1.0 + a2 * (1.0 - sg)))
            da1_ref[pl.ds(base, CONV_ROWS), :] = _ln_bwd(da2 * lg_ref[...], n, rstd)
            return (dlg + jnp.sum(da2 * n, axis=0, keepdims=True), dlb + jnp.sum(da2, axis=0, keepdims=True))

        zero = jnp.zeros((1, ch), F32)
        dlg, dlb = lax.fori_loop(0, s // CONV_ROWS, grad_a1, (zero, zero))
        dlg_ref[...] += dlg
        dlb_ref[...] += dlb

        def grad_z(i, dcb):
            base = pl.multiple_of(i * CONV_ROWS, CONV_ROWS)
            ahead = _row_shifts(da1_ref[pl.ds(base, n_rows), :])
            dyc = ahead[0][:CONV_ROWS]
            da0 = w_ref[pl.ds(kw - 1, 1), :] * dyc
            for k in range(kw - 1):
                da0 = da0 + w_ref[pl.ds(k, 1), :] * _shifted_rows(ahead, kw - 1 - k)
            behind = _row_shifts(ext_ref[pl.ds(base, n_rows), :])
            for k in range(kw):
                dw_ref[pl.ds(k, 1), :] += jnp.sum(dyc * _shifted_rows(behind, CONV_PAD - (kw - 1) + k),
                                                  axis=0, keepdims=True)
            zz = z_ref[pl.ds(base, CONV_ROWS), :].astype(F32)
            sg = _sigmoid(zz[:, ch:])
            dz_ref[pl.ds(base, CONV_ROWS), :ch] = (da0 * sg).astype(dz_ref.dtype)
            dz_ref[pl.ds(base, CONV_ROWS), ch:] = (da0 * zz[:, :ch] * sg * (1.0 - sg)).astype(dz_ref.dtype)
            return dcb + jnp.sum(dyc, axis=0, keepdims=True)

        dcb_ref[...] += lax.fori_loop(0, s // CONV_ROWS, grad_z, zero)

    vec = pl.BlockSpec((1, ch), lambda b: (0, 0))
    taps = pl.BlockSpec((kw, ch), lambda b: (0, 0))
    return _call(body, name=name, grid=(bsz,),
                 in_specs=[pl.BlockSpec((s, 2 * ch), lambda b: (b, 0)), pl.BlockSpec((s, ch), lambda b: (b, 0)),
                           taps, vec, vec, vec],
                 out_specs=[pl.BlockSpec((s, 2 * ch), lambda b: (b, 0)), taps, vec, vec, vec],
                 out_shape=[_sds((t, 2 * ch), BF16), _sds((kw, ch), F32)] + [_sds((1, ch), F32)] * 3,
                 scratch=[pltpu.VMEM((CONV_PAD + s, ch), F32), pltpu.VMEM((s + CONV_PAD, ch), F32)],
                 sem=("arbitrary",))(z, da3, w, cb, lg, lb)


FFN_ROWS = 64


def _gelu_parts(v):
    cdf = 0.5 * (1.0 + lax.erf(v * (2.0 ** -0.5)))
    return cdf, v * cdf


def _ffn_conv_piece(ext_ref, wb_ref, base):
    win = ext_ref[pl.ds(base, FFN_ROWS + FFN_PAD), :]
    acc = wb_ref[pl.ds(3, 1), :] + wb_ref[pl.ds(2, 1), :] * win[FFN_PAD:]
    acc = acc + wb_ref[pl.ds(1, 1), :] * pltpu.roll(win, 1, axis=0)[FFN_PAD:]
    acc = acc + wb_ref[pl.ds(0, 1), :] * pltpu.roll(win, 2, axis=0)[FFN_PAD:]
    return acc


def _ffn_stage(hg_ref, hl_ref, wg_ref, wl_ref, bg_ref, bl_ref, ext_ref, wb_ref, s, tcf):
    ext_ref[pl.ds(0, FFN_PAD), :] = jnp.zeros((FFN_PAD, 2 * tcf), F32)
    ext_ref[pl.ds(FFN_PAD, s), :tcf] = hg_ref[...].astype(F32)
    ext_ref[pl.ds(FFN_PAD, s), tcf:] = hl_ref[...].astype(F32)
    wb_ref[pl.ds(0, 3), :tcf] = wg_ref[...]
    wb_ref[pl.ds(0, 3), tcf:] = wl_ref[...]
    wb_ref[pl.ds(3, 1), :tcf] = bg_ref[...]
    wb_ref[pl.ds(3, 1), tcf:] = bl_ref[...]


def _ffn_specs(s, tcf, n_f, batch_first):
    def spec(rows, shift):
        if batch_first:
            return pl.BlockSpec((rows, tcf), lambda bb, j: (bb if rows == s else 0, shift + j))
        return pl.BlockSpec((rows, tcf), lambda j, bb: (bb if rows == s else 0, shift + j))

    return [spec(s, 0), spec(s, n_f), spec(3, 0), spec(3, n_f), spec(1, 0), spec(1, n_f)]


def _ffn_act_fwd(hp, w, b, bsz, tcf, *, name):
    t, two_f = hp.shape
    s = t // bsz
    n_f = two_f // (2 * tcf)

    def body(hg_ref, hl_ref, wg_ref, wl_ref, bg_ref, bl_ref, f_ref, ext_ref, wb_ref):
        _ffn_stage(hg_ref, hl_ref, wg_ref, wl_ref, bg_ref, bl_ref, ext_ref, wb_ref, s, tcf)

        def step(i, carry):
            base = pl.multiple_of(i * FFN_ROWS, FFN_ROWS)
            hh = _ffn_conv_piece(ext_ref, wb_ref, base)
            _, gelu = _gelu_parts(hh[:, :tcf])
            f_ref[pl.ds(base, FFN_ROWS), :] = (gelu * hh[:, tcf:]).astype(f_ref.dtype)
            return carry

        lax.fori_loop(0, s // FFN_ROWS, step, 0)

    return _call(body, name=name, grid=(bsz, n_f), in_specs=_ffn_specs(s, tcf, n_f, True),
                 out_specs=pl.BlockSpec((s, tcf), lambda bb, j: (bb, j)),
                 out_shape=_sds((t, two_f // 2), BF16),
                 scratch=[pltpu.VMEM((FFN_PAD + s, 2 * tcf), F32), pltpu.VMEM((SUBLANES, 2 * tcf), F32)],
                 sem=("parallel", "parallel"))(hp, hp, w, w, b, b)


def _ffn_act_bwd(hp, df, w, b, bsz, tcf, *, name):
    t, two_f = hp.shape
    s = t // bsz
    f_dim = two_f // 2
    n_f = f_dim // tcf
    gw = 2 * tcf
    n_rows = FFN_ROWS + FFN_PAD

    def body(hg_ref, hl_ref, wg_ref, wl_ref, bg_ref, bl_ref, df_ref,
             dhg_ref, dhl_ref, dwg_ref, dwl_ref, dbg_ref, dbl_ref, ext_ref, wb_ref, dh_ref):
        @pl.when(pl.program_id(1) == 0)
        def _():
            for ref in (dwg_ref, dwl_ref, dbg_ref, dbl_ref):
                ref[...] = jnp.zeros_like(ref)

        _ffn_stage(hg_ref, hl_ref, wg_ref, wl_ref, bg_ref, bl_ref, ext_ref, wb_ref, s, tcf)
        dh_ref[pl.ds(s, FFN_PAD), :] = jnp.zeros((FFN_PAD, gw), F32)

        def grad_h(i, carry):
            base = pl.multiple_of(i * FFN_ROWS, FFN_ROWS)
            hh = _ffn_conv_piece(ext_ref, wb_ref, base)
            hg = hh[:, :tcf]
            d = df_ref[pl.ds(base, FFN_ROWS), :]
            cdf, gelu = _gelu_parts(hg)
            pdf = jnp.exp(-0.5 * hg * hg) * (1.0 / math.sqrt(2.0 * math.pi))
            dh_ref[pl.ds(base, FFN_ROWS), :tcf] = d * hh[:, tcf:] * (cdf + hg * pdf)
            dh_ref[pl.ds(base, FFN_ROWS), tcf:] = d * gelu
            return carry

        lax.fori_loop(0, s // FFN_ROWS, grad_h, 0)

        def grad_x(i, carry):
            dw0, dw1, dw2, dbs = carry
            base = pl.multiple_of(i * FFN_ROWS, FFN_ROWS)
            nxt = dh_ref[pl.ds(base, n_rows), :]
            dyc = nxt[:FFN_ROWS]
            dx = wb_ref[pl.ds(2, 1), :] * dyc
            dx = dx + wb_ref[pl.ds(1, 1), :] * pltpu.roll(nxt, n_rows - 1, axis=0)[:FFN_ROWS]
            dx = dx + wb_ref[pl.ds(0, 1), :] * pltpu.roll(nxt, n_rows - 2, axis=0)[:FFN_ROWS]
            dhg_ref[pl.ds(base, FFN_ROWS), :] = dx[:, :tcf].astype(dhg_ref.dtype)
            dhl_ref[pl.ds(base, FFN_ROWS), :] = dx[:, tcf:].astype(dhl_ref.dtype)
            win = ext_ref[pl.ds(base, n_rows), :]
            dw2 = dw2 + jnp.sum(dyc * win[FFN_PAD:], axis=0, keepdims=True)
            dw1 = dw1 + jnp.sum(dyc * pltpu.roll(win, 1, axis=0)[FFN_PAD:], axis=0, keepdims=True)
            dw0 = dw0 + jnp.sum(dyc * pltpu.roll(win, 2, axis=0)[FFN_PAD:], axis=0, keepdims=True)
            return dw0, dw1, dw2, dbs + jnp.sum(dyc, axis=0, keepdims=True)

        zero = jnp.zeros((1, gw), F32)
        sums = lax.fori_loop(0, s // FFN_ROWS, grad_x, (zero, zero, zero, zero))
        for k in range(3):
            dwg_ref[pl.ds(k, 1), :] += sums[k][:, :tcf]
            dwl_ref[pl.ds(k, 1), :] += sums[k][:, tcf:]
        dbg_ref[...] += sums[3][:, :tcf]
        dbl_ref[...] += sums[3][:, tcf:]

    half = pl.BlockSpec((s, tcf), lambda j, bb: (bb, j))
    taps = pl.BlockSpec((3, tcf), lambda j, bb: (0, j))
    bias = pl.BlockSpec((1, tcf), lambda j, bb: (0, j))
    return _call(body, name=name, grid=(n_f, bsz), in_specs=_ffn_specs(s, tcf, n_f, False) + [half],
                 out_specs=[half, half, taps, taps, bias, bias],
                 out_shape=[_sds((t, f_dim), BF16)] * 2 + [_sds((3, f_dim), F32)] * 2 + [_sds((1, f_dim), F32)] * 2,
                 scratch=[pltpu.VMEM((FFN_PAD + s, gw), F32), pltpu.VMEM((SUBLANES, gw), F32),
                          pltpu.VMEM((s + FFN_PAD, gw), F32)],
                 sem=("parallel", "arbitrary"))(hp, hp, w, w, b, b, df)


def _split3(v):
    hi = v.astype(BF16)
    r = v - hi.astype(F32)
    mid = r.astype(BF16)
    lo = (r - mid.astype(F32)).astype(BF16)
    return hi, mid, lo


def _tri_dot(tri, v):
    out = None
    for part in _split3(v):
        term = jnp.dot(tri, part, preferred_element_type=F32)
        out = term if out is None else out + term
    return out


def _fgate_fwd(zf, bsz, heads, *, name):
    t, lanes = zf.shape
    s, blk, nb = _seq_tiles(t, bsz, (ATTN_BLOCK, 128))

    def body(z_ref, cumt_ref, cumb_ref, carry_ref):
        @pl.when(pl.program_id(1) == 0)
        def _():
            carry_ref[...] = jnp.zeros_like(carry_ref)

        z = z_ref[...]
        lf = jnp.minimum(z, 0.0) - jnp.log1p(jnp.exp(-jnp.abs(z)))
        r = lax.broadcasted_iota(jnp.int32, (blk, blk), 0)
        c = lax.broadcasted_iota(jnp.int32, (blk, blk), 1)
        tri = (r >= c).astype(BF16)
        cum = _tri_dot(tri, lf) + carry_ref[...]
        carry_ref[...] = cum[blk - 1:blk, :]
        cumt_ref[0] = jnp.transpose(cum)[:heads, :]
        for h in range(heads):
            cumb_ref[0, h] = jnp.broadcast_to(cum[:, h:h + 1], (blk, lanes))

    return _call(body, name=name, grid=(bsz, nb),
                 in_specs=[pl.BlockSpec((blk, lanes), lambda b, i: (b * nb + i, 0))],
                 out_specs=[pl.BlockSpec((1, heads, blk), lambda b, i: (b, 0, i)),
                            pl.BlockSpec((1, heads, blk, lanes), lambda b, i: (b, 0, i, 0))],
                 out_shape=[_sds((bsz, heads, s), F32), _sds((bsz, heads, s, lanes), F32)],
                 scratch=[pltpu.VMEM((1, lanes), F32)], sem=("parallel", "arbitrary"))(zf)


def _fgate_bwd(dcum, zf, bsz, *, name):
    t, lanes = zf.shape
    pairs = dcum.shape[1]
    s, blk, nb = _seq_tiles(t, bsz, (ATTN_BLOCK, 128))

    def body(d_ref, z_ref, o_ref, carry_ref):
        @pl.when(pl.program_id(1) == 0)
        def _():
            carry_ref[...] = jnp.zeros_like(carry_ref)

        dcol = d_ref[0, 0]
        for p in range(1, pairs):
            dcol = dcol + d_ref[0, p]
        r = lax.broadcasted_iota(jnp.int32, (blk, blk), 0)
        c = lax.broadcasted_iota(jnp.int32, (blk, blk), 1)
        tri = (c >= r).astype(BF16)
        suf = _tri_dot(tri, dcol) + carry_ref[...]
        carry_ref[...] = suf[0:1, :]
        o_ref[...] = suf * _sigmoid(-z_ref[...])

    return _call(body, name=name, grid=(bsz, nb),
                 in_specs=[pl.BlockSpec((1, pairs, blk, lanes), lambda b, i: (b, 0, nb - 1 - i, 0)),
                           pl.BlockSpec((blk, lanes), lambda b, i: (b * nb + nb - 1 - i, 0))],
                 out_specs=pl.BlockSpec((blk, lanes), lambda b, i: (b * nb + nb - 1 - i, 0)),
                 out_shape=_sds((t, lanes), F32), scratch=[pltpu.VMEM((1, lanes), F32)],
                 sem=("parallel", "arbitrary"))(dcum, zf)


def _to_features_major(z, col_off, width, n, *, name):
    t = z.shape[0]
    tr = _pick(t, (512, 256, 128))
    first = col_off // width

    def body(*refs):
        o_ref = refs[n]
        for g in range(n):
            o_ref[pl.ds(g * width, width), :] = jnp.transpose(refs[g][...].astype(F32)).astype(o_ref.dtype)

    return _call(body, name=name, grid=(t // tr,),
                 in_specs=[pl.BlockSpec((tr, width), lambda i, g=g: (i, first + g)) for g in range(n)],
                 out_specs=pl.BlockSpec((n * width, tr), lambda i: (0, i)),
                 out_shape=_sds((n * width, t), BF16), sem=("parallel",))(*([z] * n))


def _to_rows_major(xt, *, name):
    w, t = xt.shape
    tr = _pick(t, (512, 256, 128))

    def body(x_ref, o_ref):
        o_ref[...] = jnp.transpose(x_ref[...]).astype(o_ref.dtype)

    return _call(body, name=name, grid=(t // tr,),
                 in_specs=[pl.BlockSpec((w, tr), lambda i: (0, i))],
                 out_specs=pl.BlockSpec((tr, w), lambda i: (i, 0)),
                 out_shape=_sds((t, w), BF16), sem=("parallel",))(xt)


def _head_masks(shape, axis):
    feat = lax.broadcasted_iota(jnp.int32, shape, axis)
    return feat < HEAD_DIM, feat >= HEAD_DIM


def _attn_fwd(z, qkvt, cumt, cumb, bsz, heads, q_off, *, name):
    t = z.shape[0]
    width = heads * HEAD_DIM
    pairs = heads // 2
    s = t // bsz
    blk = ATTN_BLOCK
    nq = s // blk
    k_col = (q_off + width) // LANES
    v_row = 2 * width // LANES
    reps = blk // LANES

    def body(k_ref, qt_ref, vt_ref, cqt_ref, ckb_ref, ot_ref, lse_ref):
        p_id = pl.program_id(1)
        i = pl.program_id(2)
        qt = qt_ref[...]
        masks = _head_masks((LANES, blk), 0)
        qtm = [jnp.where(mk, qt, jnp.zeros_like(qt)) for mk in masks]
        cq = [cqt_ref[0, pl.ds(2 * p_id + hh, 1), :] for hh in range(2)]
        kidx = lax.broadcasted_iota(jnp.int32, (blk, blk), 0)
        qidx = lax.broadcasted_iota(jnp.int32, (blk, blk), 1)

        def block(j, carry, masked):
            off = pl.multiple_of(j * blk, blk)
            kp = k_ref[pl.ds(off, blk), :].astype(BF16)
            vtp = vt_ref[:, pl.ds(off, blk)]
            out = []
            for hh in range(2):
                m, l, acc = carry[hh]
                sc = jnp.dot(kp, qtm[hh], preferred_element_type=F32) * ATTN_SCALE
                ck = ckb_ref[0, hh, pl.ds(off, blk), :]
                sc = (sc + cq[hh]) - jnp.concatenate([ck] * reps, axis=1)
                if masked:
                    sc = jnp.where(qidx >= kidx, sc, NEG)
                m_new = jnp.maximum(m, jnp.max(sc, axis=0, keepdims=True))
                pr = jnp.exp(sc - m_new)
                a = jnp.exp(m - m_new)
                l = a * l + jnp.sum(pr, axis=0, keepdims=True)
                p_hi = pr.astype(BF16)
                p_lo = (pr - p_hi.astype(F32)).astype(BF16)
                pv = (jnp.dot(vtp, p_hi, preferred_element_type=F32)
                      + jnp.dot(vtp, p_lo, preferred_element_type=F32))
                acc = a * acc + pv[hh * HEAD_DIM:(hh + 1) * HEAD_DIM]
                out.append((m_new, l, acc))
            return tuple(out)

        init = tuple((jnp.full((1, blk), NEG, F32), jnp.zeros((1, blk), F32), jnp.zeros((HEAD_DIM, blk), F32))
                     for _ in range(2))
        carry = lax.fori_loop(0, i, lambda j, cr: block(j, cr, False), init)
        carry = block(i, carry, True)
        lse_ref[...] = jnp.zeros_like(lse_ref)
        for hh in range(2):
            m, l, acc = carry[hh]
            ot_ref[pl.ds(hh * HEAD_DIM, HEAD_DIM), :] = acc / l
            lse_ref[0, 0, pl.ds(hh, 1), :] = m + jnp.log(l)

    return _call(body, name=name, grid=(bsz, pairs, nq),
                 in_specs=[pl.BlockSpec((s, LANES), lambda b, p, i: (b, k_col + p)),
                           pl.BlockSpec((LANES, blk), lambda b, p, i: (p, b * nq + i)),
                           pl.BlockSpec((LANES, s), lambda b, p, i: (v_row + p, b)),
                           pl.BlockSpec((1, heads, blk), lambda b, p, i: (b, 0, i)),
                           pl.BlockSpec((1, 2, s, LANES), lambda b, p, i: (b, p, 0, 0))],
                 out_specs=[pl.BlockSpec((LANES, blk), lambda b, p, i: (p, b * nq + i)),
                            pl.BlockSpec((1, 1, SUBLANES, blk), lambda b, p, i: (b, p, 0, i))],
                 out_shape=[_sds((width, t), F32), _sds((bsz, pairs, SUBLANES, s), F32)],
                 sem=("parallel", "parallel", "parallel"))(z, qkvt, qkvt, cumt, cumb)


def _attn_bwd(z, qkvt, cumt, cumb, ot, do, dot, lse, bsz, heads, q_off, *, name):
    t = z.shape[0]
    width = heads * HEAD_DIM
    pairs = heads // 2
    s = t // bsz
    blk = ATTN_BLOCK
    nkv = s // blk
    q_col = q_off // LANES
    k_col = (q_off + width) // LANES
    v_col = (q_off + 2 * width) // LANES
    k_row = width // LANES
    reps = blk // LANES

    def body(k_ref, v_ref, kt_ref, q_ref, qt_ref, do_ref, dot_ref, ot_ref, lse_ref, ckb_ref, cqt_ref,
             dk_ref, dv_ref, dqt_ref, dcum_ref, dqt_acc, ds_acc):
        p_id = pl.program_id(1)
        j = pl.program_id(2)

        @pl.when(j == 0)
        def _():
            dqt_acc[...] = jnp.zeros_like(dqt_acc)

        kp = k_ref[...].astype(BF16)
        vp = v_ref[...].astype(BF16)
        kt = kt_ref[...]
        feat_masks = _head_masks((LANES, blk), 0)
        lane_masks = _head_masks((blk, LANES), 1)
        ktm = [jnp.where(mk, kt, jnp.zeros_like(kt)) for mk in feat_masks]
        ck = [jnp.concatenate([ckb_ref[0, hh]] * reps, axis=1) for hh in range(2)]
        kidx = lax.broadcasted_iota(jnp.int32, (blk, blk), 0)
        qidx = lax.broadcasted_iota(jnp.int32, (blk, blk), 1)
        ds_acc[...] = jnp.zeros_like(ds_acc)

        def block(i, carry, masked):
            dk, dv = carry
            off = pl.multiple_of(i * blk, blk)
            qt = qt_ref[:, pl.ds(off, blk)]
            dt = dot_ref[:, pl.ds(off, blk)]
            o_t = ot_ref[:, pl.ds(off, blk)]
            q_rows = q_ref[pl.ds(off, blk), :].astype(BF16)
            do_rows = do_ref[pl.ds(off, blk), :]
            for hh in range(2):
                qtm = jnp.where(feat_masks[hh], qt, jnp.zeros_like(qt))
                dtm = jnp.where(feat_masks[hh], dt, jnp.zeros_like(dt))
                sc = jnp.dot(kp, qtm, preferred_element_type=F32) * ATTN_SCALE
                sc = (sc + cqt_ref[0, pl.ds(2 * p_id + hh, 1), pl.ds(off, blk)]) - ck[hh]
                pr = jnp.exp(sc - lse_ref[0, 0, pl.ds(hh, 1), pl.ds(off, blk)])
                if masked:
                    pr = jnp.where(qidx >= kidx, pr, 0.0)
                dp = jnp.dot(vp, dtm, preferred_element_type=F32)
                delta = jnp.sum(dtm.astype(F32) * o_t, axis=0, keepdims=True)
                ds = pr * (dp - delta)
                ds_acc[hh] += ds
                dsb = ds.astype(BF16)
                qm = jnp.where(lane_masks[hh], q_rows, jnp.zeros_like(q_rows))
                dom = jnp.where(lane_masks[hh], do_rows, jnp.zeros_like(do_rows))
                dv = dv + jnp.dot(pr.astype(BF16), dom, preferred_element_type=F32)
                dk = dk + jnp.dot(dsb, qm, preferred_element_type=F32) * ATTN_SCALE
                dqt_acc[:, pl.ds(off, blk)] += jnp.dot(ktm[hh], dsb, preferred_element_type=F32) * ATTN_SCALE
            return dk, dv

        zero = jnp.zeros((blk, LANES), F32)
        carry = block(j, (zero, zero), True)
        dk, dv = lax.fori_loop(j + 1, nkv, lambda i, cr: block(i, cr, False), carry)
        dk_ref[...] = dk.astype(dk_ref.dtype)
        dv_ref[...] = dv.astype(dv_ref.dtype)
        lane = lax.broadcasted_iota(jnp.int32, (blk, LANES), 1)
        dcum = jnp.zeros((blk, LANES), F32)
        for hh in range(2):
            col = jnp.sum(ds_acc[hh], axis=1, keepdims=True)
            dcum = jnp.where(lane == 2 * p_id + hh, -col, dcum)
        dcum_ref[0, 0] = dcum

        @pl.when(j == nkv - 1)
        def _():
            dqt_ref[...] = dqt_acc[...]

    key_rows = lambda col: pl.BlockSpec((blk, LANES), lambda b, p, j: (b * nkv + j, col + p))
    seq_t = lambda row: pl.BlockSpec((LANES, s), lambda b, p, j: (row + p, b))
    return _call(body, name=name, grid=(bsz, pairs, nkv),
                 in_specs=[key_rows(k_col), key_rows(v_col),
                           pl.BlockSpec((LANES, blk), lambda b, p, j: (k_row + p, b * nkv + j)),
                           pl.BlockSpec((s, LANES), lambda b, p, j: (b, q_col + p)), seq_t(0),
                           pl.BlockSpec((s, LANES), lambda b, p, j: (b, p)), seq_t(0), seq_t(0),
                           pl.BlockSpec((1, 1, SUBLANES, s), lambda b, p, j: (b, p, 0, 0)),
                           pl.BlockSpec((1, 2, blk, LANES), lambda b, p, j: (b, p, j, 0)),
                           pl.BlockSpec((1, heads, s), lambda b, p, j: (b, 0, 0))],
                 out_specs=[key_rows(0), key_rows(0), seq_t(0),
                            pl.BlockSpec((1, 1, blk, LANES), lambda b, p, j: (b, p, j, 0))],
                 out_shape=[_sds((t, width), BF16), _sds((t, width), BF16), _sds((width, t), F32),
                            _sds((bsz, pairs, s, LANES), F32)],
                 scratch=[pltpu.VMEM((LANES, s), F32), pltpu.VMEM((2, blk, blk), F32)],
                 sem=("parallel", "parallel", "arbitrary"))(z, z, qkvt, z, qkvt, do, dot, ot, lse, cumb, cumt)


def _adamw(w, g, m, v, *, name):
    bc1 = 1.0 - ADAM_B1 ** ADAM_STEP
    bc2 = 1.0 - ADAM_B2 ** ADAM_STEP

    def body(w_ref, g_ref, m_ref, v_ref, d_ref, nm_ref, nv_ref):
        g_v = g_ref[...]
        nm = ADAM_B1 * m_ref[...] + (1.0 - ADAM_B1) * g_v
        nv = ADAM_B2 * v_ref[...] + (1.0 - ADAM_B2) * (g_v * g_v)
        nm_ref[...] = nm
        nv_ref[...] = nv
        d_ref[...] = -ADAM_LR * ((nm / bc1) / (jnp.sqrt(nv / bc2) + ADAM_EPS) + ADAM_WD * w_ref[...])

    if w.ndim == 2:
        grid = (1,)
        blk = pl.BlockSpec(w.shape, lambda i: (0, 0))
    else:
        layers, rows, cols = w.shape
        tr = rows if rows <= 256 else _pick(rows, (256, 128, 64, 32, 16, 8))
        grid = (layers, rows // tr)
        blk = pl.BlockSpec((1, tr, cols), lambda layer, i: (layer, i, 0))
    return tuple(_call(body, name=name, grid=grid, in_specs=[blk] * 4, out_specs=[blk] * 3,
                       out_shape=[_sds(w.shape, F32)] * 3, sem=("parallel",) * len(grid))(w, g, m, v))


_ANY = pl.BlockSpec(memory_space=pl.ANY)


def _comm_call(body, *, name, n_in, out_shape, n_sems):
    scratch = [pltpu.SemaphoreType.DMA((n_sems,)), pltpu.SemaphoreType.DMA((n_sems,)),
               pltpu.SemaphoreType.DMA((len(out_shape),))]
    return pl.pallas_call(body, name=name, in_specs=[_ANY] * n_in, out_specs=[_ANY] * len(out_shape),
                          out_shape=out_shape, scratch_shapes=scratch)


def _place():
    x, y, c = lax.axis_index("x"), lax.axis_index("y"), lax.axis_index("c")
    return x, y, c, [(1 - x, y), (x, 1 - y), (1 - x, 1 - y)]


def _remote(src, dst, send_sems, recv_sems, sem, to):
    return pltpu.make_async_remote_copy(src_ref=src, dst_ref=dst, send_sem=send_sems.at[sem],
                                        recv_sem=recv_sems.at[sem], device_id=to, device_id_type=MESH)


def _all_gather8(v, *, name):
    def body(v_ref, out_ref, send_sems, recv_sems, local_sems):
        x, y, c, _ = _place()
        me = 4 * x + 2 * y + c
        mine = pltpu.make_async_copy(v_ref, out_ref.at[me], local_sems.at[0])
        mine.start()
        peers = []
        for k in range(1, N_DEVICES):
            px = 1 - x if k & 4 else x
            py = 1 - y if k & 2 else y
            pc = 1 - c if k & 1 else c
            peers.append((px, py, pc))
        sends = [_remote(v_ref, out_ref.at[me], send_sems, recv_sems, k, peer) for k, peer in enumerate(peers)]
        for cp in sends:
            cp.start()
        for k, (px, py, pc) in enumerate(peers):
            _remote(v_ref, out_ref.at[4 * px + 2 * py + pc], send_sems, recv_sems, k, (px, py, pc)).wait_recv()
        for cp in sends:
            cp.wait_send()
        mine.wait()

    out = _comm_call(body, name=name, n_in=1, out_shape=[_sds((N_DEVICES,) + v.shape, v.dtype)],
                     n_sems=N_DEVICES - 1)(v)
    return out[0]


def _window(ref, mode, layer, chip, rows, cols):
    if mode == "slab":
        return ref.at[layer, chip]
    if mode == "cols":
        return ref.at[layer, :, pl.ds(pl.multiple_of(chip * cols, LANES), cols)]
    return ref.at[layer, pl.ds(pl.multiple_of(chip * rows, SUBLANES), rows), :]


def _whole_shape(mode, shard_shape):
    layers, rows, cols = shard_shape
    if mode == "slab":
        return (layers, N_CHIPS, rows, cols)
    if mode == "cols":
        assert cols % LANES == 0
        return (layers, rows, N_CHIPS * cols)
    assert rows % 16 == 0
    return (layers, N_CHIPS * rows, cols)


def _gather_weights(shards, modes, *, name):
    n = len(shards)
    meta = [(mode,) + tuple(a.shape[1:]) for a, mode in zip(shards, modes)]
    for a in shards:
        assert a.shape[0] == 2
    per = 7

    def body(*refs):
        ins, outs = refs[:n], refs[n:2 * n]
        send_sems, recv_sems, _ = refs[2 * n:]
        x, y, c, chips = _place()
        me = 2 * x + y
        sibling = (x, y, 1 - c)
        own, first, passed = [], [], []
        for i, (mode, rows, cols) in enumerate(meta):
            for r, (cx, cy) in enumerate(chips):
                cp = _remote(ins[i].at[c], _window(outs[i], mode, c, me, rows, cols), send_sems, recv_sems,
                             per * i + r, (cx, cy, c))
                cp.start()
                first.append(cp)
            cp = _remote(ins[i], _window(outs[i], mode, slice(None), me, rows, cols), send_sems, recv_sems,
                         per * i + 6, sibling)
            cp.start()
            own.append(cp)
        for i, (mode, rows, cols) in enumerate(meta):
            for r, (cx, cy) in enumerate(chips):
                win = _window(outs[i], mode, c, 2 * cx + cy, rows, cols)
                _remote(win, win, send_sems, recv_sems, per * i + r, (cx, cy, c)).wait_recv()
                cp = _remote(win, win, send_sems, recv_sems, per * i + 3 + r, sibling)
                cp.start()
                passed.append(cp)
        for i, (mode, rows, cols) in enumerate(meta):
            own[i].wait_recv()
            for r, (cx, cy) in enumerate(chips):
                win = _window(outs[i], mode, 1 - c, 2 * cx + cy, rows, cols)
                _remote(win, win, send_sems, recv_sems, per * i + 3 + r, sibling).wait_recv()
        for cp in first + passed + own:
            cp.wait_send()

    out_shape = [_sds(_whole_shape(mode, a.shape), a.dtype) for a, mode in zip(shards, modes)]
    return _comm_call(body, name=name, n_in=n, out_shape=out_shape, n_sems=per * n)(*shards)


def _rs_swap(grads, *, name):
    n = len(grads)

    def body(*refs):
        ins, outs = refs[:n], refs[n:2 * n]
        send_sems, recv_sems, _ = refs[2 * n:]
        x, y, c, _ = _place()
        copies = [_remote(ins[i].at[1 - c], outs[i], send_sems, recv_sems, i, (x, y, 1 - c)) for i in range(n)]
        for cp in copies:
            cp.start()
        for cp in copies:
            cp.wait()

    return _comm_call(body, name=name, n_in=n, out_shape=[_sds(g.shape[1:], g.dtype) for g in grads], n_sems=n)(*grads)


def _part(ref, mode, chip, rows, cols):
    if mode == "slab":
        return ref.at[chip]
    if mode == "cols":
        return ref.at[:, pl.ds(pl.multiple_of(chip * cols, LANES), cols)]
    return ref.at[pl.ds(pl.multiple_of(chip * rows, SUBLANES), rows), :]


def _rs_scatter(parts, modes, shard_shapes, *, name):
    n = len(parts)
    meta = [(mode,) + tuple(shp[1:]) for mode, shp in zip(modes, shard_shapes)]

    def body(*refs):
        ins, outs = refs[:n], refs[n:2 * n]
        send_sems, recv_sems, local_sems = refs[2 * n:]
        x, y, c, chips = _place()
        me = 2 * x + y
        local, sends = [], []
        for i, (mode, rows, cols) in enumerate(meta):
            cp = pltpu.make_async_copy(_part(ins[i], mode, me, rows, cols), outs[i].at[me], local_sems.at[i])
            cp.start()
            local.append(cp)
            for r, (cx, cy) in enumerate(chips):
                cp = _remote(_part(ins[i], mode, 2 * cx + cy, rows, cols), outs[i].at[me], send_sems, recv_sems,
                             3 * i + r, (cx, cy, c))
                cp.start()
                sends.append(cp)
        for i, (mode, rows, cols) in enumerate(meta):
            for r, (cx, cy) in enumerate(chips):
                k = 2 * cx + cy
                _remote(_part(ins[i], mode, k, rows, cols), outs[i].at[k], send_sems, recv_sems, 3 * i + r,
                        (cx, cy, c)).wait_recv()
        for cp in sends:
            cp.wait_send()
        for cp in local:
            cp.wait()

    out_shape = [_sds((N_CHIPS,) + tuple(shp[1:]), p.dtype) for p, shp in zip(parts, shard_shapes)]
    return _comm_call(body, name=name, n_in=n, out_shape=out_shape, n_sems=3 * n)(*parts)


def _rs_exchange(sums, *, name):
    n = len(sums)

    def body(*refs):
        ins, outs = refs[:n], refs[n:2 * n]
        send_sems, recv_sems, _ = refs[2 * n:]
        x, y, c, _ = _place()
        copies = [_remote(ins[i], outs[i], send_sems, recv_sems, i, (x, y, 1 - c)) for i in range(n)]
        for cp in copies:
            cp.start()
        for cp in copies:
            cp.wait()

    return _comm_call(body, name=name, n_in=n, out_shape=[_sds(s.shape, s.dtype) for s in sums], n_sems=n)(*sums)


def _row_tile(rows, cols, itemsize):
    target = max(SUBLANES, (2 << 20) // (cols * itemsize))
    cands = [c for c in (2048, 1024, 512, 256, 128, 64, 32, 16) if c <= target]
    tr = _pick(rows, cands)
    return tr


def _add_layer(g, other, core, *, name):
    _, rows, cols = g.shape
    tr = _row_tile(rows, cols, 4)

    def body(core_ref, g_ref, o_ref, out_ref):
        out_ref[...] = (g_ref[0] + o_ref[...]).astype(out_ref.dtype)

    grid_spec = pltpu.PrefetchScalarGridSpec(
        num_scalar_prefetch=1, grid=(rows // tr,),
        in_specs=[pl.BlockSpec((1, tr, cols), lambda i, core_ref: (core_ref[0], i, 0)),
                  pl.BlockSpec((tr, cols), lambda i, core_ref: (i, 0))],
        out_specs=pl.BlockSpec((tr, cols), lambda i, core_ref: (i, 0)))
    return pl.pallas_call(body, name=name, grid_spec=grid_spec, out_shape=_sds((rows, cols), BF16),
                          compiler_params=pltpu.CompilerParams(dimension_semantics=("parallel",),
                                                               vmem_limit_bytes=VMEM_LIMIT))(core, g, other)


def _sum_slots(parts, *, name):
    n, rows, cols = parts.shape
    tr = _row_tile(rows, cols, 4)

    def body(p_ref, o_ref):
        acc = p_ref[0].astype(F32) + p_ref[1].astype(F32)
        for k in range(2, n):
            acc = acc + p_ref[k].astype(F32)
        o_ref[...] = acc

    return _call(body, name=name, grid=(rows // tr,),
                 in_specs=[pl.BlockSpec((n, tr, cols), lambda i: (0, i, 0))],
                 out_specs=pl.BlockSpec((tr, cols), lambda i: (i, 0)),
                 out_shape=_sds((rows, cols), F32), sem=("parallel",))(parts)


def _reduce_scatter(grads, modes, shard_shapes):
    core = lax.axis_index("c").astype(jnp.int32).reshape(1)
    flat = [g.reshape(g.shape[0], -1, g.shape[-1]) for g in grads]
    from_sibling = _rs_swap(flat, name="rs_swap")
    parts = []
    for i, (g, o) in enumerate(zip(flat, from_sibling)):
        p = _add_layer(g, o, core, name=f"rs_add_{i}")
        parts.append(p.reshape(grads[i].shape[1:]))
    from_chips = _rs_scatter(parts, modes, shard_shapes, name="rs_scatter")
    sums = [_sum_slots(r, name=f"rs_sum_{i}") for i, r in enumerate(from_chips)]
    others = _rs_exchange(sums, name="rs_exchange")
    mine_first = lax.axis_index("c") == 0
    return [jnp.where(mine_first, jnp.stack([mine, other]), jnp.stack([other, mine]))
            for mine, other in zip(sums, others)]


def _layer_weights(full, rep, layer, dims):
    f_off, n_heads = dims["f_off"], dims["heads"]
    d_ff = full["w_ffn_up"].shape[-1] // 2
    w_in = full["w_in"][layer]
    b_in = rep["b_in"][layer]
    pad = LANES - n_heads
    return {
        "w_main": jnp.concatenate([w_in[:, :f_off], w_in[:, f_off + n_heads:]], axis=1),
        "b_main": jnp.concatenate([b_in[:f_off], b_in[f_off + n_heads:]])[None],
        "w_f": jnp.pad(w_in[:, f_off:f_off + n_heads], ((0, 0), (0, pad))),
        "b_f": jnp.pad(b_in[f_off:f_off + n_heads], (0, pad))[None],
        "conv_a_w": full["conv_a_w"][layer],
        "conv_a_b": rep["conv_a_b"][layer][None],
        "ln_conv_g": rep["ln_conv_g"][layer][None],
        "ln_conv_b": rep["ln_conv_b"][layer][None],
        "w_conv_proj": full["w_conv_proj"][layer],
        "w_attn_proj": full["w_attn_proj"][layer],
        "w_mix_out": full["w_mix_out"][layer],
        "b_mix_out": rep["b_mix_out"][layer][None],
        "ln1_g": rep["ln1_g"][layer][None],
        "ln1_b": rep["ln1_b"][layer][None],
        "w_ffn_up": full["w_ffn_up"][layer],
        "w_ffn_up_gate": full["w_ffn_up"][layer][:, :d_ff],
        "w_ffn_up_lin": full["w_ffn_up"][layer][:, d_ff:],
        "ffn_conv_w": full["ffn_conv_w"][layer],
        "ffn_conv_b": rep["ffn_conv_b"][layer][None],
        "w_ffn_down": full["w_ffn_down"][layer],
        "ln2_g": rep["ln2_g"][layer][None],
        "ln2_b": rep["ln2_b"][layer][None],
    }


def _layer_fwd(x, mod, p, dims, tag):
    bsz, d, ch, heads, alpha = dims["bsz"], dims["d"], dims["ch"], dims["heads"], dims["alpha"]
    mods = [mod[:, k * d:(k + 1) * d][:, None, :] for k in range(6)]
    shift1, scale1, gate1, shift2, scale2, gate2 = mods
    u = _ln_mod_fwd(x, scale1, shift1, bsz, name=f"ln_mod1_{tag}")
    zm = _matmul(u, p["w_main"], "nn", BF16, bias=p["b_main"], name=f"in_main_{tag}")
    zf = _matmul(u, p["w_f"], "nn", F32, bias=p["b_f"], name=f"in_forget_{tag}")
    a3 = _conv_branch_fwd(zm, p["conv_a_w"], p["conv_a_b"], p["ln_conv_g"], p["ln_conv_b"], bsz, ch,
                          name=f"conv_branch_{tag}")
    ya = _matmul(a3, p["w_conv_proj"], "nn", F32, name=f"conv_proj_{tag}")
    cumt, cumb = _fgate_fwd(zf, bsz, heads, name=f"fgate_{tag}")
    qkvt = _to_features_major(zm, 2 * ch, heads * HEAD_DIM, 3, name=f"qkv_t_{tag}")
    ot, lse = _attn_fwd(zm, qkvt, cumt, cumb, bsz, heads, 2 * ch, name=f"attn_{tag}")
    yb = _matmul(ot, p["w_attn_proj"], "tn", F32, name=f"attn_proj_{tag}")
    m = _gate_merge_fwd(zm, ya, yb, dims["ga_off"], name=f"merge_{tag}")
    mix = _matmul(m, p["w_mix_out"], "nn", F32, bias=p["b_mix_out"], name=f"mix_out_{tag}")
    x1 = _ln_res_fwd(x, mix, gate1, p["ln1_g"], p["ln1_b"], alpha, bsz, name=f"ln_res1_{tag}")
    u2 = _ln_mod_fwd(x1, scale2, shift2, bsz, name=f"ln_mod2_{tag}")
    hp = _matmul(u2, p["w_ffn_up"], "nn", BF16, name=f"ffn_up_{tag}")
    f = _ffn_act_fwd(hp, p["ffn_conv_w"], p["ffn_conv_b"], bsz, dims["tcf"], name=f"ffn_act_{tag}")
    ffn = _matmul(f, p["w_ffn_down"], "nn", F32, name=f"ffn_down_{tag}")
    x2 = _ln_res_fwd(x1, ffn, gate2, p["ln2_g"], p["ln2_b"], alpha, bsz, name=f"ln_res2_{tag}")
    saved = dict(x=x, mods=mods, u=u, zm=zm, zf=zf, a3=a3, ya=ya, yb=yb, cumt=cumt, cumb=cumb,
                 qkvt=qkvt, ot=ot, lse=lse, m=m, mix=mix, x1=x1, u2=u2, hp=hp, f=f, ffn=ffn)
    return x2, saved


def _layer_bwd(dx2, p, sv, dims, tag):
    bsz, ch, heads, alpha = dims["bsz"], dims["ch"], dims["heads"], dims["alpha"]
    f_off, tcf = dims["f_off"], dims["tcf"]
    shift1, scale1, gate1, shift2, scale2, gate2 = sv["mods"]
    g = {}
    dr2, dffn, dgate2, g["ln2_g"], g["ln2_b"], _ = _ln_res_bwd(
        dx2, sv["x1"], sv["ffn"], gate2, p["ln2_g"], alpha, bsz, name=f"ln_res2_bwd_{tag}")
    df = _matmul(dffn, p["w_ffn_down"], "nt", F32, name=f"ffn_down_dx_{tag}")
    g["w_ffn_down"] = _matmul(sv["f"], dffn, "tn", F32, name=f"ffn_down_dw_{tag}")
    dhg, dhl, dwg, dwl, dbg, dbl = _ffn_act_bwd(sv["hp"], df, p["ffn_conv_w"], p["ffn_conv_b"], bsz, tcf,
                                                name=f"ffn_act_bwd_{tag}")
    g["ffn_conv_w"] = jnp.concatenate([dwg, dwl], axis=1)
    g["ffn_conv_b"] = jnp.concatenate([dbg, dbl], axis=1)[0]
    du2 = _matmul(dhg, p["w_ffn_up_gate"], "nt", F32, name=f"ffn_up_gate_dx_{tag}")
    du2 = _matmul(dhl, p["w_ffn_up_lin"], "nt", F32, add=du2, name=f"ffn_up_lin_dx_{tag}")
    g["w_ffn_up"] = jnp.concatenate([_matmul(sv["u2"], dhg, "tn", F32, name=f"ffn_up_gate_dw_{tag}"),
                                     _matmul(sv["u2"], dhl, "tn", F32, name=f"ffn_up_lin_dw_{tag}")], axis=1)
    dx1, dscale2, dshift2 = _ln_mod_bwd(du2, sv["x1"], scale2, dr2, alpha, bsz, name=f"ln_mod2_bwd_{tag}")
    dr1, dmix, dgate1, g["ln1_g"], g["ln1_b"], g["b_mix_out"] = _ln_res_bwd(
        dx1, sv["x"], sv["mix"], gate1, p["ln1_g"], alpha, bsz, name=f"ln_res1_bwd_{tag}")
    dm = _matmul(dmix, p["w_mix_out"], "nt", F32, name=f"mix_out_dx_{tag}")
    g["w_mix_out"] = _matmul(sv["m"], dmix, "tn", F32, name=f"mix_out_dw_{tag}")
    dya, dyb, dzga, dzgb = _gate_merge_bwd(sv["zm"], sv["ya"], sv["yb"], dm, dims["ga_off"], name=f"merge_bwd_{tag}")
    da3 = _matmul(dya, p["w_conv_proj"], "nt", F32, name=f"conv_proj_dx_{tag}")
    g["w_conv_proj"] = _matmul(sv["a3"], dya, "tn", F32, name=f"conv_proj_dw_{tag}")
    do = _matmul(dyb, p["w_attn_proj"], "nt", BF16, name=f"attn_proj_dx_{tag}")
    dot = _matmul(p["w_attn_proj"], dyb, "nt", BF16, name=f"attn_proj_dxt_{tag}")
    g["w_attn_proj"] = _matmul(sv["ot"], dyb, "nn", F32, name=f"attn_proj_dw_{tag}")
    dzglu, g["conv_a_w"], dcb, g["ln_conv_g"], g["ln_conv_b"] = _conv_branch_bwd(
        sv["zm"], da3, p["conv_a_w"], p["conv_a_b"], p["ln_conv_g"], p["ln_conv_b"], bsz, ch,
        name=f"conv_branch_bwd_{tag}")
    g["conv_a_b"] = dcb[0]
    dk, dv, dqt, dcum = _attn_bwd(sv["zm"], sv["qkvt"], sv["cumt"], sv["cumb"], sv["ot"], do, dot, sv["lse"], bsz,
                                  heads, 2 * ch, name=f"attn_bwd_{tag}")
    dq = _to_rows_major(dqt, name=f"dq_rows_{tag}")
    dzf = _fgate_bwd(dcum, sv["zf"], bsz, name=f"fgate_bwd_{tag}")
    dzm = jnp.concatenate([dzglu, dq, dk, dv, dzga, dzgb], axis=1)
    du = _matmul(dzm, p["w_main"], "nt", F32, name=f"in_main_dx_{tag}")
    du = _matmul(dzf, p["w_f"], "nt", F32, add=du, name=f"in_forget_dx_{tag}")
    dwm, dbm = _matmul(sv["u"], dzm, "tn", F32, colsum=True, name=f"in_main_dw_{tag}")
    dwf, dbf = _matmul(sv["u"], dzf, "tn", F32, colsum=True, name=f"in_forget_dw_{tag}")
    dbm, dbf = dbm[0], dbf[0]
    g["w_in"] = jnp.concatenate([dwm[:, :f_off], dwf[:, :heads], dwm[:, f_off:]], axis=1)
    g["b_in"] = jnp.concatenate([dbm[:f_off], dbf[:heads], dbm[f_off:]])
    dx, dscale1, dshift1 = _ln_mod_bwd(du, sv["x"], scale1, dr1, alpha, bsz, name=f"ln_mod1_bwd_{tag}")
    dmod = jnp.concatenate([dshift1, dscale1, dgate1, dshift2, dscale2, dgate2], axis=2)[:, 0, :]
    return dx, g, dmod


def _local_step(x, mod, loss_target, full, rep, dims):
    bsz, seq, d = x.shape
    layers = mod.shape[0]
    params = [_layer_weights(full, rep, layer, dims) for layer in range(layers)]
    h = x.reshape(bsz * seq, d)
    saved = []
    for layer in range(layers):
        h, sv = _layer_fwd(h, mod[layer], params[layer], dims, f"l{layer}")
        saved.append(sv)
    dh, sq = _loss_head(h, loss_target.reshape(bsz * seq, d), name="loss_head")
    loss_local = 0.5 * jnp.sum(sq) / d
    grads, dmods = [None] * layers, [None] * layers
    for layer in reversed(range(layers)):
        dh, grads[layer], dmods[layer] = _layer_bwd(dh, params[layer], saved[layer], dims, f"l{layer}")
    stacked = {wname: jnp.stack([grads[layer][wname] for layer in range(layers)]) for wname in grads[0]}
    return loss_local, dh.reshape(bsz, seq, d), stacked, jnp.stack(dmods)


def _pad_rows(a):
    extra = -a.shape[-2] % (2 * SUBLANES)
    if extra == 0:
        return a
    return jnp.pad(a, [(0, 0)] * (a.ndim - 2) + [(0, extra), (0, 0)])


def _to_slab(g):
    layers, k, n4 = g.shape
    return jnp.transpose(g.reshape(layers, k, N_CHIPS, n4 // N_CHIPS), (0, 2, 1, 3))


def _from_slab(w):
    layers, _, k, n = w.shape
    return jnp.transpose(w, (0, 2, 1, 3)).reshape(layers, k, N_CHIPS * n)


def kernel(x, c, w_ada, b_ada, w_in, b_in, conv_a_w, conv_a_b, ln_conv_g, ln_conv_b, w_conv_proj, w_attn_proj, w_mix_out, b_mix_out, ln1_g, ln1_b, w_ffn_up, ffn_conv_w, ffn_conv_b, w_ffn_down, ln2_g, ln2_b, loss_target, m_w_ada, m_b_ada, m_w_in, m_b_in, m_conv_a_w, m_conv_a_b, m_ln_conv_g, m_ln_conv_b, m_w_conv_proj, m_w_attn_proj, m_w_mix_out, m_b_mix_out, m_ln1_g, m_ln1_b, m_w_ffn_up, m_ffn_conv_w, m_ffn_conv_b, m_w_ffn_down, m_ln2_g, m_ln2_b, v_w_ada, v_b_ada, v_w_in, v_b_in, v_conv_a_w, v_conv_a_b, v_ln_conv_g, v_ln_conv_b, v_w_conv_proj, v_w_attn_proj, v_w_mix_out, v_b_mix_out, v_ln1_g, v_ln1_b, v_w_ffn_up, v_ffn_conv_w, v_ffn_conv_b, v_w_ffn_down, v_ln2_g, v_ln2_b):
    weights = dict(zip(WEIGHTS, (w_ada, b_ada, w_in, b_in, conv_a_w, conv_a_b, ln_conv_g, ln_conv_b, w_conv_proj,
                                 w_attn_proj, w_mix_out, b_mix_out, ln1_g, ln1_b, w_ffn_up, ffn_conv_w, ffn_conv_b,
                                 w_ffn_down, ln2_g, ln2_b)))
    mom1 = dict(zip(WEIGHTS, (m_w_ada, m_b_ada, m_w_in, m_b_in, m_conv_a_w, m_conv_a_b, m_ln_conv_g, m_ln_conv_b,
                              m_w_conv_proj, m_w_attn_proj, m_w_mix_out, m_b_mix_out, m_ln1_g, m_ln1_b, m_w_ffn_up,
                              m_ffn_conv_w, m_ffn_conv_b, m_w_ffn_down, m_ln2_g, m_ln2_b)))
    mom2 = dict(zip(WEIGHTS, (v_w_ada, v_b_ada, v_w_in, v_b_in, v_conv_a_w, v_conv_a_b, v_ln_conv_g, v_ln_conv_b,
                              v_w_conv_proj, v_w_attn_proj, v_w_mix_out, v_b_mix_out, v_ln1_g, v_ln1_b, v_w_ffn_up,
                              v_ffn_conv_w, v_ffn_conv_b, v_w_ffn_down, v_ln2_g, v_ln2_b)))
    bsz, seq, d = x.shape
    layers = w_ada.shape[0]
    ch = conv_a_w.shape[2] * N_CHIPS
    width = w_attn_proj.shape[1]
    heads = width // HEAD_DIM
    d_ff = w_ffn_down.shape[1] * N_CHIPS
    dims = dict(bsz=bsz, d=d, ch=ch, heads=heads, alpha=(2.0 * layers) ** 0.25, f_off=2 * ch + 3 * width,
                ga_off=2 * ch + 3 * width, tcf=_pick(d_ff, (256, 128)))
    chip = 2 * lax.axis_index("x") + lax.axis_index("y")
    device = 2 * chip + lax.axis_index("c")
    ada_cols = w_ada.shape[2]

    c_act = _silu_rows(_all_gather8(c, name="gather_c").reshape(N_DEVICES * bsz, d), name="silu_c")
    b_ada_mine = lax.dynamic_slice_in_dim(b_ada, chip * ada_cols, ada_cols, axis=1)
    mod_cols = jnp.stack([_matmul(c_act, w_ada[layer], "nn", F32, bias=b_ada_mine[layer][None], name=f"ada_l{layer}")
                          for layer in range(layers)])
    mod_all = _all_gather8(mod_cols, name="gather_mod")
    mod_all = jnp.concatenate([mod_all[2 * k] for k in range(N_CHIPS)], axis=-1)
    mod = lax.dynamic_slice_in_dim(mod_all, device * bsz, bsz, axis=1)

    shards = [_pad_rows(weights[wname].astype(BF16) if as_bf16 else weights[wname]) for wname, _, as_bf16 in GATHERED]
    modes = [mode for _, mode, _ in GATHERED]
    whole = _gather_weights(shards, modes, name="gather_weights")
    full = {wname: w[:, :weights[wname].shape[1]] if mode == "cols" else w
            for (wname, mode, _), w in zip(GATHERED, whole)}
    full["w_in"] = _from_slab(full["w_in"])
    rep = {wname: weights[wname] for wname in REPLICATED}

    loss_local, grad_x, grads, dmod = _local_step(x, mod, loss_target, full, rep, dims)
    loss = lax.psum(loss_local, ("x", "y", "c"))

    grads["w_in"] = _to_slab(grads["w_in"])
    shard_shapes = [s.shape for s in shards]
    reduced = _reduce_scatter([_pad_rows(grads[wname]) for wname, _, _ in GATHERED], modes, shard_shapes)
    grad = {wname: r[:, :weights[wname].shape[1]] for (wname, _, _), r in zip(GATHERED, reduced)}

    small = jnp.concatenate([dmod.reshape(-1)] + [grads[wname].reshape(-1) for wname in REPLICATED])
    n_small = small.shape[0]
    rows = -(-n_small // (SUBLANES * LANES)) * SUBLANES
    small = jnp.pad(small, (0, rows * LANES - n_small)).reshape(rows, LANES)
    gathered = _all_gather8(small, name="gather_small")
    n_dmod = dmod.size
    dmod_all = gathered.reshape(N_DEVICES, -1)[:, :n_dmod].reshape(N_DEVICES, layers, bsz, 6 * d)
    dmod_all = jnp.transpose(dmod_all, (1, 0, 2, 3)).reshape(layers, N_DEVICES * bsz, 6 * d)
    summed = _sum_slots(gathered, name="sum_small").reshape(-1)
    off = n_dmod
    for wname in REPLICATED:
        n = weights[wname].size
        grad[wname] = summed[off:off + n].reshape(weights[wname].shape)
        off += n
    dmod_mine = lax.dynamic_slice_in_dim(dmod_all, chip * ada_cols, ada_cols, axis=2)
    grad["w_ada"] = jnp.stack([_matmul(c_act, dmod_mine[layer], "tn", F32, name=f"ada_dw_l{layer}")
                               for layer in range(layers)])
    grad["b_ada"] = jnp.stack([_colsum(dmod_all[layer], name=f"ada_db_l{layer}")[0] for layer in range(layers)])

    delta, new_m, new_v = {}, {}, {}
    for wname in WEIGHTS:
        delta[wname], new_m[wname], new_v[wname] = _adamw(weights[wname], grad[wname], mom1[wname], mom2[wname],
                                                          name=f"adamw_{wname}")
    return (loss, grad_x, *[grad[wname] for wname in WEIGHTS], *[delta[wname] for wname in WEIGHTS],
            *[new_m[wname] for wname in WEIGHTS], *[new_v[wname] for wname in WEIGHTS])
```

```python
import math

import jax
import jax.numpy as jnp
from jax import lax
from jax.experimental import pallas as pl
from jax.experimental.pallas import tpu as pltpu

F32 = jnp.float32
BF16 = jnp.bfloat16
MESH = pl.DeviceIdType.MESH

LN_EPS = 1e-5
HEAD_DIM = 64
ATTN_SCALE = HEAD_DIM ** -0.5
NEG = -1e30
FFN_PAD = 8
LANES = 128
SUBLANES = 8
ROW_CHUNK = 256
ATTN_BLOCK = 256
N_CHIPS = 4
N_DEVICES = 8
VMEM_LIMIT = 56 * 1024 * 1024

ADAM_LR = 0.001
ADAM_B1 = 0.9
ADAM_B2 = 0.999
ADAM_EPS = 1e-08
ADAM_WD = 0.01
ADAM_STEP = 10

GATHERED = (("w_in", "slab", True), ("conv_a_w", "cols", False), ("w_conv_proj", "cols", True),
            ("w_attn_proj", "cols", True), ("w_mix_out", "rows", True), ("w_ffn_up", "cols", True),
            ("ffn_conv_w", "cols", False), ("w_ffn_down", "rows", True))
REPLICATED = ("b_in", "conv_a_b", "ln_conv_g", "ln_conv_b", "b_mix_out", "ln1_g", "ln1_b",
              "ffn_conv_b", "ln2_g", "ln2_b")
WEIGHTS = ("w_ada", "b_ada", "w_in", "b_in", "conv_a_w", "conv_a_b", "ln_conv_g", "ln_conv_b",
           "w_conv_proj", "w_attn_proj", "w_mix_out", "b_mix_out", "ln1_g", "ln1_b", "w_ffn_up",
           "ffn_conv_w", "ffn_conv_b", "w_ffn_down", "ln2_g", "ln2_b")


def _pick(n, cands):
    for cand in cands:
        if n % cand == 0:
            return cand
    return n


def _call(body, *, name, grid, in_specs, out_specs, out_shape, scratch=(), sem=None):
    return pl.pallas_call(
        body, name=name, grid=grid, in_specs=in_specs, out_specs=out_specs, out_shape=out_shape,
        scratch_shapes=list(scratch),
        compiler_params=pltpu.CompilerParams(dimension_semantics=sem, vmem_limit_bytes=VMEM_LIMIT))


def _sds(shape, dtype):
    return jax.ShapeDtypeStruct(tuple(shape), dtype)


def _chunked(rows, fn):
    chunk = min(ROW_CHUNK, rows)
    if rows == chunk:
        fn(pl.ds(0, rows))
        return

    def step(i, carry):
        fn(pl.ds(pl.multiple_of(i * chunk, chunk), chunk))
        return carry

    lax.fori_loop(0, rows // chunk, step, 0)


def _matmul(a, b, mode, out_dtype, *, bias=None, add=None, colsum=False, name):
    if mode == "nn":
        (m, k), (_, n) = a.shape, b.shape
    elif mode == "nt":
        (m, k), (n, _) = a.shape, b.shape
    else:
        (k, m), (_, n) = a.shape, b.shape
    tm = _pick(m, (1024, 1408, 512, 256, 128))
    tn = _pick(n, (512, 1408, 256, 128))
    tk = k if k <= 1536 else _pick(k, (1024, 1536, 1408, 512, 256, 128))
    nk = k // tk
    if mode == "nn":
        a_spec = pl.BlockSpec((tm, tk), lambda i, j, kk: (i, kk))
        b_spec = pl.BlockSpec((tk, tn), lambda i, j, kk: (kk, j))
        dims = (((1,), (0,)), ((), ()))
    elif mode == "nt":
        a_spec = pl.BlockSpec((tm, tk), lambda i, j, kk: (i, kk))
        b_spec = pl.BlockSpec((tn, tk), lambda i, j, kk: (j, kk))
        dims = (((1,), (1,)), ((), ()))
    else:
        a_spec = pl.BlockSpec((tk, tm), lambda i, j, kk: (kk, i))
        b_spec = pl.BlockSpec((tk, tn), lambda i, j, kk: (kk, j))
        dims = (((0,), (0,)), ((), ()))
    in_specs = [a_spec, b_spec]
    operands = [a, b]
    if bias is not None:
        in_specs.append(pl.BlockSpec((1, tn), lambda i, j, kk: (0, j)))
        operands.append(bias)
    if add is not None:
        in_specs.append(pl.BlockSpec((tm, tn), lambda i, j, kk: (i, j)))
        operands.append(add)

    def body(a_ref, b_ref, *rest):
        rest = list(rest)
        bias_ref = rest.pop(0) if bias is not None else None
        add_ref = rest.pop(0) if add is not None else None
        o_ref = rest.pop(0)
        prod = lax.dot_general(a_ref[...].astype(BF16), b_ref[...].astype(BF16), dims,
                               preferred_element_type=F32)
        if colsum:
            cs_ref = rest.pop(0)
            part = jnp.sum(b_ref[...].astype(F32), axis=0, keepdims=True)

            @pl.when(pl.program_id(2) == 0)
            def _():
                cs_ref[...] = part

            @pl.when(pl.program_id(2) > 0)
            def _():
                cs_ref[...] += part

        def finish(r):
            if bias_ref is not None:
                r = r + bias_ref[...]
            if add_ref is not None:
                r = r + add_ref[...]
            o_ref[...] = r.astype(o_ref.dtype)

        if nk == 1:
            finish(prod)
            return
        acc_ref = rest.pop(0)
        kk = pl.program_id(2)

        @pl.when(kk == 0)
        def _():
            acc_ref[...] = prod

        @pl.when(kk > 0)
        def _():
            acc_ref[...] += prod

        @pl.when(kk == nk - 1)
        def _():
            finish(acc_ref[...])

    out_specs = pl.BlockSpec((tm, tn), lambda i, j, kk: (i, j))
    out_shape = _sds((m, n), out_dtype)
    if colsum:
        assert mode == "tn" and m == tm
        out_specs = [out_specs, pl.BlockSpec((1, tn), lambda i, j, kk: (0, j))]
        out_shape = [out_shape, _sds((1, n), F32)]
    return _call(body, name=name, grid=(m // tm, n // tn, nk), in_specs=in_specs, out_specs=out_specs,
                 out_shape=out_shape, scratch=[pltpu.VMEM((tm, tn), F32)] if nk > 1 else [],
                 sem=("parallel", "parallel", "arbitrary"))(*operands)


def _colsum(x, *, name):
    rows, n = x.shape
    tr = _pick(rows, (1024, 512, 256, 128))
    tn = _pick(n, (512, 256, 128))

    def body(x_ref, o_ref):
        @pl.when(pl.program_id(1) == 0)
        def _():
            o_ref[...] = jnp.zeros_like(o_ref)

        o_ref[...] += jnp.sum(x_ref[...].astype(F32), axis=0, keepdims=True)

    return _call(body, name=name, grid=(n // tn, rows // tr),
                 in_specs=[pl.BlockSpec((tr, tn), lambda j, i: (i, j))],
                 out_specs=pl.BlockSpec((1, tn), lambda j, i: (0, j)),
                 out_shape=_sds((1, n), F32), sem=("parallel", "arbitrary"))(x)


def _ln_stats(x):
    mu = jnp.mean(x, axis=-1, keepdims=True)
    xc = x - mu
    var = jnp.mean(xc * xc, axis=-1, keepdims=True)
    rstd = lax.rsqrt(var + LN_EPS)
    return xc * rstd, rstd


def _ln_bwd(dn, n, rstd):
    return rstd * (dn - jnp.mean(dn, axis=-1, keepdims=True) - n * jnp.mean(dn * n, axis=-1, keepdims=True))


def _seq_tiles(t, bsz, cands=(1024, 512, 256, 128, 64, 32, 16, 8)):
    s = t // bsz
    ts = _pick(s, cands)
    return s, ts, s // ts


def _ln_mod_fwd(x, scale, shift, bsz, *, name):
    t, d = x.shape
    _, ts, ns = _seq_tiles(t, bsz)

    def body(x_ref, sc_ref, sh_ref, u_ref):
        one_scale = 1.0 + sc_ref[0]
        shift_v = sh_ref[0]

        def piece(rows):
            n, _ = _ln_stats(x_ref[rows, :])
            u_ref[rows, :] = (n * one_scale + shift_v).astype(u_ref.dtype)

        _chunked(ts, piece)

    row = pl.BlockSpec((ts, d), lambda b, i: (b * ns + i, 0))
    per = pl.BlockSpec((1, 1, d), lambda b, i: (b, 0, 0))
    return _call(body, name=name, grid=(bsz, ns), in_specs=[row, per, per], out_specs=row,
                 out_shape=_sds((t, d), BF16), sem=("parallel", "parallel"))(x, scale, shift)


def _ln_mod_bwd(du, x, scale, dr, alpha, bsz, *, name):
    t, d = x.shape
    _, ts, ns = _seq_tiles(t, bsz)

    def body(du_ref, x_ref, sc_ref, dr_ref, dx_ref, dsc_ref, dsh_ref):
        @pl.when(pl.program_id(1) == 0)
        def _():
            dsc_ref[...] = jnp.zeros_like(dsc_ref)
            dsh_ref[...] = jnp.zeros_like(dsh_ref)

        one_scale = 1.0 + sc_ref[0]

        def piece(rows):
            du_v = du_ref[rows, :]
            n, rstd = _ln_stats(x_ref[rows, :])
            dsc_ref[0] += jnp.sum(du_v * n, axis=0, keepdims=True)
            dsh_ref[0] += jnp.sum(du_v, axis=0, keepdims=True)
            dx_ref[rows, :] = alpha * dr_ref[rows, :] + _ln_bwd(du_v * one_scale, n, rstd)

        _chunked(ts, piece)

    row = pl.BlockSpec((ts, d), lambda b, i: (b * ns + i, 0))
    per = pl.BlockSpec((1, 1, d), lambda b, i: (b, 0, 0))
    return _call(body, name=name, grid=(bsz, ns), in_specs=[row, row, per, row],
                 out_specs=[row, per, per],
                 out_shape=[_sds((t, d), F32), _sds((bsz, 1, d), F32), _sds((bsz, 1, d), F32)],
                 sem=("parallel", "arbitrary"))(du, x, scale, dr)


def _ln_res_fwd(x, y, gate, g, b, alpha, bsz, *, name):
    t, d = x.shape
    _, ts, ns = _seq_tiles(t, bsz)

    def body(x_ref, y_ref, gt_ref, g_ref, b_ref, o_ref):
        one_gate = 1.0 + gt_ref[0]

        def piece(rows):
            n, _ = _ln_stats(alpha * x_ref[rows, :] + one_gate * y_ref[rows, :])
            o_ref[rows, :] = n * g_ref[...] + b_ref[...]

        _chunked(ts, piece)

    row = pl.BlockSpec((ts, d), lambda bb, i: (bb * ns + i, 0))
    per = pl.BlockSpec((1, 1, d), lambda bb, i: (bb, 0, 0))
    vec = pl.BlockSpec((1, d), lambda bb, i: (0, 0))
    return _call(body, name=name, grid=(bsz, ns), in_specs=[row, row, per, vec, vec], out_specs=row,
                 out_shape=_sds((t, d), F32), sem=("parallel", "parallel"))(x, y, gate, g, b)


def _ln_res_bwd(do, x, y, gate, g, alpha, bsz, *, name):
    t, d = x.shape
    _, ts, ns = _seq_tiles(t, bsz)

    def body(do_ref, x_ref, y_ref, gt_ref, g_ref, dr_ref, dy_ref, dgt_ref, dg_ref, db_ref, dys_ref):
        first_tile = pl.program_id(1) == 0

        @pl.when(first_tile)
        def _():
            dgt_ref[...] = jnp.zeros_like(dgt_ref)

        @pl.when(jnp.logical_and(first_tile, pl.program_id(0) == 0))
        def _():
            dg_ref[...] = jnp.zeros_like(dg_ref)
            db_ref[...] = jnp.zeros_like(db_ref)
            dys_ref[...] = jnp.zeros_like(dys_ref)

        one_gate = 1.0 + gt_ref[0]

        def piece(rows):
            do_v = do_ref[rows, :]
            y_v = y_ref[rows, :]
            n, rstd = _ln_stats(alpha * x_ref[rows, :] + one_gate * y_v)
            dg_ref[...] += jnp.sum(do_v * n, axis=0, keepdims=True)
            db_ref[...] += jnp.sum(do_v, axis=0, keepdims=True)
            dr = _ln_bwd(do_v * g_ref[...], n, rstd)
            dr_ref[rows, :] = dr
            dy = one_gate * dr
            dy_ref[rows, :] = dy.astype(dy_ref.dtype)
            dys_ref[...] += jnp.sum(dy, axis=0, keepdims=True)
            dgt_ref[0] += jnp.sum(dr * y_v, axis=0, keepdims=True)

        _chunked(ts, piece)

    row = pl.BlockSpec((ts, d), lambda bb, i: (bb * ns + i, 0))
    per = pl.BlockSpec((1, 1, d), lambda bb, i: (bb, 0, 0))
    vec = pl.BlockSpec((1, d), lambda bb, i: (0, 0))
    return _call(body, name=name, grid=(bsz, ns), in_specs=[row, row, row, per, vec],
                 out_specs=[row, row, per, vec, vec, vec],
                 out_shape=[_sds((t, d), F32), _sds((t, d), BF16), _sds((bsz, 1, d), F32),
                            _sds((1, d), F32), _sds((1, d), F32), _sds((1, d), F32)],
                 sem=("arbitrary", "arbitrary"))(do, x, y, gate, g)


def _loss_head(y, target, *, name):
    t, d = y.shape
    tr = _pick(t, (1024, 512, 256, 128, 64, 32, 16, 8))

    def body(y_ref, t_ref, dy_ref, s_ref):
        @pl.when(pl.program_id(0) == 0)
        def _():
            s_ref[...] = jnp.zeros_like(s_ref)

        def piece(rows):
            e = y_ref[rows, :] - t_ref[rows, :]
            dy_ref[rows, :] = e * (1.0 / d)
            s_ref[...] += jnp.sum(e * e, axis=0, keepdims=True)

        _chunked(tr, piece)

    row = pl.BlockSpec((tr, d), lambda i: (i, 0))
    return _call(body, name=name, grid=(t // tr,), in_specs=[row, row],
                 out_specs=[row, pl.BlockSpec((1, d), lambda i: (0, 0))],
                 out_shape=[_sds((t, d), F32), _sds((1, d), F32)], sem=("arbitrary",))(y, target)


def _sigmoid(v):
    return 1.0 / (1.0 + jnp.exp(-v))


def _silu_rows(c, *, name):
    rows, d = c.shape

    def body(c_ref, o_ref):
        v = c_ref[...]
        o_ref[...] = (v * _sigmoid(v)).astype(o_ref.dtype)

    full = pl.BlockSpec((rows, d), lambda i: (0, 0))
    return _call(body, name=name, grid=(1,), in_specs=[full], out_specs=full,
                 out_shape=_sds((rows, d), BF16), sem=("arbitrary",))(c)


def _gate_cols(d, ga_off):
    tc = _pick(math.gcd(d, ga_off), (512, 256, 128))
    return tc, ga_off // tc, (ga_off + d) // tc


def _gate_merge_fwd(z, ya, yb, ga_off, *, name):
    t, d = ya.shape
    tr = _pick(t, (1024, 512, 256, 128, 64, 32, 16, 8))
    tc, ga_blk, gb_blk = _gate_cols(d, ga_off)

    def body(ga_ref, gb_ref, ya_ref, yb_ref, o_ref):
        def piece(rows):
            o_ref[rows, :] = (_sigmoid(ga_ref[rows, :].astype(F32)) * ya_ref[rows, :]
                              + _sigmoid(gb_ref[rows, :].astype(F32)) * yb_ref[rows, :]).astype(o_ref.dtype)

        _chunked(tr, piece)

    blk = pl.BlockSpec((tr, tc), lambda i, j: (i, j))
    return _call(body, name=name, grid=(t // tr, d // tc),
                 in_specs=[pl.BlockSpec((tr, tc), lambda i, j: (i, ga_blk + j)),
                           pl.BlockSpec((tr, tc), lambda i, j: (i, gb_blk + j)), blk, blk],
                 out_specs=blk, out_shape=_sds((t, d), BF16), sem=("parallel", "parallel"))(z, z, ya, yb)


def _gate_merge_bwd(z, ya, yb, dm, ga_off, *, name):
    t, d = ya.shape
    tr = _pick(t, (1024, 512, 256, 128, 64, 32, 16, 8))
    tc, ga_blk, gb_blk = _gate_cols(d, ga_off)

    def body(ga_ref, gb_ref, ya_ref, yb_ref, dm_ref, dya_ref, dyb_ref, dga_ref, dgb_ref):
        def piece(rows):
            dm_v = dm_ref[rows, :]
            sa = _sigmoid(ga_ref[rows, :].astype(F32))
            sb = _sigmoid(gb_ref[rows, :].astype(F32))
            dya_ref[rows, :] = (dm_v * sa).astype(dya_ref.dtype)
            dyb_ref[rows, :] = (dm_v * sb).astype(dyb_ref.dtype)
            dga_ref[rows, :] = (dm_v * ya_ref[rows, :] * sa * (1.0 - sa)).astype(dga_ref.dtype)
            dgb_ref[rows, :] = (dm_v * yb_ref[rows, :] * sb * (1.0 - sb)).astype(dgb_ref.dtype)

        _chunked(tr, piece)

    blk = pl.BlockSpec((tr, tc), lambda i, j: (i, j))
    return _call(body, name=name, grid=(t // tr, d // tc),
                 in_specs=[pl.BlockSpec((tr, tc), lambda i, j: (i, ga_blk + j)),
                           pl.BlockSpec((tr, tc), lambda i, j: (i, gb_blk + j)), blk, blk, blk],
                 out_specs=[blk, blk, blk, blk], out_shape=[_sds((t, d), BF16)] * 4,
                 sem=("parallel", "parallel"))(z, z, ya, yb, dm)


CONV_ROWS = 64
CONV_PAD = 32


def _row_shifts(win):
    total = win.shape[0]
    return [win] + [pltpu.roll(win, total - b, axis=0) for b in range(1, SUBLANES)]


def _shifted_rows(copies, shift):
    start = SUBLANES * (shift // SUBLANES)
    return copies[shift % SUBLANES][start:start + CONV_ROWS]


def _fill_glu(z_ref, ext_ref, s, ch):
    ext_ref[pl.ds(0, CONV_PAD), :] = jnp.zeros((CONV_PAD, ch), F32)

    chunk = min(ROW_CHUNK, s)

    def piece(i, carry):
        start = pl.multiple_of(i * chunk, chunk)
        zz = z_ref[pl.ds(start, chunk), :].astype(F32)
        ext_ref[pl.ds(pl.multiple_of(CONV_PAD + start, CONV_PAD), chunk), :] = zz[:, :ch] * _sigmoid(zz[:, ch:])
        return carry

    lax.fori_loop(0, s // chunk, piece, 0)


def _conv_piece(ext_ref, w_ref, cb_ref, base, kw):
    copies = _row_shifts(ext_ref[pl.ds(base, CONV_ROWS + CONV_PAD), :])
    acc = cb_ref[...] + w_ref[pl.ds(0, 1), :] * _shifted_rows(copies, CONV_PAD - (kw - 1))
    for k in range(1, kw):
        acc = acc + w_ref[pl.ds(k, 1), :] * _shifted_rows(copies, CONV_PAD - (kw - 1) + k)
    return acc, copies


def _conv_branch_fwd(z, w, cb, lg, lb, bsz, ch, *, name):
    t = z.shape[0]
    s = t // bsz
    kw = w.shape[0]

    def body(z_ref, w_ref, cb_ref, lg_ref, lb_ref, o_ref, ext_ref):
        _fill_glu(z_ref, ext_ref, s, ch)

        def step(i, carry):
            base = pl.multiple_of(i * CONV_ROWS, CONV_ROWS)
            a1, _ = _conv_piece(ext_ref, w_ref, cb_ref, base, kw)
            n, _ = _ln_stats(a1)
            a2 = n * lg_ref[...] + lb_ref[...]
            o_ref[pl.ds(base, CONV_ROWS), :] = (a2 * _sigmoid(a2)).astype(o_ref.dtype)
            return carry

        lax.fori_loop(0, s // CONV_ROWS, step, 0)

    vec = pl.BlockSpec((1, ch), lambda b: (0, 0))
    return _call(body, name=name, grid=(bsz,),
                 in_specs=[pl.BlockSpec((s, 2 * ch), lambda b: (b, 0)), pl.BlockSpec((kw, ch), lambda b: (0, 0)),
                           vec, vec, vec],
                 out_specs=pl.BlockSpec((s, ch), lambda b: (b, 0)), out_shape=_sds((t, ch), BF16),
                 scratch=[pltpu.VMEM((CONV_PAD + s, ch), F32)], sem=("parallel",))(z, w, cb, lg, lb)


def _conv_branch_bwd(z, da3, w, cb, lg, lb, bsz, ch, *, name):
    t = z.shape[0]
    s = t // bsz
    kw = w.shape[0]
    n_rows = CONV_ROWS + CONV_PAD

    def body(z_ref, d_ref, w_ref, cb_ref, lg_ref, lb_ref, dz_ref, dw_ref, dcb_ref, dlg_ref, dlb_ref,
             ext_ref, da1_ref):
        @pl.when(pl.program_id(0) == 0)
        def _():
            for ref in (dw_ref, dcb_ref, dlg_ref, dlb_ref):
                ref[...] = jnp.zeros_like(ref)

        _fill_glu(z_ref, ext_ref, s, ch)
        da1_ref[pl.ds(s, CONV_PAD), :] = jnp.zeros((CONV_PAD, ch), F32)

        def grad_a1(i, carry):
            dlg, dlb = carry
            base = pl.multiple_of(i * CONV_ROWS, CONV_ROWS)
            a1, _ = _conv_piece(ext_ref, w_ref, cb_ref, base, kw)
            n, rstd = _ln_stats(a1)
            a2 = n * lg_ref[...] + lb_ref[...]
            sg = _sigmoid(a2)
            da2 = d_ref[pl.ds(base, CONV_ROWS), :] * (sg * (1.0 + a2 * (1.0 - sg)))
            da1_ref[pl.ds(base, CONV_ROWS), :] = _ln_bwd(da2 * lg_ref[...], n, rstd)
            return (dlg + jnp.sum(da2 * n, axis=0, keepdims=True), dlb + jnp.sum(da2, axis=0, keepdims=True))

        zero = jnp.zeros((1, ch), F32)
        dlg, dlb = lax.fori_loop(0, s // CONV_ROWS, grad_a1, (zero, zero))
        dlg_ref[...] += dlg
        dlb_ref[...] += dlb

        def grad_z(i, dcb):
            base = pl.multiple_of(i * CONV_ROWS, CONV_ROWS)
            ahead = _row_shifts(da1_ref[pl.ds(base, n_rows), :])
            dyc = ahead[0][:CONV_ROWS]
            da0 = w_ref[pl.ds(kw - 1, 1), :] * dyc
            for k in range(kw - 1):
                da0 = da0 + w_ref[pl.ds(k, 1), :] * _shifted_rows(ahead, kw - 1 - k)
            behind = _row_shifts(ext_ref[pl.ds(base, n_rows), :])
            for k in range(kw):
                dw_ref[pl.ds(k, 1), :] += jnp.sum(dyc * _shifted_rows(behind, CONV_PAD - (kw - 1) + k),
                                                  axis=0, keepdims=True)
            zz = z_ref[pl.ds(base, CONV_ROWS), :].astype(F32)
            sg = _sigmoid(zz[:, ch:])
            dz_ref[pl.ds(base, CONV_ROWS), :ch] = (da0 * sg).astype(dz_ref.dtype)
            dz_ref[pl.ds(base, CONV_ROWS), ch:] = (da0 * zz[:, :ch] * sg * (1.0 - sg)).astype(dz_ref.dtype)
            return dcb + jnp.sum(dyc, axis=0, keepdims=True)

        dcb_ref[...] += lax.fori_loop(0, s // CONV_ROWS, grad_z, zero)

    vec = pl.BlockSpec((1, ch), lambda b: (0, 0))
    taps = pl.BlockSpec((kw, ch), lambda b: (0, 0))
    return _call(body, name=name, grid=(bsz,),
                 in_specs=[pl.BlockSpec((s, 2 * ch), lambda b: (b, 0)), pl.BlockSpec((s, ch), lambda b: (b, 0)),
                           taps, vec, vec, vec],
                 out_specs=[pl.BlockSpec((s, 2 * ch), lambda b: (b, 0)), taps, vec, vec, vec],
                 out_shape=[_sds((t, 2 * ch), BF16), _sds((kw, ch), F32)] + [_sds((1, ch), F32)] * 3,
                 scratch=[pltpu.VMEM((CONV_PAD + s, ch), F32), pltpu.VMEM((s + CONV_PAD, ch), F32)],
                 sem=("arbitrary",))(z, da3, w, cb, lg, lb)


FFN_ROWS = 64


def _gelu_parts(v):
    cdf = 0.5 * (1.0 + lax.erf(v * (2.0 ** -0.5)))
    return cdf, v * cdf


def _ffn_conv_piece(ext_ref, wb_ref, base):
    win = ext_ref[pl.ds(base, FFN_ROWS + FFN_PAD), :]
    acc = wb_ref[pl.ds(3, 1), :] + wb_ref[pl.ds(2, 1), :] * win[FFN_PAD:]
    acc = acc + wb_ref[pl.ds(1, 1), :] * pltpu.roll(win, 1, axis=0)[FFN_PAD:]
    acc = acc + wb_ref[pl.ds(0, 1), :] * pltpu.roll(win, 2, axis=0)[FFN_PAD:]
    return acc


def _ffn_stage(hg_ref, hl_ref, wg_ref, wl_ref, bg_ref, bl_ref, ext_ref, wb_ref, s, tcf):
    ext_ref[pl.ds(0, FFN_PAD), :] = jnp.zeros((FFN_PAD, 2 * tcf), F32)
    ext_ref[pl.ds(FFN_PAD, s), :tcf] = hg_ref[...].astype(F32)
    ext_ref[pl.ds(FFN_PAD, s), tcf:] = hl_ref[...].astype(F32)
    wb_ref[pl.ds(0, 3), :tcf] = wg_ref[...]
    wb_ref[pl.ds(0, 3), tcf:] = wl_ref[...]
    wb_ref[pl.ds(3, 1), :tcf] = bg_ref[...]
    wb_ref[pl.ds(3, 1), tcf:] = bl_ref[...]


def _ffn_specs(s, tcf, n_f, batch_first):
    def spec(rows, shift):
        if batch_first:
            return pl.BlockSpec((rows, tcf), lambda bb, j: (bb if rows == s else 0, shift + j))
        return pl.BlockSpec((rows, tcf), lambda j, bb: (bb if rows == s else 0, shift + j))

    return [spec(s, 0), spec(s, n_f), spec(3, 0), spec(3, n_f), spec(1, 0), spec(1, n_f)]


def _ffn_act_fwd(hp, w, b, bsz, tcf, *, name):
    t, two_f = hp.shape
    s = t // bsz
    n_f = two_f // (2 * tcf)

    def body(hg_ref, hl_ref, wg_ref, wl_ref, bg_ref, bl_ref, f_ref, ext_ref, wb_ref):
        _ffn_stage(hg_ref, hl_ref, wg_ref, wl_ref, bg_ref, bl_ref, ext_ref, wb_ref, s, tcf)

        def step(i, carry):
            base = pl.multiple_of(i * FFN_ROWS, FFN_ROWS)
            hh = _ffn_conv_piece(ext_ref, wb_ref, base)
            _, gelu = _gelu_parts(hh[:, :tcf])
            f_ref[pl.ds(base, FFN_ROWS), :] = (gelu * hh[:, tcf:]).astype(f_ref.dtype)
            return carry

        lax.fori_loop(0, s // FFN_ROWS, step, 0)

    return _call(body, name=name, grid=(bsz, n_f), in_specs=_ffn_specs(s, tcf, n_f, True),
                 out_specs=pl.BlockSpec((s, tcf), lambda bb, j: (bb, j)),
                 out_shape=_sds((t, two_f // 2), BF16),
                 scratch=[pltpu.VMEM((FFN_PAD + s, 2 * tcf), F32), pltpu.VMEM((SUBLANES, 2 * tcf), F32)],
                 sem=("parallel", "parallel"))(hp, hp, w, w, b, b)


def _ffn_act_bwd(hp, df, w, b, bsz, tcf, *, name):
    t, two_f = hp.shape
    s = t // bsz
    f_dim = two_f // 2
    n_f = f_dim // tcf
    gw = 2 * tcf
    n_rows = FFN_ROWS + FFN_PAD

    def body(hg_ref, hl_ref, wg_ref, wl_ref, bg_ref, bl_ref, df_ref,
             dhg_ref, dhl_ref, dwg_ref, dwl_ref, dbg_ref, dbl_ref, ext_ref, wb_ref, dh_ref):
        @pl.when(pl.program_id(1) == 0)
        def _():
            for ref in (dwg_ref, dwl_ref, dbg_ref, dbl_ref):
                ref[...] = jnp.zeros_like(ref)

        _ffn_stage(hg_ref, hl_ref, wg_ref, wl_ref, bg_ref, bl_ref, ext_ref, wb_ref, s, tcf)
        dh_ref[pl.ds(s, FFN_PAD), :] = jnp.zeros((FFN_PAD, gw), F32)

        def grad_h(i, carry):
            base = pl.multiple_of(i * FFN_ROWS, FFN_ROWS)
            hh = _ffn_conv_piece(ext_ref, wb_ref, base)
            hg = hh[:, :tcf]
            d = df_ref[pl.ds(base, FFN_ROWS), :]
            cdf, gelu = _gelu_parts(hg)
            pdf = jnp.exp(-0.5 * hg * hg) * (1.0 / math.sqrt(2.0 * math.pi))
            dh_ref[pl.ds(base, FFN_ROWS), :tcf] = d * hh[:, tcf:] * (cdf + hg * pdf)
            dh_ref[pl.ds(base, FFN_ROWS), tcf:] = d * gelu
            return carry

        lax.fori_loop(0, s // FFN_ROWS, grad_h, 0)

        def grad_x(i, carry):
            dw0, dw1, dw2, dbs = carry
            base = pl.multiple_of(i * FFN_ROWS, FFN_ROWS)
            nxt = dh_ref[pl.ds(base, n_rows), :]
            dyc = nxt[:FFN_ROWS]
            dx = wb_ref[pl.ds(2, 1), :] * dyc
            dx = dx + wb_ref[pl.ds(1, 1), :] * pltpu.roll(nxt, n_rows - 1, axis=0)[:FFN_ROWS]
            dx = dx + wb_ref[pl.ds(0, 1), :] * pltpu.roll(nxt, n_rows - 2, axis=0)[:FFN_ROWS]
            dhg_ref[pl.ds(base, FFN_ROWS), :] = dx[:, :tcf].astype(dhg_ref.dtype)
            dhl_ref[pl.ds(base, FFN_ROWS), :] = dx[:, tcf:].astype(dhl_ref.dtype)
            win = ext_ref[pl.ds(base, n_rows), :]
            dw2 = dw2 + jnp.sum(dyc * win[FFN_PAD:], axis=0, keepdims=True)
            dw1 = dw1 + jnp.sum(dyc * pltpu.roll(win, 1, axis=0)[FFN_PAD:], axis=0, keepdims=True)
            dw0 = dw0 + jnp.sum(dyc * pltpu.roll(win, 2, axis=0)[FFN_PAD:], axis=0, keepdims=True)
            return dw0, dw1, dw2, dbs + jnp.sum(dyc, axis=0, keepdims=True)

        zero = jnp.zeros((1, gw), F32)
        sums = lax.fori_loop(0, s // FFN_ROWS, grad_x, (zero, zero, zero, zero))
        for k in range(3):
            dwg_ref[pl.ds(k, 1), :] += sums[k][:, :tcf]
            dwl_ref[pl.ds(k, 1), :] += sums[k][:, tcf:]
        dbg_ref[...] += sums[3][:, :tcf]
        dbl_ref[...] += sums[3][:, tcf:]

    half = pl.BlockSpec((s, tcf), lambda j, bb: (bb, j))
    taps = pl.BlockSpec((3, tcf), lambda j, bb: (0, j))
    bias = pl.BlockSpec((1, tcf), lambda j, bb: (0, j))
    return _call(body, name=name, grid=(n_f, bsz), in_specs=_ffn_specs(s, tcf, n_f, False) + [half],
                 out_specs=[half, half, taps, taps, bias, bias],
                 out_shape=[_sds((t, f_dim), BF16)] * 2 + [_sds((3, f_dim), F32)] * 2 + [_sds((1, f_dim), F32)] * 2,
                 scratch=[pltpu.VMEM((FFN_PAD + s, gw), F32), pltpu.VMEM((SUBLANES, gw), F32),
                          pltpu.VMEM((s + FFN_PAD, gw), F32)],
                 sem=("parallel", "arbitrary"))(hp, hp, w, w, b, b, df)


def _split3(v):
    hi = v.astype(BF16)
    r = v - hi.astype(F32)
    mid = r.astype(BF16)
    lo = (r - mid.astype(F32)).astype(BF16)
    return hi, mid, lo


def _tri_dot(tri, v):
    out = None
    for part in _split3(v):
        term = jnp.dot(tri, part, preferred_element_type=F32)
        out = term if out is None else out + term
    return out


def _fgate_fwd(zf, bsz, heads, *, name):
    t, lanes = zf.shape
    s, blk, nb = _seq_tiles(t, bsz, (ATTN_BLOCK, 128))

    def body(z_ref, cumt_ref, cumb_ref, carry_ref):
        @pl.when(pl.program_id(1) == 0)
        def _():
            carry_ref[...] = jnp.zeros_like(carry_ref)

        z = z_ref[...]
        lf = jnp.minimum(z, 0.0) - jnp.log1p(jnp.exp(-jnp.abs(z)))
        r = lax.broadcasted_iota(jnp.int32, (blk, blk), 0)
        c = lax.broadcasted_iota(jnp.int32, (blk, blk), 1)
        tri = (r >= c).astype(BF16)
        cum = _tri_dot(tri, lf) + carry_ref[...]
        carry_ref[...] = cum[blk - 1:blk, :]
        cumt_ref[0] = jnp.transpose(cum)[:heads, :]
        for h in range(heads):
            cumb_ref[0, h] = jnp.broadcast_to(cum[:, h:h + 1], (blk, lanes))

    return _call(body, name=name, grid=(bsz, nb),
                 in_specs=[pl.BlockSpec((blk, lanes), lambda b, i: (b * nb + i, 0))],
                 out_specs=[pl.BlockSpec((1, heads, blk), lambda b, i: (b, 0, i)),
                            pl.BlockSpec((1, heads, blk, lanes), lambda b, i: (b, 0, i, 0))],
                 out_shape=[_sds((bsz, heads, s), F32), _sds((bsz, heads, s, lanes), F32)],
                 scratch=[pltpu.VMEM((1, lanes), F32)], sem=("parallel", "arbitrary"))(zf)


def _fgate_bwd(dcum, zf, bsz, *, name):
    t, lanes = zf.shape
    pairs = dcum.shape[1]
    s, blk, nb = _seq_tiles(t, bsz, (ATTN_BLOCK, 128))

    def body(d_ref, z_ref, o_ref, carry_ref):
        @pl.when(pl.program_id(1) == 0)
        def _():
            carry_ref[...] = jnp.zeros_like(carry_ref)

        dcol = d_ref[0, 0]
        for p in range(1, pairs):
            dcol = dcol + d_ref[0, p]
        r = lax.broadcasted_iota(jnp.int32, (blk, blk), 0)
        c = lax.broadcasted_iota(jnp.int32, (blk, blk), 1)
        tri = (c >= r).astype(BF16)
        suf = _tri_dot(tri, dcol) + carry_ref[...]
        carry_ref[...] = suf[0:1, :]
        o_ref[...] = suf * _sigmoid(-z_ref[...])

    return _call(body, name=name, grid=(bsz, nb),
                 in_specs=[pl.BlockSpec((1, pairs, blk, lanes), lambda b, i: (b, 0, nb - 1 - i, 0)),
                           pl.BlockSpec((blk, lanes), lambda b, i: (b * nb + nb - 1 - i, 0))],
                 out_specs=pl.BlockSpec((blk, lanes), lambda b, i: (b * nb + nb - 1 - i, 0)),
                 out_shape=_sds((t, lanes), F32), scratch=[pltpu.VMEM((1, lanes), F32)],
                 sem=("parallel", "arbitrary"))(dcum, zf)


def _to_features_major(z, col_off, width, n, *, name):
    t = z.shape[0]
    tr = _pick(t, (512, 256, 128))
    first = col_off // width

    def body(*refs):
        o_ref = refs[n]
        for g in range(n):
            o_ref[pl.ds(g * width, width), :] = jnp.transpose(refs[g][...].astype(F32)).astype(o_ref.dtype)

    return _call(body, name=name, grid=(t // tr,),
                 in_specs=[pl.BlockSpec((tr, width), lambda i, g=g: (i, first + g)) for g in range(n)],
                 out_specs=pl.BlockSpec((n * width, tr), lambda i: (0, i)),
                 out_shape=_sds((n * width, t), BF16), sem=("parallel",))(*([z] * n))


def _to_rows_major(xt, *, name):
    w, t = xt.shape
    tr = _pick(t, (512, 256, 128))

    def body(x_ref, o_ref):
        o_ref[...] = jnp.transpose(x_ref[...]).astype(o_ref.dtype)

    return _call(body, name=name, grid=(t // tr,),
                 in_specs=[pl.BlockSpec((w, tr), lambda i: (0, i))],
                 out_specs=pl.BlockSpec((tr, w), lambda i: (i, 0)),
                 out_shape=_sds((t, w), BF16), sem=("parallel",))(xt)


def _loop_by_twos(lo, hi, body, carry):
    count = hi - lo

    def two(t, cr):
        i = lo + 2 * t
        return body(i + 1, body(i, cr))

    carry = lax.fori_loop(0, count // 2, two, carry)
    return lax.cond(count % 2 == 1, lambda cr: body(hi - 1, cr), lambda cr: cr, carry)


def _head_masks(shape, axis):
    feat = lax.broadcasted_iota(jnp.int32, shape, axis)
    return feat < HEAD_DIM, feat >= HEAD_DIM


def _attn_fwd(z, qkvt, cumt, cumb, bsz, heads, q_off, *, name):
    t = z.shape[0]
    width = heads * HEAD_DIM
    pairs = heads // 2
    s = t // bsz
    blk = ATTN_BLOCK
    nq = s // blk
    k_col = (q_off + width) // LANES
    v_row = 2 * width // LANES
    reps = blk // LANES

    def body(k_ref, qt_ref, vt_ref, cqt_ref, ckb_ref, ot_ref, lse_ref):
        p_id = pl.program_id(1)
        i = pl.program_id(2)
        qt = qt_ref[...]
        masks = _head_masks((LANES, blk), 0)
        qtm = [jnp.where(mk, qt, jnp.zeros_like(qt)) for mk in masks]
        cq = [cqt_ref[0, pl.ds(2 * p_id + hh, 1), :] for hh in range(2)]
        kidx = lax.broadcasted_iota(jnp.int32, (blk, blk), 0)
        qidx = lax.broadcasted_iota(jnp.int32, (blk, blk), 1)

        def block(j, carry, masked):
            off = pl.multiple_of(j * blk, blk)
            kp = k_ref[pl.ds(off, blk), :].astype(BF16)
            vtp = vt_ref[:, pl.ds(off, blk)]
            out = []
            for hh in range(2):
                m, l, acc = carry[hh]
                sc = jnp.dot(kp, qtm[hh], preferred_element_type=F32) * ATTN_SCALE
                ck = ckb_ref[0, hh, pl.ds(off, blk), :]
                sc = (sc + cq[hh]) - jnp.concatenate([ck] * reps, axis=1)
                if masked:
                    sc = jnp.where(qidx >= kidx, sc, NEG)
                m_new = jnp.maximum(m, jnp.max(sc, axis=0, keepdims=True))
                pr = jnp.exp(sc - m_new)
                a = jnp.exp(m - m_new)
                l = a * l + jnp.sum(pr, axis=0, keepdims=True)
                p_hi = pr.astype(BF16)
                p_lo = (pr - p_hi.astype(F32)).astype(BF16)
                pv = (jnp.dot(vtp, p_hi, preferred_element_type=F32)
                      + jnp.dot(vtp, p_lo, preferred_element_type=F32))
                acc = a * acc + pv[hh * HEAD_DIM:(hh + 1) * HEAD_DIM]
                out.append((m_new, l, acc))
            return tuple(out)

        init = tuple((jnp.full((1, blk), NEG, F32), jnp.zeros((1, blk), F32), jnp.zeros((HEAD_DIM, blk), F32))
                     for _ in range(2))
        carry = _loop_by_twos(0, i, lambda j, cr: block(j, cr, False), init)
        carry = block(i, carry, True)
        lse_ref[...] = jnp.zeros_like(lse_ref)
        for hh in range(2):
            m, l, acc = carry[hh]
            ot_ref[pl.ds(hh * HEAD_DIM, HEAD_DIM), :] = acc / l
            lse_ref[0, 0, pl.ds(hh, 1), :] = m + jnp.log(l)

    return _call(body, name=name, grid=(bsz, pairs, nq),
                 in_specs=[pl.BlockSpec((s, LANES), lambda b, p, i: (b, k_col + p)),
                           pl.BlockSpec((LANES, blk), lambda b, p, i: (p, b * nq + i)),
                           pl.BlockSpec((LANES, s), lambda b, p, i: (v_row + p, b)),
                           pl.BlockSpec((1, heads, blk), lambda b, p, i: (b, 0, i)),
                           pl.BlockSpec((1, 2, s, LANES), lambda b, p, i: (b, p, 0, 0))],
                 out_specs=[pl.BlockSpec((LANES, blk), lambda b, p, i: (p, b * nq + i)),
                            pl.BlockSpec((1, 1, SUBLANES, blk), lambda b, p, i: (b, p, 0, i))],
                 out_shape=[_sds((width, t), F32), _sds((bsz, pairs, SUBLANES, s), F32)],
                 sem=("parallel", "parallel", "parallel"))(z, qkvt, qkvt, cumt, cumb)


def _attn_bwd(z, qkvt, cumt, cumb, ot, do, dot, lse, bsz, heads, q_off, *, name):
    t = z.shape[0]
    width = heads * HEAD_DIM
    pairs = heads // 2
    s = t // bsz
    blk = ATTN_BLOCK
    nkv = s // blk
    q_col = q_off // LANES
    k_col = (q_off + width) // LANES
    v_col = (q_off + 2 * width) // LANES
    k_row = width // LANES
    reps = blk // LANES

    def body(k_ref, v_ref, kt_ref, q_ref, qt_ref, do_ref, dot_ref, ot_ref, lse_ref, ckb_ref, cqt_ref,
             dk_ref, dv_ref, dqt_ref, dcum_ref, dqt_acc, ds_acc):
        p_id = pl.program_id(1)
        j = pl.program_id(2)

        @pl.when(j == 0)
        def _():
            dqt_acc[...] = jnp.zeros_like(dqt_acc)

        kp = k_ref[...].astype(BF16)
        vp = v_ref[...].astype(BF16)
        kt = kt_ref[...]
        feat_masks = _head_masks((LANES, blk), 0)
        lane_masks = _head_masks((blk, LANES), 1)
        ktm = [jnp.where(mk, kt, jnp.zeros_like(kt)) for mk in feat_masks]
        ck = [jnp.concatenate([ckb_ref[0, hh]] * reps, axis=1) for hh in range(2)]
        kidx = lax.broadcasted_iota(jnp.int32, (blk, blk), 0)
        qidx = lax.broadcasted_iota(jnp.int32, (blk, blk), 1)
        ds_acc[...] = jnp.zeros_like(ds_acc)

        def block(i, carry, masked):
            dk, dv = carry
            off = pl.multiple_of(i * blk, blk)
            qt = qt_ref[:, pl.ds(off, blk)]
            dt = dot_ref[:, pl.ds(off, blk)]
            o_t = ot_ref[:, pl.ds(off, blk)]
            q_rows = q_ref[pl.ds(off, blk), :].astype(BF16)
            do_rows = do_ref[pl.ds(off, blk), :]
            for hh in range(2):
                qtm = jnp.where(feat_masks[hh], qt, jnp.zeros_like(qt))
                dtm = jnp.where(feat_masks[hh], dt, jnp.zeros_like(dt))
                sc = jnp.dot(kp, qtm, preferred_element_type=F32) * ATTN_SCALE
                sc = (sc + cqt_ref[0, pl.ds(2 * p_id + hh, 1), pl.ds(off, blk)]) - ck[hh]
                pr = jnp.exp(sc - lse_ref[0, 0, pl.ds(hh, 1), pl.ds(off, blk)])
                if masked:
                    pr = jnp.where(qidx >= kidx, pr, 0.0)
                dp = jnp.dot(vp, dtm, preferred_element_type=F32)
                delta = jnp.sum(dtm.astype(F32) * o_t, axis=0, keepdims=True)
                ds = pr * (dp - delta)
                ds_acc[hh] += ds
                dsb = ds.astype(BF16)
                qm = jnp.where(lane_masks[hh], q_rows, jnp.zeros_like(q_rows))
                dom = jnp.where(lane_masks[hh], do_rows, jnp.zeros_like(do_rows))
                dv = dv + jnp.dot(pr.astype(BF16), dom, preferred_element_type=F32)
                dk = dk + jnp.dot(dsb, qm, preferred_element_type=F32) * ATTN_SCALE
                dqt_acc[:, pl.ds(off, blk)] += jnp.dot(ktm[hh], dsb, preferred_element_type=F32) * ATTN_SCALE
            return dk, dv

        zero = jnp.zeros((blk, LANES), F32)
        carry = block(j, (zero, zero), True)
        dk, dv = _loop_by_twos(j + 1, nkv, lambda i, cr: block(i, cr, False), carry)
        dk_ref[...] = dk.astype(dk_ref.dtype)
        dv_ref[...] = dv.astype(dv_ref.dtype)
        lane = lax.broadcasted_iota(jnp.int32, (blk, LANES), 1)
        dcum = jnp.zeros((blk, LANES), F32)
        for hh in range(2):
            col = jnp.sum(ds_acc[hh], axis=1, keepdims=True)
            dcum = jnp.where(lane == 2 * p_id + hh, -col, dcum)
        dcum_ref[0, 0] = dcum

        @pl.when(j == nkv - 1)
        def _():
            dqt_ref[...] = dqt_acc[...]

    key_rows = lambda col: pl.BlockSpec((blk, LANES), lambda b, p, j: (b * nkv + j, col + p))
    seq_t = lambda row: pl.BlockSpec((LANES, s), lambda b, p, j: (row + p, b))
    return _call(body, name=name, grid=(bsz, pairs, nkv),
                 in_specs=[key_rows(k_col), key_rows(v_col),
                           pl.BlockSpec((LANES, blk), lambda b, p, j: (k_row + p, b * nkv + j)),
                           pl.BlockSpec((s, LANES), lambda b, p, j: (b, q_col + p)), seq_t(0),
                           pl.BlockSpec((s, LANES), lambda b, p, j: (b, p)), seq_t(0), seq_t(0),
                           pl.BlockSpec((1, 1, SUBLANES, s), lambda b, p, j: (b, p, 0, 0)),
                           pl.BlockSpec((1, 2, blk, LANES), lambda b, p, j: (b, p, j, 0)),
                           pl.BlockSpec((1, heads, s), lambda b, p, j: (b, 0, 0))],
                 out_specs=[key_rows(0), key_rows(0), seq_t(0),
                            pl.BlockSpec((1, 1, blk, LANES), lambda b, p, j: (b, p, j, 0))],
                 out_shape=[_sds((t, width), BF16), _sds((t, width), BF16), _sds((width, t), F32),
                            _sds((bsz, pairs, s, LANES), F32)],
                 scratch=[pltpu.VMEM((LANES, s), F32), pltpu.VMEM((2, blk, blk), F32)],
                 sem=("parallel", "parallel", "arbitrary"))(z, z, qkvt, z, qkvt, do, dot, ot, lse, cumb, cumt)


def _adamw(w, g, m, v, *, name):
    bc1 = 1.0 - ADAM_B1 ** ADAM_STEP
    bc2 = 1.0 - ADAM_B2 ** ADAM_STEP

    def body(w_ref, g_ref, m_ref, v_ref, d_ref, nm_ref, nv_ref):
        g_v = g_ref[...]
        nm = ADAM_B1 * m_ref[...] + (1.0 - ADAM_B1) * g_v
        nv = ADAM_B2 * v_ref[...] + (1.0 - ADAM_B2) * (g_v * g_v)
        nm_ref[...] = nm
        nv_ref[...] = nv
        d_ref[...] = -ADAM_LR * ((nm / bc1) / (jnp.sqrt(nv / bc2) + ADAM_EPS) + ADAM_WD * w_ref[...])

    if w.ndim == 2:
        grid = (1,)
        blk = pl.BlockSpec(w.shape, lambda i: (0, 0))
    else:
        layers, rows, cols = w.shape
        tr = rows if rows <= 256 else _pick(rows, (256, 128, 64, 32, 16, 8))
        grid = (layers, rows // tr)
        blk = pl.BlockSpec((1, tr, cols), lambda layer, i: (layer, i, 0))
    return tuple(_call(body, name=name, grid=grid, in_specs=[blk] * 4, out_specs=[blk] * 3,
                       out_shape=[_sds(w.shape, F32)] * 3, sem=("parallel",) * len(grid))(w, g, m, v))


_ANY = pl.BlockSpec(memory_space=pl.ANY)


def _comm_call(body, *, name, n_in, out_shape, n_sems):
    scratch = [pltpu.SemaphoreType.DMA((n_sems,)), pltpu.SemaphoreType.DMA((n_sems,)),
               pltpu.SemaphoreType.DMA((len(out_shape),))]
    return pl.pallas_call(body, name=name, in_specs=[_ANY] * n_in, out_specs=[_ANY] * len(out_shape),
                          out_shape=out_shape, scratch_shapes=scratch)


def _place():
    x, y, c = lax.axis_index("x"), lax.axis_index("y"), lax.axis_index("c")
    return x, y, c, [(1 - x, y), (x, 1 - y), (1 - x, 1 - y)]


def _remote(src, dst, send_sems, recv_sems, sem, to):
    return pltpu.make_async_remote_copy(src_ref=src, dst_ref=dst, send_sem=send_sems.at[sem],
                                        recv_sem=recv_sems.at[sem], device_id=to, device_id_type=MESH)


def _all_gather8(v, *, name):
    def body(v_ref, out_ref, send_sems, recv_sems, local_sems):
        x, y, c, _ = _place()
        me = 4 * x + 2 * y + c
        mine = pltpu.make_async_copy(v_ref, out_ref.at[me], local_sems.at[0])
        mine.start()
        peers = []
        for k in range(1, N_DEVICES):
            px = 1 - x if k & 4 else x
            py = 1 - y if k & 2 else y
            pc = 1 - c if k & 1 else c
            peers.append((px, py, pc))
        sends = [_remote(v_ref, out_ref.at[me], send_sems, recv_sems, k, peer) for k, peer in enumerate(peers)]
        for cp in sends:
            cp.start()
        for k, (px, py, pc) in enumerate(peers):
            _remote(v_ref, out_ref.at[4 * px + 2 * py + pc], send_sems, recv_sems, k, (px, py, pc)).wait_recv()
        for cp in sends:
            cp.wait_send()
        mine.wait()

    out = _comm_call(body, name=name, n_in=1, out_shape=[_sds((N_DEVICES,) + v.shape, v.dtype)],
                     n_sems=N_DEVICES - 1)(v)
    return out[0]


def _window(ref, mode, layer, chip, rows, cols):
    if mode == "slab":
        return ref.at[layer, chip]
    if mode == "cols":
        return ref.at[layer, :, pl.ds(pl.multiple_of(chip * cols, LANES), cols)]
    return ref.at[layer, pl.ds(pl.multiple_of(chip * rows, SUBLANES), rows), :]


def _whole_shape(mode, shard_shape):
    layers, rows, cols = shard_shape
    if mode == "slab":
        return (layers, N_CHIPS, rows, cols)
    if mode == "cols":
        assert cols % LANES == 0
        return (layers, rows, N_CHIPS * cols)
    assert rows % 16 == 0
    return (layers, N_CHIPS * rows, cols)


def _gather_weights(shards, modes, *, name):
    n = len(shards)
    meta = [(mode,) + tuple(a.shape[1:]) for a, mode in zip(shards, modes)]
    for a in shards:
        assert a.shape[0] == 2
    per = 7

    def body(*refs):
        ins, outs = refs[:n], refs[n:2 * n]
        send_sems, recv_sems, _ = refs[2 * n:]
        x, y, c, chips = _place()
        me = 2 * x + y
        sibling = (x, y, 1 - c)
        own, first, passed = [], [], []
        for i, (mode, rows, cols) in enumerate(meta):
            for r, (cx, cy) in enumerate(chips):
                cp = _remote(ins[i].at[c], _window(outs[i], mode, c, me, rows, cols), send_sems, recv_sems,
                             per * i + r, (cx, cy, c))
                cp.start()
                first.append(cp)
            cp = _remote(ins[i], _window(outs[i], mode, slice(None), me, rows, cols), send_sems, recv_sems,
                         per * i + 6, sibling)
            cp.start()
            own.append(cp)
        for i, (mode, rows, cols) in enumerate(meta):
            for r, (cx, cy) in enumerate(chips):
                win = _window(outs[i], mode, c, 2 * cx + cy, rows, cols)
                _remote(win, win, send_sems, recv_sems, per * i + r, (cx, cy, c)).wait_recv()
                cp = _remote(win, win, send_sems, recv_sems, per * i + 3 + r, sibling)
                cp.start()
                passed.append(cp)
        for i, (mode, rows, cols) in enumerate(meta):
            own[i].wait_recv()
            for r, (cx, cy) in enumerate(chips):
                win = _window(outs[i], mode, 1 - c, 2 * cx + cy, rows, cols)
                _remote(win, win, send_sems, recv_sems, per * i + 3 + r, sibling).wait_recv()
        for cp in first + passed + own:
            cp.wait_send()

    out_shape = [_sds(_whole_shape(mode, a.shape), a.dtype) for a, mode in zip(shards, modes)]
    return _comm_call(body, name=name, n_in=n, out_shape=out_shape, n_sems=per * n)(*shards)


def _rs_swap(grads, *, name):
    n = len(grads)

    def body(*refs):
        ins, outs = refs[:n], refs[n:2 * n]
        send_sems, recv_sems, _ = refs[2 * n:]
        x, y, c, _ = _place()
        copies = [_remote(ins[i].at[1 - c], outs[i], send_sems, recv_sems, i, (x, y, 1 - c)) for i in range(n)]
        for cp in copies:
            cp.start()
        for cp in copies:
            cp.wait()

    return _comm_call(body, name=name, n_in=n, out_shape=[_sds(g.shape[1:], g.dtype) for g in grads], n_sems=n)(*grads)


def _part(ref, mode, chip, rows, cols):
    if mode == "slab":
        return ref.at[chip]
    if mode == "cols":
        return ref.at[:, pl.ds(pl.multiple_of(chip * cols, LANES), cols)]
    return ref.at[pl.ds(pl.multiple_of(chip * rows, SUBLANES), rows), :]


def _rs_scatter(parts, modes, shard_shapes, *, name):
    n = len(parts)
    meta = [(mode,) + tuple(shp[1:]) for mode, shp in zip(modes, shard_shapes)]

    def body(*refs):
        ins, outs = refs[:n], refs[n:2 * n]
        send_sems, recv_sems, local_sems = refs[2 * n:]
        x, y, c, chips = _place()
        me = 2 * x + y
        local, sends = [], []
        for i, (mode, rows, cols) in enumerate(meta):
            cp = pltpu.make_async_copy(_part(ins[i], mode, me, rows, cols), outs[i].at[me], local_sems.at[i])
            cp.start()
            local.append(cp)
            for r, (cx, cy) in enumerate(chips):
                cp = _remote(_part(ins[i], mode, 2 * cx + cy, rows, cols), outs[i].at[me], send_sems, recv_sems,
                             3 * i + r, (cx, cy, c))
                cp.start()
                sends.append(cp)
        for i, (mode, rows, cols) in enumerate(meta):
            for r, (cx, cy) in enumerate(chips):
                k = 2 * cx + cy
                _remote(_part(ins[i], mode, k, rows, cols), outs[i].at[k], send_sems, recv_sems, 3 * i + r,
                        (cx, cy, c)).wait_recv()
        for cp in sends:
            cp.wait_send()
        for cp in local:
            cp.wait()

    out_shape = [_sds((N_CHIPS,) + tuple(shp[1:]), p.dtype) for p, shp in zip(parts, shard_shapes)]
    return _comm_call(body, name=name, n_in=n, out_shape=out_shape, n_sems=3 * n)(*parts)


def _rs_exchange(sums, *, name):
    n = len(sums)

    def body(*refs):
        ins, outs = refs[:n], refs[n:2 * n]
        send_sems, recv_sems, _ = refs[2 * n:]
        x, y, c, _ = _place()
        copies = [_remote(ins[i], outs[i], send_sems, recv_sems, i, (x, y, 1 - c)) for i in range(n)]
        for cp in copies:
            cp.start()
        for cp in copies:
            cp.wait()

    return _comm_call(body, name=name, n_in=n, out_shape=[_sds(s.shape, s.dtype) for s in sums], n_sems=n)(*sums)


def _row_tile(rows, cols, itemsize):
    target = max(SUBLANES, (2 << 20) // (cols * itemsize))
    cands = [c for c in (2048, 1024, 512, 256, 128, 64, 32, 16) if c <= target]
    tr = _pick(rows, cands)
    return tr


def _add_layer(g, other, core, *, name):
    _, rows, cols = g.shape
    tr = _row_tile(rows, cols, 4)

    def body(core_ref, g_ref, o_ref, out_ref):
        out_ref[...] = (g_ref[0] + o_ref[...]).astype(out_ref.dtype)

    grid_spec = pltpu.PrefetchScalarGridSpec(
        num_scalar_prefetch=1, grid=(rows // tr,),
        in_specs=[pl.BlockSpec((1, tr, cols), lambda i, core_ref: (core_ref[0], i, 0)),
                  pl.BlockSpec((tr, cols), lambda i, core_ref: (i, 0))],
        out_specs=pl.BlockSpec((tr, cols), lambda i, core_ref: (i, 0)))
    return pl.pallas_call(body, name=name, grid_spec=grid_spec, out_shape=_sds((rows, cols), BF16),
                          compiler_params=pltpu.CompilerParams(dimension_semantics=("parallel",),
                                                               vmem_limit_bytes=VMEM_LIMIT))(core, g, other)


def _sum_slots(parts, *, name):
    n, rows, cols = parts.shape
    tr = _row_tile(rows, cols, 4)

    def body(p_ref, o_ref):
        acc = p_ref[0].astype(F32) + p_ref[1].astype(F32)
        for k in range(2, n):
            acc = acc + p_ref[k].astype(F32)
        o_ref[...] = acc

    return _call(body, name=name, grid=(rows // tr,),
                 in_specs=[pl.BlockSpec((n, tr, cols), lambda i: (0, i, 0))],
                 out_specs=pl.BlockSpec((tr, cols), lambda i: (i, 0)),
                 out_shape=_sds((rows, cols), F32), sem=("parallel",))(parts)


def _reduce_scatter(grads, modes, shard_shapes):
    core = lax.axis_index("c").astype(jnp.int32).reshape(1)
    flat = [g.reshape(g.shape[0], -1, g.shape[-1]) for g in grads]
    from_sibling = _rs_swap(flat, name="rs_swap")
    parts = []
    for i, (g, o) in enumerate(zip(flat, from_sibling)):
        p = _add_layer(g, o, core, name=f"rs_add_{i}")
        parts.append(p.reshape(grads[i].shape[1:]))
    from_chips = _rs_scatter(parts, modes, shard_shapes, name="rs_scatter")
    sums = [_sum_slots(r, name=f"rs_sum_{i}") for i, r in enumerate(from_chips)]
    others = _rs_exchange(sums, name="rs_exchange")
    mine_first = lax.axis_index("c") == 0
    return [jnp.where(mine_first, jnp.stack([mine, other]), jnp.stack([other, mine]))
            for mine, other in zip(sums, others)]


def _layer_weights(full, rep, layer, dims):
    f_off, n_heads = dims["f_off"], dims["heads"]
    d_ff = full["w_ffn_up"].shape[-1] // 2
    w_in = full["w_in"][layer]
    b_in = rep["b_in"][layer]
    pad = LANES - n_heads
    return {
        "w_main": jnp.concatenate([w_in[:, :f_off], w_in[:, f_off + n_heads:]], axis=1),
        "b_main": jnp.concatenate([b_in[:f_off], b_in[f_off + n_heads:]])[None],
        "w_f": jnp.pad(w_in[:, f_off:f_off + n_heads], ((0, 0), (0, pad))),
        "b_f": jnp.pad(b_in[f_off:f_off + n_heads], (0, pad))[None],
        "conv_a_w": full["conv_a_w"][layer],
        "conv_a_b": rep["conv_a_b"][layer][None],
        "ln_conv_g": rep["ln_conv_g"][layer][None],
        "ln_conv_b": rep["ln_conv_b"][layer][None],
        "w_conv_proj": full["w_conv_proj"][layer],
        "w_attn_proj": full["w_attn_proj"][layer],
        "w_mix_out": full["w_mix_out"][layer],
        "b_mix_out": rep["b_mix_out"][layer][None],
        "ln1_g": rep["ln1_g"][layer][None],
        "ln1_b": rep["ln1_b"][layer][None],
        "w_ffn_up": full["w_ffn_up"][layer],
        "w_ffn_up_gate": full["w_ffn_up"][layer][:, :d_ff],
        "w_ffn_up_lin": full["w_ffn_up"][layer][:, d_ff:],
        "ffn_conv_w": full["ffn_conv_w"][layer],
        "ffn_conv_b": rep["ffn_conv_b"][layer][None],
        "w_ffn_down": full["w_ffn_down"][layer],
        "ln2_g": rep["ln2_g"][layer][None],
        "ln2_b": rep["ln2_b"][layer][None],
    }


def _layer_fwd(x, mod, p, dims, tag):
    bsz, d, ch, heads, alpha = dims["bsz"], dims["d"], dims["ch"], dims["heads"], dims["alpha"]
    mods = [mod[:, k * d:(k + 1) * d][:, None, :] for k in range(6)]
    shift1, scale1, gate1, shift2, scale2, gate2 = mods
    u = _ln_mod_fwd(x, scale1, shift1, bsz, name=f"ln_mod1_{tag}")
    zm = _matmul(u, p["w_main"], "nn", BF16, bias=p["b_main"], name=f"in_main_{tag}")
    zf = _matmul(u, p["w_f"], "nn", F32, bias=p["b_f"], name=f"in_forget_{tag}")
    a3 = _conv_branch_fwd(zm, p["conv_a_w"], p["conv_a_b"], p["ln_conv_g"], p["ln_conv_b"], bsz, ch,
                          name=f"conv_branch_{tag}")
    ya = _matmul(a3, p["w_conv_proj"], "nn", F32, name=f"conv_proj_{tag}")
    cumt, cumb = _fgate_fwd(zf, bsz, heads, name=f"fgate_{tag}")
    qkvt = _to_features_major(zm, 2 * ch, heads * HEAD_DIM, 3, name=f"qkv_t_{tag}")
    ot, lse = _attn_fwd(zm, qkvt, cumt, cumb, bsz, heads, 2 * ch, name=f"attn_{tag}")
    yb = _matmul(ot, p["w_attn_proj"], "tn", F32, name=f"attn_proj_{tag}")
    m = _gate_merge_fwd(zm, ya, yb, dims["ga_off"], name=f"merge_{tag}")
    mix = _matmul(m, p["w_mix_out"], "nn", F32, bias=p["b_mix_out"], name=f"mix_out_{tag}")
    x1 = _ln_res_fwd(x, mix, gate1, p["ln1_g"], p["ln1_b"], alpha, bsz, name=f"ln_res1_{tag}")
    u2 = _ln_mod_fwd(x1, scale2, shift2, bsz, name=f"ln_mod2_{tag}")
    hp = _matmul(u2, p["w_ffn_up"], "nn", BF16, name=f"ffn_up_{tag}")
    f = _ffn_act_fwd(hp, p["ffn_conv_w"], p["ffn_conv_b"], bsz, dims["tcf"], name=f"ffn_act_{tag}")
    ffn = _matmul(f, p["w_ffn_down"], "nn", F32, name=f"ffn_down_{tag}")
    x2 = _ln_res_fwd(x1, ffn, gate2, p["ln2_g"], p["ln2_b"], alpha, bsz, name=f"ln_res2_{tag}")
    saved = dict(x=x, mods=mods, u=u, zm=zm, zf=zf, a3=a3, ya=ya, yb=yb, cumt=cumt, cumb=cumb,
                 qkvt=qkvt, ot=ot, lse=lse, m=m, mix=mix, x1=x1, u2=u2, hp=hp, f=f, ffn=ffn)
    return x2, saved


def _layer_bwd(dx2, p, sv, dims, tag):
    bsz, ch, heads, alpha = dims["bsz"], dims["ch"], dims["heads"], dims["alpha"]
    f_off, tcf = dims["f_off"], dims["tcf"]
    shift1, scale1, gate1, shift2, scale2, gate2 = sv["mods"]
    g = {}
    dr2, dffn, dgate2, g["ln2_g"], g["ln2_b"], _ = _ln_res_bwd(
        dx2, sv["x1"], sv["ffn"], gate2, p["ln2_g"], alpha, bsz, name=f"ln_res2_bwd_{tag}")
    df = _matmul(dffn, p["w_ffn_down"], "nt", F32, name=f"ffn_down_dx_{tag}")
    g["w_ffn_down"] = _matmul(sv["f"], dffn, "tn", F32, name=f"ffn_down_dw_{tag}")
    dhg, dhl, dwg, dwl, dbg, dbl = _ffn_act_bwd(sv["hp"], df, p["ffn_conv_w"], p["ffn_conv_b"], bsz, tcf,
                                                name=f"ffn_act_bwd_{tag}")
    g["ffn_conv_w"] = jnp.concatenate([dwg, dwl], axis=1)
    g["ffn_conv_b"] = jnp.concatenate([dbg, dbl], axis=1)[0]
    du2 = _matmul(dhg, p["w_ffn_up_gate"], "nt", F32, name=f"ffn_up_gate_dx_{tag}")
    du2 = _matmul(dhl, p["w_ffn_up_lin"], "nt", F32, add=du2, name=f"ffn_up_lin_dx_{tag}")
    g["w_ffn_up"] = jnp.concatenate([_matmul(sv["u2"], dhg, "tn", F32, name=f"ffn_up_gate_dw_{tag}"),
                                     _matmul(sv["u2"], dhl, "tn", F32, name=f"ffn_up_lin_dw_{tag}")], axis=1)
    dx1, dscale2, dshift2 = _ln_mod_bwd(du2, sv["x1"], scale2, dr2, alpha, bsz, name=f"ln_mod2_bwd_{tag}")
    dr1, dmix, dgate1, g["ln1_g"], g["ln1_b"], g["b_mix_out"] = _ln_res_bwd(
        dx1, sv["x"], sv["mix"], gate1, p["ln1_g"], alpha, bsz, name=f"ln_res1_bwd_{tag}")
    dm = _matmul(dmix, p["w_mix_out"], "nt", F32, name=f"mix_out_dx_{tag}")
    g["w_mix_out"] = _matmul(sv["m"], dmix, "tn", F32, name=f"mix_out_dw_{tag}")
    dya, dyb, dzga, dzgb = _gate_merge_bwd(sv["zm"], sv["ya"], sv["yb"], dm, dims["ga_off"], name=f"merge_bwd_{tag}")
    da3 = _matmul(dya, p["w_conv_proj"], "nt", F32, name=f"conv_proj_dx_{tag}")
    g["w_conv_proj"] = _matmul(sv["a3"], dya, "tn", F32, name=f"conv_proj_dw_{tag}")
    do = _matmul(dyb, p["w_attn_proj"], "nt", BF16, name=f"attn_proj_dx_{tag}")
    dot = _matmul(p["w_attn_proj"], dyb, "nt", BF16, name=f"attn_proj_dxt_{tag}")
    g["w_attn_proj"] = _matmul(sv["ot"], dyb, "nn", F32, name=f"attn_proj_dw_{tag}")
    dzglu, g["conv_a_w"], dcb, g["ln_conv_g"], g["ln_conv_b"] = _conv_branch_bwd(
        sv["zm"], da3, p["conv_a_w"], p["conv_a_b"], p["ln_conv_g"], p["ln_conv_b"], bsz, ch,
        name=f"conv_branch_bwd_{tag}")
    g["conv_a_b"] = dcb[0]
    dk, dv, dqt, dcum = _attn_bwd(sv["zm"], sv["qkvt"], sv["cumt"], sv["cumb"], sv["ot"], do, dot, sv["lse"], bsz,
                                  heads, 2 * ch, name=f"attn_bwd_{tag}")
    dq = _to_rows_major(dqt, name=f"dq_rows_{tag}")
    dzf = _fgate_bwd(dcum, sv["zf"], bsz, name=f"fgate_bwd_{tag}")
    dzm = jnp.concatenate([dzglu, dq, dk, dv, dzga, dzgb], axis=1)
    du = _matmul(dzm, p["w_main"], "nt", F32, name=f"in_main_dx_{tag}")
    du = _matmul(dzf, p["w_f"], "nt", F32, add=du, name=f"in_forget_dx_{tag}")
    dwm, dbm = _matmul(sv["u"], dzm, "tn", F32, colsum=True, name=f"in_main_dw_{tag}")
    dwf, dbf = _matmul(sv["u"], dzf, "tn", F32, colsum=True, name=f"in_forget_dw_{tag}")
    dbm, dbf = dbm[0], dbf[0]
    g["w_in"] = jnp.concatenate([dwm[:, :f_off], dwf[:, :heads], dwm[:, f_off:]], axis=1)
    g["b_in"] = jnp.concatenate([dbm[:f_off], dbf[:heads], dbm[f_off:]])
    dx, dscale1, dshift1 = _ln_mod_bwd(du, sv["x"], scale1, dr1, alpha, bsz, name=f"ln_mod1_bwd_{tag}")
    dmod = jnp.concatenate([dshift1, dscale1, dgate1, dshift2, dscale2, dgate2], axis=2)[:, 0, :]
    return dx, g, dmod


def _local_step(x, mod, loss_target, full, rep, dims):
    bsz, seq, d = x.shape
    layers = mod.shape[0]
    params = [_layer_weights(full, rep, layer, dims) for layer in range(layers)]
    h = x.reshape(bsz * seq, d)
    saved = []
    for layer in range(layers):
        h, sv = _layer_fwd(h, mod[layer], params[layer], dims, f"l{layer}")
        saved.append(sv)
    dh, sq = _loss_head(h, loss_target.reshape(bsz * seq, d), name="loss_head")
    loss_local = 0.5 * jnp.sum(sq) / d
    grads, dmods = [None] * layers, [None] * layers
    for layer in reversed(range(layers)):
        dh, grads[layer], dmods[layer] = _layer_bwd(dh, params[layer], saved[layer], dims, f"l{layer}")
    stacked = {wname: jnp.stack([grads[layer][wname] for layer in range(layers)]) for wname in grads[0]}
    return loss_local, dh.reshape(bsz, seq, d), stacked, jnp.stack(dmods)


def _pad_rows(a):
    extra = -a.shape[-2] % (2 * SUBLANES)
    if extra == 0:
        return a
    return jnp.pad(a, [(0, 0)] * (a.ndim - 2) + [(0, extra), (0, 0)])


def _to_slab(g):
    layers, k, n4 = g.shape
    return jnp.transpose(g.reshape(layers, k, N_CHIPS, n4 // N_CHIPS), (0, 2, 1, 3))


def _from_slab(w):
    layers, _, k, n = w.shape
    return jnp.transpose(w, (0, 2, 1, 3)).reshape(layers, k, N_CHIPS * n)


def kernel(x, c, w_ada, b_ada, w_in, b_in, conv_a_w, conv_a_b, ln_conv_g, ln_conv_b, w_conv_proj, w_attn_proj, w_mix_out, b_mix_out, ln1_g, ln1_b, w_ffn_up, ffn_conv_w, ffn_conv_b, w_ffn_down, ln2_g, ln2_b, loss_target, m_w_ada, m_b_ada, m_w_in, m_b_in, m_conv_a_w, m_conv_a_b, m_ln_conv_g, m_ln_conv_b, m_w_conv_proj, m_w_attn_proj, m_w_mix_out, m_b_mix_out, m_ln1_g, m_ln1_b, m_w_ffn_up, m_ffn_conv_w, m_ffn_conv_b, m_w_ffn_down, m_ln2_g, m_ln2_b, v_w_ada, v_b_ada, v_w_in, v_b_in, v_conv_a_w, v_conv_a_b, v_ln_conv_g, v_ln_conv_b, v_w_conv_proj, v_w_attn_proj, v_w_mix_out, v_b_mix_out, v_ln1_g, v_ln1_b, v_w_ffn_up, v_ffn_conv_w, v_ffn_conv_b, v_w_ffn_down, v_ln2_g, v_ln2_b):
    weights = dict(zip(WEIGHTS, (w_ada, b_ada, w_in, b_in, conv_a_w, conv_a_b, ln_conv_g, ln_conv_b, w_conv_proj,
                                 w_attn_proj, w_mix_out, b_mix_out, ln1_g, ln1_b, w_ffn_up, ffn_conv_w, ffn_conv_b,
                                 w_ffn_down, ln2_g, ln2_b)))
    mom1 = dict(zip(WEIGHTS, (m_w_ada, m_b_ada, m_w_in, m_b_in, m_conv_a_w, m_conv_a_b, m_ln_conv_g, m_ln_conv_b,
                              m_w_conv_proj, m_w_attn_proj, m_w_mix_out, m_b_mix_out, m_ln1_g, m_ln1_b, m_w_ffn_up,
                              m_ffn_conv_w, m_ffn_conv_b, m_w_ffn_down, m_ln2_g, m_ln2_b)))
    mom2 = dict(zip(WEIGHTS, (v_w_ada, v_b_ada, v_w_in, v_b_in, v_conv_a_w, v_conv_a_b, v_ln_conv_g, v_ln_conv_b,
                              v_w_conv_proj, v_w_attn_proj, v_w_mix_out, v_b_mix_out, v_ln1_g, v_ln1_b, v_w_ffn_up,
                              v_ffn_conv_w, v_ffn_conv_b, v_w_ffn_down, v_ln2_g, v_ln2_b)))
    bsz, seq, d = x.shape
    layers = w_ada.shape[0]
    ch = conv_a_w.shape[2] * N_CHIPS
    width = w_attn_proj.shape[1]
    heads = width // HEAD_DIM
    d_ff = w_ffn_down.shape[1] * N_CHIPS
    dims = dict(bsz=bsz, d=d, ch=ch, heads=heads, alpha=(2.0 * layers) ** 0.25, f_off=2 * ch + 3 * width,
                ga_off=2 * ch + 3 * width, tcf=_pick(d_ff, (256, 128)))
    chip = 2 * lax.axis_index("x") + lax.axis_index("y")
    device = 2 * chip + lax.axis_index("c")
    ada_cols = w_ada.shape[2]

    c_act = _silu_rows(_all_gather8(c, name="gather_c").reshape(N_DEVICES * bsz, d), name="silu_c")
    b_ada_mine = lax.dynamic_slice_in_dim(b_ada, chip * ada_cols, ada_cols, axis=1)
    mod_cols = jnp.stack([_matmul(c_act, w_ada[layer], "nn", F32, bias=b_ada_mine[layer][None], name=f"ada_l{layer}")
                          for layer in range(layers)])
    mod_all = _all_gather8(mod_cols, name="gather_mod")
    mod_all = jnp.concatenate([mod_all[2 * k] for k in range(N_CHIPS)], axis=-1)
    mod = lax.dynamic_slice_in_dim(mod_all, device * bsz, bsz, axis=1)

    shards = [_pad_rows(weights[wname].astype(BF16) if as_bf16 else weights[wname]) for wname, _, as_bf16 in GATHERED]
    modes = [mode for _, mode, _ in GATHERED]
    whole = _gather_weights(shards, modes, name="gather_weights")
    full = {wname: w[:, :weights[wname].shape[1]] if mode == "cols" else w
            for (wname, mode, _), w in zip(GATHERED, whole)}
    full["w_in"] = _from_slab(full["w_in"])
    rep = {wname: weights[wname] for wname in REPLICATED}

    loss_local, grad_x, grads, dmod = _local_step(x, mod, loss_target, full, rep, dims)
    loss = lax.psum(loss_local, ("x", "y", "c"))

    grads["w_in"] = _to_slab(grads["w_in"])
    shard_shapes = [s.shape for s in shards]
    reduced = _reduce_scatter([_pad_rows(grads[wname]) for wname, _, _ in GATHERED], modes, shard_shapes)
    grad = {wname: r[:, :weights[wname].shape[1]] for (wname, _, _), r in zip(GATHERED, reduced)}

    small = jnp.concatenate([dmod.reshape(-1)] + [grads[wname].reshape(-1) for wname in REPLICATED])
    n_small = small.shape[0]
    rows = -(-n_small // (SUBLANES * LANES)) * SUBLANES
    small = jnp.pad(small, (0, rows * LANES - n_small)).reshape(rows, LANES)
    gathered = _all_gather8(small, name="gather_small")
    n_dmod = dmod.size
    dmod_all = gathered.reshape(N_DEVICES, -1)[:, :n_dmod].reshape(N_DEVICES, layers, bsz, 6 * d)
    dmod_all = jnp.transpose(dmod_all, (1, 0, 2, 3)).reshape(layers, N_DEVICES * bsz, 6 * d)
    summed = _sum_slots(gathered, name="sum_small").reshape(-1)
    off = n_dmod
    for wname in REPLICATED:
        n = weights[wname].size
        grad[wname] = summed[off:off + n].reshape(weights[wname].shape)
        off += n
    dmod_mine = lax.dynamic_slice_in_dim(dmod_all, chip * ada_cols, ada_cols, axis=2)
    grad["w_ada"] = jnp.stack([_matmul(c_act, dmod_mine[layer], "tn", F32, name=f"ada_dw_l{layer}")
                               for layer in range(layers)])
    grad["b_ada"] = jnp.stack([_colsum(dmod_all[layer], name=f"ada_db_l{layer}")[0] for layer in range(layers)])

    delta, new_m, new_v = {}, {}, {}
    for wname in WEIGHTS:
        delta[wname], new_m[wname], new_v[wname] = _adamw(weights[wname], grad[wname], mom1[wname], mom2[wname],
                                                          name=f"adamw_{wname}")
    return (loss, grad_x, *[grad[wname] for wname in WEIGHTS], *[delta[wname] for wname in WEIGHTS],
            *[new_m[wname] for wname in WEIGHTS], *[new_v[wname] for wname in WEIGHTS])
```

```python
import math

import jax
import jax.numpy as jnp
from jax import lax
from jax.experimental import pallas as pl
from jax.experimental.pallas import tpu as pltpu

F32 = jnp.float32
BF16 = jnp.bfloat16
MESH = pl.DeviceIdType.MESH

LN_EPS = 1e-5
HEAD_DIM = 64
ATTN_SCALE = HEAD_DIM ** -0.5
NEG = -1e30
FFN_PAD = 8
LANES = 128
SUBLANES = 8
ROW_CHUNK = 256
ATTN_BLOCK = 256
N_CHIPS = 4
N_DEVICES = 8
VMEM_LIMIT = 56 * 1024 * 1024

ADAM_LR = 0.001
ADAM_B1 = 0.9
ADAM_B2 = 0.999
ADAM_EPS = 1e-08
ADAM_WD = 0.01
ADAM_STEP = 10

GATHERED = (("w_in", "slab", True), ("conv_a_w", "cols", False), ("w_conv_proj", "cols", True),
            ("w_attn_proj", "cols", True), ("w_mix_out", "rows", True), ("w_ffn_up", "cols", True),
            ("ffn_conv_w", "cols", False), ("w_ffn_down", "rows", True))
REPLICATED = ("b_in", "conv_a_b", "ln_conv_g", "ln_conv_b", "b_mix_out", "ln1_g", "ln1_b",
              "ffn_conv_b", "ln2_g", "ln2_b")
WEIGHTS = ("w_ada", "b_ada", "w_in", "b_in", "conv_a_w", "conv_a_b", "ln_conv_g", "ln_conv_b",
           "w_conv_proj", "w_attn_proj", "w_mix_out", "b_mix_out", "ln1_g", "ln1_b", "w_ffn_up",
           "ffn_conv_w", "ffn_conv_b", "w_ffn_down", "ln2_g", "ln2_b")


def _pick(n, cands):
    for cand in cands:
        if n % cand == 0:
            return cand
    return n


def _call(body, *, name, grid, in_specs, out_specs, out_shape, scratch=(), sem=None):
    return pl.pallas_call(
        body, name=name, grid=grid, in_specs=in_specs, out_specs=out_specs, out_shape=out_shape,
        scratch_shapes=list(scratch),
        compiler_params=pltpu.CompilerParams(dimension_semantics=sem, vmem_limit_bytes=VMEM_LIMIT))


def _sds(shape, dtype):
    return jax.ShapeDtypeStruct(tuple(shape), dtype)


def _chunked(rows, fn):
    chunk = min(ROW_CHUNK, rows)
    if rows == chunk:
        fn(pl.ds(0, rows))
        return

    def step(i, carry):
        fn(pl.ds(pl.multiple_of(i * chunk, chunk), chunk))
        return carry

    lax.fori_loop(0, rows // chunk, step, 0)


def _matmul(a, b, mode, out_dtype, *, bias=None, add=None, colsum=False, name):
    if mode == "nn":
        (m, k), (_, n) = a.shape, b.shape
    elif mode == "nt":
        (m, k), (n, _) = a.shape, b.shape
    else:
        (k, m), (_, n) = a.shape, b.shape
    tm = _pick(m, (1024, 1408, 512, 256, 128))
    tn = _pick(n, (1536, 1408, 1024, 512, 256, 128))
    tk = k if k <= 1536 else _pick(k, (1024, 1536, 1408, 512, 256, 128))
    nk = k // tk
    if mode == "nn":
        a_spec = pl.BlockSpec((tm, tk), lambda i, j, kk: (i, kk))
        b_spec = pl.BlockSpec((tk, tn), lambda i, j, kk: (kk, j))
        dims = (((1,), (0,)), ((), ()))
    elif mode == "nt":
        a_spec = pl.BlockSpec((tm, tk), lambda i, j, kk: (i, kk))
        b_spec = pl.BlockSpec((tn, tk), lambda i, j, kk: (j, kk))
        dims = (((1,), (1,)), ((), ()))
    else:
        a_spec = pl.BlockSpec((tk, tm), lambda i, j, kk: (kk, i))
        b_spec = pl.BlockSpec((tk, tn), lambda i, j, kk: (kk, j))
        dims = (((0,), (0,)), ((), ()))
    in_specs = [a_spec, b_spec]
    operands = [a, b]
    if bias is not None:
        in_specs.append(pl.BlockSpec((1, tn), lambda i, j, kk: (0, j)))
        operands.append(bias)
    if add is not None:
        in_specs.append(pl.BlockSpec((tm, tn), lambda i, j, kk: (i, j)))
        operands.append(add)

    def body(a_ref, b_ref, *rest):
        rest = list(rest)
        bias_ref = rest.pop(0) if bias is not None else None
        add_ref = rest.pop(0) if add is not None else None
        o_ref = rest.pop(0)
        prod = lax.dot_general(a_ref[...].astype(BF16), b_ref[...].astype(BF16), dims,
                               preferred_element_type=F32)
        if colsum:
            cs_ref = rest.pop(0)
            part = jnp.sum(b_ref[...].astype(F32), axis=0, keepdims=True)

            @pl.when(pl.program_id(2) == 0)
            def _():
                cs_ref[...] = part

            @pl.when(pl.program_id(2) > 0)
            def _():
                cs_ref[...] += part

        def finish(r):
            if bias_ref is not None:
                r = r + bias_ref[...]
            if add_ref is not None:
                r = r + add_ref[...]
            o_ref[...] = r.astype(o_ref.dtype)

        if nk == 1:
            finish(prod)
            return
        acc_ref = rest.pop(0)
        kk = pl.program_id(2)

        @pl.when(kk == 0)
        def _():
            acc_ref[...] = prod

        @pl.when(kk > 0)
        def _():
            acc_ref[...] += prod

        @pl.when(kk == nk - 1)
        def _():
            finish(acc_ref[...])

    out_specs = pl.BlockSpec((tm, tn), lambda i, j, kk: (i, j))
    out_shape = _sds((m, n), out_dtype)
    if colsum:
        assert mode == "tn" and m == tm
        out_specs = [out_specs, pl.BlockSpec((1, tn), lambda i, j, kk: (0, j))]
        out_shape = [out_shape, _sds((1, n), F32)]
    return _call(body, name=name, grid=(m // tm, n // tn, nk), in_specs=in_specs, out_specs=out_specs,
                 out_shape=out_shape, scratch=[pltpu.VMEM((tm, tn), F32)] if nk > 1 else [],
                 sem=("parallel", "parallel", "arbitrary"))(*operands)


def _colsum(x, *, name):
    rows, n = x.shape
    tr = _pick(rows, (1024, 512, 256, 128))
    tn = _pick(n, (512, 256, 128))

    def body(x_ref, o_ref):
        @pl.when(pl.program_id(1) == 0)
        def _():
            o_ref[...] = jnp.zeros_like(o_ref)

        o_ref[...] += jnp.sum(x_ref[...].astype(F32), axis=0, keepdims=True)

    return _call(body, name=name, grid=(n // tn, rows // tr),
                 in_specs=[pl.BlockSpec((tr, tn), lambda j, i: (i, j))],
                 out_specs=pl.BlockSpec((1, tn), lambda j, i: (0, j)),
                 out_shape=_sds((1, n), F32), sem=("parallel", "arbitrary"))(x)


def _ln_stats(x):
    mu = jnp.mean(x, axis=-1, keepdims=True)
    xc = x - mu
    var = jnp.mean(xc * xc, axis=-1, keepdims=True)
    rstd = lax.rsqrt(var + LN_EPS)
    return xc * rstd, rstd


def _ln_bwd(dn, n, rstd):
    return rstd * (dn - jnp.mean(dn, axis=-1, keepdims=True) - n * jnp.mean(dn * n, axis=-1, keepdims=True))


def _seq_tiles(t, bsz, cands=(1024, 512, 256, 128, 64, 32, 16, 8)):
    s = t // bsz
    ts = _pick(s, cands)
    return s, ts, s // ts


def _ln_mod_fwd(x, scale, shift, bsz, *, name):
    t, d = x.shape
    _, ts, ns = _seq_tiles(t, bsz)

    def body(x_ref, sc_ref, sh_ref, u_ref):
        one_scale = 1.0 + sc_ref[0]
        shift_v = sh_ref[0]

        def piece(rows):
            n, _ = _ln_stats(x_ref[rows, :])
            u_ref[rows, :] = (n * one_scale + shift_v).astype(u_ref.dtype)

        _chunked(ts, piece)

    row = pl.BlockSpec((ts, d), lambda b, i: (b * ns + i, 0))
    per = pl.BlockSpec((1, 1, d), lambda b, i: (b, 0, 0))
    return _call(body, name=name, grid=(bsz, ns), in_specs=[row, per, per], out_specs=row,
                 out_shape=_sds((t, d), BF16), sem=("parallel", "parallel"))(x, scale, shift)


def _ln_mod_bwd(du, x, scale, dr, alpha, bsz, *, name):
    t, d = x.shape
    _, ts, ns = _seq_tiles(t, bsz)

    def body(du_ref, x_ref, sc_ref, dr_ref, dx_ref, dsc_ref, dsh_ref):
        @pl.when(pl.program_id(1) == 0)
        def _():
            dsc_ref[...] = jnp.zeros_like(dsc_ref)
            dsh_ref[...] = jnp.zeros_like(dsh_ref)

        one_scale = 1.0 + sc_ref[0]

        def piece(rows):
            du_v = du_ref[rows, :]
            n, rstd = _ln_stats(x_ref[rows, :])
            dsc_ref[0] += jnp.sum(du_v * n, axis=0, keepdims=True)
            dsh_ref[0] += jnp.sum(du_v, axis=0, keepdims=True)
            dx_ref[rows, :] = alpha * dr_ref[rows, :] + _ln_bwd(du_v * one_scale, n, rstd)

        _chunked(ts, piece)

    row = pl.BlockSpec((ts, d), lambda b, i: (b * ns + i, 0))
    per = pl.BlockSpec((1, 1, d), lambda b, i: (b, 0, 0))
    return _call(body, name=name, grid=(bsz, ns), in_specs=[row, row, per, row],
                 out_specs=[row, per, per],
                 out_shape=[_sds((t, d), F32), _sds((bsz, 1, d), F32), _sds((bsz, 1, d), F32)],
                 sem=("parallel", "arbitrary"))(du, x, scale, dr)


def _ln_res_fwd(x, y, gate, g, b, alpha, bsz, *, name):
    t, d = x.shape
    _, ts, ns = _seq_tiles(t, bsz)

    def body(x_ref, y_ref, gt_ref, g_ref, b_ref, o_ref):
        one_gate = 1.0 + gt_ref[0]

        def piece(rows):
            n, _ = _ln_stats(alpha * x_ref[rows, :] + one_gate * y_ref[rows, :])
            o_ref[rows, :] = n * g_ref[...] + b_ref[...]

        _chunked(ts, piece)

    row = pl.BlockSpec((ts, d), lambda bb, i: (bb * ns + i, 0))
    per = pl.BlockSpec((1, 1, d), lambda bb, i: (bb, 0, 0))
    vec = pl.BlockSpec((1, d), lambda bb, i: (0, 0))
    return _call(body, name=name, grid=(bsz, ns), in_specs=[row, row, per, vec, vec], out_specs=row,
                 out_shape=_sds((t, d), F32), sem=("parallel", "parallel"))(x, y, gate, g, b)


def _ln_res_bwd(do, x, y, gate, g, alpha, bsz, *, name):
    t, d = x.shape
    _, ts, ns = _seq_tiles(t, bsz)

    def body(do_ref, x_ref, y_ref, gt_ref, g_ref, dr_ref, dy_ref, dgt_ref, dg_ref, db_ref, dys_ref):
        first_tile = pl.program_id(1) == 0

        @pl.when(first_tile)
        def _():
            dgt_ref[...] = jnp.zeros_like(dgt_ref)

        @pl.when(jnp.logical_and(first_tile, pl.program_id(0) == 0))
        def _():
            dg_ref[...] = jnp.zeros_like(dg_ref)
            db_ref[...] = jnp.zeros_like(db_ref)
            dys_ref[...] = jnp.zeros_like(dys_ref)

        one_gate = 1.0 + gt_ref[0]

        def piece(rows):
            do_v = do_ref[rows, :]
            y_v = y_ref[rows, :]
            n, rstd = _ln_stats(alpha * x_ref[rows, :] + one_gate * y_v)
            dg_ref[...] += jnp.sum(do_v * n, axis=0, keepdims=True)
            db_ref[...] += jnp.sum(do_v, axis=0, keepdims=True)
            dr = _ln_bwd(do_v * g_ref[...], n, rstd)
            dr_ref[rows, :] = dr
            dy = one_gate * dr
            dy_ref[rows, :] = dy.astype(dy_ref.dtype)
            dys_ref[...] += jnp.sum(dy, axis=0, keepdims=True)
            dgt_ref[0] += jnp.sum(dr * y_v, axis=0, keepdims=True)

        _chunked(ts, piece)

    row = pl.BlockSpec((ts, d), lambda bb, i: (bb * ns + i, 0))
    per = pl.BlockSpec((1, 1, d), lambda bb, i: (bb, 0, 0))
    vec = pl.BlockSpec((1, d), lambda bb, i: (0, 0))
    return _call(body, name=name, grid=(bsz, ns), in_specs=[row, row, row, per, vec],
                 out_specs=[row, row, per, vec, vec, vec],
                 out_shape=[_sds((t, d), F32), _sds((t, d), BF16), _sds((bsz, 1, d), F32),
                            _sds((1, d), F32), _sds((1, d), F32), _sds((1, d), F32)],
                 sem=("arbitrary", "arbitrary"))(do, x, y, gate, g)


def _loss_head(y, target, *, name):
    t, d = y.shape
    tr = _pick(t, (1024, 512, 256, 128, 64, 32, 16, 8))

    def body(y_ref, t_ref, dy_ref, s_ref):
        @pl.when(pl.program_id(0) == 0)
        def _():
            s_ref[...] = jnp.zeros_like(s_ref)

        def piece(rows):
            e = y_ref[rows, :] - t_ref[rows, :]
            dy_ref[rows, :] = e * (1.0 / d)
            s_ref[...] += jnp.sum(e * e, axis=0, keepdims=True)

        _chunked(tr, piece)

    row = pl.BlockSpec((tr, d), lambda i: (i, 0))
    return _call(body, name=name, grid=(t // tr,), in_specs=[row, row],
                 out_specs=[row, pl.BlockSpec((1, d), lambda i: (0, 0))],
                 out_shape=[_sds((t, d), F32), _sds((1, d), F32)], sem=("arbitrary",))(y, target)


def _sigmoid(v):
    return 1.0 / (1.0 + jnp.exp(-v))


def _silu_rows(c, *, name):
    rows, d = c.shape

    def body(c_ref, o_ref):
        v = c_ref[...]
        o_ref[...] = (v * _sigmoid(v)).astype(o_ref.dtype)

    full = pl.BlockSpec((rows, d), lambda i: (0, 0))
    return _call(body, name=name, grid=(1,), in_specs=[full], out_specs=full,
                 out_shape=_sds((rows, d), BF16), sem=("arbitrary",))(c)


def _gate_cols(d, ga_off):
    tc = _pick(math.gcd(d, ga_off), (512, 256, 128))
    return tc, ga_off // tc, (ga_off + d) // tc


def _gate_merge_fwd(z, ya, yb, ga_off, *, name):
    t, d = ya.shape
    tr = _pick(t, (1024, 512, 256, 128, 64, 32, 16, 8))
    tc, ga_blk, gb_blk = _gate_cols(d, ga_off)

    def body(ga_ref, gb_ref, ya_ref, yb_ref, o_ref):
        def piece(rows):
            o_ref[rows, :] = (_sigmoid(ga_ref[rows, :].astype(F32)) * ya_ref[rows, :]
                              + _sigmoid(gb_ref[rows, :].astype(F32)) * yb_ref[rows, :]).astype(o_ref.dtype)

        _chunked(tr, piece)

    blk = pl.BlockSpec((tr, tc), lambda i, j: (i, j))
    return _call(body, name=name, grid=(t // tr, d // tc),
                 in_specs=[pl.BlockSpec((tr, tc), lambda i, j: (i, ga_blk + j)),
                           pl.BlockSpec((tr, tc), lambda i, j: (i, gb_blk + j)), blk, blk],
                 out_specs=blk, out_shape=_sds((t, d), BF16), sem=("parallel", "parallel"))(z, z, ya, yb)


def _gate_merge_bwd(z, ya, yb, dm, ga_off, *, name):
    t, d = ya.shape
    tr = _pick(t, (1024, 512, 256, 128, 64, 32, 16, 8))
    tc, ga_blk, gb_blk = _gate_cols(d, ga_off)

    def body(ga_ref, gb_ref, ya_ref, yb_ref, dm_ref, dya_ref, dyb_ref, dga_ref, dgb_ref):
        def piece(rows):
            dm_v = dm_ref[rows, :]
            sa = _sigmoid(ga_ref[rows, :].astype(F32))
            sb = _sigmoid(gb_ref[rows, :].astype(F32))
            dya_ref[rows, :] = (dm_v * sa).astype(dya_ref.dtype)
            dyb_ref[rows, :] = (dm_v * sb).astype(dyb_ref.dtype)
            dga_ref[rows, :] = (dm_v * ya_ref[rows, :] * sa * (1.0 - sa)).astype(dga_ref.dtype)
            dgb_ref[rows, :] = (dm_v * yb_ref[rows, :] * sb * (1.0 - sb)).astype(dgb_ref.dtype)

        _chunked(tr, piece)

    blk = pl.BlockSpec((tr, tc), lambda i, j: (i, j))
    return _call(body, name=name, grid=(t // tr, d // tc),
                 in_specs=[pl.BlockSpec((tr, tc), lambda i, j: (i, ga_blk + j)),
                           pl.BlockSpec((tr, tc), lambda i, j: (i, gb_blk + j)), blk, blk, blk],
                 out_specs=[blk, blk, blk, blk], out_shape=[_sds((t, d), BF16)] * 4,
                 sem=("parallel", "parallel"))(z, z, ya, yb, dm)


CONV_ROWS = 64
CONV_PAD = 32


def _row_shifts(win):
    total = win.shape[0]
    return [win] + [pltpu.roll(win, total - b, axis=0) for b in range(1, SUBLANES)]


def _shifted_rows(copies, shift):
    start = SUBLANES * (shift // SUBLANES)
    return copies[shift % SUBLANES][start:start + CONV_ROWS]


def _fill_glu(z_ref, ext_ref, s, ch):
    ext_ref[pl.ds(0, CONV_PAD), :] = jnp.zeros((CONV_PAD, ch), F32)

    chunk = min(ROW_CHUNK, s)

    def piece(i, carry):
        start = pl.multiple_of(i * chunk, chunk)
        zz = z_ref[pl.ds(start, chunk), :].astype(F32)
        ext_ref[pl.ds(pl.multiple_of(CONV_PAD + start, CONV_PAD), chunk), :] = zz[:, :ch] * _sigmoid(zz[:, ch:])
        return carry

    lax.fori_loop(0, s // chunk, piece, 0)


def _conv_piece(ext_ref, w_ref, cb_ref, base, kw):
    copies = _row_shifts(ext_ref[pl.ds(base, CONV_ROWS + CONV_PAD), :])
    acc = cb_ref[...] + w_ref[pl.ds(0, 1), :] * _shifted_rows(copies, CONV_PAD - (kw - 1))
    for k in range(1, kw):
        acc = acc + w_ref[pl.ds(k, 1), :] * _shifted_rows(copies, CONV_PAD - (kw - 1) + k)
    return acc, copies


def _conv_branch_fwd(z, w, cb, lg, lb, bsz, ch, *, name):
    t = z.shape[0]
    s = t // bsz
    kw = w.shape[0]

    def body(z_ref, w_ref, cb_ref, lg_ref, lb_ref, o_ref, ext_ref):
        _fill_glu(z_ref, ext_ref, s, ch)

        def step(i, carry):
            base = pl.multiple_of(i * CONV_ROWS, CONV_ROWS)
            a1, _ = _conv_piece(ext_ref, w_ref, cb_ref, base, kw)
            n, _ = _ln_stats(a1)
            a2 = n * lg_ref[...] + lb_ref[...]
            o_ref[pl.ds(base, CONV_ROWS), :] = (a2 * _sigmoid(a2)).astype(o_ref.dtype)
            return carry

        lax.fori_loop(0, s // CONV_ROWS, step, 0)

    vec = pl.BlockSpec((1, ch), lambda b: (0, 0))
    return _call(body, name=name, grid=(bsz,),
                 in_specs=[pl.BlockSpec((s, 2 * ch), lambda b: (b, 0)), pl.BlockSpec((kw, ch), lambda b: (0, 0)),
                           vec, vec, vec],
                 out_specs=pl.BlockSpec((s, ch), lambda b: (b, 0)), out_shape=_sds((t, ch), BF16),
                 scratch=[pltpu.VMEM((CONV_PAD + s, ch), F32)], sem=("parallel",))(z, w, cb, lg, lb)


def _conv_branch_bwd(z, da3, w, cb, lg, lb, bsz, ch, *, name):
    t = z.shape[0]
    s = t // bsz
    kw = w.shape[0]
    n_rows = CONV_ROWS + CONV_PAD

    def body(z_ref, d_ref, w_ref, cb_ref, lg_ref, lb_ref, dz_ref, dw_ref, dcb_ref, dlg_ref, dlb_ref,
             ext_ref, da1_ref):
        @pl.when(pl.program_id(0) == 0)
        def _():
            for ref in (dw_ref, dcb_ref, dlg_ref, dlb_ref):
                ref[...] = jnp.zeros_like(ref)

        _fill_glu(z_ref, ext_ref, s, ch)
        da1_ref[pl.ds(s, CONV_PAD), :] = jnp.zeros((CONV_PAD, ch), F32)

        def grad_a1(i, carry):
            dlg, dlb = carry
            base = pl.multiple_of(i * CONV_ROWS, CONV_ROWS)
            a1, _ = _conv_piece(ext_ref, w_ref, cb_ref, base, kw)
            n, rstd = _ln_stats(a1)
            a2 = n * lg_ref[...] + lb_ref[...]
            sg = _sigmoid(a2)
            da2 = d_ref[pl.ds(base, CONV_ROWS), :] * (sg * (1.0 + a2 * (1.0 - sg)))
            da1_ref[pl.ds(base, CONV_ROWS), :] = _ln_bwd(da2 * lg_ref[...], n, rstd)
            return (dlg + jnp.sum(da2 * n, axis=0, keepdims=True), dlb + jnp.sum(da2, axis=0, keepdims=True))

        zero = jnp.zeros((1, ch), F32)
        dlg, dlb = lax.fori_loop(0, s // CONV_ROWS, grad_a1, (zero, zero))
        dlg_ref[...] += dlg
        dlb_ref[...] += dlb

        def grad_z(i, dcb):
            base = pl.multiple_of(i * CONV_ROWS, CONV_ROWS)
            ahead = _row_shifts(da1_ref[pl.ds(base, n_rows), :])
            dyc = ahead[0][:CONV_ROWS]
            da0 = w_ref[pl.ds(kw - 1, 1), :] * dyc
            for k in range(kw - 1):
                da0 = da0 + w_ref[pl.ds(k, 1), :] * _shifted_rows(ahead, kw - 1 - k)
            behind = _row_shifts(ext_ref[pl.ds(base, n_rows), :])
            for k in range(kw):
                dw_ref[pl.ds(k, 1), :] += jnp.sum(dyc * _shifted_rows(behind, CONV_PAD - (kw - 1) + k),
                                                  axis=0, keepdims=True)
            zz = z_ref[pl.ds(base, CONV_ROWS), :].astype(F32)
            sg = _sigmoid(zz[:, ch:])
            dz_ref[pl.ds(base, CONV_ROWS), :ch] = (da0 * sg).astype(dz_ref.dtype)
            dz_ref[pl.ds(base, CONV_ROWS), ch:] = (da0 * zz[:, :ch] * sg * (1.0 - sg)).astype(dz_ref.dtype)
            return dcb + jnp.sum(dyc, axis=0, keepdims=True)

        dcb_ref[...] += lax.fori_loop(0, s // CONV_ROWS, grad_z, zero)

    vec = pl.BlockSpec((1, ch), lambda b: (0, 0))
    taps = pl.BlockSpec((kw, ch), lambda b: (0, 0))
    return _call(body, name=name, grid=(bsz,),
                 in_specs=[pl.BlockSpec((s, 2 * ch), lambda b: (b, 0)), pl.BlockSpec((s, ch), lambda b: (b, 0)),
                           taps, vec, vec, vec],
                 out_specs=[pl.BlockSpec((s, 2 * ch), lambda b: (b, 0)), taps, vec, vec, vec],
                 out_shape=[_sds((t, 2 * ch), BF16), _sds((kw, ch), F32)] + [_sds((1, ch), F32)] * 3,
                 scratch=[pltpu.VMEM((CONV_PAD + s, ch), F32), pltpu.VMEM((s + CONV_PAD, ch), F32)],
                 sem=("arbitrary",))(z, da3, w, cb, lg, lb)


FFN_ROWS = 64


def _gelu_parts(v):
    cdf = 0.5 * (1.0 + lax.erf(v * (2.0 ** -0.5)))
    return cdf, v * cdf


def _ffn_conv_piece(ext_ref, wb_ref, base):
    win = ext_ref[pl.ds(base, FFN_ROWS + FFN_PAD), :]
    acc = wb_ref[pl.ds(3, 1), :] + wb_ref[pl.ds(2, 1), :] * win[FFN_PAD:]
    acc = acc + wb_ref[pl.ds(1, 1), :] * pltpu.roll(win, 1, axis=0)[FFN_PAD:]
    acc = acc + wb_ref[pl.ds(0, 1), :] * pltpu.roll(win, 2, axis=0)[FFN_PAD:]
    return acc


def _ffn_stage(hg_ref, hl_ref, wg_ref, wl_ref, bg_ref, bl_ref, ext_ref, wb_ref, s, tcf):
    ext_ref[pl.ds(0, FFN_PAD), :] = jnp.zeros((FFN_PAD, 2 * tcf), F32)
    ext_ref[pl.ds(FFN_PAD, s), :tcf] = hg_ref[...].astype(F32)
    ext_ref[pl.ds(FFN_PAD, s), tcf:] = hl_ref[...].astype(F32)
    wb_ref[pl.ds(0, 3), :tcf] = wg_ref[...]
    wb_ref[pl.ds(0, 3), tcf:] = wl_ref[...]
    wb_ref[pl.ds(3, 1), :tcf] = bg_ref[...]
    wb_ref[pl.ds(3, 1), tcf:] = bl_ref[...]


def _ffn_specs(s, tcf, n_f, batch_first):
    def spec(rows, shift):
        if batch_first:
            return pl.BlockSpec((rows, tcf), lambda bb, j: (bb if rows == s else 0, shift + j))
        return pl.BlockSpec((rows, tcf), lambda j, bb: (bb if rows == s else 0, shift + j))

    return [spec(s, 0), spec(s, n_f), spec(3, 0), spec(3, n_f), spec(1, 0), spec(1, n_f)]


def _ffn_act_fwd(hp, w, b, bsz, tcf, *, name):
    t, two_f = hp.shape
    s = t // bsz
    n_f = two_f // (2 * tcf)

    def body(hg_ref, hl_ref, wg_ref, wl_ref, bg_ref, bl_ref, f_ref, ext_ref, wb_ref):
        _ffn_stage(hg_ref, hl_ref, wg_ref, wl_ref, bg_ref, bl_ref, ext_ref, wb_ref, s, tcf)

        def step(i, carry):
            base = pl.multiple_of(i * FFN_ROWS, FFN_ROWS)
            hh = _ffn_conv_piece(ext_ref, wb_ref, base)
            _, gelu = _gelu_parts(hh[:, :tcf])
            f_ref[pl.ds(base, FFN_ROWS), :] = (gelu * hh[:, tcf:]).astype(f_ref.dtype)
            return carry

        lax.fori_loop(0, s // FFN_ROWS, step, 0)

    return _call(body, name=name, grid=(bsz, n_f), in_specs=_ffn_specs(s, tcf, n_f, True),
                 out_specs=pl.BlockSpec((s, tcf), lambda bb, j: (bb, j)),
                 out_shape=_sds((t, two_f // 2), BF16),
                 scratch=[pltpu.VMEM((FFN_PAD + s, 2 * tcf), F32), pltpu.VMEM((SUBLANES, 2 * tcf), F32)],
                 sem=("parallel", "parallel"))(hp, hp, w, w, b, b)


def _ffn_act_bwd(hp, df, w, b, bsz, tcf, *, name):
    t, two_f = hp.shape
    s = t // bsz
    f_dim = two_f // 2
    n_f = f_dim // tcf
    gw = 2 * tcf
    n_rows = FFN_ROWS + FFN_PAD

    def body(hg_ref, hl_ref, wg_ref, wl_ref, bg_ref, bl_ref, df_ref,
             dhg_ref, dhl_ref, dwg_ref, dwl_ref, dbg_ref, dbl_ref, ext_ref, wb_ref, dh_ref):
        @pl.when(pl.program_id(1) == 0)
        def _():
            for ref in (dwg_ref, dwl_ref, dbg_ref, dbl_ref):
                ref[...] = jnp.zeros_like(ref)

        _ffn_stage(hg_ref, hl_ref, wg_ref, wl_ref, bg_ref, bl_ref, ext_ref, wb_ref, s, tcf)
        dh_ref[pl.ds(s, FFN_PAD), :] = jnp.zeros((FFN_PAD, gw), F32)

        def grad_h(i, carry):
            base = pl.multiple_of(i * FFN_ROWS, FFN_ROWS)
            hh = _ffn_conv_piece(ext_ref, wb_ref, base)
            hg = hh[:, :tcf]
            d = df_ref[pl.ds(base, FFN_ROWS), :]
            cdf, gelu = _gelu_parts(hg)
            pdf = jnp.exp(-0.5 * hg * hg) * (1.0 / math.sqrt(2.0 * math.pi))
            dh_ref[pl.ds(base, FFN_ROWS), :tcf] = d * hh[:, tcf:] * (cdf + hg * pdf)
            dh_ref[pl.ds(base, FFN_ROWS), tcf:] = d * gelu
            return carry

        lax.fori_loop(0, s // FFN_ROWS, grad_h, 0)

        def grad_x(i, carry):
            dw0, dw1, dw2, dbs = carry
            base = pl.multiple_of(i * FFN_ROWS, FFN_ROWS)
            nxt = dh_ref[pl.ds(base, n_rows), :]
            dyc = nxt[:FFN_ROWS]
            dx = wb_ref[pl.ds(2, 1), :] * dyc
            dx = dx + wb_ref[pl.ds(1, 1), :] * pltpu.roll(nxt, n_rows - 1, axis=0)[:FFN_ROWS]
            dx = dx + wb_ref[pl.ds(0, 1), :] * pltpu.roll(nxt, n_rows - 2, axis=0)[:FFN_ROWS]
            dhg_ref[pl.ds(base, FFN_ROWS), :] = dx[:, :tcf].astype(dhg_ref.dtype)
            dhl_ref[pl.ds(base, FFN_ROWS), :] = dx[:, tcf:].astype(dhl_ref.dtype)
            win = ext_ref[pl.ds(base, n_rows), :]
            dw2 = dw2 + jnp.sum(dyc * win[FFN_PAD:], axis=0, keepdims=True)
            dw1 = dw1 + jnp.sum(dyc * pltpu.roll(win, 1, axis=0)[FFN_PAD:], axis=0, keepdims=True)
            dw0 = dw0 + jnp.sum(dyc * pltpu.roll(win, 2, axis=0)[FFN_PAD:], axis=0, keepdims=True)
            return dw0, dw1, dw2, dbs + jnp.sum(dyc, axis=0, keepdims=True)

        zero = jnp.zeros((1, gw), F32)
        sums = lax.fori_loop(0, s // FFN_ROWS, grad_x, (zero, zero, zero, zero))
        for k in range(3):
            dwg_ref[pl.ds(k, 1), :] += sums[k][:, :tcf]
            dwl_ref[pl.ds(k, 1), :] += sums[k][:, tcf:]
        dbg_ref[...] += sums[3][:, :tcf]
        dbl_ref[...] += sums[3][:, tcf:]

    half = pl.BlockSpec((s, tcf), lambda j, bb: (bb, j))
    taps = pl.BlockSpec((3, tcf), lambda j, bb: (0, j))
    bias = pl.BlockSpec((1, tcf), lambda j, bb: (0, j))
    return _call(body, name=name, grid=(n_f, bsz), in_specs=_ffn_specs(s, tcf, n_f, False) + [half],
                 out_specs=[half, half, taps, taps, bias, bias],
                 out_shape=[_sds((t, f_dim), BF16)] * 2 + [_sds((3, f_dim), F32)] * 2 + [_sds((1, f_dim), F32)] * 2,
                 scratch=[pltpu.VMEM((FFN_PAD + s, gw), F32), pltpu.VMEM((SUBLANES, gw), F32),
                          pltpu.VMEM((s + FFN_PAD, gw), F32)],
                 sem=("parallel", "arbitrary"))(hp, hp, w, w, b, b, df)


def _split3(v):
    hi = v.astype(BF16)
    r = v - hi.astype(F32)
    mid = r.astype(BF16)
    lo = (r - mid.astype(F32)).astype(BF16)
    return hi, mid, lo


def _tri_dot(tri, v):
    out = None
    for part in _split3(v):
        term = jnp.dot(tri, part, preferred_element_type=F32)
        out = term if out is None else out + term
    return out


def _fgate_fwd(zf, bsz, heads, *, name):
    t, lanes = zf.shape
    s, blk, nb = _seq_tiles(t, bsz, (ATTN_BLOCK, 128))

    def body(z_ref, cumt_ref, cumb_ref, carry_ref):
        @pl.when(pl.program_id(1) == 0)
        def _():
            carry_ref[...] = jnp.zeros_like(carry_ref)

        z = z_ref[...]
        lf = jnp.minimum(z, 0.0) - jnp.log1p(jnp.exp(-jnp.abs(z)))
        r = lax.broadcasted_iota(jnp.int32, (blk, blk), 0)
        c = lax.broadcasted_iota(jnp.int32, (blk, blk), 1)
        tri = (r >= c).astype(BF16)
        cum = _tri_dot(tri, lf) + carry_ref[...]
        carry_ref[...] = cum[blk - 1:blk, :]
        cumt_ref[0] = jnp.transpose(cum)[:heads, :]
        for h in range(heads):
            cumb_ref[0, h] = jnp.broadcast_to(cum[:, h:h + 1], (blk, lanes))

    return _call(body, name=name, grid=(bsz, nb),
                 in_specs=[pl.BlockSpec((blk, lanes), lambda b, i: (b * nb + i, 0))],
                 out_specs=[pl.BlockSpec((1, heads, blk), lambda b, i: (b, 0, i)),
                            pl.BlockSpec((1, heads, blk, lanes), lambda b, i: (b, 0, i, 0))],
                 out_shape=[_sds((bsz, heads, s), F32), _sds((bsz, heads, s, lanes), F32)],
                 scratch=[pltpu.VMEM((1, lanes), F32)], sem=("parallel", "arbitrary"))(zf)


def _fgate_bwd(dcum, zf, bsz, *, name):
    t, lanes = zf.shape
    pairs = dcum.shape[1]
    s, blk, nb = _seq_tiles(t, bsz, (ATTN_BLOCK, 128))

    def body(d_ref, z_ref, o_ref, carry_ref):
        @pl.when(pl.program_id(1) == 0)
        def _():
            carry_ref[...] = jnp.zeros_like(carry_ref)

        dcol = d_ref[0, 0]
        for p in range(1, pairs):
            dcol = dcol + d_ref[0, p]
        r = lax.broadcasted_iota(jnp.int32, (blk, blk), 0)
        c = lax.broadcasted_iota(jnp.int32, (blk, blk), 1)
        tri = (c >= r).astype(BF16)
        suf = _tri_dot(tri, dcol) + carry_ref[...]
        carry_ref[...] = suf[0:1, :]
        o_ref[...] = suf * _sigmoid(-z_ref[...])

    return _call(body, name=name, grid=(bsz, nb),
                 in_specs=[pl.BlockSpec((1, pairs, blk, lanes), lambda b, i: (b, 0, nb - 1 - i, 0)),
                           pl.BlockSpec((blk, lanes), lambda b, i: (b * nb + nb - 1 - i, 0))],
                 out_specs=pl.BlockSpec((blk, lanes), lambda b, i: (b * nb + nb - 1 - i, 0)),
                 out_shape=_sds((t, lanes), F32), scratch=[pltpu.VMEM((1, lanes), F32)],
                 sem=("parallel", "arbitrary"))(dcum, zf)


def _to_features_major(z, col_off, width, n, *, name):
    t = z.shape[0]
    tr = _pick(t, (512, 256, 128))
    first = col_off // width

    def body(*refs):
        o_ref = refs[n]
        for g in range(n):
            o_ref[pl.ds(g * width, width), :] = jnp.transpose(refs[g][...].astype(F32)).astype(o_ref.dtype)

    return _call(body, name=name, grid=(t // tr,),
                 in_specs=[pl.BlockSpec((tr, width), lambda i, g=g: (i, first + g)) for g in range(n)],
                 out_specs=pl.BlockSpec((n * width, tr), lambda i: (0, i)),
                 out_shape=_sds((n * width, t), BF16), sem=("parallel",))(*([z] * n))


def _to_rows_major(xt, *, name):
    w, t = xt.shape
    tr = _pick(t, (512, 256, 128))

    def body(x_ref, o_ref):
        o_ref[...] = jnp.transpose(x_ref[...]).astype(o_ref.dtype)

    return _call(body, name=name, grid=(t // tr,),
                 in_specs=[pl.BlockSpec((w, tr), lambda i: (0, i))],
                 out_specs=pl.BlockSpec((tr, w), lambda i: (i, 0)),
                 out_shape=_sds((t, w), BF16), sem=("parallel",))(xt)


def _loop_by_twos(lo, hi, body, carry):
    count = hi - lo

    def two(t, cr):
        i = lo + 2 * t
        return body(i + 1, body(i, cr))

    carry = lax.fori_loop(0, count // 2, two, carry)
    return lax.cond(count % 2 == 1, lambda cr: body(hi - 1, cr), lambda cr: cr, carry)


def _head_masks(shape, axis):
    feat = lax.broadcasted_iota(jnp.int32, shape, axis)
    return feat < HEAD_DIM, feat >= HEAD_DIM


def _attn_fwd(z, qkvt, cumt, cumb, bsz, heads, q_off, *, name):
    t = z.shape[0]
    width = heads * HEAD_DIM
    pairs = heads // 2
    s = t // bsz
    blk = ATTN_BLOCK
    nq = s // blk
    k_col = (q_off + width) // LANES
    v_row = 2 * width // LANES
    reps = blk // LANES

    def body(k_ref, qt_ref, vt_ref, cqt_ref, ckb_ref, ot_ref, lse_ref):
        p_id = pl.program_id(1)
        i = pl.program_id(2)
        qt = qt_ref[...]
        masks = _head_masks((LANES, blk), 0)
        qtm = [jnp.where(mk, qt, jnp.zeros_like(qt)) for mk in masks]
        cq = [cqt_ref[0, pl.ds(2 * p_id + hh, 1), :] for hh in range(2)]
        kidx = lax.broadcasted_iota(jnp.int32, (blk, blk), 0)
        qidx = lax.broadcasted_iota(jnp.int32, (blk, blk), 1)

        def block(j, carry, masked):
            off = pl.multiple_of(j * blk, blk)
            kp = k_ref[pl.ds(off, blk), :].astype(BF16)
            vtp = vt_ref[:, pl.ds(off, blk)]
            out = []
            for hh in range(2):
                m, l, acc = carry[hh]
                sc = jnp.dot(kp, qtm[hh], preferred_element_type=F32) * ATTN_SCALE
                ck = ckb_ref[0, hh, pl.ds(off, blk), :]
                sc = (sc + cq[hh]) - jnp.concatenate([ck] * reps, axis=1)
                if masked:
                    sc = jnp.where(qidx >= kidx, sc, NEG)
                m_new = jnp.maximum(m, jnp.max(sc, axis=0, keepdims=True))
                pr = jnp.exp(sc - m_new)
                a = jnp.exp(m - m_new)
                l = a * l + jnp.sum(pr, axis=0, keepdims=True)
                p_hi = pr.astype(BF16)
                p_lo = (pr - p_hi.astype(F32)).astype(BF16)
                pv = (jnp.dot(vtp, p_hi, preferred_element_type=F32)
                      + jnp.dot(vtp, p_lo, preferred_element_type=F32))
                acc = a * acc + pv[hh * HEAD_DIM:(hh + 1) * HEAD_DIM]
                out.append((m_new, l, acc))
            return tuple(out)

        init = tuple((jnp.full((1, blk), NEG, F32), jnp.zeros((1, blk), F32), jnp.zeros((HEAD_DIM, blk), F32))
                     for _ in range(2))
        carry = _loop_by_twos(0, i, lambda j, cr: block(j, cr, False), init)
        carry = block(i, carry, True)
        lse_ref[...] = jnp.zeros_like(lse_ref)
        for hh in range(2):
            m, l, acc = carry[hh]
            ot_ref[pl.ds(hh * HEAD_DIM, HEAD_DIM), :] = acc / l
            lse_ref[0, 0, pl.ds(hh, 1), :] = m + jnp.log(l)

    return _call(body, name=name, grid=(bsz, pairs, nq),
                 in_specs=[pl.BlockSpec((s, LANES), lambda b, p, i: (b, k_col + p)),
                           pl.BlockSpec((LANES, blk), lambda b, p, i: (p, b * nq + i)),
                           pl.BlockSpec((LANES, s), lambda b, p, i: (v_row + p, b)),
                           pl.BlockSpec((1, heads, blk), lambda b, p, i: (b, 0, i)),
                           pl.BlockSpec((1, 2, s, LANES), lambda b, p, i: (b, p, 0, 0))],
                 out_specs=[pl.BlockSpec((LANES, blk), lambda b, p, i: (p, b * nq + i)),
                            pl.BlockSpec((1, 1, SUBLANES, blk), lambda b, p, i: (b, p, 0, i))],
                 out_shape=[_sds((width, t), F32), _sds((bsz, pairs, SUBLANES, s), F32)],
                 sem=("parallel", "parallel", "parallel"))(z, qkvt, qkvt, cumt, cumb)


def _attn_bwd(z, qkvt, cumt, cumb, ot, do, dot, lse, bsz, heads, q_off, *, name):
    t = z.shape[0]
    width = heads * HEAD_DIM
    pairs = heads // 2
    s = t // bsz
    blk = ATTN_BLOCK
    nkv = s // blk
    q_col = q_off // LANES
    k_col = (q_off + width) // LANES
    v_col = (q_off + 2 * width) // LANES
    k_row = width // LANES
    reps = blk // LANES

    def body(k_ref, v_ref, kt_ref, q_ref, qt_ref, do_ref, dot_ref, ot_ref, lse_ref, ckb_ref, cqt_ref,
             dk_ref, dv_ref, dqt_ref, dcum_ref, dqt_acc, ds_acc):
        p_id = pl.program_id(1)
        j = pl.program_id(2)

        @pl.when(j == 0)
        def _():
            dqt_acc[...] = jnp.zeros_like(dqt_acc)

        kp = k_ref[...].astype(BF16)
        vp = v_ref[...].astype(BF16)
        kt = kt_ref[...]
        feat_masks = _head_masks((LANES, blk), 0)
        lane_masks = _head_masks((blk, LANES), 1)
        ktm = [jnp.where(mk, kt, jnp.zeros_like(kt)) for mk in feat_masks]
        ck = [jnp.concatenate([ckb_ref[0, hh]] * reps, axis=1) for hh in range(2)]
        kidx = lax.broadcasted_iota(jnp.int32, (blk, blk), 0)
        qidx = lax.broadcasted_iota(jnp.int32, (blk, blk), 1)
        ds_acc[...] = jnp.zeros_like(ds_acc)

        def block(i, carry, masked):
            dk, dv = carry
            off = pl.multiple_of(i * blk, blk)
            qt = qt_ref[:, pl.ds(off, blk)]
            dt = dot_ref[:, pl.ds(off, blk)]
            o_t = ot_ref[:, pl.ds(off, blk)]
            q_rows = q_ref[pl.ds(off, blk), :].astype(BF16)
            do_rows = do_ref[pl.ds(off, blk), :]
            for hh in range(2):
                qtm = jnp.where(feat_masks[hh], qt, jnp.zeros_like(qt))
                dtm = jnp.where(feat_masks[hh], dt, jnp.zeros_like(dt))
                sc = jnp.dot(kp, qtm, preferred_element_type=F32) * ATTN_SCALE
                sc = (sc + cqt_ref[0, pl.ds(2 * p_id + hh, 1), pl.ds(off, blk)]) - ck[hh]
                pr = jnp.exp(sc - lse_ref[0, 0, pl.ds(hh, 1), pl.ds(off, blk)])
                if masked:
                    pr = jnp.where(qidx >= kidx, pr, 0.0)
                dp = jnp.dot(vp, dtm, preferred_element_type=F32)
                delta = jnp.sum(dtm.astype(F32) * o_t, axis=0, keepdims=True)
                ds = pr * (dp - delta)
                ds_acc[hh] += ds
                dsb = ds.astype(BF16)
                qm = jnp.where(lane_masks[hh], q_rows, jnp.zeros_like(q_rows))
                dom = jnp.where(lane_masks[hh], do_rows, jnp.zeros_like(do_rows))
                dv = dv + jnp.dot(pr.astype(BF16), dom, preferred_element_type=F32)
                dk = dk + jnp.dot(dsb, qm, preferred_element_type=F32) * ATTN_SCALE
                dqt_acc[:, pl.ds(off, blk)] += jnp.dot(ktm[hh], dsb, preferred_element_type=F32) * ATTN_SCALE
            return dk, dv

        zero = jnp.zeros((blk, LANES), F32)
        carry = block(j, (zero, zero), True)
        dk, dv = _loop_by_twos(j + 1, nkv, lambda i, cr: block(i, cr, False), carry)
        dk_ref[...] = dk.astype(dk_ref.dtype)
        dv_ref[...] = dv.astype(dv_ref.dtype)
        lane = lax.broadcasted_iota(jnp.int32, (blk, LANES), 1)
        dcum = jnp.zeros((blk, LANES), F32)
        for hh in range(2):
            col = jnp.sum(ds_acc[hh], axis=1, keepdims=True)
            dcum = jnp.where(lane == 2 * p_id + hh, -col, dcum)
        dcum_ref[0, 0] = dcum

        @pl.when(j == nkv - 1)
        def _():
            dqt_ref[...] = dqt_acc[...]

    key_rows = lambda col: pl.BlockSpec((blk, LANES), lambda b, p, j: (b * nkv + j, col + p))
    seq_t = lambda row: pl.BlockSpec((LANES, s), lambda b, p, j: (row + p, b))
    return _call(body, name=name, grid=(bsz, pairs, nkv),
                 in_specs=[key_rows(k_col), key_rows(v_col),
                           pl.BlockSpec((LANES, blk), lambda b, p, j: (k_row + p, b * nkv + j)),
                           pl.BlockSpec((s, LANES), lambda b, p, j: (b, q_col + p)), seq_t(0),
                           pl.BlockSpec((s, LANES), lambda b, p, j: (b, p)), seq_t(0), seq_t(0),
                           pl.BlockSpec((1, 1, SUBLANES, s), lambda b, p, j: (b, p, 0, 0)),
                           pl.BlockSpec((1, 2, blk, LANES), lambda b, p, j: (b, p, j, 0)),
                           pl.BlockSpec((1, heads, s), lambda b, p, j: (b, 0, 0))],
                 out_specs=[key_rows(0), key_rows(0), seq_t(0),
                            pl.BlockSpec((1, 1, blk, LANES), lambda b, p, j: (b, p, j, 0))],
                 out_shape=[_sds((t, width), BF16), _sds((t, width), BF16), _sds((width, t), F32),
                            _sds((bsz, pairs, s, LANES), F32)],
                 scratch=[pltpu.VMEM((LANES, s), F32), pltpu.VMEM((2, blk, blk), F32)],
                 sem=("parallel", "parallel", "arbitrary"))(z, z, qkvt, z, qkvt, do, dot, ot, lse, cumb, cumt)


def _adamw(w, g, m, v, *, name):
    bc1 = 1.0 - ADAM_B1 ** ADAM_STEP
    bc2 = 1.0 - ADAM_B2 ** ADAM_STEP

    def body(w_ref, g_ref, m_ref, v_ref, d_ref, nm_ref, nv_ref):
        g_v = g_ref[...]
        nm = ADAM_B1 * m_ref[...] + (1.0 - ADAM_B1) * g_v
        nv = ADAM_B2 * v_ref[...] + (1.0 - ADAM_B2) * (g_v * g_v)
        nm_ref[...] = nm
        nv_ref[...] = nv
        d_ref[...] = -ADAM_LR * ((nm / bc1) / (jnp.sqrt(nv / bc2) + ADAM_EPS) + ADAM_WD * w_ref[...])

    if w.ndim == 2:
        grid = (1,)
        blk = pl.BlockSpec(w.shape, lambda i: (0, 0))
    else:
        layers, rows, cols = w.shape
        tr = rows if rows <= 256 else _pick(rows, (256, 128, 64, 32, 16, 8))
        grid = (layers, rows // tr)
        blk = pl.BlockSpec((1, tr, cols), lambda layer, i: (layer, i, 0))
    return tuple(_call(body, name=name, grid=grid, in_specs=[blk] * 4, out_specs=[blk] * 3,
                       out_shape=[_sds(w.shape, F32)] * 3, sem=("parallel",) * len(grid))(w, g, m, v))


_ANY = pl.BlockSpec(memory_space=pl.ANY)


def _comm_call(body, *, name, n_in, out_shape, n_sems):
    scratch = [pltpu.SemaphoreType.DMA((n_sems,)), pltpu.SemaphoreType.DMA((n_sems,)),
               pltpu.SemaphoreType.DMA((len(out_shape),))]
    return pl.pallas_call(body, name=name, in_specs=[_ANY] * n_in, out_specs=[_ANY] * len(out_shape),
                          out_shape=out_shape, scratch_shapes=scratch)


def _place():
    x, y, c = lax.axis_index("x"), lax.axis_index("y"), lax.axis_index("c")
    return x, y, c, [(1 - x, y), (x, 1 - y), (1 - x, 1 - y)]


def _remote(src, dst, send_sems, recv_sems, sem, to):
    return pltpu.make_async_remote_copy(src_ref=src, dst_ref=dst, send_sem=send_sems.at[sem],
                                        recv_sem=recv_sems.at[sem], device_id=to, device_id_type=MESH)


def _all_gather8(v, *, name):
    def body(v_ref, out_ref, send_sems, recv_sems, local_sems):
        x, y, c, _ = _place()
        me = 4 * x + 2 * y + c
        mine = pltpu.make_async_copy(v_ref, out_ref.at[me], local_sems.at[0])
        mine.start()
        peers = []
        for k in range(1, N_DEVICES):
            px = 1 - x if k & 4 else x
            py = 1 - y if k & 2 else y
            pc = 1 - c if k & 1 else c
            peers.append((px, py, pc))
        sends = [_remote(v_ref, out_ref.at[me], send_sems, recv_sems, k, peer) for k, peer in enumerate(peers)]
        for cp in sends:
            cp.start()
        for k, (px, py, pc) in enumerate(peers):
            _remote(v_ref, out_ref.at[4 * px + 2 * py + pc], send_sems, recv_sems, k, (px, py, pc)).wait_recv()
        for cp in sends:
            cp.wait_send()
        mine.wait()

    out = _comm_call(body, name=name, n_in=1, out_shape=[_sds((N_DEVICES,) + v.shape, v.dtype)],
                     n_sems=N_DEVICES - 1)(v)
    return out[0]


def _window(ref, mode, layer, chip, rows, cols):
    if mode == "slab":
        return ref.at[layer, chip]
    if mode == "cols":
        return ref.at[layer, :, pl.ds(pl.multiple_of(chip * cols, LANES), cols)]
    return ref.at[layer, pl.ds(pl.multiple_of(chip * rows, SUBLANES), rows), :]


def _whole_shape(mode, shard_shape):
    layers, rows, cols = shard_shape
    if mode == "slab":
        return (layers, N_CHIPS, rows, cols)
    if mode == "cols":
        assert cols % LANES == 0
        return (layers, rows, N_CHIPS * cols)
    assert rows % 16 == 0
    return (layers, N_CHIPS * rows, cols)


def _gather_weights(shards, modes, *, name):
    n = len(shards)
    meta = [(mode,) + tuple(a.shape[1:]) for a, mode in zip(shards, modes)]
    for a in shards:
        assert a.shape[0] == 2
    per = 7

    def body(*refs):
        ins, outs = refs[:n], refs[n:2 * n]
        send_sems, recv_sems, _ = refs[2 * n:]
        x, y, c, chips = _place()
        me = 2 * x + y
        sibling = (x, y, 1 - c)
        own, first, passed = [], [], []
        for i, (mode, rows, cols) in enumerate(meta):
            for r, (cx, cy) in enumerate(chips):
                cp = _remote(ins[i].at[c], _window(outs[i], mode, c, me, rows, cols), send_sems, recv_sems,
                             per * i + r, (cx, cy, c))
                cp.start()
                first.append(cp)
            cp = _remote(ins[i], _window(outs[i], mode, slice(None), me, rows, cols), send_sems, recv_sems,
                         per * i + 6, sibling)
            cp.start()
            own.append(cp)
        for i, (mode, rows, cols) in enumerate(meta):
            for r, (cx, cy) in enumerate(chips):
                win = _window(outs[i], mode, c, 2 * cx + cy, rows, cols)
                _remote(win, win, send_sems, recv_sems, per * i + r, (cx, cy, c)).wait_recv()
                cp = _remote(win, win, send_sems, recv_sems, per * i + 3 + r, sibling)
                cp.start()
                passed.append(cp)
        for i, (mode, rows, cols) in enumerate(meta):
            own[i].wait_recv()
            for r, (cx, cy) in enumerate(chips):
                win = _window(outs[i], mode, 1 - c, 2 * cx + cy, rows, cols)
                _remote(win, win, send_sems, recv_sems, per * i + 3 + r, sibling).wait_recv()
        for cp in first + passed + own:
            cp.wait_send()

    out_shape = [_sds(_whole_shape(mode, a.shape), a.dtype) for a, mode in zip(shards, modes)]
    return _comm_call(body, name=name, n_in=n, out_shape=out_shape, n_sems=per * n)(*shards)


def _rs_swap(grads, *, name):
    n = len(grads)

    def body(*refs):
        ins, outs = refs[:n], refs[n:2 * n]
        send_sems, recv_sems, _ = refs[2 * n:]
        x, y, c, _ = _place()
        copies = [_remote(ins[i].at[1 - c], outs[i], send_sems, recv_sems, i, (x, y, 1 - c)) for i in range(n)]
        for cp in copies:
            cp.start()
        for cp in copies:
            cp.wait()

    return _comm_call(body, name=name, n_in=n, out_shape=[_sds(g.shape[1:], g.dtype) for g in grads], n_sems=n)(*grads)


def _part(ref, mode, chip, rows, cols):
    if mode == "slab":
        return ref.at[chip]
    if mode == "cols":
        return ref.at[:, pl.ds(pl.multiple_of(chip * cols, LANES), cols)]
    return ref.at[pl.ds(pl.multiple_of(chip * rows, SUBLANES), rows), :]


def _rs_scatter(parts, modes, shard_shapes, *, name):
    n = len(parts)
    meta = [(mode,) + tuple(shp[1:]) for mode, shp in zip(modes, shard_shapes)]

    def body(*refs):
        ins, outs = refs[:n], refs[n:2 * n]
        send_sems, recv_sems, local_sems = refs[2 * n:]
        x, y, c, chips = _place()
        me = 2 * x + y
        local, sends = [], []
        for i, (mode, rows, cols) in enumerate(meta):
            cp = pltpu.make_async_copy(_part(ins[i], mode, me, rows, cols), outs[i].at[me], local_sems.at[i])
            cp.start()
            local.append(cp)
            for r, (cx, cy) in enumerate(chips):
                cp = _remote(_part(ins[i], mode, 2 * cx + cy, rows, cols), outs[i].at[me], send_sems, recv_sems,
                             3 * i + r, (cx, cy, c))
                cp.start()
                sends.append(cp)
        for i, (mode, rows, cols) in enumerate(meta):
            for r, (cx, cy) in enumerate(chips):
                k = 2 * cx + cy
                _remote(_part(ins[i], mode, k, rows, cols), outs[i].at[k], send_sems, recv_sems, 3 * i + r,
                        (cx, cy, c)).wait_recv()
        for cp in sends:
            cp.wait_send()
        for cp in local:
            cp.wait()

    out_shape = [_sds((N_CHIPS,) + tuple(shp[1:]), p.dtype) for p, shp in zip(parts, shard_shapes)]
    return _comm_call(body, name=name, n_in=n, out_shape=out_shape, n_sems=3 * n)(*parts)


def _rs_exchange(sums, *, name):
    n = len(sums)

    def body(*refs):
        ins, outs = refs[:n], refs[n:2 * n]
        send_sems, recv_sems, _ = refs[2 * n:]
        x, y, c, _ = _place()
        copies = [_remote(ins[i], outs[i], send_sems, recv_sems, i, (x, y, 1 - c)) for i in range(n)]
        for cp in copies:
            cp.start()
        for cp in copies:
            cp.wait()

    return _comm_call(body, name=name, n_in=n, out_shape=[_sds(s.shape, s.dtype) for s in sums], n_sems=n)(*sums)


def _row_tile(rows, cols, itemsize):
    target = max(SUBLANES, (2 << 20) // (cols * itemsize))
    cands = [c for c in (2048, 1024, 512, 256, 128, 64, 32, 16) if c <= target]
    tr = _pick(rows, cands)
    return tr


def _add_layer(g, other, core, *, name):
    _, rows, cols = g.shape
    tr = _row_tile(rows, cols, 4)

    def body(core_ref, g_ref, o_ref, out_ref):
        out_ref[...] = (g_ref[0] + o_ref[...]).astype(out_ref.dtype)

    grid_spec = pltpu.PrefetchScalarGridSpec(
        num_scalar_prefetch=1, grid=(rows // tr,),
        in_specs=[pl.BlockSpec((1, tr, cols), lambda i, core_ref: (core_ref[0], i, 0)),
                  pl.BlockSpec((tr, cols), lambda i, core_ref: (i, 0))],
        out_specs=pl.BlockSpec((tr, cols), lambda i, core_ref: (i, 0)))
    return pl.pallas_call(body, name=name, grid_spec=grid_spec, out_shape=_sds((rows, cols), BF16),
                          compiler_params=pltpu.CompilerParams(dimension_semantics=("parallel",),
                                                               vmem_limit_bytes=VMEM_LIMIT))(core, g, other)


def _sum_slots(parts, *, name):
    n, rows, cols = parts.shape
    tr = _row_tile(rows, cols, 4)

    def body(p_ref, o_ref):
        acc = p_ref[0].astype(F32) + p_ref[1].astype(F32)
        for k in range(2, n):
            acc = acc + p_ref[k].astype(F32)
        o_ref[...] = acc

    return _call(body, name=name, grid=(rows // tr,),
                 in_specs=[pl.BlockSpec((n, tr, cols), lambda i: (0, i, 0))],
                 out_specs=pl.BlockSpec((tr, cols), lambda i: (i, 0)),
                 out_shape=_sds((rows, cols), F32), sem=("parallel",))(parts)


def _reduce_scatter(grads, modes, shard_shapes):
    core = lax.axis_index("c").astype(jnp.int32).reshape(1)
    flat = [g.reshape(g.shape[0], -1, g.shape[-1]) for g in grads]
    from_sibling = _rs_swap(flat, name="rs_swap")
    parts = []
    for i, (g, o) in enumerate(zip(flat, from_sibling)):
        p = _add_layer(g, o, core, name=f"rs_add_{i}")
        parts.append(p.reshape(grads[i].shape[1:]))
    from_chips = _rs_scatter(parts, modes, shard_shapes, name="rs_scatter")
    sums = [_sum_slots(r, name=f"rs_sum_{i}") for i, r in enumerate(from_chips)]
    others = _rs_exchange(sums, name="rs_exchange")
    mine_first = lax.axis_index("c") == 0
    return [jnp.where(mine_first, jnp.stack([mine, other]), jnp.stack([other, mine]))
            for mine, other in zip(sums, others)]


def _layer_weights(full, rep, layer, dims):
    f_off, n_heads = dims["f_off"], dims["heads"]
    d_ff = full["w_ffn_up"].shape[-1] // 2
    w_in = full["w_in"][layer]
    b_in = rep["b_in"][layer]
    pad = LANES - n_heads
    return {
        "w_main": jnp.concatenate([w_in[:, :f_off], w_in[:, f_off + n_heads:]], axis=1),
        "b_main": jnp.concatenate([b_in[:f_off], b_in[f_off + n_heads:]])[None],
        "w_f": jnp.pad(w_in[:, f_off:f_off + n_heads], ((0, 0), (0, pad))),
        "b_f": jnp.pad(b_in[f_off:f_off + n_heads], (0, pad))[None],
        "conv_a_w": full["conv_a_w"][layer],
        "conv_a_b": rep["conv_a_b"][layer][None],
        "ln_conv_g": rep["ln_conv_g"][layer][None],
        "ln_conv_b": rep["ln_conv_b"][layer][None],
        "w_conv_proj": full["w_conv_proj"][layer],
        "w_attn_proj": full["w_attn_proj"][layer],
        "w_mix_out": full["w_mix_out"][layer],
        "b_mix_out": rep["b_mix_out"][layer][None],
        "ln1_g": rep["ln1_g"][layer][None],
        "ln1_b": rep["ln1_b"][layer][None],
        "w_ffn_up": full["w_ffn_up"][layer],
        "w_ffn_up_gate": full["w_ffn_up"][layer][:, :d_ff],
        "w_ffn_up_lin": full["w_ffn_up"][layer][:, d_ff:],
        "ffn_conv_w": full["ffn_conv_w"][layer],
        "ffn_conv_b": rep["ffn_conv_b"][layer][None],
        "w_ffn_down": full["w_ffn_down"][layer],
        "ln2_g": rep["ln2_g"][layer][None],
        "ln2_b": rep["ln2_b"][layer][None],
    }


def _layer_fwd(x, mod, p, dims, tag):
    bsz, d, ch, heads, alpha = dims["bsz"], dims["d"], dims["ch"], dims["heads"], dims["alpha"]
    mods = [mod[:, k * d:(k + 1) * d][:, None, :] for k in range(6)]
    shift1, scale1, gate1, shift2, scale2, gate2 = mods
    u = _ln_mod_fwd(x, scale1, shift1, bsz, name=f"ln_mod1_{tag}")
    zm = _matmul(u, p["w_main"], "nn", BF16, bias=p["b_main"], name=f"in_main_{tag}")
    zf = _matmul(u, p["w_f"], "nn", F32, bias=p["b_f"], name=f"in_forget_{tag}")
    a3 = _conv_branch_fwd(zm, p["conv_a_w"], p["conv_a_b"], p["ln_conv_g"], p["ln_conv_b"], bsz, ch,
                          name=f"conv_branch_{tag}")
    ya = _matmul(a3, p["w_conv_proj"], "nn", F32, name=f"conv_proj_{tag}")
    cumt, cumb = _fgate_fwd(zf, bsz, heads, name=f"fgate_{tag}")
    qkvt = _to_features_major(zm, 2 * ch, heads * HEAD_DIM, 3, name=f"qkv_t_{tag}")
    ot, lse = _attn_fwd(zm, qkvt, cumt, cumb, bsz, heads, 2 * ch, name=f"attn_{tag}")
    yb = _matmul(ot, p["w_attn_proj"], "tn", F32, name=f"attn_proj_{tag}")
    m = _gate_merge_fwd(zm, ya, yb, dims["ga_off"], name=f"merge_{tag}")
    mix = _matmul(m, p["w_mix_out"], "nn", F32, bias=p["b_mix_out"], name=f"mix_out_{tag}")
    x1 = _ln_res_fwd(x, mix, gate1, p["ln1_g"], p["ln1_b"], alpha, bsz, name=f"ln_res1_{tag}")
    u2 = _ln_mod_fwd(x1, scale2, shift2, bsz, name=f"ln_mod2_{tag}")
    hp = _matmul(u2, p["w_ffn_up"], "nn", BF16, name=f"ffn_up_{tag}")
    f = _ffn_act_fwd(hp, p["ffn_conv_w"], p["ffn_conv_b"], bsz, dims["tcf"], name=f"ffn_act_{tag}")
    ffn = _matmul(f, p["w_ffn_down"], "nn", F32, name=f"ffn_down_{tag}")
    x2 = _ln_res_fwd(x1, ffn, gate2, p["ln2_g"], p["ln2_b"], alpha, bsz, name=f"ln_res2_{tag}")
    saved = dict(x=x, mods=mods, u=u, zm=zm, zf=zf, a3=a3, ya=ya, yb=yb, cumt=cumt, cumb=cumb,
                 qkvt=qkvt, ot=ot, lse=lse, m=m, mix=mix, x1=x1, u2=u2, hp=hp, f=f, ffn=ffn)
    return x2, saved


def _layer_bwd(dx2, p, sv, dims, tag):
    bsz, ch, heads, alpha = dims["bsz"], dims["ch"], dims["heads"], dims["alpha"]
    f_off, tcf = dims["f_off"], dims["tcf"]
    shift1, scale1, gate1, shift2, scale2, gate2 = sv["mods"]
    g = {}
    dr2, dffn, dgate2, g["ln2_g"], g["ln2_b"], _ = _ln_res_bwd(
        dx2, sv["x1"], sv["ffn"], gate2, p["ln2_g"], alpha, bsz, name=f"ln_res2_bwd_{tag}")
    df = _matmul(dffn, p["w_ffn_down"], "nt", F32, name=f"ffn_down_dx_{tag}")
    g["w_ffn_down"] = _matmul(sv["f"], dffn, "tn", F32, name=f"ffn_down_dw_{tag}")
    dhg, dhl, dwg, dwl, dbg, dbl = _ffn_act_bwd(sv["hp"], df, p["ffn_conv_w"], p["ffn_conv_b"], bsz, tcf,
                                                name=f"ffn_act_bwd_{tag}")
    g["ffn_conv_w"] = jnp.concatenate([dwg, dwl], axis=1)
    g["ffn_conv_b"] = jnp.concatenate([dbg, dbl], axis=1)[0]
    du2 = _matmul(dhg, p["w_ffn_up_gate"], "nt", F32, name=f"ffn_up_gate_dx_{tag}")
    du2 = _matmul(dhl, p["w_ffn_up_lin"], "nt", F32, add=du2, name=f"ffn_up_lin_dx_{tag}")
    g["w_ffn_up"] = jnp.concatenate([_matmul(sv["u2"], dhg, "tn", F32, name=f"ffn_up_gate_dw_{tag}"),
                                     _matmul(sv["u2"], dhl, "tn", F32, name=f"ffn_up_lin_dw_{tag}")], axis=1)
    dx1, dscale2, dshift2 = _ln_mod_bwd(du2, sv["x1"], scale2, dr2, alpha, bsz, name=f"ln_mod2_bwd_{tag}")
    dr1, dmix, dgate1, g["ln1_g"], g["ln1_b"], g["b_mix_out"] = _ln_res_bwd(
        dx1, sv["x"], sv["mix"], gate1, p["ln1_g"], alpha, bsz, name=f"ln_res1_bwd_{tag}")
    dm = _matmul(dmix, p["w_mix_out"], "nt", F32, name=f"mix_out_dx_{tag}")
    g["w_mix_out"] = _matmul(sv["m"], dmix, "tn", F32, name=f"mix_out_dw_{tag}")
    dya, dyb, dzga, dzgb = _gate_merge_bwd(sv["zm"], sv["ya"], sv["yb"], dm, dims["ga_off"], name=f"merge_bwd_{tag}")
    da3 = _matmul(dya, p["w_conv_proj"], "nt", F32, name=f"conv_proj_dx_{tag}")
    g["w_conv_proj"] = _matmul(sv["a3"], dya, "tn", F32, name=f"conv_proj_dw_{tag}")
    do = _matmul(dyb, p["w_attn_proj"], "nt", BF16, name=f"attn_proj_dx_{tag}")
    dot = _matmul(p["w_attn_proj"], dyb, "nt", BF16, name=f"attn_proj_dxt_{tag}")
    g["w_attn_proj"] = _matmul(sv["ot"], dyb, "nn", F32, name=f"attn_proj_dw_{tag}")
    dzglu, g["conv_a_w"], dcb, g["ln_conv_g"], g["ln_conv_b"] = _conv_branch_bwd(
        sv["zm"], da3, p["conv_a_w"], p["conv_a_b"], p["ln_conv_g"], p["ln_conv_b"], bsz, ch,
        name=f"conv_branch_bwd_{tag}")
    g["conv_a_b"] = dcb[0]
    dk, dv, dqt, dcum = _attn_bwd(sv["zm"], sv["qkvt"], sv["cumt"], sv["cumb"], sv["ot"], do, dot, sv["lse"], bsz,
                                  heads, 2 * ch, name=f"attn_bwd_{tag}")
    dq = _to_rows_major(dqt, name=f"dq_rows_{tag}")
    dzf = _fgate_bwd(dcum, sv["zf"], bsz, name=f"fgate_bwd_{tag}")
    dzm = jnp.concatenate([dzglu, dq, dk, dv, dzga, dzgb], axis=1)
    du = _matmul(dzm, p["w_main"], "nt", F32, name=f"in_main_dx_{tag}")
    du = _matmul(dzf, p["w_f"], "nt", F32, add=du, name=f"in_forget_dx_{tag}")
    dwm, dbm = _matmul(sv["u"], dzm, "tn", F32, colsum=True, name=f"in_main_dw_{tag}")
    dwf, dbf = _matmul(sv["u"], dzf, "tn", F32, colsum=True, name=f"in_forget_dw_{tag}")
    dbm, dbf = dbm[0], dbf[0]
    g["w_in"] = jnp.concatenate([dwm[:, :f_off], dwf[:, :heads], dwm[:, f_off:]], axis=1)
    g["b_in"] = jnp.concatenate([dbm[:f_off], dbf[:heads], dbm[f_off:]])
    dx, dscale1, dshift1 = _ln_mod_bwd(du, sv["x"], scale1, dr1, alpha, bsz, name=f"ln_mod1_bwd_{tag}")
    dmod = jnp.concatenate([dshift1, dscale1, dgate1, dshift2, dscale2, dgate2], axis=2)[:, 0, :]
    return dx, g, dmod


def _local_step(x, mod, loss_target, full, rep, dims):
    bsz, seq, d = x.shape
    layers = mod.shape[0]
    params = [_layer_weights(full, rep, layer, dims) for layer in range(layers)]
    h = x.reshape(bsz * seq, d)
    saved = []
    for layer in range(layers):
        h, sv = _layer_fwd(h, mod[layer], params[layer], dims, f"l{layer}")
        saved.append(sv)
    dh, sq = _loss_head(h, loss_target.reshape(bsz * seq, d), name="loss_head")
    loss_local = 0.5 * jnp.sum(sq) / d
    grads, dmods = [None] * layers, [None] * layers
    for layer in reversed(range(layers)):
        dh, grads[layer], dmods[layer] = _layer_bwd(dh, params[layer], saved[layer], dims, f"l{layer}")
    stacked = {wname: jnp.stack([grads[layer][wname] for layer in range(layers)]) for wname in grads[0]}
    return loss_local, dh.reshape(bsz, seq, d), stacked, jnp.stack(dmods)


def _pad_rows(a):
    extra = -a.shape[-2] % (2 * SUBLANES)
    if extra == 0:
        return a
    return jnp.pad(a, [(0, 0)] * (a.ndim - 2) + [(0, extra), (0, 0)])


def _to_slab(g):
    layers, k, n4 = g.shape
    return jnp.transpose(g.reshape(layers, k, N_CHIPS, n4 // N_CHIPS), (0, 2, 1, 3))


def _from_slab(w):
    layers, _, k, n = w.shape
    return jnp.transpose(w, (0, 2, 1, 3)).reshape(layers, k, N_CHIPS * n)


def kernel(x, c, w_ada, b_ada, w_in, b_in, conv_a_w, conv_a_b, ln_conv_g, ln_conv_b, w_conv_proj, w_attn_proj, w_mix_out, b_mix_out, ln1_g, ln1_b, w_ffn_up, ffn_conv_w, ffn_conv_b, w_ffn_down, ln2_g, ln2_b, loss_target, m_w_ada, m_b_ada, m_w_in, m_b_in, m_conv_a_w, m_conv_a_b, m_ln_conv_g, m_ln_conv_b, m_w_conv_proj, m_w_attn_proj, m_w_mix_out, m_b_mix_out, m_ln1_g, m_ln1_b, m_w_ffn_up, m_ffn_conv_w, m_ffn_conv_b, m_w_ffn_down, m_ln2_g, m_ln2_b, v_w_ada, v_b_ada, v_w_in, v_b_in, v_conv_a_w, v_conv_a_b, v_ln_conv_g, v_ln_conv_b, v_w_conv_proj, v_w_attn_proj, v_w_mix_out, v_b_mix_out, v_ln1_g, v_ln1_b, v_w_ffn_up, v_ffn_conv_w, v_ffn_conv_b, v_w_ffn_down, v_ln2_g, v_ln2_b):
    weights = dict(zip(WEIGHTS, (w_ada, b_ada, w_in, b_in, conv_a_w, conv_a_b, ln_conv_g, ln_conv_b, w_conv_proj,
                                 w_attn_proj, w_mix_out, b_mix_out, ln1_g, ln1_b, w_ffn_up, ffn_conv_w, ffn_conv_b,
                                 w_ffn_down, ln2_g, ln2_b)))
    mom1 = dict(zip(WEIGHTS, (m_w_ada, m_b_ada, m_w_in, m_b_in, m_conv_a_w, m_conv_a_b, m_ln_conv_g, m_ln_conv_b,
                              m_w_conv_proj, m_w_attn_proj, m_w_mix_out, m_b_mix_out, m_ln1_g, m_ln1_b, m_w_ffn_up,
                              m_ffn_conv_w, m_ffn_conv_b, m_w_ffn_down, m_ln2_g, m_ln2_b)))
    mom2 = dict(zip(WEIGHTS, (v_w_ada, v_b_ada, v_w_in, v_b_in, v_conv_a_w, v_conv_a_b, v_ln_conv_g, v_ln_conv_b,
                              v_w_conv_proj, v_w_attn_proj, v_w_mix_out, v_b_mix_out, v_ln1_g, v_ln1_b, v_w_ffn_up,
                              v_ffn_conv_w, v_ffn_conv_b, v_w_ffn_down, v_ln2_g, v_ln2_b)))
    bsz, seq, d = x.shape
    layers = w_ada.shape[0]
    ch = conv_a_w.shape[2] * N_CHIPS
    width = w_attn_proj.shape[1]
    heads = width // HEAD_DIM
    d_ff = w_ffn_down.shape[1] * N_CHIPS
    dims = dict(bsz=bsz, d=d, ch=ch, heads=heads, alpha=(2.0 * layers) ** 0.25, f_off=2 * ch + 3 * width,
                ga_off=2 * ch + 3 * width, tcf=_pick(d_ff, (256, 128)))
    chip = 2 * lax.axis_index("x") + lax.axis_index("y")
    device = 2 * chip + lax.axis_index("c")
    ada_cols = w_ada.shape[2]

    c_act = _silu_rows(_all_gather8(c, name="gather_c").reshape(N_DEVICES * bsz, d), name="silu_c")
    b_ada_mine = lax.dynamic_slice_in_dim(b_ada, chip * ada_cols, ada_cols, axis=1)
    mod_cols = jnp.stack([_matmul(c_act, w_ada[layer], "nn", F32, bias=b_ada_mine[layer][None], name=f"ada_l{layer}")
                          for layer in range(layers)])
    mod_all = _all_gather8(mod_cols, name="gather_mod")
    mod_all = jnp.concatenate([mod_all[2 * k] for k in range(N_CHIPS)], axis=-1)
    mod = lax.dynamic_slice_in_dim(mod_all, device * bsz, bsz, axis=1)

    shards = [_pad_rows(weights[wname].astype(BF16) if as_bf16 else weights[wname]) for wname, _, as_bf16 in GATHERED]
    modes = [mode for _, mode, _ in GATHERED]
    whole = _gather_weights(shards, modes, name="gather_weights")
    full = {wname: w[:, :weights[wname].shape[1]] if mode == "cols" else w
            for (wname, mode, _), w in zip(GATHERED, whole)}
    full["w_in"] = _from_slab(full["w_in"])
    rep = {wname: weights[wname] for wname in REPLICATED}

    loss_local, grad_x, grads, dmod = _local_step(x, mod, loss_target, full, rep, dims)
    loss = lax.psum(loss_local, ("x", "y", "c"))

    grads["w_in"] = _to_slab(grads["w_in"])
    shard_shapes = [s.shape for s in shards]
    reduced = _reduce_scatter([_pad_rows(grads[wname]) for wname, _, _ in GATHERED], modes, shard_shapes)
    grad = {wname: r[:, :weights[wname].shape[1]] for (wname, _, _), r in zip(GATHERED, reduced)}

    small = jnp.concatenate([dmod.reshape(-1)] + [grads[wname].reshape(-1) for wname in REPLICATED])
    n_small = small.shape[0]
    rows = -(-n_small // (SUBLANES * LANES)) * SUBLANES
    small = jnp.pad(small, (0, rows * LANES - n_small)).reshape(rows, LANES)
    gathered = _all_gather8(small, name="gather_small")
    n_dmod = dmod.size
    dmod_all = gathered.reshape(N_DEVICES, -1)[:, :n_dmod].reshape(N_DEVICES, layers, bsz, 6 * d)
    dmod_all = jnp.transpose(dmod_all, (1, 0, 2, 3)).reshape(layers, N_DEVICES * bsz, 6 * d)
    summed = _sum_slots(gathered, name="sum_small").reshape(-1)
    off = n_dmod
    for wname in REPLICATED:
        n = weights[wname].size
        grad[wname] = summed[off:off + n].reshape(weights[wname].shape)
        off += n
    dmod_mine = lax.dynamic_slice_in_dim(dmod_all, chip * ada_cols, ada_cols, axis=2)
    grad["w_ada"] = jnp.stack([_matmul(c_act, dmod_mine[layer], "tn", F32, name=f"ada_dw_l{layer}")
                               for layer in range(layers)])
    grad["b_ada"] = jnp.stack([_colsum(dmod_all[layer], name=f"ada_db_l{layer}")[0] for layer in range(layers)])

    delta, new_m, new_v = {}, {}, {}
    for wname in WEIGHTS:
        delta[wname], new_m[wname], new_v[wname] = _adamw(weights[wname], grad[wname], mom1[wname], mom2[wname],
                                                          name=f"adamw_{wname}")
    return (loss, grad_x, *[grad[wname] for wname in WEIGHTS], *[delta[wname] for wname in WEIGHTS],
            *[new_m[wname] for wname in WEIGHTS], *[new_v[wname] for wname in WEIGHTS])
```

```python
import math

import jax
import jax.numpy as jnp
from jax import lax
from jax.experimental import pallas as pl
from jax.experimental.pallas import tpu as pltpu

F32 = jnp.float32
BF16 = jnp.bfloat16
MESH = pl.DeviceIdType.MESH

LN_EPS = 1e-5
HEAD_DIM = 64
ATTN_SCALE = HEAD_DIM ** -0.5
NEG = -1e30
FFN_PAD = 8
LANES = 128
SUBLANES = 8
ROW_CHUNK = 256
ATTN_BLOCK = 256
ATTN_UNROLL = 3
N_CHIPS = 4
N_DEVICES = 8
VMEM_LIMIT = 56 * 1024 * 1024

ADAM_LR = 0.001
ADAM_B1 = 0.9
ADAM_B2 = 0.999
ADAM_EPS = 1e-08
ADAM_WD = 0.01
ADAM_STEP = 10

GATHERED = (("w_in", "slab", True), ("conv_a_w", "cols", False), ("w_conv_proj", "cols", True),
            ("w_attn_proj", "cols", True), ("w_mix_out", "rows", True), ("w_ffn_up", "cols", True),
            ("ffn_conv_w", "cols", False), ("w_ffn_down", "rows", True))
REPLICATED = ("b_in", "conv_a_b", "ln_conv_g", "ln_conv_b", "b_mix_out", "ln1_g", "ln1_b",
              "ffn_conv_b", "ln2_g", "ln2_b")
WEIGHTS = ("w_ada", "b_ada", "w_in", "b_in", "conv_a_w", "conv_a_b", "ln_conv_g", "ln_conv_b",
           "w_conv_proj", "w_attn_proj", "w_mix_out", "b_mix_out", "ln1_g", "ln1_b", "w_ffn_up",
           "ffn_conv_w", "ffn_conv_b", "w_ffn_down", "ln2_g", "ln2_b")


def _pick(n, cands):
    for cand in cands:
        if n % cand == 0:
            return cand
    return n


def _call(body, *, name, grid, in_specs, out_specs, out_shape, scratch=(), sem=None):
    return pl.pallas_call(
        body, name=name, grid=grid, in_specs=in_specs, out_specs=out_specs, out_shape=out_shape,
        scratch_shapes=list(scratch),
        compiler_params=pltpu.CompilerParams(dimension_semantics=sem, vmem_limit_bytes=VMEM_LIMIT))


def _sds(shape, dtype):
    return jax.ShapeDtypeStruct(tuple(shape), dtype)


def _chunked(rows, fn):
    chunk = min(ROW_CHUNK, rows)
    if rows == chunk:
        fn(pl.ds(0, rows))
        return

    def step(i, carry):
        fn(pl.ds(pl.multiple_of(i * chunk, chunk), chunk))
        return carry

    lax.fori_loop(0, rows // chunk, step, 0)


def _matmul(a, b, mode, out_dtype, *, bias=None, add=None, colsum=False, name):
    if mode == "nn":
        (m, k), (_, n) = a.shape, b.shape
    elif mode == "nt":
        (m, k), (n, _) = a.shape, b.shape
    else:
        (k, m), (_, n) = a.shape, b.shape
    tm = _pick(m, (1024, 1408, 512, 256, 128))
    tn = _pick(n, (1536, 1408, 1024, 512, 256, 128))
    tk = k if k <= 1536 else _pick(k, (1024, 1536, 1408, 512, 256, 128))
    nk = k // tk
    if mode == "nn":
        a_spec = pl.BlockSpec((tm, tk), lambda i, j, kk: (i, kk))
        b_spec = pl.BlockSpec((tk, tn), lambda i, j, kk: (kk, j))
        dims = (((1,), (0,)), ((), ()))
    elif mode == "nt":
        a_spec = pl.BlockSpec((tm, tk), lambda i, j, kk: (i, kk))
        b_spec = pl.BlockSpec((tn, tk), lambda i, j, kk: (j, kk))
        dims = (((1,), (1,)), ((), ()))
    else:
        a_spec = pl.BlockSpec((tk, tm), lambda i, j, kk: (kk, i))
        b_spec = pl.BlockSpec((tk, tn), lambda i, j, kk: (kk, j))
        dims = (((0,), (0,)), ((), ()))
    in_specs = [a_spec, b_spec]
    operands = [a, b]
    if bias is not None:
        in_specs.append(pl.BlockSpec((1, tn), lambda i, j, kk: (0, j)))
        operands.append(bias)
    if add is not None:
        in_specs.append(pl.BlockSpec((tm, tn), lambda i, j, kk: (i, j)))
        operands.append(add)

    def body(a_ref, b_ref, *rest):
        rest = list(rest)
        bias_ref = rest.pop(0) if bias is not None else None
        add_ref = rest.pop(0) if add is not None else None
        o_ref = rest.pop(0)
        prod = lax.dot_general(a_ref[...].astype(BF16), b_ref[...].astype(BF16), dims,
                               preferred_element_type=F32)
        if colsum:
            cs_ref = rest.pop(0)
            part = jnp.sum(b_ref[...].astype(F32), axis=0, keepdims=True)

            @pl.when(pl.program_id(2) == 0)
            def _():
                cs_ref[...] = part

            @pl.when(pl.program_id(2) > 0)
            def _():
                cs_ref[...] += part

        def finish(r):
            if bias_ref is not None:
                r = r + bias_ref[...]
            if add_ref is not None:
                r = r + add_ref[...]
            o_ref[...] = r.astype(o_ref.dtype)

        if nk == 1:
            finish(prod)
            return
        acc_ref = rest.pop(0)
        kk = pl.program_id(2)

        @pl.when(kk == 0)
        def _():
            acc_ref[...] = prod

        @pl.when(kk > 0)
        def _():
            acc_ref[...] += prod

        @pl.when(kk == nk - 1)
        def _():
            finish(acc_ref[...])

    out_specs = pl.BlockSpec((tm, tn), lambda i, j, kk: (i, j))
    out_shape = _sds((m, n), out_dtype)
    if colsum:
        assert mode == "tn" and m == tm
        out_specs = [out_specs, pl.BlockSpec((1, tn), lambda i, j, kk: (0, j))]
        out_shape = [out_shape, _sds((1, n), F32)]
    return _call(body, name=name, grid=(m // tm, n // tn, nk), in_specs=in_specs, out_specs=out_specs,
                 out_shape=out_shape, scratch=[pltpu.VMEM((tm, tn), F32)] if nk > 1 else [],
                 sem=("parallel", "parallel", "arbitrary"))(*operands)


def _colsum(x, *, name):
    rows, n = x.shape
    tr = _pick(rows, (1024, 512, 256, 128))
    tn = _pick(n, (512, 256, 128))

    def body(x_ref, o_ref):
        @pl.when(pl.program_id(1) == 0)
        def _():
            o_ref[...] = jnp.zeros_like(o_ref)

        o_ref[...] += jnp.sum(x_ref[...].astype(F32), axis=0, keepdims=True)

    return _call(body, name=name, grid=(n // tn, rows // tr),
                 in_specs=[pl.BlockSpec((tr, tn), lambda j, i: (i, j))],
                 out_specs=pl.BlockSpec((1, tn), lambda j, i: (0, j)),
                 out_shape=_sds((1, n), F32), sem=("parallel", "arbitrary"))(x)


def _ln_stats(x):
    mu = jnp.mean(x, axis=-1, keepdims=True)
    xc = x - mu
    var = jnp.mean(xc * xc, axis=-1, keepdims=True)
    rstd = lax.rsqrt(var + LN_EPS)
    return xc * rstd, rstd


def _ln_bwd(dn, n, rstd):
    return rstd * (dn - jnp.mean(dn, axis=-1, keepdims=True) - n * jnp.mean(dn * n, axis=-1, keepdims=True))


def _seq_tiles(t, bsz, cands=(1024, 512, 256, 128, 64, 32, 16, 8)):
    s = t // bsz
    ts = _pick(s, cands)
    return s, ts, s // ts


def _ln_mod_fwd(x, scale, shift, bsz, *, name):
    t, d = x.shape
    _, ts, ns = _seq_tiles(t, bsz)

    def body(x_ref, sc_ref, sh_ref, u_ref):
        one_scale = 1.0 + sc_ref[0]
        shift_v = sh_ref[0]

        def piece(rows):
            n, _ = _ln_stats(x_ref[rows, :])
            u_ref[rows, :] = (n * one_scale + shift_v).astype(u_ref.dtype)

        _chunked(ts, piece)

    row = pl.BlockSpec((ts, d), lambda b, i: (b * ns + i, 0))
    per = pl.BlockSpec((1, 1, d), lambda b, i: (b, 0, 0))
    return _call(body, name=name, grid=(bsz, ns), in_specs=[row, per, per], out_specs=row,
                 out_shape=_sds((t, d), BF16), sem=("parallel", "parallel"))(x, scale, shift)


def _ln_mod_bwd(du, x, scale, dr, alpha, bsz, *, name):
    t, d = x.shape
    _, ts, ns = _seq_tiles(t, bsz)

    def body(du_ref, x_ref, sc_ref, dr_ref, dx_ref, dsc_ref, dsh_ref):
        @pl.when(pl.program_id(1) == 0)
        def _():
            dsc_ref[...] = jnp.zeros_like(dsc_ref)
            dsh_ref[...] = jnp.zeros_like(dsh_ref)

        one_scale = 1.0 + sc_ref[0]

        def piece(rows):
            du_v = du_ref[rows, :]
            n, rstd = _ln_stats(x_ref[rows, :])
            dsc_ref[0] += jnp.sum(du_v * n, axis=0, keepdims=True)
            dsh_ref[0] += jnp.sum(du_v, axis=0, keepdims=True)
            dx_ref[rows, :] = alpha * dr_ref[rows, :] + _ln_bwd(du_v * one_scale, n, rstd)

        _chunked(ts, piece)

    row = pl.BlockSpec((ts, d), lambda b, i: (b * ns + i, 0))
    per = pl.BlockSpec((1, 1, d), lambda b, i: (b, 0, 0))
    return _call(body, name=name, grid=(bsz, ns), in_specs=[row, row, per, row],
                 out_specs=[row, per, per],
                 out_shape=[_sds((t, d), F32), _sds((bsz, 1, d), F32), _sds((bsz, 1, d), F32)],
                 sem=("parallel", "arbitrary"))(du, x, scale, dr)


def _ln_res_fwd(x, y, gate, g, b, alpha, bsz, *, name):
    t, d = x.shape
    _, ts, ns = _seq_tiles(t, bsz)

    def body(x_ref, y_ref, gt_ref, g_ref, b_ref, o_ref):
        one_gate = 1.0 + gt_ref[0]

        def piece(rows):
            n, _ = _ln_stats(alpha * x_ref[rows, :] + one_gate * y_ref[rows, :])
            o_ref[rows, :] = n * g_ref[...] + b_ref[...]

        _chunked(ts, piece)

    row = pl.BlockSpec((ts, d), lambda bb, i: (bb * ns + i, 0))
    per = pl.BlockSpec((1, 1, d), lambda bb, i: (bb, 0, 0))
    vec = pl.BlockSpec((1, d), lambda bb, i: (0, 0))
    return _call(body, name=name, grid=(bsz, ns), in_specs=[row, row, per, vec, vec], out_specs=row,
                 out_shape=_sds((t, d), F32), sem=("parallel", "parallel"))(x, y, gate, g, b)


def _ln_res_bwd(do, x, y, gate, g, alpha, bsz, *, name):
    t, d = x.shape
    _, ts, ns = _seq_tiles(t, bsz)

    def body(do_ref, x_ref, y_ref, gt_ref, g_ref, dr_ref, dy_ref, dgt_ref, dg_ref, db_ref, dys_ref):
        first_tile = pl.program_id(1) == 0

        @pl.when(first_tile)
        def _():
            dgt_ref[...] = jnp.zeros_like(dgt_ref)

        @pl.when(jnp.logical_and(first_tile, pl.program_id(0) == 0))
        def _():
            dg_ref[...] = jnp.zeros_like(dg_ref)
            db_ref[...] = jnp.zeros_like(db_ref)
            dys_ref[...] = jnp.zeros_like(dys_ref)

        one_gate = 1.0 + gt_ref[0]

        def piece(rows):
            do_v = do_ref[rows, :]
            y_v = y_ref[rows, :]
            n, rstd = _ln_stats(alpha * x_ref[rows, :] + one_gate * y_v)
            dg_ref[...] += jnp.sum(do_v * n, axis=0, keepdims=True)
            db_ref[...] += jnp.sum(do_v, axis=0, keepdims=True)
            dr = _ln_bwd(do_v * g_ref[...], n, rstd)
            dr_ref[rows, :] = dr
            dy = one_gate * dr
            dy_ref[rows, :] = dy.astype(dy_ref.dtype)
            dys_ref[...] += jnp.sum(dy, axis=0, keepdims=True)
            dgt_ref[0] += jnp.sum(dr * y_v, axis=0, keepdims=True)

        _chunked(ts, piece)

    row = pl.BlockSpec((ts, d), lambda bb, i: (bb * ns + i, 0))
    per = pl.BlockSpec((1, 1, d), lambda bb, i: (bb, 0, 0))
    vec = pl.BlockSpec((1, d), lambda bb, i: (0, 0))
    return _call(body, name=name, grid=(bsz, ns), in_specs=[row, row, row, per, vec],
                 out_specs=[row, row, per, vec, vec, vec],
                 out_shape=[_sds((t, d), F32), _sds((t, d), BF16), _sds((bsz, 1, d), F32),
                            _sds((1, d), F32), _sds((1, d), F32), _sds((1, d), F32)],
                 sem=("arbitrary", "arbitrary"))(do, x, y, gate, g)


def _loss_head(y, target, *, name):
    t, d = y.shape
    tr = _pick(t, (1024, 512, 256, 128, 64, 32, 16, 8))

    def body(y_ref, t_ref, dy_ref, s_ref):
        @pl.when(pl.program_id(0) == 0)
        def _():
            s_ref[...] = jnp.zeros_like(s_ref)

        def piece(rows):
            e = y_ref[rows, :] - t_ref[rows, :]
            dy_ref[rows, :] = e * (1.0 / d)
            s_ref[...] += jnp.sum(e * e, axis=0, keepdims=True)

        _chunked(tr, piece)

    row = pl.BlockSpec((tr, d), lambda i: (i, 0))
    return _call(body, name=name, grid=(t // tr,), in_specs=[row, row],
                 out_specs=[row, pl.BlockSpec((1, d), lambda i: (0, 0))],
                 out_shape=[_sds((t, d), F32), _sds((1, d), F32)], sem=("arbitrary",))(y, target)


def _sigmoid(v):
    return 1.0 / (1.0 + jnp.exp(-v))


def _silu_rows(c, *, name):
    rows, d = c.shape

    def body(c_ref, o_ref):
        v = c_ref[...]
        o_ref[...] = (v * _sigmoid(v)).astype(o_ref.dtype)

    full = pl.BlockSpec((rows, d), lambda i: (0, 0))
    return _call(body, name=name, grid=(1,), in_specs=[full], out_specs=full,
                 out_shape=_sds((rows, d), BF16), sem=("arbitrary",))(c)


def _gate_cols(d, ga_off):
    tc = _pick(math.gcd(d, ga_off), (512, 256, 128))
    return tc, ga_off // tc, (ga_off + d) // tc


def _gate_merge_fwd(z, ya, yb, ga_off, *, name):
    t, d = ya.shape
    tr = _pick(t, (1024, 512, 256, 128, 64, 32, 16, 8))
    tc, ga_blk, gb_blk = _gate_cols(d, ga_off)

    def body(ga_ref, gb_ref, ya_ref, yb_ref, o_ref):
        def piece(rows):
            o_ref[rows, :] = (_sigmoid(ga_ref[rows, :].astype(F32)) * ya_ref[rows, :]
                              + _sigmoid(gb_ref[rows, :].astype(F32)) * yb_ref[rows, :]).astype(o_ref.dtype)

        _chunked(tr, piece)

    blk = pl.BlockSpec((tr, tc), lambda i, j: (i, j))
    return _call(body, name=name, grid=(t // tr, d // tc),
                 in_specs=[pl.BlockSpec((tr, tc), lambda i, j: (i, ga_blk + j)),
                           pl.BlockSpec((tr, tc), lambda i, j: (i, gb_blk + j)), blk, blk],
                 out_specs=blk, out_shape=_sds((t, d), BF16), sem=("parallel", "parallel"))(z, z, ya, yb)


def _gate_merge_bwd(z, ya, yb, dm, ga_off, *, name):
    t, d = ya.shape
    tr = _pick(t, (1024, 512, 256, 128, 64, 32, 16, 8))
    tc, ga_blk, gb_blk = _gate_cols(d, ga_off)

    def body(ga_ref, gb_ref, ya_ref, yb_ref, dm_ref, dya_ref, dyb_ref, dga_ref, dgb_ref):
        def piece(rows):
            dm_v = dm_ref[rows, :]
            sa = _sigmoid(ga_ref[rows, :].astype(F32))
            sb = _sigmoid(gb_ref[rows, :].astype(F32))
            dya_ref[rows, :] = (dm_v * sa).astype(dya_ref.dtype)
            dyb_ref[rows, :] = (dm_v * sb).astype(dyb_ref.dtype)
            dga_ref[rows, :] = (dm_v * ya_ref[rows, :] * sa * (1.0 - sa)).astype(dga_ref.dtype)
            dgb_ref[rows, :] = (dm_v * yb_ref[rows, :] * sb * (1.0 - sb)).astype(dgb_ref.dtype)

        _chunked(tr, piece)

    blk = pl.BlockSpec((tr, tc), lambda i, j: (i, j))
    return _call(body, name=name, grid=(t // tr, d // tc),
                 in_specs=[pl.BlockSpec((tr, tc), lambda i, j: (i, ga_blk + j)),
                           pl.BlockSpec((tr, tc), lambda i, j: (i, gb_blk + j)), blk, blk, blk],
                 out_specs=[blk, blk, blk, blk], out_shape=[_sds((t, d), BF16)] * 4,
                 sem=("parallel", "parallel"))(z, z, ya, yb, dm)


CONV_ROWS = 64
CONV_PAD = 32


def _row_shifts(win):
    total = win.shape[0]
    return [win] + [pltpu.roll(win, total - b, axis=0) for b in range(1, SUBLANES)]


def _shifted_rows(copies, shift):
    start = SUBLANES * (shift // SUBLANES)
    return copies[shift % SUBLANES][start:start + CONV_ROWS]


def _fill_glu(z_ref, ext_ref, s, ch):
    ext_ref[pl.ds(0, CONV_PAD), :] = jnp.zeros((CONV_PAD, ch), F32)

    chunk = min(ROW_CHUNK, s)

    def piece(i, carry):
        start = pl.multiple_of(i * chunk, chunk)
        zz = z_ref[pl.ds(start, chunk), :].astype(F32)
        ext_ref[pl.ds(pl.multiple_of(CONV_PAD + start, CONV_PAD), chunk), :] = zz[:, :ch] * _sigmoid(zz[:, ch:])
        return carry

    lax.fori_loop(0, s // chunk, piece, 0)


def _conv_piece(ext_ref, w_ref, cb_ref, base, kw):
    copies = _row_shifts(ext_ref[pl.ds(base, CONV_ROWS + CONV_PAD), :])
    acc = cb_ref[...] + w_ref[pl.ds(0, 1), :] * _shifted_rows(copies, CONV_PAD - (kw - 1))
    for k in range(1, kw):
        acc = acc + w_ref[pl.ds(k, 1), :] * _shifted_rows(copies, CONV_PAD - (kw - 1) + k)
    return acc, copies


def _conv_branch_fwd(z, w, cb, lg, lb, bsz, ch, *, name):
    t = z.shape[0]
    s = t // bsz
    kw = w.shape[0]

    def body(z_ref, w_ref, cb_ref, lg_ref, lb_ref, o_ref, ext_ref):
        _fill_glu(z_ref, ext_ref, s, ch)

        def step(i, carry):
            base = pl.multiple_of(i * CONV_ROWS, CONV_ROWS)
            a1, _ = _conv_piece(ext_ref, w_ref, cb_ref, base, kw)
            n, _ = _ln_stats(a1)
            a2 = n * lg_ref[...] + lb_ref[...]
            o_ref[pl.ds(base, CONV_ROWS), :] = (a2 * _sigmoid(a2)).astype(o_ref.dtype)
            return carry

        lax.fori_loop(0, s // CONV_ROWS, step, 0)

    vec = pl.BlockSpec((1, ch), lambda b: (0, 0))
    return _call(body, name=name, grid=(bsz,),
                 in_specs=[pl.BlockSpec((s, 2 * ch), lambda b: (b, 0)), pl.BlockSpec((kw, ch), lambda b: (0, 0)),
                           vec, vec, vec],
                 out_specs=pl.BlockSpec((s, ch), lambda b: (b, 0)), out_shape=_sds((t, ch), BF16),
                 scratch=[pltpu.VMEM((CONV_PAD + s, ch), F32)], sem=("parallel",))(z, w, cb, lg, lb)


def _conv_branch_bwd(z, da3, w, cb, lg, lb, bsz, ch, *, name):
    t = z.shape[0]
    s = t // bsz
    kw = w.shape[0]
    n_rows = CONV_ROWS + CONV_PAD

    def body(z_ref, d_ref, w_ref, cb_ref, lg_ref, lb_ref, dz_ref, dw_ref, dcb_ref, dlg_ref, dlb_ref,
             ext_ref, da1_ref):
        @pl.when(pl.program_id(0) == 0)
        def _():
            for ref in (dw_ref, dcb_ref, dlg_ref, dlb_ref):
                ref[...] = jnp.zeros_like(ref)

        _fill_glu(z_ref, ext_ref, s, ch)
        da1_ref[pl.ds(s, CONV_PAD), :] = jnp.zeros((CONV_PAD, ch), F32)

        def grad_a1(i, carry):
            dlg, dlb = carry
            base = pl.multiple_of(i * CONV_ROWS, CONV_ROWS)
            a1, _ = _conv_piece(ext_ref, w_ref, cb_ref, base, kw)
            n, rstd = _ln_stats(a1)
            a2 = n * lg_ref[...] + lb_ref[...]
            sg = _sigmoid(a2)
            da2 = d_ref[pl.ds(base, CONV_ROWS), :] * (sg * (1.0 + a2 * (1.0 - sg)))
            da1_ref[pl.ds(base, CONV_ROWS), :] = _ln_bwd(da2 * lg_ref[...], n, rstd)
            return (dlg + jnp.sum(da2 * n, axis=0, keepdims=True), dlb + jnp.sum(da2, axis=0, keepdims=True))

        zero = jnp.zeros((1, ch), F32)
        dlg, dlb = lax.fori_loop(0, s // CONV_ROWS, grad_a1, (zero, zero))
        dlg_ref[...] += dlg
        dlb_ref[...] += dlb

        def grad_z(i, dcb):
            base = pl.multiple_of(i * CONV_ROWS, CONV_ROWS)
            ahead = _row_shifts(da1_ref[pl.ds(base, n_rows), :])
            dyc = ahead[0][:CONV_ROWS]
            da0 = w_ref[pl.ds(kw - 1, 1), :] * dyc
            for k in range(kw - 1):
                da0 = da0 + w_ref[pl.ds(k, 1), :] * _shifted_rows(ahead, kw - 1 - k)
            behind = _row_shifts(ext_ref[pl.ds(base, n_rows), :])
            for k in range(kw):
                dw_ref[pl.ds(k, 1), :] += jnp.sum(dyc * _shifted_rows(behind, CONV_PAD - (kw - 1) + k),
                                                  axis=0, keepdims=True)
            zz = z_ref[pl.ds(base, CONV_ROWS), :].astype(F32)
            sg = _sigmoid(zz[:, ch:])
            dz_ref[pl.ds(base, CONV_ROWS), :ch] = (da0 * sg).astype(dz_ref.dtype)
            dz_ref[pl.ds(base, CONV_ROWS), ch:] = (da0 * zz[:, :ch] * sg * (1.0 - sg)).astype(dz_ref.dtype)
            return dcb + jnp.sum(dyc, axis=0, keepdims=True)

        dcb_ref[...] += lax.fori_loop(0, s // CONV_ROWS, grad_z, zero)

    vec = pl.BlockSpec((1, ch), lambda b: (0, 0))
    taps = pl.BlockSpec((kw, ch), lambda b: (0, 0))
    return _call(body, name=name, grid=(bsz,),
                 in_specs=[pl.BlockSpec((s, 2 * ch), lambda b: (b, 0)), pl.BlockSpec((s, ch), lambda b: (b, 0)),
                           taps, vec, vec, vec],
                 out_specs=[pl.BlockSpec((s, 2 * ch), lambda b: (b, 0)), taps, vec, vec, vec],
                 out_shape=[_sds((t, 2 * ch), BF16), _sds((kw, ch), F32)] + [_sds((1, ch), F32)] * 3,
                 scratch=[pltpu.VMEM((CONV_PAD + s, ch), F32), pltpu.VMEM((s + CONV_PAD, ch), F32)],
                 sem=("arbitrary",))(z, da3, w, cb, lg, lb)


FFN_ROWS = 64


def _gelu_parts(v):
    cdf = 0.5 * (1.0 + lax.erf(v * (2.0 ** -0.5)))
    return cdf, v * cdf


def _ffn_conv_piece(ext_ref, wb_ref, base):
    win = ext_ref[pl.ds(base, FFN_ROWS + FFN_PAD), :]
    acc = wb_ref[pl.ds(3, 1), :] + wb_ref[pl.ds(2, 1), :] * win[FFN_PAD:]
    acc = acc + wb_ref[pl.ds(1, 1), :] * pltpu.roll(win, 1, axis=0)[FFN_PAD:]
    acc = acc + wb_ref[pl.ds(0, 1), :] * pltpu.roll(win, 2, axis=0)[FFN_PAD:]
    return acc


def _ffn_stage(hg_ref, hl_ref, wg_ref, wl_ref, bg_ref, bl_ref, ext_ref, wb_ref, s, tcf):
    ext_ref[pl.ds(0, FFN_PAD), :] = jnp.zeros((FFN_PAD, 2 * tcf), F32)
    ext_ref[pl.ds(FFN_PAD, s), :tcf] = hg_ref[...].astype(F32)
    ext_ref[pl.ds(FFN_PAD, s), tcf:] = hl_ref[...].astype(F32)
    wb_ref[pl.ds(0, 3), :tcf] = wg_ref[...]
    wb_ref[pl.ds(0, 3), tcf:] = wl_ref[...]
    wb_ref[pl.ds(3, 1), :tcf] = bg_ref[...]
    wb_ref[pl.ds(3, 1), tcf:] = bl_ref[...]


def _ffn_specs(s, tcf, n_f, batch_first):
    def spec(rows, shift):
        if batch_first:
            return pl.BlockSpec((rows, tcf), lambda bb, j: (bb if rows == s else 0, shift + j))
        return pl.BlockSpec((rows, tcf), lambda j, bb: (bb if rows == s else 0, shift + j))

    return [spec(s, 0), spec(s, n_f), spec(3, 0), spec(3, n_f), spec(1, 0), spec(1, n_f)]


def _ffn_act_fwd(hp, w, b, bsz, tcf, *, name):
    t, two_f = hp.shape
    s = t // bsz
    n_f = two_f // (2 * tcf)

    def body(hg_ref, hl_ref, wg_ref, wl_ref, bg_ref, bl_ref, f_ref, ext_ref, wb_ref):
        _ffn_stage(hg_ref, hl_ref, wg_ref, wl_ref, bg_ref, bl_ref, ext_ref, wb_ref, s, tcf)

        def step(i, carry):
            base = pl.multiple_of(i * FFN_ROWS, FFN_ROWS)
            hh = _ffn_conv_piece(ext_ref, wb_ref, base)
            _, gelu = _gelu_parts(hh[:, :tcf])
            f_ref[pl.ds(base, FFN_ROWS), :] = (gelu * hh[:, tcf:]).astype(f_ref.dtype)
            return carry

        lax.fori_loop(0, s // FFN_ROWS, step, 0)

    return _call(body, name=name, grid=(bsz, n_f), in_specs=_ffn_specs(s, tcf, n_f, True),
                 out_specs=pl.BlockSpec((s, tcf), lambda bb, j: (bb, j)),
                 out_shape=_sds((t, two_f // 2), BF16),
                 scratch=[pltpu.VMEM((FFN_PAD + s, 2 * tcf), F32), pltpu.VMEM((SUBLANES, 2 * tcf), F32)],
                 sem=("parallel", "parallel"))(hp, hp, w, w, b, b)


def _ffn_act_bwd(hp, df, w, b, bsz, tcf, *, name):
    t, two_f = hp.shape
    s = t // bsz
    f_dim = two_f // 2
    n_f = f_dim // tcf
    gw = 2 * tcf
    n_rows = FFN_ROWS + FFN_PAD

    def body(hg_ref, hl_ref, wg_ref, wl_ref, bg_ref, bl_ref, df_ref,
             dhg_ref, dhl_ref, dwg_ref, dwl_ref, dbg_ref, dbl_ref, ext_ref, wb_ref, dh_ref):
        @pl.when(pl.program_id(1) == 0)
        def _():
            for ref in (dwg_ref, dwl_ref, dbg_ref, dbl_ref):
                ref[...] = jnp.zeros_like(ref)

        _ffn_stage(hg_ref, hl_ref, wg_ref, wl_ref, bg_ref, bl_ref, ext_ref, wb_ref, s, tcf)
        dh_ref[pl.ds(s, FFN_PAD), :] = jnp.zeros((FFN_PAD, gw), F32)

        def grad_h(i, carry):
            base = pl.multiple_of(i * FFN_ROWS, FFN_ROWS)
            hh = _ffn_conv_piece(ext_ref, wb_ref, base)
            hg = hh[:, :tcf]
            d = df_ref[pl.ds(base, FFN_ROWS), :]
            cdf, gelu = _gelu_parts(hg)
            pdf = jnp.exp(-0.5 * hg * hg) * (1.0 / math.sqrt(2.0 * math.pi))
            dh_ref[pl.ds(base, FFN_ROWS), :tcf] = d * hh[:, tcf:] * (cdf + hg * pdf)
            dh_ref[pl.ds(base, FFN_ROWS), tcf:] = d * gelu
            return carry

        lax.fori_loop(0, s // FFN_ROWS, grad_h, 0)

        def grad_x(i, carry):
            dw0, dw1, dw2, dbs = carry
            base = pl.multiple_of(i * FFN_ROWS, FFN_ROWS)
            nxt = dh_ref[pl.ds(base, n_rows), :]
            dyc = nxt[:FFN_ROWS]
            dx = wb_ref[pl.ds(2, 1), :] * dyc
            dx = dx + wb_ref[pl.ds(1, 1), :] * pltpu.roll(nxt, n_rows - 1, axis=0)[:FFN_ROWS]
            dx = dx + wb_ref[pl.ds(0, 1), :] * pltpu.roll(nxt, n_rows - 2, axis=0)[:FFN_ROWS]
            dhg_ref[pl.ds(base, FFN_ROWS), :] = dx[:, :tcf].astype(dhg_ref.dtype)
            dhl_ref[pl.ds(base, FFN_ROWS), :] = dx[:, tcf:].astype(dhl_ref.dtype)
            win = ext_ref[pl.ds(base, n_rows), :]
            dw2 = dw2 + jnp.sum(dyc * win[FFN_PAD:], axis=0, keepdims=True)
            dw1 = dw1 + jnp.sum(dyc * pltpu.roll(win, 1, axis=0)[FFN_PAD:], axis=0, keepdims=True)
            dw0 = dw0 + jnp.sum(dyc * pltpu.roll(win, 2, axis=0)[FFN_PAD:], axis=0, keepdims=True)
            return dw0, dw1, dw2, dbs + jnp.sum(dyc, axis=0, keepdims=True)

        zero = jnp.zeros((1, gw), F32)
        sums = lax.fori_loop(0, s // FFN_ROWS, grad_x, (zero, zero, zero, zero))
        for k in range(3):
            dwg_ref[pl.ds(k, 1), :] += sums[k][:, :tcf]
            dwl_ref[pl.ds(k, 1), :] += sums[k][:, tcf:]
        dbg_ref[...] += sums[3][:, :tcf]
        dbl_ref[...] += sums[3][:, tcf:]

    half = pl.BlockSpec((s, tcf), lambda j, bb: (bb, j))
    taps = pl.BlockSpec((3, tcf), lambda j, bb: (0, j))
    bias = pl.BlockSpec((1, tcf), lambda j, bb: (0, j))
    return _call(body, name=name, grid=(n_f, bsz), in_specs=_ffn_specs(s, tcf, n_f, False) + [half],
                 out_specs=[half, half, taps, taps, bias, bias],
                 out_shape=[_sds((t, f_dim), BF16)] * 2 + [_sds((3, f_dim), F32)] * 2 + [_sds((1, f_dim), F32)] * 2,
                 scratch=[pltpu.VMEM((FFN_PAD + s, gw), F32), pltpu.VMEM((SUBLANES, gw), F32),
                          pltpu.VMEM((s + FFN_PAD, gw), F32)],
                 sem=("parallel", "arbitrary"))(hp, hp, w, w, b, b, df)


def _split3(v):
    hi = v.astype(BF16)
    r = v - hi.astype(F32)
    mid = r.astype(BF16)
    lo = (r - mid.astype(F32)).astype(BF16)
    return hi, mid, lo


def _tri_dot(tri, v):
    out = None
    for part in _split3(v):
        term = jnp.dot(tri, part, preferred_element_type=F32)
        out = term if out is None else out + term
    return out


def _fgate_fwd(zf, bsz, heads, *, name):
    t, lanes = zf.shape
    s, blk, nb = _seq_tiles(t, bsz, (ATTN_BLOCK, 128))

    def body(z_ref, cumt_ref, cumb_ref, carry_ref):
        @pl.when(pl.program_id(1) == 0)
        def _():
            carry_ref[...] = jnp.zeros_like(carry_ref)

        z = z_ref[...]
        lf = jnp.minimum(z, 0.0) - jnp.log1p(jnp.exp(-jnp.abs(z)))
        r = lax.broadcasted_iota(jnp.int32, (blk, blk), 0)
        c = lax.broadcasted_iota(jnp.int32, (blk, blk), 1)
        tri = (r >= c).astype(BF16)
        cum = _tri_dot(tri, lf) + carry_ref[...]
        carry_ref[...] = cum[blk - 1:blk, :]
        cumt_ref[0] = jnp.transpose(cum)[:heads, :]
        for h in range(heads):
            cumb_ref[0, h] = jnp.broadcast_to(cum[:, h:h + 1], (blk, lanes))

    return _call(body, name=name, grid=(bsz, nb),
                 in_specs=[pl.BlockSpec((blk, lanes), lambda b, i: (b * nb + i, 0))],
                 out_specs=[pl.BlockSpec((1, heads, blk), lambda b, i: (b, 0, i)),
                            pl.BlockSpec((1, heads, blk, lanes), lambda b, i: (b, 0, i, 0))],
                 out_shape=[_sds((bsz, heads, s), F32), _sds((bsz, heads, s, lanes), F32)],
                 scratch=[pltpu.VMEM((1, lanes), F32)], sem=("parallel", "arbitrary"))(zf)


def _fgate_bwd(dcum, zf, bsz, *, name):
    t, lanes = zf.shape
    pairs = dcum.shape[1]
    s, blk, nb = _seq_tiles(t, bsz, (ATTN_BLOCK, 128))

    def body(d_ref, z_ref, o_ref, carry_ref):
        @pl.when(pl.program_id(1) == 0)
        def _():
            carry_ref[...] = jnp.zeros_like(carry_ref)

        dcol = d_ref[0, 0]
        for p in range(1, pairs):
            dcol = dcol + d_ref[0, p]
        r = lax.broadcasted_iota(jnp.int32, (blk, blk), 0)
        c = lax.broadcasted_iota(jnp.int32, (blk, blk), 1)
        tri = (c >= r).astype(BF16)
        suf = _tri_dot(tri, dcol) + carry_ref[...]
        carry_ref[...] = suf[0:1, :]
        o_ref[...] = suf * _sigmoid(-z_ref[...])

    return _call(body, name=name, grid=(bsz, nb),
                 in_specs=[pl.BlockSpec((1, pairs, blk, lanes), lambda b, i: (b, 0, nb - 1 - i, 0)),
                           pl.BlockSpec((blk, lanes), lambda b, i: (b * nb + nb - 1 - i, 0))],
                 out_specs=pl.BlockSpec((blk, lanes), lambda b, i: (b * nb + nb - 1 - i, 0)),
                 out_shape=_sds((t, lanes), F32), scratch=[pltpu.VMEM((1, lanes), F32)],
                 sem=("parallel", "arbitrary"))(dcum, zf)


def _to_features_major(z, col_off, width, n, *, name):
    t = z.shape[0]
    tr = _pick(t, (512, 256, 128))
    first = col_off // width

    def body(*refs):
        o_ref = refs[n]
        for g in range(n):
            o_ref[pl.ds(g * width, width), :] = jnp.transpose(refs[g][...].astype(F32)).astype(o_ref.dtype)

    return _call(body, name=name, grid=(t // tr,),
                 in_specs=[pl.BlockSpec((tr, width), lambda i, g=g: (i, first + g)) for g in range(n)],
                 out_specs=pl.BlockSpec((n * width, tr), lambda i: (0, i)),
                 out_shape=_sds((n * width, t), BF16), sem=("parallel",))(*([z] * n))


def _to_rows_major(xt, *, name):
    w, t = xt.shape
    tr = _pick(t, (512, 256, 128))

    def body(x_ref, o_ref):
        o_ref[...] = jnp.transpose(x_ref[...]).astype(o_ref.dtype)

    return _call(body, name=name, grid=(t // tr,),
                 in_specs=[pl.BlockSpec((w, tr), lambda i: (0, i))],
                 out_specs=pl.BlockSpec((tr, w), lambda i: (i, 0)),
                 out_shape=_sds((t, w), BF16), sem=("parallel",))(xt)


def _loop_by_twos(lo, hi, body, carry):
    count = hi - lo

    def group(n, first, cr):
        for u in range(n):
            cr = body(first + u, cr)
        return cr

    trips = count // ATTN_UNROLL
    carry = lax.fori_loop(0, trips, lambda t, cr: group(ATTN_UNROLL, lo + ATTN_UNROLL * t, cr), carry)
    rest = count - ATTN_UNROLL * trips
    first = lo + ATTN_UNROLL * trips
    for n in range(ATTN_UNROLL - 1, 0, -1):
        carry = lax.cond(rest == n, lambda cr, n=n: group(n, first, cr), lambda cr: cr, carry)
    return carry


def _head_masks(shape, axis):
    feat = lax.broadcasted_iota(jnp.int32, shape, axis)
    return feat < HEAD_DIM, feat >= HEAD_DIM


def _attn_fwd(z, qkvt, cumt, cumb, bsz, heads, q_off, *, name):
    t = z.shape[0]
    width = heads * HEAD_DIM
    pairs = heads // 2
    s = t // bsz
    blk = ATTN_BLOCK
    nq = s // blk
    k_col = (q_off + width) // LANES
    v_row = 2 * width // LANES
    reps = blk // LANES

    def body(k_ref, qt_ref, vt_ref, cqt_ref, ckb_ref, ot_ref, lse_ref):
        p_id = pl.program_id(1)
        i = pl.program_id(2)
        qt = qt_ref[...]
        masks = _head_masks((LANES, blk), 0)
        qtm = [jnp.where(mk, qt, jnp.zeros_like(qt)) for mk in masks]
        cq = [cqt_ref[0, pl.ds(2 * p_id + hh, 1), :] for hh in range(2)]
        kidx = lax.broadcasted_iota(jnp.int32, (blk, blk), 0)
        qidx = lax.broadcasted_iota(jnp.int32, (blk, blk), 1)

        def block(j, carry, masked):
            off = pl.multiple_of(j * blk, blk)
            kp = k_ref[pl.ds(off, blk), :].astype(BF16)
            vtp = vt_ref[:, pl.ds(off, blk)]
            out = []
            for hh in range(2):
                m, l, acc = carry[hh]
                sc = jnp.dot(kp, qtm[hh], preferred_element_type=F32) * ATTN_SCALE
                ck = ckb_ref[0, hh, pl.ds(off, blk), :]
                sc = (sc + cq[hh]) - jnp.concatenate([ck] * reps, axis=1)
                if masked:
                    sc = jnp.where(qidx >= kidx, sc, NEG)
                m_new = jnp.maximum(m, jnp.max(sc, axis=0, keepdims=True))
                pr = jnp.exp(sc - m_new)
                a = jnp.exp(m - m_new)
                l = a * l + jnp.sum(pr, axis=0, keepdims=True)
                p_hi = pr.astype(BF16)
                p_lo = (pr - p_hi.astype(F32)).astype(BF16)
                pv = (jnp.dot(vtp, p_hi, preferred_element_type=F32)
                      + jnp.dot(vtp, p_lo, preferred_element_type=F32))
                acc = a * acc + pv[hh * HEAD_DIM:(hh + 1) * HEAD_DIM]
                out.append((m_new, l, acc))
            return tuple(out)

        init = tuple((jnp.full((1, blk), NEG, F32), jnp.zeros((1, blk), F32), jnp.zeros((HEAD_DIM, blk), F32))
                     for _ in range(2))
        carry = _loop_by_twos(0, i, lambda j, cr: block(j, cr, False), init)
        carry = block(i, carry, True)
        lse_ref[...] = jnp.zeros_like(lse_ref)
        for hh in range(2):
            m, l, acc = carry[hh]
            ot_ref[pl.ds(hh * HEAD_DIM, HEAD_DIM), :] = acc / l
            lse_ref[0, 0, pl.ds(hh, 1), :] = m + jnp.log(l)

    return _call(body, name=name, grid=(bsz, pairs, nq),
                 in_specs=[pl.BlockSpec((s, LANES), lambda b, p, i: (b, k_col + p)),
                           pl.BlockSpec((LANES, blk), lambda b, p, i: (p, b * nq + i)),
                           pl.BlockSpec((LANES, s), lambda b, p, i: (v_row + p, b)),
                           pl.BlockSpec((1, heads, blk), lambda b, p, i: (b, 0, i)),
                           pl.BlockSpec((1, 2, s, LANES), lambda b, p, i: (b, p, 0, 0))],
                 out_specs=[pl.BlockSpec((LANES, blk), lambda b, p, i: (p, b * nq + i)),
                            pl.BlockSpec((1, 1, SUBLANES, blk), lambda b, p, i: (b, p, 0, i))],
                 out_shape=[_sds((width, t), F32), _sds((bsz, pairs, SUBLANES, s), F32)],
                 sem=("parallel", "parallel", "parallel"))(z, qkvt, qkvt, cumt, cumb)


def _attn_bwd(z, qkvt, cumt, cumb, ot, do, dot, lse, bsz, heads, q_off, *, name):
    t = z.shape[0]
    width = heads * HEAD_DIM
    pairs = heads // 2
    s = t // bsz
    blk = ATTN_BLOCK
    nkv = s // blk
    q_col = q_off // LANES
    k_col = (q_off + width) // LANES
    v_col = (q_off + 2 * width) // LANES
    k_row = width // LANES
    reps = blk // LANES

    def body(k_ref, v_ref, kt_ref, q_ref, qt_ref, do_ref, dot_ref, ot_ref, lse_ref, ckb_ref, cqt_ref,
             dk_ref, dv_ref, dqt_ref, dcum_ref, dqt_acc, ds_acc):
        p_id = pl.program_id(1)
        j = pl.program_id(2)

        @pl.when(j == 0)
        def _():
            dqt_acc[...] = jnp.zeros_like(dqt_acc)

        kp = k_ref[...].astype(BF16)
        vp = v_ref[...].astype(BF16)
        kt = kt_ref[...]
        feat_masks = _head_masks((LANES, blk), 0)
        lane_masks = _head_masks((blk, LANES), 1)
        ktm = [jnp.where(mk, kt, jnp.zeros_like(kt)) for mk in feat_masks]
        ck = [jnp.concatenate([ckb_ref[0, hh]] * reps, axis=1) for hh in range(2)]
        kidx = lax.broadcasted_iota(jnp.int32, (blk, blk), 0)
        qidx = lax.broadcasted_iota(jnp.int32, (blk, blk), 1)
        ds_acc[...] = jnp.zeros_like(ds_acc)

        def block(i, carry, masked):
            dk, dv = carry
            off = pl.multiple_of(i * blk, blk)
            qt = qt_ref[:, pl.ds(off, blk)]
            dt = dot_ref[:, pl.ds(off, blk)]
            o_t = ot_ref[:, pl.ds(off, blk)]
            q_rows = q_ref[pl.ds(off, blk), :].astype(BF16)
            do_rows = do_ref[pl.ds(off, blk), :]
            for hh in range(2):
                qtm = jnp.where(feat_masks[hh], qt, jnp.zeros_like(qt))
                dtm = jnp.where(feat_masks[hh], dt, jnp.zeros_like(dt))
                sc = jnp.dot(kp, qtm, preferred_element_type=F32) * ATTN_SCALE
                sc = (sc + cqt_ref[0, pl.ds(2 * p_id + hh, 1), pl.ds(off, blk)]) - ck[hh]
                pr = jnp.exp(sc - lse_ref[0, 0, pl.ds(hh, 1), pl.ds(off, blk)])
                if masked:
                    pr = jnp.where(qidx >= kidx, pr, 0.0)
                dp = jnp.dot(vp, dtm, preferred_element_type=F32)
                delta = jnp.sum(dtm.astype(F32) * o_t, axis=0, keepdims=True)
                ds = pr * (dp - delta)
                ds_acc[hh] += ds
                dsb = ds.astype(BF16)
                qm = jnp.where(lane_masks[hh], q_rows, jnp.zeros_like(q_rows))
                dom = jnp.where(lane_masks[hh], do_rows, jnp.zeros_like(do_rows))
                dv = dv + jnp.dot(pr.astype(BF16), dom, preferred_element_type=F32)
                dk = dk + jnp.dot(dsb, qm, preferred_element_type=F32) * ATTN_SCALE
                dqt_acc[:, pl.ds(off, blk)] += jnp.dot(ktm[hh], dsb, preferred_element_type=F32) * ATTN_SCALE
            return dk, dv

        zero = jnp.zeros((blk, LANES), F32)
        carry = block(j, (zero, zero), True)
        dk, dv = _loop_by_twos(j + 1, nkv, lambda i, cr: block(i, cr, False), carry)
        dk_ref[...] = dk.astype(dk_ref.dtype)
        dv_ref[...] = dv.astype(dv_ref.dtype)
        lane = lax.broadcasted_iota(jnp.int32, (blk, LANES), 1)
        dcum = jnp.zeros((blk, LANES), F32)
        for hh in range(2):
            col = jnp.sum(ds_acc[hh], axis=1, keepdims=True)
            dcum = jnp.where(lane == 2 * p_id + hh, -col, dcum)
        dcum_ref[0, 0] = dcum

        @pl.when(j == nkv - 1)
        def _():
            dqt_ref[...] = dqt_acc[...]

    key_rows = lambda col: pl.BlockSpec((blk, LANES), lambda b, p, j: (b * nkv + j, col + p))
    seq_t = lambda row: pl.BlockSpec((LANES, s), lambda b, p, j: (row + p, b))
    return _call(body, name=name, grid=(bsz, pairs, nkv),
                 in_specs=[key_rows(k_col), key_rows(v_col),
                           pl.BlockSpec((LANES, blk), lambda b, p, j: (k_row + p, b * nkv + j)),
                           pl.BlockSpec((s, LANES), lambda b, p, j: (b, q_col + p)), seq_t(0),
                           pl.BlockSpec((s, LANES), lambda b, p, j: (b, p)), seq_t(0), seq_t(0),
                           pl.BlockSpec((1, 1, SUBLANES, s), lambda b, p, j: (b, p, 0, 0)),
                           pl.BlockSpec((1, 2, blk, LANES), lambda b, p, j: (b, p, j, 0)),
                           pl.BlockSpec((1, heads, s), lambda b, p, j: (b, 0, 0))],
                 out_specs=[key_rows(0), key_rows(0), seq_t(0),
                            pl.BlockSpec((1, 1, blk, LANES), lambda b, p, j: (b, p, j, 0))],
                 out_shape=[_sds((t, width), BF16), _sds((t, width), BF16), _sds((width, t), F32),
                            _sds((bsz, pairs, s, LANES), F32)],
                 scratch=[pltpu.VMEM((LANES, s), F32), pltpu.VMEM((2, blk, blk), F32)],
                 sem=("parallel", "parallel", "arbitrary"))(z, z, qkvt, z, qkvt, do, dot, ot, lse, cumb, cumt)


def _adamw(w, g, m, v, *, name):
    bc1 = 1.0 - ADAM_B1 ** ADAM_STEP
    bc2 = 1.0 - ADAM_B2 ** ADAM_STEP

    def body(w_ref, g_ref, m_ref, v_ref, d_ref, nm_ref, nv_ref):
        g_v = g_ref[...]
        nm = ADAM_B1 * m_ref[...] + (1.0 - ADAM_B1) * g_v
        nv = ADAM_B2 * v_ref[...] + (1.0 - ADAM_B2) * (g_v * g_v)
        nm_ref[...] = nm
        nv_ref[...] = nv
        d_ref[...] = -ADAM_LR * ((nm / bc1) / (jnp.sqrt(nv / bc2) + ADAM_EPS) + ADAM_WD * w_ref[...])

    if w.ndim == 2:
        grid = (1,)
        blk = pl.BlockSpec(w.shape, lambda i: (0, 0))
    else:
        layers, rows, cols = w.shape
        tr = rows if rows <= 256 else _pick(rows, (256, 128, 64, 32, 16, 8))
        grid = (layers, rows // tr)
        blk = pl.BlockSpec((1, tr, cols), lambda layer, i: (layer, i, 0))
    return tuple(_call(body, name=name, grid=grid, in_specs=[blk] * 4, out_specs=[blk] * 3,
                       out_shape=[_sds(w.shape, F32)] * 3, sem=("parallel",) * len(grid))(w, g, m, v))


_ANY = pl.BlockSpec(memory_space=pl.ANY)


def _comm_call(body, *, name, n_in, out_shape, n_sems):
    scratch = [pltpu.SemaphoreType.DMA((n_sems,)), pltpu.SemaphoreType.DMA((n_sems,)),
               pltpu.SemaphoreType.DMA((len(out_shape),))]
    return pl.pallas_call(body, name=name, in_specs=[_ANY] * n_in, out_specs=[_ANY] * len(out_shape),
                          out_shape=out_shape, scratch_shapes=scratch)


def _place():
    x, y, c = lax.axis_index("x"), lax.axis_index("y"), lax.axis_index("c")
    return x, y, c, [(1 - x, y), (x, 1 - y), (1 - x, 1 - y)]


def _remote(src, dst, send_sems, recv_sems, sem, to):
    return pltpu.make_async_remote_copy(src_ref=src, dst_ref=dst, send_sem=send_sems.at[sem],
                                        recv_sem=recv_sems.at[sem], device_id=to, device_id_type=MESH)


def _all_gather8(v, *, name):
    def body(v_ref, out_ref, send_sems, recv_sems, local_sems):
        x, y, c, _ = _place()
        me = 4 * x + 2 * y + c
        mine = pltpu.make_async_copy(v_ref, out_ref.at[me], local_sems.at[0])
        mine.start()
        peers = []
        for k in range(1, N_DEVICES):
            px = 1 - x if k & 4 else x
            py = 1 - y if k & 2 else y
            pc = 1 - c if k & 1 else c
            peers.append((px, py, pc))
        sends = [_remote(v_ref, out_ref.at[me], send_sems, recv_sems, k, peer) for k, peer in enumerate(peers)]
        for cp in sends:
            cp.start()
        for k, (px, py, pc) in enumerate(peers):
            _remote(v_ref, out_ref.at[4 * px + 2 * py + pc], send_sems, recv_sems, k, (px, py, pc)).wait_recv()
        for cp in sends:
            cp.wait_send()
        mine.wait()

    out = _comm_call(body, name=name, n_in=1, out_shape=[_sds((N_DEVICES,) + v.shape, v.dtype)],
                     n_sems=N_DEVICES - 1)(v)
    return out[0]


def _window(ref, mode, layer, chip, rows, cols):
    if mode == "slab":
        return ref.at[layer, chip]
    if mode == "cols":
        return ref.at[layer, :, pl.ds(pl.multiple_of(chip * cols, LANES), cols)]
    return ref.at[layer, pl.ds(pl.multiple_of(chip * rows, SUBLANES), rows), :]


def _whole_shape(mode, shard_shape):
    layers, rows, cols = shard_shape
    if mode == "slab":
        return (layers, N_CHIPS, rows, cols)
    if mode == "cols":
        assert cols % LANES == 0
        return (layers, rows, N_CHIPS * cols)
    assert rows % 16 == 0
    return (layers, N_CHIPS * rows, cols)


def _gather_weights(shards, modes, *, name):
    n = len(shards)
    meta = [(mode,) + tuple(a.shape[1:]) for a, mode in zip(shards, modes)]
    for a in shards:
        assert a.shape[0] == 2
    per = 7

    def body(*refs):
        ins, outs = refs[:n], refs[n:2 * n]
        send_sems, recv_sems, _ = refs[2 * n:]
        x, y, c, chips = _place()
        me = 2 * x + y
        sibling = (x, y, 1 - c)
        own, first, passed = [], [], []
        for i, (mode, rows, cols) in enumerate(meta):
            for r, (cx, cy) in enumerate(chips):
                cp = _remote(ins[i].at[c], _window(outs[i], mode, c, me, rows, cols), send_sems, recv_sems,
                             per * i + r, (cx, cy, c))
                cp.start()
                first.append(cp)
            cp = _remote(ins[i], _window(outs[i], mode, slice(None), me, rows, cols), send_sems, recv_sems,
                         per * i + 6, sibling)
            cp.start()
            own.append(cp)
        for i, (mode, rows, cols) in enumerate(meta):
            for r, (cx, cy) in enumerate(chips):
                win = _window(outs[i], mode, c, 2 * cx + cy, rows, cols)
                _remote(win, win, send_sems, recv_sems, per * i + r, (cx, cy, c)).wait_recv()
                cp = _remote(win, win, send_sems, recv_sems, per * i + 3 + r, sibling)
                cp.start()
                passed.append(cp)
        for i, (mode, rows, cols) in enumerate(meta):
            own[i].wait_recv()
            for r, (cx, cy) in enumerate(chips):
                win = _window(outs[i], mode, 1 - c, 2 * cx + cy, rows, cols)
                _remote(win, win, send_sems, recv_sems, per * i + 3 + r, sibling).wait_recv()
        for cp in first + passed + own:
            cp.wait_send()

    out_shape = [_sds(_whole_shape(mode, a.shape), a.dtype) for a, mode in zip(shards, modes)]
    return _comm_call(body, name=name, n_in=n, out_shape=out_shape, n_sems=per * n)(*shards)


def _rs_swap(grads, *, name):
    n = len(grads)

    def body(*refs):
        ins, outs = refs[:n], refs[n:2 * n]
        send_sems, recv_sems, _ = refs[2 * n:]
        x, y, c, _ = _place()
        copies = [_remote(ins[i].at[1 - c], outs[i], send_sems, recv_sems, i, (x, y, 1 - c)) for i in range(n)]
        for cp in copies:
            cp.start()
        for cp in copies:
            cp.wait()

    return _comm_call(body, name=name, n_in=n, out_shape=[_sds(g.shape[1:], g.dtype) for g in grads], n_sems=n)(*grads)


def _part(ref, mode, chip, rows, cols):
    if mode == "slab":
        return ref.at[chip]
    if mode == "cols":
        return ref.at[:, pl.ds(pl.multiple_of(chip * cols, LANES), cols)]
    return ref.at[pl.ds(pl.multiple_of(chip * rows, SUBLANES), rows), :]


def _rs_scatter(parts, modes, shard_shapes, *, name):
    n = len(parts)
    meta = [(mode,) + tuple(shp[1:]) for mode, shp in zip(modes, shard_shapes)]

    def body(*refs):
        ins, outs = refs[:n], refs[n:2 * n]
        send_sems, recv_sems, local_sems = refs[2 * n:]
        x, y, c, chips = _place()
        me = 2 * x + y
        local, sends = [], []
        for i, (mode, rows, cols) in enumerate(meta):
            cp = pltpu.make_async_copy(_part(ins[i], mode, me, rows, cols), outs[i].at[me], local_sems.at[i])
            cp.start()
            local.append(cp)
            for r, (cx, cy) in enumerate(chips):
                cp = _remote(_part(ins[i], mode, 2 * cx + cy, rows, cols), outs[i].at[me], send_sems, recv_sems,
                             3 * i + r, (cx, cy, c))
                cp.start()
                sends.append(cp)
        for i, (mode, rows, cols) in enumerate(meta):
            for r, (cx, cy) in enumerate(chips):
                k = 2 * cx + cy
                _remote(_part(ins[i], mode, k, rows, cols), outs[i].at[k], send_sems, recv_sems, 3 * i + r,
                        (cx, cy, c)).wait_recv()
        for cp in sends:
            cp.wait_send()
        for cp in local:
            cp.wait()

    out_shape = [_sds((N_CHIPS,) + tuple(shp[1:]), p.dtype) for p, shp in zip(parts, shard_shapes)]
    return _comm_call(body, name=name, n_in=n, out_shape=out_shape, n_sems=3 * n)(*parts)


def _rs_exchange(sums, *, name):
    n = len(sums)

    def body(*refs):
        ins, outs = refs[:n], refs[n:2 * n]
        send_sems, recv_sems, _ = refs[2 * n:]
        x, y, c, _ = _place()
        copies = [_remote(ins[i], outs[i], send_sems, recv_sems, i, (x, y, 1 - c)) for i in range(n)]
        for cp in copies:
            cp.start()
        for cp in copies:
            cp.wait()

    return _comm_call(body, name=name, n_in=n, out_shape=[_sds(s.shape, s.dtype) for s in sums], n_sems=n)(*sums)


def _row_tile(rows, cols, itemsize):
    target = max(SUBLANES, (2 << 20) // (cols * itemsize))
    cands = [c for c in (2048, 1024, 512, 256, 128, 64, 32, 16) if c <= target]
    tr = _pick(rows, cands)
    return tr


def _add_layer(g, other, core, *, name):
    _, rows, cols = g.shape
    tr = _row_tile(rows, cols, 4)

    def body(core_ref, g_ref, o_ref, out_ref):
        out_ref[...] = (g_ref[0] + o_ref[...]).astype(out_ref.dtype)

    grid_spec = pltpu.PrefetchScalarGridSpec(
        num_scalar_prefetch=1, grid=(rows // tr,),
        in_specs=[pl.BlockSpec((1, tr, cols), lambda i, core_ref: (core_ref[0], i, 0)),
                  pl.BlockSpec((tr, cols), lambda i, core_ref: (i, 0))],
        out_specs=pl.BlockSpec((tr, cols), lambda i, core_ref: (i, 0)))
    return pl.pallas_call(body, name=name, grid_spec=grid_spec, out_shape=_sds((rows, cols), BF16),
                          compiler_params=pltpu.CompilerParams(dimension_semantics=("parallel",),
                                                               vmem_limit_bytes=VMEM_LIMIT))(core, g, other)


def _sum_slots(parts, *, name):
    n, rows, cols = parts.shape
    tr = _row_tile(rows, cols, 4)

    def body(p_ref, o_ref):
        acc = p_ref[0].astype(F32) + p_ref[1].astype(F32)
        for k in range(2, n):
            acc = acc + p_ref[k].astype(F32)
        o_ref[...] = acc

    return _call(body, name=name, grid=(rows // tr,),
                 in_specs=[pl.BlockSpec((n, tr, cols), lambda i: (0, i, 0))],
                 out_specs=pl.BlockSpec((tr, cols), lambda i: (i, 0)),
                 out_shape=_sds((rows, cols), F32), sem=("parallel",))(parts)


def _reduce_scatter(grads, modes, shard_shapes):
    core = lax.axis_index("c").astype(jnp.int32).reshape(1)
    flat = [g.reshape(g.shape[0], -1, g.shape[-1]) for g in grads]
    from_sibling = _rs_swap(flat, name="rs_swap")
    parts = []
    for i, (g, o) in enumerate(zip(flat, from_sibling)):
        p = _add_layer(g, o, core, name=f"rs_add_{i}")
        parts.append(p.reshape(grads[i].shape[1:]))
    from_chips = _rs_scatter(parts, modes, shard_shapes, name="rs_scatter")
    sums = [_sum_slots(r, name=f"rs_sum_{i}") for i, r in enumerate(from_chips)]
    others = _rs_exchange(sums, name="rs_exchange")
    mine_first = lax.axis_index("c") == 0
    return [jnp.where(mine_first, jnp.stack([mine, other]), jnp.stack([other, mine]))
            for mine, other in zip(sums, others)]


def _layer_weights(full, rep, layer, dims):
    f_off, n_heads = dims["f_off"], dims["heads"]
    d_ff = full["w_ffn_up"].shape[-1] // 2
    b_in = rep["b_in"][layer]
    pad = LANES - n_heads
    return {
        "w_main": full["w_main"][layer],
        "b_main": jnp.concatenate([b_in[:f_off], b_in[f_off + n_heads:]])[None],
        "w_f": full["w_f"][layer],
        "b_f": jnp.pad(b_in[f_off:f_off + n_heads], (0, pad))[None],
        "conv_a_w": full["conv_a_w"][layer],
        "conv_a_b": rep["conv_a_b"][layer][None],
        "ln_conv_g": rep["ln_conv_g"][layer][None],
        "ln_conv_b": rep["ln_conv_b"][layer][None],
        "w_conv_proj": full["w_conv_proj"][layer],
        "w_attn_proj": full["w_attn_proj"][layer],
        "w_mix_out": full["w_mix_out"][layer],
        "b_mix_out": rep["b_mix_out"][layer][None],
        "ln1_g": rep["ln1_g"][layer][None],
        "ln1_b": rep["ln1_b"][layer][None],
        "w_ffn_up": full["w_ffn_up"][layer],
        "w_ffn_up_gate": full["w_ffn_up"][layer][:, :d_ff],
        "w_ffn_up_lin": full["w_ffn_up"][layer][:, d_ff:],
        "ffn_conv_w": full["ffn_conv_w"][layer],
        "ffn_conv_b": rep["ffn_conv_b"][layer][None],
        "w_ffn_down": full["w_ffn_down"][layer],
        "ln2_g": rep["ln2_g"][layer][None],
        "ln2_b": rep["ln2_b"][layer][None],
    }


def _layer_fwd(x, mod, p, dims, tag):
    bsz, d, ch, heads, alpha = dims["bsz"], dims["d"], dims["ch"], dims["heads"], dims["alpha"]
    mods = [mod[:, k * d:(k + 1) * d][:, None, :] for k in range(6)]
    shift1, scale1, gate1, shift2, scale2, gate2 = mods
    u = _ln_mod_fwd(x, scale1, shift1, bsz, name=f"ln_mod1_{tag}")
    zm = _matmul(u, p["w_main"], "nn", BF16, bias=p["b_main"], name=f"in_main_{tag}")
    zf = _matmul(u, p["w_f"], "nn", F32, bias=p["b_f"], name=f"in_forget_{tag}")
    a3 = _conv_branch_fwd(zm, p["conv_a_w"], p["conv_a_b"], p["ln_conv_g"], p["ln_conv_b"], bsz, ch,
                          name=f"conv_branch_{tag}")
    ya = _matmul(a3, p["w_conv_proj"], "nn", F32, name=f"conv_proj_{tag}")
    cumt, cumb = _fgate_fwd(zf, bsz, heads, name=f"fgate_{tag}")
    qkvt = _to_features_major(zm, 2 * ch, heads * HEAD_DIM, 3, name=f"qkv_t_{tag}")
    ot, lse = _attn_fwd(zm, qkvt, cumt, cumb, bsz, heads, 2 * ch, name=f"attn_{tag}")
    yb = _matmul(ot, p["w_attn_proj"], "tn", F32, name=f"attn_proj_{tag}")
    m = _gate_merge_fwd(zm, ya, yb, dims["ga_off"], name=f"merge_{tag}")
    mix = _matmul(m, p["w_mix_out"], "nn", F32, bias=p["b_mix_out"], name=f"mix_out_{tag}")
    x1 = _ln_res_fwd(x, mix, gate1, p["ln1_g"], p["ln1_b"], alpha, bsz, name=f"ln_res1_{tag}")
    u2 = _ln_mod_fwd(x1, scale2, shift2, bsz, name=f"ln_mod2_{tag}")
    hp = _matmul(u2, p["w_ffn_up"], "nn", BF16, name=f"ffn_up_{tag}")
    f = _ffn_act_fwd(hp, p["ffn_conv_w"], p["ffn_conv_b"], bsz, dims["tcf"], name=f"ffn_act_{tag}")
    ffn = _matmul(f, p["w_ffn_down"], "nn", F32, name=f"ffn_down_{tag}")
    x2 = _ln_res_fwd(x1, ffn, gate2, p["ln2_g"], p["ln2_b"], alpha, bsz, name=f"ln_res2_{tag}")
    saved = dict(x=x, mods=mods, u=u, zm=zm, zf=zf, a3=a3, ya=ya, yb=yb, cumt=cumt, cumb=cumb,
                 qkvt=qkvt, ot=ot, lse=lse, m=m, mix=mix, x1=x1, u2=u2, hp=hp, f=f, ffn=ffn)
    return x2, saved


def _layer_bwd(dx2, p, sv, dims, tag):
    bsz, ch, heads, alpha = dims["bsz"], dims["ch"], dims["heads"], dims["alpha"]
    f_off, tcf = dims["f_off"], dims["tcf"]
    shift1, scale1, gate1, shift2, scale2, gate2 = sv["mods"]
    g = {}
    dr2, dffn, dgate2, g["ln2_g"], g["ln2_b"], _ = _ln_res_bwd(
        dx2, sv["x1"], sv["ffn"], gate2, p["ln2_g"], alpha, bsz, name=f"ln_res2_bwd_{tag}")
    df = _matmul(dffn, p["w_ffn_down"], "nt", F32, name=f"ffn_down_dx_{tag}")
    g["w_ffn_down"] = _matmul(sv["f"], dffn, "tn", F32, name=f"ffn_down_dw_{tag}")
    dhg, dhl, dwg, dwl, dbg, dbl = _ffn_act_bwd(sv["hp"], df, p["ffn_conv_w"], p["ffn_conv_b"], bsz, tcf,
                                                name=f"ffn_act_bwd_{tag}")
    g["ffn_conv_w"] = jnp.concatenate([dwg, dwl], axis=1)
    g["ffn_conv_b"] = jnp.concatenate([dbg, dbl], axis=1)[0]
    du2 = _matmul(dhg, p["w_ffn_up_gate"], "nt", F32, name=f"ffn_up_gate_dx_{tag}")
    du2 = _matmul(dhl, p["w_ffn_up_lin"], "nt", F32, add=du2, name=f"ffn_up_lin_dx_{tag}")
    g["w_ffn_up"] = jnp.concatenate([_matmul(sv["u2"], dhg, "tn", F32, name=f"ffn_up_gate_dw_{tag}"),
                                     _matmul(sv["u2"], dhl, "tn", F32, name=f"ffn_up_lin_dw_{tag}")], axis=1)
    dx1, dscale2, dshift2 = _ln_mod_bwd(du2, sv["x1"], scale2, dr2, alpha, bsz, name=f"ln_mod2_bwd_{tag}")
    dr1, dmix, dgate1, g["ln1_g"], g["ln1_b"], g["b_mix_out"] = _ln_res_bwd(
        dx1, sv["x"], sv["mix"], gate1, p["ln1_g"], alpha, bsz, name=f"ln_res1_bwd_{tag}")
    dm = _matmul(dmix, p["w_mix_out"], "nt", F32, name=f"mix_out_dx_{tag}")
    g["w_mix_out"] = _matmul(sv["m"], dmix, "tn", F32, name=f"mix_out_dw_{tag}")
    dya, dyb, dzga, dzgb = _gate_merge_bwd(sv["zm"], sv["ya"], sv["yb"], dm, dims["ga_off"], name=f"merge_bwd_{tag}")
    da3 = _matmul(dya, p["w_conv_proj"], "nt", F32, name=f"conv_proj_dx_{tag}")
    g["w_conv_proj"] = _matmul(sv["a3"], dya, "tn", F32, name=f"conv_proj_dw_{tag}")
    do = _matmul(dyb, p["w_attn_proj"], "nt", BF16, name=f"attn_proj_dx_{tag}")
    dot = _matmul(p["w_attn_proj"], dyb, "nt", BF16, name=f"attn_proj_dxt_{tag}")
    g["w_attn_proj"] = _matmul(sv["ot"], dyb, "nn", F32, name=f"attn_proj_dw_{tag}")
    dzglu, g["conv_a_w"], dcb, g["ln_conv_g"], g["ln_conv_b"] = _conv_branch_bwd(
        sv["zm"], da3, p["conv_a_w"], p["conv_a_b"], p["ln_conv_g"], p["ln_conv_b"], bsz, ch,
        name=f"conv_branch_bwd_{tag}")
    g["conv_a_b"] = dcb[0]
    dk, dv, dqt, dcum = _attn_bwd(sv["zm"], sv["qkvt"], sv["cumt"], sv["cumb"], sv["ot"], do, dot, sv["lse"], bsz,
                                  heads, 2 * ch, name=f"attn_bwd_{tag}")
    dq = _to_rows_major(dqt, name=f"dq_rows_{tag}")
    dzf = _fgate_bwd(dcum, sv["zf"], bsz, name=f"fgate_bwd_{tag}")
    dzm = jnp.concatenate([dzglu, dq, dk, dv, dzga, dzgb], axis=1)
    du = _matmul(dzm, p["w_main"], "nt", F32, name=f"in_main_dx_{tag}")
    du = _matmul(dzf, p["w_f"], "nt", F32, add=du, name=f"in_forget_dx_{tag}")
    dwm, dbm = _matmul(sv["u"], dzm, "tn", F32, colsum=True, name=f"in_main_dw_{tag}")
    dwf, dbf = _matmul(sv["u"], dzf, "tn", F32, colsum=True, name=f"in_forget_dw_{tag}")
    dbm, dbf = dbm[0], dbf[0]
    g["w_main"], g["w_f"] = dwm, dwf
    g["b_in"] = jnp.concatenate([dbm[:f_off], dbf[:heads], dbm[f_off:]])
    dx, dscale1, dshift1 = _ln_mod_bwd(du, sv["x"], scale1, dr1, alpha, bsz, name=f"ln_mod1_bwd_{tag}")
    dmod = jnp.concatenate([dshift1, dscale1, dgate1, dshift2, dscale2, dgate2], axis=2)[:, 0, :]
    return dx, g, dmod


def _local_step(x, mod, loss_target, full, rep, dims):
    bsz, seq, d = x.shape
    layers = mod.shape[0]
    params = [_layer_weights(full, rep, layer, dims) for layer in range(layers)]
    h = x.reshape(bsz * seq, d)
    saved = []
    for layer in range(layers):
        h, sv = _layer_fwd(h, mod[layer], params[layer], dims, f"l{layer}")
        saved.append(sv)
    dh, sq = _loss_head(h, loss_target.reshape(bsz * seq, d), name="loss_head")
    loss_local = 0.5 * jnp.sum(sq) / d
    grads, dmods = [None] * layers, [None] * layers
    for layer in reversed(range(layers)):
        dh, grads[layer], dmods[layer] = _layer_bwd(dh, params[layer], saved[layer], dims, f"l{layer}")
    per_layer = ("w_main", "w_f")
    stacked = {wname: [grads[layer][wname] for layer in range(layers)] if wname in per_layer
               else jnp.stack([grads[layer][wname] for layer in range(layers)]) for wname in grads[0]}
    return loss_local, dh.reshape(bsz, seq, d), stacked, jnp.stack(dmods)


def _pad_rows(a):
    extra = -a.shape[-2] % (2 * SUBLANES)
    if extra == 0:
        return a
    return jnp.pad(a, [(0, 0)] * (a.ndim - 2) + [(0, extra), (0, 0)])


def _w_in_pieces(n, f_off, heads):
    n_in = N_CHIPS * n
    segments = [(0, f_off, "main", 0), (f_off, f_off + heads, "f", 0), (f_off + heads, n_in, "main", f_off)]
    pieces = []
    for chip in range(N_CHIPS):
        lo, hi = chip * n, (chip + 1) * n
        for a, b, target, t0 in segments:
            s, e = max(lo, a), min(hi, b)
            if s < e:
                pieces.append((chip, s - lo, e - lo, target, t0 + s - a))
    return pieces


def _w_in_from_slabs(slabs, f_off, heads, *, name):
    layers, _, k, n = slabs.shape
    tr = _pick(k, (256, 128, 64, 32, 16))
    n_main = N_CHIPS * n - heads
    pieces = _w_in_pieces(n, f_off, heads)

    def body(s_ref, m_ref, f_ref):
        f_ref[...] = jnp.zeros_like(f_ref)
        for chip in range(N_CHIPS):
            slab = s_ref[0, chip].astype(F32)
            for pc, s0, s1, target, t0 in pieces:
                if pc == chip:
                    out = m_ref if target == "main" else f_ref
                    out[0, :, t0:t0 + s1 - s0] = slab[:, s0:s1].astype(out.dtype)

    return _call(body, name=name, grid=(layers, k // tr),
                 in_specs=[pl.BlockSpec((1, N_CHIPS, tr, n), lambda layer, i: (layer, 0, i, 0))],
                 out_specs=[pl.BlockSpec((1, tr, n_main), lambda layer, i: (layer, i, 0)),
                            pl.BlockSpec((1, tr, LANES), lambda layer, i: (layer, i, 0))],
                 out_shape=[_sds((layers, k, n_main), slabs.dtype), _sds((layers, k, LANES), slabs.dtype)],
                 sem=("parallel", "parallel"))(slabs)


def _w_in_to_slabs(d_main, d_f, n, f_off, heads, *, name):
    layers = len(d_main)
    k = d_main[0].shape[0]
    tr = _pick(k, (128, 64, 32, 16, 8))
    pieces = _w_in_pieces(n, f_off, heads)

    def body(*refs):
        m_refs, f_refs, o_ref = refs[:layers], refs[layers:2 * layers], refs[2 * layers]
        for layer in range(layers):
            for chip, s0, s1, target, t0 in pieces:
                src = m_refs[layer] if target == "main" else f_refs[layer]
                o_ref[layer, chip, :, s0:s1] = src[:, t0:t0 + s1 - s0]

    return _call(body, name=name, grid=(k // tr,),
                 in_specs=[pl.BlockSpec((tr, d_main[0].shape[1]), lambda i: (i, 0))] * layers
                 + [pl.BlockSpec((tr, LANES), lambda i: (i, 0))] * layers,
                 out_specs=pl.BlockSpec((layers, N_CHIPS, tr, n), lambda i: (0, 0, i, 0)),
                 out_shape=_sds((layers, N_CHIPS, k, n), F32), sem=("parallel",))(*d_main, *d_f)


def kernel(x, c, w_ada, b_ada, w_in, b_in, conv_a_w, conv_a_b, ln_conv_g, ln_conv_b, w_conv_proj, w_attn_proj, w_mix_out, b_mix_out, ln1_g, ln1_b, w_ffn_up, ffn_conv_w, ffn_conv_b, w_ffn_down, ln2_g, ln2_b, loss_target, m_w_ada, m_b_ada, m_w_in, m_b_in, m_conv_a_w, m_conv_a_b, m_ln_conv_g, m_ln_conv_b, m_w_conv_proj, m_w_attn_proj, m_w_mix_out, m_b_mix_out, m_ln1_g, m_ln1_b, m_w_ffn_up, m_ffn_conv_w, m_ffn_conv_b, m_w_ffn_down, m_ln2_g, m_ln2_b, v_w_ada, v_b_ada, v_w_in, v_b_in, v_conv_a_w, v_conv_a_b, v_ln_conv_g, v_ln_conv_b, v_w_conv_proj, v_w_attn_proj, v_w_mix_out, v_b_mix_out, v_ln1_g, v_ln1_b, v_w_ffn_up, v_ffn_conv_w, v_ffn_conv_b, v_w_ffn_down, v_ln2_g, v_ln2_b):
    weights = dict(zip(WEIGHTS, (w_ada, b_ada, w_in, b_in, conv_a_w, conv_a_b, ln_conv_g, ln_conv_b, w_conv_proj,
                                 w_attn_proj, w_mix_out, b_mix_out, ln1_g, ln1_b, w_ffn_up, ffn_conv_w, ffn_conv_b,
                                 w_ffn_down, ln2_g, ln2_b)))
    mom1 = dict(zip(WEIGHTS, (m_w_ada, m_b_ada, m_w_in, m_b_in, m_conv_a_w, m_conv_a_b, m_ln_conv_g, m_ln_conv_b,
                              m_w_conv_proj, m_w_attn_proj, m_w_mix_out, m_b_mix_out, m_ln1_g, m_ln1_b, m_w_ffn_up,
                              m_ffn_conv_w, m_ffn_conv_b, m_w_ffn_down, m_ln2_g, m_ln2_b)))
    mom2 = dict(zip(WEIGHTS, (v_w_ada, v_b_ada, v_w_in, v_b_in, v_conv_a_w, v_conv_a_b, v_ln_conv_g, v_ln_conv_b,
                              v_w_conv_proj, v_w_attn_proj, v_w_mix_out, v_b_mix_out, v_ln1_g, v_ln1_b, v_w_ffn_up,
                              v_ffn_conv_w, v_ffn_conv_b, v_w_ffn_down, v_ln2_g, v_ln2_b)))
    bsz, seq, d = x.shape
    layers = w_ada.shape[0]
    ch = conv_a_w.shape[2] * N_CHIPS
    width = w_attn_proj.shape[1]
    heads = width // HEAD_DIM
    d_ff = w_ffn_down.shape[1] * N_CHIPS
    dims = dict(bsz=bsz, d=d, ch=ch, heads=heads, alpha=(2.0 * layers) ** 0.25, f_off=2 * ch + 3 * width,
                ga_off=2 * ch + 3 * width, tcf=_pick(d_ff, (256, 128)))
    chip = 2 * lax.axis_index("x") + lax.axis_index("y")
    device = 2 * chip + lax.axis_index("c")
    ada_cols = w_ada.shape[2]

    c_act = _silu_rows(_all_gather8(c, name="gather_c").reshape(N_DEVICES * bsz, d), name="silu_c")
    b_ada_mine = lax.dynamic_slice_in_dim(b_ada, chip * ada_cols, ada_cols, axis=1)
    mod_cols = jnp.stack([_matmul(c_act, w_ada[layer], "nn", F32, bias=b_ada_mine[layer][None], name=f"ada_l{layer}")
                          for layer in range(layers)])
    mod_all = _all_gather8(mod_cols, name="gather_mod")
    mod_all = jnp.concatenate([mod_all[2 * k] for k in range(N_CHIPS)], axis=-1)
    mod = lax.dynamic_slice_in_dim(mod_all, device * bsz, bsz, axis=1)

    shards = [_pad_rows(weights[wname].astype(BF16) if as_bf16 else weights[wname]) for wname, _, as_bf16 in GATHERED]
    modes = [mode for _, mode, _ in GATHERED]
    whole = _gather_weights(shards, modes, name="gather_weights")
    full = {wname: w[:, :weights[wname].shape[1]] if mode == "cols" else w
            for (wname, mode, _), w in zip(GATHERED, whole)}
    full["w_main"], full["w_f"] = _w_in_from_slabs(full.pop("w_in"), dims["f_off"], heads, name="w_in_from_slabs")
    rep = {wname: weights[wname] for wname in REPLICATED}

    loss_local, grad_x, grads, dmod = _local_step(x, mod, loss_target, full, rep, dims)
    loss = lax.psum(loss_local, ("x", "y", "c"))

    grads["w_in"] = _w_in_to_slabs(grads.pop("w_main"), grads.pop("w_f"), w_in.shape[2], dims["f_off"], heads,
                                   name="w_in_to_slabs")
    shard_shapes = [s.shape for s in shards]
    reduced = _reduce_scatter([_pad_rows(grads[wname]) for wname, _, _ in GATHERED], modes, shard_shapes)
    grad = {wname: r[:, :weights[wname].shape[1]] for (wname, _, _), r in zip(GATHERED, reduced)}

    small = jnp.concatenate([dmod.reshape(-1)] + [grads[wname].reshape(-1) for wname in REPLICATED])
    n_small = small.shape[0]
    rows = -(-n_small // (SUBLANES * LANES)) * SUBLANES
    small = jnp.pad(small, (0, rows * LANES - n_small)).reshape(rows, LANES)
    gathered = _all_gather8(small, name="gather_small")
    n_dmod = dmod.size
    dmod_all = gathered.reshape(N_DEVICES, -1)[:, :n_dmod].reshape(N_DEVICES, layers, bsz, 6 * d)
    dmod_all = jnp.transpose(dmod_all, (1, 0, 2, 3)).reshape(layers, N_DEVICES * bsz, 6 * d)
    summed = _sum_slots(gathered, name="sum_small").reshape(-1)
    off = n_dmod
    for wname in REPLICATED:
        n = weights[wname].size
        grad[wname] = summed[off:off + n].reshape(weights[wname].shape)
        off += n
    dmod_mine = lax.dynamic_slice_in_dim(dmod_all, chip * ada_cols, ada_cols, axis=2)
    grad["w_ada"] = jnp.stack([_matmul(c_act, dmod_mine[layer], "tn", F32, name=f"ada_dw_l{layer}")
                               for layer in range(layers)])
    grad["b_ada"] = jnp.stack([_colsum(dmod_all[layer], name=f"ada_db_l{layer}")[0] for layer in range(layers)])

    delta, new_m, new_v = {}, {}, {}
    for wname in WEIGHTS:
        delta[wname], new_m[wname], new_v[wname] = _adamw(weights[wname], grad[wname], mom1[wname], mom2[wname],
                                                          name=f"adamw_{wname}")
    return (loss, grad_x, *[grad[wname] for wname in WEIGHTS], *[delta[wname] for wname in WEIGHTS],
            *[new_m[wname] for wname in WEIGHTS], *[new_v[wname] for wname in WEIGHTS])
```

```python
import math

import jax
import jax.numpy as jnp
from jax import lax
from jax.experimental import pallas as pl
from jax.experimental.pallas import tpu as pltpu

F32 = jnp.float32
BF16 = jnp.bfloat16
MESH = pl.DeviceIdType.MESH

LN_EPS = 1e-5
HEAD_DIM = 64
ATTN_SCALE = HEAD_DIM ** -0.5
NEG = -1e30
FFN_PAD = 8
LANES = 128
SUBLANES = 8
ROW_CHUNK = 256
ATTN_BLOCK = 256
ATTN_UNROLL = 3
N_CHIPS = 4
N_DEVICES = 8
VMEM_LIMIT = 56 * 1024 * 1024

ADAM_LR = 0.001
ADAM_B1 = 0.9
ADAM_B2 = 0.999
ADAM_EPS = 1e-08
ADAM_WD = 0.01
ADAM_STEP = 10

GATHERED = (("w_in", "slab", True), ("conv_a_w", "cols", False), ("w_conv_proj", "cols", True),
            ("w_attn_proj", "cols", True), ("w_mix_out", "rows", True), ("w_ffn_up", "cols", True),
            ("ffn_conv_w", "cols", False), ("w_ffn_down", "rows", True))
REPLICATED = ("b_in", "conv_a_b", "ln_conv_g", "ln_conv_b", "b_mix_out", "ln1_g", "ln1_b",
              "ffn_conv_b", "ln2_g", "ln2_b")
WEIGHTS = ("w_ada", "b_ada", "w_in", "b_in", "conv_a_w", "conv_a_b", "ln_conv_g", "ln_conv_b",
           "w_conv_proj", "w_attn_proj", "w_mix_out", "b_mix_out", "ln1_g", "ln1_b", "w_ffn_up",
           "ffn_conv_w", "ffn_conv_b", "w_ffn_down", "ln2_g", "ln2_b")


def _pick(n, cands):
    for cand in cands:
        if n % cand == 0:
            return cand
    return n


def _call(body, *, name, grid, in_specs, out_specs, out_shape, scratch=(), sem=None):
    return pl.pallas_call(
        body, name=name, grid=grid, in_specs=in_specs, out_specs=out_specs, out_shape=out_shape,
        scratch_shapes=list(scratch),
        compiler_params=pltpu.CompilerParams(dimension_semantics=sem, vmem_limit_bytes=VMEM_LIMIT))


def _sds(shape, dtype):
    return jax.ShapeDtypeStruct(tuple(shape), dtype)


def _chunked(rows, fn):
    chunk = min(ROW_CHUNK, rows)
    if rows == chunk:
        fn(pl.ds(0, rows))
        return

    def step(i, carry):
        fn(pl.ds(pl.multiple_of(i * chunk, chunk), chunk))
        return carry

    lax.fori_loop(0, rows // chunk, step, 0)


def _matmul(a, b, mode, out_dtype, *, bias=None, add=None, colsum=False, name):
    if mode == "nn":
        (m, k), (_, n) = a.shape, b.shape
    elif mode == "nt":
        (m, k), (n, _) = a.shape, b.shape
    else:
        (k, m), (_, n) = a.shape, b.shape
    tm = _pick(m, (1024, 1408, 512, 256, 128))
    tn = _pick(n, (1536, 1408, 1024, 512, 256, 128))
    tk = k if k <= 1536 else _pick(k, (1024, 1536, 1408, 512, 256, 128))
    nk = k // tk
    if mode == "nn":
        a_spec = pl.BlockSpec((tm, tk), lambda i, j, kk: (i, kk))
        b_spec = pl.BlockSpec((tk, tn), lambda i, j, kk: (kk, j))
        dims = (((1,), (0,)), ((), ()))
    elif mode == "nt":
        a_spec = pl.BlockSpec((tm, tk), lambda i, j, kk: (i, kk))
        b_spec = pl.BlockSpec((tn, tk), lambda i, j, kk: (j, kk))
        dims = (((1,), (1,)), ((), ()))
    else:
        a_spec = pl.BlockSpec((tk, tm), lambda i, j, kk: (kk, i))
        b_spec = pl.BlockSpec((tk, tn), lambda i, j, kk: (kk, j))
        dims = (((0,), (0,)), ((), ()))
    in_specs = [a_spec, b_spec]
    operands = [a, b]
    if bias is not None:
        in_specs.append(pl.BlockSpec((1, tn), lambda i, j, kk: (0, j)))
        operands.append(bias)
    if add is not None:
        in_specs.append(pl.BlockSpec((tm, tn), lambda i, j, kk: (i, j)))
        operands.append(add)

    def body(a_ref, b_ref, *rest):
        rest = list(rest)
        bias_ref = rest.pop(0) if bias is not None else None
        add_ref = rest.pop(0) if add is not None else None
        o_ref = rest.pop(0)
        prod = lax.dot_general(a_ref[...].astype(BF16), b_ref[...].astype(BF16), dims,
                               preferred_element_type=F32)
        if colsum:
            cs_ref = rest.pop(0)
            part = jnp.sum(b_ref[...].astype(F32), axis=0, keepdims=True)

            @pl.when(pl.program_id(2) == 0)
            def _():
                cs_ref[...] = part

            @pl.when(pl.program_id(2) > 0)
            def _():
                cs_ref[...] += part

        def finish(r):
            if bias_ref is not None:
                r = r + bias_ref[...]
            if add_ref is not None:
                r = r + add_ref[...]
            o_ref[...] = r.astype(o_ref.dtype)

        if nk == 1:
            finish(prod)
            return
        acc_ref = rest.pop(0)
        kk = pl.program_id(2)

        @pl.when(kk == 0)
        def _():
            acc_ref[...] = prod

        @pl.when(kk > 0)
        def _():
            acc_ref[...] += prod

        @pl.when(kk == nk - 1)
        def _():
            finish(acc_ref[...])

    out_specs = pl.BlockSpec((tm, tn), lambda i, j, kk: (i, j))
    out_shape = _sds((m, n), out_dtype)
    if colsum:
        assert mode == "tn" and m == tm
        out_specs = [out_specs, pl.BlockSpec((1, tn), lambda i, j, kk: (0, j))]
        out_shape = [out_shape, _sds((1, n), F32)]
    return _call(body, name=name, grid=(m // tm, n // tn, nk), in_specs=in_specs, out_specs=out_specs,
                 out_shape=out_shape, scratch=[pltpu.VMEM((tm, tn), F32)] if nk > 1 else [],
                 sem=("parallel", "parallel", "arbitrary"))(*operands)


def _colsum(x, *, name):
    rows, n = x.shape
    tr = _pick(rows, (1024, 512, 256, 128))
    tn = _pick(n, (512, 256, 128))

    def body(x_ref, o_ref):
        @pl.when(pl.program_id(1) == 0)
        def _():
            o_ref[...] = jnp.zeros_like(o_ref)

        o_ref[...] += jnp.sum(x_ref[...].astype(F32), axis=0, keepdims=True)

    return _call(body, name=name, grid=(n // tn, rows // tr),
                 in_specs=[pl.BlockSpec((tr, tn), lambda j, i: (i, j))],
                 out_specs=pl.BlockSpec((1, tn), lambda j, i: (0, j)),
                 out_shape=_sds((1, n), F32), sem=("parallel", "arbitrary"))(x)


def _ln_stats(x):
    mu = jnp.mean(x, axis=-1, keepdims=True)
    xc = x - mu
    var = jnp.mean(xc * xc, axis=-1, keepdims=True)
    rstd = lax.rsqrt(var + LN_EPS)
    return xc * rstd, rstd


def _ln_bwd(dn, n, rstd):
    return rstd * (dn - jnp.mean(dn, axis=-1, keepdims=True) - n * jnp.mean(dn * n, axis=-1, keepdims=True))


def _seq_tiles(t, bsz, cands=(1024, 512, 256, 128, 64, 32, 16, 8)):
    s = t // bsz
    ts = _pick(s, cands)
    return s, ts, s // ts


def _ln_mod_fwd(x, scale, shift, bsz, *, name):
    t, d = x.shape
    _, ts, ns = _seq_tiles(t, bsz)

    def body(x_ref, sc_ref, sh_ref, u_ref):
        one_scale = 1.0 + sc_ref[0]
        shift_v = sh_ref[0]

        def piece(rows):
            n, _ = _ln_stats(x_ref[rows, :])
            u_ref[rows, :] = (n * one_scale + shift_v).astype(u_ref.dtype)

        _chunked(ts, piece)

    row = pl.BlockSpec((ts, d), lambda b, i: (b * ns + i, 0))
    per = pl.BlockSpec((1, 1, d), lambda b, i: (b, 0, 0))
    return _call(body, name=name, grid=(bsz, ns), in_specs=[row, per, per], out_specs=row,
                 out_shape=_sds((t, d), BF16), sem=("parallel", "parallel"))(x, scale, shift)


def _ln_mod_bwd(du, x, scale, dr, alpha, bsz, *, name):
    t, d = x.shape
    _, ts, ns = _seq_tiles(t, bsz)

    def body(du_ref, x_ref, sc_ref, dr_ref, dx_ref, dsc_ref, dsh_ref):
        @pl.when(pl.program_id(1) == 0)
        def _():
            dsc_ref[...] = jnp.zeros_like(dsc_ref)
            dsh_ref[...] = jnp.zeros_like(dsh_ref)

        one_scale = 1.0 + sc_ref[0]

        def piece(rows):
            du_v = du_ref[rows, :]
            n, rstd = _ln_stats(x_ref[rows, :])
            dsc_ref[0] += jnp.sum(du_v * n, axis=0, keepdims=True)
            dsh_ref[0] += jnp.sum(du_v, axis=0, keepdims=True)
            dx_ref[rows, :] = alpha * dr_ref[rows, :] + _ln_bwd(du_v * one_scale, n, rstd)

        _chunked(ts, piece)

    row = pl.BlockSpec((ts, d), lambda b, i: (b * ns + i, 0))
    per = pl.BlockSpec((1, 1, d), lambda b, i: (b, 0, 0))
    return _call(body, name=name, grid=(bsz, ns), in_specs=[row, row, per, row],
                 out_specs=[row, per, per],
                 out_shape=[_sds((t, d), F32), _sds((bsz, 1, d), F32), _sds((bsz, 1, d), F32)],
                 sem=("parallel", "arbitrary"))(du, x, scale, dr)


def _ln_res_fwd(x, y, gate, g, b, alpha, bsz, *, name):
    t, d = x.shape
    _, ts, ns = _seq_tiles(t, bsz)

    def body(x_ref, y_ref, gt_ref, g_ref, b_ref, o_ref):
        one_gate = 1.0 + gt_ref[0]

        def piece(rows):
            n, _ = _ln_stats(alpha * x_ref[rows, :] + one_gate * y_ref[rows, :])
            o_ref[rows, :] = n * g_ref[...] + b_ref[...]

        _chunked(ts, piece)

    row = pl.BlockSpec((ts, d), lambda bb, i: (bb * ns + i, 0))
    per = pl.BlockSpec((1, 1, d), lambda bb, i: (bb, 0, 0))
    vec = pl.BlockSpec((1, d), lambda bb, i: (0, 0))
    return _call(body, name=name, grid=(bsz, ns), in_specs=[row, row, per, vec, vec], out_specs=row,
                 out_shape=_sds((t, d), F32), sem=("parallel", "parallel"))(x, y, gate, g, b)


def _ln_res_bwd(do, x, y, gate, g, alpha, bsz, *, name):
    t, d = x.shape
    _, ts, ns = _seq_tiles(t, bsz)

    def body(do_ref, x_ref, y_ref, gt_ref, g_ref, dr_ref, dy_ref, dgt_ref, dg_ref, db_ref, dys_ref):
        first_tile = pl.program_id(1) == 0

        @pl.when(first_tile)
        def _():
            dgt_ref[...] = jnp.zeros_like(dgt_ref)

        @pl.when(jnp.logical_and(first_tile, pl.program_id(0) == 0))
        def _():
            dg_ref[...] = jnp.zeros_like(dg_ref)
            db_ref[...] = jnp.zeros_like(db_ref)
            dys_ref[...] = jnp.zeros_like(dys_ref)

        one_gate = 1.0 + gt_ref[0]

        def piece(rows):
            do_v = do_ref[rows, :]
            y_v = y_ref[rows, :]
            n, rstd = _ln_stats(alpha * x_ref[rows, :] + one_gate * y_v)
            dg_ref[...] += jnp.sum(do_v * n, axis=0, keepdims=True)
            db_ref[...] += jnp.sum(do_v, axis=0, keepdims=True)
            dr = _ln_bwd(do_v * g_ref[...], n, rstd)
            dr_ref[rows, :] = dr
            dy = one_gate * dr
            dy_ref[rows, :] = dy.astype(dy_ref.dtype)
            dys_ref[...] += jnp.sum(dy, axis=0, keepdims=True)
            dgt_ref[0] += jnp.sum(dr * y_v, axis=0, keepdims=True)

        _chunked(ts, piece)

    row = pl.BlockSpec((ts, d), lambda bb, i: (bb * ns + i, 0))
    per = pl.BlockSpec((1, 1, d), lambda bb, i: (bb, 0, 0))
    vec = pl.BlockSpec((1, d), lambda bb, i: (0, 0))
    return _call(body, name=name, grid=(bsz, ns), in_specs=[row, row, row, per, vec],
                 out_specs=[row, row, per, vec, vec, vec],
                 out_shape=[_sds((t, d), F32), _sds((t, d), BF16), _sds((bsz, 1, d), F32),
                            _sds((1, d), F32), _sds((1, d), F32), _sds((1, d), F32)],
                 sem=("arbitrary", "arbitrary"))(do, x, y, gate, g)


def _loss_head(y, target, *, name):
    t, d = y.shape
    tr = _pick(t, (1024, 512, 256, 128, 64, 32, 16, 8))

    def body(y_ref, t_ref, dy_ref, s_ref):
        @pl.when(pl.program_id(0) == 0)
        def _():
            s_ref[...] = jnp.zeros_like(s_ref)

        def piece(rows):
            e = y_ref[rows, :] - t_ref[rows, :]
            dy_ref[rows, :] = e * (1.0 / d)
            s_ref[...] += jnp.sum(e * e, axis=0, keepdims=True)

        _chunked(tr, piece)

    row = pl.BlockSpec((tr, d), lambda i: (i, 0))
    return _call(body, name=name, grid=(t // tr,), in_specs=[row, row],
                 out_specs=[row, pl.BlockSpec((1, d), lambda i: (0, 0))],
                 out_shape=[_sds((t, d), F32), _sds((1, d), F32)], sem=("arbitrary",))(y, target)


def _sigmoid(v):
    return 1.0 / (1.0 + jnp.exp(-v))


def _silu_rows(c, *, name):
    rows, d = c.shape

    def body(c_ref, o_ref):
        v = c_ref[...]
        o_ref[...] = (v * _sigmoid(v)).astype(o_ref.dtype)

    full = pl.BlockSpec((rows, d), lambda i: (0, 0))
    return _call(body, name=name, grid=(1,), in_specs=[full], out_specs=full,
                 out_shape=_sds((rows, d), BF16), sem=("arbitrary",))(c)


def _gate_cols(d, ga_off):
    tc = _pick(math.gcd(d, ga_off), (512, 256, 128))
    return tc, ga_off // tc, (ga_off + d) // tc


def _gate_merge_fwd(z, ya, yb, ga_off, *, name):
    t, d = ya.shape
    tr = _pick(t, (1024, 512, 256, 128, 64, 32, 16, 8))
    tc, ga_blk, gb_blk = _gate_cols(d, ga_off)

    def body(ga_ref, gb_ref, ya_ref, yb_ref, o_ref):
        def piece(rows):
            o_ref[rows, :] = (_sigmoid(ga_ref[rows, :].astype(F32)) * ya_ref[rows, :]
                              + _sigmoid(gb_ref[rows, :].astype(F32)) * yb_ref[rows, :]).astype(o_ref.dtype)

        _chunked(tr, piece)

    blk = pl.BlockSpec((tr, tc), lambda i, j: (i, j))
    return _call(body, name=name, grid=(t // tr, d // tc),
                 in_specs=[pl.BlockSpec((tr, tc), lambda i, j: (i, ga_blk + j)),
                           pl.BlockSpec((tr, tc), lambda i, j: (i, gb_blk + j)), blk, blk],
                 out_specs=blk, out_shape=_sds((t, d), BF16), sem=("parallel", "parallel"))(z, z, ya, yb)


def _gate_merge_bwd(z, ya, yb, dm, ga_off, *, name):
    t, d = ya.shape
    tr = _pick(t, (1024, 512, 256, 128, 64, 32, 16, 8))
    tc, ga_blk, gb_blk = _gate_cols(d, ga_off)

    def body(ga_ref, gb_ref, ya_ref, yb_ref, dm_ref, dya_ref, dyb_ref, dga_ref, dgb_ref):
        def piece(rows):
            dm_v = dm_ref[rows, :]
            sa = _sigmoid(ga_ref[rows, :].astype(F32))
            sb = _sigmoid(gb_ref[rows, :].astype(F32))
            dya_ref[rows, :] = (dm_v * sa).astype(dya_ref.dtype)
            dyb_ref[rows, :] = (dm_v * sb).astype(dyb_ref.dtype)
            dga_ref[rows, :] = (dm_v * ya_ref[rows, :] * sa * (1.0 - sa)).astype(dga_ref.dtype)
            dgb_ref[rows, :] = (dm_v * yb_ref[rows, :] * sb * (1.0 - sb)).astype(dgb_ref.dtype)

        _chunked(tr, piece)

    blk = pl.BlockSpec((tr, tc), lambda i, j: (i, j))
    return _call(body, name=name, grid=(t // tr, d // tc),
                 in_specs=[pl.BlockSpec((tr, tc), lambda i, j: (i, ga_blk + j)),
                           pl.BlockSpec((tr, tc), lambda i, j: (i, gb_blk + j)), blk, blk, blk],
                 out_specs=[blk, blk, blk, blk], out_shape=[_sds((t, d), BF16)] * 4,
                 sem=("parallel", "parallel"))(z, z, ya, yb, dm)


CONV_ROWS = 64
CONV_PAD = 32


def _row_shifts(win):
    total = win.shape[0]
    return [win] + [pltpu.roll(win, total - b, axis=0) for b in range(1, SUBLANES)]


def _shifted_rows(copies, shift):
    start = SUBLANES * (shift // SUBLANES)
    return copies[shift % SUBLANES][start:start + CONV_ROWS]


def _fill_glu(z_ref, ext_ref, s, ch):
    ext_ref[pl.ds(0, CONV_PAD), :] = jnp.zeros((CONV_PAD, ch), F32)

    chunk = min(ROW_CHUNK, s)

    def piece(i, carry):
        start = pl.multiple_of(i * chunk, chunk)
        zz = z_ref[pl.ds(start, chunk), :].astype(F32)
        ext_ref[pl.ds(pl.multiple_of(CONV_PAD + start, CONV_PAD), chunk), :] = zz[:, :ch] * _sigmoid(zz[:, ch:])
        return carry

    lax.fori_loop(0, s // chunk, piece, 0)


def _conv_piece(ext_ref, w_ref, cb_ref, base, kw):
    copies = _row_shifts(ext_ref[pl.ds(base, CONV_ROWS + CONV_PAD), :])
    acc = cb_ref[...] + w_ref[pl.ds(0, 1), :] * _shifted_rows(copies, CONV_PAD - (kw - 1))
    for k in range(1, kw):
        acc = acc + w_ref[pl.ds(k, 1), :] * _shifted_rows(copies, CONV_PAD - (kw - 1) + k)
    return acc, copies


def _conv_branch_fwd(z, w, cb, lg, lb, bsz, ch, *, name):
    t = z.shape[0]
    s = t // bsz
    kw = w.shape[0]

    def body(z_ref, w_ref, cb_ref, lg_ref, lb_ref, o_ref, ext_ref):
        _fill_glu(z_ref, ext_ref, s, ch)

        def step(i, carry):
            base = pl.multiple_of(i * CONV_ROWS, CONV_ROWS)
            a1, _ = _conv_piece(ext_ref, w_ref, cb_ref, base, kw)
            n, _ = _ln_stats(a1)
            a2 = n * lg_ref[...] + lb_ref[...]
            o_ref[pl.ds(base, CONV_ROWS), :] = (a2 * _sigmoid(a2)).astype(o_ref.dtype)
            return carry

        lax.fori_loop(0, s // CONV_ROWS, step, 0)

    vec = pl.BlockSpec((1, ch), lambda b: (0, 0))
    return _call(body, name=name, grid=(bsz,),
                 in_specs=[pl.BlockSpec((s, 2 * ch), lambda b: (b, 0)), pl.BlockSpec((kw, ch), lambda b: (0, 0)),
                           vec, vec, vec],
                 out_specs=pl.BlockSpec((s, ch), lambda b: (b, 0)), out_shape=_sds((t, ch), BF16),
                 scratch=[pltpu.VMEM((CONV_PAD + s, ch), F32)], sem=("parallel",))(z, w, cb, lg, lb)


def _conv_branch_bwd(z, da3, w, cb, lg, lb, bsz, ch, *, name):
    t = z.shape[0]
    s = t // bsz
    kw = w.shape[0]
    n_rows = CONV_ROWS + CONV_PAD

    def body(z_ref, d_ref, w_ref, cb_ref, lg_ref, lb_ref, dz_ref, dw_ref, dcb_ref, dlg_ref, dlb_ref,
             ext_ref, da1_ref):
        @pl.when(pl.program_id(0) == 0)
        def _():
            for ref in (dw_ref, dcb_ref, dlg_ref, dlb_ref):
                ref[...] = jnp.zeros_like(ref)

        _fill_glu(z_ref, ext_ref, s, ch)
        da1_ref[pl.ds(s, CONV_PAD), :] = jnp.zeros((CONV_PAD, ch), F32)

        def grad_a1(i, carry):
            dlg, dlb = carry
            base = pl.multiple_of(i * CONV_ROWS, CONV_ROWS)
            a1, _ = _conv_piece(ext_ref, w_ref, cb_ref, base, kw)
            n, rstd = _ln_stats(a1)
            a2 = n * lg_ref[...] + lb_ref[...]
            sg = _sigmoid(a2)
            da2 = d_ref[pl.ds(base, CONV_ROWS), :] * (sg * (1.0 + a2 * (1.0 - sg)))
            da1_ref[pl.ds(base, CONV_ROWS), :] = _ln_bwd(da2 * lg_ref[...], n, rstd)
            return (dlg + jnp.sum(da2 * n, axis=0, keepdims=True), dlb + jnp.sum(da2, axis=0, keepdims=True))

        zero = jnp.zeros((1, ch), F32)
        dlg, dlb = lax.fori_loop(0, s // CONV_ROWS, grad_a1, (zero, zero))
        dlg_ref[...] += dlg
        dlb_ref[...] += dlb

        def grad_z(i, dcb):
            base = pl.multiple_of(i * CONV_ROWS, CONV_ROWS)
            ahead = _row_shifts(da1_ref[pl.ds(base, n_rows), :])
            dyc = ahead[0][:CONV_ROWS]
            da0 = w_ref[pl.ds(kw - 1, 1), :] * dyc
            for k in range(kw - 1):
                da0 = da0 + w_ref[pl.ds(k, 1), :] * _shifted_rows(ahead, kw - 1 - k)
            behind = _row_shifts(ext_ref[pl.ds(base, n_rows), :])
            for k in range(kw):
                dw_ref[pl.ds(k, 1), :] += jnp.sum(dyc * _shifted_rows(behind, CONV_PAD - (kw - 1) + k),
                                                  axis=0, keepdims=True)
            zz = z_ref[pl.ds(base, CONV_ROWS), :].astype(F32)
            sg = _sigmoid(zz[:, ch:])
            dz_ref[pl.ds(base, CONV_ROWS), :ch] = (da0 * sg).astype(dz_ref.dtype)
            dz_ref[pl.ds(base, CONV_ROWS), ch:] = (da0 * zz[:, :ch] * sg * (1.0 - sg)).astype(dz_ref.dtype)
            return dcb + jnp.sum(dyc, axis=0, keepdims=True)

        dcb_ref[...] += lax.fori_loop(0, s // CONV_ROWS, grad_z, zero)

    vec = pl.BlockSpec((1, ch), lambda b: (0, 0))
    taps = pl.BlockSpec((kw, ch), lambda b: (0, 0))
    return _call(body, name=name, grid=(bsz,),
                 in_specs=[pl.BlockSpec((s, 2 * ch), lambda b: (b, 0)), pl.BlockSpec((s, ch), lambda b: (b, 0)),
                           taps, vec, vec, vec],
                 out_specs=[pl.BlockSpec((s, 2 * ch), lambda b: (b, 0)), taps, vec, vec, vec],
                 out_shape=[_sds((t, 2 * ch), BF16), _sds((kw, ch), F32)] + [_sds((1, ch), F32)] * 3,
                 scratch=[pltpu.VMEM((CONV_PAD + s, ch), F32), pltpu.VMEM((s + CONV_PAD, ch), F32)],
                 sem=("arbitrary",))(z, da3, w, cb, lg, lb)


FFN_ROWS = 64


def _gelu_parts(v):
    cdf = 0.5 * (1.0 + lax.erf(v * (2.0 ** -0.5)))
    return cdf, v * cdf


def _ffn_conv_piece(ext_ref, wb_ref, base):
    win = ext_ref[pl.ds(base, FFN_ROWS + FFN_PAD), :]
    acc = wb_ref[pl.ds(3, 1), :] + wb_ref[pl.ds(2, 1), :] * win[FFN_PAD:]
    acc = acc + wb_ref[pl.ds(1, 1), :] * pltpu.roll(win, 1, axis=0)[FFN_PAD:]
    acc = acc + wb_ref[pl.ds(0, 1), :] * pltpu.roll(win, 2, axis=0)[FFN_PAD:]
    return acc


def _ffn_stage(hg_ref, hl_ref, wg_ref, wl_ref, bg_ref, bl_ref, ext_ref, wb_ref, s, tcf):
    ext_ref[pl.ds(0, FFN_PAD), :] = jnp.zeros((FFN_PAD, 2 * tcf), F32)
    ext_ref[pl.ds(FFN_PAD, s), :tcf] = hg_ref[...].astype(F32)
    ext_ref[pl.ds(FFN_PAD, s), tcf:] = hl_ref[...].astype(F32)
    wb_ref[pl.ds(0, 3), :tcf] = wg_ref[...]
    wb_ref[pl.ds(0, 3), tcf:] = wl_ref[...]
    wb_ref[pl.ds(3, 1), :tcf] = bg_ref[...]
    wb_ref[pl.ds(3, 1), tcf:] = bl_ref[...]


def _ffn_specs(s, tcf, n_f, batch_first):
    def spec(rows, shift):
        if batch_first:
            return pl.BlockSpec((rows, tcf), lambda bb, j: (bb if rows == s else 0, shift + j))
        return pl.BlockSpec((rows, tcf), lambda j, bb: (bb if rows == s else 0, shift + j))

    return [spec(s, 0), spec(s, n_f), spec(3, 0), spec(3, n_f), spec(1, 0), spec(1, n_f)]


def _ffn_act_fwd(hp, w, b, bsz, tcf, *, name):
    t, two_f = hp.shape
    s = t // bsz
    n_f = two_f // (2 * tcf)

    def body(hg_ref, hl_ref, wg_ref, wl_ref, bg_ref, bl_ref, f_ref, ext_ref, wb_ref):
        _ffn_stage(hg_ref, hl_ref, wg_ref, wl_ref, bg_ref, bl_ref, ext_ref, wb_ref, s, tcf)

        def step(i, carry):
            base = pl.multiple_of(i * FFN_ROWS, FFN_ROWS)
            hh = _ffn_conv_piece(ext_ref, wb_ref, base)
            _, gelu = _gelu_parts(hh[:, :tcf])
            f_ref[pl.ds(base, FFN_ROWS), :] = (gelu * hh[:, tcf:]).astype(f_ref.dtype)
            return carry

        lax.fori_loop(0, s // FFN_ROWS, step, 0)

    return _call(body, name=name, grid=(bsz, n_f), in_specs=_ffn_specs(s, tcf, n_f, True),
                 out_specs=pl.BlockSpec((s, tcf), lambda bb, j: (bb, j)),
                 out_shape=_sds((t, two_f // 2), BF16),
                 scratch=[pltpu.VMEM((FFN_PAD + s, 2 * tcf), F32), pltpu.VMEM((SUBLANES, 2 * tcf), F32)],
                 sem=("parallel", "parallel"))(hp, hp, w, w, b, b)


def _ffn_act_bwd(hp, df, w, b, bsz, tcf, *, name):
    t, two_f = hp.shape
    s = t // bsz
    f_dim = two_f // 2
    n_f = f_dim // tcf
    gw = 2 * tcf
    n_rows = FFN_ROWS + FFN_PAD

    def body(hg_ref, hl_ref, wg_ref, wl_ref, bg_ref, bl_ref, df_ref,
             dhg_ref, dhl_ref, dwg_ref, dwl_ref, dbg_ref, dbl_ref, ext_ref, wb_ref, dh_ref):
        @pl.when(pl.program_id(1) == 0)
        def _():
            for ref in (dwg_ref, dwl_ref, dbg_ref, dbl_ref):
                ref[...] = jnp.zeros_like(ref)

        _ffn_stage(hg_ref, hl_ref, wg_ref, wl_ref, bg_ref, bl_ref, ext_ref, wb_ref, s, tcf)
        dh_ref[pl.ds(s, FFN_PAD), :] = jnp.zeros((FFN_PAD, gw), F32)

        def grad_h(i, carry):
            base = pl.multiple_of(i * FFN_ROWS, FFN_ROWS)
            hh = _ffn_conv_piece(ext_ref, wb_ref, base)
            hg = hh[:, :tcf]
            d = df_ref[pl.ds(base, FFN_ROWS), :]
            cdf, gelu = _gelu_parts(hg)
            pdf = jnp.exp(-0.5 * hg * hg) * (1.0 / math.sqrt(2.0 * math.pi))
            dh_ref[pl.ds(base, FFN_ROWS), :tcf] = d * hh[:, tcf:] * (cdf + hg * pdf)
            dh_ref[pl.ds(base, FFN_ROWS), tcf:] = d * gelu
            return carry

        lax.fori_loop(0, s // FFN_ROWS, grad_h, 0)

        def grad_x(i, carry):
            dw0, dw1, dw2, dbs = carry
            base = pl.multiple_of(i * FFN_ROWS, FFN_ROWS)
            nxt = dh_ref[pl.ds(base, n_rows), :]
            dyc = nxt[:FFN_ROWS]
            dx = wb_ref[pl.ds(2, 1), :] * dyc
            dx = dx + wb_ref[pl.ds(1, 1), :] * pltpu.roll(nxt, n_rows - 1, axis=0)[:FFN_ROWS]
            dx = dx + wb_ref[pl.ds(0, 1), :] * pltpu.roll(nxt, n_rows - 2, axis=0)[:FFN_ROWS]
            dhg_ref[pl.ds(base, FFN_ROWS), :] = dx[:, :tcf].astype(dhg_ref.dtype)
            dhl_ref[pl.ds(base, FFN_ROWS), :] = dx[:, tcf:].astype(dhl_ref.dtype)
            win = ext_ref[pl.ds(base, n_rows), :]
            dw2 = dw2 + jnp.sum(dyc * win[FFN_PAD:], axis=0, keepdims=True)
            dw1 = dw1 + jnp.sum(dyc * pltpu.roll(win, 1, axis=0)[FFN_PAD:], axis=0, keepdims=True)
            dw0 = dw0 + jnp.sum(dyc * pltpu.roll(win, 2, axis=0)[FFN_PAD:], axis=0, keepdims=True)
            return dw0, dw1, dw2, dbs + jnp.sum(dyc, axis=0, keepdims=True)

        zero = jnp.zeros((1, gw), F32)
        sums = lax.fori_loop(0, s // FFN_ROWS, grad_x, (zero, zero, zero, zero))
        for k in range(3):
            dwg_ref[pl.ds(k, 1), :] += sums[k][:, :tcf]
            dwl_ref[pl.ds(k, 1), :] += sums[k][:, tcf:]
        dbg_ref[...] += sums[3][:, :tcf]
        dbl_ref[...] += sums[3][:, tcf:]

    half = pl.BlockSpec((s, tcf), lambda j, bb: (bb, j))
    taps = pl.BlockSpec((3, tcf), lambda j, bb: (0, j))
    bias = pl.BlockSpec((1, tcf), lambda j, bb: (0, j))
    return _call(body, name=name, grid=(n_f, bsz), in_specs=_ffn_specs(s, tcf, n_f, False) + [half],
                 out_specs=[half, half, taps, taps, bias, bias],
                 out_shape=[_sds((t, f_dim), BF16)] * 2 + [_sds((3, f_dim), F32)] * 2 + [_sds((1, f_dim), F32)] * 2,
                 scratch=[pltpu.VMEM((FFN_PAD + s, gw), F32), pltpu.VMEM((SUBLANES, gw), F32),
                          pltpu.VMEM((s + FFN_PAD, gw), F32)],
                 sem=("parallel", "arbitrary"))(hp, hp, w, w, b, b, df)


def _split3(v):
    hi = v.astype(BF16)
    r = v - hi.astype(F32)
    mid = r.astype(BF16)
    lo = (r - mid.astype(F32)).astype(BF16)
    return hi, mid, lo


def _tri_dot(tri, v):
    out = None
    for part in _split3(v):
        term = jnp.dot(tri, part, preferred_element_type=F32)
        out = term if out is None else out + term
    return out


def _fgate_fwd(zf, bsz, heads, *, name):
    t, lanes = zf.shape
    s, blk, nb = _seq_tiles(t, bsz, (ATTN_BLOCK, 128))

    def body(z_ref, cumt_ref, cumb_ref, carry_ref):
        @pl.when(pl.program_id(1) == 0)
        def _():
            carry_ref[...] = jnp.zeros_like(carry_ref)

        z = z_ref[...]
        lf = jnp.minimum(z, 0.0) - jnp.log1p(jnp.exp(-jnp.abs(z)))
        r = lax.broadcasted_iota(jnp.int32, (blk, blk), 0)
        c = lax.broadcasted_iota(jnp.int32, (blk, blk), 1)
        tri = (r >= c).astype(BF16)
        cum = _tri_dot(tri, lf) + carry_ref[...]
        carry_ref[...] = cum[blk - 1:blk, :]
        cumt_ref[0] = jnp.transpose(cum)[:heads, :]
        for h in range(heads):
            cumb_ref[0, h] = jnp.broadcast_to(cum[:, h:h + 1], (blk, lanes))

    return _call(body, name=name, grid=(bsz, nb),
                 in_specs=[pl.BlockSpec((blk, lanes), lambda b, i: (b * nb + i, 0))],
                 out_specs=[pl.BlockSpec((1, heads, blk), lambda b, i: (b, 0, i)),
                            pl.BlockSpec((1, heads, blk, lanes), lambda b, i: (b, 0, i, 0))],
                 out_shape=[_sds((bsz, heads, s), F32), _sds((bsz, heads, s, lanes), F32)],
                 scratch=[pltpu.VMEM((1, lanes), F32)], sem=("parallel", "arbitrary"))(zf)


def _fgate_bwd(dcum, zf, bsz, *, name):
    t, lanes = zf.shape
    pairs = dcum.shape[1]
    s, blk, nb = _seq_tiles(t, bsz, (ATTN_BLOCK, 128))

    def body(d_ref, z_ref, o_ref, carry_ref):
        @pl.when(pl.program_id(1) == 0)
        def _():
            carry_ref[...] = jnp.zeros_like(carry_ref)

        dcol = d_ref[0, 0]
        for p in range(1, pairs):
            dcol = dcol + d_ref[0, p]
        r = lax.broadcasted_iota(jnp.int32, (blk, blk), 0)
        c = lax.broadcasted_iota(jnp.int32, (blk, blk), 1)
        tri = (c >= r).astype(BF16)
        suf = _tri_dot(tri, dcol) + carry_ref[...]
        carry_ref[...] = suf[0:1, :]
        o_ref[...] = suf * _sigmoid(-z_ref[...])

    return _call(body, name=name, grid=(bsz, nb),
                 in_specs=[pl.BlockSpec((1, pairs, blk, lanes), lambda b, i: (b, 0, nb - 1 - i, 0)),
                           pl.BlockSpec((blk, lanes), lambda b, i: (b * nb + nb - 1 - i, 0))],
                 out_specs=pl.BlockSpec((blk, lanes), lambda b, i: (b * nb + nb - 1 - i, 0)),
                 out_shape=_sds((t, lanes), F32), scratch=[pltpu.VMEM((1, lanes), F32)],
                 sem=("parallel", "arbitrary"))(dcum, zf)


def _to_features_major(z, col_off, width, n, *, name):
    t = z.shape[0]
    tr = _pick(t, (512, 256, 128))
    first = col_off // width

    def body(*refs):
        o_ref = refs[n]
        for g in range(n):
            o_ref[pl.ds(g * width, width), :] = jnp.transpose(refs[g][...].astype(F32)).astype(o_ref.dtype)

    return _call(body, name=name, grid=(t // tr,),
                 in_specs=[pl.BlockSpec((tr, width), lambda i, g=g: (i, first + g)) for g in range(n)],
                 out_specs=pl.BlockSpec((n * width, tr), lambda i: (0, i)),
                 out_shape=_sds((n * width, t), BF16), sem=("parallel",))(*([z] * n))


def _to_rows_major(xt, *, name):
    w, t = xt.shape
    tr = _pick(t, (512, 256, 128))

    def body(x_ref, o_ref):
        o_ref[...] = jnp.transpose(x_ref[...]).astype(o_ref.dtype)

    return _call(body, name=name, grid=(t // tr,),
                 in_specs=[pl.BlockSpec((w, tr), lambda i: (0, i))],
                 out_specs=pl.BlockSpec((tr, w), lambda i: (i, 0)),
                 out_shape=_sds((t, w), BF16), sem=("parallel",))(xt)


def _loop_by_twos(lo, hi, body, carry):
    count = hi - lo

    def group(n, first, cr):
        for u in range(n):
            cr = body(first + u, cr)
        return cr

    trips = count // ATTN_UNROLL
    carry = lax.fori_loop(0, trips, lambda t, cr: group(ATTN_UNROLL, lo + ATTN_UNROLL * t, cr), carry)
    rest = count - ATTN_UNROLL * trips
    first = lo + ATTN_UNROLL * trips
    for n in range(ATTN_UNROLL - 1, 0, -1):
        carry = lax.cond(rest == n, lambda cr, n=n: group(n, first, cr), lambda cr: cr, carry)
    return carry


def _head_masks(shape, axis):
    feat = lax.broadcasted_iota(jnp.int32, shape, axis)
    return feat < HEAD_DIM, feat >= HEAD_DIM


def _attn_fwd(z, qkvt, cumt, cumb, bsz, heads, q_off, *, name):
    t = z.shape[0]
    width = heads * HEAD_DIM
    pairs = heads // 2
    s = t // bsz
    blk = ATTN_BLOCK
    nq = s // blk
    k_col = (q_off + width) // LANES
    v_row = 2 * width // LANES
    reps = blk // LANES

    def body(k_ref, qt_ref, vt_ref, cqt_ref, ckb_ref, ot_ref, lse_ref):
        p_id = pl.program_id(1)
        i = pl.program_id(2)
        qt = qt_ref[...]
        masks = _head_masks((LANES, blk), 0)
        qtm = [jnp.where(mk, qt, jnp.zeros_like(qt)) for mk in masks]
        cq = [cqt_ref[0, pl.ds(2 * p_id + hh, 1), :] for hh in range(2)]
        kidx = lax.broadcasted_iota(jnp.int32, (blk, blk), 0)
        qidx = lax.broadcasted_iota(jnp.int32, (blk, blk), 1)

        def block(j, carry, masked):
            off = pl.multiple_of(j * blk, blk)
            kp = k_ref[pl.ds(off, blk), :].astype(BF16)
            vtp = vt_ref[:, pl.ds(off, blk)]
            out = []
            for hh in range(2):
                m, l, acc = carry[hh]
                sc = jnp.dot(kp, qtm[hh], preferred_element_type=F32) * ATTN_SCALE
                ck = ckb_ref[0, hh, pl.ds(off, blk), :]
                sc = (sc + cq[hh]) - jnp.concatenate([ck] * reps, axis=1)
                if masked:
                    sc = jnp.where(qidx >= kidx, sc, NEG)
                m_new = jnp.maximum(m, jnp.max(sc, axis=0, keepdims=True))
                pr = jnp.exp(sc - m_new)
                a = jnp.exp(m - m_new)
                l = a * l + jnp.sum(pr, axis=0, keepdims=True)
                p_hi = pr.astype(BF16)
                p_lo = (pr - p_hi.astype(F32)).astype(BF16)
                pv = (jnp.dot(vtp, p_hi, preferred_element_type=F32)
                      + jnp.dot(vtp, p_lo, preferred_element_type=F32))
                acc = a * acc + pv[hh * HEAD_DIM:(hh + 1) * HEAD_DIM]
                out.append((m_new, l, acc))
            return tuple(out)

        init = tuple((jnp.full((1, blk), NEG, F32), jnp.zeros((1, blk), F32), jnp.zeros((HEAD_DIM, blk), F32))
                     for _ in range(2))
        carry = _loop_by_twos(0, i, lambda j, cr: block(j, cr, False), init)
        carry = block(i, carry, True)
        lse_ref[...] = jnp.zeros_like(lse_ref)
        for hh in range(2):
            m, l, acc = carry[hh]
            ot_ref[pl.ds(hh * HEAD_DIM, HEAD_DIM), :] = acc / l
            lse_ref[0, 0, pl.ds(hh, 1), :] = m + jnp.log(l)

    return _call(body, name=name, grid=(bsz, pairs, nq),
                 in_specs=[pl.BlockSpec((s, LANES), lambda b, p, i: (b, k_col + p)),
                           pl.BlockSpec((LANES, blk), lambda b, p, i: (p, b * nq + i)),
                           pl.BlockSpec((LANES, s), lambda b, p, i: (v_row + p, b)),
                           pl.BlockSpec((1, heads, blk), lambda b, p, i: (b, 0, i)),
                           pl.BlockSpec((1, 2, s, LANES), lambda b, p, i: (b, p, 0, 0))],
                 out_specs=[pl.BlockSpec((LANES, blk), lambda b, p, i: (p, b * nq + i)),
                            pl.BlockSpec((1, 1, SUBLANES, blk), lambda b, p, i: (b, p, 0, i))],
                 out_shape=[_sds((width, t), F32), _sds((bsz, pairs, SUBLANES, s), F32)],
                 sem=("parallel", "parallel", "parallel"))(z, qkvt, qkvt, cumt, cumb)


def _attn_bwd(z, qkvt, cumt, cumb, ot, do, dot, lse, bsz, heads, q_off, *, name):
    t = z.shape[0]
    width = heads * HEAD_DIM
    pairs = heads // 2
    s = t // bsz
    blk = ATTN_BLOCK
    nkv = s // blk
    q_col = q_off // LANES
    k_col = (q_off + width) // LANES
    v_col = (q_off + 2 * width) // LANES
    k_row = width // LANES
    reps = blk // LANES

    def body(k_ref, v_ref, kt_ref, q_ref, qt_ref, do_ref, dot_ref, ot_ref, lse_ref, ckb_ref, cqt_ref,
             dk_ref, dv_ref, dqt_ref, dcum_ref, dqt_acc, ds_acc):
        p_id = pl.program_id(1)
        j = pl.program_id(2)

        @pl.when(j == 0)
        def _():
            dqt_acc[...] = jnp.zeros_like(dqt_acc)

        kp = k_ref[...].astype(BF16)
        vp = v_ref[...].astype(BF16)
        kt = kt_ref[...]
        feat_masks = _head_masks((LANES, blk), 0)
        lane_masks = _head_masks((blk, LANES), 1)
        ktm = [jnp.where(mk, kt, jnp.zeros_like(kt)) for mk in feat_masks]
        ck = [jnp.concatenate([ckb_ref[0, hh]] * reps, axis=1) for hh in range(2)]
        kidx = lax.broadcasted_iota(jnp.int32, (blk, blk), 0)
        qidx = lax.broadcasted_iota(jnp.int32, (blk, blk), 1)
        ds_acc[...] = jnp.zeros_like(ds_acc)

        def block(i, carry, masked):
            dk, dv = carry
            off = pl.multiple_of(i * blk, blk)
            qt = qt_ref[:, pl.ds(off, blk)]
            dt = dot_ref[:, pl.ds(off, blk)]
            o_t = ot_ref[:, pl.ds(off, blk)]
            q_rows = q_ref[pl.ds(off, blk), :].astype(BF16)
            do_rows = do_ref[pl.ds(off, blk), :]
            for hh in range(2):
                qtm = jnp.where(feat_masks[hh], qt, jnp.zeros_like(qt))
                dtm = jnp.where(feat_masks[hh], dt, jnp.zeros_like(dt))
                sc = jnp.dot(kp, qtm, preferred_element_type=F32) * ATTN_SCALE
                sc = (sc + cqt_ref[0, pl.ds(2 * p_id + hh, 1), pl.ds(off, blk)]) - ck[hh]
                pr = jnp.exp(sc - lse_ref[0, 0, pl.ds(hh, 1), pl.ds(off, blk)])
                if masked:
                    pr = jnp.where(qidx >= kidx, pr, 0.0)
                dp = jnp.dot(vp, dtm, preferred_element_type=F32)
                delta = jnp.sum(dtm.astype(F32) * o_t, axis=0, keepdims=True)
                ds = pr * (dp - delta)
                ds_acc[hh] += ds
                dsb = ds.astype(BF16)
                qm = jnp.where(lane_masks[hh], q_rows, jnp.zeros_like(q_rows))
                dom = jnp.where(lane_masks[hh], do_rows, jnp.zeros_like(do_rows))
                dv = dv + jnp.dot(pr.astype(BF16), dom, preferred_element_type=F32)
                dk = dk + jnp.dot(dsb, qm, preferred_element_type=F32) * ATTN_SCALE
                dqt_acc[:, pl.ds(off, blk)] += jnp.dot(ktm[hh], dsb, preferred_element_type=F32) * ATTN_SCALE
            return dk, dv

        zero = jnp.zeros((blk, LANES), F32)
        carry = block(j, (zero, zero), True)
        dk, dv = _loop_by_twos(j + 1, nkv, lambda i, cr: block(i, cr, False), carry)
        dk_ref[...] = dk.astype(dk_ref.dtype)
        dv_ref[...] = dv.astype(dv_ref.dtype)
        lane = lax.broadcasted_iota(jnp.int32, (blk, LANES), 1)
        dcum = jnp.zeros((blk, LANES), F32)
        for hh in range(2):
            col = jnp.sum(ds_acc[hh], axis=1, keepdims=True)
            dcum = jnp.where(lane == 2 * p_id + hh, -col, dcum)
        dcum_ref[0, 0] = dcum

        @pl.when(j == nkv - 1)
        def _():
            dqt_ref[...] = dqt_acc[...]

    key_rows = lambda col: pl.BlockSpec((blk, LANES), lambda b, p, j: (b * nkv + j, col + p))
    seq_t = lambda row: pl.BlockSpec((LANES, s), lambda b, p, j: (row + p, b))
    return _call(body, name=name, grid=(bsz, pairs, nkv),
                 in_specs=[key_rows(k_col), key_rows(v_col),
                           pl.BlockSpec((LANES, blk), lambda b, p, j: (k_row + p, b * nkv + j)),
                           pl.BlockSpec((s, LANES), lambda b, p, j: (b, q_col + p)), seq_t(0),
                           pl.BlockSpec((s, LANES), lambda b, p, j: (b, p)), seq_t(0), seq_t(0),
                           pl.BlockSpec((1, 1, SUBLANES, s), lambda b, p, j: (b, p, 0, 0)),
                           pl.BlockSpec((1, 2, blk, LANES), lambda b, p, j: (b, p, j, 0)),
                           pl.BlockSpec((1, heads, s), lambda b, p, j: (b, 0, 0))],
                 out_specs=[key_rows(0), key_rows(0), seq_t(0),
                            pl.BlockSpec((1, 1, blk, LANES), lambda b, p, j: (b, p, j, 0))],
                 out_shape=[_sds((t, width), BF16), _sds((t, width), BF16), _sds((width, t), F32),
                            _sds((bsz, pairs, s, LANES), F32)],
                 scratch=[pltpu.VMEM((LANES, s), F32), pltpu.VMEM((2, blk, blk), F32)],
                 sem=("parallel", "parallel", "arbitrary"))(z, z, qkvt, z, qkvt, do, dot, ot, lse, cumb, cumt)


def _adamw(w, g, m, v, *, name):
    bc1 = 1.0 - ADAM_B1 ** ADAM_STEP
    bc2 = 1.0 - ADAM_B2 ** ADAM_STEP

    def body(w_ref, g_ref, m_ref, v_ref, d_ref, nm_ref, nv_ref):
        g_v = g_ref[...]
        nm = ADAM_B1 * m_ref[...] + (1.0 - ADAM_B1) * g_v
        nv = ADAM_B2 * v_ref[...] + (1.0 - ADAM_B2) * (g_v * g_v)
        nm_ref[...] = nm
        nv_ref[...] = nv
        d_ref[...] = -ADAM_LR * ((nm / bc1) / (jnp.sqrt(nv / bc2) + ADAM_EPS) + ADAM_WD * w_ref[...])

    if w.ndim == 2:
        grid = (1,)
        blk = pl.BlockSpec(w.shape, lambda i: (0, 0))
    else:
        layers, rows, cols = w.shape
        tr = rows if rows <= 256 else _pick(rows, (256, 128, 64, 32, 16, 8))
        grid = (layers, rows // tr)
        blk = pl.BlockSpec((1, tr, cols), lambda layer, i: (layer, i, 0))
    return tuple(_call(body, name=name, grid=grid, in_specs=[blk] * 4, out_specs=[blk] * 3,
                       out_shape=[_sds(w.shape, F32)] * 3, sem=("parallel",) * len(grid))(w, g, m, v))


_ANY = pl.BlockSpec(memory_space=pl.ANY)


def _comm_call(body, *, name, n_in, out_shape, n_sems):
    scratch = [pltpu.SemaphoreType.DMA((n_sems,)), pltpu.SemaphoreType.DMA((n_sems,)),
               pltpu.SemaphoreType.DMA((len(out_shape),))]
    return pl.pallas_call(body, name=name, in_specs=[_ANY] * n_in, out_specs=[_ANY] * len(out_shape),
                          out_shape=out_shape, scratch_shapes=scratch)


def _place():
    x, y, c = lax.axis_index("x"), lax.axis_index("y"), lax.axis_index("c")
    return x, y, c, [(1 - x, y), (x, 1 - y), (1 - x, 1 - y)]


def _remote(src, dst, send_sems, recv_sems, sem, to):
    return pltpu.make_async_remote_copy(src_ref=src, dst_ref=dst, send_sem=send_sems.at[sem],
                                        recv_sem=recv_sems.at[sem], device_id=to, device_id_type=MESH)


def _all_gather8(v, *, name):
    def body(v_ref, out_ref, send_sems, recv_sems, local_sems):
        x, y, c, _ = _place()
        me = 4 * x + 2 * y + c
        mine = pltpu.make_async_copy(v_ref, out_ref.at[me], local_sems.at[0])
        mine.start()
        peers = []
        for k in range(1, N_DEVICES):
            px = 1 - x if k & 4 else x
            py = 1 - y if k & 2 else y
            pc = 1 - c if k & 1 else c
            peers.append((px, py, pc))
        sends = [_remote(v_ref, out_ref.at[me], send_sems, recv_sems, k, peer) for k, peer in enumerate(peers)]
        for cp in sends:
            cp.start()
        for k, (px, py, pc) in enumerate(peers):
            _remote(v_ref, out_ref.at[4 * px + 2 * py + pc], send_sems, recv_sems, k, (px, py, pc)).wait_recv()
        for cp in sends:
            cp.wait_send()
        mine.wait()

    out = _comm_call(body, name=name, n_in=1, out_shape=[_sds((N_DEVICES,) + v.shape, v.dtype)],
                     n_sems=N_DEVICES - 1)(v)
    return out[0]


def _window(ref, mode, layer, chip, rows, cols, half=None):
    first, count = (0, rows) if half is None else (half * (rows // 2), rows // 2)
    if mode == "slab":
        return ref.at[layer, chip] if half is None else ref.at[layer, chip, pl.ds(first, count), :]
    if mode == "cols":
        col_window = pl.ds(pl.multiple_of(chip * cols, LANES), cols)
        return ref.at[layer, :, col_window] if half is None else ref.at[layer, pl.ds(first, count), col_window]
    return ref.at[layer, pl.ds(pl.multiple_of(chip * rows + first, SUBLANES), count), :]


def _whole_shape(mode, shard_shape):
    layers, rows, cols = shard_shape
    if mode == "slab":
        return (layers, N_CHIPS, rows, cols)
    if mode == "cols":
        assert cols % LANES == 0
        return (layers, rows, N_CHIPS * cols)
    assert rows % 16 == 0
    return (layers, N_CHIPS * rows, cols)


def _gather_weights(shards, modes, *, name):
    n = len(shards)
    meta = [(mode,) + tuple(a.shape[1:]) for a, mode in zip(shards, modes)]
    for a in shards:
        assert a.shape[0] == 2 and a.shape[1] % 2 == 0
    per = 8

    def body(*refs):
        ins, outs = refs[:n], refs[n:2 * n]
        send_sems, recv_sems, _ = refs[2 * n:]
        x, y, c, _ = _place()
        me, x_nbr, y_nbr, diagonal = 2 * x + y, 2 * (1 - x) + y, 2 * x + 1 - y, 2 * (1 - x) + 1 - y
        to_x, to_y, sibling = (1 - x, y, c), (x, 1 - y, c), (x, y, 1 - c)
        sent = []

        def copy(src, dst, sem, to):
            cp = _remote(src, dst, send_sems, recv_sems, sem, to)
            cp.start()
            sent.append(cp)

        def arrived(win, sem):
            _remote(win, win, send_sems, recv_sems, sem, sibling).wait_recv()

        for i, (mode, rows, cols) in enumerate(meta):
            mine = _window(outs[i], mode, c, me, rows, cols)
            copy(ins[i].at[c], mine, per * i, to_x)
            copy(ins[i].at[c], mine, per * i + 1, to_y)
            copy(ins[i], _window(outs[i], mode, slice(None), me, rows, cols), per * i + 7, sibling)
        for i, (mode, rows, cols) in enumerate(meta):
            arrived(_window(outs[i], mode, c, x_nbr, rows, cols), per * i)
            half = _window(outs[i], mode, c, x_nbr, rows, cols, half=0)
            copy(half, half, per * i + 2, to_y)
            win = _window(outs[i], mode, c, x_nbr, rows, cols)
            copy(win, win, per * i + 4, sibling)
            arrived(_window(outs[i], mode, c, y_nbr, rows, cols), per * i + 1)
            half = _window(outs[i], mode, c, y_nbr, rows, cols, half=1)
            copy(half, half, per * i + 3, to_x)
            win = _window(outs[i], mode, c, y_nbr, rows, cols)
            copy(win, win, per * i + 5, sibling)
        for i, (mode, rows, cols) in enumerate(meta):
            arrived(_window(outs[i], mode, c, diagonal, rows, cols, half=0), per * i + 2)
            arrived(_window(outs[i], mode, c, diagonal, rows, cols, half=1), per * i + 3)
            win = _window(outs[i], mode, c, diagonal, rows, cols)
            copy(win, win, per * i + 6, sibling)
        for i, (mode, rows, cols) in enumerate(meta):
            arrived(_window(outs[i], mode, slice(None), me, rows, cols), per * i + 7)
            for k, chip in enumerate((x_nbr, y_nbr, diagonal)):
                arrived(_window(outs[i], mode, 1 - c, chip, rows, cols), per * i + 4 + k)
        for cp in sent:
            cp.wait_send()

    out_shape = [_sds(_whole_shape(mode, a.shape), a.dtype) for a, mode in zip(shards, modes)]
    return _comm_call(body, name=name, n_in=n, out_shape=out_shape, n_sems=per * n)(*shards)


def _rs_swap(grads, *, name):
    n = len(grads)

    def body(*refs):
        ins, outs = refs[:n], refs[n:2 * n]
        send_sems, recv_sems, _ = refs[2 * n:]
        x, y, c, _ = _place()
        copies = [_remote(ins[i].at[1 - c], outs[i], send_sems, recv_sems, i, (x, y, 1 - c)) for i in range(n)]
        for cp in copies:
            cp.start()
        for cp in copies:
            cp.wait()

    return _comm_call(body, name=name, n_in=n, out_shape=[_sds(g.shape[1:], g.dtype) for g in grads], n_sems=n)(*grads)


def _part(ref, mode, chip, rows, cols):
    if mode == "slab":
        return ref.at[chip]
    if mode == "cols":
        return ref.at[:, pl.ds(pl.multiple_of(chip * cols, LANES), cols)]
    return ref.at[pl.ds(pl.multiple_of(chip * rows, SUBLANES), rows), :]


def _rs_scatter(parts, modes, shard_shapes, *, name):
    n = len(parts)
    meta = [(mode,) + tuple(shp[1:]) for mode, shp in zip(modes, shard_shapes)]

    def body(*refs):
        ins, outs = refs[:n], refs[n:2 * n]
        send_sems, recv_sems, local_sems = refs[2 * n:]
        x, y, c, chips = _place()
        me = 2 * x + y
        local, sends = [], []
        for i, (mode, rows, cols) in enumerate(meta):
            cp = pltpu.make_async_copy(_part(ins[i], mode, me, rows, cols), outs[i].at[me], local_sems.at[i])
            cp.start()
            local.append(cp)
            for r, (cx, cy) in enumerate(chips):
                cp = _remote(_part(ins[i], mode, 2 * cx + cy, rows, cols), outs[i].at[me], send_sems, recv_sems,
                             3 * i + r, (cx, cy, c))
                cp.start()
                sends.append(cp)
        for i, (mode, rows, cols) in enumerate(meta):
            for r, (cx, cy) in enumerate(chips):
                k = 2 * cx + cy
                _remote(_part(ins[i], mode, k, rows, cols), outs[i].at[k], send_sems, recv_sems, 3 * i + r,
                        (cx, cy, c)).wait_recv()
        for cp in sends:
            cp.wait_send()
        for cp in local:
            cp.wait()

    out_shape = [_sds((N_CHIPS,) + tuple(shp[1:]), p.dtype) for p, shp in zip(parts, shard_shapes)]
    return _comm_call(body, name=name, n_in=n, out_shape=out_shape, n_sems=3 * n)(*parts)


def _rs_exchange(sums, *, name):
    n = len(sums)

    def body(*refs):
        ins, outs = refs[:n], refs[n:2 * n]
        send_sems, recv_sems, _ = refs[2 * n:]
        x, y, c, _ = _place()
        copies = [_remote(ins[i], outs[i], send_sems, recv_sems, i, (x, y, 1 - c)) for i in range(n)]
        for cp in copies:
            cp.start()
        for cp in copies:
            cp.wait()

    return _comm_call(body, name=name, n_in=n, out_shape=[_sds(s.shape, s.dtype) for s in sums], n_sems=n)(*sums)


def _row_tile(rows, cols, itemsize):
    target = max(SUBLANES, (2 << 20) // (cols * itemsize))
    cands = [c for c in (2048, 1024, 512, 256, 128, 64, 32, 16) if c <= target]
    tr = _pick(rows, cands)
    return tr


def _add_layer(g, other, core, *, name):
    _, rows, cols = g.shape
    tr = _row_tile(rows, cols, 4)

    def body(core_ref, g_ref, o_ref, out_ref):
        out_ref[...] = (g_ref[0] + o_ref[...]).astype(out_ref.dtype)

    grid_spec = pltpu.PrefetchScalarGridSpec(
        num_scalar_prefetch=1, grid=(rows // tr,),
        in_specs=[pl.BlockSpec((1, tr, cols), lambda i, core_ref: (core_ref[0], i, 0)),
                  pl.BlockSpec((tr, cols), lambda i, core_ref: (i, 0))],
        out_specs=pl.BlockSpec((tr, cols), lambda i, core_ref: (i, 0)))
    return pl.pallas_call(body, name=name, grid_spec=grid_spec, out_shape=_sds((rows, cols), BF16),
                          compiler_params=pltpu.CompilerParams(dimension_semantics=("parallel",),
                                                               vmem_limit_bytes=VMEM_LIMIT))(core, g, other)


def _sum_slots(parts, *, name):
    n, rows, cols = parts.shape
    tr = _row_tile(rows, cols, 4)

    def body(p_ref, o_ref):
        acc = p_ref[0].astype(F32) + p_ref[1].astype(F32)
        for k in range(2, n):
            acc = acc + p_ref[k].astype(F32)
        o_ref[...] = acc

    return _call(body, name=name, grid=(rows // tr,),
                 in_specs=[pl.BlockSpec((n, tr, cols), lambda i: (0, i, 0))],
                 out_specs=pl.BlockSpec((tr, cols), lambda i: (i, 0)),
                 out_shape=_sds((rows, cols), F32), sem=("parallel",))(parts)


def _reduce_scatter(grads, modes, shard_shapes):
    core = lax.axis_index("c").astype(jnp.int32).reshape(1)
    flat = [g.reshape(g.shape[0], -1, g.shape[-1]) for g in grads]
    from_sibling = _rs_swap(flat, name="rs_swap")
    parts = []
    for i, (g, o) in enumerate(zip(flat, from_sibling)):
        p = _add_layer(g, o, core, name=f"rs_add_{i}")
        parts.append(p.reshape(grads[i].shape[1:]))
    from_chips = _rs_scatter(parts, modes, shard_shapes, name="rs_scatter")
    sums = [_sum_slots(r, name=f"rs_sum_{i}") for i, r in enumerate(from_chips)]
    others = _rs_exchange(sums, name="rs_exchange")
    mine_first = lax.axis_index("c") == 0
    return [jnp.where(mine_first, jnp.stack([mine, other]), jnp.stack([other, mine]))
            for mine, other in zip(sums, others)]


def _layer_weights(full, rep, layer, dims):
    f_off, n_heads = dims["f_off"], dims["heads"]
    d_ff = full["w_ffn_up"].shape[-1] // 2
    b_in = rep["b_in"][layer]
    pad = LANES - n_heads
    return {
        "w_main": full["w_main"][layer],
        "b_main": jnp.concatenate([b_in[:f_off], b_in[f_off + n_heads:]])[None],
        "w_f": full["w_f"][layer],
        "b_f": jnp.pad(b_in[f_off:f_off + n_heads], (0, pad))[None],
        "conv_a_w": full["conv_a_w"][layer],
        "conv_a_b": rep["conv_a_b"][layer][None],
        "ln_conv_g": rep["ln_conv_g"][layer][None],
        "ln_conv_b": rep["ln_conv_b"][layer][None],
        "w_conv_proj": full["w_conv_proj"][layer],
        "w_attn_proj": full["w_attn_proj"][layer],
        "w_mix_out": full["w_mix_out"][layer],
        "b_mix_out": rep["b_mix_out"][layer][None],
        "ln1_g": rep["ln1_g"][layer][None],
        "ln1_b": rep["ln1_b"][layer][None],
        "w_ffn_up": full["w_ffn_up"][layer],
        "w_ffn_up_gate": full["w_ffn_up"][layer][:, :d_ff],
        "w_ffn_up_lin": full["w_ffn_up"][layer][:, d_ff:],
        "ffn_conv_w": full["ffn_conv_w"][layer],
        "ffn_conv_b": rep["ffn_conv_b"][layer][None],
        "w_ffn_down": full["w_ffn_down"][layer],
        "ln2_g": rep["ln2_g"][layer][None],
        "ln2_b": rep["ln2_b"][layer][None],
    }


def _layer_fwd(x, mod, p, dims, tag):
    bsz, d, ch, heads, alpha = dims["bsz"], dims["d"], dims["ch"], dims["heads"], dims["alpha"]
    mods = [mod[:, k * d:(k + 1) * d][:, None, :] for k in range(6)]
    shift1, scale1, gate1, shift2, scale2, gate2 = mods
    u = _ln_mod_fwd(x, scale1, shift1, bsz, name=f"ln_mod1_{tag}")
    zm = _matmul(u, p["w_main"], "nn", BF16, bias=p["b_main"], name=f"in_main_{tag}")
    zf = _matmul(u, p["w_f"], "nn", F32, bias=p["b_f"], name=f"in_forget_{tag}")
    a3 = _conv_branch_fwd(zm, p["conv_a_w"], p["conv_a_b"], p["ln_conv_g"], p["ln_conv_b"], bsz, ch,
                          name=f"conv_branch_{tag}")
    ya = _matmul(a3, p["w_conv_proj"], "nn", F32, name=f"conv_proj_{tag}")
    cumt, cumb = _fgate_fwd(zf, bsz, heads, name=f"fgate_{tag}")
    qkvt = _to_features_major(zm, 2 * ch, heads * HEAD_DIM, 3, name=f"qkv_t_{tag}")
    ot, lse = _attn_fwd(zm, qkvt, cumt, cumb, bsz, heads, 2 * ch, name=f"attn_{tag}")
    yb = _matmul(ot, p["w_attn_proj"], "tn", F32, name=f"attn_proj_{tag}")
    m = _gate_merge_fwd(zm, ya, yb, dims["ga_off"], name=f"merge_{tag}")
    mix = _matmul(m, p["w_mix_out"], "nn", F32, bias=p["b_mix_out"], name=f"mix_out_{tag}")
    x1 = _ln_res_fwd(x, mix, gate1, p["ln1_g"], p["ln1_b"], alpha, bsz, name=f"ln_res1_{tag}")
    u2 = _ln_mod_fwd(x1, scale2, shift2, bsz, name=f"ln_mod2_{tag}")
    hp = _matmul(u2, p["w_ffn_up"], "nn", BF16, name=f"ffn_up_{tag}")
    f = _ffn_act_fwd(hp, p["ffn_conv_w"], p["ffn_conv_b"], bsz, dims["tcf"], name=f"ffn_act_{tag}")
    ffn = _matmul(f, p["w_ffn_down"], "nn", F32, name=f"ffn_down_{tag}")
    x2 = _ln_res_fwd(x1, ffn, gate2, p["ln2_g"], p["ln2_b"], alpha, bsz, name=f"ln_res2_{tag}")
    saved = dict(x=x, mods=mods, u=u, zm=zm, zf=zf, a3=a3, ya=ya, yb=yb, cumt=cumt, cumb=cumb,
                 qkvt=qkvt, ot=ot, lse=lse, m=m, mix=mix, x1=x1, u2=u2, hp=hp, f=f, ffn=ffn)
    return x2, saved


def _layer_bwd(dx2, p, sv, dims, tag):
    bsz, ch, heads, alpha = dims["bsz"], dims["ch"], dims["heads"], dims["alpha"]
    f_off, tcf = dims["f_off"], dims["tcf"]
    shift1, scale1, gate1, shift2, scale2, gate2 = sv["mods"]
    g = {}
    dr2, dffn, dgate2, g["ln2_g"], g["ln2_b"], _ = _ln_res_bwd(
        dx2, sv["x1"], sv["ffn"], gate2, p["ln2_g"], alpha, bsz, name=f"ln_res2_bwd_{tag}")
    df = _matmul(dffn, p["w_ffn_down"], "nt", F32, name=f"ffn_down_dx_{tag}")
    g["w_ffn_down"] = _matmul(sv["f"], dffn, "tn", F32, name=f"ffn_down_dw_{tag}")
    dhg, dhl, dwg, dwl, dbg, dbl = _ffn_act_bwd(sv["hp"], df, p["ffn_conv_w"], p["ffn_conv_b"], bsz, tcf,
                                                name=f"ffn_act_bwd_{tag}")
    g["ffn_conv_w"] = jnp.concatenate([dwg, dwl], axis=1)
    g["ffn_conv_b"] = jnp.concatenate([dbg, dbl], axis=1)[0]
    du2 = _matmul(dhg, p["w_ffn_up_gate"], "nt", F32, name=f"ffn_up_gate_dx_{tag}")
    du2 = _matmul(dhl, p["w_ffn_up_lin"], "nt", F32, add=du2, name=f"ffn_up_lin_dx_{tag}")
    g["w_ffn_up"] = jnp.concatenate([_matmul(sv["u2"], dhg, "tn", F32, name=f"ffn_up_gate_dw_{tag}"),
                                     _matmul(sv["u2"], dhl, "tn", F32, name=f"ffn_up_lin_dw_{tag}")], axis=1)
    dx1, dscale2, dshift2 = _ln_mod_bwd(du2, sv["x1"], scale2, dr2, alpha, bsz, name=f"ln_mod2_bwd_{tag}")
    dr1, dmix, dgate1, g["ln1_g"], g["ln1_b"], g["b_mix_out"] = _ln_res_bwd(
        dx1, sv["x"], sv["mix"], gate1, p["ln1_g"], alpha, bsz, name=f"ln_res1_bwd_{tag}")
    dm = _matmul(dmix, p["w_mix_out"], "nt", F32, name=f"mix_out_dx_{tag}")
    g["w_mix_out"] = _matmul(sv["m"], dmix, "tn", F32, name=f"mix_out_dw_{tag}")
    dya, dyb, dzga, dzgb = _gate_merge_bwd(sv["zm"], sv["ya"], sv["yb"], dm, dims["ga_off"], name=f"merge_bwd_{tag}")
    da3 = _matmul(dya, p["w_conv_proj"], "nt", F32, name=f"conv_proj_dx_{tag}")
    g["w_conv_proj"] = _matmul(sv["a3"], dya, "tn", F32, name=f"conv_proj_dw_{tag}")
    do = _matmul(dyb, p["w_attn_proj"], "nt", BF16, name=f"attn_proj_dx_{tag}")
    dot = _matmul(p["w_attn_proj"], dyb, "nt", BF16, name=f"attn_proj_dxt_{tag}")
    g["w_attn_proj"] = _matmul(sv["ot"], dyb, "nn", F32, name=f"attn_proj_dw_{tag}")
    dzglu, g["conv_a_w"], dcb, g["ln_conv_g"], g["ln_conv_b"] = _conv_branch_bwd(
        sv["zm"], da3, p["conv_a_w"], p["conv_a_b"], p["ln_conv_g"], p["ln_conv_b"], bsz, ch,
        name=f"conv_branch_bwd_{tag}")
    g["conv_a_b"] = dcb[0]
    dk, dv, dqt, dcum = _attn_bwd(sv["zm"], sv["qkvt"], sv["cumt"], sv["cumb"], sv["ot"], do, dot, sv["lse"], bsz,
                                  heads, 2 * ch, name=f"attn_bwd_{tag}")
    dq = _to_rows_major(dqt, name=f"dq_rows_{tag}")
    dzf = _fgate_bwd(dcum, sv["zf"], bsz, name=f"fgate_bwd_{tag}")
    dzm = jnp.concatenate([dzglu, dq, dk, dv, dzga, dzgb], axis=1)
    du = _matmul(dzm, p["w_main"], "nt", F32, name=f"in_main_dx_{tag}")
    du = _matmul(dzf, p["w_f"], "nt", F32, add=du, name=f"in_forget_dx_{tag}")
    dwm, dbm = _matmul(sv["u"], dzm, "tn", F32, colsum=True, name=f"in_main_dw_{tag}")
    dwf, dbf = _matmul(sv["u"], dzf, "tn", F32, colsum=True, name=f"in_forget_dw_{tag}")
    dbm, dbf = dbm[0], dbf[0]
    g["w_main"], g["w_f"] = dwm, dwf
    g["b_in"] = jnp.concatenate([dbm[:f_off], dbf[:heads], dbm[f_off:]])
    dx, dscale1, dshift1 = _ln_mod_bwd(du, sv["x"], scale1, dr1, alpha, bsz, name=f"ln_mod1_bwd_{tag}")
    dmod = jnp.concatenate([dshift1, dscale1, dgate1, dshift2, dscale2, dgate2], axis=2)[:, 0, :]
    return dx, g, dmod


def _local_step(x, mod, loss_target, full, rep, dims):
    bsz, seq, d = x.shape
    layers = mod.shape[0]
    params = [_layer_weights(full, rep, layer, dims) for layer in range(layers)]
    h = x.reshape(bsz * seq, d)
    saved = []
    for layer in range(layers):
        h, sv = _layer_fwd(h, mod[layer], params[layer], dims, f"l{layer}")
        saved.append(sv)
    dh, sq = _loss_head(h, loss_target.reshape(bsz * seq, d), name="loss_head")
    loss_local = 0.5 * jnp.sum(sq) / d
    grads, dmods = [None] * layers, [None] * layers
    for layer in reversed(range(layers)):
        dh, grads[layer], dmods[layer] = _layer_bwd(dh, params[layer], saved[layer], dims, f"l{layer}")
    per_layer = ("w_main", "w_f")
    stacked = {wname: [grads[layer][wname] for layer in range(layers)] if wname in per_layer
               else jnp.stack([grads[layer][wname] for layer in range(layers)]) for wname in grads[0]}
    return loss_local, dh.reshape(bsz, seq, d), stacked, jnp.stack(dmods)


def _pad_rows(a):
    extra = -a.shape[-2] % (2 * SUBLANES)
    if extra == 0:
        return a
    return jnp.pad(a, [(0, 0)] * (a.ndim - 2) + [(0, extra), (0, 0)])


def _w_in_pieces(n, f_off, heads):
    n_in = N_CHIPS * n
    segments = [(0, f_off, "main", 0), (f_off, f_off + heads, "f", 0), (f_off + heads, n_in, "main", f_off)]
    pieces = []
    for chip in range(N_CHIPS):
        lo, hi = chip * n, (chip + 1) * n
        for a, b, target, t0 in segments:
            s, e = max(lo, a), min(hi, b)
            if s < e:
                pieces.append((chip, s - lo, e - lo, target, t0 + s - a))
    return pieces


def _w_in_from_slabs(slabs, f_off, heads, *, name):
    layers, _, k, n = slabs.shape
    tr = _pick(k, (256, 128, 64, 32, 16))
    n_main = N_CHIPS * n - heads
    pieces = _w_in_pieces(n, f_off, heads)

    def body(s_ref, m_ref, f_ref):
        f_ref[...] = jnp.zeros_like(f_ref)
        for chip in range(N_CHIPS):
            slab = s_ref[0, chip].astype(F32)
            for pc, s0, s1, target, t0 in pieces:
                if pc == chip:
                    out = m_ref if target == "main" else f_ref
                    out[0, :, t0:t0 + s1 - s0] = slab[:, s0:s1].astype(out.dtype)

    return _call(body, name=name, grid=(layers, k // tr),
                 in_specs=[pl.BlockSpec((1, N_CHIPS, tr, n), lambda layer, i: (layer, 0, i, 0))],
                 out_specs=[pl.BlockSpec((1, tr, n_main), lambda layer, i: (layer, i, 0)),
                            pl.BlockSpec((1, tr, LANES), lambda layer, i: (layer, i, 0))],
                 out_shape=[_sds((layers, k, n_main), slabs.dtype), _sds((layers, k, LANES), slabs.dtype)],
                 sem=("parallel", "parallel"))(slabs)


def _w_in_to_slabs(d_main, d_f, n, f_off, heads, *, name):
    layers = len(d_main)
    k = d_main[0].shape[0]
    tr = _pick(k, (128, 64, 32, 16, 8))
    pieces = _w_in_pieces(n, f_off, heads)

    def body(*refs):
        m_refs, f_refs, o_ref = refs[:layers], refs[layers:2 * layers], refs[2 * layers]
        for layer in range(layers):
            for chip, s0, s1, target, t0 in pieces:
                src = m_refs[layer] if target == "main" else f_refs[layer]
                o_ref[layer, chip, :, s0:s1] = src[:, t0:t0 + s1 - s0]

    return _call(body, name=name, grid=(k // tr,),
                 in_specs=[pl.BlockSpec((tr, d_main[0].shape[1]), lambda i: (i, 0))] * layers
                 + [pl.BlockSpec((tr, LANES), lambda i: (i, 0))] * layers,
                 out_specs=pl.BlockSpec((layers, N_CHIPS, tr, n), lambda i: (0, 0, i, 0)),
                 out_shape=_sds((layers, N_CHIPS, k, n), F32), sem=("parallel",))(*d_main, *d_f)


def kernel(x, c, w_ada, b_ada, w_in, b_in, conv_a_w, conv_a_b, ln_conv_g, ln_conv_b, w_conv_proj, w_attn_proj, w_mix_out, b_mix_out, ln1_g, ln1_b, w_ffn_up, ffn_conv_w, ffn_conv_b, w_ffn_down, ln2_g, ln2_b, loss_target, m_w_ada, m_b_ada, m_w_in, m_b_in, m_conv_a_w, m_conv_a_b, m_ln_conv_g, m_ln_conv_b, m_w_conv_proj, m_w_attn_proj, m_w_mix_out, m_b_mix_out, m_ln1_g, m_ln1_b, m_w_ffn_up, m_ffn_conv_w, m_ffn_conv_b, m_w_ffn_down, m_ln2_g, m_ln2_b, v_w_ada, v_b_ada, v_w_in, v_b_in, v_conv_a_w, v_conv_a_b, v_ln_conv_g, v_ln_conv_b, v_w_conv_proj, v_w_attn_proj, v_w_mix_out, v_b_mix_out, v_ln1_g, v_ln1_b, v_w_ffn_up, v_ffn_conv_w, v_ffn_conv_b, v_w_ffn_down, v_ln2_g, v_ln2_b):
    weights = dict(zip(WEIGHTS, (w_ada, b_ada, w_in, b_in, conv_a_w, conv_a_b, ln_conv_g, ln_conv_b, w_conv_proj,
                                 w_attn_proj, w_mix_out, b_mix_out, ln1_g, ln1_b, w_ffn_up, ffn_conv_w, ffn_conv_b,
                                 w_ffn_down, ln2_g, ln2_b)))
    mom1 = dict(zip(WEIGHTS, (m_w_ada, m_b_ada, m_w_in, m_b_in, m_conv_a_w, m_conv_a_b, m_ln_conv_g, m_ln_conv_b,
                              m_w_conv_proj, m_w_attn_proj, m_w_mix_out, m_b_mix_out, m_ln1_g, m_ln1_b, m_w_ffn_up,
                              m_ffn_conv_w, m_ffn_conv_b, m_w_ffn_down, m_ln2_g, m_ln2_b)))
    mom2 = dict(zip(WEIGHTS, (v_w_ada, v_b_ada, v_w_in, v_b_in, v_conv_a_w, v_conv_a_b, v_ln_conv_g, v_ln_conv_b,
                              v_w_conv_proj, v_w_attn_proj, v_w_mix_out, v_b_mix_out, v_ln1_g, v_ln1_b, v_w_ffn_up,
                              v_ffn_conv_w, v_ffn_conv_b, v_w_ffn_down, v_ln2_g, v_ln2_b)))
    bsz, seq, d = x.shape
    layers = w_ada.shape[0]
    ch = conv_a_w.shape[2] * N_CHIPS
    width = w_attn_proj.shape[1]
    heads = width // HEAD_DIM
    d_ff = w_ffn_down.shape[1] * N_CHIPS
    dims = dict(bsz=bsz, d=d, ch=ch, heads=heads, alpha=(2.0 * layers) ** 0.25, f_off=2 * ch + 3 * width,
                ga_off=2 * ch + 3 * width, tcf=_pick(d_ff, (256, 128)))
    chip = 2 * lax.axis_index("x") + lax.axis_index("y")
    device = 2 * chip + lax.axis_index("c")
    ada_cols = w_ada.shape[2]

    c_act = _silu_rows(_all_gather8(c, name="gather_c").reshape(N_DEVICES * bsz, d), name="silu_c")
    b_ada_mine = lax.dynamic_slice_in_dim(b_ada, chip * ada_cols, ada_cols, axis=1)
    mod_cols = jnp.stack([_matmul(c_act, w_ada[layer], "nn", F32, bias=b_ada_mine[layer][None], name=f"ada_l{layer}")
                          for layer in range(layers)])
    mod_all = _all_gather8(mod_cols, name="gather_mod")
    mod_all = jnp.concatenate([mod_all[2 * k] for k in range(N_CHIPS)], axis=-1)
    mod = lax.dynamic_slice_in_dim(mod_all, device * bsz, bsz, axis=1)

    shards = [_pad_rows(weights[wname].astype(BF16) if as_bf16 else weights[wname]) for wname, _, as_bf16 in GATHERED]
    modes = [mode for _, mode, _ in GATHERED]
    whole = _gather_weights(shards, modes, name="gather_weights")
    full = {wname: w[:, :weights[wname].shape[1]] if mode == "cols" else w
            for (wname, mode, _), w in zip(GATHERED, whole)}
    full["w_main"], full["w_f"] = _w_in_from_slabs(full.pop("w_in"), dims["f_off"], heads, name="w_in_from_slabs")
    rep = {wname: weights[wname] for wname in REPLICATED}

    loss_local, grad_x, grads, dmod = _local_step(x, mod, loss_target, full, rep, dims)
    loss = lax.psum(loss_local, ("x", "y", "c"))

    grads["w_in"] = _w_in_to_slabs(grads.pop("w_main"), grads.pop("w_f"), w_in.shape[2], dims["f_off"], heads,
                                   name="w_in_to_slabs")
    shard_shapes = [s.shape for s in shards]
    reduced = _reduce_scatter([_pad_rows(grads[wname]) for wname, _, _ in GATHERED], modes, shard_shapes)
    grad = {wname: r[:, :weights[wname].shape[1]] for (wname, _, _), r in zip(GATHERED, reduced)}

    small = jnp.concatenate([dmod.reshape(-1)] + [grads[wname].reshape(-1) for wname in REPLICATED])
    n_small = small.shape[0]
    rows = -(-n_small // (SUBLANES * LANES)) * SUBLANES
    small = jnp.pad(small, (0, rows * LANES - n_small)).reshape(rows, LANES)
    gathered = _all_gather8(small, name="gather_small")
    n_dmod = dmod.size
    dmod_all = gathered.reshape(N_DEVICES, -1)[:, :n_dmod].reshape(N_DEVICES, layers, bsz, 6 * d)
    dmod_all = jnp.transpose(dmod_all, (1, 0, 2, 3)).reshape(layers, N_DEVICES * bsz, 6 * d)
    summed = _sum_slots(gathered, name="sum_small").reshape(-1)
    off = n_dmod
    for wname in REPLICATED:
        n = weights[wname].size
        grad[wname] = summed[off:off + n].reshape(weights[wname].shape)
        off += n
    dmod_mine = lax.dynamic_slice_in_dim(dmod_all, chip * ada_cols, ada_cols, axis=2)
    grad["w_ada"] = jnp.stack([_matmul(c_act, dmod_mine[layer], "tn", F32, name=f"ada_dw_l{layer}")
                               for layer in range(layers)])
    grad["b_ada"] = jnp.stack([_colsum(dmod_all[layer], name=f"ada_db_l{layer}")[0] for layer in range(layers)])

    delta, new_m, new_v = {}, {}, {}
    for wname in WEIGHTS:
        delta[wname], new_m[wname], new_v[wname] = _adamw(weights[wname], grad[wname], mom1[wname], mom2[wname],
                                                          name=f"adamw_{wname}")
    return (loss, grad_x, *[grad[wname] for wname in WEIGHTS], *[delta[wname] for wname in WEIGHTS],
            *[new_m[wname] for wname in WEIGHTS], *[new_v[wname] for wname in WEIGHTS])
```

```python
import math

import jax
import jax.numpy as jnp
from jax import lax
from jax.experimental import pallas as pl
from jax.experimental.pallas import tpu as pltpu

F32 = jnp.float32
BF16 = jnp.bfloat16
MESH = pl.DeviceIdType.MESH

LN_EPS = 1e-5
HEAD_DIM = 64
ATTN_SCALE = HEAD_DIM ** -0.5
NEG = -1e30
FFN_PAD = 8
LANES = 128
SUBLANES = 8
ROW_CHUNK = 256
ATTN_BLOCK = 256
ATTN_UNROLL = 3
N_CHIPS = 4
N_DEVICES = 8
VMEM_LIMIT = 56 * 1024 * 1024

ADAM_LR = 0.001
ADAM_B1 = 0.9
ADAM_B2 = 0.999
ADAM_EPS = 1e-08
ADAM_WD = 0.01
ADAM_STEP = 10

GATHERED = (("w_in", "slab", True), ("conv_a_w", "cols", False), ("w_conv_proj", "cols", True),
            ("w_attn_proj", "cols", True), ("w_mix_out", "rows", True), ("w_ffn_up", "cols", True),
            ("ffn_conv_w", "cols", False), ("w_ffn_down", "rows", True))
REPLICATED = ("b_in", "conv_a_b", "ln_conv_g", "ln_conv_b", "b_mix_out", "ln1_g", "ln1_b",
              "ffn_conv_b", "ln2_g", "ln2_b")
WEIGHTS = ("w_ada", "b_ada", "w_in", "b_in", "conv_a_w", "conv_a_b", "ln_conv_g", "ln_conv_b",
           "w_conv_proj", "w_attn_proj", "w_mix_out", "b_mix_out", "ln1_g", "ln1_b", "w_ffn_up",
           "ffn_conv_w", "ffn_conv_b", "w_ffn_down", "ln2_g", "ln2_b")


def _pick(n, cands):
    for cand in cands:
        if n % cand == 0:
            return cand
    return n


def _call(body, *, name, grid, in_specs, out_specs, out_shape, scratch=(), sem=None):
    return pl.pallas_call(
        body, name=name, grid=grid, in_specs=in_specs, out_specs=out_specs, out_shape=out_shape,
        scratch_shapes=list(scratch),
        compiler_params=pltpu.CompilerParams(dimension_semantics=sem, vmem_limit_bytes=VMEM_LIMIT))


def _sds(shape, dtype):
    return jax.ShapeDtypeStruct(tuple(shape), dtype)


def _chunked(rows, fn):
    chunk = min(ROW_CHUNK, rows)
    if rows == chunk:
        fn(pl.ds(0, rows))
        return

    def step(i, carry):
        fn(pl.ds(pl.multiple_of(i * chunk, chunk), chunk))
        return carry

    lax.fori_loop(0, rows // chunk, step, 0)


def _matmul(a, b, mode, out_dtype, *, bias=None, add=None, colsum=False, name):
    if mode == "nn":
        (m, k), (_, n) = a.shape, b.shape
    elif mode == "nt":
        (m, k), (n, _) = a.shape, b.shape
    else:
        (k, m), (_, n) = a.shape, b.shape
    tm = _pick(m, (1024, 1408, 512, 256, 128))
    tn = _pick(n, (1536, 1408, 1024, 512, 256, 128))
    tk = k if k <= 1536 else _pick(k, (1024, 1536, 1408, 512, 256, 128))
    nk = k // tk
    if mode == "nn":
        a_spec = pl.BlockSpec((tm, tk), lambda i, j, kk: (i, kk))
        b_spec = pl.BlockSpec((tk, tn), lambda i, j, kk: (kk, j))
        dims = (((1,), (0,)), ((), ()))
    elif mode == "nt":
        a_spec = pl.BlockSpec((tm, tk), lambda i, j, kk: (i, kk))
        b_spec = pl.BlockSpec((tn, tk), lambda i, j, kk: (j, kk))
        dims = (((1,), (1,)), ((), ()))
    else:
        a_spec = pl.BlockSpec((tk, tm), lambda i, j, kk: (kk, i))
        b_spec = pl.BlockSpec((tk, tn), lambda i, j, kk: (kk, j))
        dims = (((0,), (0,)), ((), ()))
    in_specs = [a_spec, b_spec]
    operands = [a, b]
    if bias is not None:
        in_specs.append(pl.BlockSpec((1, tn), lambda i, j, kk: (0, j)))
        operands.append(bias)
    if add is not None:
        in_specs.append(pl.BlockSpec((tm, tn), lambda i, j, kk: (i, j)))
        operands.append(add)

    def body(a_ref, b_ref, *rest):
        rest = list(rest)
        bias_ref = rest.pop(0) if bias is not None else None
        add_ref = rest.pop(0) if add is not None else None
        o_ref = rest.pop(0)
        prod = lax.dot_general(a_ref[...].astype(BF16), b_ref[...].astype(BF16), dims,
                               preferred_element_type=F32)
        if colsum:
            cs_ref = rest.pop(0)
            part = jnp.sum(b_ref[...].astype(F32), axis=0, keepdims=True)

            @pl.when(pl.program_id(2) == 0)
            def _():
                cs_ref[...] = part

            @pl.when(pl.program_id(2) > 0)
            def _():
                cs_ref[...] += part

        def finish(r):
            if bias_ref is not None:
                r = r + bias_ref[...]
            if add_ref is not None:
                r = r + add_ref[...]
            o_ref[...] = r.astype(o_ref.dtype)

        if nk == 1:
            finish(prod)
            return
        acc_ref = rest.pop(0)
        kk = pl.program_id(2)

        @pl.when(kk == 0)
        def _():
            acc_ref[...] = prod

        @pl.when(kk > 0)
        def _():
            acc_ref[...] += prod

        @pl.when(kk == nk - 1)
        def _():
            finish(acc_ref[...])

    out_specs = pl.BlockSpec((tm, tn), lambda i, j, kk: (i, j))
    out_shape = _sds((m, n), out_dtype)
    if colsum:
        assert mode == "tn" and m == tm
        out_specs = [out_specs, pl.BlockSpec((1, tn), lambda i, j, kk: (0, j))]
        out_shape = [out_shape, _sds((1, n), F32)]
    return _call(body, name=name, grid=(m // tm, n // tn, nk), in_specs=in_specs, out_specs=out_specs,
                 out_shape=out_shape, scratch=[pltpu.VMEM((tm, tn), F32)] if nk > 1 else [],
                 sem=("parallel", "parallel", "arbitrary"))(*operands)


def _colsum(x, *, name):
    rows, n = x.shape
    tr = _pick(rows, (1024, 512, 256, 128))
    tn = _pick(n, (512, 256, 128))

    def body(x_ref, o_ref):
        @pl.when(pl.program_id(1) == 0)
        def _():
            o_ref[...] = jnp.zeros_like(o_ref)

        o_ref[...] += jnp.sum(x_ref[...].astype(F32), axis=0, keepdims=True)

    return _call(body, name=name, grid=(n // tn, rows // tr),
                 in_specs=[pl.BlockSpec((tr, tn), lambda j, i: (i, j))],
                 out_specs=pl.BlockSpec((1, tn), lambda j, i: (0, j)),
                 out_shape=_sds((1, n), F32), sem=("parallel", "arbitrary"))(x)


def _ln_stats(x):
    mu = jnp.mean(x, axis=-1, keepdims=True)
    xc = x - mu
    var = jnp.mean(xc * xc, axis=-1, keepdims=True)
    rstd = lax.rsqrt(var + LN_EPS)
    return xc * rstd, rstd


def _ln_bwd(dn, n, rstd):
    return rstd * (dn - jnp.mean(dn, axis=-1, keepdims=True) - n * jnp.mean(dn * n, axis=-1, keepdims=True))


def _seq_tiles(t, bsz, cands=(1024, 512, 256, 128, 64, 32, 16, 8)):
    s = t // bsz
    ts = _pick(s, cands)
    return s, ts, s // ts


def _ln_mod_fwd(x, scale, shift, bsz, *, name):
    t, d = x.shape
    _, ts, ns = _seq_tiles(t, bsz)

    def body(x_ref, sc_ref, sh_ref, u_ref):
        one_scale = 1.0 + sc_ref[0]
        shift_v = sh_ref[0]

        def piece(rows):
            n, _ = _ln_stats(x_ref[rows, :])
            u_ref[rows, :] = (n * one_scale + shift_v).astype(u_ref.dtype)

        _chunked(ts, piece)

    row = pl.BlockSpec((ts, d), lambda b, i: (b * ns + i, 0))
    per = pl.BlockSpec((1, 1, d), lambda b, i: (b, 0, 0))
    return _call(body, name=name, grid=(bsz, ns), in_specs=[row, per, per], out_specs=row,
                 out_shape=_sds((t, d), BF16), sem=("parallel", "parallel"))(x, scale, shift)


def _ln_mod_bwd(du, x, scale, dr, alpha, bsz, *, name):
    t, d = x.shape
    _, ts, ns = _seq_tiles(t, bsz)

    def body(du_ref, x_ref, sc_ref, dr_ref, dx_ref, dsc_ref, dsh_ref):
        @pl.when(pl.program_id(1) == 0)
        def _():
            dsc_ref[...] = jnp.zeros_like(dsc_ref)
            dsh_ref[...] = jnp.zeros_like(dsh_ref)

        one_scale = 1.0 + sc_ref[0]

        def piece(rows):
            du_v = du_ref[rows, :]
            n, rstd = _ln_stats(x_ref[rows, :])
            dsc_ref[0] += jnp.sum(du_v * n, axis=0, keepdims=True)
            dsh_ref[0] += jnp.sum(du_v, axis=0, keepdims=True)
            dx_ref[rows, :] = alpha * dr_ref[rows, :] + _ln_bwd(du_v * one_scale, n, rstd)

        _chunked(ts, piece)

    row = pl.BlockSpec((ts, d), lambda b, i: (b * ns + i, 0))
    per = pl.BlockSpec((1, 1, d), lambda b, i: (b, 0, 0))
    return _call(body, name=name, grid=(bsz, ns), in_specs=[row, row, per, row],
                 out_specs=[row, per, per],
                 out_shape=[_sds((t, d), F32), _sds((bsz, 1, d), F32), _sds((bsz, 1, d), F32)],
                 sem=("parallel", "arbitrary"))(du, x, scale, dr)


def _ln_res_fwd(x, y, gate, g, b, alpha, bsz, *, name):
    t, d = x.shape
    _, ts, ns = _seq_tiles(t, bsz)

    def body(x_ref, y_ref, gt_ref, g_ref, b_ref, o_ref):
        one_gate = 1.0 + gt_ref[0]

        def piece(rows):
            n, _ = _ln_stats(alpha * x_ref[rows, :] + one_gate * y_ref[rows, :])
            o_ref[rows, :] = n * g_ref[...] + b_ref[...]

        _chunked(ts, piece)

    row = pl.BlockSpec((ts, d), lambda bb, i: (bb * ns + i, 0))
    per = pl.BlockSpec((1, 1, d), lambda bb, i: (bb, 0, 0))
    vec = pl.BlockSpec((1, d), lambda bb, i: (0, 0))
    return _call(body, name=name, grid=(bsz, ns), in_specs=[row, row, per, vec, vec], out_specs=row,
                 out_shape=_sds((t, d), F32), sem=("parallel", "parallel"))(x, y, gate, g, b)


def _ln_res_bwd(do, x, y, gate, g, alpha, bsz, *, name):
    t, d = x.shape
    _, ts, ns = _seq_tiles(t, bsz)

    def body(do_ref, x_ref, y_ref, gt_ref, g_ref, dr_ref, dy_ref, dgt_ref, dg_ref, db_ref, dys_ref):
        first_tile = pl.program_id(1) == 0

        @pl.when(first_tile)
        def _():
            dgt_ref[...] = jnp.zeros_like(dgt_ref)

        @pl.when(jnp.logical_and(first_tile, pl.program_id(0) == 0))
        def _():
            dg_ref[...] = jnp.zeros_like(dg_ref)
            db_ref[...] = jnp.zeros_like(db_ref)
            dys_ref[...] = jnp.zeros_like(dys_ref)

        one_gate = 1.0 + gt_ref[0]

        def piece(rows):
            do_v = do_ref[rows, :]
            y_v = y_ref[rows, :]
            n, rstd = _ln_stats(alpha * x_ref[rows, :] + one_gate * y_v)
            dg_ref[...] += jnp.sum(do_v * n, axis=0, keepdims=True)
            db_ref[...] += jnp.sum(do_v, axis=0, keepdims=True)
            dr = _ln_bwd(do_v * g_ref[...], n, rstd)
            dr_ref[rows, :] = dr
            dy = one_gate * dr
            dy_ref[rows, :] = dy.astype(dy_ref.dtype)
            dys_ref[...] += jnp.sum(dy, axis=0, keepdims=True)
            dgt_ref[0] += jnp.sum(dr * y_v, axis=0, keepdims=True)

        _chunked(ts, piece)

    row = pl.BlockSpec((ts, d), lambda bb, i: (bb * ns + i, 0))
    per = pl.BlockSpec((1, 1, d), lambda bb, i: (bb, 0, 0))
    vec = pl.BlockSpec((1, d), lambda bb, i: (0, 0))
    return _call(body, name=name, grid=(bsz, ns), in_specs=[row, row, row, per, vec],
                 out_specs=[row, row, per, vec, vec, vec],
                 out_shape=[_sds((t, d), F32), _sds((t, d), BF16), _sds((bsz, 1, d), F32),
                            _sds((1, d), F32), _sds((1, d), F32), _sds((1, d), F32)],
                 sem=("arbitrary", "arbitrary"))(do, x, y, gate, g)


def _loss_head(y, target, *, name):
    t, d = y.shape
    tr = _pick(t, (1024, 512, 256, 128, 64, 32, 16, 8))

    def body(y_ref, t_ref, dy_ref, s_ref):
        @pl.when(pl.program_id(0) == 0)
        def _():
            s_ref[...] = jnp.zeros_like(s_ref)

        def piece(rows):
            e = y_ref[rows, :] - t_ref[rows, :]
            dy_ref[rows, :] = e * (1.0 / d)
            s_ref[...] += jnp.sum(e * e, axis=0, keepdims=True)

        _chunked(tr, piece)

    row = pl.BlockSpec((tr, d), lambda i: (i, 0))
    return _call(body, name=name, grid=(t // tr,), in_specs=[row, row],
                 out_specs=[row, pl.BlockSpec((1, d), lambda i: (0, 0))],
                 out_shape=[_sds((t, d), F32), _sds((1, d), F32)], sem=("arbitrary",))(y, target)


def _sigmoid(v):
    return 1.0 / (1.0 + jnp.exp(-v))


def _silu_rows(c, *, name):
    rows, d = c.shape

    def body(c_ref, o_ref):
        v = c_ref[...]
        o_ref[...] = (v * _sigmoid(v)).astype(o_ref.dtype)

    full = pl.BlockSpec((rows, d), lambda i: (0, 0))
    return _call(body, name=name, grid=(1,), in_specs=[full], out_specs=full,
                 out_shape=_sds((rows, d), BF16), sem=("arbitrary",))(c)


def _gate_cols(d, ga_off):
    tc = _pick(math.gcd(d, ga_off), (512, 256, 128))
    return tc, ga_off // tc, (ga_off + d) // tc


def _gate_merge_fwd(z, ya, yb, ga_off, *, name):
    t, d = ya.shape
    tr = _pick(t, (1024, 512, 256, 128, 64, 32, 16, 8))
    tc, ga_blk, gb_blk = _gate_cols(d, ga_off)

    def body(ga_ref, gb_ref, ya_ref, yb_ref, o_ref):
        def piece(rows):
            o_ref[rows, :] = (_sigmoid(ga_ref[rows, :].astype(F32)) * ya_ref[rows, :].astype(F32)
                              + _sigmoid(gb_ref[rows, :].astype(F32)) * yb_ref[rows, :].astype(F32)
                              ).astype(o_ref.dtype)

        _chunked(tr, piece)

    blk = pl.BlockSpec((tr, tc), lambda i, j: (i, j))
    return _call(body, name=name, grid=(t // tr, d // tc),
                 in_specs=[pl.BlockSpec((tr, tc), lambda i, j: (i, ga_blk + j)),
                           pl.BlockSpec((tr, tc), lambda i, j: (i, gb_blk + j)), blk, blk],
                 out_specs=blk, out_shape=_sds((t, d), BF16), sem=("parallel", "parallel"))(z, z, ya, yb)


def _gate_merge_bwd(z, ya, yb, dm, ga_off, *, name):
    t, d = ya.shape
    tr = _pick(t, (1024, 512, 256, 128, 64, 32, 16, 8))
    tc, ga_blk, gb_blk = _gate_cols(d, ga_off)

    def body(ga_ref, gb_ref, ya_ref, yb_ref, dm_ref, dya_ref, dyb_ref, dga_ref, dgb_ref):
        def piece(rows):
            dm_v = dm_ref[rows, :].astype(F32)
            sa = _sigmoid(ga_ref[rows, :].astype(F32))
            sb = _sigmoid(gb_ref[rows, :].astype(F32))
            dya_ref[rows, :] = (dm_v * sa).astype(dya_ref.dtype)
            dyb_ref[rows, :] = (dm_v * sb).astype(dyb_ref.dtype)
            dga_ref[rows, :] = (dm_v * ya_ref[rows, :].astype(F32) * sa * (1.0 - sa)).astype(dga_ref.dtype)
            dgb_ref[rows, :] = (dm_v * yb_ref[rows, :].astype(F32) * sb * (1.0 - sb)).astype(dgb_ref.dtype)

        _chunked(tr, piece)

    blk = pl.BlockSpec((tr, tc), lambda i, j: (i, j))
    return _call(body, name=name, grid=(t // tr, d // tc),
                 in_specs=[pl.BlockSpec((tr, tc), lambda i, j: (i, ga_blk + j)),
                           pl.BlockSpec((tr, tc), lambda i, j: (i, gb_blk + j)), blk, blk, blk],
                 out_specs=[blk, blk, blk, blk], out_shape=[_sds((t, d), BF16)] * 4,
                 sem=("parallel", "parallel"))(z, z, ya, yb, dm)


CONV_ROWS = 64
CONV_PAD = 32


def _row_shifts(win):
    total = win.shape[0]
    return [win] + [pltpu.roll(win, total - b, axis=0) for b in range(1, SUBLANES)]


def _shifted_rows(copies, shift):
    start = SUBLANES * (shift // SUBLANES)
    return copies[shift % SUBLANES][start:start + CONV_ROWS]


def _fill_glu(z_ref, ext_ref, s, ch):
    ext_ref[pl.ds(0, CONV_PAD), :] = jnp.zeros((CONV_PAD, ch), F32)

    chunk = min(ROW_CHUNK, s)

    def piece(i, carry):
        start = pl.multiple_of(i * chunk, chunk)
        zz = z_ref[pl.ds(start, chunk), :].astype(F32)
        ext_ref[pl.ds(pl.multiple_of(CONV_PAD + start, CONV_PAD), chunk), :] = zz[:, :ch] * _sigmoid(zz[:, ch:])
        return carry

    lax.fori_loop(0, s // chunk, piece, 0)


def _conv_piece(ext_ref, w_ref, cb_ref, base, kw):
    copies = _row_shifts(ext_ref[pl.ds(base, CONV_ROWS + CONV_PAD), :])
    acc = cb_ref[...] + w_ref[pl.ds(0, 1), :] * _shifted_rows(copies, CONV_PAD - (kw - 1))
    for k in range(1, kw):
        acc = acc + w_ref[pl.ds(k, 1), :] * _shifted_rows(copies, CONV_PAD - (kw - 1) + k)
    return acc, copies


def _conv_branch_fwd(z, w, cb, lg, lb, bsz, ch, *, name):
    t = z.shape[0]
    s = t // bsz
    kw = w.shape[0]

    def body(z_ref, w_ref, cb_ref, lg_ref, lb_ref, o_ref, ext_ref):
        _fill_glu(z_ref, ext_ref, s, ch)

        def step(i, carry):
            base = pl.multiple_of(i * CONV_ROWS, CONV_ROWS)
            a1, _ = _conv_piece(ext_ref, w_ref, cb_ref, base, kw)
            n, _ = _ln_stats(a1)
            a2 = n * lg_ref[...] + lb_ref[...]
            o_ref[pl.ds(base, CONV_ROWS), :] = (a2 * _sigmoid(a2)).astype(o_ref.dtype)
            return carry

        lax.fori_loop(0, s // CONV_ROWS, step, 0)

    vec = pl.BlockSpec((1, ch), lambda b: (0, 0))
    return _call(body, name=name, grid=(bsz,),
                 in_specs=[pl.BlockSpec((s, 2 * ch), lambda b: (b, 0)), pl.BlockSpec((kw, ch), lambda b: (0, 0)),
                           vec, vec, vec],
                 out_specs=pl.BlockSpec((s, ch), lambda b: (b, 0)), out_shape=_sds((t, ch), BF16),
                 scratch=[pltpu.VMEM((CONV_PAD + s, ch), F32)], sem=("parallel",))(z, w, cb, lg, lb)


def _conv_branch_bwd(z, da3, w, cb, lg, lb, bsz, ch, *, name):
    t = z.shape[0]
    s = t // bsz
    kw = w.shape[0]
    n_rows = CONV_ROWS + CONV_PAD

    def body(z_ref, d_ref, w_ref, cb_ref, lg_ref, lb_ref, dz_ref, dw_ref, dcb_ref, dlg_ref, dlb_ref,
             ext_ref, da1_ref):
        @pl.when(pl.program_id(0) == 0)
        def _():
            for ref in (dw_ref, dcb_ref, dlg_ref, dlb_ref):
                ref[...] = jnp.zeros_like(ref)

        _fill_glu(z_ref, ext_ref, s, ch)
        da1_ref[pl.ds(s, CONV_PAD), :] = jnp.zeros((CONV_PAD, ch), F32)

        def grad_a1(i, carry):
            dlg, dlb = carry
            base = pl.multiple_of(i * CONV_ROWS, CONV_ROWS)
            a1, _ = _conv_piece(ext_ref, w_ref, cb_ref, base, kw)
            n, rstd = _ln_stats(a1)
            a2 = n * lg_ref[...] + lb_ref[...]
            sg = _sigmoid(a2)
            da2 = d_ref[pl.ds(base, CONV_ROWS), :] * (sg * (1.0 + a2 * (1.0 - sg)))
            da1_ref[pl.ds(base, CONV_ROWS), :] = _ln_bwd(da2 * lg_ref[...], n, rstd)
            return (dlg + jnp.sum(da2 * n, axis=0, keepdims=True), dlb + jnp.sum(da2, axis=0, keepdims=True))

        zero = jnp.zeros((1, ch), F32)
        dlg, dlb = lax.fori_loop(0, s // CONV_ROWS, grad_a1, (zero, zero))
        dlg_ref[...] += dlg
        dlb_ref[...] += dlb

        def grad_z(i, dcb):
            base = pl.multiple_of(i * CONV_ROWS, CONV_ROWS)
            ahead = _row_shifts(da1_ref[pl.ds(base, n_rows), :])
            dyc = ahead[0][:CONV_ROWS]
            da0 = w_ref[pl.ds(kw - 1, 1), :] * dyc
            for k in range(kw - 1):
                da0 = da0 + w_ref[pl.ds(k, 1), :] * _shifted_rows(ahead, kw - 1 - k)
            behind = _row_shifts(ext_ref[pl.ds(base, n_rows), :])
            for k in range(kw):
                dw_ref[pl.ds(k, 1), :] += jnp.sum(dyc * _shifted_rows(behind, CONV_PAD - (kw - 1) + k),
                                                  axis=0, keepdims=True)
            zz = z_ref[pl.ds(base, CONV_ROWS), :].astype(F32)
            sg = _sigmoid(zz[:, ch:])
            dz_ref[pl.ds(base, CONV_ROWS), :ch] = (da0 * sg).astype(dz_ref.dtype)
            dz_ref[pl.ds(base, CONV_ROWS), ch:] = (da0 * zz[:, :ch] * sg * (1.0 - sg)).astype(dz_ref.dtype)
            return dcb + jnp.sum(dyc, axis=0, keepdims=True)

        dcb_ref[...] += lax.fori_loop(0, s // CONV_ROWS, grad_z, zero)

    vec = pl.BlockSpec((1, ch), lambda b: (0, 0))
    taps = pl.BlockSpec((kw, ch), lambda b: (0, 0))
    return _call(body, name=name, grid=(bsz,),
                 in_specs=[pl.BlockSpec((s, 2 * ch), lambda b: (b, 0)), pl.BlockSpec((s, ch), lambda b: (b, 0)),
                           taps, vec, vec, vec],
                 out_specs=[pl.BlockSpec((s, 2 * ch), lambda b: (b, 0)), taps, vec, vec, vec],
                 out_shape=[_sds((t, 2 * ch), BF16), _sds((kw, ch), F32)] + [_sds((1, ch), F32)] * 3,
                 scratch=[pltpu.VMEM((CONV_PAD + s, ch), F32), pltpu.VMEM((s + CONV_PAD, ch), F32)],
                 sem=("arbitrary",))(z, da3, w, cb, lg, lb)


FFN_ROWS = 64


def _gelu_parts(v):
    cdf = 0.5 * (1.0 + lax.erf(v * (2.0 ** -0.5)))
    return cdf, v * cdf


def _ffn_conv_piece(ext_ref, wb_ref, base):
    win = ext_ref[pl.ds(base, FFN_ROWS + FFN_PAD), :]
    acc = wb_ref[pl.ds(3, 1), :] + wb_ref[pl.ds(2, 1), :] * win[FFN_PAD:]
    acc = acc + wb_ref[pl.ds(1, 1), :] * pltpu.roll(win, 1, axis=0)[FFN_PAD:]
    acc = acc + wb_ref[pl.ds(0, 1), :] * pltpu.roll(win, 2, axis=0)[FFN_PAD:]
    return acc


def _ffn_stage(hg_ref, hl_ref, wg_ref, wl_ref, bg_ref, bl_ref, ext_ref, wb_ref, s, tcf):
    ext_ref[pl.ds(0, FFN_PAD), :] = jnp.zeros((FFN_PAD, 2 * tcf), F32)
    ext_ref[pl.ds(FFN_PAD, s), :tcf] = hg_ref[...].astype(F32)
    ext_ref[pl.ds(FFN_PAD, s), tcf:] = hl_ref[...].astype(F32)
    wb_ref[pl.ds(0, 3), :tcf] = wg_ref[...]
    wb_ref[pl.ds(0, 3), tcf:] = wl_ref[...]
    wb_ref[pl.ds(3, 1), :tcf] = bg_ref[...]
    wb_ref[pl.ds(3, 1), tcf:] = bl_ref[...]


def _ffn_specs(s, tcf, n_f, batch_first):
    def spec(rows, shift):
        if batch_first:
            return pl.BlockSpec((rows, tcf), lambda bb, j: (bb if rows == s else 0, shift + j))
        return pl.BlockSpec((rows, tcf), lambda j, bb: (bb if rows == s else 0, shift + j))

    return [spec(s, 0), spec(s, n_f), spec(3, 0), spec(3, n_f), spec(1, 0), spec(1, n_f)]


def _ffn_act_fwd(hp, w, b, bsz, tcf, *, name):
    t, two_f = hp.shape
    s = t // bsz
    n_f = two_f // (2 * tcf)

    def body(hg_ref, hl_ref, wg_ref, wl_ref, bg_ref, bl_ref, f_ref, ext_ref, wb_ref):
        _ffn_stage(hg_ref, hl_ref, wg_ref, wl_ref, bg_ref, bl_ref, ext_ref, wb_ref, s, tcf)

        def step(i, carry):
            base = pl.multiple_of(i * FFN_ROWS, FFN_ROWS)
            hh = _ffn_conv_piece(ext_ref, wb_ref, base)
            _, gelu = _gelu_parts(hh[:, :tcf])
            f_ref[pl.ds(base, FFN_ROWS), :] = (gelu * hh[:, tcf:]).astype(f_ref.dtype)
            return carry

        lax.fori_loop(0, s // FFN_ROWS, step, 0)

    return _call(body, name=name, grid=(bsz, n_f), in_specs=_ffn_specs(s, tcf, n_f, True),
                 out_specs=pl.BlockSpec((s, tcf), lambda bb, j: (bb, j)),
                 out_shape=_sds((t, two_f // 2), BF16),
                 scratch=[pltpu.VMEM((FFN_PAD + s, 2 * tcf), F32), pltpu.VMEM((SUBLANES, 2 * tcf), F32)],
                 sem=("parallel", "parallel"))(hp, hp, w, w, b, b)


def _ffn_act_bwd(hp, df, w, b, bsz, tcf, *, name):
    t, two_f = hp.shape
    s = t // bsz
    f_dim = two_f // 2
    n_f = f_dim // tcf
    gw = 2 * tcf
    n_rows = FFN_ROWS + FFN_PAD

    def body(hg_ref, hl_ref, wg_ref, wl_ref, bg_ref, bl_ref, df_ref,
             dhg_ref, dhl_ref, dwg_ref, dwl_ref, dbg_ref, dbl_ref, ext_ref, wb_ref, dh_ref):
        @pl.when(pl.program_id(1) == 0)
        def _():
            for ref in (dwg_ref, dwl_ref, dbg_ref, dbl_ref):
                ref[...] = jnp.zeros_like(ref)

        _ffn_stage(hg_ref, hl_ref, wg_ref, wl_ref, bg_ref, bl_ref, ext_ref, wb_ref, s, tcf)
        dh_ref[pl.ds(s, FFN_PAD), :] = jnp.zeros((FFN_PAD, gw), F32)

        def grad_h(i, carry):
            base = pl.multiple_of(i * FFN_ROWS, FFN_ROWS)
            hh = _ffn_conv_piece(ext_ref, wb_ref, base)
            hg = hh[:, :tcf]
            d = df_ref[pl.ds(base, FFN_ROWS), :].astype(F32)
            cdf, gelu = _gelu_parts(hg)
            pdf = jnp.exp(-0.5 * hg * hg) * (1.0 / math.sqrt(2.0 * math.pi))
            dh_ref[pl.ds(base, FFN_ROWS), :tcf] = d * hh[:, tcf:] * (cdf + hg * pdf)
            dh_ref[pl.ds(base, FFN_ROWS), tcf:] = d * gelu
            return carry

        lax.fori_loop(0, s // FFN_ROWS, grad_h, 0)

        def grad_x(i, carry):
            dw0, dw1, dw2, dbs = carry
            base = pl.multiple_of(i * FFN_ROWS, FFN_ROWS)
            nxt = dh_ref[pl.ds(base, n_rows), :]
            dyc = nxt[:FFN_ROWS]
            dx = wb_ref[pl.ds(2, 1), :] * dyc
            dx = dx + wb_ref[pl.ds(1, 1), :] * pltpu.roll(nxt, n_rows - 1, axis=0)[:FFN_ROWS]
            dx = dx + wb_ref[pl.ds(0, 1), :] * pltpu.roll(nxt, n_rows - 2, axis=0)[:FFN_ROWS]
            dhg_ref[pl.ds(base, FFN_ROWS), :] = dx[:, :tcf].astype(dhg_ref.dtype)
            dhl_ref[pl.ds(base, FFN_ROWS), :] = dx[:, tcf:].astype(dhl_ref.dtype)
            win = ext_ref[pl.ds(base, n_rows), :]
            dw2 = dw2 + jnp.sum(dyc * win[FFN_PAD:], axis=0, keepdims=True)
            dw1 = dw1 + jnp.sum(dyc * pltpu.roll(win, 1, axis=0)[FFN_PAD:], axis=0, keepdims=True)
            dw0 = dw0 + jnp.sum(dyc * pltpu.roll(win, 2, axis=0)[FFN_PAD:], axis=0, keepdims=True)
            return dw0, dw1, dw2, dbs + jnp.sum(dyc, axis=0, keepdims=True)

        zero = jnp.zeros((1, gw), F32)
        sums = lax.fori_loop(0, s // FFN_ROWS, grad_x, (zero, zero, zero, zero))
        for k in range(3):
            dwg_ref[pl.ds(k, 1), :] += sums[k][:, :tcf]
            dwl_ref[pl.ds(k, 1), :] += sums[k][:, tcf:]
        dbg_ref[...] += sums[3][:, :tcf]
        dbl_ref[...] += sums[3][:, tcf:]

    half = pl.BlockSpec((s, tcf), lambda j, bb: (bb, j))
    taps = pl.BlockSpec((3, tcf), lambda j, bb: (0, j))
    bias = pl.BlockSpec((1, tcf), lambda j, bb: (0, j))
    return _call(body, name=name, grid=(n_f, bsz), in_specs=_ffn_specs(s, tcf, n_f, False) + [half],
                 out_specs=[half, half, taps, taps, bias, bias],
                 out_shape=[_sds((t, f_dim), BF16)] * 2 + [_sds((3, f_dim), F32)] * 2 + [_sds((1, f_dim), F32)] * 2,
                 scratch=[pltpu.VMEM((FFN_PAD + s, gw), F32), pltpu.VMEM((SUBLANES, gw), F32),
                          pltpu.VMEM((s + FFN_PAD, gw), F32)],
                 sem=("parallel", "arbitrary"))(hp, hp, w, w, b, b, df)


def _split3(v):
    hi = v.astype(BF16)
    r = v - hi.astype(F32)
    mid = r.astype(BF16)
    lo = (r - mid.astype(F32)).astype(BF16)
    return hi, mid, lo


def _tri_dot(tri, v):
    out = None
    for part in _split3(v):
        term = jnp.dot(tri, part, preferred_element_type=F32)
        out = term if out is None else out + term
    return out


def _fgate_fwd(zf, bsz, heads, *, name):
    t, lanes = zf.shape
    s, blk, nb = _seq_tiles(t, bsz, (ATTN_BLOCK, 128))

    def body(z_ref, cumt_ref, cumb_ref, carry_ref):
        @pl.when(pl.program_id(1) == 0)
        def _():
            carry_ref[...] = jnp.zeros_like(carry_ref)

        z = z_ref[...]
        lf = jnp.minimum(z, 0.0) - jnp.log1p(jnp.exp(-jnp.abs(z)))
        r = lax.broadcasted_iota(jnp.int32, (blk, blk), 0)
        c = lax.broadcasted_iota(jnp.int32, (blk, blk), 1)
        tri = (r >= c).astype(BF16)
        cum = _tri_dot(tri, lf) + carry_ref[...]
        carry_ref[...] = cum[blk - 1:blk, :]
        cumt_ref[0] = jnp.transpose(cum)[:heads, :]
        for h in range(heads):
            cumb_ref[0, h] = jnp.broadcast_to(cum[:, h:h + 1], (blk, lanes))

    return _call(body, name=name, grid=(bsz, nb),
                 in_specs=[pl.BlockSpec((blk, lanes), lambda b, i: (b * nb + i, 0))],
                 out_specs=[pl.BlockSpec((1, heads, blk), lambda b, i: (b, 0, i)),
                            pl.BlockSpec((1, heads, blk, lanes), lambda b, i: (b, 0, i, 0))],
                 out_shape=[_sds((bsz, heads, s), F32), _sds((bsz, heads, s, lanes), F32)],
                 scratch=[pltpu.VMEM((1, lanes), F32)], sem=("parallel", "arbitrary"))(zf)


def _fgate_bwd(dcum, zf, bsz, *, name):
    t, lanes = zf.shape
    pairs = dcum.shape[1]
    s, blk, nb = _seq_tiles(t, bsz, (ATTN_BLOCK, 128))

    def body(d_ref, z_ref, o_ref, carry_ref):
        @pl.when(pl.program_id(1) == 0)
        def _():
            carry_ref[...] = jnp.zeros_like(carry_ref)

        dcol = d_ref[0, 0]
        for p in range(1, pairs):
            dcol = dcol + d_ref[0, p]
        r = lax.broadcasted_iota(jnp.int32, (blk, blk), 0)
        c = lax.broadcasted_iota(jnp.int32, (blk, blk), 1)
        tri = (c >= r).astype(BF16)
        suf = _tri_dot(tri, dcol) + carry_ref[...]
        carry_ref[...] = suf[0:1, :]
        o_ref[...] = suf * _sigmoid(-z_ref[...])

    return _call(body, name=name, grid=(bsz, nb),
                 in_specs=[pl.BlockSpec((1, pairs, blk, lanes), lambda b, i: (b, 0, nb - 1 - i, 0)),
                           pl.BlockSpec((blk, lanes), lambda b, i: (b * nb + nb - 1 - i, 0))],
                 out_specs=pl.BlockSpec((blk, lanes), lambda b, i: (b * nb + nb - 1 - i, 0)),
                 out_shape=_sds((t, lanes), F32), scratch=[pltpu.VMEM((1, lanes), F32)],
                 sem=("parallel", "arbitrary"))(dcum, zf)


def _to_features_major(z, col_off, width, n, *, name):
    t = z.shape[0]
    tr = _pick(t, (512, 256, 128))
    first = col_off // width

    def body(*refs):
        o_ref = refs[n]
        for g in range(n):
            o_ref[pl.ds(g * width, width), :] = jnp.transpose(refs[g][...].astype(F32)).astype(o_ref.dtype)

    return _call(body, name=name, grid=(t // tr,),
                 in_specs=[pl.BlockSpec((tr, width), lambda i, g=g: (i, first + g)) for g in range(n)],
                 out_specs=pl.BlockSpec((n * width, tr), lambda i: (0, i)),
                 out_shape=_sds((n * width, t), BF16), sem=("parallel",))(*([z] * n))


def _to_rows_major(xt, *, name):
    w, t = xt.shape
    tr = _pick(t, (512, 256, 128))

    def body(x_ref, o_ref):
        o_ref[...] = jnp.transpose(x_ref[...]).astype(o_ref.dtype)

    return _call(body, name=name, grid=(t // tr,),
                 in_specs=[pl.BlockSpec((w, tr), lambda i: (0, i))],
                 out_specs=pl.BlockSpec((tr, w), lambda i: (i, 0)),
                 out_shape=_sds((t, w), BF16), sem=("parallel",))(xt)


def _loop_by_twos(lo, hi, body, carry):
    count = hi - lo

    def group(n, first, cr):
        for u in range(n):
            cr = body(first + u, cr)
        return cr

    trips = count // ATTN_UNROLL
    carry = lax.fori_loop(0, trips, lambda t, cr: group(ATTN_UNROLL, lo + ATTN_UNROLL * t, cr), carry)
    rest = count - ATTN_UNROLL * trips
    first = lo + ATTN_UNROLL * trips
    for n in range(ATTN_UNROLL - 1, 0, -1):
        carry = lax.cond(rest == n, lambda cr, n=n: group(n, first, cr), lambda cr: cr, carry)
    return carry


def _head_masks(shape, axis):
    feat = lax.broadcasted_iota(jnp.int32, shape, axis)
    return feat < HEAD_DIM, feat >= HEAD_DIM


def _attn_fwd(z, qkvt, cumt, cumb, bsz, heads, q_off, *, name):
    t = z.shape[0]
    width = heads * HEAD_DIM
    pairs = heads // 2
    s = t // bsz
    blk = ATTN_BLOCK
    nq = s // blk
    k_col = (q_off + width) // LANES
    v_row = 2 * width // LANES
    reps = blk // LANES

    def body(k_ref, qt_ref, vt_ref, cqt_ref, ckb_ref, ot_ref, lse_ref):
        p_id = pl.program_id(1)
        i = pl.program_id(2)
        qt = qt_ref[...]
        masks = _head_masks((LANES, blk), 0)
        qtm = [jnp.where(mk, qt, jnp.zeros_like(qt)) for mk in masks]
        cq = [cqt_ref[0, pl.ds(2 * p_id + hh, 1), :] for hh in range(2)]
        kidx = lax.broadcasted_iota(jnp.int32, (blk, blk), 0)
        qidx = lax.broadcasted_iota(jnp.int32, (blk, blk), 1)

        def block(j, carry, masked):
            off = pl.multiple_of(j * blk, blk)
            kp = k_ref[pl.ds(off, blk), :].astype(BF16)
            vtp = vt_ref[:, pl.ds(off, blk)]
            out = []
            for hh in range(2):
                m, l, acc = carry[hh]
                sc = jnp.dot(kp, qtm[hh], preferred_element_type=F32) * ATTN_SCALE
                ck = ckb_ref[0, hh, pl.ds(off, blk), :]
                sc = (sc + cq[hh]) - jnp.concatenate([ck] * reps, axis=1)
                if masked:
                    sc = jnp.where(qidx >= kidx, sc, NEG)
                m_new = jnp.maximum(m, jnp.max(sc, axis=0, keepdims=True))
                pr = jnp.exp(sc - m_new)
                a = jnp.exp(m - m_new)
                l = a * l + jnp.sum(pr, axis=0, keepdims=True)
                p_hi = pr.astype(BF16)
                p_lo = (pr - p_hi.astype(F32)).astype(BF16)
                pv = (jnp.dot(vtp, p_hi, preferred_element_type=F32)
                      + jnp.dot(vtp, p_lo, preferred_element_type=F32))
                acc = a * acc + pv[hh * HEAD_DIM:(hh + 1) * HEAD_DIM]
                out.append((m_new, l, acc))
            return tuple(out)

        init = tuple((jnp.full((1, blk), NEG, F32), jnp.zeros((1, blk), F32), jnp.zeros((HEAD_DIM, blk), F32))
                     for _ in range(2))
        carry = _loop_by_twos(0, i, lambda j, cr: block(j, cr, False), init)
        carry = block(i, carry, True)
        lse_ref[...] = jnp.zeros_like(lse_ref)
        for hh in range(2):
            m, l, acc = carry[hh]
            ot_ref[pl.ds(hh * HEAD_DIM, HEAD_DIM), :] = acc / l
            lse_ref[0, 0, pl.ds(hh, 1), :] = m + jnp.log(l)

    return _call(body, name=name, grid=(bsz, pairs, nq),
                 in_specs=[pl.BlockSpec((s, LANES), lambda b, p, i: (b, k_col + p)),
                           pl.BlockSpec((LANES, blk), lambda b, p, i: (p, b * nq + i)),
                           pl.BlockSpec((LANES, s), lambda b, p, i: (v_row + p, b)),
                           pl.BlockSpec((1, heads, blk), lambda b, p, i: (b, 0, i)),
                           pl.BlockSpec((1, 2, s, LANES), lambda b, p, i: (b, p, 0, 0))],
                 out_specs=[pl.BlockSpec((LANES, blk), lambda b, p, i: (p, b * nq + i)),
                            pl.BlockSpec((1, 1, SUBLANES, blk), lambda b, p, i: (b, p, 0, i))],
                 out_shape=[_sds((width, t), F32), _sds((bsz, pairs, SUBLANES, s), F32)],
                 sem=("parallel", "parallel", "parallel"))(z, qkvt, qkvt, cumt, cumb)


def _attn_bwd(z, qkvt, cumt, cumb, ot, do, dot, lse, bsz, heads, q_off, *, name):
    t = z.shape[0]
    width = heads * HEAD_DIM
    pairs = heads // 2
    s = t // bsz
    blk = ATTN_BLOCK
    nkv = s // blk
    q_col = q_off // LANES
    k_col = (q_off + width) // LANES
    v_col = (q_off + 2 * width) // LANES
    k_row = width // LANES
    reps = blk // LANES

    def body(k_ref, v_ref, kt_ref, q_ref, qt_ref, do_ref, dot_ref, ot_ref, lse_ref, ckb_ref, cqt_ref,
             dk_ref, dv_ref, dqt_ref, dcum_ref, dqt_acc, ds_acc):
        p_id = pl.program_id(1)
        j = pl.program_id(2)

        @pl.when(j == 0)
        def _():
            dqt_acc[...] = jnp.zeros_like(dqt_acc)

        kp = k_ref[...].astype(BF16)
        vp = v_ref[...].astype(BF16)
        kt = kt_ref[...]
        feat_masks = _head_masks((LANES, blk), 0)
        lane_masks = _head_masks((blk, LANES), 1)
        ktm = [jnp.where(mk, kt, jnp.zeros_like(kt)) for mk in feat_masks]
        ck = [jnp.concatenate([ckb_ref[0, hh]] * reps, axis=1) for hh in range(2)]
        kidx = lax.broadcasted_iota(jnp.int32, (blk, blk), 0)
        qidx = lax.broadcasted_iota(jnp.int32, (blk, blk), 1)
        ds_acc[...] = jnp.zeros_like(ds_acc)

        def block(i, carry, masked):
            dk, dv = carry
            off = pl.multiple_of(i * blk, blk)
            qt = qt_ref[:, pl.ds(off, blk)]
            dt = dot_ref[:, pl.ds(off, blk)]
            o_t = ot_ref[:, pl.ds(off, blk)]
            q_rows = q_ref[pl.ds(off, blk), :].astype(BF16)
            do_rows = do_ref[pl.ds(off, blk), :]
            for hh in range(2):
                qtm = jnp.where(feat_masks[hh], qt, jnp.zeros_like(qt))
                dtm = jnp.where(feat_masks[hh], dt, jnp.zeros_like(dt))
                sc = jnp.dot(kp, qtm, preferred_element_type=F32) * ATTN_SCALE
                sc = (sc + cqt_ref[0, pl.ds(2 * p_id + hh, 1), pl.ds(off, blk)]) - ck[hh]
                pr = jnp.exp(sc - lse_ref[0, 0, pl.ds(hh, 1), pl.ds(off, blk)])
                if masked:
                    pr = jnp.where(qidx >= kidx, pr, 0.0)
                dp = jnp.dot(vp, dtm, preferred_element_type=F32)
                delta = jnp.sum(dtm.astype(F32) * o_t, axis=0, keepdims=True)
                ds = pr * (dp - delta)
                ds_acc[hh] += ds
                dsb = ds.astype(BF16)
                qm = jnp.where(lane_masks[hh], q_rows, jnp.zeros_like(q_rows))
                dom = jnp.where(lane_masks[hh], do_rows, jnp.zeros_like(do_rows))
                dv = dv + jnp.dot(pr.astype(BF16), dom, preferred_element_type=F32)
                dk = dk + jnp.dot(dsb, qm, preferred_element_type=F32) * ATTN_SCALE
                dqt_acc[:, pl.ds(off, blk)] += jnp.dot(ktm[hh], dsb, preferred_element_type=F32) * ATTN_SCALE
            return dk, dv

        zero = jnp.zeros((blk, LANES), F32)
        carry = block(j, (zero, zero), True)
        dk, dv = _loop_by_twos(j + 1, nkv, lambda i, cr: block(i, cr, False), carry)
        dk_ref[...] = dk.astype(dk_ref.dtype)
        dv_ref[...] = dv.astype(dv_ref.dtype)
        lane = lax.broadcasted_iota(jnp.int32, (blk, LANES), 1)
        dcum = jnp.zeros((blk, LANES), F32)
        for hh in range(2):
            col = jnp.sum(ds_acc[hh], axis=1, keepdims=True)
            dcum = jnp.where(lane == 2 * p_id + hh, -col, dcum)
        dcum_ref[0, 0] = dcum

        @pl.when(j == nkv - 1)
        def _():
            dqt_ref[...] = dqt_acc[...]

    key_rows = lambda col: pl.BlockSpec((blk, LANES), lambda b, p, j: (b * nkv + j, col + p))
    seq_t = lambda row: pl.BlockSpec((LANES, s), lambda b, p, j: (row + p, b))
    return _call(body, name=name, grid=(bsz, pairs, nkv),
                 in_specs=[key_rows(k_col), key_rows(v_col),
                           pl.BlockSpec((LANES, blk), lambda b, p, j: (k_row + p, b * nkv + j)),
                           pl.BlockSpec((s, LANES), lambda b, p, j: (b, q_col + p)), seq_t(0),
                           pl.BlockSpec((s, LANES), lambda b, p, j: (b, p)), seq_t(0), seq_t(0),
                           pl.BlockSpec((1, 1, SUBLANES, s), lambda b, p, j: (b, p, 0, 0)),
                           pl.BlockSpec((1, 2, blk, LANES), lambda b, p, j: (b, p, j, 0)),
                           pl.BlockSpec((1, heads, s), lambda b, p, j: (b, 0, 0))],
                 out_specs=[key_rows(0), key_rows(0), seq_t(0),
                            pl.BlockSpec((1, 1, blk, LANES), lambda b, p, j: (b, p, j, 0))],
                 out_shape=[_sds((t, width), BF16), _sds((t, width), BF16), _sds((width, t), F32),
                            _sds((bsz, pairs, s, LANES), F32)],
                 scratch=[pltpu.VMEM((LANES, s), F32), pltpu.VMEM((2, blk, blk), F32)],
                 sem=("parallel", "parallel", "arbitrary"))(z, z, qkvt, z, qkvt, do, dot, ot, lse, cumb, cumt)


def _adamw(w, g, m, v, *, name):
    bc1 = 1.0 - ADAM_B1 ** ADAM_STEP
    bc2 = 1.0 - ADAM_B2 ** ADAM_STEP

    def body(w_ref, g_ref, m_ref, v_ref, d_ref, nm_ref, nv_ref):
        g_v = g_ref[...]
        nm = ADAM_B1 * m_ref[...] + (1.0 - ADAM_B1) * g_v
        nv = ADAM_B2 * v_ref[...] + (1.0 - ADAM_B2) * (g_v * g_v)
        nm_ref[...] = nm
        nv_ref[...] = nv
        d_ref[...] = -ADAM_LR * ((nm / bc1) / (jnp.sqrt(nv / bc2) + ADAM_EPS) + ADAM_WD * w_ref[...])

    if w.ndim == 2:
        grid = (1,)
        blk = pl.BlockSpec(w.shape, lambda i: (0, 0))
    else:
        layers, rows, cols = w.shape
        tr = rows if rows <= 256 else _pick(rows, (256, 128, 64, 32, 16, 8))
        grid = (layers, rows // tr)
        blk = pl.BlockSpec((1, tr, cols), lambda layer, i: (layer, i, 0))
    return tuple(_call(body, name=name, grid=grid, in_specs=[blk] * 4, out_specs=[blk] * 3,
                       out_shape=[_sds(w.shape, F32)] * 3, sem=("parallel",) * len(grid))(w, g, m, v))


_ANY = pl.BlockSpec(memory_space=pl.ANY)


def _comm_call(body, *, name, n_in, out_shape, n_sems):
    scratch = [pltpu.SemaphoreType.DMA((n_sems,)), pltpu.SemaphoreType.DMA((n_sems,)),
               pltpu.SemaphoreType.DMA((len(out_shape),))]
    return pl.pallas_call(body, name=name, in_specs=[_ANY] * n_in, out_specs=[_ANY] * len(out_shape),
                          out_shape=out_shape, scratch_shapes=scratch)


def _place():
    x, y, c = lax.axis_index("x"), lax.axis_index("y"), lax.axis_index("c")
    return x, y, c, [(1 - x, y), (x, 1 - y), (1 - x, 1 - y)]


def _remote(src, dst, send_sems, recv_sems, sem, to):
    return pltpu.make_async_remote_copy(src_ref=src, dst_ref=dst, send_sem=send_sems.at[sem],
                                        recv_sem=recv_sems.at[sem], device_id=to, device_id_type=MESH)


def _all_gather8(v, *, name):
    def body(v_ref, out_ref, send_sems, recv_sems, local_sems):
        x, y, c, _ = _place()
        me = 4 * x + 2 * y + c
        mine = pltpu.make_async_copy(v_ref, out_ref.at[me], local_sems.at[0])
        mine.start()
        peers = []
        for k in range(1, N_DEVICES):
            px = 1 - x if k & 4 else x
            py = 1 - y if k & 2 else y
            pc = 1 - c if k & 1 else c
            peers.append((px, py, pc))
        sends = [_remote(v_ref, out_ref.at[me], send_sems, recv_sems, k, peer) for k, peer in enumerate(peers)]
        for cp in sends:
            cp.start()
        for k, (px, py, pc) in enumerate(peers):
            _remote(v_ref, out_ref.at[4 * px + 2 * py + pc], send_sems, recv_sems, k, (px, py, pc)).wait_recv()
        for cp in sends:
            cp.wait_send()
        mine.wait()

    out = _comm_call(body, name=name, n_in=1, out_shape=[_sds((N_DEVICES,) + v.shape, v.dtype)],
                     n_sems=N_DEVICES - 1)(v)
    return out[0]


def _window(ref, mode, layer, chip, rows, cols, half=None):
    first, count = (0, rows) if half is None else (half * (rows // 2), rows // 2)
    if mode == "slab":
        return ref.at[layer, chip] if half is None else ref.at[layer, chip, pl.ds(first, count), :]
    if mode == "cols":
        col_window = pl.ds(pl.multiple_of(chip * cols, LANES), cols)
        return ref.at[layer, :, col_window] if half is None else ref.at[layer, pl.ds(first, count), col_window]
    return ref.at[layer, pl.ds(pl.multiple_of(chip * rows + first, SUBLANES), count), :]


def _whole_shape(mode, shard_shape):
    layers, rows, cols = shard_shape
    if mode == "slab":
        return (layers, N_CHIPS, rows, cols)
    if mode == "cols":
        assert cols % LANES == 0
        return (layers, rows, N_CHIPS * cols)
    assert rows % 16 == 0
    return (layers, N_CHIPS * rows, cols)


def _gather_weights(shards, modes, *, name):
    n = len(shards)
    meta = [(mode,) + tuple(a.shape[1:]) for a, mode in zip(shards, modes)]
    for a in shards:
        assert a.shape[0] == 2 and a.shape[1] % 2 == 0
    per = 8

    def body(*refs):
        ins, outs = refs[:n], refs[n:2 * n]
        send_sems, recv_sems, _ = refs[2 * n:]
        x, y, c, _ = _place()
        me, x_nbr, y_nbr, diagonal = 2 * x + y, 2 * (1 - x) + y, 2 * x + 1 - y, 2 * (1 - x) + 1 - y
        to_x, to_y, sibling = (1 - x, y, c), (x, 1 - y, c), (x, y, 1 - c)
        sent = []

        def copy(src, dst, sem, to):
            cp = _remote(src, dst, send_sems, recv_sems, sem, to)
            cp.start()
            sent.append(cp)

        def arrived(win, sem):
            _remote(win, win, send_sems, recv_sems, sem, sibling).wait_recv()

        for i, (mode, rows, cols) in enumerate(meta):
            mine = _window(outs[i], mode, c, me, rows, cols)
            copy(ins[i].at[c], mine, per * i, to_x)
            copy(ins[i].at[c], mine, per * i + 1, to_y)
            copy(ins[i], _window(outs[i], mode, slice(None), me, rows, cols), per * i + 7, sibling)
        for i, (mode, rows, cols) in enumerate(meta):
            arrived(_window(outs[i], mode, c, x_nbr, rows, cols), per * i)
            half = _window(outs[i], mode, c, x_nbr, rows, cols, half=0)
            copy(half, half, per * i + 2, to_y)
            win = _window(outs[i], mode, c, x_nbr, rows, cols)
            copy(win, win, per * i + 4, sibling)
            arrived(_window(outs[i], mode, c, y_nbr, rows, cols), per * i + 1)
            half = _window(outs[i], mode, c, y_nbr, rows, cols, half=1)
            copy(half, half, per * i + 3, to_x)
            win = _window(outs[i], mode, c, y_nbr, rows, cols)
            copy(win, win, per * i + 5, sibling)
        for i, (mode, rows, cols) in enumerate(meta):
            arrived(_window(outs[i], mode, c, diagonal, rows, cols, half=0), per * i + 2)
            arrived(_window(outs[i], mode, c, diagonal, rows, cols, half=1), per * i + 3)
            win = _window(outs[i], mode, c, diagonal, rows, cols)
            copy(win, win, per * i + 6, sibling)
        for i, (mode, rows, cols) in enumerate(meta):
            arrived(_window(outs[i], mode, slice(None), me, rows, cols), per * i + 7)
            for k, chip in enumerate((x_nbr, y_nbr, diagonal)):
                arrived(_window(outs[i], mode, 1 - c, chip, rows, cols), per * i + 4 + k)
        for cp in sent:
            cp.wait_send()

    out_shape = [_sds(_whole_shape(mode, a.shape), a.dtype) for a, mode in zip(shards, modes)]
    return _comm_call(body, name=name, n_in=n, out_shape=out_shape, n_sems=per * n)(*shards)


def _rs_swap(grads, *, name):
    n = len(grads)

    def body(*refs):
        ins, outs = refs[:n], refs[n:2 * n]
        send_sems, recv_sems, _ = refs[2 * n:]
        x, y, c, _ = _place()
        copies = [_remote(ins[i].at[1 - c], outs[i], send_sems, recv_sems, i, (x, y, 1 - c)) for i in range(n)]
        for cp in copies:
            cp.start()
        for cp in copies:
            cp.wait()

    return _comm_call(body, name=name, n_in=n, out_shape=[_sds(g.shape[1:], g.dtype) for g in grads], n_sems=n)(*grads)


def _part(ref, mode, chip, rows, cols):
    if mode == "slab":
        return ref.at[chip]
    if mode == "cols":
        return ref.at[:, pl.ds(pl.multiple_of(chip * cols, LANES), cols)]
    return ref.at[pl.ds(pl.multiple_of(chip * rows, SUBLANES), rows), :]


def _rs_scatter(parts, modes, shard_shapes, *, name):
    n = len(parts)
    meta = [(mode,) + tuple(shp[1:]) for mode, shp in zip(modes, shard_shapes)]

    def body(*refs):
        ins, outs = refs[:n], refs[n:2 * n]
        send_sems, recv_sems, local_sems = refs[2 * n:]
        x, y, c, chips = _place()
        me = 2 * x + y
        local, sends = [], []
        for i, (mode, rows, cols) in enumerate(meta):
            cp = pltpu.make_async_copy(_part(ins[i], mode, me, rows, cols), outs[i].at[me], local_sems.at[i])
            cp.start()
            local.append(cp)
            for r, (cx, cy) in enumerate(chips):
                cp = _remote(_part(ins[i], mode, 2 * cx + cy, rows, cols), outs[i].at[me], send_sems, recv_sems,
                             3 * i + r, (cx, cy, c))
                cp.start()
                sends.append(cp)
        for i, (mode, rows, cols) in enumerate(meta):
            for r, (cx, cy) in enumerate(chips):
                k = 2 * cx + cy
                _remote(_part(ins[i], mode, k, rows, cols), outs[i].at[k], send_sems, recv_sems, 3 * i + r,
                        (cx, cy, c)).wait_recv()
        for cp in sends:
            cp.wait_send()
        for cp in local:
            cp.wait()

    out_shape = [_sds((N_CHIPS,) + tuple(shp[1:]), p.dtype) for p, shp in zip(parts, shard_shapes)]
    return _comm_call(body, name=name, n_in=n, out_shape=out_shape, n_sems=3 * n)(*parts)


def _rs_exchange(sums, *, name):
    n = len(sums)

    def body(*refs):
        ins, outs = refs[:n], refs[n:2 * n]
        send_sems, recv_sems, _ = refs[2 * n:]
        x, y, c, _ = _place()
        copies = [_remote(ins[i], outs[i], send_sems, recv_sems, i, (x, y, 1 - c)) for i in range(n)]
        for cp in copies:
            cp.start()
        for cp in copies:
            cp.wait()

    return _comm_call(body, name=name, n_in=n, out_shape=[_sds(s.shape, s.dtype) for s in sums], n_sems=n)(*sums)


def _row_tile(rows, cols, itemsize):
    target = max(SUBLANES, (2 << 20) // (cols * itemsize))
    cands = [c for c in (2048, 1024, 512, 256, 128, 64, 32, 16) if c <= target]
    tr = _pick(rows, cands)
    return tr


def _add_layer(g, other, core, *, name):
    _, rows, cols = g.shape
    tr = _row_tile(rows, cols, 4)

    def body(core_ref, g_ref, o_ref, out_ref):
        out_ref[...] = (g_ref[0] + o_ref[...]).astype(out_ref.dtype)

    grid_spec = pltpu.PrefetchScalarGridSpec(
        num_scalar_prefetch=1, grid=(rows // tr,),
        in_specs=[pl.BlockSpec((1, tr, cols), lambda i, core_ref: (core_ref[0], i, 0)),
                  pl.BlockSpec((tr, cols), lambda i, core_ref: (i, 0))],
        out_specs=pl.BlockSpec((tr, cols), lambda i, core_ref: (i, 0)))
    return pl.pallas_call(body, name=name, grid_spec=grid_spec, out_shape=_sds((rows, cols), BF16),
                          compiler_params=pltpu.CompilerParams(dimension_semantics=("parallel",),
                                                               vmem_limit_bytes=VMEM_LIMIT))(core, g, other)


def _sum_slots(parts, *, name):
    n, rows, cols = parts.shape
    tr = _row_tile(rows, cols, 4)

    def body(p_ref, o_ref):
        acc = p_ref[0].astype(F32) + p_ref[1].astype(F32)
        for k in range(2, n):
            acc = acc + p_ref[k].astype(F32)
        o_ref[...] = acc

    return _call(body, name=name, grid=(rows // tr,),
                 in_specs=[pl.BlockSpec((n, tr, cols), lambda i: (0, i, 0))],
                 out_specs=pl.BlockSpec((tr, cols), lambda i: (i, 0)),
                 out_shape=_sds((rows, cols), F32), sem=("parallel",))(parts)


def _reduce_scatter(grads, modes, shard_shapes):
    core = lax.axis_index("c").astype(jnp.int32).reshape(1)
    flat = [g.reshape(g.shape[0], -1, g.shape[-1]) for g in grads]
    from_sibling = _rs_swap(flat, name="rs_swap")
    parts = []
    for i, (g, o) in enumerate(zip(flat, from_sibling)):
        p = _add_layer(g, o, core, name=f"rs_add_{i}")
        parts.append(p.reshape(grads[i].shape[1:]))
    from_chips = _rs_scatter(parts, modes, shard_shapes, name="rs_scatter")
    sums = [_sum_slots(r, name=f"rs_sum_{i}") for i, r in enumerate(from_chips)]
    others = _rs_exchange(sums, name="rs_exchange")
    mine_first = lax.axis_index("c") == 0
    return [jnp.where(mine_first, jnp.stack([mine, other]), jnp.stack([other, mine]))
            for mine, other in zip(sums, others)]


def _layer_weights(full, rep, layer, dims):
    f_off, n_heads = dims["f_off"], dims["heads"]
    d_ff = full["w_ffn_up"].shape[-1] // 2
    b_in = rep["b_in"][layer]
    pad = LANES - n_heads
    return {
        "w_main": full["w_main"][layer],
        "b_main": jnp.concatenate([b_in[:f_off], b_in[f_off + n_heads:]])[None],
        "w_f": full["w_f"][layer],
        "b_f": jnp.pad(b_in[f_off:f_off + n_heads], (0, pad))[None],
        "conv_a_w": full["conv_a_w"][layer],
        "conv_a_b": rep["conv_a_b"][layer][None],
        "ln_conv_g": rep["ln_conv_g"][layer][None],
        "ln_conv_b": rep["ln_conv_b"][layer][None],
        "w_conv_proj": full["w_conv_proj"][layer],
        "w_attn_proj": full["w_attn_proj"][layer],
        "w_mix_out": full["w_mix_out"][layer],
        "b_mix_out": rep["b_mix_out"][layer][None],
        "ln1_g": rep["ln1_g"][layer][None],
        "ln1_b": rep["ln1_b"][layer][None],
        "w_ffn_up": full["w_ffn_up"][layer],
        "w_ffn_up_gate": full["w_ffn_up"][layer][:, :d_ff],
        "w_ffn_up_lin": full["w_ffn_up"][layer][:, d_ff:],
        "ffn_conv_w": full["ffn_conv_w"][layer],
        "ffn_conv_b": rep["ffn_conv_b"][layer][None],
        "w_ffn_down": full["w_ffn_down"][layer],
        "ln2_g": rep["ln2_g"][layer][None],
        "ln2_b": rep["ln2_b"][layer][None],
    }


def _layer_fwd(x, mod, p, dims, tag):
    bsz, d, ch, heads, alpha = dims["bsz"], dims["d"], dims["ch"], dims["heads"], dims["alpha"]
    mods = [mod[:, k * d:(k + 1) * d][:, None, :] for k in range(6)]
    shift1, scale1, gate1, shift2, scale2, gate2 = mods
    u = _ln_mod_fwd(x, scale1, shift1, bsz, name=f"ln_mod1_{tag}")
    zm = _matmul(u, p["w_main"], "nn", BF16, bias=p["b_main"], name=f"in_main_{tag}")
    zf = _matmul(u, p["w_f"], "nn", F32, bias=p["b_f"], name=f"in_forget_{tag}")
    a3 = _conv_branch_fwd(zm, p["conv_a_w"], p["conv_a_b"], p["ln_conv_g"], p["ln_conv_b"], bsz, ch,
                          name=f"conv_branch_{tag}")
    ya = _matmul(a3, p["w_conv_proj"], "nn", BF16, name=f"conv_proj_{tag}")
    cumt, cumb = _fgate_fwd(zf, bsz, heads, name=f"fgate_{tag}")
    qkvt = _to_features_major(zm, 2 * ch, heads * HEAD_DIM, 3, name=f"qkv_t_{tag}")
    ot, lse = _attn_fwd(zm, qkvt, cumt, cumb, bsz, heads, 2 * ch, name=f"attn_{tag}")
    yb = _matmul(ot, p["w_attn_proj"], "tn", BF16, name=f"attn_proj_{tag}")
    m = _gate_merge_fwd(zm, ya, yb, dims["ga_off"], name=f"merge_{tag}")
    mix = _matmul(m, p["w_mix_out"], "nn", F32, bias=p["b_mix_out"], name=f"mix_out_{tag}")
    x1 = _ln_res_fwd(x, mix, gate1, p["ln1_g"], p["ln1_b"], alpha, bsz, name=f"ln_res1_{tag}")
    u2 = _ln_mod_fwd(x1, scale2, shift2, bsz, name=f"ln_mod2_{tag}")
    hp = _matmul(u2, p["w_ffn_up"], "nn", BF16, name=f"ffn_up_{tag}")
    f = _ffn_act_fwd(hp, p["ffn_conv_w"], p["ffn_conv_b"], bsz, dims["tcf"], name=f"ffn_act_{tag}")
    ffn = _matmul(f, p["w_ffn_down"], "nn", F32, name=f"ffn_down_{tag}")
    x2 = _ln_res_fwd(x1, ffn, gate2, p["ln2_g"], p["ln2_b"], alpha, bsz, name=f"ln_res2_{tag}")
    saved = dict(x=x, mods=mods, u=u, zm=zm, zf=zf, a3=a3, ya=ya, yb=yb, cumt=cumt, cumb=cumb,
                 qkvt=qkvt, ot=ot, lse=lse, m=m, mix=mix, x1=x1, u2=u2, hp=hp, f=f, ffn=ffn)
    return x2, saved


def _layer_bwd(dx2, p, sv, dims, tag):
    bsz, ch, heads, alpha = dims["bsz"], dims["ch"], dims["heads"], dims["alpha"]
    f_off, tcf = dims["f_off"], dims["tcf"]
    shift1, scale1, gate1, shift2, scale2, gate2 = sv["mods"]
    g = {}
    dr2, dffn, dgate2, g["ln2_g"], g["ln2_b"], _ = _ln_res_bwd(
        dx2, sv["x1"], sv["ffn"], gate2, p["ln2_g"], alpha, bsz, name=f"ln_res2_bwd_{tag}")
    df = _matmul(dffn, p["w_ffn_down"], "nt", BF16, name=f"ffn_down_dx_{tag}")
    g["w_ffn_down"] = _matmul(sv["f"], dffn, "tn", F32, name=f"ffn_down_dw_{tag}")
    dhg, dhl, dwg, dwl, dbg, dbl = _ffn_act_bwd(sv["hp"], df, p["ffn_conv_w"], p["ffn_conv_b"], bsz, tcf,
                                                name=f"ffn_act_bwd_{tag}")
    g["ffn_conv_w"] = jnp.concatenate([dwg, dwl], axis=1)
    g["ffn_conv_b"] = jnp.concatenate([dbg, dbl], axis=1)[0]
    du2 = _matmul(dhg, p["w_ffn_up_gate"], "nt", F32, name=f"ffn_up_gate_dx_{tag}")
    du2 = _matmul(dhl, p["w_ffn_up_lin"], "nt", F32, add=du2, name=f"ffn_up_lin_dx_{tag}")
    g["w_ffn_up"] = jnp.concatenate([_matmul(sv["u2"], dhg, "tn", F32, name=f"ffn_up_gate_dw_{tag}"),
                                     _matmul(sv["u2"], dhl, "tn", F32, name=f"ffn_up_lin_dw_{tag}")], axis=1)
    dx1, dscale2, dshift2 = _ln_mod_bwd(du2, sv["x1"], scale2, dr2, alpha, bsz, name=f"ln_mod2_bwd_{tag}")
    dr1, dmix, dgate1, g["ln1_g"], g["ln1_b"], g["b_mix_out"] = _ln_res_bwd(
        dx1, sv["x"], sv["mix"], gate1, p["ln1_g"], alpha, bsz, name=f"ln_res1_bwd_{tag}")
    dm = _matmul(dmix, p["w_mix_out"], "nt", BF16, name=f"mix_out_dx_{tag}")
    g["w_mix_out"] = _matmul(sv["m"], dmix, "tn", F32, name=f"mix_out_dw_{tag}")
    dya, dyb, dzga, dzgb = _gate_merge_bwd(sv["zm"], sv["ya"], sv["yb"], dm, dims["ga_off"], name=f"merge_bwd_{tag}")
    da3 = _matmul(dya, p["w_conv_proj"], "nt", F32, name=f"conv_proj_dx_{tag}")
    g["w_conv_proj"] = _matmul(sv["a3"], dya, "tn", F32, name=f"conv_proj_dw_{tag}")
    do = _matmul(dyb, p["w_attn_proj"], "nt", BF16, name=f"attn_proj_dx_{tag}")
    dot = _matmul(p["w_attn_proj"], dyb, "nt", BF16, name=f"attn_proj_dxt_{tag}")
    g["w_attn_proj"] = _matmul(sv["ot"], dyb, "nn", F32, name=f"attn_proj_dw_{tag}")
    dzglu, g["conv_a_w"], dcb, g["ln_conv_g"], g["ln_conv_b"] = _conv_branch_bwd(
        sv["zm"], da3, p["conv_a_w"], p["conv_a_b"], p["ln_conv_g"], p["ln_conv_b"], bsz, ch,
        name=f"conv_branch_bwd_{tag}")
    g["conv_a_b"] = dcb[0]
    dk, dv, dqt, dcum = _attn_bwd(sv["zm"], sv["qkvt"], sv["cumt"], sv["cumb"], sv["ot"], do, dot, sv["lse"], bsz,
                                  heads, 2 * ch, name=f"attn_bwd_{tag}")
    dq = _to_rows_major(dqt, name=f"dq_rows_{tag}")
    dzf = _fgate_bwd(dcum, sv["zf"], bsz, name=f"fgate_bwd_{tag}")
    dzm = jnp.concatenate([dzglu, dq, dk, dv, dzga, dzgb], axis=1)
    du = _matmul(dzm, p["w_main"], "nt", F32, name=f"in_main_dx_{tag}")
    du = _matmul(dzf, p["w_f"], "nt", F32, add=du, name=f"in_forget_dx_{tag}")
    dwm, dbm = _matmul(sv["u"], dzm, "tn", F32, colsum=True, name=f"in_main_dw_{tag}")
    dwf, dbf = _matmul(sv["u"], dzf, "tn", F32, colsum=True, name=f"in_forget_dw_{tag}")
    dbm, dbf = dbm[0], dbf[0]
    g["w_main"], g["w_f"] = dwm, dwf
    g["b_in"] = jnp.concatenate([dbm[:f_off], dbf[:heads], dbm[f_off:]])
    dx, dscale1, dshift1 = _ln_mod_bwd(du, sv["x"], scale1, dr1, alpha, bsz, name=f"ln_mod1_bwd_{tag}")
    dmod = jnp.concatenate([dshift1, dscale1, dgate1, dshift2, dscale2, dgate2], axis=2)[:, 0, :]
    return dx, g, dmod


def _local_step(x, mod, loss_target, full, rep, dims):
    bsz, seq, d = x.shape
    layers = mod.shape[0]
    params = [_layer_weights(full, rep, layer, dims) for layer in range(layers)]
    h = x.reshape(bsz * seq, d)
    saved = []
    for layer in range(layers):
        h, sv = _layer_fwd(h, mod[layer], params[layer], dims, f"l{layer}")
        saved.append(sv)
    dh, sq = _loss_head(h, loss_target.reshape(bsz * seq, d), name="loss_head")
    loss_local = 0.5 * jnp.sum(sq) / d
    grads, dmods = [None] * layers, [None] * layers
    for layer in reversed(range(layers)):
        dh, grads[layer], dmods[layer] = _layer_bwd(dh, params[layer], saved[layer], dims, f"l{layer}")
    per_layer = ("w_main", "w_f")
    stacked = {wname: [grads[layer][wname] for layer in range(layers)] if wname in per_layer
               else jnp.stack([grads[layer][wname] for layer in range(layers)]) for wname in grads[0]}
    return loss_local, dh.reshape(bsz, seq, d), stacked, jnp.stack(dmods)


def _pad_rows(a):
    extra = -a.shape[-2] % (2 * SUBLANES)
    if extra == 0:
        return a
    return jnp.pad(a, [(0, 0)] * (a.ndim - 2) + [(0, extra), (0, 0)])


def _w_in_pieces(n, f_off, heads):
    n_in = N_CHIPS * n
    segments = [(0, f_off, "main", 0), (f_off, f_off + heads, "f", 0), (f_off + heads, n_in, "main", f_off)]
    pieces = []
    for chip in range(N_CHIPS):
        lo, hi = chip * n, (chip + 1) * n
        for a, b, target, t0 in segments:
            s, e = max(lo, a), min(hi, b)
            if s < e:
                pieces.append((chip, s - lo, e - lo, target, t0 + s - a))
    return pieces


def _w_in_from_slabs(slabs, f_off, heads, *, name):
    layers, _, k, n = slabs.shape
    tr = _pick(k, (256, 128, 64, 32, 16))
    n_main = N_CHIPS * n - heads
    pieces = _w_in_pieces(n, f_off, heads)

    def body(s_ref, m_ref, f_ref):
        f_ref[...] = jnp.zeros_like(f_ref)
        for chip in range(N_CHIPS):
            slab = s_ref[0, chip].astype(F32)
            for pc, s0, s1, target, t0 in pieces:
                if pc == chip:
                    out = m_ref if target == "main" else f_ref
                    out[0, :, t0:t0 + s1 - s0] = slab[:, s0:s1].astype(out.dtype)

    return _call(body, name=name, grid=(layers, k // tr),
                 in_specs=[pl.BlockSpec((1, N_CHIPS, tr, n), lambda layer, i: (layer, 0, i, 0))],
                 out_specs=[pl.BlockSpec((1, tr, n_main), lambda layer, i: (layer, i, 0)),
                            pl.BlockSpec((1, tr, LANES), lambda layer, i: (layer, i, 0))],
                 out_shape=[_sds((layers, k, n_main), slabs.dtype), _sds((layers, k, LANES), slabs.dtype)],
                 sem=("parallel", "parallel"))(slabs)


def _w_in_to_slabs(d_main, d_f, n, f_off, heads, *, name):
    layers = len(d_main)
    k = d_main[0].shape[0]
    tr = _pick(k, (128, 64, 32, 16, 8))
    pieces = _w_in_pieces(n, f_off, heads)

    def body(*refs):
        m_refs, f_refs, o_ref = refs[:layers], refs[layers:2 * layers], refs[2 * layers]
        for layer in range(layers):
            for chip, s0, s1, target, t0 in pieces:
                src = m_refs[layer] if target == "main" else f_refs[layer]
                o_ref[layer, chip, :, s0:s1] = src[:, t0:t0 + s1 - s0]

    return _call(body, name=name, grid=(k // tr,),
                 in_specs=[pl.BlockSpec((tr, d_main[0].shape[1]), lambda i: (i, 0))] * layers
                 + [pl.BlockSpec((tr, LANES), lambda i: (i, 0))] * layers,
                 out_specs=pl.BlockSpec((layers, N_CHIPS, tr, n), lambda i: (0, 0, i, 0)),
                 out_shape=_sds((layers, N_CHIPS, k, n), F32), sem=("parallel",))(*d_main, *d_f)


def kernel(x, c, w_ada, b_ada, w_in, b_in, conv_a_w, conv_a_b, ln_conv_g, ln_conv_b, w_conv_proj, w_attn_proj, w_mix_out, b_mix_out, ln1_g, ln1_b, w_ffn_up, ffn_conv_w, ffn_conv_b, w_ffn_down, ln2_g, ln2_b, loss_target, m_w_ada, m_b_ada, m_w_in, m_b_in, m_conv_a_w, m_conv_a_b, m_ln_conv_g, m_ln_conv_b, m_w_conv_proj, m_w_attn_proj, m_w_mix_out, m_b_mix_out, m_ln1_g, m_ln1_b, m_w_ffn_up, m_ffn_conv_w, m_ffn_conv_b, m_w_ffn_down, m_ln2_g, m_ln2_b, v_w_ada, v_b_ada, v_w_in, v_b_in, v_conv_a_w, v_conv_a_b, v_ln_conv_g, v_ln_conv_b, v_w_conv_proj, v_w_attn_proj, v_w_mix_out, v_b_mix_out, v_ln1_g, v_ln1_b, v_w_ffn_up, v_ffn_conv_w, v_ffn_conv_b, v_w_ffn_down, v_ln2_g, v_ln2_b):
    weights = dict(zip(WEIGHTS, (w_ada, b_ada, w_in, b_in, conv_a_w, conv_a_b, ln_conv_g, ln_conv_b, w_conv_proj,
                                 w_attn_proj, w_mix_out, b_mix_out, ln1_g, ln1_b, w_ffn_up, ffn_conv_w, ffn_conv_b,
                                 w_ffn_down, ln2_g, ln2_b)))
    mom1 = dict(zip(WEIGHTS, (m_w_ada, m_b_ada, m_w_in, m_b_in, m_conv_a_w, m_conv_a_b, m_ln_conv_g, m_ln_conv_b,
                              m_w_conv_proj, m_w_attn_proj, m_w_mix_out, m_b_mix_out, m_ln1_g, m_ln1_b, m_w_ffn_up,
                              m_ffn_conv_w, m_ffn_conv_b, m_w_ffn_down, m_ln2_g, m_ln2_b)))
    mom2 = dict(zip(WEIGHTS, (v_w_ada, v_b_ada, v_w_in, v_b_in, v_conv_a_w, v_conv_a_b, v_ln_conv_g, v_ln_conv_b,
                              v_w_conv_proj, v_w_attn_proj, v_w_mix_out, v_b_mix_out, v_ln1_g, v_ln1_b, v_w_ffn_up,
                              v_ffn_conv_w, v_ffn_conv_b, v_w_ffn_down, v_ln2_g, v_ln2_b)))
    bsz, seq, d = x.shape
    layers = w_ada.shape[0]
    ch = conv_a_w.shape[2] * N_CHIPS
    width = w_attn_proj.shape[1]
    heads = width // HEAD_DIM
    d_ff = w_ffn_down.shape[1] * N_CHIPS
    dims = dict(bsz=bsz, d=d, ch=ch, heads=heads, alpha=(2.0 * layers) ** 0.25, f_off=2 * ch + 3 * width,
                ga_off=2 * ch + 3 * width, tcf=_pick(d_ff, (256, 128)))
    chip = 2 * lax.axis_index("x") + lax.axis_index("y")
    device = 2 * chip + lax.axis_index("c")
    ada_cols = w_ada.shape[2]

    c_act = _silu_rows(_all_gather8(c, name="gather_c").reshape(N_DEVICES * bsz, d), name="silu_c")
    b_ada_mine = lax.dynamic_slice_in_dim(b_ada, chip * ada_cols, ada_cols, axis=1)
    mod_cols = jnp.stack([_matmul(c_act, w_ada[layer], "nn", F32, bias=b_ada_mine[layer][None], name=f"ada_l{layer}")
                          for layer in range(layers)])
    mod_all = _all_gather8(mod_cols, name="gather_mod")
    mod_all = jnp.concatenate([mod_all[2 * k] for k in range(N_CHIPS)], axis=-1)
    mod = lax.dynamic_slice_in_dim(mod_all, device * bsz, bsz, axis=1)

    shards = [_pad_rows(weights[wname].astype(BF16) if as_bf16 else weights[wname]) for wname, _, as_bf16 in GATHERED]
    modes = [mode for _, mode, _ in GATHERED]
    whole = _gather_weights(shards, modes, name="gather_weights")
    full = {wname: w[:, :weights[wname].shape[1]] if mode == "cols" else w
            for (wname, mode, _), w in zip(GATHERED, whole)}
    full["w_main"], full["w_f"] = _w_in_from_slabs(full.pop("w_in"), dims["f_off"], heads, name="w_in_from_slabs")
    rep = {wname: weights[wname] for wname in REPLICATED}

    loss_local, grad_x, grads, dmod = _local_step(x, mod, loss_target, full, rep, dims)
    loss = lax.psum(loss_local, ("x", "y", "c"))

    grads["w_in"] = _w_in_to_slabs(grads.pop("w_main"), grads.pop("w_f"), w_in.shape[2], dims["f_off"], heads,
                                   name="w_in_to_slabs")
    shard_shapes = [s.shape for s in shards]
    reduced = _reduce_scatter([_pad_rows(grads[wname]) for wname, _, _ in GATHERED], modes, shard_shapes)
    grad = {wname: r[:, :weights[wname].shape[1]] for (wname, _, _), r in zip(GATHERED, reduced)}

    small = jnp.concatenate([dmod.reshape(-1)] + [grads[wname].reshape(-1) for wname in REPLICATED])
    n_small = small.shape[0]
    rows = -(-n_small // (SUBLANES * LANES)) * SUBLANES
    small = jnp.pad(small, (0, rows * LANES - n_small)).reshape(rows, LANES)
    gathered = _all_gather8(small, name="gather_small")
    n_dmod = dmod.size
    dmod_all = gathered.reshape(N_DEVICES, -1)[:, :n_dmod].reshape(N_DEVICES, layers, bsz, 6 * d)
    dmod_all = jnp.transpose(dmod_all, (1, 0, 2, 3)).reshape(layers, N_DEVICES * bsz, 6 * d)
    summed = _sum_slots(gathered, name="sum_small").reshape(-1)
    off = n_dmod
    for wname in REPLICATED:
        n = weights[wname].size
        grad[wname] = summed[off:off + n].reshape(weights[wname].shape)
        off += n
    dmod_mine = lax.dynamic_slice_in_dim(dmod_all, chip * ada_cols, ada_cols, axis=2)
    grad["w_ada"] = jnp.stack([_matmul(c_act, dmod_mine[layer], "tn", F32, name=f"ada_dw_l{layer}")
                               for layer in range(layers)])
    grad["b_ada"] = jnp.stack([_colsum(dmod_all[layer], name=f"ada_db_l{layer}")[0] for layer in range(layers)])

    delta, new_m, new_v = {}, {}, {}
    for wname in WEIGHTS:
        delta[wname], new_m[wname], new_v[wname] = _adamw(weights[wname], grad[wname], mom1[wname], mom2[wname],
                                                          name=f"adamw_{wname}")
    return (loss, grad_x, *[grad[wname] for wname in WEIGHTS], *[delta[wname] for wname in WEIGHTS],
            *[new_m[wname] for wname in WEIGHTS], *[new_v[wname] for wname in WEIGHTS])
```

```python
import math

import jax
import jax.numpy as jnp
from jax import lax
from jax.experimental import pallas as pl
from jax.experimental.pallas import tpu as pltpu

F32 = jnp.float32
BF16 = jnp.bfloat16
MESH = pl.DeviceIdType.MESH

LN_EPS = 1e-5
HEAD_DIM = 64
ATTN_SCALE = HEAD_DIM ** -0.5
NEG = -1e30
FFN_PAD = 8
LANES = 128
SUBLANES = 8
ROW_CHUNK = 256
ATTN_BLOCK = 256
ATTN_UNROLL = 3
N_CHIPS = 4
N_DEVICES = 8
VMEM_LIMIT = 56 * 1024 * 1024

ADAM_LR = 0.001
ADAM_B1 = 0.9
ADAM_B2 = 0.999
ADAM_EPS = 1e-08
ADAM_WD = 0.01
ADAM_STEP = 10

GATHERED = (("w_in", "slab", True), ("conv_a_w", "cols", False), ("w_conv_proj", "cols", True),
            ("w_attn_proj", "cols", True), ("w_mix_out", "rows", True), ("w_ffn_up", "cols", True),
            ("ffn_conv_w", "cols", False), ("w_ffn_down", "rows", True))
REPLICATED = ("b_in", "conv_a_b", "ln_conv_g", "ln_conv_b", "b_mix_out", "ln1_g", "ln1_b",
              "ffn_conv_b", "ln2_g", "ln2_b")
WEIGHTS = ("w_ada", "b_ada", "w_in", "b_in", "conv_a_w", "conv_a_b", "ln_conv_g", "ln_conv_b",
           "w_conv_proj", "w_attn_proj", "w_mix_out", "b_mix_out", "ln1_g", "ln1_b", "w_ffn_up",
           "ffn_conv_w", "ffn_conv_b", "w_ffn_down", "ln2_g", "ln2_b")


def _pick(n, cands):
    for cand in cands:
        if n % cand == 0:
            return cand
    return n


def _call(body, *, name, grid, in_specs, out_specs, out_shape, scratch=(), sem=None):
    return pl.pallas_call(
        body, name=name, grid=grid, in_specs=in_specs, out_specs=out_specs, out_shape=out_shape,
        scratch_shapes=list(scratch),
        compiler_params=pltpu.CompilerParams(dimension_semantics=sem, vmem_limit_bytes=VMEM_LIMIT))


def _sds(shape, dtype):
    return jax.ShapeDtypeStruct(tuple(shape), dtype)


def _chunked(rows, fn):
    chunk = min(ROW_CHUNK, rows)
    if rows == chunk:
        fn(pl.ds(0, rows))
        return

    def step(i, carry):
        fn(pl.ds(pl.multiple_of(i * chunk, chunk), chunk))
        return carry

    lax.fori_loop(0, rows // chunk, step, 0)


def _matmul(a, b, mode, out_dtype, *, bias=None, add=None, colsum=False, b_k_first=0, name):
    if mode == "nn":
        (m, k), (_, n) = a.shape, b.shape
    elif mode == "nt":
        (m, k), (n, _) = a.shape, b.shape
    else:
        (k, m), (_, n) = a.shape, b.shape
    tm = _pick(m, (1024, 1408, 512, 256, 128))
    tn = _pick(n, (1536, 1408, 1024, 512, 256, 128))
    tk = k if k <= 1536 else _pick(k, (1024, 1536, 1408, 512, 256, 128))
    nk = k // tk
    assert b_k_first % tk == 0 and (b_k_first == 0 or mode == "nt")
    k_blk0 = b_k_first // tk
    if mode == "nn":
        a_spec = pl.BlockSpec((tm, tk), lambda i, j, kk: (i, kk))
        b_spec = pl.BlockSpec((tk, tn), lambda i, j, kk: (kk, j))
        dims = (((1,), (0,)), ((), ()))
    elif mode == "nt":
        a_spec = pl.BlockSpec((tm, tk), lambda i, j, kk: (i, kk))
        b_spec = pl.BlockSpec((tn, tk), lambda i, j, kk: (j, k_blk0 + kk))
        dims = (((1,), (1,)), ((), ()))
    else:
        a_spec = pl.BlockSpec((tk, tm), lambda i, j, kk: (kk, i))
        b_spec = pl.BlockSpec((tk, tn), lambda i, j, kk: (kk, j))
        dims = (((0,), (0,)), ((), ()))
    in_specs = [a_spec, b_spec]
    operands = [a, b]
    if bias is not None:
        in_specs.append(pl.BlockSpec((1, tn), lambda i, j, kk: (0, j)))
        operands.append(bias)
    if add is not None:
        in_specs.append(pl.BlockSpec((tm, tn), lambda i, j, kk: (i, j)))
        operands.append(add)

    def body(a_ref, b_ref, *rest):
        rest = list(rest)
        bias_ref = rest.pop(0) if bias is not None else None
        add_ref = rest.pop(0) if add is not None else None
        o_ref = rest.pop(0)
        prod = lax.dot_general(a_ref[...].astype(BF16), b_ref[...].astype(BF16), dims,
                               preferred_element_type=F32)
        if colsum:
            cs_ref = rest.pop(0)
            part = jnp.sum(b_ref[...].astype(F32), axis=0, keepdims=True)

            @pl.when(pl.program_id(2) == 0)
            def _():
                cs_ref[...] = part

            @pl.when(pl.program_id(2) > 0)
            def _():
                cs_ref[...] += part

        def finish(r):
            if bias_ref is not None:
                r = r + bias_ref[...]
            if add_ref is not None:
                r = r + add_ref[...]
            o_ref[...] = r.astype(o_ref.dtype)

        if nk == 1:
            finish(prod)
            return
        acc_ref = rest.pop(0)
        kk = pl.program_id(2)

        @pl.when(kk == 0)
        def _():
            acc_ref[...] = prod

        @pl.when(kk > 0)
        def _():
            acc_ref[...] += prod

        @pl.when(kk == nk - 1)
        def _():
            finish(acc_ref[...])

    out_specs = pl.BlockSpec((tm, tn), lambda i, j, kk: (i, j))
    out_shape = _sds((m, n), out_dtype)
    if colsum:
        assert mode == "tn" and m == tm
        out_specs = [out_specs, pl.BlockSpec((1, tn), lambda i, j, kk: (0, j))]
        out_shape = [out_shape, _sds((1, n), F32)]
    return _call(body, name=name, grid=(m // tm, n // tn, nk), in_specs=in_specs, out_specs=out_specs,
                 out_shape=out_shape, scratch=[pltpu.VMEM((tm, tn), F32)] if nk > 1 else [],
                 sem=("parallel", "parallel", "arbitrary"))(*operands)


def _colsum(x, *, name):
    rows, n = x.shape
    tr = _pick(rows, (1024, 512, 256, 128))
    tn = _pick(n, (512, 256, 128))

    def body(x_ref, o_ref):
        @pl.when(pl.program_id(1) == 0)
        def _():
            o_ref[...] = jnp.zeros_like(o_ref)

        o_ref[...] += jnp.sum(x_ref[...].astype(F32), axis=0, keepdims=True)

    return _call(body, name=name, grid=(n // tn, rows // tr),
                 in_specs=[pl.BlockSpec((tr, tn), lambda j, i: (i, j))],
                 out_specs=pl.BlockSpec((1, tn), lambda j, i: (0, j)),
                 out_shape=_sds((1, n), F32), sem=("parallel", "arbitrary"))(x)


def _ln_stats(x):
    mu = jnp.mean(x, axis=-1, keepdims=True)
    xc = x - mu
    var = jnp.mean(xc * xc, axis=-1, keepdims=True)
    rstd = lax.rsqrt(var + LN_EPS)
    return xc * rstd, rstd


def _ln_bwd(dn, n, rstd):
    return rstd * (dn - jnp.mean(dn, axis=-1, keepdims=True) - n * jnp.mean(dn * n, axis=-1, keepdims=True))


def _seq_tiles(t, bsz, cands=(1024, 512, 256, 128, 64, 32, 16, 8)):
    s = t // bsz
    ts = _pick(s, cands)
    return s, ts, s // ts


def _ln_mod_fwd(x, scale, shift, bsz, *, name):
    t, d = x.shape
    _, ts, ns = _seq_tiles(t, bsz)

    def body(x_ref, sc_ref, sh_ref, u_ref):
        one_scale = 1.0 + sc_ref[0]
        shift_v = sh_ref[0]

        def piece(rows):
            n, _ = _ln_stats(x_ref[rows, :])
            u_ref[rows, :] = (n * one_scale + shift_v).astype(u_ref.dtype)

        _chunked(ts, piece)

    row = pl.BlockSpec((ts, d), lambda b, i: (b * ns + i, 0))
    per = pl.BlockSpec((1, 1, d), lambda b, i: (b, 0, 0))
    return _call(body, name=name, grid=(bsz, ns), in_specs=[row, per, per], out_specs=row,
                 out_shape=_sds((t, d), BF16), sem=("parallel", "parallel"))(x, scale, shift)


def _ln_mod_bwd(du, x, scale, dr, alpha, bsz, *, name):
    t, d = x.shape
    _, ts, ns = _seq_tiles(t, bsz)

    def body(du_ref, x_ref, sc_ref, dr_ref, dx_ref, dsc_ref, dsh_ref):
        @pl.when(pl.program_id(1) == 0)
        def _():
            dsc_ref[...] = jnp.zeros_like(dsc_ref)
            dsh_ref[...] = jnp.zeros_like(dsh_ref)

        one_scale = 1.0 + sc_ref[0]

        def piece(rows):
            du_v = du_ref[rows, :]
            n, rstd = _ln_stats(x_ref[rows, :])
            dsc_ref[0] += jnp.sum(du_v * n, axis=0, keepdims=True)
            dsh_ref[0] += jnp.sum(du_v, axis=0, keepdims=True)
            dx_ref[rows, :] = alpha * dr_ref[rows, :] + _ln_bwd(du_v * one_scale, n, rstd)

        _chunked(ts, piece)

    row = pl.BlockSpec((ts, d), lambda b, i: (b * ns + i, 0))
    per = pl.BlockSpec((1, 1, d), lambda b, i: (b, 0, 0))
    return _call(body, name=name, grid=(bsz, ns), in_specs=[row, row, per, row],
                 out_specs=[row, per, per],
                 out_shape=[_sds((t, d), F32), _sds((bsz, 1, d), F32), _sds((bsz, 1, d), F32)],
                 sem=("parallel", "arbitrary"))(du, x, scale, dr)


def _ln_res_fwd(x, y, gate, g, b, alpha, bsz, *, name):
    t, d = x.shape
    _, ts, ns = _seq_tiles(t, bsz)

    def body(x_ref, y_ref, gt_ref, g_ref, b_ref, o_ref):
        one_gate = 1.0 + gt_ref[0]

        def piece(rows):
            n, _ = _ln_stats(alpha * x_ref[rows, :] + one_gate * y_ref[rows, :])
            o_ref[rows, :] = n * g_ref[...] + b_ref[...]

        _chunked(ts, piece)

    row = pl.BlockSpec((ts, d), lambda bb, i: (bb * ns + i, 0))
    per = pl.BlockSpec((1, 1, d), lambda bb, i: (bb, 0, 0))
    vec = pl.BlockSpec((1, d), lambda bb, i: (0, 0))
    return _call(body, name=name, grid=(bsz, ns), in_specs=[row, row, per, vec, vec], out_specs=row,
                 out_shape=_sds((t, d), F32), sem=("parallel", "parallel"))(x, y, gate, g, b)


def _ln_res_bwd(do, x, y, gate, g, alpha, bsz, *, name):
    t, d = x.shape
    _, ts, ns = _seq_tiles(t, bsz)

    def body(do_ref, x_ref, y_ref, gt_ref, g_ref, dr_ref, dy_ref, dgt_ref, dg_ref, db_ref, dys_ref):
        first_tile = pl.program_id(1) == 0

        @pl.when(first_tile)
        def _():
            dgt_ref[...] = jnp.zeros_like(dgt_ref)

        @pl.when(jnp.logical_and(first_tile, pl.program_id(0) == 0))
        def _():
            dg_ref[...] = jnp.zeros_like(dg_ref)
            db_ref[...] = jnp.zeros_like(db_ref)
            dys_ref[...] = jnp.zeros_like(dys_ref)

        one_gate = 1.0 + gt_ref[0]

        def piece(rows):
            do_v = do_ref[rows, :]
            y_v = y_ref[rows, :]
            n, rstd = _ln_stats(alpha * x_ref[rows, :] + one_gate * y_v)
            dg_ref[...] += jnp.sum(do_v * n, axis=0, keepdims=True)
            db_ref[...] += jnp.sum(do_v, axis=0, keepdims=True)
            dr = _ln_bwd(do_v * g_ref[...], n, rstd)
            dr_ref[rows, :] = dr
            dy = one_gate * dr
            dy_ref[rows, :] = dy.astype(dy_ref.dtype)
            dys_ref[...] += jnp.sum(dy, axis=0, keepdims=True)
            dgt_ref[0] += jnp.sum(dr * y_v, axis=0, keepdims=True)

        _chunked(ts, piece)

    row = pl.BlockSpec((ts, d), lambda bb, i: (bb * ns + i, 0))
    per = pl.BlockSpec((1, 1, d), lambda bb, i: (bb, 0, 0))
    vec = pl.BlockSpec((1, d), lambda bb, i: (0, 0))
    return _call(body, name=name, grid=(bsz, ns), in_specs=[row, row, row, per, vec],
                 out_specs=[row, row, per, vec, vec, vec],
                 out_shape=[_sds((t, d), F32), _sds((t, d), BF16), _sds((bsz, 1, d), F32),
                            _sds((1, d), F32), _sds((1, d), F32), _sds((1, d), F32)],
                 sem=("arbitrary", "arbitrary"))(do, x, y, gate, g)


def _ln_res_mod_fwd(x, y, gate, g, b, scale, shift, alpha, bsz, *, name):
    t, d = x.shape
    _, ts, ns = _seq_tiles(t, bsz)

    def body(x_ref, y_ref, gt_ref, g_ref, b_ref, sc_ref, sh_ref, o_ref, u_ref):
        one_gate = 1.0 + gt_ref[0]
        one_scale = 1.0 + sc_ref[0]
        shift_v = sh_ref[0]

        def piece(rows):
            n, _ = _ln_stats(alpha * x_ref[rows, :] + one_gate * y_ref[rows, :])
            x1 = n * g_ref[...] + b_ref[...]
            o_ref[rows, :] = x1
            n1, _ = _ln_stats(x1)
            u_ref[rows, :] = (n1 * one_scale + shift_v).astype(u_ref.dtype)

        _chunked(ts, piece)

    row = pl.BlockSpec((ts, d), lambda bb, i: (bb * ns + i, 0))
    per = pl.BlockSpec((1, 1, d), lambda bb, i: (bb, 0, 0))
    vec = pl.BlockSpec((1, d), lambda bb, i: (0, 0))
    return _call(body, name=name, grid=(bsz, ns), in_specs=[row, row, per, vec, vec, per, per],
                 out_specs=[row, row], out_shape=[_sds((t, d), F32), _sds((t, d), BF16)],
                 sem=("parallel", "parallel"))(x, y, gate, g, b, scale, shift)


def _ln_mod_res_bwd(du, dr_up, scale, x, y, gate, g, b, alpha, bsz, *, name):
    t, d = x.shape
    _, ts, ns = _seq_tiles(t, bsz, (512, 256, 128, 64, 32, 16, 8))

    def body(du_ref, up_ref, sc_ref, x_ref, y_ref, gt_ref, g_ref, b_ref,
             dr_ref, dy_ref, dgt_ref, dg_ref, db_ref, dys_ref, dsc_ref, dsh_ref):
        first_tile = pl.program_id(1) == 0

        @pl.when(first_tile)
        def _():
            for ref in (dgt_ref, dsc_ref, dsh_ref):
                ref[...] = jnp.zeros_like(ref)

        @pl.when(jnp.logical_and(first_tile, pl.program_id(0) == 0))
        def _():
            for ref in (dg_ref, db_ref, dys_ref):
                ref[...] = jnp.zeros_like(ref)

        one_gate = 1.0 + gt_ref[0]
        one_scale = 1.0 + sc_ref[0]

        def piece(rows):
            y_v = y_ref[rows, :]
            n, rstd = _ln_stats(alpha * x_ref[rows, :] + one_gate * y_v)
            n1, rstd1 = _ln_stats(n * g_ref[...] + b_ref[...])
            du_v = du_ref[rows, :]
            dsc_ref[0] += jnp.sum(du_v * n1, axis=0, keepdims=True)
            dsh_ref[0] += jnp.sum(du_v, axis=0, keepdims=True)
            dx1 = alpha * up_ref[rows, :] + _ln_bwd(du_v * one_scale, n1, rstd1)
            dg_ref[...] += jnp.sum(dx1 * n, axis=0, keepdims=True)
            db_ref[...] += jnp.sum(dx1, axis=0, keepdims=True)
            dr = _ln_bwd(dx1 * g_ref[...], n, rstd)
            dr_ref[rows, :] = dr
            dy = one_gate * dr
            dy_ref[rows, :] = dy.astype(dy_ref.dtype)
            dys_ref[...] += jnp.sum(dy, axis=0, keepdims=True)
            dgt_ref[0] += jnp.sum(dr * y_v, axis=0, keepdims=True)

        _chunked(ts, piece)

    row = pl.BlockSpec((ts, d), lambda bb, i: (bb * ns + i, 0))
    per = pl.BlockSpec((1, 1, d), lambda bb, i: (bb, 0, 0))
    vec = pl.BlockSpec((1, d), lambda bb, i: (0, 0))
    return _call(body, name=name, grid=(bsz, ns), in_specs=[row, row, per, row, row, per, vec, vec],
                 out_specs=[row, row, per, vec, vec, vec, per, per],
                 out_shape=[_sds((t, d), F32), _sds((t, d), BF16), _sds((bsz, 1, d), F32), _sds((1, d), F32),
                            _sds((1, d), F32), _sds((1, d), F32), _sds((bsz, 1, d), F32), _sds((bsz, 1, d), F32)],
                 sem=("arbitrary", "arbitrary"))(du, dr_up, scale, x, y, gate, g, b)


def _ln_res_loss(x, y, gate, g, b, target, alpha, bsz, *, name):
    t, d = x.shape
    _, ts, ns = _seq_tiles(t, bsz)

    def body(x_ref, y_ref, gt_ref, g_ref, b_ref, t_ref, dy_ref, s_ref):
        @pl.when(jnp.logical_and(pl.program_id(0) == 0, pl.program_id(1) == 0))
        def _():
            s_ref[...] = jnp.zeros_like(s_ref)

        one_gate = 1.0 + gt_ref[0]

        def piece(rows):
            n, _ = _ln_stats(alpha * x_ref[rows, :] + one_gate * y_ref[rows, :])
            e = (n * g_ref[...] + b_ref[...]) - t_ref[rows, :]
            dy_ref[rows, :] = e * (1.0 / d)
            s_ref[...] += jnp.sum(e * e, axis=0, keepdims=True)

        _chunked(ts, piece)

    row = pl.BlockSpec((ts, d), lambda bb, i: (bb * ns + i, 0))
    per = pl.BlockSpec((1, 1, d), lambda bb, i: (bb, 0, 0))
    vec = pl.BlockSpec((1, d), lambda bb, i: (0, 0))
    return _call(body, name=name, grid=(bsz, ns), in_specs=[row, row, per, vec, vec, row], out_specs=[row, vec],
                 out_shape=[_sds((t, d), F32), _sds((1, d), F32)],
                 sem=("arbitrary", "arbitrary"))(x, y, gate, g, b, target)


def _sigmoid(v):
    return 1.0 / (1.0 + jnp.exp(-v))


def _silu_rows(c, *, name):
    rows, d = c.shape

    def body(c_ref, o_ref):
        v = c_ref[...]
        o_ref[...] = (v * _sigmoid(v)).astype(o_ref.dtype)

    full = pl.BlockSpec((rows, d), lambda i: (0, 0))
    return _call(body, name=name, grid=(1,), in_specs=[full], out_specs=full,
                 out_shape=_sds((rows, d), BF16), sem=("arbitrary",))(c)


def _gate_cols(d, ga_off):
    tc = _pick(math.gcd(d, ga_off), (512, 256, 128))
    return tc, ga_off // tc, (ga_off + d) // tc


def _gate_merge_fwd(z, ya, yb, ga_off, *, name):
    t, d = ya.shape
    tr = _pick(t, (1024, 512, 256, 128, 64, 32, 16, 8))
    tc, ga_blk, gb_blk = _gate_cols(d, ga_off)

    def body(ga_ref, gb_ref, ya_ref, yb_ref, o_ref):
        def piece(rows):
            o_ref[rows, :] = (_sigmoid(ga_ref[rows, :].astype(F32)) * ya_ref[rows, :].astype(F32)
                              + _sigmoid(gb_ref[rows, :].astype(F32)) * yb_ref[rows, :].astype(F32)
                              ).astype(o_ref.dtype)

        _chunked(tr, piece)

    blk = pl.BlockSpec((tr, tc), lambda i, j: (i, j))
    return _call(body, name=name, grid=(t // tr, d // tc),
                 in_specs=[pl.BlockSpec((tr, tc), lambda i, j: (i, ga_blk + j)),
                           pl.BlockSpec((tr, tc), lambda i, j: (i, gb_blk + j)), blk, blk],
                 out_specs=blk, out_shape=_sds((t, d), BF16), sem=("parallel", "parallel"))(z, z, ya, yb)


def _gate_merge_bwd(z, ya, yb, dm, ga_off, *, name):
    t, d = ya.shape
    tr = _pick(t, (1024, 512, 256, 128, 64, 32, 16, 8))
    tc, ga_blk, gb_blk = _gate_cols(d, ga_off)

    def body(ga_ref, gb_ref, ya_ref, yb_ref, dm_ref, dya_ref, dyb_ref, dga_ref, dgb_ref):
        def piece(rows):
            dm_v = dm_ref[rows, :].astype(F32)
            sa = _sigmoid(ga_ref[rows, :].astype(F32))
            sb = _sigmoid(gb_ref[rows, :].astype(F32))
            dya_ref[rows, :] = (dm_v * sa).astype(dya_ref.dtype)
            dyb_ref[rows, :] = (dm_v * sb).astype(dyb_ref.dtype)
            dga_ref[rows, :] = (dm_v * ya_ref[rows, :].astype(F32) * sa * (1.0 - sa)).astype(dga_ref.dtype)
            dgb_ref[rows, :] = (dm_v * yb_ref[rows, :].astype(F32) * sb * (1.0 - sb)).astype(dgb_ref.dtype)

        _chunked(tr, piece)

    blk = pl.BlockSpec((tr, tc), lambda i, j: (i, j))
    return _call(body, name=name, grid=(t // tr, d // tc),
                 in_specs=[pl.BlockSpec((tr, tc), lambda i, j: (i, ga_blk + j)),
                           pl.BlockSpec((tr, tc), lambda i, j: (i, gb_blk + j)), blk, blk, blk],
                 out_specs=[blk, blk, blk, blk], out_shape=[_sds((t, d), BF16)] * 4,
                 sem=("parallel", "parallel"))(z, z, ya, yb, dm)


CONV_ROWS = 64
CONV_PAD = 32


def _row_shifts(win):
    total = win.shape[0]
    return [win] + [pltpu.roll(win, total - b, axis=0) for b in range(1, SUBLANES)]


def _shifted_rows(copies, shift):
    start = SUBLANES * (shift // SUBLANES)
    return copies[shift % SUBLANES][start:start + CONV_ROWS]


def _fill_glu(z_ref, ext_ref, s, ch):
    ext_ref[pl.ds(0, CONV_PAD), :] = jnp.zeros((CONV_PAD, ch), F32)

    chunk = min(ROW_CHUNK, s)

    def piece(i, carry):
        start = pl.multiple_of(i * chunk, chunk)
        zz = z_ref[pl.ds(start, chunk), :].astype(F32)
        ext_ref[pl.ds(pl.multiple_of(CONV_PAD + start, CONV_PAD), chunk), :] = zz[:, :ch] * _sigmoid(zz[:, ch:])
        return carry

    lax.fori_loop(0, s // chunk, piece, 0)


def _conv_piece(ext_ref, w_ref, cb_ref, base, kw):
    copies = _row_shifts(ext_ref[pl.ds(base, CONV_ROWS + CONV_PAD), :])
    acc = cb_ref[...] + w_ref[pl.ds(0, 1), :] * _shifted_rows(copies, CONV_PAD - (kw - 1))
    for k in range(1, kw):
        acc = acc + w_ref[pl.ds(k, 1), :] * _shifted_rows(copies, CONV_PAD - (kw - 1) + k)
    return acc, copies


def _conv_branch_fwd(z, w, cb, lg, lb, bsz, ch, *, name):
    t = z.shape[0]
    s = t // bsz
    kw = w.shape[0]

    def body(z_ref, w_ref, cb_ref, lg_ref, lb_ref, o_ref, ext_ref):
        _fill_glu(z_ref, ext_ref, s, ch)

        def step(i, carry):
            base = pl.multiple_of(i * CONV_ROWS, CONV_ROWS)
            a1, _ = _conv_piece(ext_ref, w_ref, cb_ref, base, kw)
            n, _ = _ln_stats(a1)
            a2 = n * lg_ref[...] + lb_ref[...]
            o_ref[pl.ds(base, CONV_ROWS), :] = (a2 * _sigmoid(a2)).astype(o_ref.dtype)
            return carry

        lax.fori_loop(0, s // CONV_ROWS, step, 0)

    vec = pl.BlockSpec((1, ch), lambda b: (0, 0))
    return _call(body, name=name, grid=(bsz,),
                 in_specs=[pl.BlockSpec((s, 2 * ch), lambda b: (b, 0)), pl.BlockSpec((kw, ch), lambda b: (0, 0)),
                           vec, vec, vec],
                 out_specs=pl.BlockSpec((s, ch), lambda b: (b, 0)), out_shape=_sds((t, ch), BF16),
                 scratch=[pltpu.VMEM((CONV_PAD + s, ch), F32)], sem=("parallel",))(z, w, cb, lg, lb)


def _conv_branch_bwd(z, da3, w, cb, lg, lb, bsz, ch, *, name):
    t = z.shape[0]
    s = t // bsz
    kw = w.shape[0]
    n_rows = CONV_ROWS + CONV_PAD

    def body(z_ref, d_ref, w_ref, cb_ref, lg_ref, lb_ref, dz_ref, dw_ref, dcb_ref, dlg_ref, dlb_ref,
             ext_ref, da1_ref):
        @pl.when(pl.program_id(0) == 0)
        def _():
            for ref in (dw_ref, dcb_ref, dlg_ref, dlb_ref):
                ref[...] = jnp.zeros_like(ref)

        _fill_glu(z_ref, ext_ref, s, ch)
        da1_ref[pl.ds(s, CONV_PAD), :] = jnp.zeros((CONV_PAD, ch), F32)

        def grad_a1(i, carry):
            dlg, dlb = carry
            base = pl.multiple_of(i * CONV_ROWS, CONV_ROWS)
            a1, _ = _conv_piece(ext_ref, w_ref, cb_ref, base, kw)
            n, rstd = _ln_stats(a1)
            a2 = n * lg_ref[...] + lb_ref[...]
            sg = _sigmoid(a2)
            da2 = d_ref[pl.ds(base, CONV_ROWS), :] * (sg * (1.0 + a2 * (1.0 - sg)))
            da1_ref[pl.ds(base, CONV_ROWS), :] = _ln_bwd(da2 * lg_ref[...], n, rstd)
            return (dlg + jnp.sum(da2 * n, axis=0, keepdims=True), dlb + jnp.sum(da2, axis=0, keepdims=True))

        zero = jnp.zeros((1, ch), F32)
        dlg, dlb = lax.fori_loop(0, s // CONV_ROWS, grad_a1, (zero, zero))
        dlg_ref[...] += dlg
        dlb_ref[...] += dlb

        def grad_z(i, dcb):
            base = pl.multiple_of(i * CONV_ROWS, CONV_ROWS)
            ahead = _row_shifts(da1_ref[pl.ds(base, n_rows), :])
            dyc = ahead[0][:CONV_ROWS]
            da0 = w_ref[pl.ds(kw - 1, 1), :] * dyc
            for k in range(kw - 1):
                da0 = da0 + w_ref[pl.ds(k, 1), :] * _shifted_rows(ahead, kw - 1 - k)
            behind = _row_shifts(ext_ref[pl.ds(base, n_rows), :])
            for k in range(kw):
                dw_ref[pl.ds(k, 1), :] += jnp.sum(dyc * _shifted_rows(behind, CONV_PAD - (kw - 1) + k),
                                                  axis=0, keepdims=True)
            zz = z_ref[pl.ds(base, CONV_ROWS), :].astype(F32)
            sg = _sigmoid(zz[:, ch:])
            dz_ref[pl.ds(base, CONV_ROWS), :ch] = (da0 * sg).astype(dz_ref.dtype)
            dz_ref[pl.ds(base, CONV_ROWS), ch:] = (da0 * zz[:, :ch] * sg * (1.0 - sg)).astype(dz_ref.dtype)
            return dcb + jnp.sum(dyc, axis=0, keepdims=True)

        dcb_ref[...] += lax.fori_loop(0, s // CONV_ROWS, grad_z, zero)

    vec = pl.BlockSpec((1, ch), lambda b: (0, 0))
    taps = pl.BlockSpec((kw, ch), lambda b: (0, 0))
    return _call(body, name=name, grid=(bsz,),
                 in_specs=[pl.BlockSpec((s, 2 * ch), lambda b: (b, 0)), pl.BlockSpec((s, ch), lambda b: (b, 0)),
                           taps, vec, vec, vec],
                 out_specs=[pl.BlockSpec((s, 2 * ch), lambda b: (b, 0)), taps, vec, vec, vec],
                 out_shape=[_sds((t, 2 * ch), BF16), _sds((kw, ch), F32)] + [_sds((1, ch), F32)] * 3,
                 scratch=[pltpu.VMEM((CONV_PAD + s, ch), F32), pltpu.VMEM((s + CONV_PAD, ch), F32)],
                 sem=("arbitrary",))(z, da3, w, cb, lg, lb)


FFN_ROWS = 64


def _gelu_parts(v):
    cdf = 0.5 * (1.0 + lax.erf(v * (2.0 ** -0.5)))
    return cdf, v * cdf


def _ffn_conv_piece(ext_ref, wb_ref, base):
    win = ext_ref[pl.ds(base, FFN_ROWS + FFN_PAD), :]
    acc = wb_ref[pl.ds(3, 1), :] + wb_ref[pl.ds(2, 1), :] * win[FFN_PAD:]
    acc = acc + wb_ref[pl.ds(1, 1), :] * pltpu.roll(win, 1, axis=0)[FFN_PAD:]
    acc = acc + wb_ref[pl.ds(0, 1), :] * pltpu.roll(win, 2, axis=0)[FFN_PAD:]
    return acc


def _ffn_stage(hg_ref, hl_ref, wg_ref, wl_ref, bg_ref, bl_ref, ext_ref, wb_ref, s, tcf):
    ext_ref[pl.ds(0, FFN_PAD), :] = jnp.zeros((FFN_PAD, 2 * tcf), F32)
    ext_ref[pl.ds(FFN_PAD, s), :tcf] = hg_ref[...].astype(F32)
    ext_ref[pl.ds(FFN_PAD, s), tcf:] = hl_ref[...].astype(F32)
    wb_ref[pl.ds(0, 3), :tcf] = wg_ref[...]
    wb_ref[pl.ds(0, 3), tcf:] = wl_ref[...]
    wb_ref[pl.ds(3, 1), :tcf] = bg_ref[...]
    wb_ref[pl.ds(3, 1), tcf:] = bl_ref[...]


def _ffn_specs(s, tcf, n_f, batch_first):
    def spec(rows, shift):
        if batch_first:
            return pl.BlockSpec((rows, tcf), lambda bb, j: (bb if rows == s else 0, shift + j))
        return pl.BlockSpec((rows, tcf), lambda j, bb: (bb if rows == s else 0, shift + j))

    return [spec(s, 0), spec(s, n_f), spec(3, 0), spec(3, n_f), spec(1, 0), spec(1, n_f)]


def _ffn_act_fwd(hp, w, b, bsz, tcf, *, name):
    t, two_f = hp.shape
    s = t // bsz
    n_f = two_f // (2 * tcf)

    def body(hg_ref, hl_ref, wg_ref, wl_ref, bg_ref, bl_ref, f_ref, ext_ref, wb_ref):
        _ffn_stage(hg_ref, hl_ref, wg_ref, wl_ref, bg_ref, bl_ref, ext_ref, wb_ref, s, tcf)

        def step(i, carry):
            base = pl.multiple_of(i * FFN_ROWS, FFN_ROWS)
            hh = _ffn_conv_piece(ext_ref, wb_ref, base)
            _, gelu = _gelu_parts(hh[:, :tcf])
            f_ref[pl.ds(base, FFN_ROWS), :] = (gelu * hh[:, tcf:]).astype(f_ref.dtype)
            return carry

        lax.fori_loop(0, s // FFN_ROWS, step, 0)

    return _call(body, name=name, grid=(bsz, n_f), in_specs=_ffn_specs(s, tcf, n_f, True),
                 out_specs=pl.BlockSpec((s, tcf), lambda bb, j: (bb, j)),
                 out_shape=_sds((t, two_f // 2), BF16),
                 scratch=[pltpu.VMEM((FFN_PAD + s, 2 * tcf), F32), pltpu.VMEM((SUBLANES, 2 * tcf), F32)],
                 sem=("parallel", "parallel"))(hp, hp, w, w, b, b)


def _ffn_act_bwd(hp, df, w, b, bsz, tcf, *, name):
    t, two_f = hp.shape
    s = t // bsz
    f_dim = two_f // 2
    n_f = f_dim // tcf
    gw = 2 * tcf
    n_rows = FFN_ROWS + FFN_PAD

    def body(hg_ref, hl_ref, wg_ref, wl_ref, bg_ref, bl_ref, df_ref,
             dhg_ref, dhl_ref, dwg_ref, dwl_ref, dbg_ref, dbl_ref, ext_ref, wb_ref, dh_ref):
        @pl.when(pl.program_id(1) == 0)
        def _():
            for ref in (dwg_ref, dwl_ref, dbg_ref, dbl_ref):
                ref[...] = jnp.zeros_like(ref)

        _ffn_stage(hg_ref, hl_ref, wg_ref, wl_ref, bg_ref, bl_ref, ext_ref, wb_ref, s, tcf)
        dh_ref[pl.ds(s, FFN_PAD), :] = jnp.zeros((FFN_PAD, gw), F32)

        def grad_h(i, carry):
            base = pl.multiple_of(i * FFN_ROWS, FFN_ROWS)
            hh = _ffn_conv_piece(ext_ref, wb_ref, base)
            hg = hh[:, :tcf]
            d = df_ref[pl.ds(base, FFN_ROWS), :].astype(F32)
            cdf, gelu = _gelu_parts(hg)
            pdf = jnp.exp(-0.5 * hg * hg) * (1.0 / math.sqrt(2.0 * math.pi))
            dh_ref[pl.ds(base, FFN_ROWS), :tcf] = d * hh[:, tcf:] * (cdf + hg * pdf)
            dh_ref[pl.ds(base, FFN_ROWS), tcf:] = d * gelu
            return carry

        lax.fori_loop(0, s // FFN_ROWS, grad_h, 0)

        def grad_x(i, carry):
            dw0, dw1, dw2, dbs = carry
            base = pl.multiple_of(i * FFN_ROWS, FFN_ROWS)
            nxt = dh_ref[pl.ds(base, n_rows), :]
            dyc = nxt[:FFN_ROWS]
            dx = wb_ref[pl.ds(2, 1), :] * dyc
            dx = dx + wb_ref[pl.ds(1, 1), :] * pltpu.roll(nxt, n_rows - 1, axis=0)[:FFN_ROWS]
            dx = dx + wb_ref[pl.ds(0, 1), :] * pltpu.roll(nxt, n_rows - 2, axis=0)[:FFN_ROWS]
            dhg_ref[pl.ds(base, FFN_ROWS), :] = dx[:, :tcf].astype(dhg_ref.dtype)
            dhl_ref[pl.ds(base, FFN_ROWS), :] = dx[:, tcf:].astype(dhl_ref.dtype)
            win = ext_ref[pl.ds(base, n_rows), :]
            dw2 = dw2 + jnp.sum(dyc * win[FFN_PAD:], axis=0, keepdims=True)
            dw1 = dw1 + jnp.sum(dyc * pltpu.roll(win, 1, axis=0)[FFN_PAD:], axis=0, keepdims=True)
            dw0 = dw0 + jnp.sum(dyc * pltpu.roll(win, 2, axis=0)[FFN_PAD:], axis=0, keepdims=True)
            return dw0, dw1, dw2, dbs + jnp.sum(dyc, axis=0, keepdims=True)

        zero = jnp.zeros((1, gw), F32)
        sums = lax.fori_loop(0, s // FFN_ROWS, grad_x, (zero, zero, zero, zero))
        for k in range(3):
            dwg_ref[pl.ds(k, 1), :] += sums[k][:, :tcf]
            dwl_ref[pl.ds(k, 1), :] += sums[k][:, tcf:]
        dbg_ref[...] += sums[3][:, :tcf]
        dbl_ref[...] += sums[3][:, tcf:]

    half = pl.BlockSpec((s, tcf), lambda j, bb: (bb, j))
    taps = pl.BlockSpec((3, tcf), lambda j, bb: (0, j))
    bias = pl.BlockSpec((1, tcf), lambda j, bb: (0, j))
    return _call(body, name=name, grid=(n_f, bsz), in_specs=_ffn_specs(s, tcf, n_f, False) + [half],
                 out_specs=[half, half, taps, taps, bias, bias],
                 out_shape=[_sds((t, f_dim), BF16)] * 2 + [_sds((3, f_dim), F32)] * 2 + [_sds((1, f_dim), F32)] * 2,
                 scratch=[pltpu.VMEM((FFN_PAD + s, gw), F32), pltpu.VMEM((SUBLANES, gw), F32),
                          pltpu.VMEM((s + FFN_PAD, gw), F32)],
                 sem=("parallel", "arbitrary"))(hp, hp, w, w, b, b, df)


def _split3(v):
    hi = v.astype(BF16)
    r = v - hi.astype(F32)
    mid = r.astype(BF16)
    lo = (r - mid.astype(F32)).astype(BF16)
    return hi, mid, lo


def _tri_dot(tri, v):
    out = None
    for part in _split3(v):
        term = jnp.dot(tri, part, preferred_element_type=F32)
        out = term if out is None else out + term
    return out


def _fgate_fwd(zf, bsz, heads, *, name):
    t, lanes = zf.shape
    s, blk, nb = _seq_tiles(t, bsz, (ATTN_BLOCK, 128))

    def body(z_ref, cumt_ref, cumb_ref, carry_ref):
        @pl.when(pl.program_id(1) == 0)
        def _():
            carry_ref[...] = jnp.zeros_like(carry_ref)

        z = z_ref[...]
        lf = jnp.minimum(z, 0.0) - jnp.log1p(jnp.exp(-jnp.abs(z)))
        r = lax.broadcasted_iota(jnp.int32, (blk, blk), 0)
        c = lax.broadcasted_iota(jnp.int32, (blk, blk), 1)
        tri = (r >= c).astype(BF16)
        cum = _tri_dot(tri, lf) + carry_ref[...]
        carry_ref[...] = cum[blk - 1:blk, :]
        cumt_ref[0] = jnp.transpose(cum)[:heads, :]
        for h in range(heads):
            cumb_ref[0, h] = jnp.broadcast_to(cum[:, h:h + 1], (blk, lanes))

    return _call(body, name=name, grid=(bsz, nb),
                 in_specs=[pl.BlockSpec((blk, lanes), lambda b, i: (b * nb + i, 0))],
                 out_specs=[pl.BlockSpec((1, heads, blk), lambda b, i: (b, 0, i)),
                            pl.BlockSpec((1, heads, blk, lanes), lambda b, i: (b, 0, i, 0))],
                 out_shape=[_sds((bsz, heads, s), F32), _sds((bsz, heads, s, lanes), F32)],
                 scratch=[pltpu.VMEM((1, lanes), F32)], sem=("parallel", "arbitrary"))(zf)


def _fgate_bwd(dcum, zf, bsz, *, name):
    t, lanes = zf.shape
    pairs = dcum.shape[1]
    s, blk, nb = _seq_tiles(t, bsz, (ATTN_BLOCK, 128))

    def body(d_ref, z_ref, o_ref, carry_ref):
        @pl.when(pl.program_id(1) == 0)
        def _():
            carry_ref[...] = jnp.zeros_like(carry_ref)

        dcol = d_ref[0, 0]
        for p in range(1, pairs):
            dcol = dcol + d_ref[0, p]
        r = lax.broadcasted_iota(jnp.int32, (blk, blk), 0)
        c = lax.broadcasted_iota(jnp.int32, (blk, blk), 1)
        tri = (c >= r).astype(BF16)
        suf = _tri_dot(tri, dcol) + carry_ref[...]
        carry_ref[...] = suf[0:1, :]
        o_ref[...] = suf * _sigmoid(-z_ref[...])

    return _call(body, name=name, grid=(bsz, nb),
                 in_specs=[pl.BlockSpec((1, pairs, blk, lanes), lambda b, i: (b, 0, nb - 1 - i, 0)),
                           pl.BlockSpec((blk, lanes), lambda b, i: (b * nb + nb - 1 - i, 0))],
                 out_specs=pl.BlockSpec((blk, lanes), lambda b, i: (b * nb + nb - 1 - i, 0)),
                 out_shape=_sds((t, lanes), F32), scratch=[pltpu.VMEM((1, lanes), F32)],
                 sem=("parallel", "arbitrary"))(dcum, zf)


def _to_features_major(z, col_off, width, n, *, name):
    t = z.shape[0]
    tr = _pick(t, (512, 256, 128))
    first = col_off // width

    def body(*refs):
        o_ref = refs[n]
        for g in range(n):
            o_ref[pl.ds(g * width, width), :] = jnp.transpose(refs[g][...].astype(F32)).astype(o_ref.dtype)

    return _call(body, name=name, grid=(t // tr,),
                 in_specs=[pl.BlockSpec((tr, width), lambda i, g=g: (i, first + g)) for g in range(n)],
                 out_specs=pl.BlockSpec((n * width, tr), lambda i: (0, i)),
                 out_shape=_sds((n * width, t), BF16), sem=("parallel",))(*([z] * n))


def _to_rows_major(xt, *, name):
    w, t = xt.shape
    tr = _pick(t, (512, 256, 128))

    def body(x_ref, o_ref):
        o_ref[...] = jnp.transpose(x_ref[...]).astype(o_ref.dtype)

    return _call(body, name=name, grid=(t // tr,),
                 in_specs=[pl.BlockSpec((w, tr), lambda i: (0, i))],
                 out_specs=pl.BlockSpec((tr, w), lambda i: (i, 0)),
                 out_shape=_sds((t, w), BF16), sem=("parallel",))(xt)


def _loop_by_twos(lo, hi, body, carry):
    count = hi - lo

    def group(n, first, cr):
        for u in range(n):
            cr = body(first + u, cr)
        return cr

    trips = count // ATTN_UNROLL
    carry = lax.fori_loop(0, trips, lambda t, cr: group(ATTN_UNROLL, lo + ATTN_UNROLL * t, cr), carry)
    rest = count - ATTN_UNROLL * trips
    first = lo + ATTN_UNROLL * trips
    for n in range(ATTN_UNROLL - 1, 0, -1):
        carry = lax.cond(rest == n, lambda cr, n=n: group(n, first, cr), lambda cr: cr, carry)
    return carry


def _head_masks(shape, axis):
    feat = lax.broadcasted_iota(jnp.int32, shape, axis)
    return feat < HEAD_DIM, feat >= HEAD_DIM


def _attn_fwd(z, qkvt, cumt, cumb, bsz, heads, q_off, *, name):
    t = z.shape[0]
    width = heads * HEAD_DIM
    pairs = heads // 2
    s = t // bsz
    blk = ATTN_BLOCK
    nq = s // blk
    k_col = (q_off + width) // LANES
    v_row = 2 * width // LANES
    reps = blk // LANES

    def body(k_ref, qt_ref, vt_ref, cqt_ref, ckb_ref, ot_ref, lse_ref):
        p_id = pl.program_id(1)
        i = pl.program_id(2)
        qt = qt_ref[...]
        masks = _head_masks((LANES, blk), 0)
        qtm = [jnp.where(mk, qt, jnp.zeros_like(qt)) for mk in masks]
        cq = [cqt_ref[0, pl.ds(2 * p_id + hh, 1), :] for hh in range(2)]
        kidx = lax.broadcasted_iota(jnp.int32, (blk, blk), 0)
        qidx = lax.broadcasted_iota(jnp.int32, (blk, blk), 1)

        def block(j, carry, masked):
            off = pl.multiple_of(j * blk, blk)
            kp = k_ref[pl.ds(off, blk), :].astype(BF16)
            vtp = vt_ref[:, pl.ds(off, blk)]
            out = []
            for hh in range(2):
                m, l, acc = carry[hh]
                sc = jnp.dot(kp, qtm[hh], preferred_element_type=F32) * ATTN_SCALE
                ck = ckb_ref[0, hh, pl.ds(off, blk), :]
                sc = (sc + cq[hh]) - jnp.concatenate([ck] * reps, axis=1)
                if masked:
                    sc = jnp.where(qidx >= kidx, sc, NEG)
                m_new = jnp.maximum(m, jnp.max(sc, axis=0, keepdims=True))
                pr = jnp.exp(sc - m_new)
                a = jnp.exp(m - m_new)
                l = a * l + jnp.sum(pr, axis=0, keepdims=True)
                p_hi = pr.astype(BF16)
                p_lo = (pr - p_hi.astype(F32)).astype(BF16)
                pv = (jnp.dot(vtp, p_hi, preferred_element_type=F32)
                      + jnp.dot(vtp, p_lo, preferred_element_type=F32))
                acc = a * acc + pv[hh * HEAD_DIM:(hh + 1) * HEAD_DIM]
                out.append((m_new, l, acc))
            return tuple(out)

        init = tuple((jnp.full((1, blk), NEG, F32), jnp.zeros((1, blk), F32), jnp.zeros((HEAD_DIM, blk), F32))
                     for _ in range(2))
        carry = _loop_by_twos(0, i, lambda j, cr: block(j, cr, False), init)
        carry = block(i, carry, True)
        lse_ref[...] = jnp.zeros_like(lse_ref)
        for hh in range(2):
            m, l, acc = carry[hh]
            ot_ref[pl.ds(hh * HEAD_DIM, HEAD_DIM), :] = acc / l
            lse_ref[0, 0, pl.ds(hh, 1), :] = m + jnp.log(l)

    return _call(body, name=name, grid=(bsz, pairs, nq),
                 in_specs=[pl.BlockSpec((s, LANES), lambda b, p, i: (b, k_col + p)),
                           pl.BlockSpec((LANES, blk), lambda b, p, i: (p, b * nq + i)),
                           pl.BlockSpec((LANES, s), lambda b, p, i: (v_row + p, b)),
                           pl.BlockSpec((1, heads, blk), lambda b, p, i: (b, 0, i)),
                           pl.BlockSpec((1, 2, s, LANES), lambda b, p, i: (b, p, 0, 0))],
                 out_specs=[pl.BlockSpec((LANES, blk), lambda b, p, i: (p, b * nq + i)),
                            pl.BlockSpec((1, 1, SUBLANES, blk), lambda b, p, i: (b, p, 0, i))],
                 out_shape=[_sds((width, t), F32), _sds((bsz, pairs, SUBLANES, s), F32)],
                 sem=("parallel", "parallel", "parallel"))(z, qkvt, qkvt, cumt, cumb)


def _attn_bwd(z, qkvt, cumt, cumb, ot, do, dot, lse, bsz, heads, q_off, *, name):
    t = z.shape[0]
    width = heads * HEAD_DIM
    pairs = heads // 2
    s = t // bsz
    blk = ATTN_BLOCK
    nkv = s // blk
    q_col = q_off // LANES
    k_col = (q_off + width) // LANES
    v_col = (q_off + 2 * width) // LANES
    k_row = width // LANES
    reps = blk // LANES

    def body(k_ref, v_ref, kt_ref, q_ref, qt_ref, do_ref, dot_ref, ot_ref, lse_ref, ckb_ref, cqt_ref,
             dk_ref, dv_ref, dqt_ref, dcum_ref, dqt_acc, ds_acc):
        p_id = pl.program_id(1)
        j = pl.program_id(2)

        @pl.when(j == 0)
        def _():
            dqt_acc[...] = jnp.zeros_like(dqt_acc)

        kp = k_ref[...].astype(BF16)
        vp = v_ref[...].astype(BF16)
        kt = kt_ref[...]
        feat_masks = _head_masks((LANES, blk), 0)
        lane_masks = _head_masks((blk, LANES), 1)
        ktm = [jnp.where(mk, kt, jnp.zeros_like(kt)) for mk in feat_masks]
        ck = [jnp.concatenate([ckb_ref[0, hh]] * reps, axis=1) for hh in range(2)]
        kidx = lax.broadcasted_iota(jnp.int32, (blk, blk), 0)
        qidx = lax.broadcasted_iota(jnp.int32, (blk, blk), 1)
        ds_acc[...] = jnp.zeros_like(ds_acc)

        def block(i, carry, masked):
            dk, dv = carry
            off = pl.multiple_of(i * blk, blk)
            qt = qt_ref[:, pl.ds(off, blk)]
            dt = dot_ref[:, pl.ds(off, blk)]
            o_t = ot_ref[:, pl.ds(off, blk)]
            q_rows = q_ref[pl.ds(off, blk), :].astype(BF16)
            do_rows = do_ref[pl.ds(off, blk), :]
            for hh in range(2):
                qtm = jnp.where(feat_masks[hh], qt, jnp.zeros_like(qt))
                dtm = jnp.where(feat_masks[hh], dt, jnp.zeros_like(dt))
                sc = jnp.dot(kp, qtm, preferred_element_type=F32) * ATTN_SCALE
                sc = (sc + cqt_ref[0, pl.ds(2 * p_id + hh, 1), pl.ds(off, blk)]) - ck[hh]
                pr = jnp.exp(sc - lse_ref[0, 0, pl.ds(hh, 1), pl.ds(off, blk)])
                if masked:
                    pr = jnp.where(qidx >= kidx, pr, 0.0)
                dp = jnp.dot(vp, dtm, preferred_element_type=F32)
                delta = jnp.sum(dtm.astype(F32) * o_t, axis=0, keepdims=True)
                ds = pr * (dp - delta)
                ds_acc[hh] += ds
                dsb = ds.astype(BF16)
                qm = jnp.where(lane_masks[hh], q_rows, jnp.zeros_like(q_rows))
                dom = jnp.where(lane_masks[hh], do_rows, jnp.zeros_like(do_rows))
                dv = dv + jnp.dot(pr.astype(BF16), dom, preferred_element_type=F32)
                dk = dk + jnp.dot(dsb, qm, preferred_element_type=F32) * ATTN_SCALE
                dqt_acc[:, pl.ds(off, blk)] += jnp.dot(ktm[hh], dsb, preferred_element_type=F32) * ATTN_SCALE
            return dk, dv

        zero = jnp.zeros((blk, LANES), F32)
        carry = block(j, (zero, zero), True)
        dk, dv = _loop_by_twos(j + 1, nkv, lambda i, cr: block(i, cr, False), carry)
        dk_ref[...] = dk.astype(dk_ref.dtype)
        dv_ref[...] = dv.astype(dv_ref.dtype)
        lane = lax.broadcasted_iota(jnp.int32, (blk, LANES), 1)
        dcum = jnp.zeros((blk, LANES), F32)
        for hh in range(2):
            col = jnp.sum(ds_acc[hh], axis=1, keepdims=True)
            dcum = jnp.where(lane == 2 * p_id + hh, -col, dcum)
        dcum_ref[0, 0] = dcum

        @pl.when(j == nkv - 1)
        def _():
            dqt_ref[...] = dqt_acc[...]

    key_rows = lambda col: pl.BlockSpec((blk, LANES), lambda b, p, j: (b * nkv + j, col + p))
    seq_t = lambda row: pl.BlockSpec((LANES, s), lambda b, p, j: (row + p, b))
    return _call(body, name=name, grid=(bsz, pairs, nkv),
                 in_specs=[key_rows(k_col), key_rows(v_col),
                           pl.BlockSpec((LANES, blk), lambda b, p, j: (k_row + p, b * nkv + j)),
                           pl.BlockSpec((s, LANES), lambda b, p, j: (b, q_col + p)), seq_t(0),
                           pl.BlockSpec((s, LANES), lambda b, p, j: (b, p)), seq_t(0), seq_t(0),
                           pl.BlockSpec((1, 1, SUBLANES, s), lambda b, p, j: (b, p, 0, 0)),
                           pl.BlockSpec((1, 2, blk, LANES), lambda b, p, j: (b, p, j, 0)),
                           pl.BlockSpec((1, heads, s), lambda b, p, j: (b, 0, 0))],
                 out_specs=[key_rows(0), key_rows(0), seq_t(0),
                            pl.BlockSpec((1, 1, blk, LANES), lambda b, p, j: (b, p, j, 0))],
                 out_shape=[_sds((t, width), BF16), _sds((t, width), BF16), _sds((width, t), F32),
                            _sds((bsz, pairs, s, LANES), F32)],
                 scratch=[pltpu.VMEM((LANES, s), F32), pltpu.VMEM((2, blk, blk), F32)],
                 sem=("parallel", "parallel", "arbitrary"))(z, z, qkvt, z, qkvt, do, dot, ot, lse, cumb, cumt)


def _adamw(w, g, m, v, *, name):
    bc1 = 1.0 - ADAM_B1 ** ADAM_STEP
    bc2 = 1.0 - ADAM_B2 ** ADAM_STEP

    def body(w_ref, g_ref, m_ref, v_ref, d_ref, nm_ref, nv_ref):
        g_v = g_ref[...]
        nm = ADAM_B1 * m_ref[...] + (1.0 - ADAM_B1) * g_v
        nv = ADAM_B2 * v_ref[...] + (1.0 - ADAM_B2) * (g_v * g_v)
        nm_ref[...] = nm
        nv_ref[...] = nv
        d_ref[...] = -ADAM_LR * ((nm / bc1) / (jnp.sqrt(nv / bc2) + ADAM_EPS) + ADAM_WD * w_ref[...])

    if w.ndim == 2:
        grid = (1,)
        blk = pl.BlockSpec(w.shape, lambda i: (0, 0))
    else:
        layers, rows, cols = w.shape
        tr = rows if rows <= 256 else _pick(rows, (256, 128, 64, 32, 16, 8))
        grid = (layers, rows // tr)
        blk = pl.BlockSpec((1, tr, cols), lambda layer, i: (layer, i, 0))
    return tuple(_call(body, name=name, grid=grid, in_specs=[blk] * 4, out_specs=[blk] * 3,
                       out_shape=[_sds(w.shape, F32)] * 3, sem=("parallel",) * len(grid))(w, g, m, v))


_ANY = pl.BlockSpec(memory_space=pl.ANY)


def _comm_call(body, *, name, n_in, out_shape, n_sems):
    scratch = [pltpu.SemaphoreType.DMA((n_sems,)), pltpu.SemaphoreType.DMA((n_sems,)),
               pltpu.SemaphoreType.DMA((len(out_shape),))]
    return pl.pallas_call(body, name=name, in_specs=[_ANY] * n_in, out_specs=[_ANY] * len(out_shape),
                          out_shape=out_shape, scratch_shapes=scratch)


def _place():
    x, y, c = lax.axis_index("x"), lax.axis_index("y"), lax.axis_index("c")
    return x, y, c, [(1 - x, y), (x, 1 - y), (1 - x, 1 - y)]


def _remote(src, dst, send_sems, recv_sems, sem, to):
    return pltpu.make_async_remote_copy(src_ref=src, dst_ref=dst, send_sem=send_sems.at[sem],
                                        recv_sem=recv_sems.at[sem], device_id=to, device_id_type=MESH)


def _all_gather8(v, *, name):
    def body(v_ref, out_ref, send_sems, recv_sems, local_sems):
        x, y, c, _ = _place()
        me = 4 * x + 2 * y + c
        mine = pltpu.make_async_copy(v_ref, out_ref.at[me], local_sems.at[0])
        mine.start()
        peers = []
        for k in range(1, N_DEVICES):
            px = 1 - x if k & 4 else x
            py = 1 - y if k & 2 else y
            pc = 1 - c if k & 1 else c
            peers.append((px, py, pc))
        sends = [_remote(v_ref, out_ref.at[me], send_sems, recv_sems, k, peer) for k, peer in enumerate(peers)]
        for cp in sends:
            cp.start()
        for k, (px, py, pc) in enumerate(peers):
            _remote(v_ref, out_ref.at[4 * px + 2 * py + pc], send_sems, recv_sems, k, (px, py, pc)).wait_recv()
        for cp in sends:
            cp.wait_send()
        mine.wait()

    out = _comm_call(body, name=name, n_in=1, out_shape=[_sds((N_DEVICES,) + v.shape, v.dtype)],
                     n_sems=N_DEVICES - 1)(v)
    return out[0]


def _window(ref, mode, layer, chip, rows, cols, half=None):
    first, count = (0, rows) if half is None else (half * (rows // 2), rows // 2)
    if mode == "slab":
        return ref.at[layer, chip] if half is None else ref.at[layer, chip, pl.ds(first, count), :]
    if mode == "cols":
        col_window = pl.ds(pl.multiple_of(chip * cols, LANES), cols)
        return ref.at[layer, :, col_window] if half is None else ref.at[layer, pl.ds(first, count), col_window]
    return ref.at[layer, pl.ds(pl.multiple_of(chip * rows + first, SUBLANES), count), :]


def _whole_shape(mode, shard_shape):
    layers, rows, cols = shard_shape
    if mode == "slab":
        return (layers, N_CHIPS, rows, cols)
    if mode == "cols":
        assert cols % LANES == 0
        return (layers, rows, N_CHIPS * cols)
    assert rows % 16 == 0
    return (layers, N_CHIPS * rows, cols)


def _gather_weights(shards, modes, *, name):
    n = len(shards)
    meta = [(mode,) + tuple(a.shape[1:]) for a, mode in zip(shards, modes)]
    for a in shards:
        assert a.shape[0] == 2 and a.shape[1] % 2 == 0
    per = 8

    def body(*refs):
        ins, outs = refs[:n], refs[n:2 * n]
        send_sems, recv_sems, _ = refs[2 * n:]
        x, y, c, _ = _place()
        me, x_nbr, y_nbr, diagonal = 2 * x + y, 2 * (1 - x) + y, 2 * x + 1 - y, 2 * (1 - x) + 1 - y
        to_x, to_y, sibling = (1 - x, y, c), (x, 1 - y, c), (x, y, 1 - c)
        sent = []

        def copy(src, dst, sem, to):
            cp = _remote(src, dst, send_sems, recv_sems, sem, to)
            cp.start()
            sent.append(cp)

        def arrived(win, sem):
            _remote(win, win, send_sems, recv_sems, sem, sibling).wait_recv()

        for i, (mode, rows, cols) in enumerate(meta):
            mine = _window(outs[i], mode, c, me, rows, cols)
            copy(ins[i].at[c], mine, per * i, to_x)
            copy(ins[i].at[c], mine, per * i + 1, to_y)
            copy(ins[i], _window(outs[i], mode, slice(None), me, rows, cols), per * i + 7, sibling)
        for i, (mode, rows, cols) in enumerate(meta):
            arrived(_window(outs[i], mode, c, x_nbr, rows, cols), per * i)
            half = _window(outs[i], mode, c, x_nbr, rows, cols, half=0)
            copy(half, half, per * i + 2, to_y)
            win = _window(outs[i], mode, c, x_nbr, rows, cols)
            copy(win, win, per * i + 4, sibling)
            arrived(_window(outs[i], mode, c, y_nbr, rows, cols), per * i + 1)
            half = _window(outs[i], mode, c, y_nbr, rows, cols, half=1)
            copy(half, half, per * i + 3, to_x)
            win = _window(outs[i], mode, c, y_nbr, rows, cols)
            copy(win, win, per * i + 5, sibling)
        for i, (mode, rows, cols) in enumerate(meta):
            arrived(_window(outs[i], mode, c, diagonal, rows, cols, half=0), per * i + 2)
            arrived(_window(outs[i], mode, c, diagonal, rows, cols, half=1), per * i + 3)
            win = _window(outs[i], mode, c, diagonal, rows, cols)
            copy(win, win, per * i + 6, sibling)
        for i, (mode, rows, cols) in enumerate(meta):
            arrived(_window(outs[i], mode, slice(None), me, rows, cols), per * i + 7)
            for k, chip in enumerate((x_nbr, y_nbr, diagonal)):
                arrived(_window(outs[i], mode, 1 - c, chip, rows, cols), per * i + 4 + k)
        for cp in sent:
            cp.wait_send()

    out_shape = [_sds(_whole_shape(mode, a.shape), a.dtype) for a, mode in zip(shards, modes)]
    return _comm_call(body, name=name, n_in=n, out_shape=out_shape, n_sems=per * n)(*shards)


def _rs_swap(grads, *, name):
    n = len(grads)

    def body(*refs):
        ins, outs = refs[:n], refs[n:2 * n]
        send_sems, recv_sems, _ = refs[2 * n:]
        x, y, c, _ = _place()
        copies = [_remote(ins[i].at[1 - c], outs[i], send_sems, recv_sems, i, (x, y, 1 - c)) for i in range(n)]
        for cp in copies:
            cp.start()
        for cp in copies:
            cp.wait()

    return _comm_call(body, name=name, n_in=n, out_shape=[_sds(g.shape[1:], g.dtype) for g in grads], n_sems=n)(*grads)


def _part(ref, mode, chip, rows, cols):
    if mode == "slab":
        return ref.at[chip]
    if mode == "cols":
        return ref.at[:, pl.ds(pl.multiple_of(chip * cols, LANES), cols)]
    return ref.at[pl.ds(pl.multiple_of(chip * rows, SUBLANES), rows), :]


def _rs_scatter(parts, modes, shard_shapes, *, name):
    n = len(parts)
    meta = [(mode,) + tuple(shp[1:]) for mode, shp in zip(modes, shard_shapes)]

    def body(*refs):
        ins, outs = refs[:n], refs[n:2 * n]
        send_sems, recv_sems, local_sems = refs[2 * n:]
        x, y, c, chips = _place()
        me = 2 * x + y
        local, sends = [], []
        for i, (mode, rows, cols) in enumerate(meta):
            cp = pltpu.make_async_copy(_part(ins[i], mode, me, rows, cols), outs[i].at[me], local_sems.at[i])
            cp.start()
            local.append(cp)
            for r, (cx, cy) in enumerate(chips):
                cp = _remote(_part(ins[i], mode, 2 * cx + cy, rows, cols), outs[i].at[me], send_sems, recv_sems,
                             3 * i + r, (cx, cy, c))
                cp.start()
                sends.append(cp)
        for i, (mode, rows, cols) in enumerate(meta):
            for r, (cx, cy) in enumerate(chips):
                k = 2 * cx + cy
                _remote(_part(ins[i], mode, k, rows, cols), outs[i].at[k], send_sems, recv_sems, 3 * i + r,
                        (cx, cy, c)).wait_recv()
        for cp in sends:
            cp.wait_send()
        for cp in local:
            cp.wait()

    out_shape = [_sds((N_CHIPS,) + tuple(shp[1:]), p.dtype) for p, shp in zip(parts, shard_shapes)]
    return _comm_call(body, name=name, n_in=n, out_shape=out_shape, n_sems=3 * n)(*parts)


def _rs_exchange(sums, *, name):
    n = len(sums)

    def body(*refs):
        ins, outs = refs[:n], refs[n:2 * n]
        send_sems, recv_sems, _ = refs[2 * n:]
        x, y, c, _ = _place()
        copies = [_remote(ins[i], outs[i], send_sems, recv_sems, i, (x, y, 1 - c)) for i in range(n)]
        for cp in copies:
            cp.start()
        for cp in copies:
            cp.wait()

    return _comm_call(body, name=name, n_in=n, out_shape=[_sds(s.shape, s.dtype) for s in sums], n_sems=n)(*sums)


def _row_tile(rows, cols, itemsize):
    target = max(SUBLANES, (2 << 20) // (cols * itemsize))
    cands = [c for c in (2048, 1024, 512, 256, 128, 64, 32, 16) if c <= target]
    tr = _pick(rows, cands)
    return tr


def _add_layer(g, other, core, *, name):
    _, rows, cols = g.shape
    tr = _row_tile(rows, cols, 4)

    def body(core_ref, g_ref, o_ref, out_ref):
        out_ref[...] = (g_ref[0] + o_ref[...]).astype(out_ref.dtype)

    grid_spec = pltpu.PrefetchScalarGridSpec(
        num_scalar_prefetch=1, grid=(rows // tr,),
        in_specs=[pl.BlockSpec((1, tr, cols), lambda i, core_ref: (core_ref[0], i, 0)),
                  pl.BlockSpec((tr, cols), lambda i, core_ref: (i, 0))],
        out_specs=pl.BlockSpec((tr, cols), lambda i, core_ref: (i, 0)))
    return pl.pallas_call(body, name=name, grid_spec=grid_spec, out_shape=_sds((rows, cols), BF16),
                          compiler_params=pltpu.CompilerParams(dimension_semantics=("parallel",),
                                                               vmem_limit_bytes=VMEM_LIMIT))(core, g, other)


def _sum_slots(parts, *, name):
    n, rows, cols = parts.shape
    tr = _row_tile(rows, cols, 4)

    def body(p_ref, o_ref):
        acc = p_ref[0].astype(F32) + p_ref[1].astype(F32)
        for k in range(2, n):
            acc = acc + p_ref[k].astype(F32)
        o_ref[...] = acc

    return _call(body, name=name, grid=(rows // tr,),
                 in_specs=[pl.BlockSpec((n, tr, cols), lambda i: (0, i, 0))],
                 out_specs=pl.BlockSpec((tr, cols), lambda i: (i, 0)),
                 out_shape=_sds((rows, cols), F32), sem=("parallel",))(parts)


def _reduce_scatter(grads, modes, shard_shapes):
    core = lax.axis_index("c").astype(jnp.int32).reshape(1)
    flat = [g.reshape(g.shape[0], -1, g.shape[-1]) for g in grads]
    from_sibling = _rs_swap(flat, name="rs_swap")
    parts = []
    for i, (g, o) in enumerate(zip(flat, from_sibling)):
        p = _add_layer(g, o, core, name=f"rs_add_{i}")
        parts.append(p.reshape(grads[i].shape[1:]))
    from_chips = _rs_scatter(parts, modes, shard_shapes, name="rs_scatter")
    sums = [_sum_slots(r, name=f"rs_sum_{i}") for i, r in enumerate(from_chips)]
    others = _rs_exchange(sums, name="rs_exchange")
    mine_first = lax.axis_index("c") == 0
    return [jnp.where(mine_first, jnp.stack([mine, other]), jnp.stack([other, mine]))
            for mine, other in zip(sums, others)]


def _layer_weights(full, rep, layer, dims):
    f_off, n_heads = dims["f_off"], dims["heads"]
    b_in = rep["b_in"][layer]
    pad = LANES - n_heads
    return {
        "w_main": full["w_main"][layer],
        "b_main": jnp.concatenate([b_in[:f_off], b_in[f_off + n_heads:]])[None],
        "w_f": full["w_f"][layer],
        "b_f": jnp.pad(b_in[f_off:f_off + n_heads], (0, pad))[None],
        "conv_a_w": full["conv_a_w"][layer],
        "conv_a_b": rep["conv_a_b"][layer][None],
        "ln_conv_g": rep["ln_conv_g"][layer][None],
        "ln_conv_b": rep["ln_conv_b"][layer][None],
        "w_conv_proj": full["w_conv_proj"][layer],
        "w_attn_proj": full["w_attn_proj"][layer],
        "w_mix_out": full["w_mix_out"][layer],
        "b_mix_out": rep["b_mix_out"][layer][None],
        "ln1_g": rep["ln1_g"][layer][None],
        "ln1_b": rep["ln1_b"][layer][None],
        "w_ffn_up": full["w_ffn_up"][layer],
        "ffn_conv_w": full["ffn_conv_w"][layer],
        "ffn_conv_b": rep["ffn_conv_b"][layer][None],
        "w_ffn_down": full["w_ffn_down"][layer],
        "ln2_g": rep["ln2_g"][layer][None],
        "ln2_b": rep["ln2_b"][layer][None],
    }


def _layer_fwd(x, mod, p, dims, tag, target=None):
    bsz, d, ch, heads, alpha = dims["bsz"], dims["d"], dims["ch"], dims["heads"], dims["alpha"]
    mods = [mod[:, k * d:(k + 1) * d][:, None, :] for k in range(6)]
    shift1, scale1, gate1, shift2, scale2, gate2 = mods
    u = _ln_mod_fwd(x, scale1, shift1, bsz, name=f"ln_mod1_{tag}")
    zm = _matmul(u, p["w_main"], "nn", BF16, bias=p["b_main"], name=f"in_main_{tag}")
    zf = _matmul(u, p["w_f"], "nn", F32, bias=p["b_f"], name=f"in_forget_{tag}")
    a3 = _conv_branch_fwd(zm, p["conv_a_w"], p["conv_a_b"], p["ln_conv_g"], p["ln_conv_b"], bsz, ch,
                          name=f"conv_branch_{tag}")
    ya = _matmul(a3, p["w_conv_proj"], "nn", BF16, name=f"conv_proj_{tag}")
    cumt, cumb = _fgate_fwd(zf, bsz, heads, name=f"fgate_{tag}")
    qkvt = _to_features_major(zm, 2 * ch, heads * HEAD_DIM, 3, name=f"qkv_t_{tag}")
    ot, lse = _attn_fwd(zm, qkvt, cumt, cumb, bsz, heads, 2 * ch, name=f"attn_{tag}")
    yb = _matmul(ot, p["w_attn_proj"], "tn", BF16, name=f"attn_proj_{tag}")
    m = _gate_merge_fwd(zm, ya, yb, dims["ga_off"], name=f"merge_{tag}")
    mix = _matmul(m, p["w_mix_out"], "nn", F32, bias=p["b_mix_out"], name=f"mix_out_{tag}")
    x1, u2 = _ln_res_mod_fwd(x, mix, gate1, p["ln1_g"], p["ln1_b"], scale2, shift2, alpha, bsz,
                             name=f"ln_res1_mod2_{tag}")
    hp = _matmul(u2, p["w_ffn_up"], "nn", BF16, name=f"ffn_up_{tag}")
    f = _ffn_act_fwd(hp, p["ffn_conv_w"], p["ffn_conv_b"], bsz, dims["tcf"], name=f"ffn_act_{tag}")
    ffn = _matmul(f, p["w_ffn_down"], "nn", F32, name=f"ffn_down_{tag}")
    if target is None:
        x2 = _ln_res_fwd(x1, ffn, gate2, p["ln2_g"], p["ln2_b"], alpha, bsz, name=f"ln_res2_{tag}")
    else:
        x2 = _ln_res_loss(x1, ffn, gate2, p["ln2_g"], p["ln2_b"], target, alpha, bsz, name=f"ln_res2_loss_{tag}")
    saved = dict(x=x, mods=mods, u=u, zm=zm, zf=zf, a3=a3, ya=ya, yb=yb, cumt=cumt, cumb=cumb,
                 qkvt=qkvt, ot=ot, lse=lse, m=m, mix=mix, x1=x1, u2=u2, hp=hp, f=f, ffn=ffn)
    return x2, saved


def _layer_bwd(dx2, p, sv, dims, tag):
    bsz, ch, heads, alpha = dims["bsz"], dims["ch"], dims["heads"], dims["alpha"]
    f_off, tcf = dims["f_off"], dims["tcf"]
    shift1, scale1, gate1, shift2, scale2, gate2 = sv["mods"]
    g = {}
    dr2, dffn, dgate2, g["ln2_g"], g["ln2_b"], _ = _ln_res_bwd(
        dx2, sv["x1"], sv["ffn"], gate2, p["ln2_g"], alpha, bsz, name=f"ln_res2_bwd_{tag}")
    df = _matmul(dffn, p["w_ffn_down"], "nt", BF16, name=f"ffn_down_dx_{tag}")
    g["w_ffn_down"] = _matmul(sv["f"], dffn, "tn", F32, name=f"ffn_down_dw_{tag}")
    dhg, dhl, dwg, dwl, dbg, dbl = _ffn_act_bwd(sv["hp"], df, p["ffn_conv_w"], p["ffn_conv_b"], bsz, tcf,
                                                name=f"ffn_act_bwd_{tag}")
    g["ffn_conv_w"] = jnp.concatenate([dwg, dwl], axis=1)
    g["ffn_conv_b"] = jnp.concatenate([dbg, dbl], axis=1)[0]
    du2 = _matmul(dhg, p["w_ffn_up"], "nt", F32, name=f"ffn_up_gate_dx_{tag}")
    du2 = _matmul(dhl, p["w_ffn_up"], "nt", F32, add=du2, b_k_first=dhg.shape[1], name=f"ffn_up_lin_dx_{tag}")
    g["w_ffn_up"] = jnp.concatenate([_matmul(sv["u2"], dhg, "tn", F32, name=f"ffn_up_gate_dw_{tag}"),
                                     _matmul(sv["u2"], dhl, "tn", F32, name=f"ffn_up_lin_dw_{tag}")], axis=1)
    dr1, dmix, dgate1, g["ln1_g"], g["ln1_b"], g["b_mix_out"], dscale2, dshift2 = _ln_mod_res_bwd(
        du2, dr2, scale2, sv["x"], sv["mix"], gate1, p["ln1_g"], p["ln1_b"], alpha, bsz,
        name=f"ln_mod2_res1_bwd_{tag}")
    dm = _matmul(dmix, p["w_mix_out"], "nt", BF16, name=f"mix_out_dx_{tag}")
    g["w_mix_out"] = _matmul(sv["m"], dmix, "tn", F32, name=f"mix_out_dw_{tag}")
    dya, dyb, dzga, dzgb = _gate_merge_bwd(sv["zm"], sv["ya"], sv["yb"], dm, dims["ga_off"], name=f"merge_bwd_{tag}")
    da3 = _matmul(dya, p["w_conv_proj"], "nt", F32, name=f"conv_proj_dx_{tag}")
    g["w_conv_proj"] = _matmul(sv["a3"], dya, "tn", F32, name=f"conv_proj_dw_{tag}")
    do = _matmul(dyb, p["w_attn_proj"], "nt", BF16, name=f"attn_proj_dx_{tag}")
    dot = _matmul(p["w_attn_proj"], dyb, "nt", BF16, name=f"attn_proj_dxt_{tag}")
    g["w_attn_proj"] = _matmul(sv["ot"], dyb, "nn", F32, name=f"attn_proj_dw_{tag}")
    dzglu, g["conv_a_w"], dcb, g["ln_conv_g"], g["ln_conv_b"] = _conv_branch_bwd(
        sv["zm"], da3, p["conv_a_w"], p["conv_a_b"], p["ln_conv_g"], p["ln_conv_b"], bsz, ch,
        name=f"conv_branch_bwd_{tag}")
    g["conv_a_b"] = dcb[0]
    dk, dv, dqt, dcum = _attn_bwd(sv["zm"], sv["qkvt"], sv["cumt"], sv["cumb"], sv["ot"], do, dot, sv["lse"], bsz,
                                  heads, 2 * ch, name=f"attn_bwd_{tag}")
    dq = _to_rows_major(dqt, name=f"dq_rows_{tag}")
    dzf = _fgate_bwd(dcum, sv["zf"], bsz, name=f"fgate_bwd_{tag}")
    dzm = jnp.concatenate([dzglu, dq, dk, dv, dzga, dzgb], axis=1)
    du = _matmul(dzm, p["w_main"], "nt", F32, name=f"in_main_dx_{tag}")
    du = _matmul(dzf, p["w_f"], "nt", F32, add=du, name=f"in_forget_dx_{tag}")
    dwm, dbm = _matmul(sv["u"], dzm, "tn", F32, colsum=True, name=f"in_main_dw_{tag}")
    dwf, dbf = _matmul(sv["u"], dzf, "tn", F32, colsum=True, name=f"in_forget_dw_{tag}")
    dbm, dbf = dbm[0], dbf[0]
    g["w_main"], g["w_f"] = dwm, dwf
    g["b_in"] = jnp.concatenate([dbm[:f_off], dbf[:heads], dbm[f_off:]])
    dx, dscale1, dshift1 = _ln_mod_bwd(du, sv["x"], scale1, dr1, alpha, bsz, name=f"ln_mod1_bwd_{tag}")
    dmod = jnp.concatenate([dshift1, dscale1, dgate1, dshift2, dscale2, dgate2], axis=2)[:, 0, :]
    return dx, g, dmod


def _local_step(x, mod, loss_target, full, rep, dims):
    bsz, seq, d = x.shape
    layers = mod.shape[0]
    params = [_layer_weights(full, rep, layer, dims) for layer in range(layers)]
    h = x.reshape(bsz * seq, d)
    saved = []
    for layer in range(layers):
        target = loss_target.reshape(bsz * seq, d) if layer == layers - 1 else None
        h, sv = _layer_fwd(h, mod[layer], params[layer], dims, f"l{layer}", target)
        saved.append(sv)
    dh, sq = h
    loss_local = 0.5 * jnp.sum(sq) / d
    grads, dmods = [None] * layers, [None] * layers
    for layer in reversed(range(layers)):
        dh, grads[layer], dmods[layer] = _layer_bwd(dh, params[layer], saved[layer], dims, f"l{layer}")
    per_layer = ("w_main", "w_f")
    stacked = {wname: [grads[layer][wname] for layer in range(layers)] if wname in per_layer
               else jnp.stack([grads[layer][wname] for layer in range(layers)]) for wname in grads[0]}
    return loss_local, dh.reshape(bsz, seq, d), stacked, jnp.stack(dmods)


def _pad_rows(a):
    extra = -a.shape[-2] % (2 * SUBLANES)
    if extra == 0:
        return a
    return jnp.pad(a, [(0, 0)] * (a.ndim - 2) + [(0, extra), (0, 0)])


def _w_in_pieces(n, f_off, heads):
    n_in = N_CHIPS * n
    segments = [(0, f_off, "main", 0), (f_off, f_off + heads, "f", 0), (f_off + heads, n_in, "main", f_off)]
    pieces = []
    for chip in range(N_CHIPS):
        lo, hi = chip * n, (chip + 1) * n
        for a, b, target, t0 in segments:
            s, e = max(lo, a), min(hi, b)
            if s < e:
                pieces.append((chip, s - lo, e - lo, target, t0 + s - a))
    return pieces


def _w_in_from_slabs(slabs, f_off, heads, *, name):
    layers, _, k, n = slabs.shape
    tr = _pick(k, (256, 128, 64, 32, 16))
    n_main = N_CHIPS * n - heads
    pieces = _w_in_pieces(n, f_off, heads)

    def body(s_ref, m_ref, f_ref):
        f_ref[...] = jnp.zeros_like(f_ref)
        for chip in range(N_CHIPS):
            slab = s_ref[0, chip].astype(F32)
            for pc, s0, s1, target, t0 in pieces:
                if pc == chip:
                    out = m_ref if target == "main" else f_ref
                    out[0, :, t0:t0 + s1 - s0] = slab[:, s0:s1].astype(out.dtype)

    return _call(body, name=name, grid=(layers, k // tr),
                 in_specs=[pl.BlockSpec((1, N_CHIPS, tr, n), lambda layer, i: (layer, 0, i, 0))],
                 out_specs=[pl.BlockSpec((1, tr, n_main), lambda layer, i: (layer, i, 0)),
                            pl.BlockSpec((1, tr, LANES), lambda layer, i: (layer, i, 0))],
                 out_shape=[_sds((layers, k, n_main), slabs.dtype), _sds((layers, k, LANES), slabs.dtype)],
                 sem=("parallel", "parallel"))(slabs)


def _w_in_to_slabs(d_main, d_f, n, f_off, heads, *, name):
    layers = len(d_main)
    k = d_main[0].shape[0]
    tr = _pick(k, (128, 64, 32, 16, 8))
    pieces = _w_in_pieces(n, f_off, heads)

    def body(*refs):
        m_refs, f_refs, o_ref = refs[:layers], refs[layers:2 * layers], refs[2 * layers]
        for layer in range(layers):
            for chip, s0, s1, target, t0 in pieces:
                src = m_refs[layer] if target == "main" else f_refs[layer]
                o_ref[layer, chip, :, s0:s1] = src[:, t0:t0 + s1 - s0]

    return _call(body, name=name, grid=(k // tr,),
                 in_specs=[pl.BlockSpec((tr, d_main[0].shape[1]), lambda i: (i, 0))] * layers
                 + [pl.BlockSpec((tr, LANES), lambda i: (i, 0))] * layers,
                 out_specs=pl.BlockSpec((layers, N_CHIPS, tr, n), lambda i: (0, 0, i, 0)),
                 out_shape=_sds((layers, N_CHIPS, k, n), F32), sem=("parallel",))(*d_main, *d_f)


def kernel(x, c, w_ada, b_ada, w_in, b_in, conv_a_w, conv_a_b, ln_conv_g, ln_conv_b, w_conv_proj, w_attn_proj, w_mix_out, b_mix_out, ln1_g, ln1_b, w_ffn_up, ffn_conv_w, ffn_conv_b, w_ffn_down, ln2_g, ln2_b, loss_target, m_w_ada, m_b_ada, m_w_in, m_b_in, m_conv_a_w, m_conv_a_b, m_ln_conv_g, m_ln_conv_b, m_w_conv_proj, m_w_attn_proj, m_w_mix_out, m_b_mix_out, m_ln1_g, m_ln1_b, m_w_ffn_up, m_ffn_conv_w, m_ffn_conv_b, m_w_ffn_down, m_ln2_g, m_ln2_b, v_w_ada, v_b_ada, v_w_in, v_b_in, v_conv_a_w, v_conv_a_b, v_ln_conv_g, v_ln_conv_b, v_w_conv_proj, v_w_attn_proj, v_w_mix_out, v_b_mix_out, v_ln1_g, v_ln1_b, v_w_ffn_up, v_ffn_conv_w, v_ffn_conv_b, v_w_ffn_down, v_ln2_g, v_ln2_b):
    weights = dict(zip(WEIGHTS, (w_ada, b_ada, w_in, b_in, conv_a_w, conv_a_b, ln_conv_g, ln_conv_b, w_conv_proj,
                                 w_attn_proj, w_mix_out, b_mix_out, ln1_g, ln1_b, w_ffn_up, ffn_conv_w, ffn_conv_b,
                                 w_ffn_down, ln2_g, ln2_b)))
    mom1 = dict(zip(WEIGHTS, (m_w_ada, m_b_ada, m_w_in, m_b_in, m_conv_a_w, m_conv_a_b, m_ln_conv_g, m_ln_conv_b,
                              m_w_conv_proj, m_w_attn_proj, m_w_mix_out, m_b_mix_out, m_ln1_g, m_ln1_b, m_w_ffn_up,
                              m_ffn_conv_w, m_ffn_conv_b, m_w_ffn_down, m_ln2_g, m_ln2_b)))
    mom2 = dict(zip(WEIGHTS, (v_w_ada, v_b_ada, v_w_in, v_b_in, v_conv_a_w, v_conv_a_b, v_ln_conv_g, v_ln_conv_b,
                              v_w_conv_proj, v_w_attn_proj, v_w_mix_out, v_b_mix_out, v_ln1_g, v_ln1_b, v_w_ffn_up,
                              v_ffn_conv_w, v_ffn_conv_b, v_w_ffn_down, v_ln2_g, v_ln2_b)))
    bsz, seq, d = x.shape
    layers = w_ada.shape[0]
    ch = conv_a_w.shape[2] * N_CHIPS
    width = w_attn_proj.shape[1]
    heads = width // HEAD_DIM
    d_ff = w_ffn_down.shape[1] * N_CHIPS
    dims = dict(bsz=bsz, d=d, ch=ch, heads=heads, alpha=(2.0 * layers) ** 0.25, f_off=2 * ch + 3 * width,
                ga_off=2 * ch + 3 * width, tcf=_pick(d_ff, (256, 128)))
    chip = 2 * lax.axis_index("x") + lax.axis_index("y")
    device = 2 * chip + lax.axis_index("c")
    ada_cols = w_ada.shape[2]

    c_act = _silu_rows(_all_gather8(c, name="gather_c").reshape(N_DEVICES * bsz, d), name="silu_c")
    b_ada_mine = lax.dynamic_slice_in_dim(b_ada, chip * ada_cols, ada_cols, axis=1)
    mod_cols = jnp.stack([_matmul(c_act, w_ada[layer], "nn", F32, bias=b_ada_mine[layer][None], name=f"ada_l{layer}")
                          for layer in range(layers)])
    mod_all = _all_gather8(mod_cols, name="gather_mod")
    mod_all = jnp.concatenate([mod_all[2 * k] for k in range(N_CHIPS)], axis=-1)
    mod = lax.dynamic_slice_in_dim(mod_all, device * bsz, bsz, axis=1)

    shards = [_pad_rows(weights[wname].astype(BF16) if as_bf16 else weights[wname]) for wname, _, as_bf16 in GATHERED]
    modes = [mode for _, mode, _ in GATHERED]
    whole = _gather_weights(shards, modes, name="gather_weights")
    full = {wname: w[:, :weights[wname].shape[1]] if mode == "cols" else w
            for (wname, mode, _), w in zip(GATHERED, whole)}
    full["w_main"], full["w_f"] = _w_in_from_slabs(full.pop("w_in"), dims["f_off"], heads, name="w_in_from_slabs")
    rep = {wname: weights[wname] for wname in REPLICATED}

    loss_local, grad_x, grads, dmod = _local_step(x, mod, loss_target, full, rep, dims)
    loss = lax.psum(loss_local, ("x", "y", "c"))

    grads["w_in"] = _w_in_to_slabs(grads.pop("w_main"), grads.pop("w_f"), w_in.shape[2], dims["f_off"], heads,
                                   name="w_in_to_slabs")
    shard_shapes = [s.shape for s in shards]
    reduced = _reduce_scatter([_pad_rows(grads[wname]) for wname, _, _ in GATHERED], modes, shard_shapes)
    grad = {wname: r[:, :weights[wname].shape[1]] for (wname, _, _), r in zip(GATHERED, reduced)}

    small = jnp.concatenate([dmod.reshape(-1)] + [grads[wname].reshape(-1) for wname in REPLICATED])
    n_small = small.shape[0]
    rows = -(-n_small // (SUBLANES * LANES)) * SUBLANES
    small = jnp.pad(small, (0, rows * LANES - n_small)).reshape(rows, LANES)
    gathered = _all_gather8(small, name="gather_small")
    n_dmod = dmod.size
    dmod_all = gathered.reshape(N_DEVICES, -1)[:, :n_dmod].reshape(N_DEVICES, layers, bsz, 6 * d)
    dmod_all = jnp.transpose(dmod_all, (1, 0, 2, 3)).reshape(layers, N_DEVICES * bsz, 6 * d)
    summed = _sum_slots(gathered, name="sum_small").reshape(-1)
    off = n_dmod
    for wname in REPLICATED:
        n = weights[wname].size
        grad[wname] = summed[off:off + n].reshape(weights[wname].shape)
        off += n
    dmod_mine = lax.dynamic_slice_in_dim(dmod_all, chip * ada_cols, ada_cols, axis=2)
    grad["w_ada"] = jnp.stack([_matmul(c_act, dmod_mine[layer], "tn", F32, name=f"ada_dw_l{layer}")
                               for layer in range(layers)])
    grad["b_ada"] = jnp.stack([_colsum(dmod_all[layer], name=f"ada_db_l{layer}")[0] for layer in range(layers)])

    delta, new_m, new_v = {}, {}, {}
    for wname in WEIGHTS:
        delta[wname], new_m[wname], new_v[wname] = _adamw(weights[wname], grad[wname], mom1[wname], mom2[wname],
                                                          name=f"adamw_{wname}")
    return (loss, grad_x, *[grad[wname] for wname in WEIGHTS], *[delta[wname] for wname in WEIGHTS],
            *[new_m[wname] for wname in WEIGHTS], *[new_v[wname] for wname in WEIGHTS])
```

```python
import math

import jax
import jax.numpy as jnp
from jax import lax
from jax.experimental import pallas as pl
from jax.experimental.pallas import tpu as pltpu

F32 = jnp.float32
BF16 = jnp.bfloat16
MESH = pl.DeviceIdType.MESH

LN_EPS = 1e-5
HEAD_DIM = 64
ATTN_SCALE = HEAD_DIM ** -0.5
NEG = -1e30
FFN_PAD = 8
LANES = 128
SUBLANES = 8
ROW_CHUNK = 256
ATTN_BLOCK = 256
ATTN_UNROLL = 4
N_CHIPS = 4
N_DEVICES = 8
VMEM_LIMIT = 56 * 1024 * 1024

ADAM_LR = 0.001
ADAM_B1 = 0.9
ADAM_B2 = 0.999
ADAM_EPS = 1e-08
ADAM_WD = 0.01
ADAM_STEP = 10

GATHERED = (("w_in", "slab", True), ("conv_a_w", "cols", False), ("w_conv_proj", "cols", True),
            ("w_attn_proj", "cols", True), ("w_mix_out", "rows", True), ("w_ffn_up", "cols", True),
            ("ffn_conv_w", "cols", False), ("w_ffn_down", "rows", True))
REPLICATED = ("b_in", "conv_a_b", "ln_conv_g", "ln_conv_b", "b_mix_out", "ln1_g", "ln1_b",
              "ffn_conv_b", "ln2_g", "ln2_b")
WEIGHTS = ("w_ada", "b_ada", "w_in", "b_in", "conv_a_w", "conv_a_b", "ln_conv_g", "ln_conv_b",
           "w_conv_proj", "w_attn_proj", "w_mix_out", "b_mix_out", "ln1_g", "ln1_b", "w_ffn_up",
           "ffn_conv_w", "ffn_conv_b", "w_ffn_down", "ln2_g", "ln2_b")


def _pick(n, cands):
    for cand in cands:
        if n % cand == 0:
            return cand
    return n


def _call(body, *, name, grid, in_specs, out_specs, out_shape, scratch=(), sem=None):
    return pl.pallas_call(
        body, name=name, grid=grid, in_specs=in_specs, out_specs=out_specs, out_shape=out_shape,
        scratch_shapes=list(scratch),
        compiler_params=pltpu.CompilerParams(dimension_semantics=sem, vmem_limit_bytes=VMEM_LIMIT))


def _sds(shape, dtype):
    return jax.ShapeDtypeStruct(tuple(shape), dtype)


def _chunked(rows, fn):
    chunk = min(ROW_CHUNK, rows)
    if rows == chunk:
        fn(pl.ds(0, rows))
        return

    def step(i, carry):
        fn(pl.ds(pl.multiple_of(i * chunk, chunk), chunk))
        return carry

    lax.fori_loop(0, rows // chunk, step, 0)


def _matmul(a, b, mode, out_dtype, *, bias=None, add=None, colsum=False, b_k_first=0, out_cols=None, into=None,
            name):
    if mode == "nn":
        (m, k), (_, n) = a.shape, b.shape
    elif mode == "nt":
        (m, k), (n, _) = a.shape, b.shape
    else:
        (k, m), (_, n) = a.shape, b.shape
    tm = _pick(m, (1024, 1408, 512, 256, 128))
    tn = _pick(n, (1536, 1408, 1024, 512, 256, 128))
    tk = k if k <= 1536 else _pick(k, (1024, 1536, 1408, 512, 256, 128))
    nk = k // tk
    assert b_k_first % tk == 0 and (b_k_first == 0 or mode == "nt")
    k_blk0 = b_k_first // tk
    if mode == "nn":
        a_spec = pl.BlockSpec((tm, tk), lambda i, j, kk: (i, kk))
        b_spec = pl.BlockSpec((tk, tn), lambda i, j, kk: (kk, j))
        dims = (((1,), (0,)), ((), ()))
    elif mode == "nt":
        a_spec = pl.BlockSpec((tm, tk), lambda i, j, kk: (i, kk))
        b_spec = pl.BlockSpec((tn, tk), lambda i, j, kk: (j, k_blk0 + kk))
        dims = (((1,), (1,)), ((), ()))
    else:
        a_spec = pl.BlockSpec((tk, tm), lambda i, j, kk: (kk, i))
        b_spec = pl.BlockSpec((tk, tn), lambda i, j, kk: (kk, j))
        dims = (((0,), (0,)), ((), ()))
    in_specs = [a_spec, b_spec]
    operands = [a, b]
    if bias is not None:
        in_specs.append(pl.BlockSpec((1, tn), lambda i, j, kk: (0, j)))
        operands.append(bias)
    if add is not None:
        in_specs.append(pl.BlockSpec((tm, tn), lambda i, j, kk: (i, j)))
        operands.append(add)

    def body(a_ref, b_ref, *rest):
        rest = list(rest)
        bias_ref = rest.pop(0) if bias is not None else None
        add_ref = rest.pop(0) if add is not None else None
        if into is not None:
            rest.pop(0)
        o_ref = rest.pop(0)
        prod = lax.dot_general(a_ref[...].astype(BF16), b_ref[...].astype(BF16), dims,
                               preferred_element_type=F32)
        if colsum:
            cs_ref = rest.pop(0)
            part = jnp.sum(b_ref[...].astype(F32), axis=0, keepdims=True)

            @pl.when(pl.program_id(2) == 0)
            def _():
                cs_ref[...] = part

            @pl.when(pl.program_id(2) > 0)
            def _():
                cs_ref[...] += part

        def finish(r):
            if bias_ref is not None:
                r = r + bias_ref[...]
            if add_ref is not None:
                r = r + add_ref[...]
            o_ref[...] = r.astype(o_ref.dtype)

        if nk == 1:
            finish(prod)
            return
        acc_ref = rest.pop(0)
        kk = pl.program_id(2)

        @pl.when(kk == 0)
        def _():
            acc_ref[...] = prod

        @pl.when(kk > 0)
        def _():
            acc_ref[...] += prod

        @pl.when(kk == nk - 1)
        def _():
            finish(acc_ref[...])

    col_blk0 = 0
    aliases = {}
    out_shape = _sds((m, n if out_cols is None else out_cols), out_dtype)
    if into is not None:
        wide, col_first = into
        assert col_first % tn == 0 and not colsum
        col_blk0 = col_first // tn
        out_shape = _sds(wide.shape, wide.dtype)
        aliases = {len(operands): 0}
        in_specs.append(pl.BlockSpec(memory_space=pl.ANY))
        operands.append(wide)
    out_specs = pl.BlockSpec((tm, tn), lambda i, j, kk: (i, col_blk0 + j))
    if colsum:
        assert mode == "tn" and m == tm
        out_specs = [out_specs, pl.BlockSpec((1, tn), lambda i, j, kk: (0, j))]
        out_shape = [out_shape, _sds((1, n), F32)]
    return pl.pallas_call(
        body, name=name, grid=(m // tm, n // tn, nk), in_specs=in_specs, out_specs=out_specs, out_shape=out_shape,
        scratch_shapes=[pltpu.VMEM((tm, tn), F32)] if nk > 1 else [], input_output_aliases=aliases,
        compiler_params=pltpu.CompilerParams(dimension_semantics=("parallel", "parallel", "arbitrary"),
                                             vmem_limit_bytes=VMEM_LIMIT))(*operands)


def _colsum(x, *, name):
    rows, n = x.shape
    tr = _pick(rows, (1024, 512, 256, 128))
    tn = _pick(n, (512, 256, 128))

    def body(x_ref, o_ref):
        @pl.when(pl.program_id(1) == 0)
        def _():
            o_ref[...] = jnp.zeros_like(o_ref)

        o_ref[...] += jnp.sum(x_ref[...].astype(F32), axis=0, keepdims=True)

    return _call(body, name=name, grid=(n // tn, rows // tr),
                 in_specs=[pl.BlockSpec((tr, tn), lambda j, i: (i, j))],
                 out_specs=pl.BlockSpec((1, tn), lambda j, i: (0, j)),
                 out_shape=_sds((1, n), F32), sem=("parallel", "arbitrary"))(x)


def _ln_stats(x):
    mu = jnp.mean(x, axis=-1, keepdims=True)
    xc = x - mu
    var = jnp.mean(xc * xc, axis=-1, keepdims=True)
    rstd = lax.rsqrt(var + LN_EPS)
    return xc * rstd, rstd


def _ln_bwd(dn, n, rstd):
    return rstd * (dn - jnp.mean(dn, axis=-1, keepdims=True) - n * jnp.mean(dn * n, axis=-1, keepdims=True))


def _seq_tiles(t, bsz, cands=(1024, 512, 256, 128, 64, 32, 16, 8)):
    s = t // bsz
    ts = _pick(s, cands)
    return s, ts, s // ts


def _ln_mod_fwd(x, scale, shift, bsz, *, name):
    t, d = x.shape
    _, ts, ns = _seq_tiles(t, bsz)

    def body(x_ref, sc_ref, sh_ref, u_ref):
        one_scale = 1.0 + sc_ref[0]
        shift_v = sh_ref[0]

        def piece(rows):
            n, _ = _ln_stats(x_ref[rows, :])
            u_ref[rows, :] = (n * one_scale + shift_v).astype(u_ref.dtype)

        _chunked(ts, piece)

    row = pl.BlockSpec((ts, d), lambda b, i: (b * ns + i, 0))
    per = pl.BlockSpec((1, 1, d), lambda b, i: (b, 0, 0))
    return _call(body, name=name, grid=(bsz, ns), in_specs=[row, per, per], out_specs=row,
                 out_shape=_sds((t, d), BF16), sem=("parallel", "parallel"))(x, scale, shift)


def _ln_mod_bwd(du, dz_narrow, w_narrow, x, scale, dr, alpha, bsz, *, name):
    t, d = x.shape
    _, ts, ns = _seq_tiles(t, bsz)
    lanes = dz_narrow.shape[1]

    def body(du_ref, dzn_ref, wn_ref, x_ref, sc_ref, dr_ref, dx_ref, dsc_ref, dsh_ref):
        @pl.when(pl.program_id(1) == 0)
        def _():
            dsc_ref[...] = jnp.zeros_like(dsc_ref)
            dsh_ref[...] = jnp.zeros_like(dsh_ref)

        one_scale = 1.0 + sc_ref[0]
        w_n = wn_ref[...].astype(BF16)

        def piece(rows):
            du_v = du_ref[rows, :] + lax.dot_general(dzn_ref[rows, :].astype(BF16), w_n, (((1,), (1,)), ((), ())),
                                                     preferred_element_type=F32)
            n, rstd = _ln_stats(x_ref[rows, :])
            dsc_ref[0] += jnp.sum(du_v * n, axis=0, keepdims=True)
            dsh_ref[0] += jnp.sum(du_v, axis=0, keepdims=True)
            dx_ref[rows, :] = alpha * dr_ref[rows, :] + _ln_bwd(du_v * one_scale, n, rstd)

        _chunked(ts, piece)

    row = pl.BlockSpec((ts, d), lambda b, i: (b * ns + i, 0))
    per = pl.BlockSpec((1, 1, d), lambda b, i: (b, 0, 0))
    return _call(body, name=name, grid=(bsz, ns),
                 in_specs=[row, pl.BlockSpec((ts, lanes), lambda b, i: (b * ns + i, 0)),
                           pl.BlockSpec((d, lanes), lambda b, i: (0, 0)), row, per, row],
                 out_specs=[row, per, per],
                 out_shape=[_sds((t, d), F32), _sds((bsz, 1, d), F32), _sds((bsz, 1, d), F32)],
                 sem=("parallel", "arbitrary"))(du, dz_narrow, w_narrow, x, scale, dr)


def _ln_res_fwd(x, y, gate, g, b, alpha, bsz, *, name):
    t, d = x.shape
    _, ts, ns = _seq_tiles(t, bsz)

    def body(x_ref, y_ref, gt_ref, g_ref, b_ref, o_ref):
        one_gate = 1.0 + gt_ref[0]

        def piece(rows):
            n, _ = _ln_stats(alpha * x_ref[rows, :] + one_gate * y_ref[rows, :])
            o_ref[rows, :] = n * g_ref[...] + b_ref[...]

        _chunked(ts, piece)

    row = pl.BlockSpec((ts, d), lambda bb, i: (bb * ns + i, 0))
    per = pl.BlockSpec((1, 1, d), lambda bb, i: (bb, 0, 0))
    vec = pl.BlockSpec((1, d), lambda bb, i: (0, 0))
    return _call(body, name=name, grid=(bsz, ns), in_specs=[row, row, per, vec, vec], out_specs=row,
                 out_shape=_sds((t, d), F32), sem=("parallel", "parallel"))(x, y, gate, g, b)


def _ln_res_bwd(do, x, y, gate, g, alpha, bsz, *, name):
    t, d = x.shape
    _, ts, ns = _seq_tiles(t, bsz)

    def body(do_ref, x_ref, y_ref, gt_ref, g_ref, dr_ref, dy_ref, dgt_ref, dg_ref, db_ref, dys_ref):
        first_tile = pl.program_id(1) == 0

        @pl.when(first_tile)
        def _():
            dgt_ref[...] = jnp.zeros_like(dgt_ref)

        @pl.when(jnp.logical_and(first_tile, pl.program_id(0) == 0))
        def _():
            dg_ref[...] = jnp.zeros_like(dg_ref)
            db_ref[...] = jnp.zeros_like(db_ref)
            dys_ref[...] = jnp.zeros_like(dys_ref)

        one_gate = 1.0 + gt_ref[0]

        def piece(rows):
            do_v = do_ref[rows, :]
            y_v = y_ref[rows, :]
            n, rstd = _ln_stats(alpha * x_ref[rows, :] + one_gate * y_v)
            dg_ref[...] += jnp.sum(do_v * n, axis=0, keepdims=True)
            db_ref[...] += jnp.sum(do_v, axis=0, keepdims=True)
            dr = _ln_bwd(do_v * g_ref[...], n, rstd)
            dr_ref[rows, :] = dr
            dy = one_gate * dr
            dy_ref[rows, :] = dy.astype(dy_ref.dtype)
            dys_ref[...] += jnp.sum(dy, axis=0, keepdims=True)
            dgt_ref[0] += jnp.sum(dr * y_v, axis=0, keepdims=True)

        _chunked(ts, piece)

    row = pl.BlockSpec((ts, d), lambda bb, i: (bb * ns + i, 0))
    per = pl.BlockSpec((1, 1, d), lambda bb, i: (bb, 0, 0))
    vec = pl.BlockSpec((1, d), lambda bb, i: (0, 0))
    return _call(body, name=name, grid=(bsz, ns), in_specs=[row, row, row, per, vec],
                 out_specs=[row, row, per, vec, vec, vec],
                 out_shape=[_sds((t, d), F32), _sds((t, d), BF16), _sds((bsz, 1, d), F32),
                            _sds((1, d), F32), _sds((1, d), F32), _sds((1, d), F32)],
                 sem=("arbitrary", "arbitrary"))(do, x, y, gate, g)


def _ln_res_mod_fwd(x, y, gate, g, b, scale, shift, alpha, bsz, *, name):
    t, d = x.shape
    _, ts, ns = _seq_tiles(t, bsz)

    def body(x_ref, y_ref, gt_ref, g_ref, b_ref, sc_ref, sh_ref, o_ref, u_ref):
        one_gate = 1.0 + gt_ref[0]
        one_scale = 1.0 + sc_ref[0]
        shift_v = sh_ref[0]

        def piece(rows):
            n, _ = _ln_stats(alpha * x_ref[rows, :] + one_gate * y_ref[rows, :])
            x1 = n * g_ref[...] + b_ref[...]
            o_ref[rows, :] = x1
            n1, _ = _ln_stats(x1)
            u_ref[rows, :] = (n1 * one_scale + shift_v).astype(u_ref.dtype)

        _chunked(ts, piece)

    row = pl.BlockSpec((ts, d), lambda bb, i: (bb * ns + i, 0))
    per = pl.BlockSpec((1, 1, d), lambda bb, i: (bb, 0, 0))
    vec = pl.BlockSpec((1, d), lambda bb, i: (0, 0))
    return _call(body, name=name, grid=(bsz, ns), in_specs=[row, row, per, vec, vec, per, per],
                 out_specs=[row, row], out_shape=[_sds((t, d), F32), _sds((t, d), BF16)],
                 sem=("parallel", "parallel"))(x, y, gate, g, b, scale, shift)


def _ln_mod_res_bwd(du, dr_up, scale, x, y, gate, g, b, alpha, bsz, *, name):
    t, d = x.shape
    _, ts, ns = _seq_tiles(t, bsz, (512, 256, 128, 64, 32, 16, 8))

    def body(du_ref, up_ref, sc_ref, x_ref, y_ref, gt_ref, g_ref, b_ref,
             dr_ref, dy_ref, dgt_ref, dg_ref, db_ref, dys_ref, dsc_ref, dsh_ref):
        first_tile = pl.program_id(1) == 0

        @pl.when(first_tile)
        def _():
            for ref in (dgt_ref, dsc_ref, dsh_ref):
                ref[...] = jnp.zeros_like(ref)

        @pl.when(jnp.logical_and(first_tile, pl.program_id(0) == 0))
        def _():
            for ref in (dg_ref, db_ref, dys_ref):
                ref[...] = jnp.zeros_like(ref)

        one_gate = 1.0 + gt_ref[0]
        one_scale = 1.0 + sc_ref[0]

        def piece(rows):
            y_v = y_ref[rows, :]
            n, rstd = _ln_stats(alpha * x_ref[rows, :] + one_gate * y_v)
            n1, rstd1 = _ln_stats(n * g_ref[...] + b_ref[...])
            du_v = du_ref[rows, :]
            dsc_ref[0] += jnp.sum(du_v * n1, axis=0, keepdims=True)
            dsh_ref[0] += jnp.sum(du_v, axis=0, keepdims=True)
            dx1 = alpha * up_ref[rows, :] + _ln_bwd(du_v * one_scale, n1, rstd1)
            dg_ref[...] += jnp.sum(dx1 * n, axis=0, keepdims=True)
            db_ref[...] += jnp.sum(dx1, axis=0, keepdims=True)
            dr = _ln_bwd(dx1 * g_ref[...], n, rstd)
            dr_ref[rows, :] = dr
            dy = one_gate * dr
            dy_ref[rows, :] = dy.astype(dy_ref.dtype)
            dys_ref[...] += jnp.sum(dy, axis=0, keepdims=True)
            dgt_ref[0] += jnp.sum(dr * y_v, axis=0, keepdims=True)

        _chunked(ts, piece)

    row = pl.BlockSpec((ts, d), lambda bb, i: (bb * ns + i, 0))
    per = pl.BlockSpec((1, 1, d), lambda bb, i: (bb, 0, 0))
    vec = pl.BlockSpec((1, d), lambda bb, i: (0, 0))
    return _call(body, name=name, grid=(bsz, ns), in_specs=[row, row, per, row, row, per, vec, vec],
                 out_specs=[row, row, per, vec, vec, vec, per, per],
                 out_shape=[_sds((t, d), F32), _sds((t, d), BF16), _sds((bsz, 1, d), F32), _sds((1, d), F32),
                            _sds((1, d), F32), _sds((1, d), F32), _sds((bsz, 1, d), F32), _sds((bsz, 1, d), F32)],
                 sem=("arbitrary", "arbitrary"))(du, dr_up, scale, x, y, gate, g, b)


def _ln_res_loss(x, y, gate, g, b, target, alpha, bsz, *, name):
    t, d = x.shape
    _, ts, ns = _seq_tiles(t, bsz)

    def body(x_ref, y_ref, gt_ref, g_ref, b_ref, t_ref, dy_ref, s_ref):
        @pl.when(jnp.logical_and(pl.program_id(0) == 0, pl.program_id(1) == 0))
        def _():
            s_ref[...] = jnp.zeros_like(s_ref)

        one_gate = 1.0 + gt_ref[0]

        def piece(rows):
            n, _ = _ln_stats(alpha * x_ref[rows, :] + one_gate * y_ref[rows, :])
            e = (n * g_ref[...] + b_ref[...]) - t_ref[rows, :]
            dy_ref[rows, :] = e * (1.0 / d)
            s_ref[...] += jnp.sum(e * e, axis=0, keepdims=True)

        _chunked(ts, piece)

    row = pl.BlockSpec((ts, d), lambda bb, i: (bb * ns + i, 0))
    per = pl.BlockSpec((1, 1, d), lambda bb, i: (bb, 0, 0))
    vec = pl.BlockSpec((1, d), lambda bb, i: (0, 0))
    return _call(body, name=name, grid=(bsz, ns), in_specs=[row, row, per, vec, vec, row], out_specs=[row, vec],
                 out_shape=[_sds((t, d), F32), _sds((1, d), F32)],
                 sem=("arbitrary", "arbitrary"))(x, y, gate, g, b, target)


def _sigmoid(v):
    return 1.0 / (1.0 + jnp.exp(-v))


def _silu_rows(c, *, name):
    rows, d = c.shape

    def body(c_ref, o_ref):
        v = c_ref[...]
        o_ref[...] = (v * _sigmoid(v)).astype(o_ref.dtype)

    full = pl.BlockSpec((rows, d), lambda i: (0, 0))
    return _call(body, name=name, grid=(1,), in_specs=[full], out_specs=full,
                 out_shape=_sds((rows, d), BF16), sem=("arbitrary",))(c)


def _gate_cols(d, ga_off):
    tc = _pick(math.gcd(d, ga_off), (512, 256, 128))
    return tc, ga_off // tc, (ga_off + d) // tc


def _gate_merge_fwd(z, ya, yb, ga_off, *, name):
    t, d = ya.shape
    tr = _pick(t, (1024, 512, 256, 128, 64, 32, 16, 8))
    tc, ga_blk, gb_blk = _gate_cols(d, ga_off)

    def body(ga_ref, gb_ref, ya_ref, yb_ref, o_ref):
        def piece(rows):
            o_ref[rows, :] = (_sigmoid(ga_ref[rows, :].astype(F32)) * ya_ref[rows, :].astype(F32)
                              + _sigmoid(gb_ref[rows, :].astype(F32)) * yb_ref[rows, :].astype(F32)
                              ).astype(o_ref.dtype)

        _chunked(tr, piece)

    blk = pl.BlockSpec((tr, tc), lambda i, j: (i, j))
    return _call(body, name=name, grid=(t // tr, d // tc),
                 in_specs=[pl.BlockSpec((tr, tc), lambda i, j: (i, ga_blk + j)),
                           pl.BlockSpec((tr, tc), lambda i, j: (i, gb_blk + j)), blk, blk],
                 out_specs=blk, out_shape=_sds((t, d), BF16), sem=("parallel", "parallel"))(z, z, ya, yb)


def _gate_merge_bwd(z, ya, yb, dm, ga_off, *, name):
    t, d = ya.shape
    tr = _pick(t, (1024, 512, 256, 128, 64, 32, 16, 8))
    tc, ga_blk, gb_blk = _gate_cols(d, ga_off)

    def body(ga_ref, gb_ref, ya_ref, yb_ref, dm_ref, dya_ref, dyb_ref, dga_ref, dgb_ref):
        def piece(rows):
            dm_v = dm_ref[rows, :].astype(F32)
            sa = _sigmoid(ga_ref[rows, :].astype(F32))
            sb = _sigmoid(gb_ref[rows, :].astype(F32))
            dya_ref[rows, :] = (dm_v * sa).astype(dya_ref.dtype)
            dyb_ref[rows, :] = (dm_v * sb).astype(dyb_ref.dtype)
            dga_ref[rows, :] = (dm_v * ya_ref[rows, :].astype(F32) * sa * (1.0 - sa)).astype(dga_ref.dtype)
            dgb_ref[rows, :] = (dm_v * yb_ref[rows, :].astype(F32) * sb * (1.0 - sb)).astype(dgb_ref.dtype)

        _chunked(tr, piece)

    blk = pl.BlockSpec((tr, tc), lambda i, j: (i, j))
    return _call(body, name=name, grid=(t // tr, d // tc),
                 in_specs=[pl.BlockSpec((tr, tc), lambda i, j: (i, ga_blk + j)),
                           pl.BlockSpec((tr, tc), lambda i, j: (i, gb_blk + j)), blk, blk, blk],
                 out_specs=[blk, blk, blk, blk], out_shape=[_sds((t, d), BF16)] * 4,
                 sem=("parallel", "parallel"))(z, z, ya, yb, dm)


CONV_ROWS = 64
CONV_PAD = 32


def _row_shifts(win):
    total = win.shape[0]
    return [win] + [pltpu.roll(win, total - b, axis=0) for b in range(1, SUBLANES)]


def _shifted_rows(copies, shift):
    start = SUBLANES * (shift // SUBLANES)
    return copies[shift % SUBLANES][start:start + CONV_ROWS]


def _fill_glu(z_ref, ext_ref, s, ch):
    ext_ref[pl.ds(0, CONV_PAD), :] = jnp.zeros((CONV_PAD, ch), F32)

    chunk = min(ROW_CHUNK, s)

    def piece(i, carry):
        start = pl.multiple_of(i * chunk, chunk)
        zz = z_ref[pl.ds(start, chunk), :].astype(F32)
        ext_ref[pl.ds(pl.multiple_of(CONV_PAD + start, CONV_PAD), chunk), :] = zz[:, :ch] * _sigmoid(zz[:, ch:])
        return carry

    lax.fori_loop(0, s // chunk, piece, 0)


def _conv_piece(ext_ref, w_ref, cb_ref, base, kw):
    copies = _row_shifts(ext_ref[pl.ds(base, CONV_ROWS + CONV_PAD), :])
    acc = cb_ref[...] + w_ref[pl.ds(0, 1), :] * _shifted_rows(copies, CONV_PAD - (kw - 1))
    for k in range(1, kw):
        acc = acc + w_ref[pl.ds(k, 1), :] * _shifted_rows(copies, CONV_PAD - (kw - 1) + k)
    return acc, copies


def _conv_branch_fwd(z, w, cb, lg, lb, bsz, ch, *, name):
    t = z.shape[0]
    s = t // bsz
    kw = w.shape[0]

    def body(z_ref, w_ref, cb_ref, lg_ref, lb_ref, o_ref, ext_ref):
        _fill_glu(z_ref, ext_ref, s, ch)

        def step(i, carry):
            base = pl.multiple_of(i * CONV_ROWS, CONV_ROWS)
            a1, _ = _conv_piece(ext_ref, w_ref, cb_ref, base, kw)
            n, _ = _ln_stats(a1)
            a2 = n * lg_ref[...] + lb_ref[...]
            o_ref[pl.ds(base, CONV_ROWS), :] = (a2 * _sigmoid(a2)).astype(o_ref.dtype)
            return carry

        lax.fori_loop(0, s // CONV_ROWS, step, 0)

    vec = pl.BlockSpec((1, ch), lambda b: (0, 0))
    return _call(body, name=name, grid=(bsz,),
                 in_specs=[pl.BlockSpec((s, 2 * ch), lambda b: (b, 0)), pl.BlockSpec((kw, ch), lambda b: (0, 0)),
                           vec, vec, vec],
                 out_specs=pl.BlockSpec((s, ch), lambda b: (b, 0)), out_shape=_sds((t, ch), BF16),
                 scratch=[pltpu.VMEM((CONV_PAD + s, ch), F32)], sem=("parallel",))(z, w, cb, lg, lb)


def _conv_branch_bwd(z, da3, w, cb, lg, lb, bsz, ch, *, name):
    t = z.shape[0]
    s = t // bsz
    kw = w.shape[0]
    n_rows = CONV_ROWS + CONV_PAD

    def body(z_ref, d_ref, w_ref, cb_ref, lg_ref, lb_ref, dz_ref, dw_ref, dcb_ref, dlg_ref, dlb_ref,
             ext_ref, da1_ref):
        @pl.when(pl.program_id(0) == 0)
        def _():
            for ref in (dw_ref, dcb_ref, dlg_ref, dlb_ref):
                ref[...] = jnp.zeros_like(ref)

        _fill_glu(z_ref, ext_ref, s, ch)
        da1_ref[pl.ds(s, CONV_PAD), :] = jnp.zeros((CONV_PAD, ch), F32)

        def grad_a1(i, carry):
            dlg, dlb = carry
            base = pl.multiple_of(i * CONV_ROWS, CONV_ROWS)
            a1, _ = _conv_piece(ext_ref, w_ref, cb_ref, base, kw)
            n, rstd = _ln_stats(a1)
            a2 = n * lg_ref[...] + lb_ref[...]
            sg = _sigmoid(a2)
            da2 = d_ref[pl.ds(base, CONV_ROWS), :] * (sg * (1.0 + a2 * (1.0 - sg)))
            da1_ref[pl.ds(base, CONV_ROWS), :] = _ln_bwd(da2 * lg_ref[...], n, rstd)
            return (dlg + jnp.sum(da2 * n, axis=0, keepdims=True), dlb + jnp.sum(da2, axis=0, keepdims=True))

        zero = jnp.zeros((1, ch), F32)
        dlg, dlb = lax.fori_loop(0, s // CONV_ROWS, grad_a1, (zero, zero))
        dlg_ref[...] += dlg
        dlb_ref[...] += dlb

        def grad_z(i, dcb):
            base = pl.multiple_of(i * CONV_ROWS, CONV_ROWS)
            ahead = _row_shifts(da1_ref[pl.ds(base, n_rows), :])
            dyc = ahead[0][:CONV_ROWS]
            da0 = w_ref[pl.ds(kw - 1, 1), :] * dyc
            for k in range(kw - 1):
                da0 = da0 + w_ref[pl.ds(k, 1), :] * _shifted_rows(ahead, kw - 1 - k)
            behind = _row_shifts(ext_ref[pl.ds(base, n_rows), :])
            for k in range(kw):
                dw_ref[pl.ds(k, 1), :] += jnp.sum(dyc * _shifted_rows(behind, CONV_PAD - (kw - 1) + k),
                                                  axis=0, keepdims=True)
            zz = z_ref[pl.ds(base, CONV_ROWS), :].astype(F32)
            sg = _sigmoid(zz[:, ch:])
            dz_ref[pl.ds(base, CONV_ROWS), :ch] = (da0 * sg).astype(dz_ref.dtype)
            dz_ref[pl.ds(base, CONV_ROWS), ch:] = (da0 * zz[:, :ch] * sg * (1.0 - sg)).astype(dz_ref.dtype)
            return dcb + jnp.sum(dyc, axis=0, keepdims=True)

        dcb_ref[...] += lax.fori_loop(0, s // CONV_ROWS, grad_z, zero)

    vec = pl.BlockSpec((1, ch), lambda b: (0, 0))
    taps = pl.BlockSpec((kw, ch), lambda b: (0, 0))
    return _call(body, name=name, grid=(bsz,),
                 in_specs=[pl.BlockSpec((s, 2 * ch), lambda b: (b, 0)), pl.BlockSpec((s, ch), lambda b: (b, 0)),
                           taps, vec, vec, vec],
                 out_specs=[pl.BlockSpec((s, 2 * ch), lambda b: (b, 0)), taps, vec, vec, vec],
                 out_shape=[_sds((t, 2 * ch), BF16), _sds((kw, ch), F32)] + [_sds((1, ch), F32)] * 3,
                 scratch=[pltpu.VMEM((CONV_PAD + s, ch), F32), pltpu.VMEM((s + CONV_PAD, ch), F32)],
                 sem=("arbitrary",))(z, da3, w, cb, lg, lb)


FFN_ROWS = 64


def _gelu_parts(v):
    cdf = 0.5 * (1.0 + lax.erf(v * (2.0 ** -0.5)))
    return cdf, v * cdf


def _ffn_conv_piece(ext_ref, wb_ref, base):
    win = ext_ref[pl.ds(base, FFN_ROWS + FFN_PAD), :]
    acc = wb_ref[pl.ds(3, 1), :] + wb_ref[pl.ds(2, 1), :] * win[FFN_PAD:]
    acc = acc + wb_ref[pl.ds(1, 1), :] * pltpu.roll(win, 1, axis=0)[FFN_PAD:]
    acc = acc + wb_ref[pl.ds(0, 1), :] * pltpu.roll(win, 2, axis=0)[FFN_PAD:]
    return acc


def _ffn_stage(hg_ref, hl_ref, wg_ref, wl_ref, bg_ref, bl_ref, ext_ref, wb_ref, s, tcf):
    ext_ref[pl.ds(0, FFN_PAD), :] = jnp.zeros((FFN_PAD, 2 * tcf), F32)
    ext_ref[pl.ds(FFN_PAD, s), :tcf] = hg_ref[...].astype(F32)
    ext_ref[pl.ds(FFN_PAD, s), tcf:] = hl_ref[...].astype(F32)
    wb_ref[pl.ds(0, 3), :tcf] = wg_ref[...]
    wb_ref[pl.ds(0, 3), tcf:] = wl_ref[...]
    wb_ref[pl.ds(3, 1), :tcf] = bg_ref[...]
    wb_ref[pl.ds(3, 1), tcf:] = bl_ref[...]


def _ffn_specs(s, tcf, n_f, batch_first):
    def spec(rows, shift):
        if batch_first:
            return pl.BlockSpec((rows, tcf), lambda bb, j: (bb if rows == s else 0, shift + j))
        return pl.BlockSpec((rows, tcf), lambda j, bb: (bb if rows == s else 0, shift + j))

    return [spec(s, 0), spec(s, n_f), spec(3, 0), spec(3, n_f), spec(1, 0), spec(1, n_f)]


def _ffn_act_fwd(hp, w, b, bsz, tcf, *, name):
    t, two_f = hp.shape
    s = t // bsz
    n_f = two_f // (2 * tcf)

    def body(hg_ref, hl_ref, wg_ref, wl_ref, bg_ref, bl_ref, f_ref, ext_ref, wb_ref):
        _ffn_stage(hg_ref, hl_ref, wg_ref, wl_ref, bg_ref, bl_ref, ext_ref, wb_ref, s, tcf)

        def step(i, carry):
            base = pl.multiple_of(i * FFN_ROWS, FFN_ROWS)
            hh = _ffn_conv_piece(ext_ref, wb_ref, base)
            _, gelu = _gelu_parts(hh[:, :tcf])
            f_ref[pl.ds(base, FFN_ROWS), :] = (gelu * hh[:, tcf:]).astype(f_ref.dtype)
            return carry

        lax.fori_loop(0, s // FFN_ROWS, step, 0)

    return _call(body, name=name, grid=(bsz, n_f), in_specs=_ffn_specs(s, tcf, n_f, True),
                 out_specs=pl.BlockSpec((s, tcf), lambda bb, j: (bb, j)),
                 out_shape=_sds((t, two_f // 2), BF16),
                 scratch=[pltpu.VMEM((FFN_PAD + s, 2 * tcf), F32), pltpu.VMEM((SUBLANES, 2 * tcf), F32)],
                 sem=("parallel", "parallel"))(hp, hp, w, w, b, b)


def _ffn_act_bwd(hp, df, w, b, bsz, tcf, *, name):
    t, two_f = hp.shape
    s = t // bsz
    f_dim = two_f // 2
    n_f = f_dim // tcf
    gw = 2 * tcf
    n_rows = FFN_ROWS + FFN_PAD

    def body(hg_ref, hl_ref, wg_ref, wl_ref, bg_ref, bl_ref, df_ref,
             dhg_ref, dhl_ref, dwg_ref, dwl_ref, dbg_ref, dbl_ref, ext_ref, wb_ref, dh_ref):
        @pl.when(pl.program_id(1) == 0)
        def _():
            for ref in (dwg_ref, dwl_ref, dbg_ref, dbl_ref):
                ref[...] = jnp.zeros_like(ref)

        _ffn_stage(hg_ref, hl_ref, wg_ref, wl_ref, bg_ref, bl_ref, ext_ref, wb_ref, s, tcf)
        dh_ref[pl.ds(s, FFN_PAD), :] = jnp.zeros((FFN_PAD, gw), F32)

        def grad_h(i, carry):
            base = pl.multiple_of(i * FFN_ROWS, FFN_ROWS)
            hh = _ffn_conv_piece(ext_ref, wb_ref, base)
            hg = hh[:, :tcf]
            d = df_ref[pl.ds(base, FFN_ROWS), :].astype(F32)
            cdf, gelu = _gelu_parts(hg)
            pdf = jnp.exp(-0.5 * hg * hg) * (1.0 / math.sqrt(2.0 * math.pi))
            dh_ref[pl.ds(base, FFN_ROWS), :tcf] = d * hh[:, tcf:] * (cdf + hg * pdf)
            dh_ref[pl.ds(base, FFN_ROWS), tcf:] = d * gelu
            return carry

        lax.fori_loop(0, s // FFN_ROWS, grad_h, 0)

        def grad_x(i, carry):
            dw0, dw1, dw2, dbs = carry
            base = pl.multiple_of(i * FFN_ROWS, FFN_ROWS)
            nxt = dh_ref[pl.ds(base, n_rows), :]
            dyc = nxt[:FFN_ROWS]
            dx = wb_ref[pl.ds(2, 1), :] * dyc
            dx = dx + wb_ref[pl.ds(1, 1), :] * pltpu.roll(nxt, n_rows - 1, axis=0)[:FFN_ROWS]
            dx = dx + wb_ref[pl.ds(0, 1), :] * pltpu.roll(nxt, n_rows - 2, axis=0)[:FFN_ROWS]
            dhg_ref[pl.ds(base, FFN_ROWS), :] = dx[:, :tcf].astype(dhg_ref.dtype)
            dhl_ref[pl.ds(base, FFN_ROWS), :] = dx[:, tcf:].astype(dhl_ref.dtype)
            win = ext_ref[pl.ds(base, n_rows), :]
            dw2 = dw2 + jnp.sum(dyc * win[FFN_PAD:], axis=0, keepdims=True)
            dw1 = dw1 + jnp.sum(dyc * pltpu.roll(win, 1, axis=0)[FFN_PAD:], axis=0, keepdims=True)
            dw0 = dw0 + jnp.sum(dyc * pltpu.roll(win, 2, axis=0)[FFN_PAD:], axis=0, keepdims=True)
            return dw0, dw1, dw2, dbs + jnp.sum(dyc, axis=0, keepdims=True)

        zero = jnp.zeros((1, gw), F32)
        sums = lax.fori_loop(0, s // FFN_ROWS, grad_x, (zero, zero, zero, zero))
        for k in range(3):
            dwg_ref[pl.ds(k, 1), :] += sums[k][:, :tcf]
            dwl_ref[pl.ds(k, 1), :] += sums[k][:, tcf:]
        dbg_ref[...] += sums[3][:, :tcf]
        dbl_ref[...] += sums[3][:, tcf:]

    half = pl.BlockSpec((s, tcf), lambda j, bb: (bb, j))
    taps = pl.BlockSpec((3, tcf), lambda j, bb: (0, j))
    bias = pl.BlockSpec((1, tcf), lambda j, bb: (0, j))
    return _call(body, name=name, grid=(n_f, bsz), in_specs=_ffn_specs(s, tcf, n_f, False) + [half],
                 out_specs=[half, half, taps, taps, bias, bias],
                 out_shape=[_sds((t, f_dim), BF16)] * 2 + [_sds((3, f_dim), F32)] * 2 + [_sds((1, f_dim), F32)] * 2,
                 scratch=[pltpu.VMEM((FFN_PAD + s, gw), F32), pltpu.VMEM((SUBLANES, gw), F32),
                          pltpu.VMEM((s + FFN_PAD, gw), F32)],
                 sem=("parallel", "arbitrary"))(hp, hp, w, w, b, b, df)


def _split3(v):
    hi = v.astype(BF16)
    r = v - hi.astype(F32)
    mid = r.astype(BF16)
    lo = (r - mid.astype(F32)).astype(BF16)
    return hi, mid, lo


def _tri_dot(tri, v):
    out = None
    for part in _split3(v):
        term = jnp.dot(tri, part, preferred_element_type=F32)
        out = term if out is None else out + term
    return out


def _fgate_fwd(zf, bsz, heads, *, name):
    t, lanes = zf.shape
    s, blk, nb = _seq_tiles(t, bsz, (ATTN_BLOCK, 128))

    def body(z_ref, cumt_ref, cumb_ref, carry_ref):
        @pl.when(pl.program_id(1) == 0)
        def _():
            carry_ref[...] = jnp.zeros_like(carry_ref)

        z = z_ref[...]
        lf = jnp.minimum(z, 0.0) - jnp.log1p(jnp.exp(-jnp.abs(z)))
        r = lax.broadcasted_iota(jnp.int32, (blk, blk), 0)
        c = lax.broadcasted_iota(jnp.int32, (blk, blk), 1)
        tri = (r >= c).astype(BF16)
        cum = _tri_dot(tri, lf) + carry_ref[...]
        carry_ref[...] = cum[blk - 1:blk, :]
        cumt_ref[0] = jnp.transpose(cum)[:heads, :]
        for h in range(heads):
            cumb_ref[0, h] = jnp.broadcast_to(cum[:, h:h + 1], (blk, lanes))

    return _call(body, name=name, grid=(bsz, nb),
                 in_specs=[pl.BlockSpec((blk, lanes), lambda b, i: (b * nb + i, 0))],
                 out_specs=[pl.BlockSpec((1, heads, blk), lambda b, i: (b, 0, i)),
                            pl.BlockSpec((1, heads, blk, lanes), lambda b, i: (b, 0, i, 0))],
                 out_shape=[_sds((bsz, heads, s), F32), _sds((bsz, heads, s, lanes), F32)],
                 scratch=[pltpu.VMEM((1, lanes), F32)], sem=("parallel", "arbitrary"))(zf)


def _fgate_bwd(dcum, zf, bsz, *, name):
    t, lanes = zf.shape
    pairs = dcum.shape[1]
    s, blk, nb = _seq_tiles(t, bsz, (ATTN_BLOCK, 128))

    def body(d_ref, z_ref, o_ref, carry_ref):
        @pl.when(pl.program_id(1) == 0)
        def _():
            carry_ref[...] = jnp.zeros_like(carry_ref)

        dcol = d_ref[0, 0]
        for p in range(1, pairs):
            dcol = dcol + d_ref[0, p]
        r = lax.broadcasted_iota(jnp.int32, (blk, blk), 0)
        c = lax.broadcasted_iota(jnp.int32, (blk, blk), 1)
        tri = (c >= r).astype(BF16)
        suf = _tri_dot(tri, dcol) + carry_ref[...]
        carry_ref[...] = suf[0:1, :]
        o_ref[...] = suf * _sigmoid(-z_ref[...])

    return _call(body, name=name, grid=(bsz, nb),
                 in_specs=[pl.BlockSpec((1, pairs, blk, lanes), lambda b, i: (b, 0, nb - 1 - i, 0)),
                           pl.BlockSpec((blk, lanes), lambda b, i: (b * nb + nb - 1 - i, 0))],
                 out_specs=pl.BlockSpec((blk, lanes), lambda b, i: (b * nb + nb - 1 - i, 0)),
                 out_shape=_sds((t, lanes), F32), scratch=[pltpu.VMEM((1, lanes), F32)],
                 sem=("parallel", "arbitrary"))(dcum, zf)


def _to_features_major(z, col_off, width, n, *, name):
    t = z.shape[0]
    tr = _pick(t, (512, 256, 128))
    first = col_off // width

    def body(*refs):
        o_ref = refs[n]
        for g in range(n):
            o_ref[pl.ds(g * width, width), :] = jnp.transpose(refs[g][...].astype(F32)).astype(o_ref.dtype)

    return _call(body, name=name, grid=(t // tr,),
                 in_specs=[pl.BlockSpec((tr, width), lambda i, g=g: (i, first + g)) for g in range(n)],
                 out_specs=pl.BlockSpec((n * width, tr), lambda i: (0, i)),
                 out_shape=_sds((n * width, t), BF16), sem=("parallel",))(*([z] * n))


def _to_rows_major(xt, *, name):
    w, t = xt.shape
    tr = _pick(t, (512, 256, 128))

    def body(x_ref, o_ref):
        o_ref[...] = jnp.transpose(x_ref[...]).astype(o_ref.dtype)

    return _call(body, name=name, grid=(t // tr,),
                 in_specs=[pl.BlockSpec((w, tr), lambda i: (0, i))],
                 out_specs=pl.BlockSpec((tr, w), lambda i: (i, 0)),
                 out_shape=_sds((t, w), BF16), sem=("parallel",))(xt)


def _loop_by_twos(lo, hi, body, carry):
    count = hi - lo

    def group(n, first, cr):
        for u in range(n):
            cr = body(first + u, cr)
        return cr

    trips = count // ATTN_UNROLL
    carry = lax.fori_loop(0, trips, lambda t, cr: group(ATTN_UNROLL, lo + ATTN_UNROLL * t, cr), carry)
    rest = count - ATTN_UNROLL * trips
    first = lo + ATTN_UNROLL * trips
    for n in range(ATTN_UNROLL - 1, 0, -1):
        carry = lax.cond(rest == n, lambda cr, n=n: group(n, first, cr), lambda cr: cr, carry)
    return carry


def _head_masks(shape, axis):
    feat = lax.broadcasted_iota(jnp.int32, shape, axis)
    return feat < HEAD_DIM, feat >= HEAD_DIM


def _attn_fwd(z, qkvt, cumt, cumb, bsz, heads, q_off, *, name):
    t = z.shape[0]
    width = heads * HEAD_DIM
    pairs = heads // 2
    s = t // bsz
    blk = ATTN_BLOCK
    nq = s // blk
    k_col = (q_off + width) // LANES
    v_row = 2 * width // LANES
    reps = blk // LANES

    def body(k_ref, qt_ref, vt_ref, cqt_ref, ckb_ref, ot_ref, lse_ref):
        p_id = pl.program_id(1)
        i = pl.program_id(2)
        qt = qt_ref[...]
        masks = _head_masks((LANES, blk), 0)
        qtm = [jnp.where(mk, qt, jnp.zeros_like(qt)) for mk in masks]
        cq = [cqt_ref[0, pl.ds(2 * p_id + hh, 1), :] for hh in range(2)]
        kidx = lax.broadcasted_iota(jnp.int32, (blk, blk), 0)
        qidx = lax.broadcasted_iota(jnp.int32, (blk, blk), 1)

        def block(j, carry, masked):
            off = pl.multiple_of(j * blk, blk)
            kp = k_ref[pl.ds(off, blk), :].astype(BF16)
            vtp = vt_ref[:, pl.ds(off, blk)]
            out = []
            for hh in range(2):
                m, l, acc = carry[hh]
                sc = jnp.dot(kp, qtm[hh], preferred_element_type=F32) * ATTN_SCALE
                ck = ckb_ref[0, hh, pl.ds(off, blk), :]
                sc = (sc + cq[hh]) - jnp.concatenate([ck] * reps, axis=1)
                if masked:
                    sc = jnp.where(qidx >= kidx, sc, NEG)
                m_new = jnp.maximum(m, jnp.max(sc, axis=0, keepdims=True))
                pr = jnp.exp(sc - m_new)
                a = jnp.exp(m - m_new)
                l = a * l + jnp.sum(pr, axis=0, keepdims=True)
                p_hi = pr.astype(BF16)
                p_lo = (pr - p_hi.astype(F32)).astype(BF16)
                pv = (jnp.dot(vtp, p_hi, preferred_element_type=F32)
                      + jnp.dot(vtp, p_lo, preferred_element_type=F32))
                acc = a * acc + pv[hh * HEAD_DIM:(hh + 1) * HEAD_DIM]
                out.append((m_new, l, acc))
            return tuple(out)

        init = tuple((jnp.full((1, blk), NEG, F32), jnp.zeros((1, blk), F32), jnp.zeros((HEAD_DIM, blk), F32))
                     for _ in range(2))
        carry = _loop_by_twos(0, i, lambda j, cr: block(j, cr, False), init)
        carry = block(i, carry, True)
        lse_ref[...] = jnp.zeros_like(lse_ref)
        for hh in range(2):
            m, l, acc = carry[hh]
            ot_ref[pl.ds(hh * HEAD_DIM, HEAD_DIM), :] = acc / l
            lse_ref[0, 0, pl.ds(hh, 1), :] = m + jnp.log(l)

    return _call(body, name=name, grid=(bsz, pairs, nq),
                 in_specs=[pl.BlockSpec((s, LANES), lambda b, p, i: (b, k_col + p)),
                           pl.BlockSpec((LANES, blk), lambda b, p, i: (p, b * nq + i)),
                           pl.BlockSpec((LANES, s), lambda b, p, i: (v_row + p, b)),
                           pl.BlockSpec((1, heads, blk), lambda b, p, i: (b, 0, i)),
                           pl.BlockSpec((1, 2, s, LANES), lambda b, p, i: (b, p, 0, 0))],
                 out_specs=[pl.BlockSpec((LANES, blk), lambda b, p, i: (p, b * nq + i)),
                            pl.BlockSpec((1, 1, SUBLANES, blk), lambda b, p, i: (b, p, 0, i))],
                 out_shape=[_sds((width, t), F32), _sds((bsz, pairs, SUBLANES, s), F32)],
                 sem=("parallel", "parallel", "parallel"))(z, qkvt, qkvt, cumt, cumb)


def _attn_bwd(z, qkvt, cumt, cumb, ot, do, dot, lse, bsz, heads, q_off, *, name):
    t = z.shape[0]
    width = heads * HEAD_DIM
    pairs = heads // 2
    s = t // bsz
    blk = ATTN_BLOCK
    nkv = s // blk
    q_col = q_off // LANES
    k_col = (q_off + width) // LANES
    v_col = (q_off + 2 * width) // LANES
    k_row = width // LANES
    reps = blk // LANES

    def body(k_ref, v_ref, kt_ref, q_ref, qt_ref, do_ref, dot_ref, ot_ref, lse_ref, ckb_ref, cqt_ref,
             dk_ref, dv_ref, dqt_ref, dcum_ref, dqt_acc, ds_acc):
        p_id = pl.program_id(1)
        j = pl.program_id(2)

        @pl.when(j == 0)
        def _():
            dqt_acc[...] = jnp.zeros_like(dqt_acc)

        kp = k_ref[...].astype(BF16)
        vp = v_ref[...].astype(BF16)
        kt = kt_ref[...]
        feat_masks = _head_masks((LANES, blk), 0)
        lane_masks = _head_masks((blk, LANES), 1)
        ktm = [jnp.where(mk, kt, jnp.zeros_like(kt)) for mk in feat_masks]
        ck = [jnp.concatenate([ckb_ref[0, hh]] * reps, axis=1) for hh in range(2)]
        kidx = lax.broadcasted_iota(jnp.int32, (blk, blk), 0)
        qidx = lax.broadcasted_iota(jnp.int32, (blk, blk), 1)
        ds_acc[...] = jnp.zeros_like(ds_acc)

        def block(i, carry, masked):
            dk, dv = carry
            off = pl.multiple_of(i * blk, blk)
            qt = qt_ref[:, pl.ds(off, blk)]
            dt = dot_ref[:, pl.ds(off, blk)]
            o_t = ot_ref[:, pl.ds(off, blk)]
            q_rows = q_ref[pl.ds(off, blk), :].astype(BF16)
            do_rows = do_ref[pl.ds(off, blk), :]
            for hh in range(2):
                qtm = jnp.where(feat_masks[hh], qt, jnp.zeros_like(qt))
                dtm = jnp.where(feat_masks[hh], dt, jnp.zeros_like(dt))
                sc = jnp.dot(kp, qtm, preferred_element_type=F32) * ATTN_SCALE
                sc = (sc + cqt_ref[0, pl.ds(2 * p_id + hh, 1), pl.ds(off, blk)]) - ck[hh]
                pr = jnp.exp(sc - lse_ref[0, 0, pl.ds(hh, 1), pl.ds(off, blk)])
                if masked:
                    pr = jnp.where(qidx >= kidx, pr, 0.0)
                dp = jnp.dot(vp, dtm, preferred_element_type=F32)
                delta = jnp.sum(dtm.astype(F32) * o_t, axis=0, keepdims=True)
                ds = pr * (dp - delta)
                ds_acc[hh] += ds
                dsb = ds.astype(BF16)
                qm = jnp.where(lane_masks[hh], q_rows, jnp.zeros_like(q_rows))
                dom = jnp.where(lane_masks[hh], do_rows, jnp.zeros_like(do_rows))
                dv = dv + jnp.dot(pr.astype(BF16), dom, preferred_element_type=F32)
                dk = dk + jnp.dot(dsb, qm, preferred_element_type=F32) * ATTN_SCALE
                dqt_acc[:, pl.ds(off, blk)] += jnp.dot(ktm[hh], dsb, preferred_element_type=F32) * ATTN_SCALE
            return dk, dv

        zero = jnp.zeros((blk, LANES), F32)
        carry = block(j, (zero, zero), True)
        dk, dv = _loop_by_twos(j + 1, nkv, lambda i, cr: block(i, cr, False), carry)
        dk_ref[...] = dk.astype(dk_ref.dtype)
        dv_ref[...] = dv.astype(dv_ref.dtype)
        lane = lax.broadcasted_iota(jnp.int32, (blk, LANES), 1)
        dcum = jnp.zeros((blk, LANES), F32)
        for hh in range(2):
            col = jnp.sum(ds_acc[hh], axis=1, keepdims=True)
            dcum = jnp.where(lane == 2 * p_id + hh, -col, dcum)
        dcum_ref[0, 0] = dcum

        @pl.when(j == nkv - 1)
        def _():
            dqt_ref[...] = dqt_acc[...]

    key_rows = lambda col: pl.BlockSpec((blk, LANES), lambda b, p, j: (b * nkv + j, col + p))
    seq_t = lambda row: pl.BlockSpec((LANES, s), lambda b, p, j: (row + p, b))
    return _call(body, name=name, grid=(bsz, pairs, nkv),
                 in_specs=[key_rows(k_col), key_rows(v_col),
                           pl.BlockSpec((LANES, blk), lambda b, p, j: (k_row + p, b * nkv + j)),
                           pl.BlockSpec((s, LANES), lambda b, p, j: (b, q_col + p)), seq_t(0),
                           pl.BlockSpec((s, LANES), lambda b, p, j: (b, p)), seq_t(0), seq_t(0),
                           pl.BlockSpec((1, 1, SUBLANES, s), lambda b, p, j: (b, p, 0, 0)),
                           pl.BlockSpec((1, 2, blk, LANES), lambda b, p, j: (b, p, j, 0)),
                           pl.BlockSpec((1, heads, s), lambda b, p, j: (b, 0, 0))],
                 out_specs=[key_rows(0), key_rows(0), seq_t(0),
                            pl.BlockSpec((1, 1, blk, LANES), lambda b, p, j: (b, p, j, 0))],
                 out_shape=[_sds((t, width), BF16), _sds((t, width), BF16), _sds((width, t), F32),
                            _sds((bsz, pairs, s, LANES), F32)],
                 scratch=[pltpu.VMEM((LANES, s), F32), pltpu.VMEM((2, blk, blk), F32)],
                 sem=("parallel", "parallel", "arbitrary"))(z, z, qkvt, z, qkvt, do, dot, ot, lse, cumb, cumt)


def _adamw(w, g, m, v, *, name):
    bc1 = 1.0 - ADAM_B1 ** ADAM_STEP
    bc2 = 1.0 - ADAM_B2 ** ADAM_STEP

    def body(w_ref, g_ref, m_ref, v_ref, d_ref, nm_ref, nv_ref):
        g_v = g_ref[...]
        nm = ADAM_B1 * m_ref[...] + (1.0 - ADAM_B1) * g_v
        nv = ADAM_B2 * v_ref[...] + (1.0 - ADAM_B2) * (g_v * g_v)
        nm_ref[...] = nm
        nv_ref[...] = nv
        d_ref[...] = -ADAM_LR * ((nm / bc1) / (jnp.sqrt(nv / bc2) + ADAM_EPS) + ADAM_WD * w_ref[...])

    if w.ndim == 2:
        grid = (1,)
        blk = pl.BlockSpec(w.shape, lambda i: (0, 0))
    else:
        layers, rows, cols = w.shape
        tr = rows if rows <= 256 else _pick(rows, (256, 128, 64, 32, 16, 8))
        grid = (layers, rows // tr)
        blk = pl.BlockSpec((1, tr, cols), lambda layer, i: (layer, i, 0))
    return tuple(_call(body, name=name, grid=grid, in_specs=[blk] * 4, out_specs=[blk] * 3,
                       out_shape=[_sds(w.shape, F32)] * 3, sem=("parallel",) * len(grid))(w, g, m, v))


_ANY = pl.BlockSpec(memory_space=pl.ANY)


def _comm_call(body, *, name, n_in, out_shape, n_sems):
    scratch = [pltpu.SemaphoreType.DMA((n_sems,)), pltpu.SemaphoreType.DMA((n_sems,)),
               pltpu.SemaphoreType.DMA((len(out_shape),))]
    return pl.pallas_call(body, name=name, in_specs=[_ANY] * n_in, out_specs=[_ANY] * len(out_shape),
                          out_shape=out_shape, scratch_shapes=scratch)


def _place():
    x, y, c = lax.axis_index("x"), lax.axis_index("y"), lax.axis_index("c")
    return x, y, c, [(1 - x, y), (x, 1 - y), (1 - x, 1 - y)]


def _remote(src, dst, send_sems, recv_sems, sem, to):
    return pltpu.make_async_remote_copy(src_ref=src, dst_ref=dst, send_sem=send_sems.at[sem],
                                        recv_sem=recv_sems.at[sem], device_id=to, device_id_type=MESH)


def _all_gather8(v, *, name):
    def body(v_ref, out_ref, send_sems, recv_sems, local_sems):
        x, y, c, _ = _place()
        me = 4 * x + 2 * y + c
        mine = pltpu.make_async_copy(v_ref, out_ref.at[me], local_sems.at[0])
        mine.start()
        peers = []
        for k in range(1, N_DEVICES):
            px = 1 - x if k & 4 else x
            py = 1 - y if k & 2 else y
            pc = 1 - c if k & 1 else c
            peers.append((px, py, pc))
        sends = [_remote(v_ref, out_ref.at[me], send_sems, recv_sems, k, peer) for k, peer in enumerate(peers)]
        for cp in sends:
            cp.start()
        for k, (px, py, pc) in enumerate(peers):
            _remote(v_ref, out_ref.at[4 * px + 2 * py + pc], send_sems, recv_sems, k, (px, py, pc)).wait_recv()
        for cp in sends:
            cp.wait_send()
        mine.wait()

    out = _comm_call(body, name=name, n_in=1, out_shape=[_sds((N_DEVICES,) + v.shape, v.dtype)],
                     n_sems=N_DEVICES - 1)(v)
    return out[0]


def _window(ref, mode, layer, chip, rows, cols, half=None):
    first, count = (0, rows) if half is None else (half * (rows // 2), rows // 2)
    if mode == "slab":
        return ref.at[layer, chip] if half is None else ref.at[layer, chip, pl.ds(first, count), :]
    if mode == "cols":
        col_window = pl.ds(pl.multiple_of(chip * cols, LANES), cols)
        return ref.at[layer, :, col_window] if half is None else ref.at[layer, pl.ds(first, count), col_window]
    return ref.at[layer, pl.ds(pl.multiple_of(chip * rows + first, SUBLANES), count), :]


def _whole_shape(mode, shard_shape):
    layers, rows, cols = shard_shape
    if mode == "slab":
        return (layers, N_CHIPS, rows, cols)
    if mode == "cols":
        assert cols % LANES == 0
        return (layers, rows, N_CHIPS * cols)
    assert rows % 16 == 0
    return (layers, N_CHIPS * rows, cols)


def _gather_weights(shards, modes, *, name):
    n = len(shards)
    meta = [(mode,) + tuple(a.shape[1:]) for a, mode in zip(shards, modes)]
    for a in shards:
        assert a.shape[0] == 2 and a.shape[1] % 2 == 0
    per = 8

    def body(*refs):
        ins, outs = refs[:n], refs[n:2 * n]
        send_sems, recv_sems, _ = refs[2 * n:]
        x, y, c, _ = _place()
        me, x_nbr, y_nbr, diagonal = 2 * x + y, 2 * (1 - x) + y, 2 * x + 1 - y, 2 * (1 - x) + 1 - y
        to_x, to_y, sibling = (1 - x, y, c), (x, 1 - y, c), (x, y, 1 - c)
        sent = []

        def copy(src, dst, sem, to):
            cp = _remote(src, dst, send_sems, recv_sems, sem, to)
            cp.start()
            sent.append(cp)

        def arrived(win, sem):
            _remote(win, win, send_sems, recv_sems, sem, sibling).wait_recv()

        for i, (mode, rows, cols) in enumerate(meta):
            mine = _window(outs[i], mode, c, me, rows, cols)
            copy(ins[i].at[c], mine, per * i, to_x)
            copy(ins[i].at[c], mine, per * i + 1, to_y)
            copy(ins[i], _window(outs[i], mode, slice(None), me, rows, cols), per * i + 7, sibling)
        for i, (mode, rows, cols) in enumerate(meta):
            arrived(_window(outs[i], mode, c, x_nbr, rows, cols), per * i)
            half = _window(outs[i], mode, c, x_nbr, rows, cols, half=0)
            copy(half, half, per * i + 2, to_y)
            win = _window(outs[i], mode, c, x_nbr, rows, cols)
            copy(win, win, per * i + 4, sibling)
            arrived(_window(outs[i], mode, c, y_nbr, rows, cols), per * i + 1)
            half = _window(outs[i], mode, c, y_nbr, rows, cols, half=1)
            copy(half, half, per * i + 3, to_x)
            win = _window(outs[i], mode, c, y_nbr, rows, cols)
            copy(win, win, per * i + 5, sibling)
        for i, (mode, rows, cols) in enumerate(meta):
            arrived(_window(outs[i], mode, c, diagonal, rows, cols, half=0), per * i + 2)
            arrived(_window(outs[i], mode, c, diagonal, rows, cols, half=1), per * i + 3)
            win = _window(outs[i], mode, c, diagonal, rows, cols)
            copy(win, win, per * i + 6, sibling)
        for i, (mode, rows, cols) in enumerate(meta):
            arrived(_window(outs[i], mode, slice(None), me, rows, cols), per * i + 7)
            for k, chip in enumerate((x_nbr, y_nbr, diagonal)):
                arrived(_window(outs[i], mode, 1 - c, chip, rows, cols), per * i + 4 + k)
        for cp in sent:
            cp.wait_send()

    out_shape = [_sds(_whole_shape(mode, a.shape), a.dtype) for a, mode in zip(shards, modes)]
    return _comm_call(body, name=name, n_in=n, out_shape=out_shape, n_sems=per * n)(*shards)


def _rs_swap(grads, *, name):
    n = len(grads)

    def body(*refs):
        ins, outs = refs[:n], refs[n:2 * n]
        send_sems, recv_sems, _ = refs[2 * n:]
        x, y, c, _ = _place()
        copies = [_remote(ins[i].at[1 - c], outs[i], send_sems, recv_sems, i, (x, y, 1 - c)) for i in range(n)]
        for cp in copies:
            cp.start()
        for cp in copies:
            cp.wait()

    return _comm_call(body, name=name, n_in=n, out_shape=[_sds(g.shape[1:], g.dtype) for g in grads], n_sems=n)(*grads)


def _part(ref, mode, chip, rows, cols):
    if mode == "slab":
        return ref.at[chip]
    if mode == "cols":
        return ref.at[:, pl.ds(pl.multiple_of(chip * cols, LANES), cols)]
    return ref.at[pl.ds(pl.multiple_of(chip * rows, SUBLANES), rows), :]


def _rs_scatter(parts, modes, shard_shapes, *, name):
    n = len(parts)
    meta = [(mode,) + tuple(shp[1:]) for mode, shp in zip(modes, shard_shapes)]

    def body(*refs):
        ins, outs = refs[:n], refs[n:2 * n]
        send_sems, recv_sems, local_sems = refs[2 * n:]
        x, y, c, chips = _place()
        me = 2 * x + y
        local, sends = [], []
        for i, (mode, rows, cols) in enumerate(meta):
            cp = pltpu.make_async_copy(_part(ins[i], mode, me, rows, cols), outs[i].at[me], local_sems.at[i])
            cp.start()
            local.append(cp)
            for r, (cx, cy) in enumerate(chips):
                cp = _remote(_part(ins[i], mode, 2 * cx + cy, rows, cols), outs[i].at[me], send_sems, recv_sems,
                             3 * i + r, (cx, cy, c))
                cp.start()
                sends.append(cp)
        for i, (mode, rows, cols) in enumerate(meta):
            for r, (cx, cy) in enumerate(chips):
                k = 2 * cx + cy
                _remote(_part(ins[i], mode, k, rows, cols), outs[i].at[k], send_sems, recv_sems, 3 * i + r,
                        (cx, cy, c)).wait_recv()
        for cp in sends:
            cp.wait_send()
        for cp in local:
            cp.wait()

    out_shape = [_sds((N_CHIPS,) + tuple(shp[1:]), p.dtype) for p, shp in zip(parts, shard_shapes)]
    return _comm_call(body, name=name, n_in=n, out_shape=out_shape, n_sems=3 * n)(*parts)


def _rs_exchange(sums, *, name):
    n = len(sums)

    def body(*refs):
        ins, outs = refs[:n], refs[n:2 * n]
        send_sems, recv_sems, _ = refs[2 * n:]
        x, y, c, _ = _place()
        copies = [_remote(ins[i], outs[i], send_sems, recv_sems, i, (x, y, 1 - c)) for i in range(n)]
        for cp in copies:
            cp.start()
        for cp in copies:
            cp.wait()

    return _comm_call(body, name=name, n_in=n, out_shape=[_sds(s.shape, s.dtype) for s in sums], n_sems=n)(*sums)


def _row_tile(rows, cols, itemsize):
    target = max(SUBLANES, (2 << 20) // (cols * itemsize))
    cands = [c for c in (2048, 1024, 512, 256, 128, 64, 32, 16) if c <= target]
    tr = _pick(rows, cands)
    return tr


def _add_layer(g, other, core, *, name):
    _, rows, cols = g.shape
    tr = _row_tile(rows, cols, 4)

    def body(core_ref, g_ref, o_ref, out_ref):
        out_ref[...] = (g_ref[0] + o_ref[...]).astype(out_ref.dtype)

    grid_spec = pltpu.PrefetchScalarGridSpec(
        num_scalar_prefetch=1, grid=(rows // tr,),
        in_specs=[pl.BlockSpec((1, tr, cols), lambda i, core_ref: (core_ref[0], i, 0)),
                  pl.BlockSpec((tr, cols), lambda i, core_ref: (i, 0))],
        out_specs=pl.BlockSpec((tr, cols), lambda i, core_ref: (i, 0)))
    return pl.pallas_call(body, name=name, grid_spec=grid_spec, out_shape=_sds((rows, cols), BF16),
                          compiler_params=pltpu.CompilerParams(dimension_semantics=("parallel",),
                                                               vmem_limit_bytes=VMEM_LIMIT))(core, g, other)


def _sum_slots(parts, *, name):
    n, rows, cols = parts.shape
    tr = _row_tile(rows, cols, 4)

    def body(p_ref, o_ref):
        acc = p_ref[0].astype(F32) + p_ref[1].astype(F32)
        for k in range(2, n):
            acc = acc + p_ref[k].astype(F32)
        o_ref[...] = acc

    return _call(body, name=name, grid=(rows // tr,),
                 in_specs=[pl.BlockSpec((n, tr, cols), lambda i: (0, i, 0))],
                 out_specs=pl.BlockSpec((tr, cols), lambda i: (i, 0)),
                 out_shape=_sds((rows, cols), F32), sem=("parallel",))(parts)


def _reduce_scatter(grads, modes, shard_shapes):
    core = lax.axis_index("c").astype(jnp.int32).reshape(1)
    flat = [g.reshape(g.shape[0], -1, g.shape[-1]) for g in grads]
    from_sibling = _rs_swap(flat, name="rs_swap")
    parts = []
    for i, (g, o) in enumerate(zip(flat, from_sibling)):
        p = _add_layer(g, o, core, name=f"rs_add_{i}")
        parts.append(p.reshape(grads[i].shape[1:]))
    from_chips = _rs_scatter(parts, modes, shard_shapes, name="rs_scatter")
    sums = [_sum_slots(r, name=f"rs_sum_{i}") for i, r in enumerate(from_chips)]
    others = _rs_exchange(sums, name="rs_exchange")
    mine_first = lax.axis_index("c") == 0
    return [jnp.where(mine_first, jnp.stack([mine, other]), jnp.stack([other, mine]))
            for mine, other in zip(sums, others)]


def _layer_weights(full, rep, layer, dims):
    f_off, n_heads = dims["f_off"], dims["heads"]
    b_in = rep["b_in"][layer]
    pad = LANES - n_heads
    return {
        "w_main": full["w_main"][layer],
        "b_main": jnp.concatenate([b_in[:f_off], b_in[f_off + n_heads:]])[None],
        "w_f": full["w_f"][layer],
        "b_f": jnp.pad(b_in[f_off:f_off + n_heads], (0, pad))[None],
        "conv_a_w": full["conv_a_w"][layer],
        "conv_a_b": rep["conv_a_b"][layer][None],
        "ln_conv_g": rep["ln_conv_g"][layer][None],
        "ln_conv_b": rep["ln_conv_b"][layer][None],
        "w_conv_proj": full["w_conv_proj"][layer],
        "w_attn_proj": full["w_attn_proj"][layer],
        "w_mix_out": full["w_mix_out"][layer],
        "b_mix_out": rep["b_mix_out"][layer][None],
        "ln1_g": rep["ln1_g"][layer][None],
        "ln1_b": rep["ln1_b"][layer][None],
        "w_ffn_up": full["w_ffn_up"][layer],
        "ffn_conv_w": full["ffn_conv_w"][layer],
        "ffn_conv_b": rep["ffn_conv_b"][layer][None],
        "w_ffn_down": full["w_ffn_down"][layer],
        "ln2_g": rep["ln2_g"][layer][None],
        "ln2_b": rep["ln2_b"][layer][None],
    }


def _layer_fwd(x, mod, p, dims, tag, target=None):
    bsz, d, ch, heads, alpha = dims["bsz"], dims["d"], dims["ch"], dims["heads"], dims["alpha"]
    mods = [mod[:, k * d:(k + 1) * d][:, None, :] for k in range(6)]
    shift1, scale1, gate1, shift2, scale2, gate2 = mods
    u = _ln_mod_fwd(x, scale1, shift1, bsz, name=f"ln_mod1_{tag}")
    zm = _matmul(u, p["w_main"], "nn", BF16, bias=p["b_main"], name=f"in_main_{tag}")
    zf = _matmul(u, p["w_f"], "nn", F32, bias=p["b_f"], name=f"in_forget_{tag}")
    a3 = _conv_branch_fwd(zm, p["conv_a_w"], p["conv_a_b"], p["ln_conv_g"], p["ln_conv_b"], bsz, ch,
                          name=f"conv_branch_{tag}")
    ya = _matmul(a3, p["w_conv_proj"], "nn", BF16, name=f"conv_proj_{tag}")
    cumt, cumb = _fgate_fwd(zf, bsz, heads, name=f"fgate_{tag}")
    qkvt = _to_features_major(zm, 2 * ch, heads * HEAD_DIM, 3, name=f"qkv_t_{tag}")
    ot, lse = _attn_fwd(zm, qkvt, cumt, cumb, bsz, heads, 2 * ch, name=f"attn_{tag}")
    yb = _matmul(ot, p["w_attn_proj"], "tn", BF16, name=f"attn_proj_{tag}")
    m = _gate_merge_fwd(zm, ya, yb, dims["ga_off"], name=f"merge_{tag}")
    mix = _matmul(m, p["w_mix_out"], "nn", F32, bias=p["b_mix_out"], name=f"mix_out_{tag}")
    x1, u2 = _ln_res_mod_fwd(x, mix, gate1, p["ln1_g"], p["ln1_b"], scale2, shift2, alpha, bsz,
                             name=f"ln_res1_mod2_{tag}")
    hp = _matmul(u2, p["w_ffn_up"], "nn", BF16, name=f"ffn_up_{tag}")
    f = _ffn_act_fwd(hp, p["ffn_conv_w"], p["ffn_conv_b"], bsz, dims["tcf"], name=f"ffn_act_{tag}")
    ffn = _matmul(f, p["w_ffn_down"], "nn", F32, name=f"ffn_down_{tag}")
    if target is None:
        x2 = _ln_res_fwd(x1, ffn, gate2, p["ln2_g"], p["ln2_b"], alpha, bsz, name=f"ln_res2_{tag}")
    else:
        x2 = _ln_res_loss(x1, ffn, gate2, p["ln2_g"], p["ln2_b"], target, alpha, bsz, name=f"ln_res2_loss_{tag}")
    saved = dict(x=x, mods=mods, u=u, zm=zm, zf=zf, a3=a3, ya=ya, yb=yb, cumt=cumt, cumb=cumb,
                 qkvt=qkvt, ot=ot, lse=lse, m=m, mix=mix, x1=x1, u2=u2, hp=hp, f=f, ffn=ffn)
    return x2, saved


def _layer_bwd(dx2, p, sv, dims, tag):
    bsz, ch, heads, alpha = dims["bsz"], dims["ch"], dims["heads"], dims["alpha"]
    f_off, tcf = dims["f_off"], dims["tcf"]
    shift1, scale1, gate1, shift2, scale2, gate2 = sv["mods"]
    g = {}
    dr2, dffn, dgate2, g["ln2_g"], g["ln2_b"], _ = _ln_res_bwd(
        dx2, sv["x1"], sv["ffn"], gate2, p["ln2_g"], alpha, bsz, name=f"ln_res2_bwd_{tag}")
    df = _matmul(dffn, p["w_ffn_down"], "nt", BF16, name=f"ffn_down_dx_{tag}")
    g["w_ffn_down"] = _matmul(sv["f"], dffn, "tn", F32, name=f"ffn_down_dw_{tag}")
    dhg, dhl, dwg, dwl, dbg, dbl = _ffn_act_bwd(sv["hp"], df, p["ffn_conv_w"], p["ffn_conv_b"], bsz, tcf,
                                                name=f"ffn_act_bwd_{tag}")
    g["ffn_conv_w"] = jnp.concatenate([dwg, dwl], axis=1)
    g["ffn_conv_b"] = jnp.concatenate([dbg, dbl], axis=1)[0]
    du2 = _matmul(dhg, p["w_ffn_up"], "nt", F32, name=f"ffn_up_gate_dx_{tag}")
    du2 = _matmul(dhl, p["w_ffn_up"], "nt", F32, add=du2, b_k_first=dhg.shape[1], name=f"ffn_up_lin_dx_{tag}")
    d_ff = dhg.shape[1]
    dw_up = _matmul(sv["u2"], dhg, "tn", F32, out_cols=2 * d_ff, name=f"ffn_up_gate_dw_{tag}")
    g["w_ffn_up"] = _matmul(sv["u2"], dhl, "tn", F32, into=(dw_up, d_ff), name=f"ffn_up_lin_dw_{tag}")
    dr1, dmix, dgate1, g["ln1_g"], g["ln1_b"], g["b_mix_out"], dscale2, dshift2 = _ln_mod_res_bwd(
        du2, dr2, scale2, sv["x"], sv["mix"], gate1, p["ln1_g"], p["ln1_b"], alpha, bsz,
        name=f"ln_mod2_res1_bwd_{tag}")
    dm = _matmul(dmix, p["w_mix_out"], "nt", BF16, name=f"mix_out_dx_{tag}")
    g["w_mix_out"] = _matmul(sv["m"], dmix, "tn", F32, name=f"mix_out_dw_{tag}")
    dya, dyb, dzga, dzgb = _gate_merge_bwd(sv["zm"], sv["ya"], sv["yb"], dm, dims["ga_off"], name=f"merge_bwd_{tag}")
    da3 = _matmul(dya, p["w_conv_proj"], "nt", F32, name=f"conv_proj_dx_{tag}")
    g["w_conv_proj"] = _matmul(sv["a3"], dya, "tn", F32, name=f"conv_proj_dw_{tag}")
    do = _matmul(dyb, p["w_attn_proj"], "nt", BF16, name=f"attn_proj_dx_{tag}")
    dot = _matmul(p["w_attn_proj"], dyb, "nt", BF16, name=f"attn_proj_dxt_{tag}")
    g["w_attn_proj"] = _matmul(sv["ot"], dyb, "nn", F32, name=f"attn_proj_dw_{tag}")
    dzglu, g["conv_a_w"], dcb, g["ln_conv_g"], g["ln_conv_b"] = _conv_branch_bwd(
        sv["zm"], da3, p["conv_a_w"], p["conv_a_b"], p["ln_conv_g"], p["ln_conv_b"], bsz, ch,
        name=f"conv_branch_bwd_{tag}")
    g["conv_a_b"] = dcb[0]
    dk, dv, dqt, dcum = _attn_bwd(sv["zm"], sv["qkvt"], sv["cumt"], sv["cumb"], sv["ot"], do, dot, sv["lse"], bsz,
                                  heads, 2 * ch, name=f"attn_bwd_{tag}")
    dq = _to_rows_major(dqt, name=f"dq_rows_{tag}")
    dzf = _fgate_bwd(dcum, sv["zf"], bsz, name=f"fgate_bwd_{tag}")
    dzm = jnp.concatenate([dzglu, dq, dk, dv, dzga, dzgb], axis=1)
    du = _matmul(dzm, p["w_main"], "nt", F32, name=f"in_main_dx_{tag}")
    dwm, dbm = _matmul(sv["u"], dzm, "tn", F32, colsum=True, name=f"in_main_dw_{tag}")
    dwf, dbf = _matmul(sv["u"], dzf, "tn", F32, colsum=True, name=f"in_forget_dw_{tag}")
    dbm, dbf = dbm[0], dbf[0]
    g["w_main"], g["w_f"] = dwm, dwf
    g["b_in"] = jnp.concatenate([dbm[:f_off], dbf[:heads], dbm[f_off:]])
    dx, dscale1, dshift1 = _ln_mod_bwd(du, dzf, p["w_f"], sv["x"], scale1, dr1, alpha, bsz,
                                       name=f"ln_mod1_bwd_{tag}")
    dmod = jnp.concatenate([dshift1, dscale1, dgate1, dshift2, dscale2, dgate2], axis=2)[:, 0, :]
    return dx, g, dmod


def _local_step(x, mod, loss_target, full, rep, dims):
    bsz, seq, d = x.shape
    layers = mod.shape[0]
    params = [_layer_weights(full, rep, layer, dims) for layer in range(layers)]
    h = x.reshape(bsz * seq, d)
    saved = []
    for layer in range(layers):
        target = loss_target.reshape(bsz * seq, d) if layer == layers - 1 else None
        h, sv = _layer_fwd(h, mod[layer], params[layer], dims, f"l{layer}", target)
        saved.append(sv)
    dh, sq = h
    loss_local = 0.5 * jnp.sum(sq) / d
    grads, dmods = [None] * layers, [None] * layers
    for layer in reversed(range(layers)):
        dh, grads[layer], dmods[layer] = _layer_bwd(dh, params[layer], saved[layer], dims, f"l{layer}")
    per_layer = ("w_main", "w_f")
    stacked = {wname: [grads[layer][wname] for layer in range(layers)] if wname in per_layer
               else jnp.stack([grads[layer][wname] for layer in range(layers)]) for wname in grads[0]}
    return loss_local, dh.reshape(bsz, seq, d), stacked, jnp.stack(dmods)


def _pad_rows(a):
    extra = -a.shape[-2] % (2 * SUBLANES)
    if extra == 0:
        return a
    return jnp.pad(a, [(0, 0)] * (a.ndim - 2) + [(0, extra), (0, 0)])


def _w_in_pieces(n, f_off, heads):
    n_in = N_CHIPS * n
    segments = [(0, f_off, "main", 0), (f_off, f_off + heads, "f", 0), (f_off + heads, n_in, "main", f_off)]
    pieces = []
    for chip in range(N_CHIPS):
        lo, hi = chip * n, (chip + 1) * n
        for a, b, target, t0 in segments:
            s, e = max(lo, a), min(hi, b)
            if s < e:
                pieces.append((chip, s - lo, e - lo, target, t0 + s - a))
    return pieces


def _w_in_from_slabs(slabs, f_off, heads, *, name):
    layers, _, k, n = slabs.shape
    tr = _pick(k, (256, 128, 64, 32, 16))
    n_main = N_CHIPS * n - heads
    pieces = _w_in_pieces(n, f_off, heads)

    def body(s_ref, m_ref, f_ref):
        f_ref[...] = jnp.zeros_like(f_ref)
        for chip in range(N_CHIPS):
            slab = s_ref[0, chip].astype(F32)
            for pc, s0, s1, target, t0 in pieces:
                if pc == chip:
                    out = m_ref if target == "main" else f_ref
                    out[0, :, t0:t0 + s1 - s0] = slab[:, s0:s1].astype(out.dtype)

    return _call(body, name=name, grid=(layers, k // tr),
                 in_specs=[pl.BlockSpec((1, N_CHIPS, tr, n), lambda layer, i: (layer, 0, i, 0))],
                 out_specs=[pl.BlockSpec((1, tr, n_main), lambda layer, i: (layer, i, 0)),
                            pl.BlockSpec((1, tr, LANES), lambda layer, i: (layer, i, 0))],
                 out_shape=[_sds((layers, k, n_main), slabs.dtype), _sds((layers, k, LANES), slabs.dtype)],
                 sem=("parallel", "parallel"))(slabs)


def _w_in_to_slabs(d_main, d_f, n, f_off, heads, *, name):
    layers = len(d_main)
    k = d_main[0].shape[0]
    tr = _pick(k, (128, 64, 32, 16, 8))
    pieces = _w_in_pieces(n, f_off, heads)

    def body(*refs):
        m_refs, f_refs, o_ref = refs[:layers], refs[layers:2 * layers], refs[2 * layers]
        for layer in range(layers):
            for chip, s0, s1, target, t0 in pieces:
                src = m_refs[layer] if target == "main" else f_refs[layer]
                o_ref[layer, chip, :, s0:s1] = src[:, t0:t0 + s1 - s0]

    return _call(body, name=name, grid=(k // tr,),
                 in_specs=[pl.BlockSpec((tr, d_main[0].shape[1]), lambda i: (i, 0))] * layers
                 + [pl.BlockSpec((tr, LANES), lambda i: (i, 0))] * layers,
                 out_specs=pl.BlockSpec((layers, N_CHIPS, tr, n), lambda i: (0, 0, i, 0)),
                 out_shape=_sds((layers, N_CHIPS, k, n), F32), sem=("parallel",))(*d_main, *d_f)


def kernel(x, c, w_ada, b_ada, w_in, b_in, conv_a_w, conv_a_b, ln_conv_g, ln_conv_b, w_conv_proj, w_attn_proj, w_mix_out, b_mix_out, ln1_g, ln1_b, w_ffn_up, ffn_conv_w, ffn_conv_b, w_ffn_down, ln2_g, ln2_b, loss_target, m_w_ada, m_b_ada, m_w_in, m_b_in, m_conv_a_w, m_conv_a_b, m_ln_conv_g, m_ln_conv_b, m_w_conv_proj, m_w_attn_proj, m_w_mix_out, m_b_mix_out, m_ln1_g, m_ln1_b, m_w_ffn_up, m_ffn_conv_w, m_ffn_conv_b, m_w_ffn_down, m_ln2_g, m_ln2_b, v_w_ada, v_b_ada, v_w_in, v_b_in, v_conv_a_w, v_conv_a_b, v_ln_conv_g, v_ln_conv_b, v_w_conv_proj, v_w_attn_proj, v_w_mix_out, v_b_mix_out, v_ln1_g, v_ln1_b, v_w_ffn_up, v_ffn_conv_w, v_ffn_conv_b, v_w_ffn_down, v_ln2_g, v_ln2_b):
    weights = dict(zip(WEIGHTS, (w_ada, b_ada, w_in, b_in, conv_a_w, conv_a_b, ln_conv_g, ln_conv_b, w_conv_proj,
                                 w_attn_proj, w_mix_out, b_mix_out, ln1_g, ln1_b, w_ffn_up, ffn_conv_w, ffn_conv_b,
                                 w_ffn_down, ln2_g, ln2_b)))
    mom1 = dict(zip(WEIGHTS, (m_w_ada, m_b_ada, m_w_in, m_b_in, m_conv_a_w, m_conv_a_b, m_ln_conv_g, m_ln_conv_b,
                              m_w_conv_proj, m_w_attn_proj, m_w_mix_out, m_b_mix_out, m_ln1_g, m_ln1_b, m_w_ffn_up,
                              m_ffn_conv_w, m_ffn_conv_b, m_w_ffn_down, m_ln2_g, m_ln2_b)))
    mom2 = dict(zip(WEIGHTS, (v_w_ada, v_b_ada, v_w_in, v_b_in, v_conv_a_w, v_conv_a_b, v_ln_conv_g, v_ln_conv_b,
                              v_w_conv_proj, v_w_attn_proj, v_w_mix_out, v_b_mix_out, v_ln1_g, v_ln1_b, v_w_ffn_up,
                              v_ffn_conv_w, v_ffn_conv_b, v_w_ffn_down, v_ln2_g, v_ln2_b)))
    bsz, seq, d = x.shape
    layers = w_ada.shape[0]
    ch = conv_a_w.shape[2] * N_CHIPS
    width = w_attn_proj.shape[1]
    heads = width // HEAD_DIM
    d_ff = w_ffn_down.shape[1] * N_CHIPS
    dims = dict(bsz=bsz, d=d, ch=ch, heads=heads, alpha=(2.0 * layers) ** 0.25, f_off=2 * ch + 3 * width,
                ga_off=2 * ch + 3 * width, tcf=_pick(d_ff, (256, 128)))
    chip = 2 * lax.axis_index("x") + lax.axis_index("y")
    device = 2 * chip + lax.axis_index("c")
    ada_cols = w_ada.shape[2]

    c_act = _silu_rows(_all_gather8(c, name="gather_c").reshape(N_DEVICES * bsz, d), name="silu_c")
    b_ada_mine = lax.dynamic_slice_in_dim(b_ada, chip * ada_cols, ada_cols, axis=1)
    mod_cols = jnp.stack([_matmul(c_act, w_ada[layer], "nn", F32, bias=b_ada_mine[layer][None], name=f"ada_l{layer}")
                          for layer in range(layers)])
    mod_all = _all_gather8(mod_cols, name="gather_mod")
    mod_all = jnp.concatenate([mod_all[2 * k] for k in range(N_CHIPS)], axis=-1)
    mod = lax.dynamic_slice_in_dim(mod_all, device * bsz, bsz, axis=1)

    shards = [_pad_rows(weights[wname].astype(BF16) if as_bf16 else weights[wname]) for wname, _, as_bf16 in GATHERED]
    modes = [mode for _, mode, _ in GATHERED]
    whole = _gather_weights(shards, modes, name="gather_weights")
    full = {wname: w[:, :weights[wname].shape[1]] if mode == "cols" else w
            for (wname, mode, _), w in zip(GATHERED, whole)}
    full["w_main"], full["w_f"] = _w_in_from_slabs(full.pop("w_in"), dims["f_off"], heads, name="w_in_from_slabs")
    rep = {wname: weights[wname] for wname in REPLICATED}

    loss_local, grad_x, grads, dmod = _local_step(x, mod, loss_target, full, rep, dims)
    loss = lax.psum(loss_local, ("x", "y", "c"))

    grads["w_in"] = _w_in_to_slabs(grads.pop("w_main"), grads.pop("w_f"), w_in.shape[2], dims["f_off"], heads,
                                   name="w_in_to_slabs")
    shard_shapes = [s.shape for s in shards]
    reduced = _reduce_scatter([_pad_rows(grads[wname]) for wname, _, _ in GATHERED], modes, shard_shapes)
    grad = {wname: r[:, :weights[wname].shape[1]] for (wname, _, _), r in zip(GATHERED, reduced)}

    small = jnp.concatenate([dmod.reshape(-1)] + [grads[wname].reshape(-1) for wname in REPLICATED])
    n_small = small.shape[0]
    rows = -(-n_small // (SUBLANES * LANES)) * SUBLANES
    small = jnp.pad(small, (0, rows * LANES - n_small)).reshape(rows, LANES)
    gathered = _all_gather8(small, name="gather_small")
    n_dmod = dmod.size
    dmod_all = gathered.reshape(N_DEVICES, -1)[:, :n_dmod].reshape(N_DEVICES, layers, bsz, 6 * d)
    dmod_all = jnp.transpose(dmod_all, (1, 0, 2, 3)).reshape(layers, N_DEVICES * bsz, 6 * d)
    summed = _sum_slots(gathered, name="sum_small").reshape(-1)
    off = n_dmod
    for wname in REPLICATED:
        n = weights[wname].size
        grad[wname] = summed[off:off + n].reshape(weights[wname].shape)
        off += n
    dmod_mine = lax.dynamic_slice_in_dim(dmod_all, chip * ada_cols, ada_cols, axis=2)
    grad["w_ada"] = jnp.stack([_matmul(c_act, dmod_mine[layer], "tn", F32, name=f"ada_dw_l{layer}")
                               for layer in range(layers)])
    grad["b_ada"] = jnp.stack([_colsum(dmod_all[layer], name=f"ada_db_l{layer}")[0] for layer in range(layers)])

    delta, new_m, new_v = {}, {}, {}
    for wname in WEIGHTS:
        delta[wname], new_m[wname], new_v[wname] = _adamw(weights[wname], grad[wname], mom1[wname], mom2[wname],
                                                          name=f"adamw_{wname}")
    return (loss, grad_x, *[grad[wname] for wname in WEIGHTS], *[delta[wname] for wname in WEIGHTS],
            *[new_m[wname] for wname in WEIGHTS], *[new_v[wname] for wname in WEIGHTS])
```

```python
import math

import jax
import jax.numpy as jnp
from jax import lax
from jax.experimental import pallas as pl
from jax.experimental.pallas import tpu as pltpu

F32 = jnp.float32
BF16 = jnp.bfloat16
MESH = pl.DeviceIdType.MESH

LN_EPS = 1e-5
HEAD_DIM = 64
ATTN_SCALE = HEAD_DIM ** -0.5
NEG = -1e30
FFN_PAD = 8
LANES = 128
SUBLANES = 8
ROW_CHUNK = 256
ATTN_BLOCK = 256
ATTN_UNROLL = 4
N_CHIPS = 4
N_DEVICES = 8
VMEM_LIMIT = 56 * 1024 * 1024

ADAM_LR = 0.001
ADAM_B1 = 0.9
ADAM_B2 = 0.999
ADAM_EPS = 1e-08
ADAM_WD = 0.01
ADAM_STEP = 10

GATHERED = (("w_in", "slab", True), ("conv_a_w", "cols", False), ("w_conv_proj", "cols", True),
            ("w_attn_proj", "cols", True), ("w_mix_out", "rows", True), ("w_ffn_up", "cols", True),
            ("ffn_conv_w", "cols", False), ("w_ffn_down", "rows", True))
REPLICATED = ("b_in", "conv_a_b", "ln_conv_g", "ln_conv_b", "b_mix_out", "ln1_g", "ln1_b",
              "ffn_conv_b", "ln2_g", "ln2_b")
WEIGHTS = ("w_ada", "b_ada", "w_in", "b_in", "conv_a_w", "conv_a_b", "ln_conv_g", "ln_conv_b",
           "w_conv_proj", "w_attn_proj", "w_mix_out", "b_mix_out", "ln1_g", "ln1_b", "w_ffn_up",
           "ffn_conv_w", "ffn_conv_b", "w_ffn_down", "ln2_g", "ln2_b")


def _pick(n, cands):
    for cand in cands:
        if n % cand == 0:
            return cand
    return n


def _call(body, *, name, grid, in_specs, out_specs, out_shape, scratch=(), sem=None):
    return pl.pallas_call(
        body, name=name, grid=grid, in_specs=in_specs, out_specs=out_specs, out_shape=out_shape,
        scratch_shapes=list(scratch),
        compiler_params=pltpu.CompilerParams(dimension_semantics=sem, vmem_limit_bytes=VMEM_LIMIT))


def _sds(shape, dtype):
    return jax.ShapeDtypeStruct(tuple(shape), dtype)


def _chunked(rows, fn):
    chunk = min(ROW_CHUNK, rows)
    if rows == chunk:
        fn(pl.ds(0, rows))
        return

    def step(i, carry):
        fn(pl.ds(pl.multiple_of(i * chunk, chunk), chunk))
        return carry

    lax.fori_loop(0, rows // chunk, step, 0)


def _matmul(a, b, mode, out_dtype, *, bias=None, add=None, colsum=False, b_k_first=0, out_cols=None, into=None,
            name):
    a, a_layer = a if isinstance(a, tuple) else (a, None)
    b, b_layer = b if isinstance(b, tuple) else (b, None)
    if mode == "nn":
        (m, k), (_, n) = a.shape[-2:], b.shape[-2:]
    elif mode == "nt":
        (m, k), (n, _) = a.shape[-2:], b.shape[-2:]
    else:
        (k, m), (_, n) = a.shape[-2:], b.shape[-2:]
    tm = _pick(m, (1024, 1408, 512, 256, 128))
    tn = _pick(n, (1536, 1408, 1024, 512, 256, 128))
    tk = k if k <= 1536 else _pick(k, (1024, 1536, 1408, 512, 256, 128))
    nk = k // tk
    assert b_k_first % tk == 0 and (b_k_first == 0 or mode == "nt")
    k_blk0 = b_k_first // tk

    def spec(layer, shape, index):
        if layer is None:
            return pl.BlockSpec(shape, index)
        return pl.BlockSpec((None,) + shape, lambda i, j, kk: (layer,) + index(i, j, kk))

    if mode == "nn":
        a_spec = spec(a_layer, (tm, tk), lambda i, j, kk: (i, kk))
        b_spec = spec(b_layer, (tk, tn), lambda i, j, kk: (kk, j))
        dims = (((1,), (0,)), ((), ()))
    elif mode == "nt":
        a_spec = spec(a_layer, (tm, tk), lambda i, j, kk: (i, kk))
        b_spec = spec(b_layer, (tn, tk), lambda i, j, kk: (j, k_blk0 + kk))
        dims = (((1,), (1,)), ((), ()))
    else:
        a_spec = spec(a_layer, (tk, tm), lambda i, j, kk: (kk, i))
        b_spec = spec(b_layer, (tk, tn), lambda i, j, kk: (kk, j))
        dims = (((0,), (0,)), ((), ()))
    in_specs = [a_spec, b_spec]
    operands = [a, b]
    if bias is not None:
        in_specs.append(pl.BlockSpec((1, tn), lambda i, j, kk: (0, j)))
        operands.append(bias)
    if add is not None:
        in_specs.append(pl.BlockSpec((tm, tn), lambda i, j, kk: (i, j)))
        operands.append(add)

    def body(a_ref, b_ref, *rest):
        rest = list(rest)
        bias_ref = rest.pop(0) if bias is not None else None
        add_ref = rest.pop(0) if add is not None else None
        if into is not None:
            rest.pop(0)
        o_ref = rest.pop(0)
        prod = lax.dot_general(a_ref[...].astype(BF16), b_ref[...].astype(BF16), dims,
                               preferred_element_type=F32)
        if colsum:
            cs_ref = rest.pop(0)
            part = jnp.sum(b_ref[...].astype(F32), axis=0, keepdims=True)

            @pl.when(pl.program_id(2) == 0)
            def _():
                cs_ref[...] = part

            @pl.when(pl.program_id(2) > 0)
            def _():
                cs_ref[...] += part

        def finish(r):
            if bias_ref is not None:
                r = r + bias_ref[...]
            if add_ref is not None:
                r = r + add_ref[...]
            o_ref[...] = r.astype(o_ref.dtype)

        if nk == 1:
            finish(prod)
            return
        acc_ref = rest.pop(0)
        kk = pl.program_id(2)

        @pl.when(kk == 0)
        def _():
            acc_ref[...] = prod

        @pl.when(kk > 0)
        def _():
            acc_ref[...] += prod

        @pl.when(kk == nk - 1)
        def _():
            finish(acc_ref[...])

    col_blk0 = 0
    aliases = {}
    out_shape = _sds((m, n if out_cols is None else out_cols), out_dtype)
    if into is not None:
        wide, col_first = into
        assert col_first % tn == 0 and not colsum
        col_blk0 = col_first // tn
        out_shape = _sds(wide.shape, wide.dtype)
        aliases = {len(operands): 0}
        in_specs.append(pl.BlockSpec(memory_space=pl.ANY))
        operands.append(wide)
    out_specs = pl.BlockSpec((tm, tn), lambda i, j, kk: (i, col_blk0 + j))
    if colsum:
        assert mode == "tn" and m == tm
        out_specs = [out_specs, pl.BlockSpec((1, tn), lambda i, j, kk: (0, j))]
        out_shape = [out_shape, _sds((1, n), F32)]
    return pl.pallas_call(
        body, name=name, grid=(m // tm, n // tn, nk), in_specs=in_specs, out_specs=out_specs, out_shape=out_shape,
        scratch_shapes=[pltpu.VMEM((tm, tn), F32)] if nk > 1 else [], input_output_aliases=aliases,
        compiler_params=pltpu.CompilerParams(dimension_semantics=("parallel", "parallel", "arbitrary"),
                                             vmem_limit_bytes=VMEM_LIMIT))(*operands)


def _colsum(x, *, name):
    rows, n = x.shape
    tr = _pick(rows, (1024, 512, 256, 128))
    tn = _pick(n, (512, 256, 128))

    def body(x_ref, o_ref):
        @pl.when(pl.program_id(1) == 0)
        def _():
            o_ref[...] = jnp.zeros_like(o_ref)

        o_ref[...] += jnp.sum(x_ref[...].astype(F32), axis=0, keepdims=True)

    return _call(body, name=name, grid=(n // tn, rows // tr),
                 in_specs=[pl.BlockSpec((tr, tn), lambda j, i: (i, j))],
                 out_specs=pl.BlockSpec((1, tn), lambda j, i: (0, j)),
                 out_shape=_sds((1, n), F32), sem=("parallel", "arbitrary"))(x)


def _ln_stats(x):
    mu = jnp.mean(x, axis=-1, keepdims=True)
    xc = x - mu
    var = jnp.mean(xc * xc, axis=-1, keepdims=True)
    rstd = lax.rsqrt(var + LN_EPS)
    return xc * rstd, rstd


def _ln_bwd(dn, n, rstd):
    return rstd * (dn - jnp.mean(dn, axis=-1, keepdims=True) - n * jnp.mean(dn * n, axis=-1, keepdims=True))


def _seq_tiles(t, bsz, cands=(1024, 512, 256, 128, 64, 32, 16, 8)):
    s = t // bsz
    ts = _pick(s, cands)
    return s, ts, s // ts


def _ln_mod_fwd(x, scale, shift, bsz, *, name):
    t, d = x.shape
    _, ts, ns = _seq_tiles(t, bsz)

    def body(x_ref, sc_ref, sh_ref, u_ref):
        one_scale = 1.0 + sc_ref[0]
        shift_v = sh_ref[0]

        def piece(rows):
            n, _ = _ln_stats(x_ref[rows, :])
            u_ref[rows, :] = (n * one_scale + shift_v).astype(u_ref.dtype)

        _chunked(ts, piece)

    row = pl.BlockSpec((ts, d), lambda b, i: (b * ns + i, 0))
    per = pl.BlockSpec((1, 1, d), lambda b, i: (b, 0, 0))
    return _call(body, name=name, grid=(bsz, ns), in_specs=[row, per, per], out_specs=row,
                 out_shape=_sds((t, d), BF16), sem=("parallel", "parallel"))(x, scale, shift)


def _ln_mod_bwd(du, dz_narrow, w_narrow, x, scale, dr, alpha, bsz, *, name):
    t, d = x.shape
    _, ts, ns = _seq_tiles(t, bsz)
    lanes = dz_narrow.shape[1]

    def body(du_ref, dzn_ref, wn_ref, x_ref, sc_ref, dr_ref, dx_ref, dsc_ref, dsh_ref):
        @pl.when(pl.program_id(1) == 0)
        def _():
            dsc_ref[...] = jnp.zeros_like(dsc_ref)
            dsh_ref[...] = jnp.zeros_like(dsh_ref)

        one_scale = 1.0 + sc_ref[0]
        w_n = wn_ref[...].astype(BF16)

        def piece(rows):
            du_v = du_ref[rows, :] + lax.dot_general(dzn_ref[rows, :].astype(BF16), w_n, (((1,), (1,)), ((), ())),
                                                     preferred_element_type=F32)
            n, rstd = _ln_stats(x_ref[rows, :])
            dsc_ref[0] += jnp.sum(du_v * n, axis=0, keepdims=True)
            dsh_ref[0] += jnp.sum(du_v, axis=0, keepdims=True)
            dx_ref[rows, :] = alpha * dr_ref[rows, :] + _ln_bwd(du_v * one_scale, n, rstd)

        _chunked(ts, piece)

    row = pl.BlockSpec((ts, d), lambda b, i: (b * ns + i, 0))
    per = pl.BlockSpec((1, 1, d), lambda b, i: (b, 0, 0))
    return _call(body, name=name, grid=(bsz, ns),
                 in_specs=[row, pl.BlockSpec((ts, lanes), lambda b, i: (b * ns + i, 0)),
                           pl.BlockSpec((d, lanes), lambda b, i: (0, 0)), row, per, row],
                 out_specs=[row, per, per],
                 out_shape=[_sds((t, d), F32), _sds((bsz, 1, d), F32), _sds((bsz, 1, d), F32)],
                 sem=("parallel", "arbitrary"))(du, dz_narrow, w_narrow, x, scale, dr)


def _ln_res_fwd(x, y, gate, g, b, alpha, bsz, *, name):
    t, d = x.shape
    _, ts, ns = _seq_tiles(t, bsz)

    def body(x_ref, y_ref, gt_ref, g_ref, b_ref, o_ref):
        one_gate = 1.0 + gt_ref[0]

        def piece(rows):
            n, _ = _ln_stats(alpha * x_ref[rows, :] + one_gate * y_ref[rows, :])
            o_ref[rows, :] = n * g_ref[...] + b_ref[...]

        _chunked(ts, piece)

    row = pl.BlockSpec((ts, d), lambda bb, i: (bb * ns + i, 0))
    per = pl.BlockSpec((1, 1, d), lambda bb, i: (bb, 0, 0))
    vec = pl.BlockSpec((1, d), lambda bb, i: (0, 0))
    return _call(body, name=name, grid=(bsz, ns), in_specs=[row, row, per, vec, vec], out_specs=row,
                 out_shape=_sds((t, d), F32), sem=("parallel", "parallel"))(x, y, gate, g, b)


def _ln_res_bwd(do, x, y, gate, g, alpha, bsz, *, name):
    t, d = x.shape
    _, ts, ns = _seq_tiles(t, bsz)

    def body(do_ref, x_ref, y_ref, gt_ref, g_ref, dr_ref, dy_ref, dgt_ref, dg_ref, db_ref, dys_ref):
        first_tile = pl.program_id(1) == 0

        @pl.when(first_tile)
        def _():
            dgt_ref[...] = jnp.zeros_like(dgt_ref)

        @pl.when(jnp.logical_and(first_tile, pl.program_id(0) == 0))
        def _():
            dg_ref[...] = jnp.zeros_like(dg_ref)
            db_ref[...] = jnp.zeros_like(db_ref)
            dys_ref[...] = jnp.zeros_like(dys_ref)

        one_gate = 1.0 + gt_ref[0]

        def piece(rows):
            do_v = do_ref[rows, :]
            y_v = y_ref[rows, :]
            n, rstd = _ln_stats(alpha * x_ref[rows, :] + one_gate * y_v)
            dg_ref[...] += jnp.sum(do_v * n, axis=0, keepdims=True)
            db_ref[...] += jnp.sum(do_v, axis=0, keepdims=True)
            dr = _ln_bwd(do_v * g_ref[...], n, rstd)
            dr_ref[rows, :] = dr
            dy = one_gate * dr
            dy_ref[rows, :] = dy.astype(dy_ref.dtype)
            dys_ref[...] += jnp.sum(dy, axis=0, keepdims=True)
            dgt_ref[0] += jnp.sum(dr * y_v, axis=0, keepdims=True)

        _chunked(ts, piece)

    row = pl.BlockSpec((ts, d), lambda bb, i: (bb * ns + i, 0))
    per = pl.BlockSpec((1, 1, d), lambda bb, i: (bb, 0, 0))
    vec = pl.BlockSpec((1, d), lambda bb, i: (0, 0))
    return _call(body, name=name, grid=(bsz, ns), in_specs=[row, row, row, per, vec],
                 out_specs=[row, row, per, vec, vec, vec],
                 out_shape=[_sds((t, d), F32), _sds((t, d), BF16), _sds((bsz, 1, d), F32),
                            _sds((1, d), F32), _sds((1, d), F32), _sds((1, d), F32)],
                 sem=("arbitrary", "arbitrary"))(do, x, y, gate, g)


def _ln_res_mod_fwd(x, y, gate, g, b, scale, shift, alpha, bsz, *, name):
    t, d = x.shape
    _, ts, ns = _seq_tiles(t, bsz)

    def body(x_ref, y_ref, gt_ref, g_ref, b_ref, sc_ref, sh_ref, o_ref, u_ref):
        one_gate = 1.0 + gt_ref[0]
        one_scale = 1.0 + sc_ref[0]
        shift_v = sh_ref[0]

        def piece(rows):
            n, _ = _ln_stats(alpha * x_ref[rows, :] + one_gate * y_ref[rows, :])
            x1 = n * g_ref[...] + b_ref[...]
            o_ref[rows, :] = x1
            n1, _ = _ln_stats(x1)
            u_ref[rows, :] = (n1 * one_scale + shift_v).astype(u_ref.dtype)

        _chunked(ts, piece)

    row = pl.BlockSpec((ts, d), lambda bb, i: (bb * ns + i, 0))
    per = pl.BlockSpec((1, 1, d), lambda bb, i: (bb, 0, 0))
    vec = pl.BlockSpec((1, d), lambda bb, i: (0, 0))
    return _call(body, name=name, grid=(bsz, ns), in_specs=[row, row, per, vec, vec, per, per],
                 out_specs=[row, row], out_shape=[_sds((t, d), F32), _sds((t, d), BF16)],
                 sem=("parallel", "parallel"))(x, y, gate, g, b, scale, shift)


def _ln_mod_res_bwd(du, dr_up, scale, x, y, gate, g, b, alpha, bsz, *, name):
    t, d = x.shape
    _, ts, ns = _seq_tiles(t, bsz, (512, 256, 128, 64, 32, 16, 8))

    def body(du_ref, up_ref, sc_ref, x_ref, y_ref, gt_ref, g_ref, b_ref,
             dr_ref, dy_ref, dgt_ref, dg_ref, db_ref, dys_ref, dsc_ref, dsh_ref):
        first_tile = pl.program_id(1) == 0

        @pl.when(first_tile)
        def _():
            for ref in (dgt_ref, dsc_ref, dsh_ref):
                ref[...] = jnp.zeros_like(ref)

        @pl.when(jnp.logical_and(first_tile, pl.program_id(0) == 0))
        def _():
            for ref in (dg_ref, db_ref, dys_ref):
                ref[...] = jnp.zeros_like(ref)

        one_gate = 1.0 + gt_ref[0]
        one_scale = 1.0 + sc_ref[0]

        def piece(rows):
            y_v = y_ref[rows, :]
            n, rstd = _ln_stats(alpha * x_ref[rows, :] + one_gate * y_v)
            n1, rstd1 = _ln_stats(n * g_ref[...] + b_ref[...])
            du_v = du_ref[rows, :]
            dsc_ref[0] += jnp.sum(du_v * n1, axis=0, keepdims=True)
            dsh_ref[0] += jnp.sum(du_v, axis=0, keepdims=True)
            dx1 = alpha * up_ref[rows, :] + _ln_bwd(du_v * one_scale, n1, rstd1)
            dg_ref[...] += jnp.sum(dx1 * n, axis=0, keepdims=True)
            db_ref[...] += jnp.sum(dx1, axis=0, keepdims=True)
            dr = _ln_bwd(dx1 * g_ref[...], n, rstd)
            dr_ref[rows, :] = dr
            dy = one_gate * dr
            dy_ref[rows, :] = dy.astype(dy_ref.dtype)
            dys_ref[...] += jnp.sum(dy, axis=0, keepdims=True)
            dgt_ref[0] += jnp.sum(dr * y_v, axis=0, keepdims=True)

        _chunked(ts, piece)

    row = pl.BlockSpec((ts, d), lambda bb, i: (bb * ns + i, 0))
    per = pl.BlockSpec((1, 1, d), lambda bb, i: (bb, 0, 0))
    vec = pl.BlockSpec((1, d), lambda bb, i: (0, 0))
    return _call(body, name=name, grid=(bsz, ns), in_specs=[row, row, per, row, row, per, vec, vec],
                 out_specs=[row, row, per, vec, vec, vec, per, per],
                 out_shape=[_sds((t, d), F32), _sds((t, d), BF16), _sds((bsz, 1, d), F32), _sds((1, d), F32),
                            _sds((1, d), F32), _sds((1, d), F32), _sds((bsz, 1, d), F32), _sds((bsz, 1, d), F32)],
                 sem=("arbitrary", "arbitrary"))(du, dr_up, scale, x, y, gate, g, b)


def _ln_res_loss(x, y, gate, g, b, target, alpha, bsz, *, name):
    t, d = x.shape
    _, ts, ns = _seq_tiles(t, bsz)

    def body(x_ref, y_ref, gt_ref, g_ref, b_ref, t_ref, dy_ref, s_ref):
        @pl.when(jnp.logical_and(pl.program_id(0) == 0, pl.program_id(1) == 0))
        def _():
            s_ref[...] = jnp.zeros_like(s_ref)

        one_gate = 1.0 + gt_ref[0]

        def piece(rows):
            n, _ = _ln_stats(alpha * x_ref[rows, :] + one_gate * y_ref[rows, :])
            e = (n * g_ref[...] + b_ref[...]) - t_ref[rows, :]
            dy_ref[rows, :] = e * (1.0 / d)
            s_ref[...] += jnp.sum(e * e, axis=0, keepdims=True)

        _chunked(ts, piece)

    row = pl.BlockSpec((ts, d), lambda bb, i: (bb * ns + i, 0))
    per = pl.BlockSpec((1, 1, d), lambda bb, i: (bb, 0, 0))
    vec = pl.BlockSpec((1, d), lambda bb, i: (0, 0))
    return _call(body, name=name, grid=(bsz, ns), in_specs=[row, row, per, vec, vec, row], out_specs=[row, vec],
                 out_shape=[_sds((t, d), F32), _sds((1, d), F32)],
                 sem=("arbitrary", "arbitrary"))(x, y, gate, g, b, target)


def _sigmoid(v):
    return 1.0 / (1.0 + jnp.exp(-v))


def _silu_rows(c, *, name):
    rows, d = c.shape

    def body(c_ref, o_ref):
        v = c_ref[...]
        o_ref[...] = (v * _sigmoid(v)).astype(o_ref.dtype)

    full = pl.BlockSpec((rows, d), lambda i: (0, 0))
    return _call(body, name=name, grid=(1,), in_specs=[full], out_specs=full,
                 out_shape=_sds((rows, d), BF16), sem=("arbitrary",))(c)


def _gate_cols(d, ga_off):
    tc = _pick(math.gcd(d, ga_off), (512, 256, 128))
    return tc, ga_off // tc, (ga_off + d) // tc


def _gate_merge_fwd(z, ya, yb, ga_off, *, name):
    t, d = ya.shape
    tr = _pick(t, (1024, 512, 256, 128, 64, 32, 16, 8))
    tc, ga_blk, gb_blk = _gate_cols(d, ga_off)

    def body(ga_ref, gb_ref, ya_ref, yb_ref, o_ref):
        def piece(rows):
            o_ref[rows, :] = (_sigmoid(ga_ref[rows, :].astype(F32)) * ya_ref[rows, :].astype(F32)
                              + _sigmoid(gb_ref[rows, :].astype(F32)) * yb_ref[rows, :].astype(F32)
                              ).astype(o_ref.dtype)

        _chunked(tr, piece)

    blk = pl.BlockSpec((tr, tc), lambda i, j: (i, j))
    return _call(body, name=name, grid=(t // tr, d // tc),
                 in_specs=[pl.BlockSpec((tr, tc), lambda i, j: (i, ga_blk + j)),
                           pl.BlockSpec((tr, tc), lambda i, j: (i, gb_blk + j)), blk, blk],
                 out_specs=blk, out_shape=_sds((t, d), BF16), sem=("parallel", "parallel"))(z, z, ya, yb)


def _gate_merge_bwd(z, ya, yb, dm, ga_off, *, name):
    t, d = ya.shape
    tr = _pick(t, (1024, 512, 256, 128, 64, 32, 16, 8))
    tc, ga_blk, gb_blk = _gate_cols(d, ga_off)

    def body(ga_ref, gb_ref, ya_ref, yb_ref, dm_ref, dya_ref, dyb_ref, dga_ref, dgb_ref):
        def piece(rows):
            dm_v = dm_ref[rows, :].astype(F32)
            sa = _sigmoid(ga_ref[rows, :].astype(F32))
            sb = _sigmoid(gb_ref[rows, :].astype(F32))
            dya_ref[rows, :] = (dm_v * sa).astype(dya_ref.dtype)
            dyb_ref[rows, :] = (dm_v * sb).astype(dyb_ref.dtype)
            dga_ref[rows, :] = (dm_v * ya_ref[rows, :].astype(F32) * sa * (1.0 - sa)).astype(dga_ref.dtype)
            dgb_ref[rows, :] = (dm_v * yb_ref[rows, :].astype(F32) * sb * (1.0 - sb)).astype(dgb_ref.dtype)

        _chunked(tr, piece)

    blk = pl.BlockSpec((tr, tc), lambda i, j: (i, j))
    return _call(body, name=name, grid=(t // tr, d // tc),
                 in_specs=[pl.BlockSpec((tr, tc), lambda i, j: (i, ga_blk + j)),
                           pl.BlockSpec((tr, tc), lambda i, j: (i, gb_blk + j)), blk, blk, blk],
                 out_specs=[blk, blk, blk, blk], out_shape=[_sds((t, d), BF16)] * 4,
                 sem=("parallel", "parallel"))(z, z, ya, yb, dm)


CONV_ROWS = 64
CONV_PAD = 32


def _row_shifts(win):
    total = win.shape[0]
    return [win] + [pltpu.roll(win, total - b, axis=0) for b in range(1, SUBLANES)]


def _shifted_rows(copies, shift):
    start = SUBLANES * (shift // SUBLANES)
    return copies[shift % SUBLANES][start:start + CONV_ROWS]


def _fill_glu(z_ref, ext_ref, s, ch):
    ext_ref[pl.ds(0, CONV_PAD), :] = jnp.zeros((CONV_PAD, ch), F32)

    chunk = min(ROW_CHUNK, s)

    def piece(i, carry):
        start = pl.multiple_of(i * chunk, chunk)
        zz = z_ref[pl.ds(start, chunk), :].astype(F32)
        ext_ref[pl.ds(pl.multiple_of(CONV_PAD + start, CONV_PAD), chunk), :] = zz[:, :ch] * _sigmoid(zz[:, ch:])
        return carry

    lax.fori_loop(0, s // chunk, piece, 0)


def _conv_piece(ext_ref, w_ref, cb_ref, base, kw):
    copies = _row_shifts(ext_ref[pl.ds(base, CONV_ROWS + CONV_PAD), :])
    acc = cb_ref[...] + w_ref[pl.ds(0, 1), :] * _shifted_rows(copies, CONV_PAD - (kw - 1))
    for k in range(1, kw):
        acc = acc + w_ref[pl.ds(k, 1), :] * _shifted_rows(copies, CONV_PAD - (kw - 1) + k)
    return acc, copies


def _conv_branch_fwd(z, w, cb, lg, lb, bsz, ch, *, name):
    t = z.shape[0]
    s = t // bsz
    kw = w.shape[0]

    def body(z_ref, w_ref, cb_ref, lg_ref, lb_ref, o_ref, ext_ref):
        _fill_glu(z_ref, ext_ref, s, ch)

        def step(i, carry):
            base = pl.multiple_of(i * CONV_ROWS, CONV_ROWS)
            a1, _ = _conv_piece(ext_ref, w_ref, cb_ref, base, kw)
            n, _ = _ln_stats(a1)
            a2 = n * lg_ref[...] + lb_ref[...]
            o_ref[pl.ds(base, CONV_ROWS), :] = (a2 * _sigmoid(a2)).astype(o_ref.dtype)
            return carry

        lax.fori_loop(0, s // CONV_ROWS, step, 0)

    vec = pl.BlockSpec((1, ch), lambda b: (0, 0))
    return _call(body, name=name, grid=(bsz,),
                 in_specs=[pl.BlockSpec((s, 2 * ch), lambda b: (b, 0)), pl.BlockSpec((kw, ch), lambda b: (0, 0)),
                           vec, vec, vec],
                 out_specs=pl.BlockSpec((s, ch), lambda b: (b, 0)), out_shape=_sds((t, ch), BF16),
                 scratch=[pltpu.VMEM((CONV_PAD + s, ch), F32)], sem=("parallel",))(z, w, cb, lg, lb)


def _conv_branch_bwd(z, da3, w, cb, lg, lb, bsz, ch, *, name):
    t = z.shape[0]
    s = t // bsz
    kw = w.shape[0]
    n_rows = CONV_ROWS + CONV_PAD

    def body(z_ref, d_ref, w_ref, cb_ref, lg_ref, lb_ref, dz_ref, dw_ref, dcb_ref, dlg_ref, dlb_ref,
             ext_ref, da1_ref):
        @pl.when(pl.program_id(0) == 0)
        def _():
            for ref in (dw_ref, dcb_ref, dlg_ref, dlb_ref):
                ref[...] = jnp.zeros_like(ref)

        _fill_glu(z_ref, ext_ref, s, ch)
        da1_ref[pl.ds(s, CONV_PAD), :] = jnp.zeros((CONV_PAD, ch), F32)

        def grad_a1(i, carry):
            dlg, dlb = carry
            base = pl.multiple_of(i * CONV_ROWS, CONV_ROWS)
            a1, _ = _conv_piece(ext_ref, w_ref, cb_ref, base, kw)
            n, rstd = _ln_stats(a1)
            a2 = n * lg_ref[...] + lb_ref[...]
            sg = _sigmoid(a2)
            da2 = d_ref[pl.ds(base, CONV_ROWS), :] * (sg * (1.0 + a2 * (1.0 - sg)))
            da1_ref[pl.ds(base, CONV_ROWS), :] = _ln_bwd(da2 * lg_ref[...], n, rstd)
            return (dlg + jnp.sum(da2 * n, axis=0, keepdims=True), dlb + jnp.sum(da2, axis=0, keepdims=True))

        zero = jnp.zeros((1, ch), F32)
        dlg, dlb = lax.fori_loop(0, s // CONV_ROWS, grad_a1, (zero, zero))
        dlg_ref[...] += dlg
        dlb_ref[...] += dlb

        def grad_z(i, dcb):
            base = pl.multiple_of(i * CONV_ROWS, CONV_ROWS)
            ahead = _row_shifts(da1_ref[pl.ds(base, n_rows), :])
            dyc = ahead[0][:CONV_ROWS]
            da0 = w_ref[pl.ds(kw - 1, 1), :] * dyc
            for k in range(kw - 1):
                da0 = da0 + w_ref[pl.ds(k, 1), :] * _shifted_rows(ahead, kw - 1 - k)
            behind = _row_shifts(ext_ref[pl.ds(base, n_rows), :])
            for k in range(kw):
                dw_ref[pl.ds(k, 1), :] += jnp.sum(dyc * _shifted_rows(behind, CONV_PAD - (kw - 1) + k),
                                                  axis=0, keepdims=True)
            zz = z_ref[pl.ds(base, CONV_ROWS), :].astype(F32)
            sg = _sigmoid(zz[:, ch:])
            dz_ref[pl.ds(base, CONV_ROWS), :ch] = (da0 * sg).astype(dz_ref.dtype)
            dz_ref[pl.ds(base, CONV_ROWS), ch:] = (da0 * zz[:, :ch] * sg * (1.0 - sg)).astype(dz_ref.dtype)
            return dcb + jnp.sum(dyc, axis=0, keepdims=True)

        dcb_ref[...] += lax.fori_loop(0, s // CONV_ROWS, grad_z, zero)

    vec = pl.BlockSpec((1, ch), lambda b: (0, 0))
    taps = pl.BlockSpec((kw, ch), lambda b: (0, 0))
    return _call(body, name=name, grid=(bsz,),
                 in_specs=[pl.BlockSpec((s, 2 * ch), lambda b: (b, 0)), pl.BlockSpec((s, ch), lambda b: (b, 0)),
                           taps, vec, vec, vec],
                 out_specs=[pl.BlockSpec((s, 2 * ch), lambda b: (b, 0)), taps, vec, vec, vec],
                 out_shape=[_sds((t, 2 * ch), BF16), _sds((kw, ch), F32)] + [_sds((1, ch), F32)] * 3,
                 scratch=[pltpu.VMEM((CONV_PAD + s, ch), F32), pltpu.VMEM((s + CONV_PAD, ch), F32)],
                 sem=("arbitrary",))(z, da3, w, cb, lg, lb)


FFN_ROWS = 64


def _gelu_parts(v):
    cdf = 0.5 * (1.0 + lax.erf(v * (2.0 ** -0.5)))
    return cdf, v * cdf


def _ffn_conv_piece(ext_ref, wb_ref, base):
    win = ext_ref[pl.ds(base, FFN_ROWS + FFN_PAD), :]
    acc = wb_ref[pl.ds(3, 1), :] + wb_ref[pl.ds(2, 1), :] * win[FFN_PAD:]
    acc = acc + wb_ref[pl.ds(1, 1), :] * pltpu.roll(win, 1, axis=0)[FFN_PAD:]
    acc = acc + wb_ref[pl.ds(0, 1), :] * pltpu.roll(win, 2, axis=0)[FFN_PAD:]
    return acc


def _ffn_stage(hg_ref, hl_ref, wg_ref, wl_ref, bg_ref, bl_ref, ext_ref, wb_ref, s, tcf):
    ext_ref[pl.ds(0, FFN_PAD), :] = jnp.zeros((FFN_PAD, 2 * tcf), F32)
    ext_ref[pl.ds(FFN_PAD, s), :tcf] = hg_ref[...].astype(F32)
    ext_ref[pl.ds(FFN_PAD, s), tcf:] = hl_ref[...].astype(F32)
    wb_ref[pl.ds(0, 3), :tcf] = wg_ref[...]
    wb_ref[pl.ds(0, 3), tcf:] = wl_ref[...]
    wb_ref[pl.ds(3, 1), :tcf] = bg_ref[...]
    wb_ref[pl.ds(3, 1), tcf:] = bl_ref[...]


def _ffn_specs(s, tcf, n_f, batch_first):
    def spec(rows, shift):
        if batch_first:
            return pl.BlockSpec((rows, tcf), lambda bb, j: (bb if rows == s else 0, shift + j))
        return pl.BlockSpec((rows, tcf), lambda j, bb: (bb if rows == s else 0, shift + j))

    return [spec(s, 0), spec(s, n_f), spec(3, 0), spec(3, n_f), spec(1, 0), spec(1, n_f)]


def _ffn_act_fwd(hp, w, b, bsz, tcf, *, name):
    t, two_f = hp.shape
    s = t // bsz
    n_f = two_f // (2 * tcf)

    def body(hg_ref, hl_ref, wg_ref, wl_ref, bg_ref, bl_ref, f_ref, ext_ref, wb_ref):
        _ffn_stage(hg_ref, hl_ref, wg_ref, wl_ref, bg_ref, bl_ref, ext_ref, wb_ref, s, tcf)

        def step(i, carry):
            base = pl.multiple_of(i * FFN_ROWS, FFN_ROWS)
            hh = _ffn_conv_piece(ext_ref, wb_ref, base)
            _, gelu = _gelu_parts(hh[:, :tcf])
            f_ref[pl.ds(base, FFN_ROWS), :] = (gelu * hh[:, tcf:]).astype(f_ref.dtype)
            return carry

        lax.fori_loop(0, s // FFN_ROWS, step, 0)

    return _call(body, name=name, grid=(bsz, n_f), in_specs=_ffn_specs(s, tcf, n_f, True),
                 out_specs=pl.BlockSpec((s, tcf), lambda bb, j: (bb, j)),
                 out_shape=_sds((t, two_f // 2), BF16),
                 scratch=[pltpu.VMEM((FFN_PAD + s, 2 * tcf), F32), pltpu.VMEM((SUBLANES, 2 * tcf), F32)],
                 sem=("parallel", "parallel"))(hp, hp, w, w, b, b)


def _ffn_act_bwd(hp, df, w, b, bsz, tcf, *, name):
    t, two_f = hp.shape
    s = t // bsz
    f_dim = two_f // 2
    n_f = f_dim // tcf
    gw = 2 * tcf
    n_rows = FFN_ROWS + FFN_PAD

    def body(hg_ref, hl_ref, wg_ref, wl_ref, bg_ref, bl_ref, df_ref,
             dhg_ref, dhl_ref, dwg_ref, dwl_ref, dbg_ref, dbl_ref, ext_ref, wb_ref, dh_ref):
        @pl.when(pl.program_id(1) == 0)
        def _():
            for ref in (dwg_ref, dwl_ref, dbg_ref, dbl_ref):
                ref[...] = jnp.zeros_like(ref)

        _ffn_stage(hg_ref, hl_ref, wg_ref, wl_ref, bg_ref, bl_ref, ext_ref, wb_ref, s, tcf)
        dh_ref[pl.ds(s, FFN_PAD), :] = jnp.zeros((FFN_PAD, gw), F32)

        def grad_h(i, carry):
            base = pl.multiple_of(i * FFN_ROWS, FFN_ROWS)
            hh = _ffn_conv_piece(ext_ref, wb_ref, base)
            hg = hh[:, :tcf]
            d = df_ref[pl.ds(base, FFN_ROWS), :].astype(F32)
            cdf, gelu = _gelu_parts(hg)
            pdf = jnp.exp(-0.5 * hg * hg) * (1.0 / math.sqrt(2.0 * math.pi))
            dh_ref[pl.ds(base, FFN_ROWS), :tcf] = d * hh[:, tcf:] * (cdf + hg * pdf)
            dh_ref[pl.ds(base, FFN_ROWS), tcf:] = d * gelu
            return carry

        lax.fori_loop(0, s // FFN_ROWS, grad_h, 0)

        def grad_x(i, carry):
            dw0, dw1, dw2, dbs = carry
            base = pl.multiple_of(i * FFN_ROWS, FFN_ROWS)
            nxt = dh_ref[pl.ds(base, n_rows), :]
            dyc = nxt[:FFN_ROWS]
            dx = wb_ref[pl.ds(2, 1), :] * dyc
            dx = dx + wb_ref[pl.ds(1, 1), :] * pltpu.roll(nxt, n_rows - 1, axis=0)[:FFN_ROWS]
            dx = dx + wb_ref[pl.ds(0, 1), :] * pltpu.roll(nxt, n_rows - 2, axis=0)[:FFN_ROWS]
            dhg_ref[pl.ds(base, FFN_ROWS), :] = dx[:, :tcf].astype(dhg_ref.dtype)
            dhl_ref[pl.ds(base, FFN_ROWS), :] = dx[:, tcf:].astype(dhl_ref.dtype)
            win = ext_ref[pl.ds(base, n_rows), :]
            dw2 = dw2 + jnp.sum(dyc * win[FFN_PAD:], axis=0, keepdims=True)
            dw1 = dw1 + jnp.sum(dyc * pltpu.roll(win, 1, axis=0)[FFN_PAD:], axis=0, keepdims=True)
            dw0 = dw0 + jnp.sum(dyc * pltpu.roll(win, 2, axis=0)[FFN_PAD:], axis=0, keepdims=True)
            return dw0, dw1, dw2, dbs + jnp.sum(dyc, axis=0, keepdims=True)

        zero = jnp.zeros((1, gw), F32)
        sums = lax.fori_loop(0, s // FFN_ROWS, grad_x, (zero, zero, zero, zero))
        for k in range(3):
            dwg_ref[pl.ds(k, 1), :] += sums[k][:, :tcf]
            dwl_ref[pl.ds(k, 1), :] += sums[k][:, tcf:]
        dbg_ref[...] += sums[3][:, :tcf]
        dbl_ref[...] += sums[3][:, tcf:]

    half = pl.BlockSpec((s, tcf), lambda j, bb: (bb, j))
    taps = pl.BlockSpec((3, tcf), lambda j, bb: (0, j))
    bias = pl.BlockSpec((1, tcf), lambda j, bb: (0, j))
    return _call(body, name=name, grid=(n_f, bsz), in_specs=_ffn_specs(s, tcf, n_f, False) + [half],
                 out_specs=[half, half, taps, taps, bias, bias],
                 out_shape=[_sds((t, f_dim), BF16)] * 2 + [_sds((3, f_dim), F32)] * 2 + [_sds((1, f_dim), F32)] * 2,
                 scratch=[pltpu.VMEM((FFN_PAD + s, gw), F32), pltpu.VMEM((SUBLANES, gw), F32),
                          pltpu.VMEM((s + FFN_PAD, gw), F32)],
                 sem=("parallel", "arbitrary"))(hp, hp, w, w, b, b, df)


def _split3(v):
    hi = v.astype(BF16)
    r = v - hi.astype(F32)
    mid = r.astype(BF16)
    lo = (r - mid.astype(F32)).astype(BF16)
    return hi, mid, lo


def _tri_dot(tri, v):
    out = None
    for part in _split3(v):
        term = jnp.dot(tri, part, preferred_element_type=F32)
        out = term if out is None else out + term
    return out


def _fgate_fwd(zf, bsz, heads, *, name):
    t, lanes = zf.shape
    s, blk, nb = _seq_tiles(t, bsz, (ATTN_BLOCK, 128))

    def body(z_ref, cumt_ref, cumb_ref, carry_ref):
        @pl.when(pl.program_id(1) == 0)
        def _():
            carry_ref[...] = jnp.zeros_like(carry_ref)

        z = z_ref[...]
        lf = jnp.minimum(z, 0.0) - jnp.log1p(jnp.exp(-jnp.abs(z)))
        r = lax.broadcasted_iota(jnp.int32, (blk, blk), 0)
        c = lax.broadcasted_iota(jnp.int32, (blk, blk), 1)
        tri = (r >= c).astype(BF16)
        cum = _tri_dot(tri, lf) + carry_ref[...]
        carry_ref[...] = cum[blk - 1:blk, :]
        cumt_ref[0] = jnp.transpose(cum)[:heads, :]
        for h in range(heads):
            cumb_ref[0, h] = jnp.broadcast_to(cum[:, h:h + 1], (blk, lanes))

    return _call(body, name=name, grid=(bsz, nb),
                 in_specs=[pl.BlockSpec((blk, lanes), lambda b, i: (b * nb + i, 0))],
                 out_specs=[pl.BlockSpec((1, heads, blk), lambda b, i: (b, 0, i)),
                            pl.BlockSpec((1, heads, blk, lanes), lambda b, i: (b, 0, i, 0))],
                 out_shape=[_sds((bsz, heads, s), F32), _sds((bsz, heads, s, lanes), F32)],
                 scratch=[pltpu.VMEM((1, lanes), F32)], sem=("parallel", "arbitrary"))(zf)


def _fgate_bwd(dcum, zf, bsz, *, name):
    t, lanes = zf.shape
    pairs = dcum.shape[1]
    s, blk, nb = _seq_tiles(t, bsz, (ATTN_BLOCK, 128))

    def body(d_ref, z_ref, o_ref, carry_ref):
        @pl.when(pl.program_id(1) == 0)
        def _():
            carry_ref[...] = jnp.zeros_like(carry_ref)

        dcol = d_ref[0, 0]
        for p in range(1, pairs):
            dcol = dcol + d_ref[0, p]
        r = lax.broadcasted_iota(jnp.int32, (blk, blk), 0)
        c = lax.broadcasted_iota(jnp.int32, (blk, blk), 1)
        tri = (c >= r).astype(BF16)
        suf = _tri_dot(tri, dcol) + carry_ref[...]
        carry_ref[...] = suf[0:1, :]
        o_ref[...] = suf * _sigmoid(-z_ref[...])

    return _call(body, name=name, grid=(bsz, nb),
                 in_specs=[pl.BlockSpec((1, pairs, blk, lanes), lambda b, i: (b, 0, nb - 1 - i, 0)),
                           pl.BlockSpec((blk, lanes), lambda b, i: (b * nb + nb - 1 - i, 0))],
                 out_specs=pl.BlockSpec((blk, lanes), lambda b, i: (b * nb + nb - 1 - i, 0)),
                 out_shape=_sds((t, lanes), F32), scratch=[pltpu.VMEM((1, lanes), F32)],
                 sem=("parallel", "arbitrary"))(dcum, zf)


def _to_features_major(z, col_off, width, n, *, name):
    t = z.shape[0]
    tr = _pick(t, (512, 256, 128))
    first = col_off // width

    def body(*refs):
        o_ref = refs[n]
        for g in range(n):
            o_ref[pl.ds(g * width, width), :] = jnp.transpose(refs[g][...].astype(F32)).astype(o_ref.dtype)

    return _call(body, name=name, grid=(t // tr,),
                 in_specs=[pl.BlockSpec((tr, width), lambda i, g=g: (i, first + g)) for g in range(n)],
                 out_specs=pl.BlockSpec((n * width, tr), lambda i: (0, i)),
                 out_shape=_sds((n * width, t), BF16), sem=("parallel",))(*([z] * n))


def _to_rows_major(xt, *, name):
    w, t = xt.shape
    tr = _pick(t, (512, 256, 128))

    def body(x_ref, o_ref):
        o_ref[...] = jnp.transpose(x_ref[...]).astype(o_ref.dtype)

    return _call(body, name=name, grid=(t // tr,),
                 in_specs=[pl.BlockSpec((w, tr), lambda i: (0, i))],
                 out_specs=pl.BlockSpec((tr, w), lambda i: (i, 0)),
                 out_shape=_sds((t, w), BF16), sem=("parallel",))(xt)


def _loop_by_twos(lo, hi, body, carry):
    count = hi - lo

    def group(n, first, cr):
        for u in range(n):
            cr = body(first + u, cr)
        return cr

    trips = count // ATTN_UNROLL
    carry = lax.fori_loop(0, trips, lambda t, cr: group(ATTN_UNROLL, lo + ATTN_UNROLL * t, cr), carry)
    rest = count - ATTN_UNROLL * trips
    first = lo + ATTN_UNROLL * trips
    for n in range(ATTN_UNROLL - 1, 0, -1):
        carry = lax.cond(rest == n, lambda cr, n=n: group(n, first, cr), lambda cr: cr, carry)
    return carry


def _head_masks(shape, axis):
    feat = lax.broadcasted_iota(jnp.int32, shape, axis)
    return feat < HEAD_DIM, feat >= HEAD_DIM


def _attn_fwd(z, qkvt, cumt, cumb, bsz, heads, q_off, *, name):
    t = z.shape[0]
    width = heads * HEAD_DIM
    pairs = heads // 2
    s = t // bsz
    blk = ATTN_BLOCK
    nq = s // blk
    k_col = (q_off + width) // LANES
    v_row = 2 * width // LANES
    reps = blk // LANES

    def body(k_ref, qt_ref, vt_ref, cqt_ref, ckb_ref, ot_ref, lse_ref):
        p_id = pl.program_id(1)
        i = pl.program_id(2)
        qt = qt_ref[...]
        masks = _head_masks((LANES, blk), 0)
        qtm = [jnp.where(mk, qt, jnp.zeros_like(qt)) for mk in masks]
        cq = [cqt_ref[0, pl.ds(2 * p_id + hh, 1), :] for hh in range(2)]
        kidx = lax.broadcasted_iota(jnp.int32, (blk, blk), 0)
        qidx = lax.broadcasted_iota(jnp.int32, (blk, blk), 1)

        def block(j, carry, masked):
            off = pl.multiple_of(j * blk, blk)
            kp = k_ref[pl.ds(off, blk), :].astype(BF16)
            vtp = vt_ref[:, pl.ds(off, blk)]
            out = []
            for hh in range(2):
                m, l, acc = carry[hh]
                sc = jnp.dot(kp, qtm[hh], preferred_element_type=F32) * ATTN_SCALE
                ck = ckb_ref[0, hh, pl.ds(off, blk), :]
                sc = (sc + cq[hh]) - jnp.concatenate([ck] * reps, axis=1)
                if masked:
                    sc = jnp.where(qidx >= kidx, sc, NEG)
                m_new = jnp.maximum(m, jnp.max(sc, axis=0, keepdims=True))
                pr = jnp.exp(sc - m_new)
                a = jnp.exp(m - m_new)
                l = a * l + jnp.sum(pr, axis=0, keepdims=True)
                p_hi = pr.astype(BF16)
                p_lo = (pr - p_hi.astype(F32)).astype(BF16)
                pv = (jnp.dot(vtp, p_hi, preferred_element_type=F32)
                      + jnp.dot(vtp, p_lo, preferred_element_type=F32))
                acc = a * acc + pv[hh * HEAD_DIM:(hh + 1) * HEAD_DIM]
                out.append((m_new, l, acc))
            return tuple(out)

        init = tuple((jnp.full((1, blk), NEG, F32), jnp.zeros((1, blk), F32), jnp.zeros((HEAD_DIM, blk), F32))
                     for _ in range(2))
        carry = _loop_by_twos(0, i, lambda j, cr: block(j, cr, False), init)
        carry = block(i, carry, True)
        lse_ref[...] = jnp.zeros_like(lse_ref)
        for hh in range(2):
            m, l, acc = carry[hh]
            ot_ref[pl.ds(hh * HEAD_DIM, HEAD_DIM), :] = acc / l
            lse_ref[0, 0, pl.ds(hh, 1), :] = m + jnp.log(l)

    return _call(body, name=name, grid=(bsz, pairs, nq),
                 in_specs=[pl.BlockSpec((s, LANES), lambda b, p, i: (b, k_col + p)),
                           pl.BlockSpec((LANES, blk), lambda b, p, i: (p, b * nq + i)),
                           pl.BlockSpec((LANES, s), lambda b, p, i: (v_row + p, b)),
                           pl.BlockSpec((1, heads, blk), lambda b, p, i: (b, 0, i)),
                           pl.BlockSpec((1, 2, s, LANES), lambda b, p, i: (b, p, 0, 0))],
                 out_specs=[pl.BlockSpec((LANES, blk), lambda b, p, i: (p, b * nq + i)),
                            pl.BlockSpec((1, 1, SUBLANES, blk), lambda b, p, i: (b, p, 0, i))],
                 out_shape=[_sds((width, t), F32), _sds((bsz, pairs, SUBLANES, s), F32)],
                 sem=("parallel", "parallel", "parallel"))(z, qkvt, qkvt, cumt, cumb)


def _attn_bwd(z, qkvt, cumt, cumb, ot, do, dot, lse, bsz, heads, q_off, *, name):
    t = z.shape[0]
    width = heads * HEAD_DIM
    pairs = heads // 2
    s = t // bsz
    blk = ATTN_BLOCK
    nkv = s // blk
    q_col = q_off // LANES
    k_col = (q_off + width) // LANES
    v_col = (q_off + 2 * width) // LANES
    k_row = width // LANES
    reps = blk // LANES

    def body(k_ref, v_ref, kt_ref, q_ref, qt_ref, do_ref, dot_ref, ot_ref, lse_ref, ckb_ref, cqt_ref,
             dk_ref, dv_ref, dqt_ref, dcum_ref, dqt_acc, ds_acc):
        p_id = pl.program_id(1)
        j = pl.program_id(2)

        @pl.when(j == 0)
        def _():
            dqt_acc[...] = jnp.zeros_like(dqt_acc)

        kp = k_ref[...].astype(BF16)
        vp = v_ref[...].astype(BF16)
        kt = kt_ref[...]
        feat_masks = _head_masks((LANES, blk), 0)
        lane_masks = _head_masks((blk, LANES), 1)
        ktm = [jnp.where(mk, kt, jnp.zeros_like(kt)) for mk in feat_masks]
        ck = [jnp.concatenate([ckb_ref[0, hh]] * reps, axis=1) for hh in range(2)]
        kidx = lax.broadcasted_iota(jnp.int32, (blk, blk), 0)
        qidx = lax.broadcasted_iota(jnp.int32, (blk, blk), 1)
        ds_acc[...] = jnp.zeros_like(ds_acc)

        def block(i, carry, masked):
            dk, dv = carry
            off = pl.multiple_of(i * blk, blk)
            qt = qt_ref[:, pl.ds(off, blk)]
            dt = dot_ref[:, pl.ds(off, blk)]
            o_t = ot_ref[:, pl.ds(off, blk)]
            q_rows = q_ref[pl.ds(off, blk), :].astype(BF16)
            do_rows = do_ref[pl.ds(off, blk), :]
            for hh in range(2):
                qtm = jnp.where(feat_masks[hh], qt, jnp.zeros_like(qt))
                dtm = jnp.where(feat_masks[hh], dt, jnp.zeros_like(dt))
                sc = jnp.dot(kp, qtm, preferred_element_type=F32) * ATTN_SCALE
                sc = (sc + cqt_ref[0, pl.ds(2 * p_id + hh, 1), pl.ds(off, blk)]) - ck[hh]
                pr = jnp.exp(sc - lse_ref[0, 0, pl.ds(hh, 1), pl.ds(off, blk)])
                if masked:
                    pr = jnp.where(qidx >= kidx, pr, 0.0)
                dp = jnp.dot(vp, dtm, preferred_element_type=F32)
                delta = jnp.sum(dtm.astype(F32) * o_t, axis=0, keepdims=True)
                ds = pr * (dp - delta)
                ds_acc[hh] += ds
                dsb = ds.astype(BF16)
                qm = jnp.where(lane_masks[hh], q_rows, jnp.zeros_like(q_rows))
                dom = jnp.where(lane_masks[hh], do_rows, jnp.zeros_like(do_rows))
                dv = dv + jnp.dot(pr.astype(BF16), dom, preferred_element_type=F32)
                dk = dk + jnp.dot(dsb, qm, preferred_element_type=F32) * ATTN_SCALE
                dqt_acc[:, pl.ds(off, blk)] += jnp.dot(ktm[hh], dsb, preferred_element_type=F32) * ATTN_SCALE
            return dk, dv

        zero = jnp.zeros((blk, LANES), F32)
        carry = block(j, (zero, zero), True)
        dk, dv = _loop_by_twos(j + 1, nkv, lambda i, cr: block(i, cr, False), carry)
        dk_ref[...] = dk.astype(dk_ref.dtype)
        dv_ref[...] = dv.astype(dv_ref.dtype)
        lane = lax.broadcasted_iota(jnp.int32, (blk, LANES), 1)
        dcum = jnp.zeros((blk, LANES), F32)
        for hh in range(2):
            col = jnp.sum(ds_acc[hh], axis=1, keepdims=True)
            dcum = jnp.where(lane == 2 * p_id + hh, -col, dcum)
        dcum_ref[0, 0] = dcum

        @pl.when(j == nkv - 1)
        def _():
            dqt_ref[...] = dqt_acc[...]

    key_rows = lambda col: pl.BlockSpec((blk, LANES), lambda b, p, j: (b * nkv + j, col + p))
    seq_t = lambda row: pl.BlockSpec((LANES, s), lambda b, p, j: (row + p, b))
    return _call(body, name=name, grid=(bsz, pairs, nkv),
                 in_specs=[key_rows(k_col), key_rows(v_col),
                           pl.BlockSpec((LANES, blk), lambda b, p, j: (k_row + p, b * nkv + j)),
                           pl.BlockSpec((s, LANES), lambda b, p, j: (b, q_col + p)), seq_t(0),
                           pl.BlockSpec((s, LANES), lambda b, p, j: (b, p)), seq_t(0), seq_t(0),
                           pl.BlockSpec((1, 1, SUBLANES, s), lambda b, p, j: (b, p, 0, 0)),
                           pl.BlockSpec((1, 2, blk, LANES), lambda b, p, j: (b, p, j, 0)),
                           pl.BlockSpec((1, heads, s), lambda b, p, j: (b, 0, 0))],
                 out_specs=[key_rows(0), key_rows(0), seq_t(0),
                            pl.BlockSpec((1, 1, blk, LANES), lambda b, p, j: (b, p, j, 0))],
                 out_shape=[_sds((t, width), BF16), _sds((t, width), BF16), _sds((width, t), F32),
                            _sds((bsz, pairs, s, LANES), F32)],
                 scratch=[pltpu.VMEM((LANES, s), F32), pltpu.VMEM((2, blk, blk), F32)],
                 sem=("parallel", "parallel", "arbitrary"))(z, z, qkvt, z, qkvt, do, dot, ot, lse, cumb, cumt)


def _adamw(w, g, m, v, *, name):
    bc1 = 1.0 - ADAM_B1 ** ADAM_STEP
    bc2 = 1.0 - ADAM_B2 ** ADAM_STEP

    def body(w_ref, g_ref, m_ref, v_ref, d_ref, nm_ref, nv_ref):
        g_v = g_ref[...]
        nm = ADAM_B1 * m_ref[...] + (1.0 - ADAM_B1) * g_v
        nv = ADAM_B2 * v_ref[...] + (1.0 - ADAM_B2) * (g_v * g_v)
        nm_ref[...] = nm
        nv_ref[...] = nv
        d_ref[...] = -ADAM_LR * ((nm / bc1) / (jnp.sqrt(nv / bc2) + ADAM_EPS) + ADAM_WD * w_ref[...])

    if w.ndim == 2:
        grid = (1,)
        blk = pl.BlockSpec(w.shape, lambda i: (0, 0))
    else:
        layers, rows, cols = w.shape
        tr = rows if rows <= 256 else _pick(rows, (256, 128, 64, 32, 16, 8))
        grid = (layers, rows // tr)
        blk = pl.BlockSpec((1, tr, cols), lambda layer, i: (layer, i, 0))
    return tuple(_call(body, name=name, grid=grid, in_specs=[blk] * 4, out_specs=[blk] * 3,
                       out_shape=[_sds(w.shape, F32)] * 3, sem=("parallel",) * len(grid))(w, g, m, v))


_ANY = pl.BlockSpec(memory_space=pl.ANY)


def _comm_call(body, *, name, n_in, out_shape, n_sems):
    scratch = [pltpu.SemaphoreType.DMA((n_sems,)), pltpu.SemaphoreType.DMA((n_sems,)),
               pltpu.SemaphoreType.DMA((len(out_shape),))]
    return pl.pallas_call(body, name=name, in_specs=[_ANY] * n_in, out_specs=[_ANY] * len(out_shape),
                          out_shape=out_shape, scratch_shapes=scratch)


def _place():
    x, y, c = lax.axis_index("x"), lax.axis_index("y"), lax.axis_index("c")
    return x, y, c, [(1 - x, y), (x, 1 - y), (1 - x, 1 - y)]


def _remote(src, dst, send_sems, recv_sems, sem, to):
    return pltpu.make_async_remote_copy(src_ref=src, dst_ref=dst, send_sem=send_sems.at[sem],
                                        recv_sem=recv_sems.at[sem], device_id=to, device_id_type=MESH)


def _all_gather8(v, *, name):
    def body(v_ref, out_ref, send_sems, recv_sems, local_sems):
        x, y, c, _ = _place()
        me = 4 * x + 2 * y + c
        mine = pltpu.make_async_copy(v_ref, out_ref.at[me], local_sems.at[0])
        mine.start()
        peers = []
        for k in range(1, N_DEVICES):
            px = 1 - x if k & 4 else x
            py = 1 - y if k & 2 else y
            pc = 1 - c if k & 1 else c
            peers.append((px, py, pc))
        sends = [_remote(v_ref, out_ref.at[me], send_sems, recv_sems, k, peer) for k, peer in enumerate(peers)]
        for cp in sends:
            cp.start()
        for k, (px, py, pc) in enumerate(peers):
            _remote(v_ref, out_ref.at[4 * px + 2 * py + pc], send_sems, recv_sems, k, (px, py, pc)).wait_recv()
        for cp in sends:
            cp.wait_send()
        mine.wait()

    out = _comm_call(body, name=name, n_in=1, out_shape=[_sds((N_DEVICES,) + v.shape, v.dtype)],
                     n_sems=N_DEVICES - 1)(v)
    return out[0]


def _window(ref, mode, layer, chip, rows, cols, half=None):
    first, count = (0, rows) if half is None else (half * (rows // 2), rows // 2)
    if mode == "slab":
        return ref.at[layer, chip] if half is None else ref.at[layer, chip, pl.ds(first, count), :]
    if mode == "cols":
        col_window = pl.ds(pl.multiple_of(chip * cols, LANES), cols)
        return ref.at[layer, :, col_window] if half is None else ref.at[layer, pl.ds(first, count), col_window]
    return ref.at[layer, pl.ds(pl.multiple_of(chip * rows + first, SUBLANES), count), :]


def _whole_shape(mode, shard_shape):
    layers, rows, cols = shard_shape
    if mode == "slab":
        return (layers, N_CHIPS, rows, cols)
    if mode == "cols":
        assert cols % LANES == 0
        return (layers, rows, N_CHIPS * cols)
    assert rows % 16 == 0
    return (layers, N_CHIPS * rows, cols)


def _gather_weights(shards, modes, *, name):
    n = len(shards)
    meta = [(mode,) + tuple(a.shape[1:]) for a, mode in zip(shards, modes)]
    for a in shards:
        assert a.shape[0] == 2 and a.shape[1] % 2 == 0
    per = 8

    def body(*refs):
        ins, outs = refs[:n], refs[n:2 * n]
        send_sems, recv_sems, _ = refs[2 * n:]
        x, y, c, _ = _place()
        me, x_nbr, y_nbr, diagonal = 2 * x + y, 2 * (1 - x) + y, 2 * x + 1 - y, 2 * (1 - x) + 1 - y
        to_x, to_y, sibling = (1 - x, y, c), (x, 1 - y, c), (x, y, 1 - c)
        sent = []

        def copy(src, dst, sem, to):
            cp = _remote(src, dst, send_sems, recv_sems, sem, to)
            cp.start()
            sent.append(cp)

        def arrived(win, sem):
            _remote(win, win, send_sems, recv_sems, sem, sibling).wait_recv()

        for i, (mode, rows, cols) in enumerate(meta):
            mine = _window(outs[i], mode, c, me, rows, cols)
            copy(ins[i].at[c], mine, per * i, to_x)
            copy(ins[i].at[c], mine, per * i + 1, to_y)
            copy(ins[i], _window(outs[i], mode, slice(None), me, rows, cols), per * i + 7, sibling)
        for i, (mode, rows, cols) in enumerate(meta):
            arrived(_window(outs[i], mode, c, x_nbr, rows, cols), per * i)
            half = _window(outs[i], mode, c, x_nbr, rows, cols, half=0)
            copy(half, half, per * i + 2, to_y)
            win = _window(outs[i], mode, c, x_nbr, rows, cols)
            copy(win, win, per * i + 4, sibling)
            arrived(_window(outs[i], mode, c, y_nbr, rows, cols), per * i + 1)
            half = _window(outs[i], mode, c, y_nbr, rows, cols, half=1)
            copy(half, half, per * i + 3, to_x)
            win = _window(outs[i], mode, c, y_nbr, rows, cols)
            copy(win, win, per * i + 5, sibling)
        for i, (mode, rows, cols) in enumerate(meta):
            arrived(_window(outs[i], mode, c, diagonal, rows, cols, half=0), per * i + 2)
            arrived(_window(outs[i], mode, c, diagonal, rows, cols, half=1), per * i + 3)
            win = _window(outs[i], mode, c, diagonal, rows, cols)
            copy(win, win, per * i + 6, sibling)
        for i, (mode, rows, cols) in enumerate(meta):
            arrived(_window(outs[i], mode, slice(None), me, rows, cols), per * i + 7)
            for k, chip in enumerate((x_nbr, y_nbr, diagonal)):
                arrived(_window(outs[i], mode, 1 - c, chip, rows, cols), per * i + 4 + k)
        for cp in sent:
            cp.wait_send()

    out_shape = [_sds(_whole_shape(mode, a.shape), a.dtype) for a, mode in zip(shards, modes)]
    return _comm_call(body, name=name, n_in=n, out_shape=out_shape, n_sems=per * n)(*shards)


def _rs_swap(grads, *, name):
    n = len(grads)

    def body(*refs):
        ins, outs = refs[:n], refs[n:2 * n]
        send_sems, recv_sems, _ = refs[2 * n:]
        x, y, c, _ = _place()
        copies = [_remote(ins[i].at[1 - c], outs[i], send_sems, recv_sems, i, (x, y, 1 - c)) for i in range(n)]
        for cp in copies:
            cp.start()
        for cp in copies:
            cp.wait()

    return _comm_call(body, name=name, n_in=n, out_shape=[_sds(g.shape[1:], g.dtype) for g in grads], n_sems=n)(*grads)


def _part(ref, mode, chip, rows, cols):
    if mode == "slab":
        return ref.at[chip]
    if mode == "cols":
        return ref.at[:, pl.ds(pl.multiple_of(chip * cols, LANES), cols)]
    return ref.at[pl.ds(pl.multiple_of(chip * rows, SUBLANES), rows), :]


def _rs_scatter(parts, modes, shard_shapes, *, name):
    n = len(parts)
    meta = [(mode,) + tuple(shp[1:]) for mode, shp in zip(modes, shard_shapes)]

    def body(*refs):
        ins, outs = refs[:n], refs[n:2 * n]
        send_sems, recv_sems, local_sems = refs[2 * n:]
        x, y, c, chips = _place()
        me = 2 * x + y
        local, sends = [], []
        for i, (mode, rows, cols) in enumerate(meta):
            cp = pltpu.make_async_copy(_part(ins[i], mode, me, rows, cols), outs[i].at[me], local_sems.at[i])
            cp.start()
            local.append(cp)
            for r, (cx, cy) in enumerate(chips):
                cp = _remote(_part(ins[i], mode, 2 * cx + cy, rows, cols), outs[i].at[me], send_sems, recv_sems,
                             3 * i + r, (cx, cy, c))
                cp.start()
                sends.append(cp)
        for i, (mode, rows, cols) in enumerate(meta):
            for r, (cx, cy) in enumerate(chips):
                k = 2 * cx + cy
                _remote(_part(ins[i], mode, k, rows, cols), outs[i].at[k], send_sems, recv_sems, 3 * i + r,
                        (cx, cy, c)).wait_recv()
        for cp in sends:
            cp.wait_send()
        for cp in local:
            cp.wait()

    out_shape = [_sds((N_CHIPS,) + tuple(shp[1:]), p.dtype) for p, shp in zip(parts, shard_shapes)]
    return _comm_call(body, name=name, n_in=n, out_shape=out_shape, n_sems=3 * n)(*parts)


def _rs_exchange(sums, *, name):
    n = len(sums)

    def body(*refs):
        ins, outs = refs[:n], refs[n:2 * n]
        send_sems, recv_sems, _ = refs[2 * n:]
        x, y, c, _ = _place()
        copies = [_remote(ins[i], outs[i], send_sems, recv_sems, i, (x, y, 1 - c)) for i in range(n)]
        for cp in copies:
            cp.start()
        for cp in copies:
            cp.wait()

    return _comm_call(body, name=name, n_in=n, out_shape=[_sds(s.shape, s.dtype) for s in sums], n_sems=n)(*sums)


def _row_tile(rows, cols, itemsize):
    target = max(SUBLANES, (2 << 20) // (cols * itemsize))
    cands = [c for c in (2048, 1024, 512, 256, 128, 64, 32, 16) if c <= target]
    tr = _pick(rows, cands)
    return tr


def _add_layer(g, other, core, *, name):
    _, rows, cols = g.shape
    tr = _row_tile(rows, cols, 4)

    def body(core_ref, g_ref, o_ref, out_ref):
        out_ref[...] = (g_ref[0] + o_ref[...]).astype(out_ref.dtype)

    grid_spec = pltpu.PrefetchScalarGridSpec(
        num_scalar_prefetch=1, grid=(rows // tr,),
        in_specs=[pl.BlockSpec((1, tr, cols), lambda i, core_ref: (core_ref[0], i, 0)),
                  pl.BlockSpec((tr, cols), lambda i, core_ref: (i, 0))],
        out_specs=pl.BlockSpec((tr, cols), lambda i, core_ref: (i, 0)))
    return pl.pallas_call(body, name=name, grid_spec=grid_spec, out_shape=_sds((rows, cols), BF16),
                          compiler_params=pltpu.CompilerParams(dimension_semantics=("parallel",),
                                                               vmem_limit_bytes=VMEM_LIMIT))(core, g, other)


def _sum_slots(parts, *, name):
    n, rows, cols = parts.shape
    tr = _row_tile(rows, cols, 4)

    def body(p_ref, o_ref):
        acc = p_ref[0].astype(F32) + p_ref[1].astype(F32)
        for k in range(2, n):
            acc = acc + p_ref[k].astype(F32)
        o_ref[...] = acc

    return _call(body, name=name, grid=(rows // tr,),
                 in_specs=[pl.BlockSpec((n, tr, cols), lambda i: (0, i, 0))],
                 out_specs=pl.BlockSpec((tr, cols), lambda i: (i, 0)),
                 out_shape=_sds((rows, cols), F32), sem=("parallel",))(parts)


def _reduce_scatter(grads, modes, shard_shapes):
    core = lax.axis_index("c").astype(jnp.int32).reshape(1)
    flat = [g.reshape(g.shape[0], -1, g.shape[-1]) for g in grads]
    from_sibling = _rs_swap(flat, name="rs_swap")
    parts = []
    for i, (g, o) in enumerate(zip(flat, from_sibling)):
        p = _add_layer(g, o, core, name=f"rs_add_{i}")
        parts.append(p.reshape(grads[i].shape[1:]))
    from_chips = _rs_scatter(parts, modes, shard_shapes, name="rs_scatter")
    sums = [_sum_slots(r, name=f"rs_sum_{i}") for i, r in enumerate(from_chips)]
    others = _rs_exchange(sums, name="rs_exchange")
    mine_first = lax.axis_index("c") == 0
    return [jnp.where(mine_first, jnp.stack([mine, other]), jnp.stack([other, mine]))
            for mine, other in zip(sums, others)]


def _layer_weights(full, rep, layer, dims):
    f_off, n_heads = dims["f_off"], dims["heads"]
    b_in = rep["b_in"][layer]
    pad = LANES - n_heads
    return {
        "w_main": (full["w_main"], layer),
        "b_main": jnp.concatenate([b_in[:f_off], b_in[f_off + n_heads:]])[None],
        "w_f": (full["w_f"], layer),
        "w_f_matrix": full["w_f"][layer],
        "b_f": jnp.pad(b_in[f_off:f_off + n_heads], (0, pad))[None],
        "conv_a_w": full["conv_a_w"][layer],
        "conv_a_b": rep["conv_a_b"][layer][None],
        "ln_conv_g": rep["ln_conv_g"][layer][None],
        "ln_conv_b": rep["ln_conv_b"][layer][None],
        "w_conv_proj": (full["w_conv_proj"], layer),
        "w_attn_proj": (full["w_attn_proj"], layer),
        "w_mix_out": (full["w_mix_out"], layer),
        "b_mix_out": rep["b_mix_out"][layer][None],
        "ln1_g": rep["ln1_g"][layer][None],
        "ln1_b": rep["ln1_b"][layer][None],
        "w_ffn_up": (full["w_ffn_up"], layer),
        "ffn_conv_w": full["ffn_conv_w"][layer],
        "ffn_conv_b": rep["ffn_conv_b"][layer][None],
        "w_ffn_down": (full["w_ffn_down"], layer),
        "ln2_g": rep["ln2_g"][layer][None],
        "ln2_b": rep["ln2_b"][layer][None],
    }


def _layer_fwd(x, mod, p, dims, tag, target=None):
    bsz, d, ch, heads, alpha = dims["bsz"], dims["d"], dims["ch"], dims["heads"], dims["alpha"]
    mods = [mod[:, k * d:(k + 1) * d][:, None, :] for k in range(6)]
    shift1, scale1, gate1, shift2, scale2, gate2 = mods
    u = _ln_mod_fwd(x, scale1, shift1, bsz, name=f"ln_mod1_{tag}")
    zm = _matmul(u, p["w_main"], "nn", BF16, bias=p["b_main"], name=f"in_main_{tag}")
    zf = _matmul(u, p["w_f"], "nn", F32, bias=p["b_f"], name=f"in_forget_{tag}")
    a3 = _conv_branch_fwd(zm, p["conv_a_w"], p["conv_a_b"], p["ln_conv_g"], p["ln_conv_b"], bsz, ch,
                          name=f"conv_branch_{tag}")
    ya = _matmul(a3, p["w_conv_proj"], "nn", BF16, name=f"conv_proj_{tag}")
    cumt, cumb = _fgate_fwd(zf, bsz, heads, name=f"fgate_{tag}")
    qkvt = _to_features_major(zm, 2 * ch, heads * HEAD_DIM, 3, name=f"qkv_t_{tag}")
    ot, lse = _attn_fwd(zm, qkvt, cumt, cumb, bsz, heads, 2 * ch, name=f"attn_{tag}")
    yb = _matmul(ot, p["w_attn_proj"], "tn", BF16, name=f"attn_proj_{tag}")
    m = _gate_merge_fwd(zm, ya, yb, dims["ga_off"], name=f"merge_{tag}")
    mix = _matmul(m, p["w_mix_out"], "nn", F32, bias=p["b_mix_out"], name=f"mix_out_{tag}")
    x1, u2 = _ln_res_mod_fwd(x, mix, gate1, p["ln1_g"], p["ln1_b"], scale2, shift2, alpha, bsz,
                             name=f"ln_res1_mod2_{tag}")
    hp = _matmul(u2, p["w_ffn_up"], "nn", BF16, name=f"ffn_up_{tag}")
    f = _ffn_act_fwd(hp, p["ffn_conv_w"], p["ffn_conv_b"], bsz, dims["tcf"], name=f"ffn_act_{tag}")
    ffn = _matmul(f, p["w_ffn_down"], "nn", F32, name=f"ffn_down_{tag}")
    if target is None:
        x2 = _ln_res_fwd(x1, ffn, gate2, p["ln2_g"], p["ln2_b"], alpha, bsz, name=f"ln_res2_{tag}")
    else:
        x2 = _ln_res_loss(x1, ffn, gate2, p["ln2_g"], p["ln2_b"], target, alpha, bsz, name=f"ln_res2_loss_{tag}")
    saved = dict(x=x, mods=mods, u=u, zm=zm, zf=zf, a3=a3, ya=ya, yb=yb, cumt=cumt, cumb=cumb,
                 qkvt=qkvt, ot=ot, lse=lse, m=m, mix=mix, x1=x1, u2=u2, hp=hp, f=f, ffn=ffn)
    return x2, saved


def _layer_bwd(dx2, p, sv, dims, tag):
    bsz, ch, heads, alpha = dims["bsz"], dims["ch"], dims["heads"], dims["alpha"]
    f_off, tcf = dims["f_off"], dims["tcf"]
    shift1, scale1, gate1, shift2, scale2, gate2 = sv["mods"]
    g = {}
    dr2, dffn, dgate2, g["ln2_g"], g["ln2_b"], _ = _ln_res_bwd(
        dx2, sv["x1"], sv["ffn"], gate2, p["ln2_g"], alpha, bsz, name=f"ln_res2_bwd_{tag}")
    df = _matmul(dffn, p["w_ffn_down"], "nt", BF16, name=f"ffn_down_dx_{tag}")
    g["w_ffn_down"] = _matmul(sv["f"], dffn, "tn", F32, name=f"ffn_down_dw_{tag}")
    dhg, dhl, dwg, dwl, dbg, dbl = _ffn_act_bwd(sv["hp"], df, p["ffn_conv_w"], p["ffn_conv_b"], bsz, tcf,
                                                name=f"ffn_act_bwd_{tag}")
    g["ffn_conv_w"] = jnp.concatenate([dwg, dwl], axis=1)
    g["ffn_conv_b"] = jnp.concatenate([dbg, dbl], axis=1)[0]
    du2 = _matmul(dhg, p["w_ffn_up"], "nt", F32, name=f"ffn_up_gate_dx_{tag}")
    du2 = _matmul(dhl, p["w_ffn_up"], "nt", F32, add=du2, b_k_first=dhg.shape[1], name=f"ffn_up_lin_dx_{tag}")
    d_ff = dhg.shape[1]
    dw_up = _matmul(sv["u2"], dhg, "tn", F32, out_cols=2 * d_ff, name=f"ffn_up_gate_dw_{tag}")
    g["w_ffn_up"] = _matmul(sv["u2"], dhl, "tn", F32, into=(dw_up, d_ff), name=f"ffn_up_lin_dw_{tag}")
    dr1, dmix, dgate1, g["ln1_g"], g["ln1_b"], g["b_mix_out"], dscale2, dshift2 = _ln_mod_res_bwd(
        du2, dr2, scale2, sv["x"], sv["mix"], gate1, p["ln1_g"], p["ln1_b"], alpha, bsz,
        name=f"ln_mod2_res1_bwd_{tag}")
    dm = _matmul(dmix, p["w_mix_out"], "nt", BF16, name=f"mix_out_dx_{tag}")
    g["w_mix_out"] = _matmul(sv["m"], dmix, "tn", F32, name=f"mix_out_dw_{tag}")
    dya, dyb, dzga, dzgb = _gate_merge_bwd(sv["zm"], sv["ya"], sv["yb"], dm, dims["ga_off"], name=f"merge_bwd_{tag}")
    da3 = _matmul(dya, p["w_conv_proj"], "nt", F32, name=f"conv_proj_dx_{tag}")
    g["w_conv_proj"] = _matmul(sv["a3"], dya, "tn", F32, name=f"conv_proj_dw_{tag}")
    do = _matmul(dyb, p["w_attn_proj"], "nt", BF16, name=f"attn_proj_dx_{tag}")
    dot = _matmul(p["w_attn_proj"], dyb, "nt", BF16, name=f"attn_proj_dxt_{tag}")
    g["w_attn_proj"] = _matmul(sv["ot"], dyb, "nn", F32, name=f"attn_proj_dw_{tag}")
    dzglu, g["conv_a_w"], dcb, g["ln_conv_g"], g["ln_conv_b"] = _conv_branch_bwd(
        sv["zm"], da3, p["conv_a_w"], p["conv_a_b"], p["ln_conv_g"], p["ln_conv_b"], bsz, ch,
        name=f"conv_branch_bwd_{tag}")
    g["conv_a_b"] = dcb[0]
    dk, dv, dqt, dcum = _attn_bwd(sv["zm"], sv["qkvt"], sv["cumt"], sv["cumb"], sv["ot"], do, dot, sv["lse"], bsz,
                                  heads, 2 * ch, name=f"attn_bwd_{tag}")
    dq = _to_rows_major(dqt, name=f"dq_rows_{tag}")
    dzf = _fgate_bwd(dcum, sv["zf"], bsz, name=f"fgate_bwd_{tag}")
    dzm = jnp.concatenate([dzglu, dq, dk, dv, dzga, dzgb], axis=1)
    du = _matmul(dzm, p["w_main"], "nt", F32, name=f"in_main_dx_{tag}")
    dwm, dbm = _matmul(sv["u"], dzm, "tn", F32, colsum=True, name=f"in_main_dw_{tag}")
    dwf, dbf = _matmul(sv["u"], dzf, "tn", F32, colsum=True, name=f"in_forget_dw_{tag}")
    dbm, dbf = dbm[0], dbf[0]
    g["w_main"], g["w_f"] = dwm, dwf
    g["b_in"] = jnp.concatenate([dbm[:f_off], dbf[:heads], dbm[f_off:]])
    dx, dscale1, dshift1 = _ln_mod_bwd(du, dzf, p["w_f_matrix"], sv["x"], scale1, dr1, alpha, bsz,
                                       name=f"ln_mod1_bwd_{tag}")
    dmod = jnp.concatenate([dshift1, dscale1, dgate1, dshift2, dscale2, dgate2], axis=2)[:, 0, :]
    return dx, g, dmod


def _local_step(x, mod, loss_target, full, rep, dims):
    bsz, seq, d = x.shape
    layers = mod.shape[0]
    params = [_layer_weights(full, rep, layer, dims) for layer in range(layers)]
    h = x.reshape(bsz * seq, d)
    saved = []
    for layer in range(layers):
        target = loss_target.reshape(bsz * seq, d) if layer == layers - 1 else None
        h, sv = _layer_fwd(h, mod[layer], params[layer], dims, f"l{layer}", target)
        saved.append(sv)
    dh, sq = h
    loss_local = 0.5 * jnp.sum(sq) / d
    grads, dmods = [None] * layers, [None] * layers
    for layer in reversed(range(layers)):
        dh, grads[layer], dmods[layer] = _layer_bwd(dh, params[layer], saved[layer], dims, f"l{layer}")
    per_layer = ("w_main", "w_f")
    stacked = {wname: [grads[layer][wname] for layer in range(layers)] if wname in per_layer
               else jnp.stack([grads[layer][wname] for layer in range(layers)]) for wname in grads[0]}
    return loss_local, dh.reshape(bsz, seq, d), stacked, jnp.stack(dmods)


def _pad_rows(a):
    extra = -a.shape[-2] % (2 * SUBLANES)
    if extra == 0:
        return a
    return jnp.pad(a, [(0, 0)] * (a.ndim - 2) + [(0, extra), (0, 0)])


def _w_in_pieces(n, f_off, heads):
    n_in = N_CHIPS * n
    segments = [(0, f_off, "main", 0), (f_off, f_off + heads, "f", 0), (f_off + heads, n_in, "main", f_off)]
    pieces = []
    for chip in range(N_CHIPS):
        lo, hi = chip * n, (chip + 1) * n
        for a, b, target, t0 in segments:
            s, e = max(lo, a), min(hi, b)
            if s < e:
                pieces.append((chip, s - lo, e - lo, target, t0 + s - a))
    return pieces


def _w_in_from_slabs(slabs, f_off, heads, *, name):
    layers, _, k, n = slabs.shape
    tr = _pick(k, (256, 128, 64, 32, 16))
    n_main = N_CHIPS * n - heads
    pieces = _w_in_pieces(n, f_off, heads)

    def body(s_ref, m_ref, f_ref):
        f_ref[...] = jnp.zeros_like(f_ref)
        for chip in range(N_CHIPS):
            slab = s_ref[0, chip].astype(F32)
            for pc, s0, s1, target, t0 in pieces:
                if pc == chip:
                    out = m_ref if target == "main" else f_ref
                    out[0, :, t0:t0 + s1 - s0] = slab[:, s0:s1].astype(out.dtype)

    return _call(body, name=name, grid=(layers, k // tr),
                 in_specs=[pl.BlockSpec((1, N_CHIPS, tr, n), lambda layer, i: (layer, 0, i, 0))],
                 out_specs=[pl.BlockSpec((1, tr, n_main), lambda layer, i: (layer, i, 0)),
                            pl.BlockSpec((1, tr, LANES), lambda layer, i: (layer, i, 0))],
                 out_shape=[_sds((layers, k, n_main), slabs.dtype), _sds((layers, k, LANES), slabs.dtype)],
                 sem=("parallel", "parallel"))(slabs)


def _w_in_to_slabs(d_main, d_f, n, f_off, heads, *, name):
    layers = len(d_main)
    k = d_main[0].shape[0]
    tr = _pick(k, (128, 64, 32, 16, 8))
    pieces = _w_in_pieces(n, f_off, heads)

    def body(*refs):
        m_refs, f_refs, o_ref = refs[:layers], refs[layers:2 * layers], refs[2 * layers]
        for layer in range(layers):
            for chip, s0, s1, target, t0 in pieces:
                src = m_refs[layer] if target == "main" else f_refs[layer]
                o_ref[layer, chip, :, s0:s1] = src[:, t0:t0 + s1 - s0]

    return _call(body, name=name, grid=(k // tr,),
                 in_specs=[pl.BlockSpec((tr, d_main[0].shape[1]), lambda i: (i, 0))] * layers
                 + [pl.BlockSpec((tr, LANES), lambda i: (i, 0))] * layers,
                 out_specs=pl.BlockSpec((layers, N_CHIPS, tr, n), lambda i: (0, 0, i, 0)),
                 out_shape=_sds((layers, N_CHIPS, k, n), F32), sem=("parallel",))(*d_main, *d_f)


def kernel(x, c, w_ada, b_ada, w_in, b_in, conv_a_w, conv_a_b, ln_conv_g, ln_conv_b, w_conv_proj, w_attn_proj, w_mix_out, b_mix_out, ln1_g, ln1_b, w_ffn_up, ffn_conv_w, ffn_conv_b, w_ffn_down, ln2_g, ln2_b, loss_target, m_w_ada, m_b_ada, m_w_in, m_b_in, m_conv_a_w, m_conv_a_b, m_ln_conv_g, m_ln_conv_b, m_w_conv_proj, m_w_attn_proj, m_w_mix_out, m_b_mix_out, m_ln1_g, m_ln1_b, m_w_ffn_up, m_ffn_conv_w, m_ffn_conv_b, m_w_ffn_down, m_ln2_g, m_ln2_b, v_w_ada, v_b_ada, v_w_in, v_b_in, v_conv_a_w, v_conv_a_b, v_ln_conv_g, v_ln_conv_b, v_w_conv_proj, v_w_attn_proj, v_w_mix_out, v_b_mix_out, v_ln1_g, v_ln1_b, v_w_ffn_up, v_ffn_conv_w, v_ffn_conv_b, v_w_ffn_down, v_ln2_g, v_ln2_b):
    weights = dict(zip(WEIGHTS, (w_ada, b_ada, w_in, b_in, conv_a_w, conv_a_b, ln_conv_g, ln_conv_b, w_conv_proj,
                                 w_attn_proj, w_mix_out, b_mix_out, ln1_g, ln1_b, w_ffn_up, ffn_conv_w, ffn_conv_b,
                                 w_ffn_down, ln2_g, ln2_b)))
    mom1 = dict(zip(WEIGHTS, (m_w_ada, m_b_ada, m_w_in, m_b_in, m_conv_a_w, m_conv_a_b, m_ln_conv_g, m_ln_conv_b,
                              m_w_conv_proj, m_w_attn_proj, m_w_mix_out, m_b_mix_out, m_ln1_g, m_ln1_b, m_w_ffn_up,
                              m_ffn_conv_w, m_ffn_conv_b, m_w_ffn_down, m_ln2_g, m_ln2_b)))
    mom2 = dict(zip(WEIGHTS, (v_w_ada, v_b_ada, v_w_in, v_b_in, v_conv_a_w, v_conv_a_b, v_ln_conv_g, v_ln_conv_b,
                              v_w_conv_proj, v_w_attn_proj, v_w_mix_out, v_b_mix_out, v_ln1_g, v_ln1_b, v_w_ffn_up,
                              v_ffn_conv_w, v_ffn_conv_b, v_w_ffn_down, v_ln2_g, v_ln2_b)))
    bsz, seq, d = x.shape
    layers = w_ada.shape[0]
    ch = conv_a_w.shape[2] * N_CHIPS
    width = w_attn_proj.shape[1]
    heads = width // HEAD_DIM
    d_ff = w_ffn_down.shape[1] * N_CHIPS
    dims = dict(bsz=bsz, d=d, ch=ch, heads=heads, alpha=(2.0 * layers) ** 0.25, f_off=2 * ch + 3 * width,
                ga_off=2 * ch + 3 * width, tcf=_pick(d_ff, (256, 128)))
    chip = 2 * lax.axis_index("x") + lax.axis_index("y")
    device = 2 * chip + lax.axis_index("c")
    ada_cols = w_ada.shape[2]

    c_act = _silu_rows(_all_gather8(c, name="gather_c").reshape(N_DEVICES * bsz, d), name="silu_c")
    b_ada_mine = lax.dynamic_slice_in_dim(b_ada, chip * ada_cols, ada_cols, axis=1)
    mod_cols = jnp.stack([_matmul(c_act, (w_ada, layer), "nn", F32, bias=b_ada_mine[layer][None], name=f"ada_l{layer}")
                          for layer in range(layers)])
    mod_all = _all_gather8(mod_cols, name="gather_mod")
    mod_all = jnp.concatenate([mod_all[2 * k] for k in range(N_CHIPS)], axis=-1)
    mod = lax.dynamic_slice_in_dim(mod_all, device * bsz, bsz, axis=1)

    shards = [_pad_rows(weights[wname].astype(BF16) if as_bf16 else weights[wname]) for wname, _, as_bf16 in GATHERED]
    modes = [mode for _, mode, _ in GATHERED]
    whole = _gather_weights(shards, modes, name="gather_weights")
    full = {wname: w[:, :weights[wname].shape[1]] if mode == "cols" else w
            for (wname, mode, _), w in zip(GATHERED, whole)}
    full["w_main"], full["w_f"] = _w_in_from_slabs(full.pop("w_in"), dims["f_off"], heads, name="w_in_from_slabs")
    rep = {wname: weights[wname] for wname in REPLICATED}

    loss_local, grad_x, grads, dmod = _local_step(x, mod, loss_target, full, rep, dims)
    loss = lax.psum(loss_local, ("x", "y", "c"))

    grads["w_in"] = _w_in_to_slabs(grads.pop("w_main"), grads.pop("w_f"), w_in.shape[2], dims["f_off"], heads,
                                   name="w_in_to_slabs")
    shard_shapes = [s.shape for s in shards]
    reduced = _reduce_scatter([_pad_rows(grads[wname]) for wname, _, _ in GATHERED], modes, shard_shapes)
    grad = {wname: r[:, :weights[wname].shape[1]] for (wname, _, _), r in zip(GATHERED, reduced)}

    small = jnp.concatenate([dmod.reshape(-1)] + [grads[wname].reshape(-1) for wname in REPLICATED])
    n_small = small.shape[0]
    rows = -(-n_small // (SUBLANES * LANES)) * SUBLANES
    small = jnp.pad(small, (0, rows * LANES - n_small)).reshape(rows, LANES)
    gathered = _all_gather8(small, name="gather_small")
    n_dmod = dmod.size
    dmod_all = gathered.reshape(N_DEVICES, -1)[:, :n_dmod].reshape(N_DEVICES, layers, bsz, 6 * d)
    dmod_all = jnp.transpose(dmod_all, (1, 0, 2, 3)).reshape(layers, N_DEVICES * bsz, 6 * d)
    summed = _sum_slots(gathered, name="sum_small").reshape(-1)
    off = n_dmod
    for wname in REPLICATED:
        n = weights[wname].size
        grad[wname] = summed[off:off + n].reshape(weights[wname].shape)
        off += n
    dmod_mine = lax.dynamic_slice_in_dim(dmod_all, chip * ada_cols, ada_cols, axis=2)
    grad["w_ada"] = jnp.stack([_matmul(c_act, dmod_mine[layer], "tn", F32, name=f"ada_dw_l{layer}")
                               for layer in range(layers)])
    grad["b_ada"] = jnp.stack([_colsum(dmod_all[layer], name=f"ada_db_l{layer}")[0] for layer in range(layers)])

    delta, new_m, new_v = {}, {}, {}
    for wname in WEIGHTS:
        delta[wname], new_m[wname], new_v[wname] = _adamw(weights[wname], grad[wname], mom1[wname], mom2[wname],
                                                          name=f"adamw_{wname}")
    return (loss, grad_x, *[grad[wname] for wname in WEIGHTS], *[delta[wname] for wname in WEIGHTS],
            *[new_m[wname] for wname in WEIGHTS], *[new_v[wname] for wname in WEIGHTS])
```

```python
import math

import jax
import jax.numpy as jnp
from jax import lax
from jax.experimental import pallas as pl
from jax.experimental.pallas import tpu as pltpu

F32 = jnp.float32
BF16 = jnp.bfloat16
MESH = pl.DeviceIdType.MESH

LN_EPS = 1e-5
HEAD_DIM = 64
ATTN_SCALE = HEAD_DIM ** -0.5
NEG = -1e30
FFN_PAD = 8
LANES = 128
SUBLANES = 8
ROW_CHUNK = 256
ATTN_BLOCK = 256
ATTN_UNROLL = 4
N_CHIPS = 4
N_DEVICES = 8
VMEM_LIMIT = 56 * 1024 * 1024

ADAM_LR = 0.001
ADAM_B1 = 0.9
ADAM_B2 = 0.999
ADAM_EPS = 1e-08
ADAM_WD = 0.01
ADAM_STEP = 10

GATHERED = (("w_in", "slab", True), ("conv_a_w", "cols", False), ("w_conv_proj", "cols", True),
            ("w_attn_proj", "cols", True), ("w_mix_out", "rows", True), ("w_ffn_up", "cols", True),
            ("ffn_conv_w", "cols", False), ("w_ffn_down", "rows", True))
REPLICATED = ("b_in", "conv_a_b", "ln_conv_g", "ln_conv_b", "b_mix_out", "ln1_g", "ln1_b",
              "ffn_conv_b", "ln2_g", "ln2_b")
WEIGHTS = ("w_ada", "b_ada", "w_in", "b_in", "conv_a_w", "conv_a_b", "ln_conv_g", "ln_conv_b",
           "w_conv_proj", "w_attn_proj", "w_mix_out", "b_mix_out", "ln1_g", "ln1_b", "w_ffn_up",
           "ffn_conv_w", "ffn_conv_b", "w_ffn_down", "ln2_g", "ln2_b")


def _pick(n, cands):
    for cand in cands:
        if n % cand == 0:
            return cand
    return n


def _call(body, *, name, grid, in_specs, out_specs, out_shape, scratch=(), sem=None):
    return pl.pallas_call(
        body, name=name, grid=grid, in_specs=in_specs, out_specs=out_specs, out_shape=out_shape,
        scratch_shapes=list(scratch),
        compiler_params=pltpu.CompilerParams(dimension_semantics=sem, vmem_limit_bytes=VMEM_LIMIT))


def _sds(shape, dtype):
    return jax.ShapeDtypeStruct(tuple(shape), dtype)


def _chunked(rows, fn):
    chunk = min(ROW_CHUNK, rows)
    if rows == chunk:
        fn(pl.ds(0, rows))
        return

    def step(i, carry):
        fn(pl.ds(pl.multiple_of(i * chunk, chunk), chunk))
        return carry

    lax.fori_loop(0, rows // chunk, step, 0)


def _matmul(a, b, mode, out_dtype, *, bias=None, add=None, colsum=False, b_k_first=0, out_cols=None, into=None,
            name):
    a, a_layer = a if isinstance(a, tuple) else (a, None)
    b, b_layer = b if isinstance(b, tuple) else (b, None)
    if mode == "nn":
        (m, k), (_, n) = a.shape[-2:], b.shape[-2:]
    elif mode == "nt":
        (m, k), (n, _) = a.shape[-2:], b.shape[-2:]
    else:
        (k, m), (_, n) = a.shape[-2:], b.shape[-2:]
    tm = _pick(m, (1024, 1408, 512, 256, 128))
    tn = _pick(n, (1536, 1408, 1024, 512, 256, 128))
    tk = k if k <= 1536 else _pick(k, (1024, 1536, 1408, 512, 256, 128))
    nk = k // tk
    assert b_k_first % tk == 0 and (b_k_first == 0 or mode == "nt")
    k_blk0 = b_k_first // tk

    def spec(layer, shape, index):
        if layer is None:
            return pl.BlockSpec(shape, index)
        return pl.BlockSpec((None,) + shape, lambda i, j, kk: (layer,) + index(i, j, kk))

    if mode == "nn":
        a_spec = spec(a_layer, (tm, tk), lambda i, j, kk: (i, kk))
        b_spec = spec(b_layer, (tk, tn), lambda i, j, kk: (kk, j))
        dims = (((1,), (0,)), ((), ()))
    elif mode == "nt":
        a_spec = spec(a_layer, (tm, tk), lambda i, j, kk: (i, kk))
        b_spec = spec(b_layer, (tn, tk), lambda i, j, kk: (j, k_blk0 + kk))
        dims = (((1,), (1,)), ((), ()))
    else:
        a_spec = spec(a_layer, (tk, tm), lambda i, j, kk: (kk, i))
        b_spec = spec(b_layer, (tk, tn), lambda i, j, kk: (kk, j))
        dims = (((0,), (0,)), ((), ()))
    in_specs = [a_spec, b_spec]
    operands = [a, b]
    if bias is not None:
        in_specs.append(pl.BlockSpec((1, tn), lambda i, j, kk: (0, j)))
        operands.append(bias)
    if add is not None:
        in_specs.append(pl.BlockSpec((tm, tn), lambda i, j, kk: (i, j)))
        operands.append(add)

    def body(a_ref, b_ref, *rest):
        rest = list(rest)
        bias_ref = rest.pop(0) if bias is not None else None
        add_ref = rest.pop(0) if add is not None else None
        if into is not None:
            rest.pop(0)
        o_ref = rest.pop(0)
        prod = lax.dot_general(a_ref[...].astype(BF16), b_ref[...].astype(BF16), dims,
                               preferred_element_type=F32)
        if colsum:
            cs_ref = rest.pop(0)
            part = jnp.sum(b_ref[...].astype(F32), axis=0, keepdims=True)

            @pl.when(pl.program_id(2) == 0)
            def _():
                cs_ref[...] = part

            @pl.when(pl.program_id(2) > 0)
            def _():
                cs_ref[...] += part

        def finish(r):
            if bias_ref is not None:
                r = r + bias_ref[...]
            if add_ref is not None:
                r = r + add_ref[...]
            o_ref[...] = r.astype(o_ref.dtype)

        if nk == 1:
            finish(prod)
            return
        acc_ref = rest.pop(0)
        kk = pl.program_id(2)

        @pl.when(kk == 0)
        def _():
            acc_ref[...] = prod

        @pl.when(kk > 0)
        def _():
            acc_ref[...] += prod

        @pl.when(kk == nk - 1)
        def _():
            finish(acc_ref[...])

    col_blk0 = 0
    aliases = {}
    out_shape = _sds((m, n if out_cols is None else out_cols), out_dtype)
    if into is not None:
        wide, col_first = into
        assert col_first % tn == 0 and not colsum
        col_blk0 = col_first // tn
        out_shape = _sds(wide.shape, wide.dtype)
        aliases = {len(operands): 0}
        in_specs.append(pl.BlockSpec(memory_space=pl.ANY))
        operands.append(wide)
    out_specs = pl.BlockSpec((tm, tn), lambda i, j, kk: (i, col_blk0 + j))
    if colsum:
        assert mode == "tn" and m == tm
        out_specs = [out_specs, pl.BlockSpec((1, tn), lambda i, j, kk: (0, j))]
        out_shape = [out_shape, _sds((1, n), F32)]
    return pl.pallas_call(
        body, name=name, grid=(m // tm, n // tn, nk), in_specs=in_specs, out_specs=out_specs, out_shape=out_shape,
        scratch_shapes=[pltpu.VMEM((tm, tn), F32)] if nk > 1 else [], input_output_aliases=aliases,
        compiler_params=pltpu.CompilerParams(dimension_semantics=("parallel", "parallel", "arbitrary"),
                                             vmem_limit_bytes=VMEM_LIMIT))(*operands)


def _colsum(x, *, name):
    rows, n = x.shape
    tr = _pick(rows, (1024, 512, 256, 128))
    tn = _pick(n, (512, 256, 128))

    def body(x_ref, o_ref):
        @pl.when(pl.program_id(1) == 0)
        def _():
            o_ref[...] = jnp.zeros_like(o_ref)

        o_ref[...] += jnp.sum(x_ref[...].astype(F32), axis=0, keepdims=True)

    return _call(body, name=name, grid=(n // tn, rows // tr),
                 in_specs=[pl.BlockSpec((tr, tn), lambda j, i: (i, j))],
                 out_specs=pl.BlockSpec((1, tn), lambda j, i: (0, j)),
                 out_shape=_sds((1, n), F32), sem=("parallel", "arbitrary"))(x)


def _ln_stats(x):
    mu = jnp.mean(x, axis=-1, keepdims=True)
    xc = x - mu
    var = jnp.mean(xc * xc, axis=-1, keepdims=True)
    rstd = lax.rsqrt(var + LN_EPS)
    return xc * rstd, rstd


def _ln_bwd(dn, n, rstd):
    return rstd * (dn - jnp.mean(dn, axis=-1, keepdims=True) - n * jnp.mean(dn * n, axis=-1, keepdims=True))


def _seq_tiles(t, bsz, cands=(1024, 512, 256, 128, 64, 32, 16, 8)):
    s = t // bsz
    ts = _pick(s, cands)
    return s, ts, s // ts


def _ln_mod_fwd(x, scale, shift, bsz, *, name):
    t, d = x.shape
    _, ts, ns = _seq_tiles(t, bsz)

    def body(x_ref, sc_ref, sh_ref, u_ref):
        one_scale = 1.0 + sc_ref[0]
        shift_v = sh_ref[0]

        def piece(rows):
            n, _ = _ln_stats(x_ref[rows, :])
            u_ref[rows, :] = (n * one_scale + shift_v).astype(u_ref.dtype)

        _chunked(ts, piece)

    row = pl.BlockSpec((ts, d), lambda b, i: (b * ns + i, 0))
    per = pl.BlockSpec((1, 1, d), lambda b, i: (b, 0, 0))
    return _call(body, name=name, grid=(bsz, ns), in_specs=[row, per, per], out_specs=row,
                 out_shape=_sds((t, d), BF16), sem=("parallel", "parallel"))(x, scale, shift)


def _ln_mod_bwd(du, dz_narrow, w_narrow, x, scale, dr, alpha, bsz, *, name):
    t, d = x.shape
    _, ts, ns = _seq_tiles(t, bsz)
    lanes = dz_narrow.shape[1]

    def body(du_ref, dzn_ref, wn_ref, x_ref, sc_ref, dr_ref, dx_ref, dsc_ref, dsh_ref):
        @pl.when(pl.program_id(1) == 0)
        def _():
            dsc_ref[...] = jnp.zeros_like(dsc_ref)
            dsh_ref[...] = jnp.zeros_like(dsh_ref)

        one_scale = 1.0 + sc_ref[0]
        w_n = wn_ref[...].astype(BF16)

        def piece(rows):
            du_v = du_ref[rows, :] + lax.dot_general(dzn_ref[rows, :].astype(BF16), w_n, (((1,), (1,)), ((), ())),
                                                     preferred_element_type=F32)
            n, rstd = _ln_stats(x_ref[rows, :])
            dsc_ref[0] += jnp.sum(du_v * n, axis=0, keepdims=True)
            dsh_ref[0] += jnp.sum(du_v, axis=0, keepdims=True)
            dx_ref[rows, :] = alpha * dr_ref[rows, :] + _ln_bwd(du_v * one_scale, n, rstd)

        _chunked(ts, piece)

    row = pl.BlockSpec((ts, d), lambda b, i: (b * ns + i, 0))
    per = pl.BlockSpec((1, 1, d), lambda b, i: (b, 0, 0))
    return _call(body, name=name, grid=(bsz, ns),
                 in_specs=[row, pl.BlockSpec((ts, lanes), lambda b, i: (b * ns + i, 0)),
                           pl.BlockSpec((d, lanes), lambda b, i: (0, 0)), row, per, row],
                 out_specs=[row, per, per],
                 out_shape=[_sds((t, d), F32), _sds((bsz, 1, d), F32), _sds((bsz, 1, d), F32)],
                 sem=("parallel", "arbitrary"))(du, dz_narrow, w_narrow, x, scale, dr)


def _ln_res_bwd(do, x, y, gate, g, alpha, bsz, *, name):
    t, d = x.shape
    _, ts, ns = _seq_tiles(t, bsz)

    def body(do_ref, x_ref, y_ref, gt_ref, g_ref, dr_ref, dy_ref, dgt_ref, dg_ref, db_ref, dys_ref):
        first_tile = pl.program_id(1) == 0

        @pl.when(first_tile)
        def _():
            dgt_ref[...] = jnp.zeros_like(dgt_ref)

        @pl.when(jnp.logical_and(first_tile, pl.program_id(0) == 0))
        def _():
            dg_ref[...] = jnp.zeros_like(dg_ref)
            db_ref[...] = jnp.zeros_like(db_ref)
            dys_ref[...] = jnp.zeros_like(dys_ref)

        one_gate = 1.0 + gt_ref[0]

        def piece(rows):
            do_v = do_ref[rows, :]
            y_v = y_ref[rows, :]
            n, rstd = _ln_stats(alpha * x_ref[rows, :] + one_gate * y_v)
            dg_ref[...] += jnp.sum(do_v * n, axis=0, keepdims=True)
            db_ref[...] += jnp.sum(do_v, axis=0, keepdims=True)
            dr = _ln_bwd(do_v * g_ref[...], n, rstd)
            dr_ref[rows, :] = dr
            dy = one_gate * dr
            dy_ref[rows, :] = dy.astype(dy_ref.dtype)
            dys_ref[...] += jnp.sum(dy, axis=0, keepdims=True)
            dgt_ref[0] += jnp.sum(dr * y_v, axis=0, keepdims=True)

        _chunked(ts, piece)

    row = pl.BlockSpec((ts, d), lambda bb, i: (bb * ns + i, 0))
    per = pl.BlockSpec((1, 1, d), lambda bb, i: (bb, 0, 0))
    vec = pl.BlockSpec((1, d), lambda bb, i: (0, 0))
    return _call(body, name=name, grid=(bsz, ns), in_specs=[row, row, row, per, vec],
                 out_specs=[row, row, per, vec, vec, vec],
                 out_shape=[_sds((t, d), F32), _sds((t, d), BF16), _sds((bsz, 1, d), F32),
                            _sds((1, d), F32), _sds((1, d), F32), _sds((1, d), F32)],
                 sem=("arbitrary", "arbitrary"))(do, x, y, gate, g)


def _ln_res_mod_fwd(x, y, gate, g, b, scale, shift, alpha, bsz, *, name):
    t, d = x.shape
    _, ts, ns = _seq_tiles(t, bsz)

    def body(x_ref, y_ref, gt_ref, g_ref, b_ref, sc_ref, sh_ref, o_ref, u_ref):
        one_gate = 1.0 + gt_ref[0]
        one_scale = 1.0 + sc_ref[0]
        shift_v = sh_ref[0]

        def piece(rows):
            n, _ = _ln_stats(alpha * x_ref[rows, :] + one_gate * y_ref[rows, :])
            x1 = n * g_ref[...] + b_ref[...]
            o_ref[rows, :] = x1
            n1, _ = _ln_stats(x1)
            u_ref[rows, :] = (n1 * one_scale + shift_v).astype(u_ref.dtype)

        _chunked(ts, piece)

    row = pl.BlockSpec((ts, d), lambda bb, i: (bb * ns + i, 0))
    per = pl.BlockSpec((1, 1, d), lambda bb, i: (bb, 0, 0))
    vec = pl.BlockSpec((1, d), lambda bb, i: (0, 0))
    return _call(body, name=name, grid=(bsz, ns), in_specs=[row, row, per, vec, vec, per, per],
                 out_specs=[row, row], out_shape=[_sds((t, d), F32), _sds((t, d), BF16)],
                 sem=("parallel", "parallel"))(x, y, gate, g, b, scale, shift)


def _ln_mod_res_bwd(du, dr_up, scale, x, y, gate, g, b, alpha, bsz, *, narrow=None, name):
    t, d = x.shape
    _, ts, ns = _seq_tiles(t, bsz, (512, 256, 128, 64, 32, 16, 8))

    def body(du_ref, *refs):
        refs = list(refs)
        dzn_ref, wn_ref = (refs.pop(0), refs.pop(0)) if narrow is not None else (None, None)
        (up_ref, sc_ref, x_ref, y_ref, gt_ref, g_ref, b_ref,
         dr_ref, dy_ref, dgt_ref, dg_ref, db_ref, dys_ref, dsc_ref, dsh_ref) = refs
        first_tile = pl.program_id(1) == 0

        @pl.when(first_tile)
        def _():
            for ref in (dgt_ref, dsc_ref, dsh_ref):
                ref[...] = jnp.zeros_like(ref)

        @pl.when(jnp.logical_and(first_tile, pl.program_id(0) == 0))
        def _():
            for ref in (dg_ref, db_ref, dys_ref):
                ref[...] = jnp.zeros_like(ref)

        one_gate = 1.0 + gt_ref[0]
        one_scale = 1.0 + sc_ref[0]

        def piece(rows):
            y_v = y_ref[rows, :]
            n, rstd = _ln_stats(alpha * x_ref[rows, :] + one_gate * y_v)
            n1, rstd1 = _ln_stats(n * g_ref[...] + b_ref[...])
            du_v = du_ref[rows, :]
            if narrow is not None:
                du_v = du_v + lax.dot_general(dzn_ref[rows, :].astype(BF16), wn_ref[...].astype(BF16),
                                              (((1,), (1,)), ((), ())), preferred_element_type=F32)
            dsc_ref[0] += jnp.sum(du_v * n1, axis=0, keepdims=True)
            dsh_ref[0] += jnp.sum(du_v, axis=0, keepdims=True)
            dx1 = alpha * up_ref[rows, :] + _ln_bwd(du_v * one_scale, n1, rstd1)
            dg_ref[...] += jnp.sum(dx1 * n, axis=0, keepdims=True)
            db_ref[...] += jnp.sum(dx1, axis=0, keepdims=True)
            dr = _ln_bwd(dx1 * g_ref[...], n, rstd)
            dr_ref[rows, :] = dr
            dy = one_gate * dr
            dy_ref[rows, :] = dy.astype(dy_ref.dtype)
            dys_ref[...] += jnp.sum(dy, axis=0, keepdims=True)
            dgt_ref[0] += jnp.sum(dr * y_v, axis=0, keepdims=True)

        _chunked(ts, piece)

    row = pl.BlockSpec((ts, d), lambda bb, i: (bb * ns + i, 0))
    per = pl.BlockSpec((1, 1, d), lambda bb, i: (bb, 0, 0))
    vec = pl.BlockSpec((1, d), lambda bb, i: (0, 0))
    extra_specs, extra = [], []
    if narrow is not None:
        lanes = narrow[0].shape[1]
        extra_specs = [pl.BlockSpec((ts, lanes), lambda bb, i: (bb * ns + i, 0)), pl.BlockSpec((d, lanes), lambda bb, i: (0, 0))]
        extra = list(narrow)
    return _call(body, name=name, grid=(bsz, ns), in_specs=[row] + extra_specs + [row, per, row, row, per, vec, vec],
                 out_specs=[row, row, per, vec, vec, vec, per, per],
                 out_shape=[_sds((t, d), F32), _sds((t, d), BF16), _sds((bsz, 1, d), F32), _sds((1, d), F32),
                            _sds((1, d), F32), _sds((1, d), F32), _sds((bsz, 1, d), F32), _sds((bsz, 1, d), F32)],
                 sem=("arbitrary", "arbitrary"))(du, *extra, dr_up, scale, x, y, gate, g, b)


def _ln_res_loss(x, y, gate, g, b, target, alpha, bsz, *, name):
    t, d = x.shape
    _, ts, ns = _seq_tiles(t, bsz)

    def body(x_ref, y_ref, gt_ref, g_ref, b_ref, t_ref, dy_ref, s_ref):
        @pl.when(jnp.logical_and(pl.program_id(0) == 0, pl.program_id(1) == 0))
        def _():
            s_ref[...] = jnp.zeros_like(s_ref)

        one_gate = 1.0 + gt_ref[0]

        def piece(rows):
            n, _ = _ln_stats(alpha * x_ref[rows, :] + one_gate * y_ref[rows, :])
            e = (n * g_ref[...] + b_ref[...]) - t_ref[rows, :]
            dy_ref[rows, :] = e * (1.0 / d)
            s_ref[...] += jnp.sum(e * e, axis=0, keepdims=True)

        _chunked(ts, piece)

    row = pl.BlockSpec((ts, d), lambda bb, i: (bb * ns + i, 0))
    per = pl.BlockSpec((1, 1, d), lambda bb, i: (bb, 0, 0))
    vec = pl.BlockSpec((1, d), lambda bb, i: (0, 0))
    return _call(body, name=name, grid=(bsz, ns), in_specs=[row, row, per, vec, vec, row], out_specs=[row, vec],
                 out_shape=[_sds((t, d), F32), _sds((1, d), F32)],
                 sem=("arbitrary", "arbitrary"))(x, y, gate, g, b, target)


def _sigmoid(v):
    return 1.0 / (1.0 + jnp.exp(-v))


def _silu_rows(c, *, name):
    rows, d = c.shape

    def body(c_ref, o_ref):
        v = c_ref[...]
        o_ref[...] = (v * _sigmoid(v)).astype(o_ref.dtype)

    full = pl.BlockSpec((rows, d), lambda i: (0, 0))
    return _call(body, name=name, grid=(1,), in_specs=[full], out_specs=full,
                 out_shape=_sds((rows, d), BF16), sem=("arbitrary",))(c)


def _gate_cols(d, ga_off):
    tc = _pick(math.gcd(d, ga_off), (512, 256, 128))
    return tc, ga_off // tc, (ga_off + d) // tc


def _gate_merge_fwd(z, ya, yb, ga_off, *, name):
    t, d = ya.shape
    tr = _pick(t, (1024, 512, 256, 128, 64, 32, 16, 8))
    tc, ga_blk, gb_blk = _gate_cols(d, ga_off)

    def body(ga_ref, gb_ref, ya_ref, yb_ref, o_ref):
        def piece(rows):
            o_ref[rows, :] = (_sigmoid(ga_ref[rows, :].astype(F32)) * ya_ref[rows, :].astype(F32)
                              + _sigmoid(gb_ref[rows, :].astype(F32)) * yb_ref[rows, :].astype(F32)
                              ).astype(o_ref.dtype)

        _chunked(tr, piece)

    blk = pl.BlockSpec((tr, tc), lambda i, j: (i, j))
    return _call(body, name=name, grid=(t // tr, d // tc),
                 in_specs=[pl.BlockSpec((tr, tc), lambda i, j: (i, ga_blk + j)),
                           pl.BlockSpec((tr, tc), lambda i, j: (i, gb_blk + j)), blk, blk],
                 out_specs=blk, out_shape=_sds((t, d), BF16), sem=("parallel", "parallel"))(z, z, ya, yb)


def _gate_merge_bwd(z, ya, yb, dm, ga_off, *, name):
    t, d = ya.shape
    tr = _pick(t, (1024, 512, 256, 128, 64, 32, 16, 8))
    tc, ga_blk, gb_blk = _gate_cols(d, ga_off)

    def body(ga_ref, gb_ref, ya_ref, yb_ref, dm_ref, dya_ref, dyb_ref, dga_ref, dgb_ref):
        def piece(rows):
            dm_v = dm_ref[rows, :].astype(F32)
            sa = _sigmoid(ga_ref[rows, :].astype(F32))
            sb = _sigmoid(gb_ref[rows, :].astype(F32))
            dya_ref[rows, :] = (dm_v * sa).astype(dya_ref.dtype)
            dyb_ref[rows, :] = (dm_v * sb).astype(dyb_ref.dtype)
            dga_ref[rows, :] = (dm_v * ya_ref[rows, :].astype(F32) * sa * (1.0 - sa)).astype(dga_ref.dtype)
            dgb_ref[rows, :] = (dm_v * yb_ref[rows, :].astype(F32) * sb * (1.0 - sb)).astype(dgb_ref.dtype)

        _chunked(tr, piece)

    blk = pl.BlockSpec((tr, tc), lambda i, j: (i, j))
    return _call(body, name=name, grid=(t // tr, d // tc),
                 in_specs=[pl.BlockSpec((tr, tc), lambda i, j: (i, ga_blk + j)),
                           pl.BlockSpec((tr, tc), lambda i, j: (i, gb_blk + j)), blk, blk, blk],
                 out_specs=[blk, blk, blk, blk], out_shape=[_sds((t, d), BF16)] * 4,
                 sem=("parallel", "parallel"))(z, z, ya, yb, dm)


CONV_ROWS = 64
CONV_PAD = 32


def _row_shifts(win):
    total = win.shape[0]
    return [win] + [pltpu.roll(win, total - b, axis=0) for b in range(1, SUBLANES)]


def _shifted_rows(copies, shift):
    start = SUBLANES * (shift // SUBLANES)
    return copies[shift % SUBLANES][start:start + CONV_ROWS]


def _fill_glu(z_ref, ext_ref, s, ch):
    ext_ref[pl.ds(0, CONV_PAD), :] = jnp.zeros((CONV_PAD, ch), F32)

    chunk = min(ROW_CHUNK, s)

    def piece(i, carry):
        start = pl.multiple_of(i * chunk, chunk)
        zz = z_ref[pl.ds(start, chunk), :].astype(F32)
        ext_ref[pl.ds(pl.multiple_of(CONV_PAD + start, CONV_PAD), chunk), :] = zz[:, :ch] * _sigmoid(zz[:, ch:])
        return carry

    lax.fori_loop(0, s // chunk, piece, 0)


def _conv_piece(ext_ref, w_ref, cb_ref, base, kw):
    copies = _row_shifts(ext_ref[pl.ds(base, CONV_ROWS + CONV_PAD), :])
    acc = cb_ref[...] + w_ref[pl.ds(0, 1), :] * _shifted_rows(copies, CONV_PAD - (kw - 1))
    for k in range(1, kw):
        acc = acc + w_ref[pl.ds(k, 1), :] * _shifted_rows(copies, CONV_PAD - (kw - 1) + k)
    return acc, copies


def _conv_branch_fwd(z, w, cb, lg, lb, bsz, ch, *, name):
    t = z.shape[0]
    s = t // bsz
    kw = w.shape[0]

    def body(z_ref, w_ref, cb_ref, lg_ref, lb_ref, o_ref, ext_ref):
        _fill_glu(z_ref, ext_ref, s, ch)

        def step(i, carry):
            base = pl.multiple_of(i * CONV_ROWS, CONV_ROWS)
            a1, _ = _conv_piece(ext_ref, w_ref, cb_ref, base, kw)
            n, _ = _ln_stats(a1)
            a2 = n * lg_ref[...] + lb_ref[...]
            o_ref[pl.ds(base, CONV_ROWS), :] = (a2 * _sigmoid(a2)).astype(o_ref.dtype)
            return carry

        lax.fori_loop(0, s // CONV_ROWS, step, 0)

    vec = pl.BlockSpec((1, ch), lambda b: (0, 0))
    return _call(body, name=name, grid=(bsz,),
                 in_specs=[pl.BlockSpec((s, 2 * ch), lambda b: (b, 0)), pl.BlockSpec((kw, ch), lambda b: (0, 0)),
                           vec, vec, vec],
                 out_specs=pl.BlockSpec((s, ch), lambda b: (b, 0)), out_shape=_sds((t, ch), BF16),
                 scratch=[pltpu.VMEM((CONV_PAD + s, ch), F32)], sem=("parallel",))(z, w, cb, lg, lb)


def _conv_branch_bwd(z, da3, w, cb, lg, lb, bsz, ch, *, name):
    t = z.shape[0]
    s = t // bsz
    kw = w.shape[0]
    n_rows = CONV_ROWS + CONV_PAD

    def body(z_ref, d_ref, w_ref, cb_ref, lg_ref, lb_ref, dz_ref, dw_ref, dcb_ref, dlg_ref, dlb_ref,
             ext_ref, da1_ref):
        @pl.when(pl.program_id(0) == 0)
        def _():
            for ref in (dw_ref, dcb_ref, dlg_ref, dlb_ref):
                ref[...] = jnp.zeros_like(ref)

        _fill_glu(z_ref, ext_ref, s, ch)
        da1_ref[pl.ds(s, CONV_PAD), :] = jnp.zeros((CONV_PAD, ch), F32)

        def grad_a1(i, carry):
            dlg, dlb = carry
            base = pl.multiple_of(i * CONV_ROWS, CONV_ROWS)
            a1, _ = _conv_piece(ext_ref, w_ref, cb_ref, base, kw)
            n, rstd = _ln_stats(a1)
            a2 = n * lg_ref[...] + lb_ref[...]
            sg = _sigmoid(a2)
            da2 = d_ref[pl.ds(base, CONV_ROWS), :] * (sg * (1.0 + a2 * (1.0 - sg)))
            da1_ref[pl.ds(base, CONV_ROWS), :] = _ln_bwd(da2 * lg_ref[...], n, rstd)
            return (dlg + jnp.sum(da2 * n, axis=0, keepdims=True), dlb + jnp.sum(da2, axis=0, keepdims=True))

        zero = jnp.zeros((1, ch), F32)
        dlg, dlb = lax.fori_loop(0, s // CONV_ROWS, grad_a1, (zero, zero))
        dlg_ref[...] += dlg
        dlb_ref[...] += dlb

        def grad_z(i, dcb):
            base = pl.multiple_of(i * CONV_ROWS, CONV_ROWS)
            ahead = _row_shifts(da1_ref[pl.ds(base, n_rows), :])
            dyc = ahead[0][:CONV_ROWS]
            da0 = w_ref[pl.ds(kw - 1, 1), :] * dyc
            for k in range(kw - 1):
                da0 = da0 + w_ref[pl.ds(k, 1), :] * _shifted_rows(ahead, kw - 1 - k)
            behind = _row_shifts(ext_ref[pl.ds(base, n_rows), :])
            for k in range(kw):
                dw_ref[pl.ds(k, 1), :] += jnp.sum(dyc * _shifted_rows(behind, CONV_PAD - (kw - 1) + k),
                                                  axis=0, keepdims=True)
            zz = z_ref[pl.ds(base, CONV_ROWS), :].astype(F32)
            sg = _sigmoid(zz[:, ch:])
            dz_ref[pl.ds(base, CONV_ROWS), :ch] = (da0 * sg).astype(dz_ref.dtype)
            dz_ref[pl.ds(base, CONV_ROWS), ch:] = (da0 * zz[:, :ch] * sg * (1.0 - sg)).astype(dz_ref.dtype)
            return dcb + jnp.sum(dyc, axis=0, keepdims=True)

        dcb_ref[...] += lax.fori_loop(0, s // CONV_ROWS, grad_z, zero)

    vec = pl.BlockSpec((1, ch), lambda b: (0, 0))
    taps = pl.BlockSpec((kw, ch), lambda b: (0, 0))
    return _call(body, name=name, grid=(bsz,),
                 in_specs=[pl.BlockSpec((s, 2 * ch), lambda b: (b, 0)), pl.BlockSpec((s, ch), lambda b: (b, 0)),
                           taps, vec, vec, vec],
                 out_specs=[pl.BlockSpec((s, 2 * ch), lambda b: (b, 0)), taps, vec, vec, vec],
                 out_shape=[_sds((t, 2 * ch), BF16), _sds((kw, ch), F32)] + [_sds((1, ch), F32)] * 3,
                 scratch=[pltpu.VMEM((CONV_PAD + s, ch), F32), pltpu.VMEM((s + CONV_PAD, ch), F32)],
                 sem=("arbitrary",))(z, da3, w, cb, lg, lb)


FFN_ROWS = 64


def _gelu_parts(v):
    cdf = 0.5 * (1.0 + lax.erf(v * (2.0 ** -0.5)))
    return cdf, v * cdf


def _ffn_conv_piece(ext_ref, wb_ref, base):
    win = ext_ref[pl.ds(base, FFN_ROWS + FFN_PAD), :]
    acc = wb_ref[pl.ds(3, 1), :] + wb_ref[pl.ds(2, 1), :] * win[FFN_PAD:]
    acc = acc + wb_ref[pl.ds(1, 1), :] * pltpu.roll(win, 1, axis=0)[FFN_PAD:]
    acc = acc + wb_ref[pl.ds(0, 1), :] * pltpu.roll(win, 2, axis=0)[FFN_PAD:]
    return acc


def _ffn_stage(hg_ref, hl_ref, wg_ref, wl_ref, bg_ref, bl_ref, ext_ref, wb_ref, s, tcf):
    ext_ref[pl.ds(0, FFN_PAD), :] = jnp.zeros((FFN_PAD, 2 * tcf), F32)
    ext_ref[pl.ds(FFN_PAD, s), :tcf] = hg_ref[...].astype(F32)
    ext_ref[pl.ds(FFN_PAD, s), tcf:] = hl_ref[...].astype(F32)
    wb_ref[pl.ds(0, 3), :tcf] = wg_ref[...]
    wb_ref[pl.ds(0, 3), tcf:] = wl_ref[...]
    wb_ref[pl.ds(3, 1), :tcf] = bg_ref[...]
    wb_ref[pl.ds(3, 1), tcf:] = bl_ref[...]


def _ffn_specs(s, tcf, n_f, batch_first):
    def spec(rows, shift):
        if batch_first:
            return pl.BlockSpec((rows, tcf), lambda bb, j: (bb if rows == s else 0, shift + j))
        return pl.BlockSpec((rows, tcf), lambda j, bb: (bb if rows == s else 0, shift + j))

    return [spec(s, 0), spec(s, n_f), spec(3, 0), spec(3, n_f), spec(1, 0), spec(1, n_f)]


def _ffn_act_fwd(hp, w, b, bsz, tcf, *, name):
    t, two_f = hp.shape
    s = t // bsz
    n_f = two_f // (2 * tcf)

    def body(hg_ref, hl_ref, wg_ref, wl_ref, bg_ref, bl_ref, f_ref, ext_ref, wb_ref):
        _ffn_stage(hg_ref, hl_ref, wg_ref, wl_ref, bg_ref, bl_ref, ext_ref, wb_ref, s, tcf)

        def step(i, carry):
            base = pl.multiple_of(i * FFN_ROWS, FFN_ROWS)
            hh = _ffn_conv_piece(ext_ref, wb_ref, base)
            _, gelu = _gelu_parts(hh[:, :tcf])
            f_ref[pl.ds(base, FFN_ROWS), :] = (gelu * hh[:, tcf:]).astype(f_ref.dtype)
            return carry

        lax.fori_loop(0, s // FFN_ROWS, step, 0)

    return _call(body, name=name, grid=(bsz, n_f), in_specs=_ffn_specs(s, tcf, n_f, True),
                 out_specs=pl.BlockSpec((s, tcf), lambda bb, j: (bb, j)),
                 out_shape=_sds((t, two_f // 2), BF16),
                 scratch=[pltpu.VMEM((FFN_PAD + s, 2 * tcf), F32), pltpu.VMEM((SUBLANES, 2 * tcf), F32)],
                 sem=("parallel", "parallel"))(hp, hp, w, w, b, b)


def _ffn_act_bwd(hp, df, w, b, bsz, tcf, *, name):
    t, two_f = hp.shape
    s = t // bsz
    f_dim = two_f // 2
    n_f = f_dim // tcf
    gw = 2 * tcf
    n_rows = FFN_ROWS + FFN_PAD

    def body(hg_ref, hl_ref, wg_ref, wl_ref, bg_ref, bl_ref, df_ref,
             dhg_ref, dhl_ref, dwg_ref, dwl_ref, dbg_ref, dbl_ref, ext_ref, wb_ref, dh_ref):
        @pl.when(pl.program_id(1) == 0)
        def _():
            for ref in (dwg_ref, dwl_ref, dbg_ref, dbl_ref):
                ref[...] = jnp.zeros_like(ref)

        _ffn_stage(hg_ref, hl_ref, wg_ref, wl_ref, bg_ref, bl_ref, ext_ref, wb_ref, s, tcf)
        dh_ref[pl.ds(s, FFN_PAD), :] = jnp.zeros((FFN_PAD, gw), F32)

        def grad_h(i, carry):
            base = pl.multiple_of(i * FFN_ROWS, FFN_ROWS)
            hh = _ffn_conv_piece(ext_ref, wb_ref, base)
            hg = hh[:, :tcf]
            d = df_ref[pl.ds(base, FFN_ROWS), :].astype(F32)
            cdf, gelu = _gelu_parts(hg)
            pdf = jnp.exp(-0.5 * hg * hg) * (1.0 / math.sqrt(2.0 * math.pi))
            dh_ref[pl.ds(base, FFN_ROWS), :tcf] = d * hh[:, tcf:] * (cdf + hg * pdf)
            dh_ref[pl.ds(base, FFN_ROWS), tcf:] = d * gelu
            return carry

        lax.fori_loop(0, s // FFN_ROWS, grad_h, 0)

        def grad_x(i, carry):
            dw0, dw1, dw2, dbs = carry
            base = pl.multiple_of(i * FFN_ROWS, FFN_ROWS)
            nxt = dh_ref[pl.ds(base, n_rows), :]
            dyc = nxt[:FFN_ROWS]
            dx = wb_ref[pl.ds(2, 1), :] * dyc
            dx = dx + wb_ref[pl.ds(1, 1), :] * pltpu.roll(nxt, n_rows - 1, axis=0)[:FFN_ROWS]
            dx = dx + wb_ref[pl.ds(0, 1), :] * pltpu.roll(nxt, n_rows - 2, axis=0)[:FFN_ROWS]
            dhg_ref[pl.ds(base, FFN_ROWS), :] = dx[:, :tcf].astype(dhg_ref.dtype)
            dhl_ref[pl.ds(base, FFN_ROWS), :] = dx[:, tcf:].astype(dhl_ref.dtype)
            win = ext_ref[pl.ds(base, n_rows), :]
            dw2 = dw2 + jnp.sum(dyc * win[FFN_PAD:], axis=0, keepdims=True)
            dw1 = dw1 + jnp.sum(dyc * pltpu.roll(win, 1, axis=0)[FFN_PAD:], axis=0, keepdims=True)
            dw0 = dw0 + jnp.sum(dyc * pltpu.roll(win, 2, axis=0)[FFN_PAD:], axis=0, keepdims=True)
            return dw0, dw1, dw2, dbs + jnp.sum(dyc, axis=0, keepdims=True)

        zero = jnp.zeros((1, gw), F32)
        sums = lax.fori_loop(0, s // FFN_ROWS, grad_x, (zero, zero, zero, zero))
        for k in range(3):
            dwg_ref[pl.ds(k, 1), :] += sums[k][:, :tcf]
            dwl_ref[pl.ds(k, 1), :] += sums[k][:, tcf:]
        dbg_ref[...] += sums[3][:, :tcf]
        dbl_ref[...] += sums[3][:, tcf:]

    half = pl.BlockSpec((s, tcf), lambda j, bb: (bb, j))
    taps = pl.BlockSpec((3, tcf), lambda j, bb: (0, j))
    bias = pl.BlockSpec((1, tcf), lambda j, bb: (0, j))
    return _call(body, name=name, grid=(n_f, bsz), in_specs=_ffn_specs(s, tcf, n_f, False) + [half],
                 out_specs=[half, half, taps, taps, bias, bias],
                 out_shape=[_sds((t, f_dim), BF16)] * 2 + [_sds((3, f_dim), F32)] * 2 + [_sds((1, f_dim), F32)] * 2,
                 scratch=[pltpu.VMEM((FFN_PAD + s, gw), F32), pltpu.VMEM((SUBLANES, gw), F32),
                          pltpu.VMEM((s + FFN_PAD, gw), F32)],
                 sem=("parallel", "arbitrary"))(hp, hp, w, w, b, b, df)


def _split3(v):
    hi = v.astype(BF16)
    r = v - hi.astype(F32)
    mid = r.astype(BF16)
    lo = (r - mid.astype(F32)).astype(BF16)
    return hi, mid, lo


def _tri_dot(tri, v):
    out = None
    for part in _split3(v):
        term = jnp.dot(tri, part, preferred_element_type=F32)
        out = term if out is None else out + term
    return out


def _fgate_fwd(zf, bsz, heads, *, name):
    t, lanes = zf.shape
    s, blk, nb = _seq_tiles(t, bsz, (ATTN_BLOCK, 128))

    def body(z_ref, cumt_ref, cumb_ref, carry_ref):
        @pl.when(pl.program_id(1) == 0)
        def _():
            carry_ref[...] = jnp.zeros_like(carry_ref)

        z = z_ref[...]
        lf = jnp.minimum(z, 0.0) - jnp.log1p(jnp.exp(-jnp.abs(z)))
        r = lax.broadcasted_iota(jnp.int32, (blk, blk), 0)
        c = lax.broadcasted_iota(jnp.int32, (blk, blk), 1)
        tri = (r >= c).astype(BF16)
        cum = _tri_dot(tri, lf) + carry_ref[...]
        carry_ref[...] = cum[blk - 1:blk, :]
        cumt_ref[0] = jnp.transpose(cum)[:heads, :]
        for h in range(heads):
            cumb_ref[0, h] = jnp.broadcast_to(cum[:, h:h + 1], (blk, lanes))

    return _call(body, name=name, grid=(bsz, nb),
                 in_specs=[pl.BlockSpec((blk, lanes), lambda b, i: (b * nb + i, 0))],
                 out_specs=[pl.BlockSpec((1, heads, blk), lambda b, i: (b, 0, i)),
                            pl.BlockSpec((1, heads, blk, lanes), lambda b, i: (b, 0, i, 0))],
                 out_shape=[_sds((bsz, heads, s), F32), _sds((bsz, heads, s, lanes), F32)],
                 scratch=[pltpu.VMEM((1, lanes), F32)], sem=("parallel", "arbitrary"))(zf)


def _fgate_bwd(dcum, zf, bsz, *, name):
    t, lanes = zf.shape
    pairs = dcum.shape[1]
    s, blk, nb = _seq_tiles(t, bsz, (ATTN_BLOCK, 128))

    def body(d_ref, z_ref, o_ref, carry_ref):
        @pl.when(pl.program_id(1) == 0)
        def _():
            carry_ref[...] = jnp.zeros_like(carry_ref)

        dcol = d_ref[0, 0]
        for p in range(1, pairs):
            dcol = dcol + d_ref[0, p]
        r = lax.broadcasted_iota(jnp.int32, (blk, blk), 0)
        c = lax.broadcasted_iota(jnp.int32, (blk, blk), 1)
        tri = (c >= r).astype(BF16)
        suf = _tri_dot(tri, dcol) + carry_ref[...]
        carry_ref[...] = suf[0:1, :]
        o_ref[...] = suf * _sigmoid(-z_ref[...])

    return _call(body, name=name, grid=(bsz, nb),
                 in_specs=[pl.BlockSpec((1, pairs, blk, lanes), lambda b, i: (b, 0, nb - 1 - i, 0)),
                           pl.BlockSpec((blk, lanes), lambda b, i: (b * nb + nb - 1 - i, 0))],
                 out_specs=pl.BlockSpec((blk, lanes), lambda b, i: (b * nb + nb - 1 - i, 0)),
                 out_shape=_sds((t, lanes), F32), scratch=[pltpu.VMEM((1, lanes), F32)],
                 sem=("parallel", "arbitrary"))(dcum, zf)


def _to_features_major(z, col_off, width, n, *, name):
    t = z.shape[0]
    tr = _pick(t, (512, 256, 128))
    first = col_off // width

    def body(*refs):
        o_ref = refs[n]
        for g in range(n):
            o_ref[pl.ds(g * width, width), :] = jnp.transpose(refs[g][...].astype(F32)).astype(o_ref.dtype)

    return _call(body, name=name, grid=(t // tr,),
                 in_specs=[pl.BlockSpec((tr, width), lambda i, g=g: (i, first + g)) for g in range(n)],
                 out_specs=pl.BlockSpec((n * width, tr), lambda i: (0, i)),
                 out_shape=_sds((n * width, t), BF16), sem=("parallel",))(*([z] * n))


def _to_rows_major(xt, *, name):
    w, t = xt.shape
    tr = _pick(t, (512, 256, 128))

    def body(x_ref, o_ref):
        o_ref[...] = jnp.transpose(x_ref[...]).astype(o_ref.dtype)

    return _call(body, name=name, grid=(t // tr,),
                 in_specs=[pl.BlockSpec((w, tr), lambda i: (0, i))],
                 out_specs=pl.BlockSpec((tr, w), lambda i: (i, 0)),
                 out_shape=_sds((t, w), BF16), sem=("parallel",))(xt)


def _loop_by_twos(lo, hi, body, carry):
    count = hi - lo

    def group(n, first, cr):
        for u in range(n):
            cr = body(first + u, cr)
        return cr

    trips = count // ATTN_UNROLL
    carry = lax.fori_loop(0, trips, lambda t, cr: group(ATTN_UNROLL, lo + ATTN_UNROLL * t, cr), carry)
    rest = count - ATTN_UNROLL * trips
    first = lo + ATTN_UNROLL * trips
    for n in range(ATTN_UNROLL - 1, 0, -1):
        carry = lax.cond(rest == n, lambda cr, n=n: group(n, first, cr), lambda cr: cr, carry)
    return carry


def _head_masks(shape, axis):
    feat = lax.broadcasted_iota(jnp.int32, shape, axis)
    return feat < HEAD_DIM, feat >= HEAD_DIM


def _attn_fwd(z, qkvt, cumt, cumb, bsz, heads, q_off, *, name):
    t = z.shape[0]
    width = heads * HEAD_DIM
    pairs = heads // 2
    s = t // bsz
    blk = ATTN_BLOCK
    nq = s // blk
    k_col = (q_off + width) // LANES
    v_row = 2 * width // LANES
    reps = blk // LANES

    def body(k_ref, qt_ref, vt_ref, cqt_ref, ckb_ref, ot_ref, lse_ref):
        p_id = pl.program_id(1)
        i = pl.program_id(2)
        qt = qt_ref[...]
        masks = _head_masks((LANES, blk), 0)
        qtm = [jnp.where(mk, qt, jnp.zeros_like(qt)) for mk in masks]
        cq = [cqt_ref[0, pl.ds(2 * p_id + hh, 1), :] for hh in range(2)]
        kidx = lax.broadcasted_iota(jnp.int32, (blk, blk), 0)
        qidx = lax.broadcasted_iota(jnp.int32, (blk, blk), 1)

        def block(j, carry, masked):
            off = pl.multiple_of(j * blk, blk)
            kp = k_ref[pl.ds(off, blk), :].astype(BF16)
            vtp = vt_ref[:, pl.ds(off, blk)]
            out = []
            for hh in range(2):
                m, l, acc = carry[hh]
                sc = jnp.dot(kp, qtm[hh], preferred_element_type=F32) * ATTN_SCALE
                ck = ckb_ref[0, hh, pl.ds(off, blk), :]
                sc = (sc + cq[hh]) - jnp.concatenate([ck] * reps, axis=1)
                if masked:
                    sc = jnp.where(qidx >= kidx, sc, NEG)
                m_new = jnp.maximum(m, jnp.max(sc, axis=0, keepdims=True))
                pr = jnp.exp(sc - m_new)
                a = jnp.exp(m - m_new)
                l = a * l + jnp.sum(pr, axis=0, keepdims=True)
                p_hi = pr.astype(BF16)
                p_lo = (pr - p_hi.astype(F32)).astype(BF16)
                pv = (jnp.dot(vtp, p_hi, preferred_element_type=F32)
                      + jnp.dot(vtp, p_lo, preferred_element_type=F32))
                acc = a * acc + pv[hh * HEAD_DIM:(hh + 1) * HEAD_DIM]
                out.append((m_new, l, acc))
            return tuple(out)

        init = tuple((jnp.full((1, blk), NEG, F32), jnp.zeros((1, blk), F32), jnp.zeros((HEAD_DIM, blk), F32))
                     for _ in range(2))
        carry = _loop_by_twos(0, i, lambda j, cr: block(j, cr, False), init)
        carry = block(i, carry, True)
        lse_ref[...] = jnp.zeros_like(lse_ref)
        for hh in range(2):
            m, l, acc = carry[hh]
            ot_ref[pl.ds(hh * HEAD_DIM, HEAD_DIM), :] = acc / l
            lse_ref[0, 0, pl.ds(hh, 1), :] = m + jnp.log(l)

    return _call(body, name=name, grid=(bsz, pairs, nq),
                 in_specs=[pl.BlockSpec((s, LANES), lambda b, p, i: (b, k_col + p)),
                           pl.BlockSpec((LANES, blk), lambda b, p, i: (p, b * nq + i)),
                           pl.BlockSpec((LANES, s), lambda b, p, i: (v_row + p, b)),
                           pl.BlockSpec((1, heads, blk), lambda b, p, i: (b, 0, i)),
                           pl.BlockSpec((1, 2, s, LANES), lambda b, p, i: (b, p, 0, 0))],
                 out_specs=[pl.BlockSpec((LANES, blk), lambda b, p, i: (p, b * nq + i)),
                            pl.BlockSpec((1, 1, SUBLANES, blk), lambda b, p, i: (b, p, 0, i))],
                 out_shape=[_sds((width, t), F32), _sds((bsz, pairs, SUBLANES, s), F32)],
                 sem=("parallel", "parallel", "parallel"))(z, qkvt, qkvt, cumt, cumb)


def _attn_bwd(z, qkvt, cumt, cumb, ot, do, dot, lse, bsz, heads, q_off, *, name):
    t = z.shape[0]
    width = heads * HEAD_DIM
    pairs = heads // 2
    s = t // bsz
    blk = ATTN_BLOCK
    nkv = s // blk
    q_col = q_off // LANES
    k_col = (q_off + width) // LANES
    v_col = (q_off + 2 * width) // LANES
    k_row = width // LANES
    reps = blk // LANES

    def body(k_ref, v_ref, kt_ref, q_ref, qt_ref, do_ref, dot_ref, ot_ref, lse_ref, ckb_ref, cqt_ref,
             dk_ref, dv_ref, dqt_ref, dcum_ref, dqt_acc, ds_acc):
        p_id = pl.program_id(1)
        j = pl.program_id(2)

        @pl.when(j == 0)
        def _():
            dqt_acc[...] = jnp.zeros_like(dqt_acc)

        kp = k_ref[...].astype(BF16)
        vp = v_ref[...].astype(BF16)
        kt = kt_ref[...]
        feat_masks = _head_masks((LANES, blk), 0)
        lane_masks = _head_masks((blk, LANES), 1)
        ktm = [jnp.where(mk, kt, jnp.zeros_like(kt)) for mk in feat_masks]
        ck = [jnp.concatenate([ckb_ref[0, hh]] * reps, axis=1) for hh in range(2)]
        kidx = lax.broadcasted_iota(jnp.int32, (blk, blk), 0)
        qidx = lax.broadcasted_iota(jnp.int32, (blk, blk), 1)
        ds_acc[...] = jnp.zeros_like(ds_acc)

        def block(i, carry, masked):
            dk, dv = carry
            off = pl.multiple_of(i * blk, blk)
            qt = qt_ref[:, pl.ds(off, blk)]
            dt = dot_ref[:, pl.ds(off, blk)]
            o_t = ot_ref[:, pl.ds(off, blk)]
            q_rows = q_ref[pl.ds(off, blk), :].astype(BF16)
            do_rows = do_ref[pl.ds(off, blk), :]
            for hh in range(2):
                qtm = jnp.where(feat_masks[hh], qt, jnp.zeros_like(qt))
                dtm = jnp.where(feat_masks[hh], dt, jnp.zeros_like(dt))
                sc = jnp.dot(kp, qtm, preferred_element_type=F32) * ATTN_SCALE
                sc = (sc + cqt_ref[0, pl.ds(2 * p_id + hh, 1), pl.ds(off, blk)]) - ck[hh]
                pr = jnp.exp(sc - lse_ref[0, 0, pl.ds(hh, 1), pl.ds(off, blk)])
                if masked:
                    pr = jnp.where(qidx >= kidx, pr, 0.0)
                dp = jnp.dot(vp, dtm, preferred_element_type=F32)
                delta = jnp.sum(dtm.astype(F32) * o_t, axis=0, keepdims=True)
                ds = pr * (dp - delta)
                ds_acc[hh] += ds
                dsb = ds.astype(BF16)
                qm = jnp.where(lane_masks[hh], q_rows, jnp.zeros_like(q_rows))
                dom = jnp.where(lane_masks[hh], do_rows, jnp.zeros_like(do_rows))
                dv = dv + jnp.dot(pr.astype(BF16), dom, preferred_element_type=F32)
                dk = dk + jnp.dot(dsb, qm, preferred_element_type=F32) * ATTN_SCALE
                dqt_acc[:, pl.ds(off, blk)] += jnp.dot(ktm[hh], dsb, preferred_element_type=F32) * ATTN_SCALE
            return dk, dv

        zero = jnp.zeros((blk, LANES), F32)
        carry = block(j, (zero, zero), True)
        dk, dv = _loop_by_twos(j + 1, nkv, lambda i, cr: block(i, cr, False), carry)
        dk_ref[...] = dk.astype(dk_ref.dtype)
        dv_ref[...] = dv.astype(dv_ref.dtype)
        lane = lax.broadcasted_iota(jnp.int32, (blk, LANES), 1)
        dcum = jnp.zeros((blk, LANES), F32)
        for hh in range(2):
            col = jnp.sum(ds_acc[hh], axis=1, keepdims=True)
            dcum = jnp.where(lane == 2 * p_id + hh, -col, dcum)
        dcum_ref[0, 0] = dcum

        @pl.when(j == nkv - 1)
        def _():
            dqt_ref[...] = dqt_acc[...]

    key_rows = lambda col: pl.BlockSpec((blk, LANES), lambda b, p, j: (b * nkv + j, col + p))
    seq_t = lambda row: pl.BlockSpec((LANES, s), lambda b, p, j: (row + p, b))
    return _call(body, name=name, grid=(bsz, pairs, nkv),
                 in_specs=[key_rows(k_col), key_rows(v_col),
                           pl.BlockSpec((LANES, blk), lambda b, p, j: (k_row + p, b * nkv + j)),
                           pl.BlockSpec((s, LANES), lambda b, p, j: (b, q_col + p)), seq_t(0),
                           pl.BlockSpec((s, LANES), lambda b, p, j: (b, p)), seq_t(0), seq_t(0),
                           pl.BlockSpec((1, 1, SUBLANES, s), lambda b, p, j: (b, p, 0, 0)),
                           pl.BlockSpec((1, 2, blk, LANES), lambda b, p, j: (b, p, j, 0)),
                           pl.BlockSpec((1, heads, s), lambda b, p, j: (b, 0, 0))],
                 out_specs=[key_rows(0), key_rows(0), seq_t(0),
                            pl.BlockSpec((1, 1, blk, LANES), lambda b, p, j: (b, p, j, 0))],
                 out_shape=[_sds((t, width), BF16), _sds((t, width), BF16), _sds((width, t), F32),
                            _sds((bsz, pairs, s, LANES), F32)],
                 scratch=[pltpu.VMEM((LANES, s), F32), pltpu.VMEM((2, blk, blk), F32)],
                 sem=("parallel", "parallel", "arbitrary"))(z, z, qkvt, z, qkvt, do, dot, ot, lse, cumb, cumt)


def _adamw(w, g, m, v, *, name):
    bc1 = 1.0 - ADAM_B1 ** ADAM_STEP
    bc2 = 1.0 - ADAM_B2 ** ADAM_STEP

    def body(w_ref, g_ref, m_ref, v_ref, d_ref, nm_ref, nv_ref):
        g_v = g_ref[...]
        nm = ADAM_B1 * m_ref[...] + (1.0 - ADAM_B1) * g_v
        nv = ADAM_B2 * v_ref[...] + (1.0 - ADAM_B2) * (g_v * g_v)
        nm_ref[...] = nm
        nv_ref[...] = nv
        d_ref[...] = -ADAM_LR * ((nm / bc1) / (jnp.sqrt(nv / bc2) + ADAM_EPS) + ADAM_WD * w_ref[...])

    if w.ndim == 2:
        grid = (1,)
        blk = pl.BlockSpec(w.shape, lambda i: (0, 0))
    else:
        layers, rows, cols = w.shape
        tr = rows if rows <= 256 else _pick(rows, (256, 128, 64, 32, 16, 8))
        grid = (layers, rows // tr)
        blk = pl.BlockSpec((1, tr, cols), lambda layer, i: (layer, i, 0))
    return tuple(_call(body, name=name, grid=grid, in_specs=[blk] * 4, out_specs=[blk] * 3,
                       out_shape=[_sds(w.shape, F32)] * 3, sem=("parallel",) * len(grid))(w, g, m, v))


_ANY = pl.BlockSpec(memory_space=pl.ANY)


def _comm_call(body, *, name, n_in, out_shape, n_sems):
    scratch = [pltpu.SemaphoreType.DMA((n_sems,)), pltpu.SemaphoreType.DMA((n_sems,)),
               pltpu.SemaphoreType.DMA((len(out_shape),))]
    return pl.pallas_call(body, name=name, in_specs=[_ANY] * n_in, out_specs=[_ANY] * len(out_shape),
                          out_shape=out_shape, scratch_shapes=scratch)


def _place():
    x, y, c = lax.axis_index("x"), lax.axis_index("y"), lax.axis_index("c")
    return x, y, c, [(1 - x, y), (x, 1 - y), (1 - x, 1 - y)]


def _remote(src, dst, send_sems, recv_sems, sem, to):
    return pltpu.make_async_remote_copy(src_ref=src, dst_ref=dst, send_sem=send_sems.at[sem],
                                        recv_sem=recv_sems.at[sem], device_id=to, device_id_type=MESH)


def _all_gather8(v, *, name):
    def body(v_ref, out_ref, send_sems, recv_sems, local_sems):
        x, y, c, _ = _place()
        me = 4 * x + 2 * y + c
        mine = pltpu.make_async_copy(v_ref, out_ref.at[me], local_sems.at[0])
        mine.start()
        peers = []
        for k in range(1, N_DEVICES):
            px = 1 - x if k & 4 else x
            py = 1 - y if k & 2 else y
            pc = 1 - c if k & 1 else c
            peers.append((px, py, pc))
        sends = [_remote(v_ref, out_ref.at[me], send_sems, recv_sems, k, peer) for k, peer in enumerate(peers)]
        for cp in sends:
            cp.start()
        for k, (px, py, pc) in enumerate(peers):
            _remote(v_ref, out_ref.at[4 * px + 2 * py + pc], send_sems, recv_sems, k, (px, py, pc)).wait_recv()
        for cp in sends:
            cp.wait_send()
        mine.wait()

    out = _comm_call(body, name=name, n_in=1, out_shape=[_sds((N_DEVICES,) + v.shape, v.dtype)],
                     n_sems=N_DEVICES - 1)(v)
    return out[0]


def _window(ref, mode, layer, chip, rows, cols, half=None):
    first, count = (0, rows) if half is None else (half * (rows // 2), rows // 2)
    if mode == "slab":
        return ref.at[layer, chip] if half is None else ref.at[layer, chip, pl.ds(first, count), :]
    if mode == "cols":
        col_window = pl.ds(pl.multiple_of(chip * cols, LANES), cols)
        return ref.at[layer, :, col_window] if half is None else ref.at[layer, pl.ds(first, count), col_window]
    return ref.at[layer, pl.ds(pl.multiple_of(chip * rows + first, SUBLANES), count), :]


def _whole_shape(mode, shard_shape):
    layers, rows, cols = shard_shape
    if mode == "slab":
        return (layers, N_CHIPS, rows, cols)
    if mode == "cols":
        assert cols % LANES == 0
        return (layers, rows, N_CHIPS * cols)
    assert rows % 16 == 0
    return (layers, N_CHIPS * rows, cols)


def _gather_weights(shards, modes, *, name):
    n = len(shards)
    meta = [(mode,) + tuple(a.shape[1:]) for a, mode in zip(shards, modes)]
    for a in shards:
        assert a.shape[0] == 2 and a.shape[1] % 2 == 0
    per = 8

    def body(*refs):
        ins, outs = refs[:n], refs[n:2 * n]
        send_sems, recv_sems, _ = refs[2 * n:]
        x, y, c, _ = _place()
        me, x_nbr, y_nbr, diagonal = 2 * x + y, 2 * (1 - x) + y, 2 * x + 1 - y, 2 * (1 - x) + 1 - y
        to_x, to_y, sibling = (1 - x, y, c), (x, 1 - y, c), (x, y, 1 - c)
        sent = []

        def copy(src, dst, sem, to):
            cp = _remote(src, dst, send_sems, recv_sems, sem, to)
            cp.start()
            sent.append(cp)

        def arrived(win, sem):
            _remote(win, win, send_sems, recv_sems, sem, sibling).wait_recv()

        for i, (mode, rows, cols) in enumerate(meta):
            mine = _window(outs[i], mode, c, me, rows, cols)
            copy(ins[i].at[c], mine, per * i, to_x)
            copy(ins[i].at[c], mine, per * i + 1, to_y)
            copy(ins[i], _window(outs[i], mode, slice(None), me, rows, cols), per * i + 7, sibling)
        for i, (mode, rows, cols) in enumerate(meta):
            arrived(_window(outs[i], mode, c, x_nbr, rows, cols), per * i)
            half = _window(outs[i], mode, c, x_nbr, rows, cols, half=0)
            copy(half, half, per * i + 2, to_y)
            win = _window(outs[i], mode, c, x_nbr, rows, cols)
            copy(win, win, per * i + 4, sibling)
            arrived(_window(outs[i], mode, c, y_nbr, rows, cols), per * i + 1)
            half = _window(outs[i], mode, c, y_nbr, rows, cols, half=1)
            copy(half, half, per * i + 3, to_x)
            win = _window(outs[i], mode, c, y_nbr, rows, cols)
            copy(win, win, per * i + 5, sibling)
        for i, (mode, rows, cols) in enumerate(meta):
            arrived(_window(outs[i], mode, c, diagonal, rows, cols, half=0), per * i + 2)
            arrived(_window(outs[i], mode, c, diagonal, rows, cols, half=1), per * i + 3)
            win = _window(outs[i], mode, c, diagonal, rows, cols)
            copy(win, win, per * i + 6, sibling)
        for i, (mode, rows, cols) in enumerate(meta):
            arrived(_window(outs[i], mode, slice(None), me, rows, cols), per * i + 7)
            for k, chip in enumerate((x_nbr, y_nbr, diagonal)):
                arrived(_window(outs[i], mode, 1 - c, chip, rows, cols), per * i + 4 + k)
        for cp in sent:
            cp.wait_send()

    out_shape = [_sds(_whole_shape(mode, a.shape), a.dtype) for a, mode in zip(shards, modes)]
    return _comm_call(body, name=name, n_in=n, out_shape=out_shape, n_sems=per * n)(*shards)


def _rs_swap(grads, *, name):
    n = len(grads)

    def body(*refs):
        ins, outs = refs[:n], refs[n:2 * n]
        send_sems, recv_sems, _ = refs[2 * n:]
        x, y, c, _ = _place()
        copies = [_remote(ins[i].at[1 - c], outs[i], send_sems, recv_sems, i, (x, y, 1 - c)) for i in range(n)]
        for cp in copies:
            cp.start()
        for cp in copies:
            cp.wait()

    return _comm_call(body, name=name, n_in=n, out_shape=[_sds(g.shape[1:], g.dtype) for g in grads], n_sems=n)(*grads)


def _part(ref, mode, chip, rows, cols):
    if mode == "slab":
        return ref.at[chip]
    if mode == "cols":
        return ref.at[:, pl.ds(pl.multiple_of(chip * cols, LANES), cols)]
    return ref.at[pl.ds(pl.multiple_of(chip * rows, SUBLANES), rows), :]


def _rs_scatter(parts, modes, shard_shapes, *, name):
    n = len(parts)
    meta = [(mode,) + tuple(shp[1:]) for mode, shp in zip(modes, shard_shapes)]

    def body(*refs):
        ins, outs = refs[:n], refs[n:2 * n]
        send_sems, recv_sems, local_sems = refs[2 * n:]
        x, y, c, chips = _place()
        me = 2 * x + y
        local, sends = [], []
        for i, (mode, rows, cols) in enumerate(meta):
            cp = pltpu.make_async_copy(_part(ins[i], mode, me, rows, cols), outs[i].at[me], local_sems.at[i])
            cp.start()
            local.append(cp)
            for r, (cx, cy) in enumerate(chips):
                cp = _remote(_part(ins[i], mode, 2 * cx + cy, rows, cols), outs[i].at[me], send_sems, recv_sems,
                             3 * i + r, (cx, cy, c))
                cp.start()
                sends.append(cp)
        for i, (mode, rows, cols) in enumerate(meta):
            for r, (cx, cy) in enumerate(chips):
                k = 2 * cx + cy
                _remote(_part(ins[i], mode, k, rows, cols), outs[i].at[k], send_sems, recv_sems, 3 * i + r,
                        (cx, cy, c)).wait_recv()
        for cp in sends:
            cp.wait_send()
        for cp in local:
            cp.wait()

    out_shape = [_sds((N_CHIPS,) + tuple(shp[1:]), p.dtype) for p, shp in zip(parts, shard_shapes)]
    return _comm_call(body, name=name, n_in=n, out_shape=out_shape, n_sems=3 * n)(*parts)


def _rs_exchange(sums, *, name):
    n = len(sums)

    def body(*refs):
        ins, outs = refs[:n], refs[n:2 * n]
        send_sems, recv_sems, _ = refs[2 * n:]
        x, y, c, _ = _place()
        copies = [_remote(ins[i], outs[i], send_sems, recv_sems, i, (x, y, 1 - c)) for i in range(n)]
        for cp in copies:
            cp.start()
        for cp in copies:
            cp.wait()

    return _comm_call(body, name=name, n_in=n, out_shape=[_sds(s.shape, s.dtype) for s in sums], n_sems=n)(*sums)


def _row_tile(rows, cols, itemsize):
    target = max(SUBLANES, (2 << 20) // (cols * itemsize))
    cands = [c for c in (2048, 1024, 512, 256, 128, 64, 32, 16) if c <= target]
    tr = _pick(rows, cands)
    return tr


def _add_layer(g, other, core, *, name):
    _, rows, cols = g.shape
    tr = _row_tile(rows, cols, 4)

    def body(core_ref, g_ref, o_ref, out_ref):
        out_ref[...] = (g_ref[0] + o_ref[...]).astype(out_ref.dtype)

    grid_spec = pltpu.PrefetchScalarGridSpec(
        num_scalar_prefetch=1, grid=(rows // tr,),
        in_specs=[pl.BlockSpec((1, tr, cols), lambda i, core_ref: (core_ref[0], i, 0)),
                  pl.BlockSpec((tr, cols), lambda i, core_ref: (i, 0))],
        out_specs=pl.BlockSpec((tr, cols), lambda i, core_ref: (i, 0)))
    return pl.pallas_call(body, name=name, grid_spec=grid_spec, out_shape=_sds((rows, cols), BF16),
                          compiler_params=pltpu.CompilerParams(dimension_semantics=("parallel",),
                                                               vmem_limit_bytes=VMEM_LIMIT))(core, g, other)


def _sum_slots(parts, *, name):
    n, rows, cols = parts.shape
    tr = _row_tile(rows, cols, 4)

    def body(p_ref, o_ref):
        acc = p_ref[0].astype(F32) + p_ref[1].astype(F32)
        for k in range(2, n):
            acc = acc + p_ref[k].astype(F32)
        o_ref[...] = acc

    return _call(body, name=name, grid=(rows // tr,),
                 in_specs=[pl.BlockSpec((n, tr, cols), lambda i: (0, i, 0))],
                 out_specs=pl.BlockSpec((tr, cols), lambda i: (i, 0)),
                 out_shape=_sds((rows, cols), F32), sem=("parallel",))(parts)


def _reduce_scatter(grads, modes, shard_shapes):
    core = lax.axis_index("c").astype(jnp.int32).reshape(1)
    flat = [g.reshape(g.shape[0], -1, g.shape[-1]) for g in grads]
    from_sibling = _rs_swap(flat, name="rs_swap")
    parts = []
    for i, (g, o) in enumerate(zip(flat, from_sibling)):
        p = _add_layer(g, o, core, name=f"rs_add_{i}")
        parts.append(p.reshape(grads[i].shape[1:]))
    from_chips = _rs_scatter(parts, modes, shard_shapes, name="rs_scatter")
    sums = [_sum_slots(r, name=f"rs_sum_{i}") for i, r in enumerate(from_chips)]
    others = _rs_exchange(sums, name="rs_exchange")
    mine_first = lax.axis_index("c") == 0
    return [jnp.where(mine_first, jnp.stack([mine, other]), jnp.stack([other, mine]))
            for mine, other in zip(sums, others)]


def _layer_weights(full, rep, layer, dims):
    f_off, n_heads = dims["f_off"], dims["heads"]
    b_in = rep["b_in"][layer]
    pad = LANES - n_heads
    return {
        "w_main": (full["w_main"], layer),
        "b_main": jnp.concatenate([b_in[:f_off], b_in[f_off + n_heads:]])[None],
        "w_f": (full["w_f"], layer),
        "w_f_matrix": full["w_f"][layer],
        "b_f": jnp.pad(b_in[f_off:f_off + n_heads], (0, pad))[None],
        "conv_a_w": full["conv_a_w"][layer],
        "conv_a_b": rep["conv_a_b"][layer][None],
        "ln_conv_g": rep["ln_conv_g"][layer][None],
        "ln_conv_b": rep["ln_conv_b"][layer][None],
        "w_conv_proj": (full["w_conv_proj"], layer),
        "w_attn_proj": (full["w_attn_proj"], layer),
        "w_mix_out": (full["w_mix_out"], layer),
        "b_mix_out": rep["b_mix_out"][layer][None],
        "ln1_g": rep["ln1_g"][layer][None],
        "ln1_b": rep["ln1_b"][layer][None],
        "w_ffn_up": (full["w_ffn_up"], layer),
        "ffn_conv_w": full["ffn_conv_w"][layer],
        "ffn_conv_b": rep["ffn_conv_b"][layer][None],
        "w_ffn_down": (full["w_ffn_down"], layer),
        "ln2_g": rep["ln2_g"][layer][None],
        "ln2_b": rep["ln2_b"][layer][None],
    }


def _split_mod(mod, d):
    return [mod[:, k * d:(k + 1) * d][:, None, :] for k in range(6)]


def _layer_fwd(x, u, mods, p, dims, tag, after):
    bsz, ch, heads, alpha = dims["bsz"], dims["ch"], dims["heads"], dims["alpha"]
    _, _, gate1, shift2, scale2, gate2 = mods
    zm = _matmul(u, p["w_main"], "nn", BF16, bias=p["b_main"], name=f"in_main_{tag}")
    zf = _matmul(u, p["w_f"], "nn", F32, bias=p["b_f"], name=f"in_forget_{tag}")
    a3 = _conv_branch_fwd(zm, p["conv_a_w"], p["conv_a_b"], p["ln_conv_g"], p["ln_conv_b"], bsz, ch,
                          name=f"conv_branch_{tag}")
    ya = _matmul(a3, p["w_conv_proj"], "nn", BF16, name=f"conv_proj_{tag}")
    cumt, cumb = _fgate_fwd(zf, bsz, heads, name=f"fgate_{tag}")
    qkvt = _to_features_major(zm, 2 * ch, heads * HEAD_DIM, 3, name=f"qkv_t_{tag}")
    ot, lse = _attn_fwd(zm, qkvt, cumt, cumb, bsz, heads, 2 * ch, name=f"attn_{tag}")
    yb = _matmul(ot, p["w_attn_proj"], "tn", BF16, name=f"attn_proj_{tag}")
    m = _gate_merge_fwd(zm, ya, yb, dims["ga_off"], name=f"merge_{tag}")
    mix = _matmul(m, p["w_mix_out"], "nn", F32, bias=p["b_mix_out"], name=f"mix_out_{tag}")
    x1, u2 = _ln_res_mod_fwd(x, mix, gate1, p["ln1_g"], p["ln1_b"], scale2, shift2, alpha, bsz,
                             name=f"ln_res1_mod2_{tag}")
    hp = _matmul(u2, p["w_ffn_up"], "nn", BF16, name=f"ffn_up_{tag}")
    f = _ffn_act_fwd(hp, p["ffn_conv_w"], p["ffn_conv_b"], bsz, dims["tcf"], name=f"ffn_act_{tag}")
    ffn = _matmul(f, p["w_ffn_down"], "nn", F32, name=f"ffn_down_{tag}")
    if isinstance(after, tuple):
        x2 = _ln_res_mod_fwd(x1, ffn, gate2, p["ln2_g"], p["ln2_b"], after[0], after[1], alpha, bsz,
                             name=f"ln_res2_mod1_{tag}")
    else:
        x2 = _ln_res_loss(x1, ffn, gate2, p["ln2_g"], p["ln2_b"], after, alpha, bsz, name=f"ln_res2_loss_{tag}")
    saved = dict(x=x, mods=mods, u=u, zm=zm, zf=zf, a3=a3, ya=ya, yb=yb, cumt=cumt, cumb=cumb,
                 qkvt=qkvt, ot=ot, lse=lse, m=m, mix=mix, x1=x1, u2=u2, hp=hp, f=f, ffn=ffn)
    return x2, saved


def _layer_bwd(top, p, sv, dims, tag, below):
    bsz, ch, heads, alpha = dims["bsz"], dims["ch"], dims["heads"], dims["alpha"]
    f_off, tcf = dims["f_off"], dims["tcf"]
    shift1, scale1, gate1, shift2, scale2, gate2 = sv["mods"]
    g = {}
    dr2, dffn, dgate2, g["ln2_g"], g["ln2_b"] = top
    df = _matmul(dffn, p["w_ffn_down"], "nt", BF16, name=f"ffn_down_dx_{tag}")
    g["w_ffn_down"] = _matmul(sv["f"], dffn, "tn", F32, name=f"ffn_down_dw_{tag}")
    dhg, dhl, dwg, dwl, dbg, dbl = _ffn_act_bwd(sv["hp"], df, p["ffn_conv_w"], p["ffn_conv_b"], bsz, tcf,
                                                name=f"ffn_act_bwd_{tag}")
    g["ffn_conv_w"] = jnp.concatenate([dwg, dwl], axis=1)
    g["ffn_conv_b"] = jnp.concatenate([dbg, dbl], axis=1)[0]
    du2 = _matmul(dhg, p["w_ffn_up"], "nt", F32, name=f"ffn_up_gate_dx_{tag}")
    du2 = _matmul(dhl, p["w_ffn_up"], "nt", F32, add=du2, b_k_first=dhg.shape[1], name=f"ffn_up_lin_dx_{tag}")
    d_ff = dhg.shape[1]
    dw_up = _matmul(sv["u2"], dhg, "tn", F32, out_cols=2 * d_ff, name=f"ffn_up_gate_dw_{tag}")
    g["w_ffn_up"] = _matmul(sv["u2"], dhl, "tn", F32, into=(dw_up, d_ff), name=f"ffn_up_lin_dw_{tag}")
    dr1, dmix, dgate1, g["ln1_g"], g["ln1_b"], g["b_mix_out"], dscale2, dshift2 = _ln_mod_res_bwd(
        du2, dr2, scale2, sv["x"], sv["mix"], gate1, p["ln1_g"], p["ln1_b"], alpha, bsz,
        name=f"ln_mod2_res1_bwd_{tag}")
    dm = _matmul(dmix, p["w_mix_out"], "nt", BF16, name=f"mix_out_dx_{tag}")
    g["w_mix_out"] = _matmul(sv["m"], dmix, "tn", F32, name=f"mix_out_dw_{tag}")
    dya, dyb, dzga, dzgb = _gate_merge_bwd(sv["zm"], sv["ya"], sv["yb"], dm, dims["ga_off"], name=f"merge_bwd_{tag}")
    da3 = _matmul(dya, p["w_conv_proj"], "nt", F32, name=f"conv_proj_dx_{tag}")
    g["w_conv_proj"] = _matmul(sv["a3"], dya, "tn", F32, name=f"conv_proj_dw_{tag}")
    do = _matmul(dyb, p["w_attn_proj"], "nt", BF16, name=f"attn_proj_dx_{tag}")
    dot = _matmul(p["w_attn_proj"], dyb, "nt", BF16, name=f"attn_proj_dxt_{tag}")
    g["w_attn_proj"] = _matmul(sv["ot"], dyb, "nn", F32, name=f"attn_proj_dw_{tag}")
    dzglu, g["conv_a_w"], dcb, g["ln_conv_g"], g["ln_conv_b"] = _conv_branch_bwd(
        sv["zm"], da3, p["conv_a_w"], p["conv_a_b"], p["ln_conv_g"], p["ln_conv_b"], bsz, ch,
        name=f"conv_branch_bwd_{tag}")
    g["conv_a_b"] = dcb[0]
    dk, dv, dqt, dcum = _attn_bwd(sv["zm"], sv["qkvt"], sv["cumt"], sv["cumb"], sv["ot"], do, dot, sv["lse"], bsz,
                                  heads, 2 * ch, name=f"attn_bwd_{tag}")
    dq = _to_rows_major(dqt, name=f"dq_rows_{tag}")
    dzf = _fgate_bwd(dcum, sv["zf"], bsz, name=f"fgate_bwd_{tag}")
    dzm = jnp.concatenate([dzglu, dq, dk, dv, dzga, dzgb], axis=1)
    du = _matmul(dzm, p["w_main"], "nt", F32, name=f"in_main_dx_{tag}")
    dwm, dbm = _matmul(sv["u"], dzm, "tn", F32, colsum=True, name=f"in_main_dw_{tag}")
    dwf, dbf = _matmul(sv["u"], dzf, "tn", F32, colsum=True, name=f"in_forget_dw_{tag}")
    dbm, dbf = dbm[0], dbf[0]
    g["w_main"], g["w_f"] = dwm, dwf
    g["b_in"] = jnp.concatenate([dbm[:f_off], dbf[:heads], dbm[f_off:]])
    if below is None:
        out, dscale1, dshift1 = _ln_mod_bwd(du, dzf, p["w_f_matrix"], sv["x"], scale1, dr1, alpha, bsz,
                                            name=f"ln_mod1_bwd_{tag}")
    else:
        sv_b, p_b = below
        *out, _, dscale1, dshift1 = _ln_mod_res_bwd(
            du, dr1, scale1, sv_b["x1"], sv_b["ffn"], sv_b["mods"][5], p_b["ln2_g"], p_b["ln2_b"], alpha, bsz,
            narrow=(dzf, p["w_f_matrix"]), name=f"ln_mod1_res2_bwd_{tag}")
    dmod = jnp.concatenate([dshift1, dscale1, dgate1, dshift2, dscale2, dgate2], axis=2)[:, 0, :]
    return out, g, dmod


def _local_step(x, mod, loss_target, full, rep, dims):
    bsz, seq, d = x.shape
    layers = mod.shape[0]
    params = [_layer_weights(full, rep, layer, dims) for layer in range(layers)]
    mods = [_split_mod(mod[layer], d) for layer in range(layers)]
    h = x.reshape(bsz * seq, d)
    u = _ln_mod_fwd(h, mods[0][1], mods[0][0], bsz, name="ln_mod1_l0")
    saved = []
    for layer in range(layers):
        last = layer == layers - 1
        after = loss_target.reshape(bsz * seq, d) if last else (mods[layer + 1][1], mods[layer + 1][0])
        (h, u), sv = _layer_fwd(h, u, mods[layer], params[layer], dims, f"l{layer}", after)
        saved.append(sv)
    dh, sq = h, u
    loss_local = 0.5 * jnp.sum(sq) / d
    top_sv, top_p = saved[-1], params[-1]
    dh = _ln_res_bwd(dh, top_sv["x1"], top_sv["ffn"], top_sv["mods"][5], top_p["ln2_g"], dims["alpha"], bsz,
                     name=f"ln_res2_bwd_l{layers - 1}")[:5]
    grads, dmods = [None] * layers, [None] * layers
    for layer in reversed(range(layers)):
        below = (saved[layer - 1], params[layer - 1]) if layer > 0 else None
        dh, grads[layer], dmods[layer] = _layer_bwd(dh, params[layer], saved[layer], dims, f"l{layer}", below)
    per_layer = ("w_main", "w_f")
    stacked = {wname: [grads[layer][wname] for layer in range(layers)] if wname in per_layer
               else jnp.stack([grads[layer][wname] for layer in range(layers)]) for wname in grads[0]}
    return loss_local, dh.reshape(bsz, seq, d), stacked, jnp.stack(dmods)


def _pad_rows(a):
    extra = -a.shape[-2] % (2 * SUBLANES)
    if extra == 0:
        return a
    return jnp.pad(a, [(0, 0)] * (a.ndim - 2) + [(0, extra), (0, 0)])


def _w_in_pieces(n, f_off, heads):
    n_in = N_CHIPS * n
    segments = [(0, f_off, "main", 0), (f_off, f_off + heads, "f", 0), (f_off + heads, n_in, "main", f_off)]
    pieces = []
    for chip in range(N_CHIPS):
        lo, hi = chip * n, (chip + 1) * n
        for a, b, target, t0 in segments:
            s, e = max(lo, a), min(hi, b)
            if s < e:
                pieces.append((chip, s - lo, e - lo, target, t0 + s - a))
    return pieces


def _w_in_from_slabs(slabs, f_off, heads, *, name):
    layers, _, k, n = slabs.shape
    tr = _pick(k, (256, 128, 64, 32, 16))
    n_main = N_CHIPS * n - heads
    pieces = _w_in_pieces(n, f_off, heads)

    def body(s_ref, m_ref, f_ref):
        f_ref[...] = jnp.zeros_like(f_ref)
        for chip in range(N_CHIPS):
            slab = s_ref[0, chip].astype(F32)
            for pc, s0, s1, target, t0 in pieces:
                if pc == chip:
                    out = m_ref if target == "main" else f_ref
                    out[0, :, t0:t0 + s1 - s0] = slab[:, s0:s1].astype(out.dtype)

    return _call(body, name=name, grid=(layers, k // tr),
                 in_specs=[pl.BlockSpec((1, N_CHIPS, tr, n), lambda layer, i: (layer, 0, i, 0))],
                 out_specs=[pl.BlockSpec((1, tr, n_main), lambda layer, i: (layer, i, 0)),
                            pl.BlockSpec((1, tr, LANES), lambda layer, i: (layer, i, 0))],
                 out_shape=[_sds((layers, k, n_main), slabs.dtype), _sds((layers, k, LANES), slabs.dtype)],
                 sem=("parallel", "parallel"))(slabs)


def _w_in_to_slabs(d_main, d_f, n, f_off, heads, *, name):
    layers = len(d_main)
    k = d_main[0].shape[0]
    tr = _pick(k, (128, 64, 32, 16, 8))
    pieces = _w_in_pieces(n, f_off, heads)

    def body(*refs):
        m_refs, f_refs, o_ref = refs[:layers], refs[layers:2 * layers], refs[2 * layers]
        for layer in range(layers):
            for chip, s0, s1, target, t0 in pieces:
                src = m_refs[layer] if target == "main" else f_refs[layer]
                o_ref[layer, chip, :, s0:s1] = src[:, t0:t0 + s1 - s0]

    return _call(body, name=name, grid=(k // tr,),
                 in_specs=[pl.BlockSpec((tr, d_main[0].shape[1]), lambda i: (i, 0))] * layers
                 + [pl.BlockSpec((tr, LANES), lambda i: (i, 0))] * layers,
                 out_specs=pl.BlockSpec((layers, N_CHIPS, tr, n), lambda i: (0, 0, i, 0)),
                 out_shape=_sds((layers, N_CHIPS, k, n), F32), sem=("parallel",))(*d_main, *d_f)


def kernel(x, c, w_ada, b_ada, w_in, b_in, conv_a_w, conv_a_b, ln_conv_g, ln_conv_b, w_conv_proj, w_attn_proj, w_mix_out, b_mix_out, ln1_g, ln1_b, w_ffn_up, ffn_conv_w, ffn_conv_b, w_ffn_down, ln2_g, ln2_b, loss_target, m_w_ada, m_b_ada, m_w_in, m_b_in, m_conv_a_w, m_conv_a_b, m_ln_conv_g, m_ln_conv_b, m_w_conv_proj, m_w_attn_proj, m_w_mix_out, m_b_mix_out, m_ln1_g, m_ln1_b, m_w_ffn_up, m_ffn_conv_w, m_ffn_conv_b, m_w_ffn_down, m_ln2_g, m_ln2_b, v_w_ada, v_b_ada, v_w_in, v_b_in, v_conv_a_w, v_conv_a_b, v_ln_conv_g, v_ln_conv_b, v_w_conv_proj, v_w_attn_proj, v_w_mix_out, v_b_mix_out, v_ln1_g, v_ln1_b, v_w_ffn_up, v_ffn_conv_w, v_ffn_conv_b, v_w_ffn_down, v_ln2_g, v_ln2_b):
    weights = dict(zip(WEIGHTS, (w_ada, b_ada, w_in, b_in, conv_a_w, conv_a_b, ln_conv_g, ln_conv_b, w_conv_proj,
                                 w_attn_proj, w_mix_out, b_mix_out, ln1_g, ln1_b, w_ffn_up, ffn_conv_w, ffn_conv_b,
                                 w_ffn_down, ln2_g, ln2_b)))
    mom1 = dict(zip(WEIGHTS, (m_w_ada, m_b_ada, m_w_in, m_b_in, m_conv_a_w, m_conv_a_b, m_ln_conv_g, m_ln_conv_b,
                              m_w_conv_proj, m_w_attn_proj, m_w_mix_out, m_b_mix_out, m_ln1_g, m_ln1_b, m_w_ffn_up,
                              m_ffn_conv_w, m_ffn_conv_b, m_w_ffn_down, m_ln2_g, m_ln2_b)))
    mom2 = dict(zip(WEIGHTS, (v_w_ada, v_b_ada, v_w_in, v_b_in, v_conv_a_w, v_conv_a_b, v_ln_conv_g, v_ln_conv_b,
                              v_w_conv_proj, v_w_attn_proj, v_w_mix_out, v_b_mix_out, v_ln1_g, v_ln1_b, v_w_ffn_up,
                              v_ffn_conv_w, v_ffn_conv_b, v_w_ffn_down, v_ln2_g, v_ln2_b)))
    bsz, seq, d = x.shape
    layers = w_ada.shape[0]
    ch = conv_a_w.shape[2] * N_CHIPS
    width = w_attn_proj.shape[1]
    heads = width // HEAD_DIM
    d_ff = w_ffn_down.shape[1] * N_CHIPS
    dims = dict(bsz=bsz, d=d, ch=ch, heads=heads, alpha=(2.0 * layers) ** 0.25, f_off=2 * ch + 3 * width,
                ga_off=2 * ch + 3 * width, tcf=_pick(d_ff, (256, 128)))
    chip = 2 * lax.axis_index("x") + lax.axis_index("y")
    device = 2 * chip + lax.axis_index("c")
    ada_cols = w_ada.shape[2]

    c_act = _silu_rows(_all_gather8(c, name="gather_c").reshape(N_DEVICES * bsz, d), name="silu_c")
    b_ada_mine = lax.dynamic_slice_in_dim(b_ada, chip * ada_cols, ada_cols, axis=1)
    mod_cols = jnp.stack([_matmul(c_act, (w_ada, layer), "nn", F32, bias=b_ada_mine[layer][None], name=f"ada_l{layer}")
                          for layer in range(layers)])
    mod_all = _all_gather8(mod_cols, name="gather_mod")
    mod_all = jnp.concatenate([mod_all[2 * k] for k in range(N_CHIPS)], axis=-1)
    mod = lax.dynamic_slice_in_dim(mod_all, device * bsz, bsz, axis=1)

    shards = [_pad_rows(weights[wname].astype(BF16) if as_bf16 else weights[wname]) for wname, _, as_bf16 in GATHERED]
    modes = [mode for _, mode, _ in GATHERED]
    whole = _gather_weights(shards, modes, name="gather_weights")
    full = {wname: w[:, :weights[wname].shape[1]] if mode == "cols" else w
            for (wname, mode, _), w in zip(GATHERED, whole)}
    full["w_main"], full["w_f"] = _w_in_from_slabs(full.pop("w_in"), dims["f_off"], heads, name="w_in_from_slabs")
    rep = {wname: weights[wname] for wname in REPLICATED}

    loss_local, grad_x, grads, dmod = _local_step(x, mod, loss_target, full, rep, dims)
    loss = lax.psum(loss_local, ("x", "y", "c"))

    grads["w_in"] = _w_in_to_slabs(grads.pop("w_main"), grads.pop("w_f"), w_in.shape[2], dims["f_off"], heads,
                                   name="w_in_to_slabs")
    shard_shapes = [s.shape for s in shards]
    reduced = _reduce_scatter([_pad_rows(grads[wname]) for wname, _, _ in GATHERED], modes, shard_shapes)
    grad = {wname: r[:, :weights[wname].shape[1]] for (wname, _, _), r in zip(GATHERED, reduced)}

    small = jnp.concatenate([dmod.reshape(-1)] + [grads[wname].reshape(-1) for wname in REPLICATED])
    n_small = small.shape[0]
    rows = -(-n_small // (SUBLANES * LANES)) * SUBLANES
    small = jnp.pad(small, (0, rows * LANES - n_small)).reshape(rows, LANES)
    gathered = _all_gather8(small, name="gather_small")
    n_dmod = dmod.size
    dmod_all = gathered.reshape(N_DEVICES, -1)[:, :n_dmod].reshape(N_DEVICES, layers, bsz, 6 * d)
    dmod_all = jnp.transpose(dmod_all, (1, 0, 2, 3)).reshape(layers, N_DEVICES * bsz, 6 * d)
    summed = _sum_slots(gathered, name="sum_small").reshape(-1)
    off = n_dmod
    for wname in REPLICATED:
        n = weights[wname].size
        grad[wname] = summed[off:off + n].reshape(weights[wname].shape)
        off += n
    dmod_mine = lax.dynamic_slice_in_dim(dmod_all, chip * ada_cols, ada_cols, axis=2)
    grad["w_ada"] = jnp.stack([_matmul(c_act, dmod_mine[layer], "tn", F32, name=f"ada_dw_l{layer}")
                               for layer in range(layers)])
    grad["b_ada"] = jnp.stack([_colsum(dmod_all[layer], name=f"ada_db_l{layer}")[0] for layer in range(layers)])

    delta, new_m, new_v = {}, {}, {}
    for wname in WEIGHTS:
        delta[wname], new_m[wname], new_v[wname] = _adamw(weights[wname], grad[wname], mom1[wname], mom2[wname],
                                                          name=f"adamw_{wname}")
    return (loss, grad_x, *[grad[wname] for wname in WEIGHTS], *[delta[wname] for wname in WEIGHTS],
            *[new_m[wname] for wname in WEIGHTS], *[new_v[wname] for wname in WEIGHTS])
```

```python
import math

import jax
import jax.numpy as jnp
from jax import lax
from jax.experimental import pallas as pl
from jax.experimental.pallas import tpu as pltpu

F32 = jnp.float32
BF16 = jnp.bfloat16
MESH = pl.DeviceIdType.MESH

LN_EPS = 1e-5
HEAD_DIM = 64
ATTN_SCALE = HEAD_DIM ** -0.5
NEG = -1e30
FFN_PAD = 8
LANES = 128
SUBLANES = 8
ROW_CHUNK = 256
ATTN_BLOCK = 256
ATTN_UNROLL = 4
N_CHIPS = 4
N_DEVICES = 8
VMEM_LIMIT = 56 * 1024 * 1024

ADAM_LR = 0.001
ADAM_B1 = 0.9
ADAM_B2 = 0.999
ADAM_EPS = 1e-08
ADAM_WD = 0.01
ADAM_STEP = 10

GATHERED = (("w_in", "slab", True), ("conv_a_w", "cols", False), ("w_conv_proj", "cols", True),
            ("w_attn_proj", "cols", True), ("w_mix_out", "rows", True), ("w_ffn_up", "cols", True),
            ("ffn_conv_w", "cols", False), ("w_ffn_down", "rows", True))
REPLICATED = ("b_in", "conv_a_b", "ln_conv_g", "ln_conv_b", "b_mix_out", "ln1_g", "ln1_b",
              "ffn_conv_b", "ln2_g", "ln2_b")
WEIGHTS = ("w_ada", "b_ada", "w_in", "b_in", "conv_a_w", "conv_a_b", "ln_conv_g", "ln_conv_b",
           "w_conv_proj", "w_attn_proj", "w_mix_out", "b_mix_out", "ln1_g", "ln1_b", "w_ffn_up",
           "ffn_conv_w", "ffn_conv_b", "w_ffn_down", "ln2_g", "ln2_b")


def _pick(n, cands):
    for cand in cands:
        if n % cand == 0:
            return cand
    return n


def _call(body, *, name, grid, in_specs, out_specs, out_shape, scratch=(), sem=None):
    return pl.pallas_call(
        body, name=name, grid=grid, in_specs=in_specs, out_specs=out_specs, out_shape=out_shape,
        scratch_shapes=list(scratch),
        compiler_params=pltpu.CompilerParams(dimension_semantics=sem, vmem_limit_bytes=VMEM_LIMIT))


def _sds(shape, dtype):
    return jax.ShapeDtypeStruct(tuple(shape), dtype)


def _chunked(rows, fn):
    chunk = min(ROW_CHUNK, rows)
    if rows == chunk:
        fn(pl.ds(0, rows))
        return

    def step(i, carry):
        fn(pl.ds(pl.multiple_of(i * chunk, chunk), chunk))
        return carry

    lax.fori_loop(0, rows // chunk, step, 0)


def _matmul(a, b, mode, out_dtype, *, bias=None, add=None, colsum=False, b_k_first=0, out_cols=None, into=None,
            name):
    a, a_layer = a if isinstance(a, tuple) else (a, None)
    b, b_layer = b if isinstance(b, tuple) else (b, None)
    if mode == "nn":
        (m, k), (_, n) = a.shape[-2:], b.shape[-2:]
    elif mode == "nt":
        (m, k), (n, _) = a.shape[-2:], b.shape[-2:]
    else:
        (k, m), (_, n) = a.shape[-2:], b.shape[-2:]
    tn = _pick(n, (1536, 1408, 1024, 512, 256, 128))
    tk = k if k <= 1536 else _pick(k, (1024, 1536, 1408, 512, 256, 128))
    nk = k // tk
    tall = (2048,) if nk == 1 and mode != "tn" and add is None else ()
    tm = _pick(m, tall + (1024, 1408, 512, 256, 128))
    assert b_k_first % tk == 0 and (b_k_first == 0 or mode == "nt")
    k_blk0 = b_k_first // tk

    def spec(layer, shape, index):
        if layer is None:
            return pl.BlockSpec(shape, index)
        return pl.BlockSpec((None,) + shape, lambda i, j, kk: (layer,) + index(i, j, kk))

    if mode == "nn":
        a_spec = spec(a_layer, (tm, tk), lambda i, j, kk: (i, kk))
        b_spec = spec(b_layer, (tk, tn), lambda i, j, kk: (kk, j))
        dims = (((1,), (0,)), ((), ()))
    elif mode == "nt":
        a_spec = spec(a_layer, (tm, tk), lambda i, j, kk: (i, kk))
        b_spec = spec(b_layer, (tn, tk), lambda i, j, kk: (j, k_blk0 + kk))
        dims = (((1,), (1,)), ((), ()))
    else:
        a_spec = spec(a_layer, (tk, tm), lambda i, j, kk: (kk, i))
        b_spec = spec(b_layer, (tk, tn), lambda i, j, kk: (kk, j))
        dims = (((0,), (0,)), ((), ()))
    in_specs = [a_spec, b_spec]
    operands = [a, b]
    if bias is not None:
        in_specs.append(pl.BlockSpec((1, tn), lambda i, j, kk: (0, j)))
        operands.append(bias)
    if add is not None:
        in_specs.append(pl.BlockSpec((tm, tn), lambda i, j, kk: (i, j)))
        operands.append(add)

    def body(a_ref, b_ref, *rest):
        rest = list(rest)
        bias_ref = rest.pop(0) if bias is not None else None
        add_ref = rest.pop(0) if add is not None else None
        if into is not None:
            rest.pop(0)
        o_ref = rest.pop(0)
        prod = lax.dot_general(a_ref[...].astype(BF16), b_ref[...].astype(BF16), dims,
                               preferred_element_type=F32)
        if colsum:
            cs_ref = rest.pop(0)
            part = jnp.sum(b_ref[...].astype(F32), axis=0, keepdims=True)

            @pl.when(pl.program_id(2) == 0)
            def _():
                cs_ref[...] = part

            @pl.when(pl.program_id(2) > 0)
            def _():
                cs_ref[...] += part

        def finish(r):
            if bias_ref is not None:
                r = r + bias_ref[...]
            if add_ref is not None:
                r = r + add_ref[...]
            o_ref[...] = r.astype(o_ref.dtype)

        if nk == 1:
            finish(prod)
            return
        acc_ref = rest.pop(0)
        kk = pl.program_id(2)

        @pl.when(kk == 0)
        def _():
            acc_ref[...] = prod

        @pl.when(kk > 0)
        def _():
            acc_ref[...] += prod

        @pl.when(kk == nk - 1)
        def _():
            finish(acc_ref[...])

    col_blk0 = 0
    aliases = {}
    out_shape = _sds((m, n if out_cols is None else out_cols), out_dtype)
    if into is not None:
        wide, col_first = into
        assert col_first % tn == 0 and not colsum
        col_blk0 = col_first // tn
        out_shape = _sds(wide.shape, wide.dtype)
        aliases = {len(operands): 0}
        in_specs.append(pl.BlockSpec(memory_space=pl.ANY))
        operands.append(wide)
    out_specs = pl.BlockSpec((tm, tn), lambda i, j, kk: (i, col_blk0 + j))
    if colsum:
        assert mode == "tn" and m == tm
        out_specs = [out_specs, pl.BlockSpec((1, tn), lambda i, j, kk: (0, j))]
        out_shape = [out_shape, _sds((1, n), F32)]
    return pl.pallas_call(
        body, name=name, grid=(m // tm, n // tn, nk), in_specs=in_specs, out_specs=out_specs, out_shape=out_shape,
        scratch_shapes=[pltpu.VMEM((tm, tn), F32)] if nk > 1 else [], input_output_aliases=aliases,
        compiler_params=pltpu.CompilerParams(dimension_semantics=("parallel", "parallel", "arbitrary"),
                                             vmem_limit_bytes=VMEM_LIMIT))(*operands)


def _colsum(x, *, name):
    rows, n = x.shape
    tr = _pick(rows, (1024, 512, 256, 128))
    tn = _pick(n, (512, 256, 128))

    def body(x_ref, o_ref):
        @pl.when(pl.program_id(1) == 0)
        def _():
            o_ref[...] = jnp.zeros_like(o_ref)

        o_ref[...] += jnp.sum(x_ref[...].astype(F32), axis=0, keepdims=True)

    return _call(body, name=name, grid=(n // tn, rows // tr),
                 in_specs=[pl.BlockSpec((tr, tn), lambda j, i: (i, j))],
                 out_specs=pl.BlockSpec((1, tn), lambda j, i: (0, j)),
                 out_shape=_sds((1, n), F32), sem=("parallel", "arbitrary"))(x)


def _ln_stats(x):
    mu = jnp.mean(x, axis=-1, keepdims=True)
    xc = x - mu
    var = jnp.mean(xc * xc, axis=-1, keepdims=True)
    rstd = lax.rsqrt(var + LN_EPS)
    return xc * rstd, rstd


def _ln_bwd(dn, n, rstd):
    return rstd * (dn - jnp.mean(dn, axis=-1, keepdims=True) - n * jnp.mean(dn * n, axis=-1, keepdims=True))


def _seq_tiles(t, bsz, cands=(1024, 512, 256, 128, 64, 32, 16, 8)):
    s = t // bsz
    ts = _pick(s, cands)
    return s, ts, s // ts


def _ln_mod_fwd(x, scale, shift, bsz, *, name):
    t, d = x.shape
    _, ts, ns = _seq_tiles(t, bsz)

    def body(x_ref, sc_ref, sh_ref, u_ref):
        one_scale = 1.0 + sc_ref[0]
        shift_v = sh_ref[0]

        def piece(rows):
            n, _ = _ln_stats(x_ref[rows, :])
            u_ref[rows, :] = (n * one_scale + shift_v).astype(u_ref.dtype)

        _chunked(ts, piece)

    row = pl.BlockSpec((ts, d), lambda b, i: (b * ns + i, 0))
    per = pl.BlockSpec((1, 1, d), lambda b, i: (b, 0, 0))
    return _call(body, name=name, grid=(bsz, ns), in_specs=[row, per, per], out_specs=row,
                 out_shape=_sds((t, d), BF16), sem=("parallel", "parallel"))(x, scale, shift)


def _ln_mod_bwd(du, dz_narrow, w_narrow, x, scale, dr, alpha, bsz, *, name):
    t, d = x.shape
    _, ts, ns = _seq_tiles(t, bsz)
    lanes = dz_narrow.shape[1]

    def body(du_ref, dzn_ref, wn_ref, x_ref, sc_ref, dr_ref, dx_ref, dsc_ref, dsh_ref):
        @pl.when(pl.program_id(1) == 0)
        def _():
            dsc_ref[...] = jnp.zeros_like(dsc_ref)
            dsh_ref[...] = jnp.zeros_like(dsh_ref)

        one_scale = 1.0 + sc_ref[0]
        w_n = wn_ref[...].astype(BF16)

        def piece(rows):
            du_v = du_ref[rows, :] + lax.dot_general(dzn_ref[rows, :].astype(BF16), w_n, (((1,), (1,)), ((), ())),
                                                     preferred_element_type=F32)
            n, rstd = _ln_stats(x_ref[rows, :])
            dsc_ref[0] += jnp.sum(du_v * n, axis=0, keepdims=True)
            dsh_ref[0] += jnp.sum(du_v, axis=0, keepdims=True)
            dx_ref[rows, :] = alpha * dr_ref[rows, :] + _ln_bwd(du_v * one_scale, n, rstd)

        _chunked(ts, piece)

    row = pl.BlockSpec((ts, d), lambda b, i: (b * ns + i, 0))
    per = pl.BlockSpec((1, 1, d), lambda b, i: (b, 0, 0))
    return _call(body, name=name, grid=(bsz, ns),
                 in_specs=[row, pl.BlockSpec((ts, lanes), lambda b, i: (b * ns + i, 0)),
                           pl.BlockSpec((d, lanes), lambda b, i: (0, 0)), row, per, row],
                 out_specs=[row, per, per],
                 out_shape=[_sds((t, d), F32), _sds((bsz, 1, d), F32), _sds((bsz, 1, d), F32)],
                 sem=("parallel", "arbitrary"))(du, dz_narrow, w_narrow, x, scale, dr)


def _ln_res_bwd(do, x, y, gate, g, alpha, bsz, *, name):
    t, d = x.shape
    _, ts, ns = _seq_tiles(t, bsz)

    def body(do_ref, x_ref, y_ref, gt_ref, g_ref, dr_ref, dy_ref, dgt_ref, dg_ref, db_ref, dys_ref):
        first_tile = pl.program_id(1) == 0

        @pl.when(first_tile)
        def _():
            dgt_ref[...] = jnp.zeros_like(dgt_ref)

        @pl.when(jnp.logical_and(first_tile, pl.program_id(0) == 0))
        def _():
            dg_ref[...] = jnp.zeros_like(dg_ref)
            db_ref[...] = jnp.zeros_like(db_ref)
            dys_ref[...] = jnp.zeros_like(dys_ref)

        one_gate = 1.0 + gt_ref[0]

        def piece(rows):
            do_v = do_ref[rows, :]
            y_v = y_ref[rows, :]
            n, rstd = _ln_stats(alpha * x_ref[rows, :] + one_gate * y_v)
            dg_ref[...] += jnp.sum(do_v * n, axis=0, keepdims=True)
            db_ref[...] += jnp.sum(do_v, axis=0, keepdims=True)
            dr = _ln_bwd(do_v * g_ref[...], n, rstd)
            dr_ref[rows, :] = dr
            dy = one_gate * dr
            dy_ref[rows, :] = dy.astype(dy_ref.dtype)
            dys_ref[...] += jnp.sum(dy, axis=0, keepdims=True)
            dgt_ref[0] += jnp.sum(dr * y_v, axis=0, keepdims=True)

        _chunked(ts, piece)

    row = pl.BlockSpec((ts, d), lambda bb, i: (bb * ns + i, 0))
    per = pl.BlockSpec((1, 1, d), lambda bb, i: (bb, 0, 0))
    vec = pl.BlockSpec((1, d), lambda bb, i: (0, 0))
    return _call(body, name=name, grid=(bsz, ns), in_specs=[row, row, row, per, vec],
                 out_specs=[row, row, per, vec, vec, vec],
                 out_shape=[_sds((t, d), F32), _sds((t, d), BF16), _sds((bsz, 1, d), F32),
                            _sds((1, d), F32), _sds((1, d), F32), _sds((1, d), F32)],
                 sem=("arbitrary", "arbitrary"))(do, x, y, gate, g)


def _ln_res_mod_fwd(x, y, gate, g, b, scale, shift, alpha, bsz, *, name):
    t, d = x.shape
    _, ts, ns = _seq_tiles(t, bsz)

    def body(x_ref, y_ref, gt_ref, g_ref, b_ref, sc_ref, sh_ref, o_ref, u_ref):
        one_gate = 1.0 + gt_ref[0]
        one_scale = 1.0 + sc_ref[0]
        shift_v = sh_ref[0]

        def piece(rows):
            n, _ = _ln_stats(alpha * x_ref[rows, :] + one_gate * y_ref[rows, :])
            x1 = n * g_ref[...] + b_ref[...]
            o_ref[rows, :] = x1
            n1, _ = _ln_stats(x1)
            u_ref[rows, :] = (n1 * one_scale + shift_v).astype(u_ref.dtype)

        _chunked(ts, piece)

    row = pl.BlockSpec((ts, d), lambda bb, i: (bb * ns + i, 0))
    per = pl.BlockSpec((1, 1, d), lambda bb, i: (bb, 0, 0))
    vec = pl.BlockSpec((1, d), lambda bb, i: (0, 0))
    return _call(body, name=name, grid=(bsz, ns), in_specs=[row, row, per, vec, vec, per, per],
                 out_specs=[row, row], out_shape=[_sds((t, d), F32), _sds((t, d), BF16)],
                 sem=("parallel", "parallel"))(x, y, gate, g, b, scale, shift)


def _ln_mod_res_bwd(du, dr_up, scale, x, y, gate, g, b, alpha, bsz, *, narrow=None, name):
    t, d = x.shape
    _, ts, ns = _seq_tiles(t, bsz, (512, 256, 128, 64, 32, 16, 8))

    def body(du_ref, *refs):
        refs = list(refs)
        dzn_ref, wn_ref = (refs.pop(0), refs.pop(0)) if narrow is not None else (None, None)
        (up_ref, sc_ref, x_ref, y_ref, gt_ref, g_ref, b_ref,
         dr_ref, dy_ref, dgt_ref, dg_ref, db_ref, dys_ref, dsc_ref, dsh_ref) = refs
        first_tile = pl.program_id(1) == 0

        @pl.when(first_tile)
        def _():
            for ref in (dgt_ref, dsc_ref, dsh_ref):
                ref[...] = jnp.zeros_like(ref)

        @pl.when(jnp.logical_and(first_tile, pl.program_id(0) == 0))
        def _():
            for ref in (dg_ref, db_ref, dys_ref):
                ref[...] = jnp.zeros_like(ref)

        one_gate = 1.0 + gt_ref[0]
        one_scale = 1.0 + sc_ref[0]

        def piece(rows):
            y_v = y_ref[rows, :]
            n, rstd = _ln_stats(alpha * x_ref[rows, :] + one_gate * y_v)
            n1, rstd1 = _ln_stats(n * g_ref[...] + b_ref[...])
            du_v = du_ref[rows, :]
            if narrow is not None:
                du_v = du_v + lax.dot_general(dzn_ref[rows, :].astype(BF16), wn_ref[...].astype(BF16),
                                              (((1,), (1,)), ((), ())), preferred_element_type=F32)
            dsc_ref[0] += jnp.sum(du_v * n1, axis=0, keepdims=True)
            dsh_ref[0] += jnp.sum(du_v, axis=0, keepdims=True)
            dx1 = alpha * up_ref[rows, :] + _ln_bwd(du_v * one_scale, n1, rstd1)
            dg_ref[...] += jnp.sum(dx1 * n, axis=0, keepdims=True)
            db_ref[...] += jnp.sum(dx1, axis=0, keepdims=True)
            dr = _ln_bwd(dx1 * g_ref[...], n, rstd)
            dr_ref[rows, :] = dr
            dy = one_gate * dr
            dy_ref[rows, :] = dy.astype(dy_ref.dtype)
            dys_ref[...] += jnp.sum(dy, axis=0, keepdims=True)
            dgt_ref[0] += jnp.sum(dr * y_v, axis=0, keepdims=True)

        _chunked(ts, piece)

    row = pl.BlockSpec((ts, d), lambda bb, i: (bb * ns + i, 0))
    per = pl.BlockSpec((1, 1, d), lambda bb, i: (bb, 0, 0))
    vec = pl.BlockSpec((1, d), lambda bb, i: (0, 0))
    extra_specs, extra = [], []
    if narrow is not None:
        lanes = narrow[0].shape[1]
        extra_specs = [pl.BlockSpec((ts, lanes), lambda bb, i: (bb * ns + i, 0)), pl.BlockSpec((d, lanes), lambda bb, i: (0, 0))]
        extra = list(narrow)
    return _call(body, name=name, grid=(bsz, ns), in_specs=[row] + extra_specs + [row, per, row, row, per, vec, vec],
                 out_specs=[row, row, per, vec, vec, vec, per, per],
                 out_shape=[_sds((t, d), F32), _sds((t, d), BF16), _sds((bsz, 1, d), F32), _sds((1, d), F32),
                            _sds((1, d), F32), _sds((1, d), F32), _sds((bsz, 1, d), F32), _sds((bsz, 1, d), F32)],
                 sem=("arbitrary", "arbitrary"))(du, *extra, dr_up, scale, x, y, gate, g, b)


def _ln_res_loss(x, y, gate, g, b, target, alpha, bsz, *, name):
    t, d = x.shape
    _, ts, ns = _seq_tiles(t, bsz)

    def body(x_ref, y_ref, gt_ref, g_ref, b_ref, t_ref, dy_ref, s_ref):
        @pl.when(jnp.logical_and(pl.program_id(0) == 0, pl.program_id(1) == 0))
        def _():
            s_ref[...] = jnp.zeros_like(s_ref)

        one_gate = 1.0 + gt_ref[0]

        def piece(rows):
            n, _ = _ln_stats(alpha * x_ref[rows, :] + one_gate * y_ref[rows, :])
            e = (n * g_ref[...] + b_ref[...]) - t_ref[rows, :]
            dy_ref[rows, :] = e * (1.0 / d)
            s_ref[...] += jnp.sum(e * e, axis=0, keepdims=True)

        _chunked(ts, piece)

    row = pl.BlockSpec((ts, d), lambda bb, i: (bb * ns + i, 0))
    per = pl.BlockSpec((1, 1, d), lambda bb, i: (bb, 0, 0))
    vec = pl.BlockSpec((1, d), lambda bb, i: (0, 0))
    return _call(body, name=name, grid=(bsz, ns), in_specs=[row, row, per, vec, vec, row], out_specs=[row, vec],
                 out_shape=[_sds((t, d), F32), _sds((1, d), F32)],
                 sem=("arbitrary", "arbitrary"))(x, y, gate, g, b, target)


def _sigmoid(v):
    return 1.0 / (1.0 + jnp.exp(-v))


def _silu_rows(c, *, name):
    rows, d = c.shape

    def body(c_ref, o_ref):
        v = c_ref[...]
        o_ref[...] = (v * _sigmoid(v)).astype(o_ref.dtype)

    full = pl.BlockSpec((rows, d), lambda i: (0, 0))
    return _call(body, name=name, grid=(1,), in_specs=[full], out_specs=full,
                 out_shape=_sds((rows, d), BF16), sem=("arbitrary",))(c)


def _gate_cols(d, ga_off):
    tc = _pick(math.gcd(d, ga_off), (512, 256, 128))
    return tc, ga_off // tc, (ga_off + d) // tc


def _gate_merge_fwd(z, ya, yb, ga_off, *, name):
    t, d = ya.shape
    tr = _pick(t, (1024, 512, 256, 128, 64, 32, 16, 8))
    tc, ga_blk, gb_blk = _gate_cols(d, ga_off)

    def body(ga_ref, gb_ref, ya_ref, yb_ref, o_ref):
        def piece(rows):
            o_ref[rows, :] = (_sigmoid(ga_ref[rows, :].astype(F32)) * ya_ref[rows, :].astype(F32)
                              + _sigmoid(gb_ref[rows, :].astype(F32)) * yb_ref[rows, :].astype(F32)
                              ).astype(o_ref.dtype)

        _chunked(tr, piece)

    blk = pl.BlockSpec((tr, tc), lambda i, j: (i, j))
    return _call(body, name=name, grid=(t // tr, d // tc),
                 in_specs=[pl.BlockSpec((tr, tc), lambda i, j: (i, ga_blk + j)),
                           pl.BlockSpec((tr, tc), lambda i, j: (i, gb_blk + j)), blk, blk],
                 out_specs=blk, out_shape=_sds((t, d), BF16), sem=("parallel", "parallel"))(z, z, ya, yb)


def _gate_merge_bwd(z, ya, yb, dm, ga_off, *, name):
    t, d = ya.shape
    tr = _pick(t, (1024, 512, 256, 128, 64, 32, 16, 8))
    tc, ga_blk, gb_blk = _gate_cols(d, ga_off)

    def body(ga_ref, gb_ref, ya_ref, yb_ref, dm_ref, dya_ref, dyb_ref, dga_ref, dgb_ref):
        def piece(rows):
            dm_v = dm_ref[rows, :].astype(F32)
            sa = _sigmoid(ga_ref[rows, :].astype(F32))
            sb = _sigmoid(gb_ref[rows, :].astype(F32))
            dya_ref[rows, :] = (dm_v * sa).astype(dya_ref.dtype)
            dyb_ref[rows, :] = (dm_v * sb).astype(dyb_ref.dtype)
            dga_ref[rows, :] = (dm_v * ya_ref[rows, :].astype(F32) * sa * (1.0 - sa)).astype(dga_ref.dtype)
            dgb_ref[rows, :] = (dm_v * yb_ref[rows, :].astype(F32) * sb * (1.0 - sb)).astype(dgb_ref.dtype)

        _chunked(tr, piece)

    blk = pl.BlockSpec((tr, tc), lambda i, j: (i, j))
    return _call(body, name=name, grid=(t // tr, d // tc),
                 in_specs=[pl.BlockSpec((tr, tc), lambda i, j: (i, ga_blk + j)),
                           pl.BlockSpec((tr, tc), lambda i, j: (i, gb_blk + j)), blk, blk, blk],
                 out_specs=[blk, blk, blk, blk], out_shape=[_sds((t, d), BF16)] * 4,
                 sem=("parallel", "parallel"))(z, z, ya, yb, dm)


CONV_ROWS = 64
CONV_PAD = 32


def _row_shifts(win):
    total = win.shape[0]
    return [win] + [pltpu.roll(win, total - b, axis=0) for b in range(1, SUBLANES)]


def _shifted_rows(copies, shift):
    start = SUBLANES * (shift // SUBLANES)
    return copies[shift % SUBLANES][start:start + CONV_ROWS]


def _fill_glu(z_ref, ext_ref, s, ch):
    ext_ref[pl.ds(0, CONV_PAD), :] = jnp.zeros((CONV_PAD, ch), F32)

    chunk = min(ROW_CHUNK, s)

    def piece(i, carry):
        start = pl.multiple_of(i * chunk, chunk)
        zz = z_ref[pl.ds(start, chunk), :].astype(F32)
        ext_ref[pl.ds(pl.multiple_of(CONV_PAD + start, CONV_PAD), chunk), :] = zz[:, :ch] * _sigmoid(zz[:, ch:])
        return carry

    lax.fori_loop(0, s // chunk, piece, 0)


def _conv_piece(ext_ref, w_ref, cb_ref, base, kw):
    copies = _row_shifts(ext_ref[pl.ds(base, CONV_ROWS + CONV_PAD), :])
    acc = cb_ref[...] + w_ref[pl.ds(0, 1), :] * _shifted_rows(copies, CONV_PAD - (kw - 1))
    for k in range(1, kw):
        acc = acc + w_ref[pl.ds(k, 1), :] * _shifted_rows(copies, CONV_PAD - (kw - 1) + k)
    return acc, copies


def _conv_branch_fwd(z, w, cb, lg, lb, bsz, ch, *, name):
    t = z.shape[0]
    s = t // bsz
    kw = w.shape[0]

    def body(z_ref, w_ref, cb_ref, lg_ref, lb_ref, o_ref, ext_ref):
        _fill_glu(z_ref, ext_ref, s, ch)

        def step(i, carry):
            base = pl.multiple_of(i * CONV_ROWS, CONV_ROWS)
            a1, _ = _conv_piece(ext_ref, w_ref, cb_ref, base, kw)
            n, _ = _ln_stats(a1)
            a2 = n * lg_ref[...] + lb_ref[...]
            o_ref[pl.ds(base, CONV_ROWS), :] = (a2 * _sigmoid(a2)).astype(o_ref.dtype)
            return carry

        lax.fori_loop(0, s // CONV_ROWS, step, 0)

    vec = pl.BlockSpec((1, ch), lambda b: (0, 0))
    return _call(body, name=name, grid=(bsz,),
                 in_specs=[pl.BlockSpec((s, 2 * ch), lambda b: (b, 0)), pl.BlockSpec((kw, ch), lambda b: (0, 0)),
                           vec, vec, vec],
                 out_specs=pl.BlockSpec((s, ch), lambda b: (b, 0)), out_shape=_sds((t, ch), BF16),
                 scratch=[pltpu.VMEM((CONV_PAD + s, ch), F32)], sem=("parallel",))(z, w, cb, lg, lb)


def _conv_branch_bwd(z, da3, w, cb, lg, lb, bsz, ch, *, name):
    t = z.shape[0]
    s = t // bsz
    kw = w.shape[0]
    n_rows = CONV_ROWS + CONV_PAD

    def body(z_ref, d_ref, w_ref, cb_ref, lg_ref, lb_ref, dz_ref, dw_ref, dcb_ref, dlg_ref, dlb_ref,
             ext_ref, da1_ref):
        @pl.when(pl.program_id(0) == 0)
        def _():
            for ref in (dw_ref, dcb_ref, dlg_ref, dlb_ref):
                ref[...] = jnp.zeros_like(ref)

        _fill_glu(z_ref, ext_ref, s, ch)
        da1_ref[pl.ds(s, CONV_PAD), :] = jnp.zeros((CONV_PAD, ch), F32)

        def grad_a1(i, carry):
            dlg, dlb = carry
            base = pl.multiple_of(i * CONV_ROWS, CONV_ROWS)
            a1, _ = _conv_piece(ext_ref, w_ref, cb_ref, base, kw)
            n, rstd = _ln_stats(a1)
            a2 = n * lg_ref[...] + lb_ref[...]
            sg = _sigmoid(a2)
            da2 = d_ref[pl.ds(base, CONV_ROWS), :] * (sg * (1.0 + a2 * (1.0 - sg)))
            da1_ref[pl.ds(base, CONV_ROWS), :] = _ln_bwd(da2 * lg_ref[...], n, rstd)
            return (dlg + jnp.sum(da2 * n, axis=0, keepdims=True), dlb + jnp.sum(da2, axis=0, keepdims=True))

        zero = jnp.zeros((1, ch), F32)
        dlg, dlb = lax.fori_loop(0, s // CONV_ROWS, grad_a1, (zero, zero))
        dlg_ref[...] += dlg
        dlb_ref[...] += dlb

        def grad_z(i, dcb):
            base = pl.multiple_of(i * CONV_ROWS, CONV_ROWS)
            ahead = _row_shifts(da1_ref[pl.ds(base, n_rows), :])
            dyc = ahead[0][:CONV_ROWS]
            da0 = w_ref[pl.ds(kw - 1, 1), :] * dyc
            for k in range(kw - 1):
                da0 = da0 + w_ref[pl.ds(k, 1), :] * _shifted_rows(ahead, kw - 1 - k)
            behind = _row_shifts(ext_ref[pl.ds(base, n_rows), :])
            for k in range(kw):
                dw_ref[pl.ds(k, 1), :] += jnp.sum(dyc * _shifted_rows(behind, CONV_PAD - (kw - 1) + k),
                                                  axis=0, keepdims=True)
            zz = z_ref[pl.ds(base, CONV_ROWS), :].astype(F32)
            sg = _sigmoid(zz[:, ch:])
            dz_ref[pl.ds(base, CONV_ROWS), :ch] = (da0 * sg).astype(dz_ref.dtype)
            dz_ref[pl.ds(base, CONV_ROWS), ch:] = (da0 * zz[:, :ch] * sg * (1.0 - sg)).astype(dz_ref.dtype)
            return dcb + jnp.sum(dyc, axis=0, keepdims=True)

        dcb_ref[...] += lax.fori_loop(0, s // CONV_ROWS, grad_z, zero)

    vec = pl.BlockSpec((1, ch), lambda b: (0, 0))
    taps = pl.BlockSpec((kw, ch), lambda b: (0, 0))
    return _call(body, name=name, grid=(bsz,),
                 in_specs=[pl.BlockSpec((s, 2 * ch), lambda b: (b, 0)), pl.BlockSpec((s, ch), lambda b: (b, 0)),
                           taps, vec, vec, vec],
                 out_specs=[pl.BlockSpec((s, 2 * ch), lambda b: (b, 0)), taps, vec, vec, vec],
                 out_shape=[_sds((t, 2 * ch), BF16), _sds((kw, ch), F32)] + [_sds((1, ch), F32)] * 3,
                 scratch=[pltpu.VMEM((CONV_PAD + s, ch), F32), pltpu.VMEM((s + CONV_PAD, ch), F32)],
                 sem=("arbitrary",))(z, da3, w, cb, lg, lb)


FFN_ROWS = 64


def _gelu_parts(v):
    cdf = 0.5 * (1.0 + lax.erf(v * (2.0 ** -0.5)))
    return cdf, v * cdf


def _ffn_conv_piece(ext_ref, wb_ref, base):
    win = ext_ref[pl.ds(base, FFN_ROWS + FFN_PAD), :]
    acc = wb_ref[pl.ds(3, 1), :] + wb_ref[pl.ds(2, 1), :] * win[FFN_PAD:]
    acc = acc + wb_ref[pl.ds(1, 1), :] * pltpu.roll(win, 1, axis=0)[FFN_PAD:]
    acc = acc + wb_ref[pl.ds(0, 1), :] * pltpu.roll(win, 2, axis=0)[FFN_PAD:]
    return acc


def _ffn_stage(hg_ref, hl_ref, wg_ref, wl_ref, bg_ref, bl_ref, ext_ref, wb_ref, s, tcf):
    ext_ref[pl.ds(0, FFN_PAD), :] = jnp.zeros((FFN_PAD, 2 * tcf), F32)
    ext_ref[pl.ds(FFN_PAD, s), :tcf] = hg_ref[...].astype(F32)
    ext_ref[pl.ds(FFN_PAD, s), tcf:] = hl_ref[...].astype(F32)
    wb_ref[pl.ds(0, 3), :tcf] = wg_ref[...]
    wb_ref[pl.ds(0, 3), tcf:] = wl_ref[...]
    wb_ref[pl.ds(3, 1), :tcf] = bg_ref[...]
    wb_ref[pl.ds(3, 1), tcf:] = bl_ref[...]


def _ffn_specs(s, tcf, n_f, batch_first):
    def spec(rows, shift):
        if batch_first:
            return pl.BlockSpec((rows, tcf), lambda bb, j: (bb if rows == s else 0, shift + j))
        return pl.BlockSpec((rows, tcf), lambda j, bb: (bb if rows == s else 0, shift + j))

    return [spec(s, 0), spec(s, n_f), spec(3, 0), spec(3, n_f), spec(1, 0), spec(1, n_f)]


def _ffn_act_fwd(hp, w, b, bsz, tcf, *, name):
    t, two_f = hp.shape
    s = t // bsz
    n_f = two_f // (2 * tcf)

    def body(hg_ref, hl_ref, wg_ref, wl_ref, bg_ref, bl_ref, f_ref, ext_ref, wb_ref):
        _ffn_stage(hg_ref, hl_ref, wg_ref, wl_ref, bg_ref, bl_ref, ext_ref, wb_ref, s, tcf)

        def step(i, carry):
            base = pl.multiple_of(i * FFN_ROWS, FFN_ROWS)
            hh = _ffn_conv_piece(ext_ref, wb_ref, base)
            _, gelu = _gelu_parts(hh[:, :tcf])
            f_ref[pl.ds(base, FFN_ROWS), :] = (gelu * hh[:, tcf:]).astype(f_ref.dtype)
            return carry

        lax.fori_loop(0, s // FFN_ROWS, step, 0)

    return _call(body, name=name, grid=(bsz, n_f), in_specs=_ffn_specs(s, tcf, n_f, True),
                 out_specs=pl.BlockSpec((s, tcf), lambda bb, j: (bb, j)),
                 out_shape=_sds((t, two_f // 2), BF16),
                 scratch=[pltpu.VMEM((FFN_PAD + s, 2 * tcf), F32), pltpu.VMEM((SUBLANES, 2 * tcf), F32)],
                 sem=("parallel", "parallel"))(hp, hp, w, w, b, b)


def _ffn_act_bwd(hp, df, w, b, bsz, tcf, *, name):
    t, two_f = hp.shape
    s = t // bsz
    f_dim = two_f // 2
    n_f = f_dim // tcf
    gw = 2 * tcf
    n_rows = FFN_ROWS + FFN_PAD

    def body(hg_ref, hl_ref, wg_ref, wl_ref, bg_ref, bl_ref, df_ref,
             dhg_ref, dhl_ref, dwg_ref, dwl_ref, dbg_ref, dbl_ref, ext_ref, wb_ref, dh_ref):
        @pl.when(pl.program_id(1) == 0)
        def _():
            for ref in (dwg_ref, dwl_ref, dbg_ref, dbl_ref):
                ref[...] = jnp.zeros_like(ref)

        _ffn_stage(hg_ref, hl_ref, wg_ref, wl_ref, bg_ref, bl_ref, ext_ref, wb_ref, s, tcf)
        dh_ref[pl.ds(s, FFN_PAD), :] = jnp.zeros((FFN_PAD, gw), F32)

        def grad_h(i, carry):
            base = pl.multiple_of(i * FFN_ROWS, FFN_ROWS)
            hh = _ffn_conv_piece(ext_ref, wb_ref, base)
            hg = hh[:, :tcf]
            d = df_ref[pl.ds(base, FFN_ROWS), :].astype(F32)
            cdf, gelu = _gelu_parts(hg)
            pdf = jnp.exp(-0.5 * hg * hg) * (1.0 / math.sqrt(2.0 * math.pi))
            dh_ref[pl.ds(base, FFN_ROWS), :tcf] = d * hh[:, tcf:] * (cdf + hg * pdf)
            dh_ref[pl.ds(base, FFN_ROWS), tcf:] = d * gelu
            return carry

        lax.fori_loop(0, s // FFN_ROWS, grad_h, 0)

        def grad_x(i, carry):
            dw0, dw1, dw2, dbs = carry
            base = pl.multiple_of(i * FFN_ROWS, FFN_ROWS)
            nxt = dh_ref[pl.ds(base, n_rows), :]
            dyc = nxt[:FFN_ROWS]
            dx = wb_ref[pl.ds(2, 1), :] * dyc
            dx = dx + wb_ref[pl.ds(1, 1), :] * pltpu.roll(nxt, n_rows - 1, axis=0)[:FFN_ROWS]
            dx = dx + wb_ref[pl.ds(0, 1), :] * pltpu.roll(nxt, n_rows - 2, axis=0)[:FFN_ROWS]
            dhg_ref[pl.ds(base, FFN_ROWS), :] = dx[:, :tcf].astype(dhg_ref.dtype)
            dhl_ref[pl.ds(base, FFN_ROWS), :] = dx[:, tcf:].astype(dhl_ref.dtype)
            win = ext_ref[pl.ds(base, n_rows), :]
            dw2 = dw2 + jnp.sum(dyc * win[FFN_PAD:], axis=0, keepdims=True)
            dw1 = dw1 + jnp.sum(dyc * pltpu.roll(win, 1, axis=0)[FFN_PAD:], axis=0, keepdims=True)
            dw0 = dw0 + jnp.sum(dyc * pltpu.roll(win, 2, axis=0)[FFN_PAD:], axis=0, keepdims=True)
            return dw0, dw1, dw2, dbs + jnp.sum(dyc, axis=0, keepdims=True)

        zero = jnp.zeros((1, gw), F32)
        sums = lax.fori_loop(0, s // FFN_ROWS, grad_x, (zero, zero, zero, zero))
        for k in range(3):
            dwg_ref[pl.ds(k, 1), :] += sums[k][:, :tcf]
            dwl_ref[pl.ds(k, 1), :] += sums[k][:, tcf:]
        dbg_ref[...] += sums[3][:, :tcf]
        dbl_ref[...] += sums[3][:, tcf:]

    half = pl.BlockSpec((s, tcf), lambda j, bb: (bb, j))
    taps = pl.BlockSpec((3, tcf), lambda j, bb: (0, j))
    bias = pl.BlockSpec((1, tcf), lambda j, bb: (0, j))
    return _call(body, name=name, grid=(n_f, bsz), in_specs=_ffn_specs(s, tcf, n_f, False) + [half],
                 out_specs=[half, half, taps, taps, bias, bias],
                 out_shape=[_sds((t, f_dim), BF16)] * 2 + [_sds((3, f_dim), F32)] * 2 + [_sds((1, f_dim), F32)] * 2,
                 scratch=[pltpu.VMEM((FFN_PAD + s, gw), F32), pltpu.VMEM((SUBLANES, gw), F32),
                          pltpu.VMEM((s + FFN_PAD, gw), F32)],
                 sem=("parallel", "arbitrary"))(hp, hp, w, w, b, b, df)


def _split3(v):
    hi = v.astype(BF16)
    r = v - hi.astype(F32)
    mid = r.astype(BF16)
    lo = (r - mid.astype(F32)).astype(BF16)
    return hi, mid, lo


def _tri_dot(tri, v):
    out = None
    for part in _split3(v):
        term = jnp.dot(tri, part, preferred_element_type=F32)
        out = term if out is None else out + term
    return out


def _fgate_fwd(zf, bsz, heads, *, name):
    t, lanes = zf.shape
    s, blk, nb = _seq_tiles(t, bsz, (ATTN_BLOCK, 128))

    def body(z_ref, cumt_ref, cumb_ref, carry_ref):
        @pl.when(pl.program_id(1) == 0)
        def _():
            carry_ref[...] = jnp.zeros_like(carry_ref)

        z = z_ref[...]
        lf = jnp.minimum(z, 0.0) - jnp.log1p(jnp.exp(-jnp.abs(z)))
        r = lax.broadcasted_iota(jnp.int32, (blk, blk), 0)
        c = lax.broadcasted_iota(jnp.int32, (blk, blk), 1)
        tri = (r >= c).astype(BF16)
        cum = _tri_dot(tri, lf) + carry_ref[...]
        carry_ref[...] = cum[blk - 1:blk, :]
        cumt_ref[0] = jnp.transpose(cum)[:heads, :]
        for h in range(heads):
            cumb_ref[0, h] = jnp.broadcast_to(cum[:, h:h + 1], (blk, lanes))

    return _call(body, name=name, grid=(bsz, nb),
                 in_specs=[pl.BlockSpec((blk, lanes), lambda b, i: (b * nb + i, 0))],
                 out_specs=[pl.BlockSpec((1, heads, blk), lambda b, i: (b, 0, i)),
                            pl.BlockSpec((1, heads, blk, lanes), lambda b, i: (b, 0, i, 0))],
                 out_shape=[_sds((bsz, heads, s), F32), _sds((bsz, heads, s, lanes), F32)],
                 scratch=[pltpu.VMEM((1, lanes), F32)], sem=("parallel", "arbitrary"))(zf)


def _fgate_bwd(dcum, zf, bsz, *, name):
    t, lanes = zf.shape
    pairs = dcum.shape[1]
    s, blk, nb = _seq_tiles(t, bsz, (ATTN_BLOCK, 128))

    def body(d_ref, z_ref, o_ref, carry_ref):
        @pl.when(pl.program_id(1) == 0)
        def _():
            carry_ref[...] = jnp.zeros_like(carry_ref)

        dcol = d_ref[0, 0]
        for p in range(1, pairs):
            dcol = dcol + d_ref[0, p]
        r = lax.broadcasted_iota(jnp.int32, (blk, blk), 0)
        c = lax.broadcasted_iota(jnp.int32, (blk, blk), 1)
        tri = (c >= r).astype(BF16)
        suf = _tri_dot(tri, dcol) + carry_ref[...]
        carry_ref[...] = suf[0:1, :]
        o_ref[...] = suf * _sigmoid(-z_ref[...])

    return _call(body, name=name, grid=(bsz, nb),
                 in_specs=[pl.BlockSpec((1, pairs, blk, lanes), lambda b, i: (b, 0, nb - 1 - i, 0)),
                           pl.BlockSpec((blk, lanes), lambda b, i: (b * nb + nb - 1 - i, 0))],
                 out_specs=pl.BlockSpec((blk, lanes), lambda b, i: (b * nb + nb - 1 - i, 0)),
                 out_shape=_sds((t, lanes), F32), scratch=[pltpu.VMEM((1, lanes), F32)],
                 sem=("parallel", "arbitrary"))(dcum, zf)


def _to_features_major(z, col_off, width, n, *, name):
    t = z.shape[0]
    tr = _pick(t, (512, 256, 128))
    first = col_off // width

    def body(*refs):
        o_ref = refs[n]
        for g in range(n):
            o_ref[pl.ds(g * width, width), :] = jnp.transpose(refs[g][...].astype(F32)).astype(o_ref.dtype)

    return _call(body, name=name, grid=(t // tr,),
                 in_specs=[pl.BlockSpec((tr, width), lambda i, g=g: (i, first + g)) for g in range(n)],
                 out_specs=pl.BlockSpec((n * width, tr), lambda i: (0, i)),
                 out_shape=_sds((n * width, t), BF16), sem=("parallel",))(*([z] * n))


def _to_rows_major(xt, *, name):
    w, t = xt.shape
    tr = _pick(t, (512, 256, 128))

    def body(x_ref, o_ref):
        o_ref[...] = jnp.transpose(x_ref[...]).astype(o_ref.dtype)

    return _call(body, name=name, grid=(t // tr,),
                 in_specs=[pl.BlockSpec((w, tr), lambda i: (0, i))],
                 out_specs=pl.BlockSpec((tr, w), lambda i: (i, 0)),
                 out_shape=_sds((t, w), BF16), sem=("parallel",))(xt)


def _loop_by_twos(lo, hi, body, carry):
    count = hi - lo

    def group(n, first, cr):
        for u in range(n):
            cr = body(first + u, cr)
        return cr

    trips = count // ATTN_UNROLL
    carry = lax.fori_loop(0, trips, lambda t, cr: group(ATTN_UNROLL, lo + ATTN_UNROLL * t, cr), carry)
    rest = count - ATTN_UNROLL * trips
    first = lo + ATTN_UNROLL * trips
    for n in range(ATTN_UNROLL - 1, 0, -1):
        carry = lax.cond(rest == n, lambda cr, n=n: group(n, first, cr), lambda cr: cr, carry)
    return carry


def _head_masks(shape, axis):
    feat = lax.broadcasted_iota(jnp.int32, shape, axis)
    return feat < HEAD_DIM, feat >= HEAD_DIM


def _attn_fwd(z, qkvt, cumt, cumb, bsz, heads, q_off, *, name):
    t = z.shape[0]
    width = heads * HEAD_DIM
    pairs = heads // 2
    s = t // bsz
    blk = ATTN_BLOCK
    nq = s // blk
    k_col = (q_off + width) // LANES
    v_row = 2 * width // LANES
    reps = blk // LANES

    def body(k_ref, qt_ref, vt_ref, cqt_ref, ckb_ref, ot_ref, lse_ref):
        p_id = pl.program_id(1)
        i = pl.program_id(2)
        qt = qt_ref[...]
        masks = _head_masks((LANES, blk), 0)
        qtm = [jnp.where(mk, qt, jnp.zeros_like(qt)) for mk in masks]
        cq = [cqt_ref[0, pl.ds(2 * p_id + hh, 1), :] for hh in range(2)]
        kidx = lax.broadcasted_iota(jnp.int32, (blk, blk), 0)
        qidx = lax.broadcasted_iota(jnp.int32, (blk, blk), 1)

        def block(j, carry, masked):
            off = pl.multiple_of(j * blk, blk)
            kp = k_ref[pl.ds(off, blk), :].astype(BF16)
            vtp = vt_ref[:, pl.ds(off, blk)]
            out = []
            for hh in range(2):
                m, l, acc = carry[hh]
                sc = jnp.dot(kp, qtm[hh], preferred_element_type=F32) * ATTN_SCALE
                ck = ckb_ref[0, hh, pl.ds(off, blk), :]
                sc = (sc + cq[hh]) - jnp.concatenate([ck] * reps, axis=1)
                if masked:
                    sc = jnp.where(qidx >= kidx, sc, NEG)
                m_new = jnp.maximum(m, jnp.max(sc, axis=0, keepdims=True))
                pr = jnp.exp(sc - m_new)
                a = jnp.exp(m - m_new)
                l = a * l + jnp.sum(pr, axis=0, keepdims=True)
                p_hi = pr.astype(BF16)
                p_lo = (pr - p_hi.astype(F32)).astype(BF16)
                pv = (jnp.dot(vtp, p_hi, preferred_element_type=F32)
                      + jnp.dot(vtp, p_lo, preferred_element_type=F32))
                acc = a * acc + pv[hh * HEAD_DIM:(hh + 1) * HEAD_DIM]
                out.append((m_new, l, acc))
            return tuple(out)

        init = tuple((jnp.full((1, blk), NEG, F32), jnp.zeros((1, blk), F32), jnp.zeros((HEAD_DIM, blk), F32))
                     for _ in range(2))
        carry = _loop_by_twos(0, i, lambda j, cr: block(j, cr, False), init)
        carry = block(i, carry, True)
        lse_ref[...] = jnp.zeros_like(lse_ref)
        for hh in range(2):
            m, l, acc = carry[hh]
            ot_ref[pl.ds(hh * HEAD_DIM, HEAD_DIM), :] = acc / l
            lse_ref[0, 0, pl.ds(hh, 1), :] = m + jnp.log(l)

    return _call(body, name=name, grid=(bsz, pairs, nq),
                 in_specs=[pl.BlockSpec((s, LANES), lambda b, p, i: (b, k_col + p)),
                           pl.BlockSpec((LANES, blk), lambda b, p, i: (p, b * nq + i)),
                           pl.BlockSpec((LANES, s), lambda b, p, i: (v_row + p, b)),
                           pl.BlockSpec((1, heads, blk), lambda b, p, i: (b, 0, i)),
                           pl.BlockSpec((1, 2, s, LANES), lambda b, p, i: (b, p, 0, 0))],
                 out_specs=[pl.BlockSpec((LANES, blk), lambda b, p, i: (p, b * nq + i)),
                            pl.BlockSpec((1, 1, SUBLANES, blk), lambda b, p, i: (b, p, 0, i))],
                 out_shape=[_sds((width, t), F32), _sds((bsz, pairs, SUBLANES, s), F32)],
                 sem=("parallel", "parallel", "parallel"))(z, qkvt, qkvt, cumt, cumb)


def _attn_bwd(z, qkvt, cumt, cumb, ot, do, dot, lse, bsz, heads, q_off, *, name):
    t = z.shape[0]
    width = heads * HEAD_DIM
    pairs = heads // 2
    s = t // bsz
    blk = ATTN_BLOCK
    nkv = s // blk
    q_col = q_off // LANES
    k_col = (q_off + width) // LANES
    v_col = (q_off + 2 * width) // LANES
    k_row = width // LANES
    reps = blk // LANES

    def body(k_ref, v_ref, kt_ref, q_ref, qt_ref, do_ref, dot_ref, ot_ref, lse_ref, ckb_ref, cqt_ref,
             dk_ref, dv_ref, dqt_ref, dcum_ref, dqt_acc, ds_acc):
        p_id = pl.program_id(1)
        j = pl.program_id(2)

        @pl.when(j == 0)
        def _():
            dqt_acc[...] = jnp.zeros_like(dqt_acc)

        kp = k_ref[...].astype(BF16)
        vp = v_ref[...].astype(BF16)
        kt = kt_ref[...]
        feat_masks = _head_masks((LANES, blk), 0)
        lane_masks = _head_masks((blk, LANES), 1)
        ktm = [jnp.where(mk, kt, jnp.zeros_like(kt)) for mk in feat_masks]
        ck = [jnp.concatenate([ckb_ref[0, hh]] * reps, axis=1) for hh in range(2)]
        kidx = lax.broadcasted_iota(jnp.int32, (blk, blk), 0)
        qidx = lax.broadcasted_iota(jnp.int32, (blk, blk), 1)
        ds_acc[...] = jnp.zeros_like(ds_acc)

        def block(i, carry, masked):
            dk, dv = carry
            off = pl.multiple_of(i * blk, blk)
            qt = qt_ref[:, pl.ds(off, blk)]
            dt = dot_ref[:, pl.ds(off, blk)]
            o_t = ot_ref[:, pl.ds(off, blk)]
            q_rows = q_ref[pl.ds(off, blk), :].astype(BF16)
            do_rows = do_ref[pl.ds(off, blk), :]
            for hh in range(2):
                qtm = jnp.where(feat_masks[hh], qt, jnp.zeros_like(qt))
                dtm = jnp.where(feat_masks[hh], dt, jnp.zeros_like(dt))
                sc = jnp.dot(kp, qtm, preferred_element_type=F32) * ATTN_SCALE
                sc = (sc + cqt_ref[0, pl.ds(2 * p_id + hh, 1), pl.ds(off, blk)]) - ck[hh]
                pr = jnp.exp(sc - lse_ref[0, 0, pl.ds(hh, 1), pl.ds(off, blk)])
                if masked:
                    pr = jnp.where(qidx >= kidx, pr, 0.0)
                dp = jnp.dot(vp, dtm, preferred_element_type=F32)
                delta = jnp.sum(dtm.astype(F32) * o_t, axis=0, keepdims=True)
                ds = pr * (dp - delta)
                ds_acc[hh] += ds
                dsb = ds.astype(BF16)
                qm = jnp.where(lane_masks[hh], q_rows, jnp.zeros_like(q_rows))
                dom = jnp.where(lane_masks[hh], do_rows, jnp.zeros_like(do_rows))
                dv = dv + jnp.dot(pr.astype(BF16), dom, preferred_element_type=F32)
                dk = dk + jnp.dot(dsb, qm, preferred_element_type=F32) * ATTN_SCALE
                dqt_acc[:, pl.ds(off, blk)] += jnp.dot(ktm[hh], dsb, preferred_element_type=F32) * ATTN_SCALE
            return dk, dv

        zero = jnp.zeros((blk, LANES), F32)
        carry = block(j, (zero, zero), True)
        dk, dv = _loop_by_twos(j + 1, nkv, lambda i, cr: block(i, cr, False), carry)
        dk_ref[...] = dk.astype(dk_ref.dtype)
        dv_ref[...] = dv.astype(dv_ref.dtype)
        lane = lax.broadcasted_iota(jnp.int32, (blk, LANES), 1)
        dcum = jnp.zeros((blk, LANES), F32)
        for hh in range(2):
            col = jnp.sum(ds_acc[hh], axis=1, keepdims=True)
            dcum = jnp.where(lane == 2 * p_id + hh, -col, dcum)
        dcum_ref[0, 0] = dcum

        @pl.when(j == nkv - 1)
        def _():
            dqt_ref[...] = dqt_acc[...]

    key_rows = lambda col: pl.BlockSpec((blk, LANES), lambda b, p, j: (b * nkv + j, col + p))
    seq_t = lambda row: pl.BlockSpec((LANES, s), lambda b, p, j: (row + p, b))
    return _call(body, name=name, grid=(bsz, pairs, nkv),
                 in_specs=[key_rows(k_col), key_rows(v_col),
                           pl.BlockSpec((LANES, blk), lambda b, p, j: (k_row + p, b * nkv + j)),
                           pl.BlockSpec((s, LANES), lambda b, p, j: (b, q_col + p)), seq_t(0),
                           pl.BlockSpec((s, LANES), lambda b, p, j: (b, p)), seq_t(0), seq_t(0),
                           pl.BlockSpec((1, 1, SUBLANES, s), lambda b, p, j: (b, p, 0, 0)),
                           pl.BlockSpec((1, 2, blk, LANES), lambda b, p, j: (b, p, j, 0)),
                           pl.BlockSpec((1, heads, s), lambda b, p, j: (b, 0, 0))],
                 out_specs=[key_rows(0), key_rows(0), seq_t(0),
                            pl.BlockSpec((1, 1, blk, LANES), lambda b, p, j: (b, p, j, 0))],
                 out_shape=[_sds((t, width), BF16), _sds((t, width), BF16), _sds((width, t), F32),
                            _sds((bsz, pairs, s, LANES), F32)],
                 scratch=[pltpu.VMEM((LANES, s), F32), pltpu.VMEM((2, blk, blk), F32)],
                 sem=("parallel", "parallel", "arbitrary"))(z, z, qkvt, z, qkvt, do, dot, ot, lse, cumb, cumt)


def _adamw(w, g, m, v, *, name):
    bc1 = 1.0 - ADAM_B1 ** ADAM_STEP
    bc2 = 1.0 - ADAM_B2 ** ADAM_STEP

    def body(w_ref, g_ref, m_ref, v_ref, d_ref, nm_ref, nv_ref):
        g_v = g_ref[...]
        nm = ADAM_B1 * m_ref[...] + (1.0 - ADAM_B1) * g_v
        nv = ADAM_B2 * v_ref[...] + (1.0 - ADAM_B2) * (g_v * g_v)
        nm_ref[...] = nm
        nv_ref[...] = nv
        d_ref[...] = -ADAM_LR * ((nm / bc1) / (jnp.sqrt(nv / bc2) + ADAM_EPS) + ADAM_WD * w_ref[...])

    if w.ndim == 2:
        grid = (1,)
        blk = pl.BlockSpec(w.shape, lambda i: (0, 0))
    else:
        layers, rows, cols = w.shape
        tr = rows if rows <= 256 else _pick(rows, (256, 128, 64, 32, 16, 8))
        grid = (layers, rows // tr)
        blk = pl.BlockSpec((1, tr, cols), lambda layer, i: (layer, i, 0))
    return tuple(_call(body, name=name, grid=grid, in_specs=[blk] * 4, out_specs=[blk] * 3,
                       out_shape=[_sds(w.shape, F32)] * 3, sem=("parallel",) * len(grid))(w, g, m, v))


_ANY = pl.BlockSpec(memory_space=pl.ANY)


def _comm_call(body, *, name, n_in, out_shape, n_sems):
    scratch = [pltpu.SemaphoreType.DMA((n_sems,)), pltpu.SemaphoreType.DMA((n_sems,)),
               pltpu.SemaphoreType.DMA((len(out_shape),))]
    return pl.pallas_call(body, name=name, in_specs=[_ANY] * n_in, out_specs=[_ANY] * len(out_shape),
                          out_shape=out_shape, scratch_shapes=scratch)


def _place():
    x, y, c = lax.axis_index("x"), lax.axis_index("y"), lax.axis_index("c")
    return x, y, c, [(1 - x, y), (x, 1 - y), (1 - x, 1 - y)]


def _remote(src, dst, send_sems, recv_sems, sem, to):
    return pltpu.make_async_remote_copy(src_ref=src, dst_ref=dst, send_sem=send_sems.at[sem],
                                        recv_sem=recv_sems.at[sem], device_id=to, device_id_type=MESH)


def _all_gather8(v, *, name):
    def body(v_ref, out_ref, send_sems, recv_sems, local_sems):
        x, y, c, _ = _place()
        me = 4 * x + 2 * y + c
        mine = pltpu.make_async_copy(v_ref, out_ref.at[me], local_sems.at[0])
        mine.start()
        peers = []
        for k in range(1, N_DEVICES):
            px = 1 - x if k & 4 else x
            py = 1 - y if k & 2 else y
            pc = 1 - c if k & 1 else c
            peers.append((px, py, pc))
        sends = [_remote(v_ref, out_ref.at[me], send_sems, recv_sems, k, peer) for k, peer in enumerate(peers)]
        for cp in sends:
            cp.start()
        for k, (px, py, pc) in enumerate(peers):
            _remote(v_ref, out_ref.at[4 * px + 2 * py + pc], send_sems, recv_sems, k, (px, py, pc)).wait_recv()
        for cp in sends:
            cp.wait_send()
        mine.wait()

    out = _comm_call(body, name=name, n_in=1, out_shape=[_sds((N_DEVICES,) + v.shape, v.dtype)],
                     n_sems=N_DEVICES - 1)(v)
    return out[0]


def _window(ref, mode, layer, chip, rows, cols, half=None):
    first, count = (0, rows) if half is None else (half * (rows // 2), rows // 2)
    if mode == "slab":
        return ref.at[layer, chip] if half is None else ref.at[layer, chip, pl.ds(first, count), :]
    if mode == "cols":
        col_window = pl.ds(pl.multiple_of(chip * cols, LANES), cols)
        return ref.at[layer, :, col_window] if half is None else ref.at[layer, pl.ds(first, count), col_window]
    return ref.at[layer, pl.ds(pl.multiple_of(chip * rows + first, SUBLANES), count), :]


def _whole_shape(mode, shard_shape):
    layers, rows, cols = shard_shape
    if mode == "slab":
        return (layers, N_CHIPS, rows, cols)
    if mode == "cols":
        assert cols % LANES == 0
        return (layers, rows, N_CHIPS * cols)
    assert rows % 16 == 0
    return (layers, N_CHIPS * rows, cols)


def _gather_weights(shards, modes, *, name):
    n = len(shards)
    meta = [(mode,) + tuple(a.shape[1:]) for a, mode in zip(shards, modes)]
    for a in shards:
        assert a.shape[0] == 2 and a.shape[1] % 2 == 0
    per = 8

    def body(*refs):
        ins, outs = refs[:n], refs[n:2 * n]
        send_sems, recv_sems, _ = refs[2 * n:]
        x, y, c, _ = _place()
        me, x_nbr, y_nbr, diagonal = 2 * x + y, 2 * (1 - x) + y, 2 * x + 1 - y, 2 * (1 - x) + 1 - y
        to_x, to_y, sibling = (1 - x, y, c), (x, 1 - y, c), (x, y, 1 - c)
        sent = []

        def copy(src, dst, sem, to):
            cp = _remote(src, dst, send_sems, recv_sems, sem, to)
            cp.start()
            sent.append(cp)

        def arrived(win, sem):
            _remote(win, win, send_sems, recv_sems, sem, sibling).wait_recv()

        for i, (mode, rows, cols) in enumerate(meta):
            mine = _window(outs[i], mode, c, me, rows, cols)
            copy(ins[i].at[c], mine, per * i, to_x)
            copy(ins[i].at[c], mine, per * i + 1, to_y)
            copy(ins[i], _window(outs[i], mode, slice(None), me, rows, cols), per * i + 7, sibling)
        for i, (mode, rows, cols) in enumerate(meta):
            arrived(_window(outs[i], mode, c, x_nbr, rows, cols), per * i)
            half = _window(outs[i], mode, c, x_nbr, rows, cols, half=0)
            copy(half, half, per * i + 2, to_y)
            win = _window(outs[i], mode, c, x_nbr, rows, cols)
            copy(win, win, per * i + 4, sibling)
            arrived(_window(outs[i], mode, c, y_nbr, rows, cols), per * i + 1)
            half = _window(outs[i], mode, c, y_nbr, rows, cols, half=1)
            copy(half, half, per * i + 3, to_x)
            win = _window(outs[i], mode, c, y_nbr, rows, cols)
            copy(win, win, per * i + 5, sibling)
        for i, (mode, rows, cols) in enumerate(meta):
            arrived(_window(outs[i], mode, c, diagonal, rows, cols, half=0), per * i + 2)
            arrived(_window(outs[i], mode, c, diagonal, rows, cols, half=1), per * i + 3)
            win = _window(outs[i], mode, c, diagonal, rows, cols)
            copy(win, win, per * i + 6, sibling)
        for i, (mode, rows, cols) in enumerate(meta):
            arrived(_window(outs[i], mode, slice(None), me, rows, cols), per * i + 7)
            for k, chip in enumerate((x_nbr, y_nbr, diagonal)):
                arrived(_window(outs[i], mode, 1 - c, chip, rows, cols), per * i + 4 + k)
        for cp in sent:
            cp.wait_send()

    out_shape = [_sds(_whole_shape(mode, a.shape), a.dtype) for a, mode in zip(shards, modes)]
    return _comm_call(body, name=name, n_in=n, out_shape=out_shape, n_sems=per * n)(*shards)


def _rs_swap(grads, *, name):
    n = len(grads)

    def body(*refs):
        ins, outs = refs[:n], refs[n:2 * n]
        send_sems, recv_sems, _ = refs[2 * n:]
        x, y, c, _ = _place()
        copies = [_remote(ins[i].at[1 - c], outs[i], send_sems, recv_sems, i, (x, y, 1 - c)) for i in range(n)]
        for cp in copies:
            cp.start()
        for cp in copies:
            cp.wait()

    return _comm_call(body, name=name, n_in=n, out_shape=[_sds(g.shape[1:], g.dtype) for g in grads], n_sems=n)(*grads)


def _part(ref, mode, chip, rows, cols):
    if mode == "slab":
        return ref.at[chip]
    if mode == "cols":
        return ref.at[:, pl.ds(pl.multiple_of(chip * cols, LANES), cols)]
    return ref.at[pl.ds(pl.multiple_of(chip * rows, SUBLANES), rows), :]


def _rs_scatter(parts, modes, shard_shapes, *, name):
    n = len(parts)
    meta = [(mode,) + tuple(shp[1:]) for mode, shp in zip(modes, shard_shapes)]

    def body(*refs):
        ins, outs = refs[:n], refs[n:2 * n]
        send_sems, recv_sems, local_sems = refs[2 * n:]
        x, y, c, chips = _place()
        me = 2 * x + y
        local, sends = [], []
        for i, (mode, rows, cols) in enumerate(meta):
            cp = pltpu.make_async_copy(_part(ins[i], mode, me, rows, cols), outs[i].at[me], local_sems.at[i])
            cp.start()
            local.append(cp)
            for r, (cx, cy) in enumerate(chips):
                cp = _remote(_part(ins[i], mode, 2 * cx + cy, rows, cols), outs[i].at[me], send_sems, recv_sems,
                             3 * i + r, (cx, cy, c))
                cp.start()
                sends.append(cp)
        for i, (mode, rows, cols) in enumerate(meta):
            for r, (cx, cy) in enumerate(chips):
                k = 2 * cx + cy
                _remote(_part(ins[i], mode, k, rows, cols), outs[i].at[k], send_sems, recv_sems, 3 * i + r,
                        (cx, cy, c)).wait_recv()
        for cp in sends:
            cp.wait_send()
        for cp in local:
            cp.wait()

    out_shape = [_sds((N_CHIPS,) + tuple(shp[1:]), p.dtype) for p, shp in zip(parts, shard_shapes)]
    return _comm_call(body, name=name, n_in=n, out_shape=out_shape, n_sems=3 * n)(*parts)


def _rs_exchange(sums, *, name):
    n = len(sums)

    def body(*refs):
        ins, outs = refs[:n], refs[n:2 * n]
        send_sems, recv_sems, _ = refs[2 * n:]
        x, y, c, _ = _place()
        copies = [_remote(ins[i], outs[i], send_sems, recv_sems, i, (x, y, 1 - c)) for i in range(n)]
        for cp in copies:
            cp.start()
        for cp in copies:
            cp.wait()

    return _comm_call(body, name=name, n_in=n, out_shape=[_sds(s.shape, s.dtype) for s in sums], n_sems=n)(*sums)


def _row_tile(rows, cols, itemsize):
    target = max(SUBLANES, (2 << 20) // (cols * itemsize))
    cands = [c for c in (2048, 1024, 512, 256, 128, 64, 32, 16) if c <= target]
    tr = _pick(rows, cands)
    return tr


def _add_layer(g, other, core, *, name):
    _, rows, cols = g.shape
    tr = _row_tile(rows, cols, 4)

    def body(core_ref, g_ref, o_ref, out_ref):
        out_ref[...] = (g_ref[0] + o_ref[...]).astype(out_ref.dtype)

    grid_spec = pltpu.PrefetchScalarGridSpec(
        num_scalar_prefetch=1, grid=(rows // tr,),
        in_specs=[pl.BlockSpec((1, tr, cols), lambda i, core_ref: (core_ref[0], i, 0)),
                  pl.BlockSpec((tr, cols), lambda i, core_ref: (i, 0))],
        out_specs=pl.BlockSpec((tr, cols), lambda i, core_ref: (i, 0)))
    return pl.pallas_call(body, name=name, grid_spec=grid_spec, out_shape=_sds((rows, cols), BF16),
                          compiler_params=pltpu.CompilerParams(dimension_semantics=("parallel",),
                                                               vmem_limit_bytes=VMEM_LIMIT))(core, g, other)


def _sum_slots(parts, *, name):
    n, rows, cols = parts.shape
    tr = _row_tile(rows, cols, 4)

    def body(p_ref, o_ref):
        acc = p_ref[0].astype(F32) + p_ref[1].astype(F32)
        for k in range(2, n):
            acc = acc + p_ref[k].astype(F32)
        o_ref[...] = acc

    return _call(body, name=name, grid=(rows // tr,),
                 in_specs=[pl.BlockSpec((n, tr, cols), lambda i: (0, i, 0))],
                 out_specs=pl.BlockSpec((tr, cols), lambda i: (i, 0)),
                 out_shape=_sds((rows, cols), F32), sem=("parallel",))(parts)


def _reduce_scatter(grads, modes, shard_shapes):
    core = lax.axis_index("c").astype(jnp.int32).reshape(1)
    flat = [g.reshape(g.shape[0], -1, g.shape[-1]) for g in grads]
    from_sibling = _rs_swap(flat, name="rs_swap")
    parts = []
    for i, (g, o) in enumerate(zip(flat, from_sibling)):
        p = _add_layer(g, o, core, name=f"rs_add_{i}")
        parts.append(p.reshape(grads[i].shape[1:]))
    from_chips = _rs_scatter(parts, modes, shard_shapes, name="rs_scatter")
    sums = [_sum_slots(r, name=f"rs_sum_{i}") for i, r in enumerate(from_chips)]
    others = _rs_exchange(sums, name="rs_exchange")
    mine_first = lax.axis_index("c") == 0
    return [jnp.where(mine_first, jnp.stack([mine, other]), jnp.stack([other, mine]))
            for mine, other in zip(sums, others)]


def _layer_weights(full, rep, layer, dims):
    f_off, n_heads = dims["f_off"], dims["heads"]
    b_in = rep["b_in"][layer]
    pad = LANES - n_heads
    return {
        "w_main": (full["w_main"], layer),
        "b_main": jnp.concatenate([b_in[:f_off], b_in[f_off + n_heads:]])[None],
        "w_f": (full["w_f"], layer),
        "w_f_matrix": full["w_f"][layer],
        "b_f": jnp.pad(b_in[f_off:f_off + n_heads], (0, pad))[None],
        "conv_a_w": full["conv_a_w"][layer],
        "conv_a_b": rep["conv_a_b"][layer][None],
        "ln_conv_g": rep["ln_conv_g"][layer][None],
        "ln_conv_b": rep["ln_conv_b"][layer][None],
        "w_conv_proj": (full["w_conv_proj"], layer),
        "w_attn_proj": (full["w_attn_proj"], layer),
        "w_mix_out": (full["w_mix_out"], layer),
        "b_mix_out": rep["b_mix_out"][layer][None],
        "ln1_g": rep["ln1_g"][layer][None],
        "ln1_b": rep["ln1_b"][layer][None],
        "w_ffn_up": (full["w_ffn_up"], layer),
        "ffn_conv_w": full["ffn_conv_w"][layer],
        "ffn_conv_b": rep["ffn_conv_b"][layer][None],
        "w_ffn_down": (full["w_ffn_down"], layer),
        "ln2_g": rep["ln2_g"][layer][None],
        "ln2_b": rep["ln2_b"][layer][None],
    }


def _split_mod(mod, d):
    return [mod[:, k * d:(k + 1) * d][:, None, :] for k in range(6)]


def _layer_fwd(x, u, mods, p, dims, tag, after):
    bsz, ch, heads, alpha = dims["bsz"], dims["ch"], dims["heads"], dims["alpha"]
    _, _, gate1, shift2, scale2, gate2 = mods
    zm = _matmul(u, p["w_main"], "nn", BF16, bias=p["b_main"], name=f"in_main_{tag}")
    zf = _matmul(u, p["w_f"], "nn", F32, bias=p["b_f"], name=f"in_forget_{tag}")
    a3 = _conv_branch_fwd(zm, p["conv_a_w"], p["conv_a_b"], p["ln_conv_g"], p["ln_conv_b"], bsz, ch,
                          name=f"conv_branch_{tag}")
    ya = _matmul(a3, p["w_conv_proj"], "nn", BF16, name=f"conv_proj_{tag}")
    cumt, cumb = _fgate_fwd(zf, bsz, heads, name=f"fgate_{tag}")
    qkvt = _to_features_major(zm, 2 * ch, heads * HEAD_DIM, 3, name=f"qkv_t_{tag}")
    ot, lse = _attn_fwd(zm, qkvt, cumt, cumb, bsz, heads, 2 * ch, name=f"attn_{tag}")
    yb = _matmul(ot, p["w_attn_proj"], "tn", BF16, name=f"attn_proj_{tag}")
    m = _gate_merge_fwd(zm, ya, yb, dims["ga_off"], name=f"merge_{tag}")
    mix = _matmul(m, p["w_mix_out"], "nn", F32, bias=p["b_mix_out"], name=f"mix_out_{tag}")
    x1, u2 = _ln_res_mod_fwd(x, mix, gate1, p["ln1_g"], p["ln1_b"], scale2, shift2, alpha, bsz,
                             name=f"ln_res1_mod2_{tag}")
    hp = _matmul(u2, p["w_ffn_up"], "nn", BF16, name=f"ffn_up_{tag}")
    f = _ffn_act_fwd(hp, p["ffn_conv_w"], p["ffn_conv_b"], bsz, dims["tcf"], name=f"ffn_act_{tag}")
    ffn = _matmul(f, p["w_ffn_down"], "nn", F32, name=f"ffn_down_{tag}")
    if isinstance(after, tuple):
        x2 = _ln_res_mod_fwd(x1, ffn, gate2, p["ln2_g"], p["ln2_b"], after[0], after[1], alpha, bsz,
                             name=f"ln_res2_mod1_{tag}")
    else:
        x2 = _ln_res_loss(x1, ffn, gate2, p["ln2_g"], p["ln2_b"], after, alpha, bsz, name=f"ln_res2_loss_{tag}")
    saved = dict(x=x, mods=mods, u=u, zm=zm, zf=zf, a3=a3, ya=ya, yb=yb, cumt=cumt, cumb=cumb,
                 qkvt=qkvt, ot=ot, lse=lse, m=m, mix=mix, x1=x1, u2=u2, hp=hp, f=f, ffn=ffn)
    return x2, saved


def _layer_bwd(top, p, sv, dims, tag, below):
    bsz, ch, heads, alpha = dims["bsz"], dims["ch"], dims["heads"], dims["alpha"]
    f_off, tcf = dims["f_off"], dims["tcf"]
    shift1, scale1, gate1, shift2, scale2, gate2 = sv["mods"]
    g = {}
    dr2, dffn, dgate2, g["ln2_g"], g["ln2_b"] = top
    df = _matmul(dffn, p["w_ffn_down"], "nt", BF16, name=f"ffn_down_dx_{tag}")
    g["w_ffn_down"] = _matmul(sv["f"], dffn, "tn", F32, name=f"ffn_down_dw_{tag}")
    dhg, dhl, dwg, dwl, dbg, dbl = _ffn_act_bwd(sv["hp"], df, p["ffn_conv_w"], p["ffn_conv_b"], bsz, tcf,
                                                name=f"ffn_act_bwd_{tag}")
    g["ffn_conv_w"] = jnp.concatenate([dwg, dwl], axis=1)
    g["ffn_conv_b"] = jnp.concatenate([dbg, dbl], axis=1)[0]
    du2 = _matmul(dhg, p["w_ffn_up"], "nt", F32, name=f"ffn_up_gate_dx_{tag}")
    du2 = _matmul(dhl, p["w_ffn_up"], "nt", F32, add=du2, b_k_first=dhg.shape[1], name=f"ffn_up_lin_dx_{tag}")
    d_ff = dhg.shape[1]
    dw_up = _matmul(sv["u2"], dhg, "tn", F32, out_cols=2 * d_ff, name=f"ffn_up_gate_dw_{tag}")
    g["w_ffn_up"] = _matmul(sv["u2"], dhl, "tn", F32, into=(dw_up, d_ff), name=f"ffn_up_lin_dw_{tag}")
    dr1, dmix, dgate1, g["ln1_g"], g["ln1_b"], g["b_mix_out"], dscale2, dshift2 = _ln_mod_res_bwd(
        du2, dr2, scale2, sv["x"], sv["mix"], gate1, p["ln1_g"], p["ln1_b"], alpha, bsz,
        name=f"ln_mod2_res1_bwd_{tag}")
    dm = _matmul(dmix, p["w_mix_out"], "nt", BF16, name=f"mix_out_dx_{tag}")
    g["w_mix_out"] = _matmul(sv["m"], dmix, "tn", F32, name=f"mix_out_dw_{tag}")
    dya, dyb, dzga, dzgb = _gate_merge_bwd(sv["zm"], sv["ya"], sv["yb"], dm, dims["ga_off"], name=f"merge_bwd_{tag}")
    da3 = _matmul(dya, p["w_conv_proj"], "nt", F32, name=f"conv_proj_dx_{tag}")
    g["w_conv_proj"] = _matmul(sv["a3"], dya, "tn", F32, name=f"conv_proj_dw_{tag}")
    do = _matmul(dyb, p["w_attn_proj"], "nt", BF16, name=f"attn_proj_dx_{tag}")
    dot = _matmul(p["w_attn_proj"], dyb, "nt", BF16, name=f"attn_proj_dxt_{tag}")
    g["w_attn_proj"] = _matmul(sv["ot"], dyb, "nn", F32, name=f"attn_proj_dw_{tag}")
    dzglu, g["conv_a_w"], dcb, g["ln_conv_g"], g["ln_conv_b"] = _conv_branch_bwd(
        sv["zm"], da3, p["conv_a_w"], p["conv_a_b"], p["ln_conv_g"], p["ln_conv_b"], bsz, ch,
        name=f"conv_branch_bwd_{tag}")
    g["conv_a_b"] = dcb[0]
    dk, dv, dqt, dcum = _attn_bwd(sv["zm"], sv["qkvt"], sv["cumt"], sv["cumb"], sv["ot"], do, dot, sv["lse"], bsz,
                                  heads, 2 * ch, name=f"attn_bwd_{tag}")
    dq = _to_rows_major(dqt, name=f"dq_rows_{tag}")
    dzf = _fgate_bwd(dcum, sv["zf"], bsz, name=f"fgate_bwd_{tag}")
    dzm = jnp.concatenate([dzglu, dq, dk, dv, dzga, dzgb], axis=1)
    du = _matmul(dzm, p["w_main"], "nt", F32, name=f"in_main_dx_{tag}")
    dwm, dbm = _matmul(sv["u"], dzm, "tn", F32, colsum=True, name=f"in_main_dw_{tag}")
    dwf, dbf = _matmul(sv["u"], dzf, "tn", F32, colsum=True, name=f"in_forget_dw_{tag}")
    dbm, dbf = dbm[0], dbf[0]
    g["w_main"], g["w_f"] = dwm, dwf
    g["b_in"] = jnp.concatenate([dbm[:f_off], dbf[:heads], dbm[f_off:]])
    if below is None:
        out, dscale1, dshift1 = _ln_mod_bwd(du, dzf, p["w_f_matrix"], sv["x"], scale1, dr1, alpha, bsz,
                                            name=f"ln_mod1_bwd_{tag}")
    else:
        sv_b, p_b = below
        *out, _, dscale1, dshift1 = _ln_mod_res_bwd(
            du, dr1, scale1, sv_b["x1"], sv_b["ffn"], sv_b["mods"][5], p_b["ln2_g"], p_b["ln2_b"], alpha, bsz,
            narrow=(dzf, p["w_f_matrix"]), name=f"ln_mod1_res2_bwd_{tag}")
    dmod = jnp.concatenate([dshift1, dscale1, dgate1, dshift2, dscale2, dgate2], axis=2)[:, 0, :]
    return out, g, dmod


def _local_step(x, mod, loss_target, full, rep, dims):
    bsz, seq, d = x.shape
    layers = mod.shape[0]
    params = [_layer_weights(full, rep, layer, dims) for layer in range(layers)]
    mods = [_split_mod(mod[layer], d) for layer in range(layers)]
    h = x.reshape(bsz * seq, d)
    u = _ln_mod_fwd(h, mods[0][1], mods[0][0], bsz, name="ln_mod1_l0")
    saved = []
    for layer in range(layers):
        last = layer == layers - 1
        after = loss_target.reshape(bsz * seq, d) if last else (mods[layer + 1][1], mods[layer + 1][0])
        (h, u), sv = _layer_fwd(h, u, mods[layer], params[layer], dims, f"l{layer}", after)
        saved.append(sv)
    dh, sq = h, u
    loss_local = 0.5 * jnp.sum(sq) / d
    top_sv, top_p = saved[-1], params[-1]
    dh = _ln_res_bwd(dh, top_sv["x1"], top_sv["ffn"], top_sv["mods"][5], top_p["ln2_g"], dims["alpha"], bsz,
                     name=f"ln_res2_bwd_l{layers - 1}")[:5]
    grads, dmods = [None] * layers, [None] * layers
    for layer in reversed(range(layers)):
        below = (saved[layer - 1], params[layer - 1]) if layer > 0 else None
        dh, grads[layer], dmods[layer] = _layer_bwd(dh, params[layer], saved[layer], dims, f"l{layer}", below)
    per_layer = ("w_main", "w_f")
    stacked = {wname: [grads[layer][wname] for layer in range(layers)] if wname in per_layer
               else jnp.stack([grads[layer][wname] for layer in range(layers)]) for wname in grads[0]}
    return loss_local, dh.reshape(bsz, seq, d), stacked, jnp.stack(dmods)


def _pad_rows(a):
    extra = -a.shape[-2] % (2 * SUBLANES)
    if extra == 0:
        return a
    return jnp.pad(a, [(0, 0)] * (a.ndim - 2) + [(0, extra), (0, 0)])


def _w_in_pieces(n, f_off, heads):
    n_in = N_CHIPS * n
    segments = [(0, f_off, "main", 0), (f_off, f_off + heads, "f", 0), (f_off + heads, n_in, "main", f_off)]
    pieces = []
    for chip in range(N_CHIPS):
        lo, hi = chip * n, (chip + 1) * n
        for a, b, target, t0 in segments:
            s, e = max(lo, a), min(hi, b)
            if s < e:
                pieces.append((chip, s - lo, e - lo, target, t0 + s - a))
    return pieces


def _w_in_from_slabs(slabs, f_off, heads, *, name):
    layers, _, k, n = slabs.shape
    tr = _pick(k, (256, 128, 64, 32, 16))
    n_main = N_CHIPS * n - heads
    pieces = _w_in_pieces(n, f_off, heads)

    def body(s_ref, m_ref, f_ref):
        f_ref[...] = jnp.zeros_like(f_ref)
        for chip in range(N_CHIPS):
            slab = s_ref[0, chip].astype(F32)
            for pc, s0, s1, target, t0 in pieces:
                if pc == chip:
                    out = m_ref if target == "main" else f_ref
                    out[0, :, t0:t0 + s1 - s0] = slab[:, s0:s1].astype(out.dtype)

    return _call(body, name=name, grid=(layers, k // tr),
                 in_specs=[pl.BlockSpec((1, N_CHIPS, tr, n), lambda layer, i: (layer, 0, i, 0))],
                 out_specs=[pl.BlockSpec((1, tr, n_main), lambda layer, i: (layer, i, 0)),
                            pl.BlockSpec((1, tr, LANES), lambda layer, i: (layer, i, 0))],
                 out_shape=[_sds((layers, k, n_main), slabs.dtype), _sds((layers, k, LANES), slabs.dtype)],
                 sem=("parallel", "parallel"))(slabs)


def _w_in_to_slabs(d_main, d_f, n, f_off, heads, *, name):
    layers = len(d_main)
    k = d_main[0].shape[0]
    tr = _pick(k, (128, 64, 32, 16, 8))
    pieces = _w_in_pieces(n, f_off, heads)

    def body(*refs):
        m_refs, f_refs, o_ref = refs[:layers], refs[layers:2 * layers], refs[2 * layers]
        for layer in range(layers):
            for chip, s0, s1, target, t0 in pieces:
                src = m_refs[layer] if target == "main" else f_refs[layer]
                o_ref[layer, chip, :, s0:s1] = src[:, t0:t0 + s1 - s0]

    return _call(body, name=name, grid=(k // tr,),
                 in_specs=[pl.BlockSpec((tr, d_main[0].shape[1]), lambda i: (i, 0))] * layers
                 + [pl.BlockSpec((tr, LANES), lambda i: (i, 0))] * layers,
                 out_specs=pl.BlockSpec((layers, N_CHIPS, tr, n), lambda i: (0, 0, i, 0)),
                 out_shape=_sds((layers, N_CHIPS, k, n), F32), sem=("parallel",))(*d_main, *d_f)


def kernel(x, c, w_ada, b_ada, w_in, b_in, conv_a_w, conv_a_b, ln_conv_g, ln_conv_b, w_conv_proj, w_attn_proj, w_mix_out, b_mix_out, ln1_g, ln1_b, w_ffn_up, ffn_conv_w, ffn_conv_b, w_ffn_down, ln2_g, ln2_b, loss_target, m_w_ada, m_b_ada, m_w_in, m_b_in, m_conv_a_w, m_conv_a_b, m_ln_conv_g, m_ln_conv_b, m_w_conv_proj, m_w_attn_proj, m_w_mix_out, m_b_mix_out, m_ln1_g, m_ln1_b, m_w_ffn_up, m_ffn_conv_w, m_ffn_conv_b, m_w_ffn_down, m_ln2_g, m_ln2_b, v_w_ada, v_b_ada, v_w_in, v_b_in, v_conv_a_w, v_conv_a_b, v_ln_conv_g, v_ln_conv_b, v_w_conv_proj, v_w_attn_proj, v_w_mix_out, v_b_mix_out, v_ln1_g, v_ln1_b, v_w_ffn_up, v_ffn_conv_w, v_ffn_conv_b, v_w_ffn_down, v_ln2_g, v_ln2_b):
    weights = dict(zip(WEIGHTS, (w_ada, b_ada, w_in, b_in, conv_a_w, conv_a_b, ln_conv_g, ln_conv_b, w_conv_proj,
                                 w_attn_proj, w_mix_out, b_mix_out, ln1_g, ln1_b, w_ffn_up, ffn_conv_w, ffn_conv_b,
                                 w_ffn_down, ln2_g, ln2_b)))
    mom1 = dict(zip(WEIGHTS, (m_w_ada, m_b_ada, m_w_in, m_b_in, m_conv_a_w, m_conv_a_b, m_ln_conv_g, m_ln_conv_b,
                              m_w_conv_proj, m_w_attn_proj, m_w_mix_out, m_b_mix_out, m_ln1_g, m_ln1_b, m_w_ffn_up,
                              m_ffn_conv_w, m_ffn_conv_b, m_w_ffn_down, m_ln2_g, m_ln2_b)))
    mom2 = dict(zip(WEIGHTS, (v_w_ada, v_b_ada, v_w_in, v_b_in, v_conv_a_w, v_conv_a_b, v_ln_conv_g, v_ln_conv_b,
                              v_w_conv_proj, v_w_attn_proj, v_w_mix_out, v_b_mix_out, v_ln1_g, v_ln1_b, v_w_ffn_up,
                              v_ffn_conv_w, v_ffn_conv_b, v_w_ffn_down, v_ln2_g, v_ln2_b)))
    bsz, seq, d = x.shape
    layers = w_ada.shape[0]
    ch = conv_a_w.shape[2] * N_CHIPS
    width = w_attn_proj.shape[1]
    heads = width // HEAD_DIM
    d_ff = w_ffn_down.shape[1] * N_CHIPS
    dims = dict(bsz=bsz, d=d, ch=ch, heads=heads, alpha=(2.0 * layers) ** 0.25, f_off=2 * ch + 3 * width,
                ga_off=2 * ch + 3 * width, tcf=_pick(d_ff, (256, 128)))
    chip = 2 * lax.axis_index("x") + lax.axis_index("y")
    device = 2 * chip + lax.axis_index("c")
    ada_cols = w_ada.shape[2]

    c_act = _silu_rows(_all_gather8(c, name="gather_c").reshape(N_DEVICES * bsz, d), name="silu_c")
    b_ada_mine = lax.dynamic_slice_in_dim(b_ada, chip * ada_cols, ada_cols, axis=1)
    mod_cols = jnp.stack([_matmul(c_act, (w_ada, layer), "nn", F32, bias=b_ada_mine[layer][None], name=f"ada_l{layer}")
                          for layer in range(layers)])
    mod_all = _all_gather8(mod_cols, name="gather_mod")
    mod_all = jnp.concatenate([mod_all[2 * k] for k in range(N_CHIPS)], axis=-1)
    mod = lax.dynamic_slice_in_dim(mod_all, device * bsz, bsz, axis=1)

    shards = [_pad_rows(weights[wname].astype(BF16) if as_bf16 else weights[wname]) for wname, _, as_bf16 in GATHERED]
    modes = [mode for _, mode, _ in GATHERED]
    whole = _gather_weights(shards, modes, name="gather_weights")
    full = {wname: w[:, :weights[wname].shape[1]] if mode == "cols" else w
            for (wname, mode, _), w in zip(GATHERED, whole)}
    full["w_main"], full["w_f"] = _w_in_from_slabs(full.pop("w_in"), dims["f_off"], heads, name="w_in_from_slabs")
    rep = {wname: weights[wname] for wname in REPLICATED}

    loss_local, grad_x, grads, dmod = _local_step(x, mod, loss_target, full, rep, dims)
    loss = lax.psum(loss_local, ("x", "y", "c"))

    grads["w_in"] = _w_in_to_slabs(grads.pop("w_main"), grads.pop("w_f"), w_in.shape[2], dims["f_off"], heads,
                                   name="w_in_to_slabs")
    shard_shapes = [s.shape for s in shards]
    reduced = _reduce_scatter([_pad_rows(grads[wname]) for wname, _, _ in GATHERED], modes, shard_shapes)
    grad = {wname: r[:, :weights[wname].shape[1]] for (wname, _, _), r in zip(GATHERED, reduced)}

    small = jnp.concatenate([dmod.reshape(-1)] + [grads[wname].reshape(-1) for wname in REPLICATED])
    n_small = small.shape[0]
    rows = -(-n_small // (SUBLANES * LANES)) * SUBLANES
    small = jnp.pad(small, (0, rows * LANES - n_small)).reshape(rows, LANES)
    gathered = _all_gather8(small, name="gather_small")
    n_dmod = dmod.size
    dmod_all = gathered.reshape(N_DEVICES, -1)[:, :n_dmod].reshape(N_DEVICES, layers, bsz, 6 * d)
    dmod_all = jnp.transpose(dmod_all, (1, 0, 2, 3)).reshape(layers, N_DEVICES * bsz, 6 * d)
    summed = _sum_slots(gathered, name="sum_small").reshape(-1)
    off = n_dmod
    for wname in REPLICATED:
        n = weights[wname].size
        grad[wname] = summed[off:off + n].reshape(weights[wname].shape)
        off += n
    dmod_mine = lax.dynamic_slice_in_dim(dmod_all, chip * ada_cols, ada_cols, axis=2)
    grad["w_ada"] = jnp.stack([_matmul(c_act, dmod_mine[layer], "tn", F32, name=f"ada_dw_l{layer}")
                               for layer in range(layers)])
    grad["b_ada"] = jnp.stack([_colsum(dmod_all[layer], name=f"ada_db_l{layer}")[0] for layer in range(layers)])

    delta, new_m, new_v = {}, {}, {}
    for wname in WEIGHTS:
        delta[wname], new_m[wname], new_v[wname] = _adamw(weights[wname], grad[wname], mom1[wname], mom2[wname],
                                                          name=f"adamw_{wname}")
    return (loss, grad_x, *[grad[wname] for wname in WEIGHTS], *[delta[wname] for wname in WEIGHTS],
            *[new_m[wname] for wname in WEIGHTS], *[new_v[wname] for wname in WEIGHTS])
```

```python
import math

import jax
import jax.numpy as jnp
from jax import lax
from jax.experimental import pallas as pl
from jax.experimental.pallas import tpu as pltpu

F32 = jnp.float32
BF16 = jnp.bfloat16
MESH = pl.DeviceIdType.MESH

LN_EPS = 1e-5
HEAD_DIM = 64
ATTN_SCALE = HEAD_DIM ** -0.5
NEG = -1e30
FFN_PAD = 8
LANES = 128
SUBLANES = 8
ROW_CHUNK = 256
ATTN_BLOCK = 256
ATTN_UNROLL = 4
N_CHIPS = 4
N_DEVICES = 8
VMEM_LIMIT = 56 * 1024 * 1024

ADAM_LR = 0.001
ADAM_B1 = 0.9
ADAM_B2 = 0.999
ADAM_EPS = 1e-08
ADAM_WD = 0.01
ADAM_STEP = 10

GATHERED = (("w_in", "slab", True), ("conv_a_w", "cols", False), ("w_conv_proj", "cols", True),
            ("w_attn_proj", "cols", True), ("w_mix_out", "rows", True), ("w_ffn_up", "cols", True),
            ("ffn_conv_w", "cols", False), ("w_ffn_down", "rows", True))
REPLICATED = ("b_in", "conv_a_b", "ln_conv_g", "ln_conv_b", "b_mix_out", "ln1_g", "ln1_b",
              "ffn_conv_b", "ln2_g", "ln2_b")
WEIGHTS = ("w_ada", "b_ada", "w_in", "b_in", "conv_a_w", "conv_a_b", "ln_conv_g", "ln_conv_b",
           "w_conv_proj", "w_attn_proj", "w_mix_out", "b_mix_out", "ln1_g", "ln1_b", "w_ffn_up",
           "ffn_conv_w", "ffn_conv_b", "w_ffn_down", "ln2_g", "ln2_b")


def _pick(n, cands):
    for cand in cands:
        if n % cand == 0:
            return cand
    return n


def _call(body, *, name, grid, in_specs, out_specs, out_shape, scratch=(), sem=None):
    return pl.pallas_call(
        body, name=name, grid=grid, in_specs=in_specs, out_specs=out_specs, out_shape=out_shape,
        scratch_shapes=list(scratch),
        compiler_params=pltpu.CompilerParams(dimension_semantics=sem, vmem_limit_bytes=VMEM_LIMIT))


def _sds(shape, dtype):
    return jax.ShapeDtypeStruct(tuple(shape), dtype)


def _chunked(rows, fn):
    chunk = min(ROW_CHUNK, rows)
    if rows == chunk:
        fn(pl.ds(0, rows))
        return

    def step(i, carry):
        fn(pl.ds(pl.multiple_of(i * chunk, chunk), chunk))
        return carry

    lax.fori_loop(0, rows // chunk, step, 0)


def _matmul(a, b, mode, out_dtype, *, bias=None, add=None, colsum=False, b_k_first=0, out_cols=None, into=None,
            name):
    a, a_layer = a if isinstance(a, tuple) else (a, None)
    b, b_layer = b if isinstance(b, tuple) else (b, None)
    if mode == "nn":
        (m, k), (_, n) = a.shape[-2:], b.shape[-2:]
    elif mode == "nt":
        (m, k), (n, _) = a.shape[-2:], b.shape[-2:]
    else:
        (k, m), (_, n) = a.shape[-2:], b.shape[-2:]
    tn = _pick(n, (1536, 1408, 1024, 512, 256, 128))
    tk = k if k <= 1536 else _pick(k, (1024, 1536, 1408, 512, 256, 128))
    nk = k // tk
    tall = (2048,) if nk == 1 and mode != "tn" and add is None else ()
    tm = _pick(m, tall + (1024, 1408, 512, 256, 128))
    assert b_k_first % tk == 0 and (b_k_first == 0 or mode == "nt")
    k_blk0 = b_k_first // tk

    def spec(layer, shape, index):
        if layer is None:
            return pl.BlockSpec(shape, index)
        return pl.BlockSpec((None,) + shape, lambda i, j, kk: (layer,) + index(i, j, kk))

    if mode == "nn":
        a_spec = spec(a_layer, (tm, tk), lambda i, j, kk: (i, kk))
        b_spec = spec(b_layer, (tk, tn), lambda i, j, kk: (kk, j))
        dims = (((1,), (0,)), ((), ()))
    elif mode == "nt":
        a_spec = spec(a_layer, (tm, tk), lambda i, j, kk: (i, kk))
        b_spec = spec(b_layer, (tn, tk), lambda i, j, kk: (j, k_blk0 + kk))
        dims = (((1,), (1,)), ((), ()))
    else:
        a_spec = spec(a_layer, (tk, tm), lambda i, j, kk: (kk, i))
        b_spec = spec(b_layer, (tk, tn), lambda i, j, kk: (kk, j))
        dims = (((0,), (0,)), ((), ()))
    in_specs = [a_spec, b_spec]
    operands = [a, b]
    if bias is not None:
        in_specs.append(pl.BlockSpec((1, tn), lambda i, j, kk: (0, j)))
        operands.append(bias)
    if add is not None:
        in_specs.append(pl.BlockSpec((tm, tn), lambda i, j, kk: (i, j)))
        operands.append(add)

    def body(a_ref, b_ref, *rest):
        rest = list(rest)
        bias_ref = rest.pop(0) if bias is not None else None
        add_ref = rest.pop(0) if add is not None else None
        if into is not None:
            rest.pop(0)
        o_ref = rest.pop(0)
        prod = lax.dot_general(a_ref[...].astype(BF16), b_ref[...].astype(BF16), dims,
                               preferred_element_type=F32)
        if colsum:
            cs_ref = rest.pop(0)
            part = jnp.sum(b_ref[...].astype(F32), axis=0, keepdims=True)

            @pl.when(pl.program_id(2) == 0)
            def _():
                cs_ref[...] = part

            @pl.when(pl.program_id(2) > 0)
            def _():
                cs_ref[...] += part

        def finish(r):
            if bias_ref is not None:
                r = r + bias_ref[...]
            if add_ref is not None:
                r = r + add_ref[...]
            o_ref[...] = r.astype(o_ref.dtype)

        if nk == 1:
            finish(prod)
            return
        acc_ref = rest.pop(0)
        kk = pl.program_id(2)

        @pl.when(kk == 0)
        def _():
            acc_ref[...] = prod

        @pl.when(kk > 0)
        def _():
            acc_ref[...] += prod

        @pl.when(kk == nk - 1)
        def _():
            finish(acc_ref[...])

    col_blk0 = 0
    aliases = {}
    out_shape = _sds((m, n if out_cols is None else out_cols), out_dtype)
    if into is not None:
        wide, col_first = into
        assert col_first % tn == 0 and not colsum
        col_blk0 = col_first // tn
        out_shape = _sds(wide.shape, wide.dtype)
        aliases = {len(operands): 0}
        in_specs.append(pl.BlockSpec(memory_space=pl.ANY))
        operands.append(wide)
    out_specs = pl.BlockSpec((tm, tn), lambda i, j, kk: (i, col_blk0 + j))
    if colsum:
        assert mode == "tn" and m == tm
        out_specs = [out_specs, pl.BlockSpec((1, tn), lambda i, j, kk: (0, j))]
        out_shape = [out_shape, _sds((1, n), F32)]
    return pl.pallas_call(
        body, name=name, grid=(m // tm, n // tn, nk), in_specs=in_specs, out_specs=out_specs, out_shape=out_shape,
        scratch_shapes=[pltpu.VMEM((tm, tn), F32)] if nk > 1 else [], input_output_aliases=aliases,
        compiler_params=pltpu.CompilerParams(dimension_semantics=("parallel", "parallel", "arbitrary"),
                                             vmem_limit_bytes=VMEM_LIMIT))(*operands)


def _colsum(x, *, name):
    rows, n = x.shape
    tr = _pick(rows, (1024, 512, 256, 128))
    tn = _pick(n, (512, 256, 128))

    def body(x_ref, o_ref):
        @pl.when(pl.program_id(1) == 0)
        def _():
            o_ref[...] = jnp.zeros_like(o_ref)

        o_ref[...] += jnp.sum(x_ref[...].astype(F32), axis=0, keepdims=True)

    return _call(body, name=name, grid=(n // tn, rows // tr),
                 in_specs=[pl.BlockSpec((tr, tn), lambda j, i: (i, j))],
                 out_specs=pl.BlockSpec((1, tn), lambda j, i: (0, j)),
                 out_shape=_sds((1, n), F32), sem=("parallel", "arbitrary"))(x)


def _ln_stats(x):
    mu = jnp.mean(x, axis=-1, keepdims=True)
    xc = x - mu
    var = jnp.mean(xc * xc, axis=-1, keepdims=True)
    rstd = lax.rsqrt(var + LN_EPS)
    return xc * rstd, rstd


def _ln_bwd(dn, n, rstd):
    return rstd * (dn - jnp.mean(dn, axis=-1, keepdims=True) - n * jnp.mean(dn * n, axis=-1, keepdims=True))


def _seq_tiles(t, bsz, cands=(1024, 512, 256, 128, 64, 32, 16, 8)):
    s = t // bsz
    ts = _pick(s, cands)
    return s, ts, s // ts


def _ln_mod_fwd(x, scale, shift, bsz, *, name):
    t, d = x.shape
    _, ts, ns = _seq_tiles(t, bsz)

    def body(x_ref, sc_ref, sh_ref, u_ref):
        one_scale = 1.0 + sc_ref[0]
        shift_v = sh_ref[0]

        def piece(rows):
            n, _ = _ln_stats(x_ref[rows, :])
            u_ref[rows, :] = (n * one_scale + shift_v).astype(u_ref.dtype)

        _chunked(ts, piece)

    row = pl.BlockSpec((ts, d), lambda b, i: (b * ns + i, 0))
    per = pl.BlockSpec((1, 1, d), lambda b, i: (b, 0, 0))
    return _call(body, name=name, grid=(bsz, ns), in_specs=[row, per, per], out_specs=row,
                 out_shape=_sds((t, d), BF16), sem=("parallel", "parallel"))(x, scale, shift)


def _ln_mod_bwd(du, dz_narrow, w_narrow, x, scale, dr, alpha, bsz, *, name):
    t, d = x.shape
    _, ts, ns = _seq_tiles(t, bsz)
    lanes = dz_narrow.shape[1]

    def body(du_ref, dzn_ref, wn_ref, x_ref, sc_ref, dr_ref, dx_ref, dsc_ref, dsh_ref):
        @pl.when(pl.program_id(1) == 0)
        def _():
            dsc_ref[...] = jnp.zeros_like(dsc_ref)
            dsh_ref[...] = jnp.zeros_like(dsh_ref)

        one_scale = 1.0 + sc_ref[0]
        w_n = wn_ref[...].astype(BF16)

        def piece(rows):
            du_v = du_ref[rows, :] + lax.dot_general(dzn_ref[rows, :].astype(BF16), w_n, (((1,), (1,)), ((), ())),
                                                     preferred_element_type=F32)
            n, rstd = _ln_stats(x_ref[rows, :])
            dsc_ref[0] += jnp.sum(du_v * n, axis=0, keepdims=True)
            dsh_ref[0] += jnp.sum(du_v, axis=0, keepdims=True)
            dx_ref[rows, :] = alpha * dr_ref[rows, :] + _ln_bwd(du_v * one_scale, n, rstd)

        _chunked(ts, piece)

    row = pl.BlockSpec((ts, d), lambda b, i: (b * ns + i, 0))
    per = pl.BlockSpec((1, 1, d), lambda b, i: (b, 0, 0))
    return _call(body, name=name, grid=(bsz, ns),
                 in_specs=[row, pl.BlockSpec((ts, lanes), lambda b, i: (b * ns + i, 0)),
                           pl.BlockSpec((d, lanes), lambda b, i: (0, 0)), row, per, row],
                 out_specs=[row, per, per],
                 out_shape=[_sds((t, d), F32), _sds((bsz, 1, d), F32), _sds((bsz, 1, d), F32)],
                 sem=("parallel", "arbitrary"))(du, dz_narrow, w_narrow, x, scale, dr)


def _ln_res_bwd(do, x, y, gate, g, alpha, bsz, *, name):
    t, d = x.shape
    _, ts, ns = _seq_tiles(t, bsz)

    def body(do_ref, x_ref, y_ref, gt_ref, g_ref, dr_ref, dy_ref, dgt_ref, dg_ref, db_ref, dys_ref):
        first_tile = pl.program_id(1) == 0

        @pl.when(first_tile)
        def _():
            dgt_ref[...] = jnp.zeros_like(dgt_ref)

        @pl.when(jnp.logical_and(first_tile, pl.program_id(0) == 0))
        def _():
            dg_ref[...] = jnp.zeros_like(dg_ref)
            db_ref[...] = jnp.zeros_like(db_ref)
            dys_ref[...] = jnp.zeros_like(dys_ref)

        one_gate = 1.0 + gt_ref[0]

        def piece(rows):
            do_v = do_ref[rows, :]
            y_v = y_ref[rows, :]
            n, rstd = _ln_stats(alpha * x_ref[rows, :] + one_gate * y_v)
            dg_ref[...] += jnp.sum(do_v * n, axis=0, keepdims=True)
            db_ref[...] += jnp.sum(do_v, axis=0, keepdims=True)
            dr = _ln_bwd(do_v * g_ref[...], n, rstd)
            dr_ref[rows, :] = dr
            dy = one_gate * dr
            dy_ref[rows, :] = dy.astype(dy_ref.dtype)
            dys_ref[...] += jnp.sum(dy, axis=0, keepdims=True)
            dgt_ref[0] += jnp.sum(dr * y_v, axis=0, keepdims=True)

        _chunked(ts, piece)

    row = pl.BlockSpec((ts, d), lambda bb, i: (bb * ns + i, 0))
    per = pl.BlockSpec((1, 1, d), lambda bb, i: (bb, 0, 0))
    vec = pl.BlockSpec((1, d), lambda bb, i: (0, 0))
    return _call(body, name=name, grid=(bsz, ns), in_specs=[row, row, row, per, vec],
                 out_specs=[row, row, per, vec, vec, vec],
                 out_shape=[_sds((t, d), F32), _sds((t, d), BF16), _sds((bsz, 1, d), F32),
                            _sds((1, d), F32), _sds((1, d), F32), _sds((1, d), F32)],
                 sem=("arbitrary", "arbitrary"))(do, x, y, gate, g)


def _ln_res_mod_fwd(x, y, gate, g, b, scale, shift, alpha, bsz, *, name):
    t, d = x.shape
    _, ts, ns = _seq_tiles(t, bsz)

    def body(x_ref, y_ref, gt_ref, g_ref, b_ref, sc_ref, sh_ref, o_ref, u_ref):
        one_gate = 1.0 + gt_ref[0]
        one_scale = 1.0 + sc_ref[0]
        shift_v = sh_ref[0]

        def piece(rows):
            n, _ = _ln_stats(alpha * x_ref[rows, :] + one_gate * y_ref[rows, :])
            x1 = n * g_ref[...] + b_ref[...]
            o_ref[rows, :] = x1
            n1, _ = _ln_stats(x1)
            u_ref[rows, :] = (n1 * one_scale + shift_v).astype(u_ref.dtype)

        _chunked(ts, piece)

    row = pl.BlockSpec((ts, d), lambda bb, i: (bb * ns + i, 0))
    per = pl.BlockSpec((1, 1, d), lambda bb, i: (bb, 0, 0))
    vec = pl.BlockSpec((1, d), lambda bb, i: (0, 0))
    return _call(body, name=name, grid=(bsz, ns), in_specs=[row, row, per, vec, vec, per, per],
                 out_specs=[row, row], out_shape=[_sds((t, d), F32), _sds((t, d), BF16)],
                 sem=("parallel", "parallel"))(x, y, gate, g, b, scale, shift)


def _ln_mod_res_bwd(du, dr_up, scale, x, y, gate, g, b, alpha, bsz, *, narrow=None, name):
    t, d = x.shape
    _, ts, ns = _seq_tiles(t, bsz, (512, 256, 128, 64, 32, 16, 8))

    def body(du_ref, *refs):
        refs = list(refs)
        dzn_ref, wn_ref = (refs.pop(0), refs.pop(0)) if narrow is not None else (None, None)
        (up_ref, sc_ref, x_ref, y_ref, gt_ref, g_ref, b_ref,
         dr_ref, dy_ref, dgt_ref, dg_ref, db_ref, dys_ref, dsc_ref, dsh_ref) = refs
        first_tile = pl.program_id(1) == 0

        @pl.when(first_tile)
        def _():
            for ref in (dgt_ref, dsc_ref, dsh_ref):
                ref[...] = jnp.zeros_like(ref)

        @pl.when(jnp.logical_and(first_tile, pl.program_id(0) == 0))
        def _():
            for ref in (dg_ref, db_ref, dys_ref):
                ref[...] = jnp.zeros_like(ref)

        one_gate = 1.0 + gt_ref[0]
        one_scale = 1.0 + sc_ref[0]

        def piece(rows):
            y_v = y_ref[rows, :]
            n, rstd = _ln_stats(alpha * x_ref[rows, :] + one_gate * y_v)
            n1, rstd1 = _ln_stats(n * g_ref[...] + b_ref[...])
            du_v = du_ref[rows, :]
            if narrow is not None:
                du_v = du_v + lax.dot_general(dzn_ref[rows, :].astype(BF16), wn_ref[...].astype(BF16),
                                              (((1,), (1,)), ((), ())), preferred_element_type=F32)
            dsc_ref[0] += jnp.sum(du_v * n1, axis=0, keepdims=True)
            dsh_ref[0] += jnp.sum(du_v, axis=0, keepdims=True)
            dx1 = alpha * up_ref[rows, :] + _ln_bwd(du_v * one_scale, n1, rstd1)
            dg_ref[...] += jnp.sum(dx1 * n, axis=0, keepdims=True)
            db_ref[...] += jnp.sum(dx1, axis=0, keepdims=True)
            dr = _ln_bwd(dx1 * g_ref[...], n, rstd)
            dr_ref[rows, :] = dr
            dy = one_gate * dr
            dy_ref[rows, :] = dy.astype(dy_ref.dtype)
            dys_ref[...] += jnp.sum(dy, axis=0, keepdims=True)
            dgt_ref[0] += jnp.sum(dr * y_v, axis=0, keepdims=True)

        _chunked(ts, piece)

    row = pl.BlockSpec((ts, d), lambda bb, i: (bb * ns + i, 0))
    per = pl.BlockSpec((1, 1, d), lambda bb, i: (bb, 0, 0))
    vec = pl.BlockSpec((1, d), lambda bb, i: (0, 0))
    extra_specs, extra = [], []
    if narrow is not None:
        lanes = narrow[0].shape[1]
        extra_specs = [pl.BlockSpec((ts, lanes), lambda bb, i: (bb * ns + i, 0)), pl.BlockSpec((d, lanes), lambda bb, i: (0, 0))]
        extra = list(narrow)
    return _call(body, name=name, grid=(bsz, ns), in_specs=[row] + extra_specs + [row, per, row, row, per, vec, vec],
                 out_specs=[row, row, per, vec, vec, vec, per, per],
                 out_shape=[_sds((t, d), F32), _sds((t, d), BF16), _sds((bsz, 1, d), F32), _sds((1, d), F32),
                            _sds((1, d), F32), _sds((1, d), F32), _sds((bsz, 1, d), F32), _sds((bsz, 1, d), F32)],
                 sem=("arbitrary", "arbitrary"))(du, *extra, dr_up, scale, x, y, gate, g, b)


def _ln_res_loss(x, y, gate, g, b, target, alpha, bsz, *, name):
    t, d = x.shape
    _, ts, ns = _seq_tiles(t, bsz)

    def body(x_ref, y_ref, gt_ref, g_ref, b_ref, t_ref, dy_ref, s_ref):
        @pl.when(jnp.logical_and(pl.program_id(0) == 0, pl.program_id(1) == 0))
        def _():
            s_ref[...] = jnp.zeros_like(s_ref)

        one_gate = 1.0 + gt_ref[0]

        def piece(rows):
            n, _ = _ln_stats(alpha * x_ref[rows, :] + one_gate * y_ref[rows, :])
            e = (n * g_ref[...] + b_ref[...]) - t_ref[rows, :]
            dy_ref[rows, :] = e * (1.0 / d)
            s_ref[...] += jnp.sum(e * e, axis=0, keepdims=True)

        _chunked(ts, piece)

    row = pl.BlockSpec((ts, d), lambda bb, i: (bb * ns + i, 0))
    per = pl.BlockSpec((1, 1, d), lambda bb, i: (bb, 0, 0))
    vec = pl.BlockSpec((1, d), lambda bb, i: (0, 0))
    return _call(body, name=name, grid=(bsz, ns), in_specs=[row, row, per, vec, vec, row], out_specs=[row, vec],
                 out_shape=[_sds((t, d), F32), _sds((1, d), F32)],
                 sem=("arbitrary", "arbitrary"))(x, y, gate, g, b, target)


def _sigmoid(v):
    return 1.0 / (1.0 + jnp.exp(-v))


def _silu_rows(c, *, name):
    rows, d = c.shape

    def body(c_ref, o_ref):
        v = c_ref[...]
        o_ref[...] = (v * _sigmoid(v)).astype(o_ref.dtype)

    full = pl.BlockSpec((rows, d), lambda i: (0, 0))
    return _call(body, name=name, grid=(1,), in_specs=[full], out_specs=full,
                 out_shape=_sds((rows, d), BF16), sem=("arbitrary",))(c)


def _gate_cols(d, ga_off):
    tc = _pick(math.gcd(d, ga_off), (512, 256, 128))
    return tc, ga_off // tc, (ga_off + d) // tc


def _gate_merge_fwd(z, ya, yb, ga_off, *, name):
    t, d = ya.shape
    tr = _pick(t, (1024, 512, 256, 128, 64, 32, 16, 8))
    tc, ga_blk, gb_blk = _gate_cols(d, ga_off)

    def body(ga_ref, gb_ref, ya_ref, yb_ref, o_ref):
        def piece(rows):
            o_ref[rows, :] = (_sigmoid(ga_ref[rows, :].astype(F32)) * ya_ref[rows, :].astype(F32)
                              + _sigmoid(gb_ref[rows, :].astype(F32)) * yb_ref[rows, :].astype(F32)
                              ).astype(o_ref.dtype)

        _chunked(tr, piece)

    blk = pl.BlockSpec((tr, tc), lambda i, j: (i, j))
    return _call(body, name=name, grid=(t // tr, d // tc),
                 in_specs=[pl.BlockSpec((tr, tc), lambda i, j: (i, ga_blk + j)),
                           pl.BlockSpec((tr, tc), lambda i, j: (i, gb_blk + j)), blk, blk],
                 out_specs=blk, out_shape=_sds((t, d), BF16), sem=("parallel", "parallel"))(z, z, ya, yb)


def _gate_merge_bwd(z, ya, yb, dm, ga_off, *, name):
    t, d = ya.shape
    tr = _pick(t, (1024, 512, 256, 128, 64, 32, 16, 8))
    tc, ga_blk, gb_blk = _gate_cols(d, ga_off)

    def body(ga_ref, gb_ref, ya_ref, yb_ref, dm_ref, dya_ref, dyb_ref, dga_ref, dgb_ref):
        def piece(rows):
            dm_v = dm_ref[rows, :].astype(F32)
            sa = _sigmoid(ga_ref[rows, :].astype(F32))
            sb = _sigmoid(gb_ref[rows, :].astype(F32))
            dya_ref[rows, :] = (dm_v * sa).astype(dya_ref.dtype)
            dyb_ref[rows, :] = (dm_v * sb).astype(dyb_ref.dtype)
            dga_ref[rows, :] = (dm_v * ya_ref[rows, :].astype(F32) * sa * (1.0 - sa)).astype(dga_ref.dtype)
            dgb_ref[rows, :] = (dm_v * yb_ref[rows, :].astype(F32) * sb * (1.0 - sb)).astype(dgb_ref.dtype)

        _chunked(tr, piece)

    blk = pl.BlockSpec((tr, tc), lambda i, j: (i, j))
    return _call(body, name=name, grid=(t // tr, d // tc),
                 in_specs=[pl.BlockSpec((tr, tc), lambda i, j: (i, ga_blk + j)),
                           pl.BlockSpec((tr, tc), lambda i, j: (i, gb_blk + j)), blk, blk, blk],
                 out_specs=[blk, blk, blk, blk], out_shape=[_sds((t, d), BF16)] * 4,
                 sem=("parallel", "parallel"))(z, z, ya, yb, dm)


CONV_ROWS = 64
CONV_PAD = 32


def _row_shifts(win):
    total = win.shape[0]
    return [win] + [pltpu.roll(win, total - b, axis=0) for b in range(1, SUBLANES)]


def _shifted_rows(copies, shift):
    start = SUBLANES * (shift // SUBLANES)
    return copies[shift % SUBLANES][start:start + CONV_ROWS]


def _fill_glu(z_ref, ext_ref, s, ch):
    ext_ref[pl.ds(0, CONV_PAD), :] = jnp.zeros((CONV_PAD, ch), F32)

    chunk = min(ROW_CHUNK, s)

    def piece(i, carry):
        start = pl.multiple_of(i * chunk, chunk)
        zz = z_ref[pl.ds(start, chunk), :].astype(F32)
        ext_ref[pl.ds(pl.multiple_of(CONV_PAD + start, CONV_PAD), chunk), :] = zz[:, :ch] * _sigmoid(zz[:, ch:])
        return carry

    lax.fori_loop(0, s // chunk, piece, 0)


def _conv_piece(ext_ref, w_ref, cb_ref, base, kw):
    copies = _row_shifts(ext_ref[pl.ds(base, CONV_ROWS + CONV_PAD), :])
    acc = cb_ref[...] + w_ref[pl.ds(0, 1), :] * _shifted_rows(copies, CONV_PAD - (kw - 1))
    for k in range(1, kw):
        acc = acc + w_ref[pl.ds(k, 1), :] * _shifted_rows(copies, CONV_PAD - (kw - 1) + k)
    return acc, copies


def _conv_branch_fwd(z, w, cb, lg, lb, bsz, ch, *, name):
    t = z.shape[0]
    s = t // bsz
    kw = w.shape[0]

    def body(z_ref, w_ref, cb_ref, lg_ref, lb_ref, o_ref, ext_ref):
        _fill_glu(z_ref, ext_ref, s, ch)

        def step(i, carry):
            base = pl.multiple_of(i * CONV_ROWS, CONV_ROWS)
            a1, _ = _conv_piece(ext_ref, w_ref, cb_ref, base, kw)
            n, _ = _ln_stats(a1)
            a2 = n * lg_ref[...] + lb_ref[...]
            o_ref[pl.ds(base, CONV_ROWS), :] = (a2 * _sigmoid(a2)).astype(o_ref.dtype)
            return carry

        lax.fori_loop(0, s // CONV_ROWS, step, 0)

    vec = pl.BlockSpec((1, ch), lambda b: (0, 0))
    return _call(body, name=name, grid=(bsz,),
                 in_specs=[pl.BlockSpec((s, 2 * ch), lambda b: (b, 0)), pl.BlockSpec((kw, ch), lambda b: (0, 0)),
                           vec, vec, vec],
                 out_specs=pl.BlockSpec((s, ch), lambda b: (b, 0)), out_shape=_sds((t, ch), BF16),
                 scratch=[pltpu.VMEM((CONV_PAD + s, ch), F32)], sem=("parallel",))(z, w, cb, lg, lb)


def _conv_branch_bwd(z, da3, w, cb, lg, lb, bsz, ch, *, name):
    t = z.shape[0]
    s = t // bsz
    kw = w.shape[0]
    n_rows = CONV_ROWS + CONV_PAD

    def body(z_ref, d_ref, w_ref, cb_ref, lg_ref, lb_ref, dz_ref, dw_ref, dcb_ref, dlg_ref, dlb_ref,
             ext_ref, da1_ref):
        @pl.when(pl.program_id(0) == 0)
        def _():
            for ref in (dw_ref, dcb_ref, dlg_ref, dlb_ref):
                ref[...] = jnp.zeros_like(ref)

        _fill_glu(z_ref, ext_ref, s, ch)
        da1_ref[pl.ds(s, CONV_PAD), :] = jnp.zeros((CONV_PAD, ch), F32)

        def grad_a1(i, carry):
            dlg, dlb = carry
            base = pl.multiple_of(i * CONV_ROWS, CONV_ROWS)
            a1, _ = _conv_piece(ext_ref, w_ref, cb_ref, base, kw)
            n, rstd = _ln_stats(a1)
            a2 = n * lg_ref[...] + lb_ref[...]
            sg = _sigmoid(a2)
            da2 = d_ref[pl.ds(base, CONV_ROWS), :] * (sg * (1.0 + a2 * (1.0 - sg)))
            da1_ref[pl.ds(base, CONV_ROWS), :] = _ln_bwd(da2 * lg_ref[...], n, rstd)
            return (dlg + jnp.sum(da2 * n, axis=0, keepdims=True), dlb + jnp.sum(da2, axis=0, keepdims=True))

        zero = jnp.zeros((1, ch), F32)
        dlg, dlb = lax.fori_loop(0, s // CONV_ROWS, grad_a1, (zero, zero))
        dlg_ref[...] += dlg
        dlb_ref[...] += dlb

        def grad_z(i, dcb):
            base = pl.multiple_of(i * CONV_ROWS, CONV_ROWS)
            ahead = _row_shifts(da1_ref[pl.ds(base, n_rows), :])
            dyc = ahead[0][:CONV_ROWS]
            da0 = w_ref[pl.ds(kw - 1, 1), :] * dyc
            for k in range(kw - 1):
                da0 = da0 + w_ref[pl.ds(k, 1), :] * _shifted_rows(ahead, kw - 1 - k)
            behind = _row_shifts(ext_ref[pl.ds(base, n_rows), :])
            for k in range(kw):
                dw_ref[pl.ds(k, 1), :] += jnp.sum(dyc * _shifted_rows(behind, CONV_PAD - (kw - 1) + k),
                                                  axis=0, keepdims=True)
            zz = z_ref[pl.ds(base, CONV_ROWS), :].astype(F32)
            sg = _sigmoid(zz[:, ch:])
            dz_ref[pl.ds(base, CONV_ROWS), :ch] = (da0 * sg).astype(dz_ref.dtype)
            dz_ref[pl.ds(base, CONV_ROWS), ch:] = (da0 * zz[:, :ch] * sg * (1.0 - sg)).astype(dz_ref.dtype)
            return dcb + jnp.sum(dyc, axis=0, keepdims=True)

        dcb_ref[...] += lax.fori_loop(0, s // CONV_ROWS, grad_z, zero)

    vec = pl.BlockSpec((1, ch), lambda b: (0, 0))
    taps = pl.BlockSpec((kw, ch), lambda b: (0, 0))
    return _call(body, name=name, grid=(bsz,),
                 in_specs=[pl.BlockSpec((s, 2 * ch), lambda b: (b, 0)), pl.BlockSpec((s, ch), lambda b: (b, 0)),
                           taps, vec, vec, vec],
                 out_specs=[pl.BlockSpec((s, 2 * ch), lambda b: (b, 0)), taps, vec, vec, vec],
                 out_shape=[_sds((t, 2 * ch), BF16), _sds((kw, ch), F32)] + [_sds((1, ch), F32)] * 3,
                 scratch=[pltpu.VMEM((CONV_PAD + s, ch), F32), pltpu.VMEM((s + CONV_PAD, ch), F32)],
                 sem=("arbitrary",))(z, da3, w, cb, lg, lb)


FFN_ROWS = 64


def _gelu_parts(v):
    cdf = 0.5 * (1.0 + lax.erf(v * (2.0 ** -0.5)))
    return cdf, v * cdf


def _ffn_conv_piece(ext_ref, wb_ref, base):
    win = ext_ref[pl.ds(base, FFN_ROWS + FFN_PAD), :]
    acc = wb_ref[pl.ds(3, 1), :] + wb_ref[pl.ds(2, 1), :] * win[FFN_PAD:]
    acc = acc + wb_ref[pl.ds(1, 1), :] * pltpu.roll(win, 1, axis=0)[FFN_PAD:]
    acc = acc + wb_ref[pl.ds(0, 1), :] * pltpu.roll(win, 2, axis=0)[FFN_PAD:]
    return acc


def _ffn_stage(hg_ref, hl_ref, wg_ref, wl_ref, bg_ref, bl_ref, ext_ref, wb_ref, s, tcf):
    ext_ref[pl.ds(0, FFN_PAD), :] = jnp.zeros((FFN_PAD, 2 * tcf), F32)
    ext_ref[pl.ds(FFN_PAD, s), :tcf] = hg_ref[...].astype(F32)
    ext_ref[pl.ds(FFN_PAD, s), tcf:] = hl_ref[...].astype(F32)
    wb_ref[pl.ds(0, 3), :tcf] = wg_ref[...]
    wb_ref[pl.ds(0, 3), tcf:] = wl_ref[...]
    wb_ref[pl.ds(3, 1), :tcf] = bg_ref[...]
    wb_ref[pl.ds(3, 1), tcf:] = bl_ref[...]


def _ffn_specs(s, tcf, n_f, batch_first):
    def spec(rows, shift):
        if batch_first:
            return pl.BlockSpec((rows, tcf), lambda bb, j: (bb if rows == s else 0, shift + j))
        return pl.BlockSpec((rows, tcf), lambda j, bb: (bb if rows == s else 0, shift + j))

    return [spec(s, 0), spec(s, n_f), spec(3, 0), spec(3, n_f), spec(1, 0), spec(1, n_f)]


def _ffn_act_fwd(hp, w, b, bsz, tcf, *, name):
    t, two_f = hp.shape
    s = t // bsz
    n_f = two_f // (2 * tcf)

    def body(hg_ref, hl_ref, wg_ref, wl_ref, bg_ref, bl_ref, f_ref, ext_ref, wb_ref):
        _ffn_stage(hg_ref, hl_ref, wg_ref, wl_ref, bg_ref, bl_ref, ext_ref, wb_ref, s, tcf)

        def step(i, carry):
            base = pl.multiple_of(i * FFN_ROWS, FFN_ROWS)
            hh = _ffn_conv_piece(ext_ref, wb_ref, base)
            _, gelu = _gelu_parts(hh[:, :tcf])
            f_ref[pl.ds(base, FFN_ROWS), :] = (gelu * hh[:, tcf:]).astype(f_ref.dtype)
            return carry

        lax.fori_loop(0, s // FFN_ROWS, step, 0)

    return _call(body, name=name, grid=(bsz, n_f), in_specs=_ffn_specs(s, tcf, n_f, True),
                 out_specs=pl.BlockSpec((s, tcf), lambda bb, j: (bb, j)),
                 out_shape=_sds((t, two_f // 2), BF16),
                 scratch=[pltpu.VMEM((FFN_PAD + s, 2 * tcf), F32), pltpu.VMEM((SUBLANES, 2 * tcf), F32)],
                 sem=("parallel", "parallel"))(hp, hp, w, w, b, b)


def _ffn_act_bwd(hp, df, w, b, bsz, tcf, *, name):
    t, two_f = hp.shape
    s = t // bsz
    f_dim = two_f // 2
    n_f = f_dim // tcf
    gw = 2 * tcf
    n_rows = FFN_ROWS + FFN_PAD

    def body(hg_ref, hl_ref, wg_ref, wl_ref, bg_ref, bl_ref, df_ref,
             dhg_ref, dhl_ref, dwg_ref, dwl_ref, dbg_ref, dbl_ref, ext_ref, wb_ref, dh_ref):
        @pl.when(pl.program_id(1) == 0)
        def _():
            for ref in (dwg_ref, dwl_ref, dbg_ref, dbl_ref):
                ref[...] = jnp.zeros_like(ref)

        _ffn_stage(hg_ref, hl_ref, wg_ref, wl_ref, bg_ref, bl_ref, ext_ref, wb_ref, s, tcf)
        dh_ref[pl.ds(s, FFN_PAD), :] = jnp.zeros((FFN_PAD, gw), F32)

        def grad_h(i, carry):
            base = pl.multiple_of(i * FFN_ROWS, FFN_ROWS)
            hh = _ffn_conv_piece(ext_ref, wb_ref, base)
            hg = hh[:, :tcf]
            d = df_ref[pl.ds(base, FFN_ROWS), :].astype(F32)
            cdf, gelu = _gelu_parts(hg)
            pdf = jnp.exp(-0.5 * hg * hg) * (1.0 / math.sqrt(2.0 * math.pi))
            dh_ref[pl.ds(base, FFN_ROWS), :tcf] = d * hh[:, tcf:] * (cdf + hg * pdf)
            dh_ref[pl.ds(base, FFN_ROWS), tcf:] = d * gelu
            return carry

        lax.fori_loop(0, s // FFN_ROWS, grad_h, 0)

        def grad_x(i, carry):
            dw0, dw1, dw2, dbs = carry
            base = pl.multiple_of(i * FFN_ROWS, FFN_ROWS)
            nxt = dh_ref[pl.ds(base, n_rows), :]
            dyc = nxt[:FFN_ROWS]
            dx = wb_ref[pl.ds(2, 1), :] * dyc
            dx = dx + wb_ref[pl.ds(1, 1), :] * pltpu.roll(nxt, n_rows - 1, axis=0)[:FFN_ROWS]
            dx = dx + wb_ref[pl.ds(0, 1), :] * pltpu.roll(nxt, n_rows - 2, axis=0)[:FFN_ROWS]
            dhg_ref[pl.ds(base, FFN_ROWS), :] = dx[:, :tcf].astype(dhg_ref.dtype)
            dhl_ref[pl.ds(base, FFN_ROWS), :] = dx[:, tcf:].astype(dhl_ref.dtype)
            win = ext_ref[pl.ds(base, n_rows), :]
            dw2 = dw2 + jnp.sum(dyc * win[FFN_PAD:], axis=0, keepdims=True)
            dw1 = dw1 + jnp.sum(dyc * pltpu.roll(win, 1, axis=0)[FFN_PAD:], axis=0, keepdims=True)
            dw0 = dw0 + jnp.sum(dyc * pltpu.roll(win, 2, axis=0)[FFN_PAD:], axis=0, keepdims=True)
            return dw0, dw1, dw2, dbs + jnp.sum(dyc, axis=0, keepdims=True)

        zero = jnp.zeros((1, gw), F32)
        sums = lax.fori_loop(0, s // FFN_ROWS, grad_x, (zero, zero, zero, zero))
        for k in range(3):
            dwg_ref[pl.ds(k, 1), :] += sums[k][:, :tcf]
            dwl_ref[pl.ds(k, 1), :] += sums[k][:, tcf:]
        dbg_ref[...] += sums[3][:, :tcf]
        dbl_ref[...] += sums[3][:, tcf:]

    half = pl.BlockSpec((s, tcf), lambda j, bb: (bb, j))
    taps = pl.BlockSpec((3, tcf), lambda j, bb: (0, j))
    bias = pl.BlockSpec((1, tcf), lambda j, bb: (0, j))
    return _call(body, name=name, grid=(n_f, bsz), in_specs=_ffn_specs(s, tcf, n_f, False) + [half],
                 out_specs=[half, half, taps, taps, bias, bias],
                 out_shape=[_sds((t, f_dim), BF16)] * 2 + [_sds((3, f_dim), F32)] * 2 + [_sds((1, f_dim), F32)] * 2,
                 scratch=[pltpu.VMEM((FFN_PAD + s, gw), F32), pltpu.VMEM((SUBLANES, gw), F32),
                          pltpu.VMEM((s + FFN_PAD, gw), F32)],
                 sem=("parallel", "arbitrary"))(hp, hp, w, w, b, b, df)


def _split3(v):
    hi = v.astype(BF16)
    r = v - hi.astype(F32)
    mid = r.astype(BF16)
    lo = (r - mid.astype(F32)).astype(BF16)
    return hi, mid, lo


def _tri_dot(tri, v):
    out = None
    for part in _split3(v):
        term = jnp.dot(tri, part, preferred_element_type=F32)
        out = term if out is None else out + term
    return out


def _fgate_fwd(zf, bsz, heads, *, name):
    t, lanes = zf.shape
    s, blk, nb = _seq_tiles(t, bsz, (ATTN_BLOCK, 128))

    def body(z_ref, cumt_ref, cumb_ref, carry_ref):
        @pl.when(pl.program_id(1) == 0)
        def _():
            carry_ref[...] = jnp.zeros_like(carry_ref)

        z = z_ref[...]
        lf = jnp.minimum(z, 0.0) - jnp.log1p(jnp.exp(-jnp.abs(z)))
        r = lax.broadcasted_iota(jnp.int32, (blk, blk), 0)
        c = lax.broadcasted_iota(jnp.int32, (blk, blk), 1)
        tri = (r >= c).astype(BF16)
        cum = _tri_dot(tri, lf) + carry_ref[...]
        carry_ref[...] = cum[blk - 1:blk, :]
        cumt_ref[0] = jnp.transpose(cum)[:heads, :]
        for h in range(heads):
            cumb_ref[0, h] = jnp.broadcast_to(cum[:, h:h + 1], (blk, lanes))

    return _call(body, name=name, grid=(bsz, nb),
                 in_specs=[pl.BlockSpec((blk, lanes), lambda b, i: (b * nb + i, 0))],
                 out_specs=[pl.BlockSpec((1, heads, blk), lambda b, i: (b, 0, i)),
                            pl.BlockSpec((1, heads, blk, lanes), lambda b, i: (b, 0, i, 0))],
                 out_shape=[_sds((bsz, heads, s), F32), _sds((bsz, heads, s, lanes), F32)],
                 scratch=[pltpu.VMEM((1, lanes), F32)], sem=("parallel", "arbitrary"))(zf)


def _fgate_bwd(dcum, zf, bsz, *, name):
    t, lanes = zf.shape
    pairs = dcum.shape[1]
    s, blk, nb = _seq_tiles(t, bsz, (ATTN_BLOCK, 128))

    def body(d_ref, z_ref, o_ref, carry_ref):
        @pl.when(pl.program_id(1) == 0)
        def _():
            carry_ref[...] = jnp.zeros_like(carry_ref)

        dcol = d_ref[0, 0]
        for p in range(1, pairs):
            dcol = dcol + d_ref[0, p]
        r = lax.broadcasted_iota(jnp.int32, (blk, blk), 0)
        c = lax.broadcasted_iota(jnp.int32, (blk, blk), 1)
        tri = (c >= r).astype(BF16)
        suf = _tri_dot(tri, dcol) + carry_ref[...]
        carry_ref[...] = suf[0:1, :]
        o_ref[...] = suf * _sigmoid(-z_ref[...])

    return _call(body, name=name, grid=(bsz, nb),
                 in_specs=[pl.BlockSpec((1, pairs, blk, lanes), lambda b, i: (b, 0, nb - 1 - i, 0)),
                           pl.BlockSpec((blk, lanes), lambda b, i: (b * nb + nb - 1 - i, 0))],
                 out_specs=pl.BlockSpec((blk, lanes), lambda b, i: (b * nb + nb - 1 - i, 0)),
                 out_shape=_sds((t, lanes), F32), scratch=[pltpu.VMEM((1, lanes), F32)],
                 sem=("parallel", "arbitrary"))(dcum, zf)


def _to_features_major(z, col_off, width, n, *, name):
    t = z.shape[0]
    tr = _pick(t, (512, 256, 128))
    first = col_off // width

    def body(*refs):
        o_ref = refs[n]
        for g in range(n):
            o_ref[pl.ds(g * width, width), :] = jnp.transpose(refs[g][...].astype(F32)).astype(o_ref.dtype)

    return _call(body, name=name, grid=(t // tr,),
                 in_specs=[pl.BlockSpec((tr, width), lambda i, g=g: (i, first + g)) for g in range(n)],
                 out_specs=pl.BlockSpec((n * width, tr), lambda i: (0, i)),
                 out_shape=_sds((n * width, t), BF16), sem=("parallel",))(*([z] * n))


def _to_rows_major(xt, *, name):
    w, t = xt.shape
    tr = _pick(t, (512, 256, 128))

    def body(x_ref, o_ref):
        o_ref[...] = jnp.transpose(x_ref[...]).astype(o_ref.dtype)

    return _call(body, name=name, grid=(t // tr,),
                 in_specs=[pl.BlockSpec((w, tr), lambda i: (0, i))],
                 out_specs=pl.BlockSpec((tr, w), lambda i: (i, 0)),
                 out_shape=_sds((t, w), BF16), sem=("parallel",))(xt)


def _loop_by_twos(lo, hi, body, carry):
    count = hi - lo

    def group(n, first, cr):
        for u in range(n):
            cr = body(first + u, cr)
        return cr

    trips = count // ATTN_UNROLL
    carry = lax.fori_loop(0, trips, lambda t, cr: group(ATTN_UNROLL, lo + ATTN_UNROLL * t, cr), carry)
    rest = count - ATTN_UNROLL * trips
    first = lo + ATTN_UNROLL * trips
    for n in range(ATTN_UNROLL - 1, 0, -1):
        carry = lax.cond(rest == n, lambda cr, n=n: group(n, first, cr), lambda cr: cr, carry)
    return carry


def _head_masks(shape, axis):
    feat = lax.broadcasted_iota(jnp.int32, shape, axis)
    return feat < HEAD_DIM, feat >= HEAD_DIM


def _attn_fwd(z, qkvt, cumt, cumb, bsz, heads, q_off, *, name):
    t = z.shape[0]
    width = heads * HEAD_DIM
    pairs = heads // 2
    s = t // bsz
    blk = ATTN_BLOCK
    nq = s // blk
    k_col = (q_off + width) // LANES
    v_row = 2 * width // LANES
    reps = blk // LANES

    def body(k_ref, qt_ref, vt_ref, cqt_ref, ckb_ref, ot_ref, lse_ref):
        p_id = pl.program_id(1)
        i = pl.program_id(2)
        qt = qt_ref[...]
        masks = _head_masks((LANES, blk), 0)
        qtm = [jnp.where(mk, qt, jnp.zeros_like(qt)) for mk in masks]
        cq = [cqt_ref[0, pl.ds(2 * p_id + hh, 1), :] for hh in range(2)]
        kidx = lax.broadcasted_iota(jnp.int32, (blk, blk), 0)
        qidx = lax.broadcasted_iota(jnp.int32, (blk, blk), 1)

        def block(j, carry, masked):
            off = pl.multiple_of(j * blk, blk)
            kp = k_ref[pl.ds(off, blk), :].astype(BF16)
            vtp = vt_ref[:, pl.ds(off, blk)]
            out = []
            for hh in range(2):
                m, l, acc = carry[hh]
                sc = jnp.dot(kp, qtm[hh], preferred_element_type=F32) * ATTN_SCALE
                ck = ckb_ref[0, hh, pl.ds(off, blk), :]
                sc = (sc + cq[hh]) - jnp.concatenate([ck] * reps, axis=1)
                if masked:
                    sc = jnp.where(qidx >= kidx, sc, NEG)
                m_new = jnp.maximum(m, jnp.max(sc, axis=0, keepdims=True))
                pr = jnp.exp(sc - m_new)
                a = jnp.exp(m - m_new)
                l = a * l + jnp.sum(pr, axis=0, keepdims=True)
                p_hi = pr.astype(BF16)
                p_lo = (pr - p_hi.astype(F32)).astype(BF16)
                pv = (jnp.dot(vtp, p_hi, preferred_element_type=F32)
                      + jnp.dot(vtp, p_lo, preferred_element_type=F32))
                acc = a * acc + pv[hh * HEAD_DIM:(hh + 1) * HEAD_DIM]
                out.append((m_new, l, acc))
            return tuple(out)

        init = tuple((jnp.full((1, blk), NEG, F32), jnp.zeros((1, blk), F32), jnp.zeros((HEAD_DIM, blk), F32))
                     for _ in range(2))
        carry = _loop_by_twos(0, i, lambda j, cr: block(j, cr, False), init)
        carry = block(i, carry, True)
        lse_ref[...] = jnp.zeros_like(lse_ref)
        for hh in range(2):
            m, l, acc = carry[hh]
            ot_ref[pl.ds(hh * HEAD_DIM, HEAD_DIM), :] = acc / l
            lse_ref[0, 0, pl.ds(hh, 1), :] = m + jnp.log(l)

    return _call(body, name=name, grid=(bsz, pairs, nq),
                 in_specs=[pl.BlockSpec((s, LANES), lambda b, p, i: (b, k_col + p)),
                           pl.BlockSpec((LANES, blk), lambda b, p, i: (p, b * nq + i)),
                           pl.BlockSpec((LANES, s), lambda b, p, i: (v_row + p, b)),
                           pl.BlockSpec((1, heads, blk), lambda b, p, i: (b, 0, i)),
                           pl.BlockSpec((1, 2, s, LANES), lambda b, p, i: (b, p, 0, 0))],
                 out_specs=[pl.BlockSpec((LANES, blk), lambda b, p, i: (p, b * nq + i)),
                            pl.BlockSpec((1, 1, SUBLANES, blk), lambda b, p, i: (b, p, 0, i))],
                 out_shape=[_sds((width, t), F32), _sds((bsz, pairs, SUBLANES, s), F32)],
                 sem=("parallel", "parallel", "parallel"))(z, qkvt, qkvt, cumt, cumb)


def _attn_bwd(z, qkvt, cumt, cumb, ot, do, dot, lse, bsz, heads, q_off, *, name):
    t = z.shape[0]
    width = heads * HEAD_DIM
    pairs = heads // 2
    s = t // bsz
    blk = ATTN_BLOCK
    nkv = s // blk
    q_col = q_off // LANES
    k_col = (q_off + width) // LANES
    v_col = (q_off + 2 * width) // LANES
    k_row = width // LANES
    reps = blk // LANES

    def body(k_ref, v_ref, kt_ref, q_ref, qt_ref, do_ref, dot_ref, ot_ref, lse_ref, ckb_ref, cqt_ref,
             dk_ref, dv_ref, dqt_ref, dcum_ref, dqt_acc, ds_acc):
        p_id = pl.program_id(1)
        j = pl.program_id(2)

        @pl.when(j == 0)
        def _():
            dqt_acc[...] = jnp.zeros_like(dqt_acc)

        kp = k_ref[...].astype(BF16)
        vp = v_ref[...].astype(BF16)
        kt = kt_ref[...]
        feat_masks = _head_masks((LANES, blk), 0)
        lane_masks = _head_masks((blk, LANES), 1)
        ktm = [jnp.where(mk, kt, jnp.zeros_like(kt)) for mk in feat_masks]
        ck = [jnp.concatenate([ckb_ref[0, hh]] * reps, axis=1) for hh in range(2)]
        kidx = lax.broadcasted_iota(jnp.int32, (blk, blk), 0)
        qidx = lax.broadcasted_iota(jnp.int32, (blk, blk), 1)
        ds_acc[...] = jnp.zeros_like(ds_acc)

        def block(i, carry, masked):
            dk, dv = carry
            off = pl.multiple_of(i * blk, blk)
            qt = qt_ref[:, pl.ds(off, blk)]
            dt = dot_ref[:, pl.ds(off, blk)]
            o_t = ot_ref[:, pl.ds(off, blk)]
            q_rows = q_ref[pl.ds(off, blk), :].astype(BF16)
            do_rows = do_ref[pl.ds(off, blk), :]
            for hh in range(2):
                qtm = jnp.where(feat_masks[hh], qt, jnp.zeros_like(qt))
                dtm = jnp.where(feat_masks[hh], dt, jnp.zeros_like(dt))
                sc = jnp.dot(kp, qtm, preferred_element_type=F32) * ATTN_SCALE
                sc = (sc + cqt_ref[0, pl.ds(2 * p_id + hh, 1), pl.ds(off, blk)]) - ck[hh]
                pr = jnp.exp(sc - lse_ref[0, 0, pl.ds(hh, 1), pl.ds(off, blk)])
                if masked:
                    pr = jnp.where(qidx >= kidx, pr, 0.0)
                dp = jnp.dot(vp, dtm, preferred_element_type=F32)
                delta = jnp.sum(dtm.astype(F32) * o_t, axis=0, keepdims=True)
                ds = pr * (dp - delta)
                ds_acc[hh] += ds
                dsb = ds.astype(BF16)
                qm = jnp.where(lane_masks[hh], q_rows, jnp.zeros_like(q_rows))
                dom = jnp.where(lane_masks[hh], do_rows, jnp.zeros_like(do_rows))
                dv = dv + jnp.dot(pr.astype(BF16), dom, preferred_element_type=F32)
                dk = dk + jnp.dot(dsb, qm, preferred_element_type=F32) * ATTN_SCALE
                dqt_acc[:, pl.ds(off, blk)] += jnp.dot(ktm[hh], dsb, preferred_element_type=F32) * ATTN_SCALE
            return dk, dv

        zero = jnp.zeros((blk, LANES), F32)
        carry = block(j, (zero, zero), True)
        dk, dv = _loop_by_twos(j + 1, nkv, lambda i, cr: block(i, cr, False), carry)
        dk_ref[...] = dk.astype(dk_ref.dtype)
        dv_ref[...] = dv.astype(dv_ref.dtype)
        lane = lax.broadcasted_iota(jnp.int32, (blk, LANES), 1)
        dcum = jnp.zeros((blk, LANES), F32)
        for hh in range(2):
            col = jnp.sum(ds_acc[hh], axis=1, keepdims=True)
            dcum = jnp.where(lane == 2 * p_id + hh, -col, dcum)
        dcum_ref[0, 0] = dcum

        @pl.when(j == nkv - 1)
        def _():
            dqt_ref[...] = dqt_acc[...]

    key_rows = lambda col: pl.BlockSpec((blk, LANES), lambda b, p, j: (b * nkv + j, col + p))
    seq_t = lambda row: pl.BlockSpec((LANES, s), lambda b, p, j: (row + p, b))
    return _call(body, name=name, grid=(bsz, pairs, nkv),
                 in_specs=[key_rows(k_col), key_rows(v_col),
                           pl.BlockSpec((LANES, blk), lambda b, p, j: (k_row + p, b * nkv + j)),
                           pl.BlockSpec((s, LANES), lambda b, p, j: (b, q_col + p)), seq_t(0),
                           pl.BlockSpec((s, LANES), lambda b, p, j: (b, p)), seq_t(0), seq_t(0),
                           pl.BlockSpec((1, 1, SUBLANES, s), lambda b, p, j: (b, p, 0, 0)),
                           pl.BlockSpec((1, 2, blk, LANES), lambda b, p, j: (b, p, j, 0)),
                           pl.BlockSpec((1, heads, s), lambda b, p, j: (b, 0, 0))],
                 out_specs=[key_rows(0), key_rows(0), seq_t(0),
                            pl.BlockSpec((1, 1, blk, LANES), lambda b, p, j: (b, p, j, 0))],
                 out_shape=[_sds((t, width), BF16), _sds((t, width), BF16), _sds((width, t), F32),
                            _sds((bsz, pairs, s, LANES), F32)],
                 scratch=[pltpu.VMEM((LANES, s), F32), pltpu.VMEM((2, blk, blk), F32)],
                 sem=("parallel", "parallel", "arbitrary"))(z, z, qkvt, z, qkvt, do, dot, ot, lse, cumb, cumt)


def _adamw(w, g, m, v, *, name):
    bc1 = 1.0 - ADAM_B1 ** ADAM_STEP
    bc2 = 1.0 - ADAM_B2 ** ADAM_STEP

    def body(w_ref, g_ref, m_ref, v_ref, d_ref, nm_ref, nv_ref):
        g_v = g_ref[...]
        nm = ADAM_B1 * m_ref[...] + (1.0 - ADAM_B1) * g_v
        nv = ADAM_B2 * v_ref[...] + (1.0 - ADAM_B2) * (g_v * g_v)
        nm_ref[...] = nm
        nv_ref[...] = nv
        d_ref[...] = -ADAM_LR * ((nm / bc1) / (jnp.sqrt(nv / bc2) + ADAM_EPS) + ADAM_WD * w_ref[...])

    if w.ndim == 2:
        grid = (1,)
        blk = pl.BlockSpec(w.shape, lambda i: (0, 0))
    else:
        layers, rows, cols = w.shape
        tr = rows if rows <= 256 else _pick(rows, (352, 256, 128, 64, 32, 16, 8))
        grid = (layers, rows // tr)
        blk = pl.BlockSpec((1, tr, cols), lambda layer, i: (layer, i, 0))
    return tuple(_call(body, name=name, grid=grid, in_specs=[blk] * 4, out_specs=[blk] * 3,
                       out_shape=[_sds(w.shape, F32)] * 3, sem=("parallel",) * len(grid))(w, g, m, v))


_ANY = pl.BlockSpec(memory_space=pl.ANY)


def _comm_call(body, *, name, n_in, out_shape, n_sems):
    scratch = [pltpu.SemaphoreType.DMA((n_sems,)), pltpu.SemaphoreType.DMA((n_sems,)),
               pltpu.SemaphoreType.DMA((len(out_shape),))]
    return pl.pallas_call(body, name=name, in_specs=[_ANY] * n_in, out_specs=[_ANY] * len(out_shape),
                          out_shape=out_shape, scratch_shapes=scratch)


def _place():
    x, y, c = lax.axis_index("x"), lax.axis_index("y"), lax.axis_index("c")
    return x, y, c, [(1 - x, y), (x, 1 - y), (1 - x, 1 - y)]


def _remote(src, dst, send_sems, recv_sems, sem, to):
    return pltpu.make_async_remote_copy(src_ref=src, dst_ref=dst, send_sem=send_sems.at[sem],
                                        recv_sem=recv_sems.at[sem], device_id=to, device_id_type=MESH)


def _all_gather8(v, *, name):
    def body(v_ref, out_ref, send_sems, recv_sems, local_sems):
        x, y, c, _ = _place()
        me = 4 * x + 2 * y + c
        mine = pltpu.make_async_copy(v_ref, out_ref.at[me], local_sems.at[0])
        mine.start()
        peers = []
        for k in range(1, N_DEVICES):
            px = 1 - x if k & 4 else x
            py = 1 - y if k & 2 else y
            pc = 1 - c if k & 1 else c
            peers.append((px, py, pc))
        sends = [_remote(v_ref, out_ref.at[me], send_sems, recv_sems, k, peer) for k, peer in enumerate(peers)]
        for cp in sends:
            cp.start()
        for k, (px, py, pc) in enumerate(peers):
            _remote(v_ref, out_ref.at[4 * px + 2 * py + pc], send_sems, recv_sems, k, (px, py, pc)).wait_recv()
        for cp in sends:
            cp.wait_send()
        mine.wait()

    out = _comm_call(body, name=name, n_in=1, out_shape=[_sds((N_DEVICES,) + v.shape, v.dtype)],
                     n_sems=N_DEVICES - 1)(v)
    return out[0]


def _window(ref, mode, layer, chip, rows, cols, half=None):
    first, count = (0, rows) if half is None else (half * (rows // 2), rows // 2)
    if mode == "slab":
        return ref.at[layer, chip] if half is None else ref.at[layer, chip, pl.ds(first, count), :]
    if mode == "cols":
        col_window = pl.ds(pl.multiple_of(chip * cols, LANES), cols)
        return ref.at[layer, :, col_window] if half is None else ref.at[layer, pl.ds(first, count), col_window]
    return ref.at[layer, pl.ds(pl.multiple_of(chip * rows + first, SUBLANES), count), :]


def _whole_shape(mode, shard_shape):
    layers, rows, cols = shard_shape
    if mode == "slab":
        return (layers, N_CHIPS, rows, cols)
    if mode == "cols":
        assert cols % LANES == 0
        return (layers, rows, N_CHIPS * cols)
    assert rows % 16 == 0
    return (layers, N_CHIPS * rows, cols)


def _gather_weights(shards, modes, *, name):
    n = len(shards)
    meta = [(mode,) + tuple(a.shape[1:]) for a, mode in zip(shards, modes)]
    for a in shards:
        assert a.shape[0] == 2 and a.shape[1] % 2 == 0
    per = 8

    def body(*refs):
        ins, outs = refs[:n], refs[n:2 * n]
        send_sems, recv_sems, _ = refs[2 * n:]
        x, y, c, _ = _place()
        me, x_nbr, y_nbr, diagonal = 2 * x + y, 2 * (1 - x) + y, 2 * x + 1 - y, 2 * (1 - x) + 1 - y
        to_x, to_y, sibling = (1 - x, y, c), (x, 1 - y, c), (x, y, 1 - c)
        sent = []

        def copy(src, dst, sem, to):
            cp = _remote(src, dst, send_sems, recv_sems, sem, to)
            cp.start()
            sent.append(cp)

        def arrived(win, sem):
            _remote(win, win, send_sems, recv_sems, sem, sibling).wait_recv()

        for i, (mode, rows, cols) in enumerate(meta):
            mine = _window(outs[i], mode, c, me, rows, cols)
            copy(ins[i].at[c], mine, per * i, to_x)
            copy(ins[i].at[c], mine, per * i + 1, to_y)
            copy(ins[i], _window(outs[i], mode, slice(None), me, rows, cols), per * i + 7, sibling)
        for i, (mode, rows, cols) in enumerate(meta):
            arrived(_window(outs[i], mode, c, x_nbr, rows, cols), per * i)
            half = _window(outs[i], mode, c, x_nbr, rows, cols, half=0)
            copy(half, half, per * i + 2, to_y)
            win = _window(outs[i], mode, c, x_nbr, rows, cols)
            copy(win, win, per * i + 4, sibling)
            arrived(_window(outs[i], mode, c, y_nbr, rows, cols), per * i + 1)
            half = _window(outs[i], mode, c, y_nbr, rows, cols, half=1)
            copy(half, half, per * i + 3, to_x)
            win = _window(outs[i], mode, c, y_nbr, rows, cols)
            copy(win, win, per * i + 5, sibling)
        for i, (mode, rows, cols) in enumerate(meta):
            arrived(_window(outs[i], mode, c, diagonal, rows, cols, half=0), per * i + 2)
            arrived(_window(outs[i], mode, c, diagonal, rows, cols, half=1), per * i + 3)
            win = _window(outs[i], mode, c, diagonal, rows, cols)
            copy(win, win, per * i + 6, sibling)
        for i, (mode, rows, cols) in enumerate(meta):
            arrived(_window(outs[i], mode, slice(None), me, rows, cols), per * i + 7)
            for k, chip in enumerate((x_nbr, y_nbr, diagonal)):
                arrived(_window(outs[i], mode, 1 - c, chip, rows, cols), per * i + 4 + k)
        for cp in sent:
            cp.wait_send()

    out_shape = [_sds(_whole_shape(mode, a.shape), a.dtype) for a, mode in zip(shards, modes)]
    return _comm_call(body, name=name, n_in=n, out_shape=out_shape, n_sems=per * n)(*shards)


def _rs_swap(grads, *, name):
    n = len(grads)

    def body(*refs):
        ins, outs = refs[:n], refs[n:2 * n]
        send_sems, recv_sems, _ = refs[2 * n:]
        x, y, c, _ = _place()
        copies = [_remote(ins[i].at[1 - c], outs[i], send_sems, recv_sems, i, (x, y, 1 - c)) for i in range(n)]
        for cp in copies:
            cp.start()
        for cp in copies:
            cp.wait()

    return _comm_call(body, name=name, n_in=n, out_shape=[_sds(g.shape[1:], g.dtype) for g in grads], n_sems=n)(*grads)


def _part(ref, mode, chip, rows, cols):
    if mode == "slab":
        return ref.at[chip]
    if mode == "cols":
        return ref.at[:, pl.ds(pl.multiple_of(chip * cols, LANES), cols)]
    return ref.at[pl.ds(pl.multiple_of(chip * rows, SUBLANES), rows), :]


def _rs_scatter(parts, modes, shard_shapes, *, name):
    n = len(parts)
    meta = [(mode,) + tuple(shp[1:]) for mode, shp in zip(modes, shard_shapes)]

    def body(*refs):
        ins, outs = refs[:n], refs[n:2 * n]
        send_sems, recv_sems, local_sems = refs[2 * n:]
        x, y, c, chips = _place()
        me = 2 * x + y
        local, sends = [], []
        for i, (mode, rows, cols) in enumerate(meta):
            cp = pltpu.make_async_copy(_part(ins[i], mode, me, rows, cols), outs[i].at[me], local_sems.at[i])
            cp.start()
            local.append(cp)
            for r, (cx, cy) in enumerate(chips):
                cp = _remote(_part(ins[i], mode, 2 * cx + cy, rows, cols), outs[i].at[me], send_sems, recv_sems,
                             3 * i + r, (cx, cy, c))
                cp.start()
                sends.append(cp)
        for i, (mode, rows, cols) in enumerate(meta):
            for r, (cx, cy) in enumerate(chips):
                k = 2 * cx + cy
                _remote(_part(ins[i], mode, k, rows, cols), outs[i].at[k], send_sems, recv_sems, 3 * i + r,
                        (cx, cy, c)).wait_recv()
        for cp in sends:
            cp.wait_send()
        for cp in local:
            cp.wait()

    out_shape = [_sds((N_CHIPS,) + tuple(shp[1:]), p.dtype) for p, shp in zip(parts, shard_shapes)]
    return _comm_call(body, name=name, n_in=n, out_shape=out_shape, n_sems=3 * n)(*parts)


def _rs_exchange(sums, *, name):
    n = len(sums)

    def body(*refs):
        ins, outs = refs[:n], refs[n:2 * n]
        send_sems, recv_sems, _ = refs[2 * n:]
        x, y, c, _ = _place()
        copies = [_remote(ins[i], outs[i], send_sems, recv_sems, i, (x, y, 1 - c)) for i in range(n)]
        for cp in copies:
            cp.start()
        for cp in copies:
            cp.wait()

    return _comm_call(body, name=name, n_in=n, out_shape=[_sds(s.shape, s.dtype) for s in sums], n_sems=n)(*sums)


def _row_tile(rows, cols, itemsize):
    target = max(SUBLANES, (2 << 20) // (cols * itemsize))
    cands = [c for c in (2048, 1024, 512, 256, 128, 64, 32, 16) if c <= target]
    tr = _pick(rows, cands)
    return tr


def _add_layer(g, other, core, *, name):
    _, rows, cols = g.shape
    tr = _row_tile(rows, cols, 4)

    def body(core_ref, g_ref, o_ref, out_ref):
        out_ref[...] = (g_ref[0] + o_ref[...]).astype(out_ref.dtype)

    grid_spec = pltpu.PrefetchScalarGridSpec(
        num_scalar_prefetch=1, grid=(rows // tr,),
        in_specs=[pl.BlockSpec((1, tr, cols), lambda i, core_ref: (core_ref[0], i, 0)),
                  pl.BlockSpec((tr, cols), lambda i, core_ref: (i, 0))],
        out_specs=pl.BlockSpec((tr, cols), lambda i, core_ref: (i, 0)))
    return pl.pallas_call(body, name=name, grid_spec=grid_spec, out_shape=_sds((rows, cols), BF16),
                          compiler_params=pltpu.CompilerParams(dimension_semantics=("parallel",),
                                                               vmem_limit_bytes=VMEM_LIMIT))(core, g, other)


def _sum_slots(parts, *, name):
    n, rows, cols = parts.shape
    tr = _row_tile(rows, cols, 4)

    def body(p_ref, o_ref):
        acc = p_ref[0].astype(F32) + p_ref[1].astype(F32)
        for k in range(2, n):
            acc = acc + p_ref[k].astype(F32)
        o_ref[...] = acc

    return _call(body, name=name, grid=(rows // tr,),
                 in_specs=[pl.BlockSpec((n, tr, cols), lambda i: (0, i, 0))],
                 out_specs=pl.BlockSpec((tr, cols), lambda i: (i, 0)),
                 out_shape=_sds((rows, cols), F32), sem=("parallel",))(parts)


def _reduce_scatter(grads, modes, shard_shapes):
    core = lax.axis_index("c").astype(jnp.int32).reshape(1)
    flat = [g.reshape(g.shape[0], -1, g.shape[-1]) for g in grads]
    from_sibling = _rs_swap(flat, name="rs_swap")
    parts = []
    for i, (g, o) in enumerate(zip(flat, from_sibling)):
        p = _add_layer(g, o, core, name=f"rs_add_{i}")
        parts.append(p.reshape(grads[i].shape[1:]))
    from_chips = _rs_scatter(parts, modes, shard_shapes, name="rs_scatter")
    sums = [_sum_slots(r, name=f"rs_sum_{i}") for i, r in enumerate(from_chips)]
    others = _rs_exchange(sums, name="rs_exchange")
    mine_first = lax.axis_index("c") == 0
    return [jnp.where(mine_first, jnp.stack([mine, other]), jnp.stack([other, mine]))
            for mine, other in zip(sums, others)]


def _layer_weights(full, rep, layer, dims):
    f_off, n_heads = dims["f_off"], dims["heads"]
    b_in = rep["b_in"][layer]
    pad = LANES - n_heads
    return {
        "w_main": (full["w_main"], layer),
        "b_main": jnp.concatenate([b_in[:f_off], b_in[f_off + n_heads:]])[None],
        "w_f": (full["w_f"], layer),
        "w_f_matrix": full["w_f"][layer],
        "b_f": jnp.pad(b_in[f_off:f_off + n_heads], (0, pad))[None],
        "conv_a_w": full["conv_a_w"][layer],
        "conv_a_b": rep["conv_a_b"][layer][None],
        "ln_conv_g": rep["ln_conv_g"][layer][None],
        "ln_conv_b": rep["ln_conv_b"][layer][None],
        "w_conv_proj": (full["w_conv_proj"], layer),
        "w_attn_proj": (full["w_attn_proj"], layer),
        "w_mix_out": (full["w_mix_out"], layer),
        "b_mix_out": rep["b_mix_out"][layer][None],
        "ln1_g": rep["ln1_g"][layer][None],
        "ln1_b": rep["ln1_b"][layer][None],
        "w_ffn_up": (full["w_ffn_up"], layer),
        "ffn_conv_w": full["ffn_conv_w"][layer],
        "ffn_conv_b": rep["ffn_conv_b"][layer][None],
        "w_ffn_down": (full["w_ffn_down"], layer),
        "ln2_g": rep["ln2_g"][layer][None],
        "ln2_b": rep["ln2_b"][layer][None],
    }


def _split_mod(mod, d):
    return [mod[:, k * d:(k + 1) * d][:, None, :] for k in range(6)]


def _layer_fwd(x, u, mods, p, dims, tag, after):
    bsz, ch, heads, alpha = dims["bsz"], dims["ch"], dims["heads"], dims["alpha"]
    _, _, gate1, shift2, scale2, gate2 = mods
    zm = _matmul(u, p["w_main"], "nn", BF16, bias=p["b_main"], name=f"in_main_{tag}")
    zf = _matmul(u, p["w_f"], "nn", F32, bias=p["b_f"], name=f"in_forget_{tag}")
    a3 = _conv_branch_fwd(zm, p["conv_a_w"], p["conv_a_b"], p["ln_conv_g"], p["ln_conv_b"], bsz, ch,
                          name=f"conv_branch_{tag}")
    ya = _matmul(a3, p["w_conv_proj"], "nn", BF16, name=f"conv_proj_{tag}")
    cumt, cumb = _fgate_fwd(zf, bsz, heads, name=f"fgate_{tag}")
    qkvt = _to_features_major(zm, 2 * ch, heads * HEAD_DIM, 3, name=f"qkv_t_{tag}")
    ot, lse = _attn_fwd(zm, qkvt, cumt, cumb, bsz, heads, 2 * ch, name=f"attn_{tag}")
    yb = _matmul(ot, p["w_attn_proj"], "tn", BF16, name=f"attn_proj_{tag}")
    m = _gate_merge_fwd(zm, ya, yb, dims["ga_off"], name=f"merge_{tag}")
    mix = _matmul(m, p["w_mix_out"], "nn", F32, bias=p["b_mix_out"], name=f"mix_out_{tag}")
    x1, u2 = _ln_res_mod_fwd(x, mix, gate1, p["ln1_g"], p["ln1_b"], scale2, shift2, alpha, bsz,
                             name=f"ln_res1_mod2_{tag}")
    hp = _matmul(u2, p["w_ffn_up"], "nn", BF16, name=f"ffn_up_{tag}")
    f = _ffn_act_fwd(hp, p["ffn_conv_w"], p["ffn_conv_b"], bsz, dims["tcf"], name=f"ffn_act_{tag}")
    ffn = _matmul(f, p["w_ffn_down"], "nn", F32, name=f"ffn_down_{tag}")
    if isinstance(after, tuple):
        x2 = _ln_res_mod_fwd(x1, ffn, gate2, p["ln2_g"], p["ln2_b"], after[0], after[1], alpha, bsz,
                             name=f"ln_res2_mod1_{tag}")
    else:
        x2 = _ln_res_loss(x1, ffn, gate2, p["ln2_g"], p["ln2_b"], after, alpha, bsz, name=f"ln_res2_loss_{tag}")
    saved = dict(x=x, mods=mods, u=u, zm=zm, zf=zf, a3=a3, ya=ya, yb=yb, cumt=cumt, cumb=cumb,
                 qkvt=qkvt, ot=ot, lse=lse, m=m, mix=mix, x1=x1, u2=u2, hp=hp, f=f, ffn=ffn)
    return x2, saved


def _layer_bwd(top, p, sv, dims, tag, below):
    bsz, ch, heads, alpha = dims["bsz"], dims["ch"], dims["heads"], dims["alpha"]
    f_off, tcf = dims["f_off"], dims["tcf"]
    shift1, scale1, gate1, shift2, scale2, gate2 = sv["mods"]
    g = {}
    dr2, dffn, dgate2, g["ln2_g"], g["ln2_b"] = top
    df = _matmul(dffn, p["w_ffn_down"], "nt", BF16, name=f"ffn_down_dx_{tag}")
    g["w_ffn_down"] = _matmul(sv["f"], dffn, "tn", F32, name=f"ffn_down_dw_{tag}")
    dhg, dhl, dwg, dwl, dbg, dbl = _ffn_act_bwd(sv["hp"], df, p["ffn_conv_w"], p["ffn_conv_b"], bsz, tcf,
                                                name=f"ffn_act_bwd_{tag}")
    g["ffn_conv_w"] = jnp.concatenate([dwg, dwl], axis=1)
    g["ffn_conv_b"] = jnp.concatenate([dbg, dbl], axis=1)[0]
    du2 = _matmul(dhg, p["w_ffn_up"], "nt", F32, name=f"ffn_up_gate_dx_{tag}")
    du2 = _matmul(dhl, p["w_ffn_up"], "nt", F32, add=du2, b_k_first=dhg.shape[1], name=f"ffn_up_lin_dx_{tag}")
    d_ff = dhg.shape[1]
    dw_up = _matmul(sv["u2"], dhg, "tn", F32, out_cols=2 * d_ff, name=f"ffn_up_gate_dw_{tag}")
    g["w_ffn_up"] = _matmul(sv["u2"], dhl, "tn", F32, into=(dw_up, d_ff), name=f"ffn_up_lin_dw_{tag}")
    dr1, dmix, dgate1, g["ln1_g"], g["ln1_b"], g["b_mix_out"], dscale2, dshift2 = _ln_mod_res_bwd(
        du2, dr2, scale2, sv["x"], sv["mix"], gate1, p["ln1_g"], p["ln1_b"], alpha, bsz,
        name=f"ln_mod2_res1_bwd_{tag}")
    dm = _matmul(dmix, p["w_mix_out"], "nt", BF16, name=f"mix_out_dx_{tag}")
    g["w_mix_out"] = _matmul(sv["m"], dmix, "tn", F32, name=f"mix_out_dw_{tag}")
    dya, dyb, dzga, dzgb = _gate_merge_bwd(sv["zm"], sv["ya"], sv["yb"], dm, dims["ga_off"], name=f"merge_bwd_{tag}")
    da3 = _matmul(dya, p["w_conv_proj"], "nt", F32, name=f"conv_proj_dx_{tag}")
    g["w_conv_proj"] = _matmul(sv["a3"], dya, "tn", F32, name=f"conv_proj_dw_{tag}")
    do = _matmul(dyb, p["w_attn_proj"], "nt", BF16, name=f"attn_proj_dx_{tag}")
    dot = _matmul(p["w_attn_proj"], dyb, "nt", BF16, name=f"attn_proj_dxt_{tag}")
    g["w_attn_proj"] = _matmul(sv["ot"], dyb, "nn", F32, name=f"attn_proj_dw_{tag}")
    dzglu, g["conv_a_w"], dcb, g["ln_conv_g"], g["ln_conv_b"] = _conv_branch_bwd(
        sv["zm"], da3, p["conv_a_w"], p["conv_a_b"], p["ln_conv_g"], p["ln_conv_b"], bsz, ch,
        name=f"conv_branch_bwd_{tag}")
    g["conv_a_b"] = dcb[0]
    dk, dv, dqt, dcum = _attn_bwd(sv["zm"], sv["qkvt"], sv["cumt"], sv["cumb"], sv["ot"], do, dot, sv["lse"], bsz,
                                  heads, 2 * ch, name=f"attn_bwd_{tag}")
    dq = _to_rows_major(dqt, name=f"dq_rows_{tag}")
    dzf = _fgate_bwd(dcum, sv["zf"], bsz, name=f"fgate_bwd_{tag}")
    dzm = jnp.concatenate([dzglu, dq, dk, dv, dzga, dzgb], axis=1)
    du = _matmul(dzm, p["w_main"], "nt", F32, name=f"in_main_dx_{tag}")
    dwm, dbm = _matmul(sv["u"], dzm, "tn", F32, colsum=True, name=f"in_main_dw_{tag}")
    dwf, dbf = _matmul(sv["u"], dzf, "tn", F32, colsum=True, name=f"in_forget_dw_{tag}")
    dbm, dbf = dbm[0], dbf[0]
    g["w_main"], g["w_f"] = dwm, dwf
    g["b_in"] = jnp.concatenate([dbm[:f_off], dbf[:heads], dbm[f_off:]])
    if below is None:
        out, dscale1, dshift1 = _ln_mod_bwd(du, dzf, p["w_f_matrix"], sv["x"], scale1, dr1, alpha, bsz,
                                            name=f"ln_mod1_bwd_{tag}")
    else:
        sv_b, p_b = below
        *out, _, dscale1, dshift1 = _ln_mod_res_bwd(
            du, dr1, scale1, sv_b["x1"], sv_b["ffn"], sv_b["mods"][5], p_b["ln2_g"], p_b["ln2_b"], alpha, bsz,
            narrow=(dzf, p["w_f_matrix"]), name=f"ln_mod1_res2_bwd_{tag}")
    dmod = jnp.concatenate([dshift1, dscale1, dgate1, dshift2, dscale2, dgate2], axis=2)[:, 0, :]
    return out, g, dmod


def _local_step(x, mod, loss_target, full, rep, dims):
    bsz, seq, d = x.shape
    layers = mod.shape[0]
    params = [_layer_weights(full, rep, layer, dims) for layer in range(layers)]
    mods = [_split_mod(mod[layer], d) for layer in range(layers)]
    h = x.reshape(bsz * seq, d)
    u = _ln_mod_fwd(h, mods[0][1], mods[0][0], bsz, name="ln_mod1_l0")
    saved = []
    for layer in range(layers):
        last = layer == layers - 1
        after = loss_target.reshape(bsz * seq, d) if last else (mods[layer + 1][1], mods[layer + 1][0])
        (h, u), sv = _layer_fwd(h, u, mods[layer], params[layer], dims, f"l{layer}", after)
        saved.append(sv)
    dh, sq = h, u
    loss_local = 0.5 * jnp.sum(sq) / d
    top_sv, top_p = saved[-1], params[-1]
    dh = _ln_res_bwd(dh, top_sv["x1"], top_sv["ffn"], top_sv["mods"][5], top_p["ln2_g"], dims["alpha"], bsz,
                     name=f"ln_res2_bwd_l{layers - 1}")[:5]
    grads, dmods = [None] * layers, [None] * layers
    for layer in reversed(range(layers)):
        below = (saved[layer - 1], params[layer - 1]) if layer > 0 else None
        dh, grads[layer], dmods[layer] = _layer_bwd(dh, params[layer], saved[layer], dims, f"l{layer}", below)
    per_layer = ("w_main", "w_f")
    stacked = {wname: [grads[layer][wname] for layer in range(layers)] if wname in per_layer
               else jnp.stack([grads[layer][wname] for layer in range(layers)]) for wname in grads[0]}
    return loss_local, dh.reshape(bsz, seq, d), stacked, jnp.stack(dmods)


def _pad_rows(a):
    extra = -a.shape[-2] % (2 * SUBLANES)
    if extra == 0:
        return a
    return jnp.pad(a, [(0, 0)] * (a.ndim - 2) + [(0, extra), (0, 0)])


def _w_in_pieces(n, f_off, heads):
    n_in = N_CHIPS * n
    segments = [(0, f_off, "main", 0), (f_off, f_off + heads, "f", 0), (f_off + heads, n_in, "main", f_off)]
    pieces = []
    for chip in range(N_CHIPS):
        lo, hi = chip * n, (chip + 1) * n
        for a, b, target, t0 in segments:
            s, e = max(lo, a), min(hi, b)
            if s < e:
                pieces.append((chip, s - lo, e - lo, target, t0 + s - a))
    return pieces


def _w_in_from_slabs(slabs, f_off, heads, *, name):
    layers, _, k, n = slabs.shape
    tr = _pick(k, (256, 128, 64, 32, 16))
    n_main = N_CHIPS * n - heads
    pieces = _w_in_pieces(n, f_off, heads)

    def body(s_ref, m_ref, f_ref):
        f_ref[...] = jnp.zeros_like(f_ref)
        for chip in range(N_CHIPS):
            slab = s_ref[0, chip].astype(F32)
            for pc, s0, s1, target, t0 in pieces:
                if pc == chip:
                    out = m_ref if target == "main" else f_ref
                    out[0, :, t0:t0 + s1 - s0] = slab[:, s0:s1].astype(out.dtype)

    return _call(body, name=name, grid=(layers, k // tr),
                 in_specs=[pl.BlockSpec((1, N_CHIPS, tr, n), lambda layer, i: (layer, 0, i, 0))],
                 out_specs=[pl.BlockSpec((1, tr, n_main), lambda layer, i: (layer, i, 0)),
                            pl.BlockSpec((1, tr, LANES), lambda layer, i: (layer, i, 0))],
                 out_shape=[_sds((layers, k, n_main), slabs.dtype), _sds((layers, k, LANES), slabs.dtype)],
                 sem=("parallel", "parallel"))(slabs)


def _w_in_to_slabs(d_main, d_f, n, f_off, heads, *, name):
    layers = len(d_main)
    k = d_main[0].shape[0]
    tr = _pick(k, (128, 64, 32, 16, 8))
    pieces = _w_in_pieces(n, f_off, heads)

    def body(*refs):
        m_refs, f_refs, o_ref = refs[:layers], refs[layers:2 * layers], refs[2 * layers]
        for layer in range(layers):
            for chip, s0, s1, target, t0 in pieces:
                src = m_refs[layer] if target == "main" else f_refs[layer]
                o_ref[layer, chip, :, s0:s1] = src[:, t0:t0 + s1 - s0]

    return _call(body, name=name, grid=(k // tr,),
                 in_specs=[pl.BlockSpec((tr, d_main[0].shape[1]), lambda i: (i, 0))] * layers
                 + [pl.BlockSpec((tr, LANES), lambda i: (i, 0))] * layers,
                 out_specs=pl.BlockSpec((layers, N_CHIPS, tr, n), lambda i: (0, 0, i, 0)),
                 out_shape=_sds((layers, N_CHIPS, k, n), F32), sem=("parallel",))(*d_main, *d_f)


def kernel(x, c, w_ada, b_ada, w_in, b_in, conv_a_w, conv_a_b, ln_conv_g, ln_conv_b, w_conv_proj, w_attn_proj, w_mix_out, b_mix_out, ln1_g, ln1_b, w_ffn_up, ffn_conv_w, ffn_conv_b, w_ffn_down, ln2_g, ln2_b, loss_target, m_w_ada, m_b_ada, m_w_in, m_b_in, m_conv_a_w, m_conv_a_b, m_ln_conv_g, m_ln_conv_b, m_w_conv_proj, m_w_attn_proj, m_w_mix_out, m_b_mix_out, m_ln1_g, m_ln1_b, m_w_ffn_up, m_ffn_conv_w, m_ffn_conv_b, m_w_ffn_down, m_ln2_g, m_ln2_b, v_w_ada, v_b_ada, v_w_in, v_b_in, v_conv_a_w, v_conv_a_b, v_ln_conv_g, v_ln_conv_b, v_w_conv_proj, v_w_attn_proj, v_w_mix_out, v_b_mix_out, v_ln1_g, v_ln1_b, v_w_ffn_up, v_ffn_conv_w, v_ffn_conv_b, v_w_ffn_down, v_ln2_g, v_ln2_b):
    weights = dict(zip(WEIGHTS, (w_ada, b_ada, w_in, b_in, conv_a_w, conv_a_b, ln_conv_g, ln_conv_b, w_conv_proj,
                                 w_attn_proj, w_mix_out, b_mix_out, ln1_g, ln1_b, w_ffn_up, ffn_conv_w, ffn_conv_b,
                                 w_ffn_down, ln2_g, ln2_b)))
    mom1 = dict(zip(WEIGHTS, (m_w_ada, m_b_ada, m_w_in, m_b_in, m_conv_a_w, m_conv_a_b, m_ln_conv_g, m_ln_conv_b,
                              m_w_conv_proj, m_w_attn_proj, m_w_mix_out, m_b_mix_out, m_ln1_g, m_ln1_b, m_w_ffn_up,
                              m_ffn_conv_w, m_ffn_conv_b, m_w_ffn_down, m_ln2_g, m_ln2_b)))
    mom2 = dict(zip(WEIGHTS, (v_w_ada, v_b_ada, v_w_in, v_b_in, v_conv_a_w, v_conv_a_b, v_ln_conv_g, v_ln_conv_b,
                              v_w_conv_proj, v_w_attn_proj, v_w_mix_out, v_b_mix_out, v_ln1_g, v_ln1_b, v_w_ffn_up,
                              v_ffn_conv_w, v_ffn_conv_b, v_w_ffn_down, v_ln2_g, v_ln2_b)))
    bsz, seq, d = x.shape
    layers = w_ada.shape[0]
    ch = conv_a_w.shape[2] * N_CHIPS
    width = w_attn_proj.shape[1]
    heads = width // HEAD_DIM
    d_ff = w_ffn_down.shape[1] * N_CHIPS
    dims = dict(bsz=bsz, d=d, ch=ch, heads=heads, alpha=(2.0 * layers) ** 0.25, f_off=2 * ch + 3 * width,
                ga_off=2 * ch + 3 * width, tcf=_pick(d_ff, (256, 128)))
    chip = 2 * lax.axis_index("x") + lax.axis_index("y")
    device = 2 * chip + lax.axis_index("c")
    ada_cols = w_ada.shape[2]

    c_act = _silu_rows(_all_gather8(c, name="gather_c").reshape(N_DEVICES * bsz, d), name="silu_c")
    b_ada_mine = lax.dynamic_slice_in_dim(b_ada, chip * ada_cols, ada_cols, axis=1)
    mod_cols = jnp.stack([_matmul(c_act, (w_ada, layer), "nn", F32, bias=b_ada_mine[layer][None], name=f"ada_l{layer}")
                          for layer in range(layers)])
    mod_all = _all_gather8(mod_cols, name="gather_mod")
    mod_all = jnp.concatenate([mod_all[2 * k] for k in range(N_CHIPS)], axis=-1)
    mod = lax.dynamic_slice_in_dim(mod_all, device * bsz, bsz, axis=1)

    shards = [_pad_rows(weights[wname].astype(BF16) if as_bf16 else weights[wname]) for wname, _, as_bf16 in GATHERED]
    modes = [mode for _, mode, _ in GATHERED]
    whole = _gather_weights(shards, modes, name="gather_weights")
    full = {wname: w[:, :weights[wname].shape[1]] if mode == "cols" else w
            for (wname, mode, _), w in zip(GATHERED, whole)}
    full["w_main"], full["w_f"] = _w_in_from_slabs(full.pop("w_in"), dims["f_off"], heads, name="w_in_from_slabs")
    rep = {wname: weights[wname] for wname in REPLICATED}

    loss_local, grad_x, grads, dmod = _local_step(x, mod, loss_target, full, rep, dims)
    loss = lax.psum(loss_local, ("x", "y", "c"))

    grads["w_in"] = _w_in_to_slabs(grads.pop("w_main"), grads.pop("w_f"), w_in.shape[2], dims["f_off"], heads,
                                   name="w_in_to_slabs")
    shard_shapes = [s.shape for s in shards]
    reduced = _reduce_scatter([_pad_rows(grads[wname]) for wname, _, _ in GATHERED], modes, shard_shapes)
    grad = {wname: r[:, :weights[wname].shape[1]] for (wname, _, _), r in zip(GATHERED, reduced)}

    small = jnp.concatenate([dmod.reshape(-1)] + [grads[wname].reshape(-1) for wname in REPLICATED])
    n_small = small.shape[0]
    rows = -(-n_small // (SUBLANES * LANES)) * SUBLANES
    small = jnp.pad(small, (0, rows * LANES - n_small)).reshape(rows, LANES)
    gathered = _all_gather8(small, name="gather_small")
    n_dmod = dmod.size
    dmod_all = gathered.reshape(N_DEVICES, -1)[:, :n_dmod].reshape(N_DEVICES, layers, bsz, 6 * d)
    dmod_all = jnp.transpose(dmod_all, (1, 0, 2, 3)).reshape(layers, N_DEVICES * bsz, 6 * d)
    summed = _sum_slots(gathered, name="sum_small").reshape(-1)
    off = n_dmod
    for wname in REPLICATED:
        n = weights[wname].size
        grad[wname] = summed[off:off + n].reshape(weights[wname].shape)
        off += n
    dmod_mine = lax.dynamic_slice_in_dim(dmod_all, chip * ada_cols, ada_cols, axis=2)
    grad["w_ada"] = jnp.stack([_matmul(c_act, dmod_mine[layer], "tn", F32, name=f"ada_dw_l{layer}")
                               for layer in range(layers)])
    grad["b_ada"] = jnp.stack([_colsum(dmod_all[layer], name=f"ada_db_l{layer}")[0] for layer in range(layers)])

    delta, new_m, new_v = {}, {}, {}
    for wname in WEIGHTS:
        delta[wname], new_m[wname], new_v[wname] = _adamw(weights[wname], grad[wname], mom1[wname], mom2[wname],
                                                          name=f"adamw_{wname}")
    return (loss, grad_x, *[grad[wname] for wname in WEIGHTS], *[delta[wname] for wname in WEIGHTS],
            *[new_m[wname] for wname in WEIGHTS], *[new_v[wname] for wname in WEIGHTS])
```
